```python
import jax, jax.numpy as jnp
from jax import lax
import numpy as np

D_MODEL = 1024
BATCH = 32
SEQ = 2048
DEPTH = 1

MLA_HEADS = 4
QK_NOPE_DIM = 128
QK_ROPE_DIM = 64
QK_DIM = QK_NOPE_DIM + QK_ROPE_DIM
V_HEAD_DIM = 128
Q_LORA_RANK = 512
KV_LORA_RANK = 256
MLA_WIDTH = MLA_HEADS * V_HEAD_DIM
ROPE_THETA = 10000.0
Q_BLOCK = 128
POOL_WINDOWS = (2, 4, 8, 16)
POOL_GROUPS = len(POOL_WINDOWS)
POOL_WIDTH = D_MODEL - MLA_WIDTH
POOL_GROUP_DIM = POOL_WIDTH // POOL_GROUPS
MIX_WIDTH = MLA_WIDTH + POOL_WIDTH
IN_SPLITS = (Q_LORA_RANK, KV_LORA_RANK, QK_ROPE_DIM, MLA_WIDTH, POOL_WIDTH, POOL_WIDTH)
IN_WIDTH = sum(IN_SPLITS)
RMS_EPS = 1e-6
LN_EPS = 1e-5

kernel_name = "hybrid_mla_multiscale_pool_deepnorm"


def rms_norm(x, g):
    xf = x.astype(jnp.float32)
    inv = lax.rsqrt(jnp.mean(xf * xf, axis=-1, keepdims=True) + RMS_EPS)
    return (xf * inv).astype(x.dtype) * g


def layer_norm(x, g, b):
    xf = x.astype(jnp.float32)
    mu = jnp.mean(xf, axis=-1, keepdims=True)
    var = jnp.mean(jnp.square(xf - mu), axis=-1, keepdims=True)
    return ((xf - mu) * lax.rsqrt(var + LN_EPS)).astype(x.dtype) * g + b


def rope_cos_sin(positions, dtype):
    half = QK_ROPE_DIM // 2
    inv_freq = ROPE_THETA ** (-jnp.arange(half, dtype=jnp.float32) / half)
    ang = positions.astype(jnp.float32)[..., None] * inv_freq
    return jnp.cos(ang).astype(dtype), jnp.sin(ang).astype(dtype)


def apply_rope(t, cos, sin):
    t1, t2 = jnp.split(t, 2, axis=-1)
    return jnp.concatenate([t1 * cos - t2 * sin, t1 * sin + t2 * cos], axis=-1)


def mla_branch(x_q, x_kv, k_rope_raw, positions, q_norm_g, w_uq, kv_norm_g, w_ukv):
    B, S, _ = x_q.shape
    cos, sin = rope_cos_sin(positions, x_q.dtype)
    q = (rms_norm(x_q, q_norm_g) @ w_uq).reshape(B, S, MLA_HEADS, QK_DIM)
    q_nope, q_rope = q[..., :QK_NOPE_DIM], q[..., QK_NOPE_DIM:]
    q_rope = apply_rope(q_rope, cos[:, :, None, :], sin[:, :, None, :])
    kv = (rms_norm(x_kv, kv_norm_g) @ w_ukv).reshape(B, S, MLA_HEADS, QK_NOPE_DIM + V_HEAD_DIM)
    k_nope, v = kv[..., :QK_NOPE_DIM], kv[..., QK_NOPE_DIM:]
    k_rope = apply_rope(k_rope_raw, cos, sin)
    scale = QK_DIM ** -0.5
    nb = S // Q_BLOCK
    qn_b = q_nope.reshape(B, nb, Q_BLOCK, MLA_HEADS, QK_NOPE_DIM).transpose(1, 0, 2, 3, 4)
    qr_b = q_rope.reshape(B, nb, Q_BLOCK, MLA_HEADS, QK_ROPE_DIM).transpose(1, 0, 2, 3, 4)
    pos_b = positions.reshape(B, nb, Q_BLOCK).transpose(1, 0, 2)
    neg = jnp.finfo(jnp.float32).min

    def attend(args):
        qn, qr, pq = args
        s = (jnp.einsum('bqhd,bkhd->bhqk', qn, k_nope)
             + jnp.einsum('bqhr,bkr->bhqk', qr, k_rope)).astype(jnp.float32) * scale
        mask = pq[:, None, :, None] >= positions[:, None, None, :]
        p = jax.nn.softmax(jnp.where(mask, s, neg), axis=-1)
        return jnp.einsum('bhqk,bkhd->bqhd', p.astype(v.dtype), v)

    o = lax.map(attend, (qn_b, qr_b, pos_b))
    return o.transpose(1, 0, 2, 3, 4).reshape(B, S, MLA_WIDTH)


def pool_branch(u, pool_w, pool_scale):
    B, S, _ = u.shape
    uf = u.astype(jnp.float32).reshape(B, S, POOL_GROUPS, POOL_GROUP_DIM)
    cs = jnp.concatenate([jnp.zeros((B, 1, POOL_GROUPS, POOL_GROUP_DIM), jnp.float32),
                          jnp.cumsum(uf, axis=1)], axis=1)
    hi = jnp.arange(S) + 1
    means = []
    for g, w in enumerate(POOL_WINDOWS):
        lo = jnp.maximum(hi - w, 0)
        cnt = (hi - lo).astype(jnp.float32)[None, :, None]
        means.append((cs[:, hi, g] - cs[:, lo, g]) / cnt)
    pooled = jnp.stack(means, axis=2) - uf
    mixed = jnp.einsum('bsgc,gcd->bsgd', pooled.astype(u.dtype), pool_w)
    return mixed.reshape(B, S, POOL_WIDTH) * pool_scale


def _fwd_setup_inputs(seed: int = 0) -> dict:
    key = jax.random.key(seed)
    ks = jax.random.split(key, 12)
    beta = (8.0 * DEPTH) ** -0.25
    nrm = jax.random.normal
    return {
        "x": nrm(ks[0], (BATCH, SEQ, D_MODEL), jnp.float32),
        "positions": jnp.broadcast_to(jnp.arange(SEQ, dtype=jnp.int32), (BATCH, SEQ)),
        "w_in": nrm(ks[1], (D_MODEL, IN_WIDTH), jnp.float32) * D_MODEL ** -0.5,
        "q_norm_g": 1.0 + 0.05 * nrm(ks[2], (Q_LORA_RANK,), jnp.float32),
        "w_uq": nrm(ks[3], (Q_LORA_RANK, MLA_HEADS * QK_DIM), jnp.float32) * Q_LORA_RANK ** -0.5,
        "kv_norm_g": 1.0 + 0.05 * nrm(ks[4], (KV_LORA_RANK,), jnp.float32),
        "w_ukv": nrm(ks[5], (KV_LORA_RANK, MLA_HEADS * (QK_NOPE_DIM + V_HEAD_DIM)), jnp.float32) * KV_LORA_RANK ** -0.5,
        "pool_w": nrm(ks[6], (POOL_GROUPS, POOL_GROUP_DIM, POOL_GROUP_DIM), jnp.float32) * POOL_GROUP_DIM ** -0.5,
        "pool_scale": 1.0 + 0.1 * nrm(ks[7], (POOL_WIDTH,), jnp.float32),
        "w_out": nrm(ks[8], (MIX_WIDTH, D_MODEL), jnp.float32) * (MIX_WIDTH ** -0.5) * beta,
        "ln_g": 1.0 + 0.05 * nrm(ks[9], (DEPTH, D_MODEL), jnp.float32),
        "ln_b": 0.02 * nrm(ks[10], (DEPTH, D_MODEL), jnp.float32),
    }


def _fwd_reference(x, positions, w_in, q_norm_g, w_uq, kv_norm_g, w_ukv, pool_w, pool_scale, w_out, ln_g, ln_b):
    alpha = (2.0 * DEPTH) ** 0.25
    splits = [int(c) for c in np.cumsum(IN_SPLITS)[:-1]]
    for layer in range(DEPTH):
        h = x @ w_in
        x_q, x_kv, k_rope_raw, gate_a, u_pool, gate_b = jnp.split(h, splits, axis=-1)
        y_a = mla_branch(x_q, x_kv, k_rope_raw, positions, q_norm_g, w_uq, kv_norm_g, w_ukv) * jax.nn.silu(gate_a)
        y_b = pool_branch(u_pool, pool_w, pool_scale) * jax.nn.silu(gate_b)
        mix = jnp.concatenate([y_a, y_b], axis=-1) @ w_out
        x = layer_norm(alpha * x + mix, ln_g[layer], ln_b[layer])
    return x


import jax as _jax
import jax.numpy as _jnp

TWIN_FORMAT = 'train_step'
FWD_PARAMS = ['x', 'positions', 'w_in', 'q_norm_g', 'w_uq', 'kv_norm_g', 'w_ukv', 'pool_w', 'pool_scale', 'w_out', 'ln_g', 'ln_b']
TWIN_WEIGHTS = ['w_in', 'q_norm_g', 'w_uq', 'kv_norm_g', 'w_ukv', 'pool_w', 'pool_scale', 'w_out', 'ln_g', 'ln_b']
TWIN_DIFF_INPUT = 'x'
TWIN_INPUTS = ['x', 'positions', 'w_in', 'q_norm_g', 'w_uq', 'kv_norm_g', 'w_ukv', 'pool_w', 'pool_scale', 'w_out', 'ln_g', 'ln_b', 'loss_target', 'm_w_in', 'm_q_norm_g', 'm_w_uq', 'm_kv_norm_g', 'm_w_ukv', 'm_pool_w', 'm_pool_scale', 'm_w_out', 'm_ln_g', 'm_ln_b', 'v_w_in', 'v_q_norm_g', 'v_w_uq', 'v_kv_norm_g', 'v_w_ukv', 'v_pool_w', 'v_pool_scale', 'v_w_out', 'v_ln_g', 'v_ln_b']
TWIN_OUTPUTS = ['loss', 'grad_x', 'grad_w_in', 'grad_q_norm_g', 'grad_w_uq', 'grad_kv_norm_g', 'grad_w_ukv', 'grad_pool_w', 'grad_pool_scale', 'grad_w_out', 'grad_ln_g', 'grad_ln_b', 'delta_w_in', 'delta_q_norm_g', 'delta_w_uq', 'delta_kv_norm_g', 'delta_w_ukv', 'delta_pool_w', 'delta_pool_scale', 'delta_w_out', 'delta_ln_g', 'delta_ln_b', 'new_m_w_in', 'new_m_q_norm_g', 'new_m_w_uq', 'new_m_kv_norm_g', 'new_m_w_ukv', 'new_m_pool_w', 'new_m_pool_scale', 'new_m_w_out', 'new_m_ln_g', 'new_m_ln_b', 'new_v_w_in', 'new_v_q_norm_g', 'new_v_w_uq', 'new_v_kv_norm_g', 'new_v_w_ukv', 'new_v_pool_w', 'new_v_pool_scale', 'new_v_w_out', 'new_v_ln_g', 'new_v_ln_b']
TWIN_LEAF_KINDS = {'loss': 'loss', 'grad_x': 'grad_x', 'grad_w_in': 'grad_w', 'grad_q_norm_g': 'grad_w', 'grad_w_uq': 'grad_w', 'grad_kv_norm_g': 'grad_w', 'grad_w_ukv': 'grad_w', 'grad_pool_w': 'grad_w', 'grad_pool_scale': 'grad_w', 'grad_w_out': 'grad_w', 'grad_ln_g': 'grad_w', 'grad_ln_b': 'grad_w', 'delta_w_in': 'delta_w', 'delta_q_norm_g': 'delta_w', 'delta_w_uq': 'delta_w', 'delta_kv_norm_g': 'delta_w', 'delta_w_ukv': 'delta_w', 'delta_pool_w': 'delta_w', 'delta_pool_scale': 'delta_w', 'delta_w_out': 'delta_w', 'delta_ln_g': 'delta_w', 'delta_ln_b': 'delta_w', 'new_m_w_in': 'new_m', 'new_m_q_norm_g': 'new_m', 'new_m_w_uq': 'new_m', 'new_m_kv_norm_g': 'new_m', 'new_m_w_ukv': 'new_m', 'new_m_pool_w': 'new_m', 'new_m_pool_scale': 'new_m', 'new_m_w_out': 'new_m', 'new_m_ln_g': 'new_m', 'new_m_ln_b': 'new_m', 'new_v_w_in': 'new_v', 'new_v_q_norm_g': 'new_v', 'new_v_w_uq': 'new_v', 'new_v_kv_norm_g': 'new_v', 'new_v_w_ukv': 'new_v', 'new_v_pool_w': 'new_v', 'new_v_pool_scale': 'new_v', 'new_v_w_out': 'new_v', 'new_v_ln_g': 'new_v', 'new_v_ln_b': 'new_v'}


def _forward(args):
    return _fwd_reference(*[args[k] for k in FWD_PARAMS])


def _output_shape():
    out = _jax.eval_shape(lambda: _forward(_fwd_setup_inputs(0)))
    return out.shape, out.dtype

N_MICROBATCH = 1
ADAM_LR = 0.001
ADAM_B1 = 0.9
ADAM_B2 = 0.999
ADAM_EPS = 1e-08
ADAM_WD = 0.01
ADAM_STEP = 10
PER_EXAMPLE_BATCH_AXIS = {'x': 0, 'positions': 0, 'loss_target': 0}
SHARED_INPUTS = []
_WEIGHT_DTYPES = {'w_in': _jnp.float32, 'q_norm_g': _jnp.float32, 'w_uq': _jnp.float32, 'kv_norm_g': _jnp.float32, 'w_ukv': _jnp.float32, 'pool_w': _jnp.float32, 'pool_scale': _jnp.float32, 'w_out': _jnp.float32, 'ln_g': _jnp.float32, 'ln_b': _jnp.float32}
MOMENT_SCALE = {'w_in': 4.495257e-02, 'q_norm_g': 1.344303e-02, 'w_uq': 1.104970e-02, 'kv_norm_g': 3.034760e-02, 'w_ukv': 1.372495e-02, 'pool_w': 6.680379e-02, 'pool_scale': 6.895978e-02, 'w_out': 7.880702e-02, 'ln_g': 6.397332e+01, 'ln_b': 7.447745e-01}


def _to_microbatches(a, axis):
    t = _jnp.moveaxis(a, axis, 0)
    t = t.reshape((N_MICROBATCH, t.shape[0] // N_MICROBATCH) + t.shape[1:])
    return _jnp.moveaxis(t, 1, axis + 1)


def setup_inputs(seed: int = 0) -> dict:
    inp = _fwd_setup_inputs(seed)
    key = _jax.random.fold_in(_jax.random.key(seed), 7919)
    shape, _ = _output_shape()
    out = dict(inp)
    out["loss_target"] = _jax.random.normal(_jax.random.fold_in(key, 0), shape, _jnp.float32)
    for i, name in enumerate(TWIN_WEIGHTS):
        w = inp[name].astype(_jnp.float32)
        if MOMENT_SCALE is None:
            s = _jnp.sqrt(_jnp.mean(_jnp.square(w)) + 1e-30)
        else:
            s = MOMENT_SCALE[name]
        km, kv = _jax.random.split(_jax.random.fold_in(key, i + 1))
        out[name] = w
        out["m_" + name] = s * _jax.random.normal(km, w.shape, _jnp.float32)
        out["v_" + name] = (s * s) * _jax.random.uniform(kv, w.shape, _jnp.float32, 0.5, 1.5)
    if N_MICROBATCH > 1:
        for name, axis in PER_EXAMPLE_BATCH_AXIS.items():
            out[name] = _to_microbatches(out[name], axis)
    return {'x': out['x'], 'positions': out['positions'], 'w_in': out['w_in'], 'q_norm_g': out['q_norm_g'], 'w_uq': out['w_uq'], 'kv_norm_g': out['kv_norm_g'], 'w_ukv': out['w_ukv'], 'pool_w': out['pool_w'], 'pool_scale': out['pool_scale'], 'w_out': out['w_out'], 'ln_g': out['ln_g'], 'ln_b': out['ln_b'], 'loss_target': out['loss_target'], 'm_w_in': out['m_w_in'], 'm_q_norm_g': out['m_q_norm_g'], 'm_w_uq': out['m_w_uq'], 'm_kv_norm_g': out['m_kv_norm_g'], 'm_w_ukv': out['m_w_ukv'], 'm_pool_w': out['m_pool_w'], 'm_pool_scale': out['m_pool_scale'], 'm_w_out': out['m_w_out'], 'm_ln_g': out['m_ln_g'], 'm_ln_b': out['m_ln_b'], 'v_w_in': out['v_w_in'], 'v_q_norm_g': out['v_q_norm_g'], 'v_w_uq': out['v_w_uq'], 'v_kv_norm_g': out['v_kv_norm_g'], 'v_w_ukv': out['v_w_ukv'], 'v_pool_w': out['v_pool_w'], 'v_pool_scale': out['v_pool_scale'], 'v_w_out': out['v_w_out'], 'v_ln_g': out['v_ln_g'], 'v_ln_b': out['v_ln_b']}


def _loss(weights, diff, rest, loss_target):
    with _jax.named_scope("forward"):
        args = {**rest, TWIN_DIFF_INPUT: diff, **{k: w.astype(_WEIGHT_DTYPES[k]) for k, w in weights.items()}}
        y = _forward(args)
    with _jax.named_scope("loss_head"):
        err = _jnp.square(y.astype(_jnp.float32) - loss_target)
        return 0.5 * _jnp.sum(_jnp.mean(err, axis=-1)) if err.ndim else 0.5 * err


def _adamw(w, g, m, v):
    m = ADAM_B1 * m + (1.0 - ADAM_B1) * g
    v = ADAM_B2 * v + (1.0 - ADAM_B2) * _jnp.square(g)
    m_hat = m / (1.0 - ADAM_B1 ** ADAM_STEP)
    v_hat = v / (1.0 - ADAM_B2 ** ADAM_STEP)
    delta = -ADAM_LR * (m_hat / (_jnp.sqrt(v_hat) + ADAM_EPS) + ADAM_WD * w)
    return delta, m, v


def reference(x, positions, w_in, q_norm_g, w_uq, kv_norm_g, w_ukv, pool_w, pool_scale, w_out, ln_g, ln_b, loss_target, m_w_in, m_q_norm_g, m_w_uq, m_kv_norm_g, m_w_ukv, m_pool_w, m_pool_scale, m_w_out, m_ln_g, m_ln_b, v_w_in, v_q_norm_g, v_w_uq, v_kv_norm_g, v_w_ukv, v_pool_w, v_pool_scale, v_w_out, v_ln_g, v_ln_b):
    given = dict(x=x, positions=positions, w_in=w_in, q_norm_g=q_norm_g, w_uq=w_uq, kv_norm_g=kv_norm_g, w_ukv=w_ukv, pool_w=pool_w, pool_scale=pool_scale, w_out=w_out, ln_g=ln_g, ln_b=ln_b, loss_target=loss_target, m_w_in=m_w_in, m_q_norm_g=m_q_norm_g, m_w_uq=m_w_uq, m_kv_norm_g=m_kv_norm_g, m_w_ukv=m_w_ukv, m_pool_w=m_pool_w, m_pool_scale=m_pool_scale, m_w_out=m_w_out, m_ln_g=m_ln_g, m_ln_b=m_ln_b, v_w_in=v_w_in, v_q_norm_g=v_q_norm_g, v_w_uq=v_w_uq, v_kv_norm_g=v_kv_norm_g, v_w_ukv=v_w_ukv, v_pool_w=v_pool_w, v_pool_scale=v_pool_scale, v_w_out=v_w_out, v_ln_g=v_ln_g, v_ln_b=v_ln_b)
    weights = {n: given[n] for n in TWIN_WEIGHTS}
    shared = {n: given[n] for n in SHARED_INPUTS}
    per_example = {n: given[n] for n in ['x', 'positions']}
    grad_fn = _jax.value_and_grad(_loss, argnums=(0, 1))

    def one_microbatch(ex, loss_target):
        ex = dict(ex)
        diff = ex.pop(TWIN_DIFF_INPUT)
        return grad_fn(weights, diff, {**shared, **ex}, loss_target)

    if N_MICROBATCH == 1:
        loss, (grad_w, grad_x) = one_microbatch(per_example, given["loss_target"])
    else:
        def body(carry, xs):
            loss_sum, grad_sum = carry
            l_k, (gw_k, gx_k) = one_microbatch(xs[0], xs[1])
            with _jax.named_scope("update"):
                return (loss_sum + l_k, _jax.tree.map(_jnp.add, grad_sum, gw_k)), gx_k

        init = (_jnp.zeros((), _jnp.float32), _jax.tree.map(_jnp.zeros_like, weights))
        (loss, grad_w), grad_x = _jax.lax.scan(body, init, (per_example, given["loss_target"]))
    with _jax.named_scope("update"):
        delta_w, new_m, new_v = {}, {}, {}
        for n in TWIN_WEIGHTS:
            delta_w[n], new_m[n], new_v[n] = _adamw(weights[n], grad_w[n], given["m_" + n], given["v_" + n])
    return (loss, grad_x, *[grad_w[n] for n in TWIN_WEIGHTS], *[delta_w[n] for n in TWIN_WEIGHTS],
            *[new_m[n] for n in TWIN_WEIGHTS], *[new_v[n] for n in TWIN_WEIGHTS])
```

```python
import functools

import jax
import jax.numpy as jnp
from jax import lax
from jax.experimental import pallas as pl
from jax.experimental.pallas import tpu as pltpu

F32 = jnp.float32
BF16 = jnp.bfloat16
MESH = pl.DeviceIdType.MESH

HEADS = 4
NOPE = 128
ROPE = 64
HEAD_PAD = 256
Q_LORA = 512
KV_LORA = 256
MLA_W = 512
POOL_W = 512
POOL_G = 4
POOL_GD = 128
D_MODEL = 1024
IN_W = 2368
IN_EXT = 2432
ROPE_THETA = 10000.0
RMS_EPS = 1e-6
LN_EPS = 1e-5
ALPHA = 2.0 ** 0.25
SCALE = 192.0 ** -0.5
NEG = float(jnp.finfo(jnp.float32).min)
HALO = 16

ADAM_LR = 0.001
ADAM_B1 = 0.9
ADAM_B2 = 0.999
ADAM_EPS = 1e-08
ADAM_WD = 0.01
ADAM_STEP = 10

N_CHIPS = 4
SHARD_SIZES = (1024 * 592, 512 * 192, 256 * 256, 256 * 1024)
SHARD_ROWS = sum(SHARD_SIZES) // 128
HALF_ROWS = SHARD_ROWS // 2
SMALL_SIZES = (POOL_G * POOL_GD * POOL_GD, Q_LORA, KV_LORA, POOL_W, D_MODEL, D_MODEL, 128)
SMALL_ROWS = 544
SMALL_HALF = SMALL_ROWS // 2

VMEM_LIMIT = 56 * 1024 * 1024


def _cparams(n_grid_dims=0, **kw):
    sem = ("arbitrary",) * n_grid_dims if n_grid_dims else None
    return pltpu.CompilerParams(dimension_semantics=sem, vmem_limit_bytes=VMEM_LIMIT, **kw)


def _full(shape):
    nd = len(shape)
    return pl.BlockSpec(shape, lambda *_: (0,) * nd)


def _dot(a, b):
    return jnp.dot(a, b, preferred_element_type=F32)


def _dot_nt(a, b):
    return lax.dot_general(a, b, (((1,), (1,)), ((), ())), preferred_element_type=F32)


def _dot_tn(a, b):
    return lax.dot_general(a, b, (((0,), (0,)), ((), ())), preferred_element_type=F32)


def _rope(g, c, sa, sb, sign):
    return g * c + sign * (pltpu.roll(g, 32, 1) * sb - pltpu.roll(g, 96, 1) * sa)


def _place():
    x, y, c = lax.axis_index("x"), lax.axis_index("y"), lax.axis_index("c")
    chips = [(1 - x, y), (x, 1 - y), (1 - x, 1 - y)]
    return x, y, c, chips


def _weight_gather(slab):
    def body(in_ref, out_ref, send_sems, recv_sems, local_sem):
        x, y, c, chips = _place()
        me = 2 * x + y
        mine = pltpu.make_async_copy(in_ref, out_ref.at[me], local_sem)
        mine.start()

        def copy(k, chip_idx, half, to, src=None):
            dst = out_ref.at[chip_idx, half]
            return pltpu.make_async_remote_copy(
                src_ref=dst if src is None else src, dst_ref=dst,
                send_sem=send_sems.at[k], recv_sem=recv_sems.at[k],
                device_id=to, device_id_type=MESH)

        first = [copy(j, me, c, (cx, cy, c), src=in_ref.at[c]) for j, (cx, cy) in enumerate(chips)]
        for cp in first:
            cp.start()
        passed = []
        for j, (cx, cy) in enumerate(chips):
            idx = 2 * cx + cy
            copy(j, idx, c, (x, y, c)).wait_recv()
            fwd = copy(3 + j, idx, c, (x, y, 1 - c))
            fwd.start()
            passed.append(fwd)
        for j, (cx, cy) in enumerate(chips):
            copy(3 + j, 2 * cx + cy, 1 - c, (x, y, c)).wait_recv()
        for cp in first + passed:
            cp.wait_send()
        mine.wait()

    return pl.pallas_call(
        body, name="weight_gather",
        out_shape=jax.ShapeDtypeStruct((N_CHIPS,) + slab.shape, slab.dtype),
        in_specs=[pl.BlockSpec(memory_space=pl.ANY)],
        out_specs=pl.BlockSpec(memory_space=pl.ANY),
        scratch_shapes=[pltpu.SemaphoreType.DMA((6,)), pltpu.SemaphoreType.DMA((6,)), pltpu.SemaphoreType.DMA],
    )(slab)


def _grad_to_sibling(g):
    def body(g_ref, r_ref, send_sem, recv_sem):
        x, y, c, _ = _place()
        cp = pltpu.make_async_remote_copy(
            src_ref=g_ref.at[1 - c], dst_ref=r_ref, send_sem=send_sem, recv_sem=recv_sem,
            device_id=(x, y, 1 - c), device_id_type=MESH)
        cp.start()
        cp.wait()

    return pl.pallas_call(
        body, name="grad_to_sibling",
        out_shape=jax.ShapeDtypeStruct(g.shape[1:], g.dtype),
        in_specs=[pl.BlockSpec(memory_space=pl.ANY)],
        out_specs=pl.BlockSpec(memory_space=pl.ANY),
        scratch_shapes=[pltpu.SemaphoreType.DMA, pltpu.SemaphoreType.DMA],
    )(g)


def _grad_to_chips(s):
    def body(s_ref, r_ref, send_sems, recv_sems):
        x, y, c, chips = _place()
        cps = []
        for j, (cx, cy) in enumerate(chips):
            cp = pltpu.make_async_remote_copy(
                src_ref=s_ref.at[2 * cx + cy], dst_ref=r_ref.at[j],
                send_sem=send_sems.at[j], recv_sem=recv_sems.at[j],
                device_id=(cx, cy, c), device_id_type=MESH)
            cp.start()
            cps.append(cp)
        for cp in cps:
            cp.wait()

    return pl.pallas_call(
        body, name="grad_to_chips",
        out_shape=jax.ShapeDtypeStruct((3,) + s.shape[1:], s.dtype),
        in_specs=[pl.BlockSpec(memory_space=pl.ANY)],
        out_specs=pl.BlockSpec(memory_space=pl.ANY),
        scratch_shapes=[pltpu.SemaphoreType.DMA((3,)), pltpu.SemaphoreType.DMA((3,))],
    )(s)


def _halves_exchange(half):
    def body(h_ref, out_ref, send_sem, recv_sem, local_sem):
        x, y, c, _ = _place()
        mine = pltpu.make_async_copy(h_ref, out_ref.at[c], local_sem)
        mine.start()
        cp = pltpu.make_async_remote_copy(
            src_ref=h_ref, dst_ref=out_ref.at[c], send_sem=send_sem, recv_sem=recv_sem,
            device_id=(x, y, 1 - c), device_id_type=MESH)
        cp.start()
        pltpu.make_async_remote_copy(
            src_ref=h_ref, dst_ref=out_ref.at[1 - c], send_sem=send_sem, recv_sem=recv_sem,
            device_id=(x, y, 1 - c), device_id_type=MESH).wait_recv()
        cp.wait_send()
        mine.wait()

    return pl.pallas_call(
        body, name="halves_exchange",
        out_shape=jax.ShapeDtypeStruct((2,) + half.shape, half.dtype),
        in_specs=[pl.BlockSpec(memory_space=pl.ANY)],
        out_specs=pl.BlockSpec(memory_space=pl.ANY),
        scratch_shapes=[pltpu.SemaphoreType.DMA, pltpu.SemaphoreType.DMA, pltpu.SemaphoreType.DMA],
    )(half)


def _small_allreduce(slab):
    hr = SMALL_HALF

    def body(in_ref, out_ref, sib_buf, sum_buf, chip_buf, send_sems, recv_sems):
        x, y, c, chips = _place()
        sib = (x, y, 1 - c)
        a = pltpu.make_async_remote_copy(
            src_ref=in_ref, dst_ref=sib_buf, send_sem=send_sems.at[0], recv_sem=recv_sems.at[0],
            device_id=sib, device_id_type=MESH)
        a.start()
        a.wait()
        sum_buf[...] = in_ref[...] + sib_buf[...]
        rows = pl.ds(pl.multiple_of(c * hr, 8), hr)
        cps = []
        for j, (cx, cy) in enumerate(chips):
            cp = pltpu.make_async_remote_copy(
                src_ref=sum_buf.at[rows], dst_ref=chip_buf.at[j],
                send_sem=send_sems.at[1 + j], recv_sem=recv_sems.at[1 + j],
                device_id=(cx, cy, c), device_id_type=MESH)
            cp.start()
            cps.append(cp)
        for cp in cps:
            cp.wait()
        out_ref[rows, :] = (sum_buf[rows, :] + chip_buf[0]) + (chip_buf[1] + chip_buf[2])
        b = pltpu.make_async_remote_copy(
            src_ref=out_ref.at[rows], dst_ref=out_ref.at[rows],
            send_sem=send_sems.at[4], recv_sem=recv_sems.at[4], device_id=sib, device_id_type=MESH)
        b.start()
        other = pl.ds(pl.multiple_of((1 - c) * hr, 8), hr)
        pltpu.make_async_remote_copy(
            src_ref=out_ref.at[other], dst_ref=out_ref.at[other],
            send_sem=send_sems.at[4], recv_sem=recv_sems.at[4], device_id=sib, device_id_type=MESH).wait_recv()
        b.wait_send()

    vm = pl.BlockSpec(memory_space=pltpu.VMEM)
    return pl.pallas_call(
        body, name="small_allreduce",
        out_shape=jax.ShapeDtypeStruct(slab.shape, slab.dtype),
        in_specs=[vm], out_specs=vm,
        scratch_shapes=[pltpu.VMEM(slab.shape, F32), pltpu.VMEM(slab.shape, F32), pltpu.VMEM((3, hr, 128), F32),
                        pltpu.SemaphoreType.DMA((5,)), pltpu.SemaphoreType.DMA((5,))],
    )(slab)


ROW_TILE = 1008


def _add_sibling_half(g, r, c_arr):
    def body(c_ref, g_ref, r_ref, o_ref):
        o_ref[...] = g_ref[...] + r_ref[...]

    nr = HALF_ROWS // ROW_TILE
    return pl.pallas_call(
        body, name="add_sibling_half",
        out_shape=jax.ShapeDtypeStruct(r.shape, F32),
        grid_spec=pltpu.PrefetchScalarGridSpec(
            num_scalar_prefetch=1, grid=(N_CHIPS, nr),
            in_specs=[pl.BlockSpec((None, None, ROW_TILE, 128), lambda k, i, c: (c[0], k, i, 0)),
                      pl.BlockSpec((None, ROW_TILE, 128), lambda k, i, c: (k, i, 0))],
            out_specs=pl.BlockSpec((None, ROW_TILE, 128), lambda k, i, c: (k, i, 0))),
        compiler_params=_cparams(2),
    )(c_arr, g, r)


def _add_chip_parts(s, r, me_arr):
    def body(me_ref, s_ref, r0, r1, r2, o_ref):
        o_ref[...] = (s_ref[...] + r0[...]) + (r1[...] + r2[...])

    nr = HALF_ROWS // ROW_TILE
    rspec = lambda j: pl.BlockSpec((None, ROW_TILE, 128), lambda i, me: (j, i, 0))
    return pl.pallas_call(
        body, name="add_chip_parts",
        out_shape=jax.ShapeDtypeStruct(s.shape[1:], F32),
        grid_spec=pltpu.PrefetchScalarGridSpec(
            num_scalar_prefetch=1, grid=(nr,),
            in_specs=[pl.BlockSpec((None, ROW_TILE, 128), lambda i, me: (me[0], i, 0)),
                      rspec(0), rspec(1), rspec(2)],
            out_specs=pl.BlockSpec((ROW_TILE, 128), lambda i, me: (i, 0))),
        compiler_params=_cparams(1),
    )(me_arr, s, r, r, r)


def _adamw(g, w, m, v, row_tile):
    def body(g_ref, w_ref, m_ref, v_ref, d_ref, nm_ref, nv_ref):
        gg = g_ref[...]
        nm = ADAM_B1 * m_ref[...] + (1.0 - ADAM_B1) * gg
        nv = ADAM_B2 * v_ref[...] + (1.0 - ADAM_B2) * (gg * gg)
        m_hat = nm / (1.0 - ADAM_B1 ** ADAM_STEP)
        v_hat = nv / (1.0 - ADAM_B2 ** ADAM_STEP)
        d_ref[...] = -ADAM_LR * (m_hat / (jnp.sqrt(v_hat) + ADAM_EPS) + ADAM_WD * w_ref[...])
        nm_ref[...] = nm
        nv_ref[...] = nv

    rows = g.shape[0]
    spec = pl.BlockSpec((row_tile, 128), lambda i: (i, 0))
    shp = jax.ShapeDtypeStruct(g.shape, F32)
    return pl.pallas_call(
        body, name="adamw_%d" % rows, out_shape=(shp, shp, shp), grid=(rows // row_tile,),
        in_specs=[spec] * 4, out_specs=(spec, spec, spec), compiler_params=_cparams(1),
    )(g, w, m, v)


def _fwd_proj(x, w_in_e, w_uq_e, w_ukv, gq, gkv, rc, rsa, rsb, tm):
    T = x.shape[0]

    def body(x_ref, win_ref, wuq_ref, wukv_ref, gq_ref, gkv_ref, c_ref, sa_ref, sb_ref,
             xq_ref, xkv_ref, ga_ref, u_ref, gb_ref, q_ref, k_ref, v_ref):
        h = _dot(x_ref[...].astype(BF16), win_ref[...])
        xq = h[:, 0:512]
        xkv = h[:, 512:768]
        xq_ref[...] = xq
        xkv_ref[...] = xkv
        ga_ref[...] = h[:, 896:1408]
        u_ref[...] = h[:, 1408:1920]
        gb_ref[...] = h[:, 1920:2432]
        c, sa, sb = c_ref[...], sa_ref[...], sb_ref[...]
        rq = lax.rsqrt(jnp.mean(xq * xq, axis=-1, keepdims=True) + RMS_EPS)
        q = _dot(((xq * rq) * gq_ref[...]).astype(BF16), wuq_ref[...])
        rkv = lax.rsqrt(jnp.mean(xkv * xkv, axis=-1, keepdims=True) + RMS_EPS)
        kv = _dot(((xkv * rkv) * gkv_ref[...]).astype(BF16), wukv_ref[...])
        kr = _rope(h[:, 768:896], c, sa, sb, 1.0).astype(BF16)
        for hh in range(HEADS):
            b0 = hh * HEAD_PAD
            q_ref[:, b0:b0 + 128] = q[:, b0:b0 + 128].astype(BF16)
            q_ref[:, b0 + 128:b0 + 256] = _rope(q[:, b0 + 128:b0 + 256], c, sa, sb, 1.0).astype(BF16)
            k_ref[:, b0:b0 + 128] = kv[:, b0:b0 + 128].astype(BF16)
            k_ref[:, b0 + 128:b0 + 256] = kr
            v_ref[:, hh * 128:(hh + 1) * 128] = kv[:, b0 + 128:b0 + 256].astype(BF16)

    row = lambda w: pl.BlockSpec((tm, w), lambda i: (i, 0))
    f = lambda w, dt: jax.ShapeDtypeStruct((T, w), dt)
    return pl.pallas_call(
        body, name="fwd_proj", grid=(T // tm,),
        in_specs=[row(D_MODEL), _full(w_in_e.shape), _full(w_uq_e.shape), _full(w_ukv.shape),
                  _full(gq.shape), _full(gkv.shape), row(128), row(128), row(128)],
        out_specs=(row(512), row(256), row(512), row(512), row(512), row(1024), row(1024), row(512)),
        out_shape=(f(512, F32), f(256, F32), f(512, F32), f(512, F32), f(512, F32),
                   f(1024, BF16), f(1024, BF16), f(512, BF16)),
        compiler_params=_cparams(1),
    )(x, w_in_e, w_uq_e, w_ukv, gq, gkv, rc, rsa, rsb)


def _attn_fwd(q, k, v, pos_col, pos_row, qmax, kmin, nb, S, tq, tk):
    T = q.shape[0]
    nq, nk = S // tq, S // tk

    def body(qmax_ref, kmin_ref, q_ref, k_ref, v_ref, pc_ref, pr_ref, o_ref, lse_ref):
        b, i = pl.program_id(0), pl.program_id(2)
        qb = q_ref[...]
        pq = pc_ref[...]
        my_qmax = qmax_ref[b * nq + i]

        def step(j, carry):
            def process(carry):
                m, l, acc = carry
                off = pl.multiple_of(j * tk, tk)
                s = _dot_nt(qb, k_ref[pl.ds(off, tk), :]) * SCALE
                s = jnp.where(pq >= pr_ref[pl.ds(j, 1), :], s, NEG)
                m_new = jnp.maximum(m, jnp.max(s, axis=1, keepdims=True))
                p = jnp.exp(s - m_new)
                a = jnp.exp(m - m_new)
                l_new = a * l + jnp.sum(p, axis=1, keepdims=True)
                acc_new = a * acc + _dot(p.astype(BF16), v_ref[pl.ds(off, tk), :])
                return m_new, l_new, acc_new
            return lax.cond(kmin_ref[b * nk + j] <= my_qmax, process, lambda cr: cr, carry)

        init = (jnp.full((tq, 1), NEG, F32), jnp.zeros((tq, 1), F32), jnp.zeros((tq, 128), F32))
        m, l, acc = lax.fori_loop(0, nk, step, init)
        o_ref[...] = acc / l
        lse_ref[...] = jnp.broadcast_to(m + jnp.log(l), (tq, 128))

    return pl.pallas_call(
        body, name="attn_fwd",
        grid_spec=pltpu.PrefetchScalarGridSpec(
            num_scalar_prefetch=2, grid=(nb, HEADS, nq),
            in_specs=[pl.BlockSpec((tq, HEAD_PAD), lambda b, h, i, *_: (b * nq + i, h)),
                      pl.BlockSpec((S, HEAD_PAD), lambda b, h, i, *_: (b, h)),
                      pl.BlockSpec((S, 128), lambda b, h, i, *_: (b, h)),
                      pl.BlockSpec((tq, 1), lambda b, h, i, *_: (b * nq + i, 0)),
                      pl.BlockSpec((None, nk, tk), lambda b, h, i, *_: (b, 0, 0))],
            out_specs=(pl.BlockSpec((tq, 128), lambda b, h, i, *_: (b * nq + i, h)),
                       pl.BlockSpec((tq, 128), lambda b, h, i, *_: (b * nq + i, h)))),
        out_shape=(jax.ShapeDtypeStruct((T, MLA_W), F32), jax.ShapeDtypeStruct((T, MLA_W), F32)),
        compiler_params=_cparams(3),
    )(qmax, kmin, q, k, v, pos_col, pos_row.reshape(nb, nk, tk))


def _mid(x, tgt, o, ga, u, gb, w_out, w_out_t, pool_w, pool_w_t, pool_scale, ln_g, ln_b, S, tm):
    T = x.shape[0]
    tps = S // tm
    hb = tm // HALO

    def body(x_ref, tgt_ref, o_ref, ga_ref, u_ref, uh_ref, gb_ref, wout_ref, woutt_ref, pw_ref, pwt_ref,
             ps_ref, lng_ref, lnb_ref,
             dz_ref, do_ref, delta_ref, dga_ref, dgb_ref, dpc_ref,
             dwout_ref, dpw_ref, dps_ref, dlng_ref, dlnb_ref, loss_ref):
        i = pl.program_id(0)

        @pl.when(i == 0)
        def _():
            dwout_ref[...] = jnp.zeros_like(dwout_ref)
            dpw_ref[...] = jnp.zeros_like(dpw_ref)
            dps_ref[...] = jnp.zeros_like(dps_ref)
            dlng_ref[...] = jnp.zeros_like(dlng_ref)
            dlnb_ref[...] = jnp.zeros_like(dlnb_ref)
            loss_ref[...] = jnp.zeros_like(loss_ref)

        seq_tile = i % tps
        tpos = seq_tile * tm + lax.broadcasted_iota(jnp.int32, (tm, 1), 0)
        ga_v = ga_ref[...]
        sig_a = jax.nn.sigmoid(ga_v)
        silu_a = ga_v * sig_a
        o_v = o_ref[...]
        ya = o_v * silu_a

        u_v = u_ref[...]
        halo = jnp.where(seq_tile == 0, 0.0, uh_ref[...])
        pooled, cnts, mixed = [], [], []
        for g in range(POOL_G):
            lanes = slice(g * POOL_GD, (g + 1) * POOL_GD)
            w = jnp.concatenate([halo[:, lanes], u_v[:, lanes]], axis=0)
            for st in range(g + 1):
                w = w + pltpu.roll(w, 1 << st, 0)
            cnt = jnp.minimum(tpos + 1, 2 << g).astype(F32)
            pg = (w[HALO:, :] / cnt - u_v[:, lanes]).astype(BF16)
            pooled.append(pg)
            cnts.append(cnt)
            mixed.append(_dot(pg, pw_ref[g]))
        mixed = jnp.concatenate(mixed, axis=1)
        ps = ps_ref[...]
        ybp = mixed * ps
        gb_v = gb_ref[...]
        sig_b = jax.nn.sigmoid(gb_v)
        silu_b = gb_v * sig_b
        yb = ybp * silu_b

        cat = jnp.concatenate([ya, yb], axis=1).astype(BF16)
        z = ALPHA * x_ref[...] + _dot(cat, wout_ref[...])
        mu = jnp.mean(z, axis=-1, keepdims=True)
        zc = z - mu
        rstd = lax.rsqrt(jnp.mean(zc * zc, axis=-1, keepdims=True) + LN_EPS)
        zhat = zc * rstd
        lng = lng_ref[...]
        err = (zhat * lng + lnb_ref[...]) - tgt_ref[...]
        row_loss = jnp.sum(err * err, axis=1, keepdims=True)
        loss_ref[...] += jnp.broadcast_to(jnp.sum(row_loss, axis=0, keepdims=True) * (0.5 / D_MODEL), (1, 128))
        dy = err * (1.0 / D_MODEL)
        dlng_ref[...] += jnp.sum(dy * zhat, axis=0, keepdims=True)
        dlnb_ref[...] += jnp.sum(dy, axis=0, keepdims=True)
        dzh = dy * lng
        dz = rstd * (dzh - jnp.mean(dzh, axis=-1, keepdims=True)
                     - zhat * jnp.mean(dzh * zhat, axis=-1, keepdims=True))
        dz_ref[...] = dz
        dzb = dz.astype(BF16)
        dwout_ref[...] += _dot_tn(cat, dzb)
        dcat = _dot(dzb, woutt_ref[...])
        dya = dcat[:, :MLA_W]
        dyb = dcat[:, MLA_W:]

        do = dya * silu_a
        do_ref[...] = do.astype(BF16)
        prod = do * o_v
        for hh in range(HEADS):
            lanes = slice(hh * 128, (hh + 1) * 128)
            delta_ref[:, lanes] = jnp.broadcast_to(jnp.sum(prod[:, lanes], axis=1, keepdims=True), (tm, 128))
        dga_ref[...] = (dya * o_v * (sig_a * (1.0 + ga_v * (1.0 - sig_a)))).astype(BF16)
        dgb_ref[...] = (dyb * ybp * (sig_b * (1.0 + gb_v * (1.0 - sig_b)))).astype(BF16)
        dybp = dyb * silu_b
        dps_ref[...] += jnp.sum(dybp * mixed, axis=0, keepdims=True)
        dmixed = (dybp * ps).astype(BF16)
        for g in range(POOL_G):
            lanes = slice(g * POOL_GD, (g + 1) * POOL_GD)
            dpw_ref[g] += _dot_tn(pooled[g], dmixed[:, lanes])
            dpc_ref[:, lanes] = _dot(dmixed[:, lanes], pwt_ref[g]) / cnts[g]

    row = lambda w: pl.BlockSpec((tm, w), lambda i: (i, 0))
    f = lambda w, dt: jax.ShapeDtypeStruct((T, w), dt)
    halo_spec = pl.BlockSpec((HALO, POOL_W), lambda i: (jnp.maximum(i * hb - 1, 0), 0))
    return pl.pallas_call(
        body, name="mid", grid=(T // tm,),
        in_specs=[row(D_MODEL), row(D_MODEL), row(MLA_W), row(MLA_W), row(POOL_W), halo_spec, row(POOL_W),
                  _full(w_out.shape), _full(w_out_t.shape), _full(pool_w.shape), _full(pool_w_t.shape),
                  _full(pool_scale.shape), _full(ln_g.shape), _full(ln_b.shape)],
        out_specs=(row(D_MODEL), row(MLA_W), row(MLA_W), row(MLA_W), row(POOL_W), row(POOL_W),
                   _full((D_MODEL, D_MODEL)), _full(pool_w.shape), _full((1, POOL_W)),
                   _full((1, D_MODEL)), _full((1, D_MODEL)), _full((1, 128))),
        out_shape=(f(D_MODEL, F32), f(MLA_W, BF16), f(MLA_W, F32), f(MLA_W, BF16), f(POOL_W, BF16), f(POOL_W, F32),
                   jax.ShapeDtypeStruct((D_MODEL, D_MODEL), F32), jax.ShapeDtypeStruct(pool_w.shape, F32),
                   jax.ShapeDtypeStruct((1, POOL_W), F32), jax.ShapeDtypeStruct((1, D_MODEL), F32),
                   jax.ShapeDtypeStruct((1, D_MODEL), F32), jax.ShapeDtypeStruct((1, 128), F32)),
        compiler_params=_cparams(1),
    )(x, tgt, o, ga, u, u, gb, w_out, w_out_t, pool_w, pool_w_t, pool_scale, ln_g, ln_b)


def _attn_bwd(q, k, v, do, lse, delta, pos_col, pos_row, qmax, kmin, nb, S, tq, tk):
    T = q.shape[0]
    nq, nk = S // tq, S // tk
    reps = tk // 128

    def body(qmax_ref, kmin_ref, q_ref, k_ref, v_ref, do_ref, lse_ref, dl_ref, pc_ref, pr_ref,
             dq_ref, dk_ref, dv_ref):
        b, j = pl.program_id(0), pl.program_id(2)

        @pl.when(j == 0)
        def _():
            dq_ref[...] = jnp.zeros_like(dq_ref)

        kb = k_ref[...]
        vb = v_ref[...]
        pk = pr_ref[...]
        my_kmin = kmin_ref[b * nk + j]

        def step(i, carry):
            def process(carry):
                dk, dv = carry
                rows = pl.ds(pl.multiple_of(i * tq, tq), tq)
                qb = q_ref[rows, :]
                dob = do_ref[rows, :]
                s = _dot_nt(qb, kb) * SCALE
                s = jnp.where(pc_ref[rows, :] >= pk, s, NEG)
                p = jnp.exp(s - jnp.concatenate([lse_ref[rows, :]] * reps, axis=1))
                dv = dv + _dot_tn(p.astype(BF16), dob)
                dp = _dot_nt(dob, vb)
                ds = (p * (dp - jnp.concatenate([dl_ref[rows, :]] * reps, axis=1)) * SCALE).astype(BF16)
                dq_ref[rows, :] += _dot(ds, kb)
                dk = dk + _dot_tn(ds, qb)
                return dk, dv
            return lax.cond(my_kmin <= qmax_ref[b * nq + i], process, lambda cr: cr, carry)

        dk, dv = lax.fori_loop(0, nq, step, (jnp.zeros((tk, HEAD_PAD), F32), jnp.zeros((tk, 128), F32)))
        dk_ref[...] = dk
        dv_ref[...] = dv

    seq = lambda w: pl.BlockSpec((S, w), lambda b, h, j, *_: (b, h))
    return pl.pallas_call(
        body, name="attn_bwd",
        grid_spec=pltpu.PrefetchScalarGridSpec(
            num_scalar_prefetch=2, grid=(nb, HEADS, nk),
            in_specs=[seq(HEAD_PAD),
                      pl.BlockSpec((tk, HEAD_PAD), lambda b, h, j, *_: (b * nk + j, h)),
                      pl.BlockSpec((tk, 128), lambda b, h, j, *_: (b * nk + j, h)),
                      seq(128), seq(128), seq(128),
                      pl.BlockSpec((S, 1), lambda b, h, j, *_: (b, 0)),
                      pl.BlockSpec((None, 1, tk), lambda b, h, j, *_: (b, 0, j))],
            out_specs=(seq(HEAD_PAD),
                       pl.BlockSpec((tk, HEAD_PAD), lambda b, h, j, *_: (b * nk + j, h)),
                       pl.BlockSpec((tk, 128), lambda b, h, j, *_: (b * nk + j, h)))),
        out_shape=(jax.ShapeDtypeStruct((T, HEADS * HEAD_PAD), F32),
                   jax.ShapeDtypeStruct((T, HEADS * HEAD_PAD), F32),
                   jax.ShapeDtypeStruct((T, MLA_W), F32)),
        compiler_params=_cparams(3),
    )(qmax, kmin, q, k, v, do, lse, delta, pos_col, pos_row)


def _bwd_proj(dq, dk, dv, xq, xkv, x, dz, dga, dgb, dpc, rc, rsa, rsb,
              w_uq_t, w_ukv_t, w_in_t, gq, gkv, S, tm):
    T = x.shape[0]
    tps = S // tm
    hb = tm // HALO
    n_steps = T // tm

    def body(dq_ref, dk_ref, dv_ref, xq_ref, xkv_ref, x_ref, dz_ref, dga_ref, dgb_ref, dpc_ref, dph_ref,
             c_ref, sa_ref, sb_ref, wuqt_ref, wukvt_ref, wint_ref, gq_ref, gkv_ref,
             dx_ref, dwin_hbm, dwuq_hbm, dwukv_hbm, dgq_ref, dgkv_ref,
             acc_win, acc_wuq, acc_wukv):
        i = pl.program_id(0)

        @pl.when(i == 0)
        def _():
            acc_win[...] = jnp.zeros_like(acc_win)
            acc_wuq[...] = jnp.zeros_like(acc_wuq)
            acc_wukv[...] = jnp.zeros_like(acc_wukv)
            dgq_ref[...] = jnp.zeros_like(dgq_ref)
            dgkv_ref[...] = jnp.zeros_like(dgkv_ref)

        c, sa, sb = c_ref[...], sa_ref[...], sb_ref[...]
        dq_v = dq_ref[...]
        dk_v = dk_ref[...]
        dv_v = dv_ref[...]
        dq_parts, dkv_parts = [], []
        dkr = jnp.zeros((tm, 128), F32)
        for hh in range(HEADS):
            b0 = hh * HEAD_PAD
            dq_parts.append(dq_v[:, b0:b0 + 128].astype(BF16))
            dq_parts.append(_rope(dq_v[:, b0 + 128:b0 + 256], c, sa, sb, -1.0).astype(BF16))
            dkv_parts.append(dk_v[:, b0:b0 + 128].astype(BF16))
            dkv_parts.append(dv_v[:, hh * 128:(hh + 1) * 128].astype(BF16))
            dkr = dkr + dk_v[:, b0 + 128:b0 + 256]
        dqp = jnp.concatenate(dq_parts, axis=1)
        dkvp = jnp.concatenate(dkv_parts, axis=1)
        dkrr = _rope(dkr, c, sa, sb, -1.0)

        def rms_bwd(xv, g, dyn, dg_ref):
            r = lax.rsqrt(jnp.mean(xv * xv, axis=-1, keepdims=True) + RMS_EPS)
            xhat = xv * r
            dg_ref[...] += jnp.sum(dyn * xhat, axis=0, keepdims=True)
            dxh = dyn * g
            return r * (dxh - xhat * jnp.mean(dxh * xhat, axis=-1, keepdims=True))

        xq_v = xq_ref[...]
        gq_v = gq_ref[...]
        rq = lax.rsqrt(jnp.mean(xq_v * xq_v, axis=-1, keepdims=True) + RMS_EPS)
        acc_wuq[...] += _dot_tn(((xq_v * rq) * gq_v).astype(BF16), dqp)
        dxq = rms_bwd(xq_v, gq_v, _dot(dqp, wuqt_ref[...]), dgq_ref)

        xkv_v = xkv_ref[...]
        gkv_v = gkv_ref[...]
        rkv = lax.rsqrt(jnp.mean(xkv_v * xkv_v, axis=-1, keepdims=True) + RMS_EPS)
        acc_wukv[...] += _dot_tn(((xkv_v * rkv) * gkv_v).astype(BF16), dkvp)
        dxkv = rms_bwd(xkv_v, gkv_v, _dot(dkvp, wukvt_ref[...]), dgkv_ref)

        seq_tile = i % tps
        tpos = seq_tile * tm + lax.broadcasted_iota(jnp.int32, (tm, 1), 0)
        dpc_v = dpc_ref[...]
        halo = jnp.where(seq_tile == tps - 1, 0.0, dph_ref[...])
        n = tm + HALO
        du = []
        for g in range(POOL_G):
            lanes = slice(g * POOL_GD, (g + 1) * POOL_GD)
            f = jnp.concatenate([dpc_v[:, lanes], halo[:, lanes]], axis=0)
            for st in range(g + 1):
                f = f + pltpu.roll(f, n - (1 << st), 0)
            cnt = jnp.minimum(tpos + 1, 2 << g).astype(F32)
            du.append((f[:tm, :] - dpc_v[:, lanes] * cnt).astype(BF16))

        dh = jnp.concatenate([dxq.astype(BF16), dxkv.astype(BF16), dkrr.astype(BF16), dga_ref[...]]
                             + du + [dgb_ref[...]], axis=1)
        dx_ref[...] = ALPHA * dz_ref[...] + _dot(dh, wint_ref[...])
        acc_win[...] += _dot_tn(x_ref[...].astype(BF16), dh)

        @pl.when(i == n_steps - 1)
        def _():
            pltpu.sync_copy(acc_win, dwin_hbm)
            pltpu.sync_copy(acc_wuq, dwuq_hbm)
            pltpu.sync_copy(acc_wukv, dwukv_hbm)

    row = lambda w: pl.BlockSpec((tm, w), lambda i: (i, 0))
    halo_spec = pl.BlockSpec((HALO, POOL_W), lambda i: (jnp.minimum((i + 1) * hb, T // HALO - 1), 0))
    anyspec = pl.BlockSpec(memory_space=pl.ANY)
    return pl.pallas_call(
        body, name="bwd_proj", grid=(n_steps,),
        in_specs=[row(1024), row(1024), row(512), row(512), row(256), row(D_MODEL), row(D_MODEL),
                  row(512), row(512), row(512), halo_spec, row(128), row(128), row(128),
                  _full(w_uq_t.shape), _full(w_ukv_t.shape), _full(w_in_t.shape), _full(gq.shape), _full(gkv.shape)],
        out_specs=(row(D_MODEL), anyspec, anyspec, anyspec, _full((1, Q_LORA)), _full((1, KV_LORA))),
        out_shape=(jax.ShapeDtypeStruct((T, D_MODEL), F32),
                   jax.ShapeDtypeStruct((D_MODEL, IN_EXT), F32),
                   jax.ShapeDtypeStruct((Q_LORA, HEADS * HEAD_PAD), F32),
                   jax.ShapeDtypeStruct((KV_LORA, 1024), F32),
                   jax.ShapeDtypeStruct((1, Q_LORA), F32), jax.ShapeDtypeStruct((1, KV_LORA), F32)),
        scratch_shapes=[pltpu.VMEM((D_MODEL, IN_EXT), F32), pltpu.VMEM((Q_LORA, HEADS * HEAD_PAD), F32),
                        pltpu.VMEM((KV_LORA, 1024), F32)],
        compiler_params=_cparams(1),
    )(dq, dk, dv, xq, xkv, x, dz, dga, dgb, dpc, dpc, rc, rsa, rsb, w_uq_t, w_ukv_t, w_in_t, gq, gkv)


def _pack_shard(a_in, a_uq, a_ukv, a_out):
    flat = jnp.concatenate([a_in.reshape(-1), a_uq.reshape(-1), a_ukv.reshape(-1), a_out.reshape(-1)])
    return flat.reshape(SHARD_ROWS, 128)


def _unpack_shard(slab):
    flat = slab.reshape(-1)
    o0, o1, o2 = SHARD_SIZES[0], SHARD_SIZES[0] + SHARD_SIZES[1], SHARD_SIZES[0] + SHARD_SIZES[1] + SHARD_SIZES[2]
    return (flat[:o0].reshape(1024, 592), flat[o0:o1].reshape(512, 192),
            flat[o1:o2].reshape(256, 256), flat[o2:].reshape(256, 1024))


def _pack_small(pool_w, gq, gkv, pool_scale, ln_g, ln_b, last):
    flat = jnp.concatenate([pool_w.reshape(-1), gq.reshape(-1), gkv.reshape(-1), pool_scale.reshape(-1),
                            ln_g.reshape(-1), ln_b.reshape(-1), last.reshape(-1)])
    flat = jnp.pad(flat, (0, SMALL_ROWS * 128 - flat.shape[0]))
    return flat.reshape(SMALL_ROWS, 128)


def _unpack_small(slab):
    flat = slab.reshape(-1)
    outs, off = [], 0
    for n in SMALL_SIZES:
        outs.append(flat[off:off + n])
        off += n
    pw, gq, gkv, ps, lg, lb, last = outs
    return (pw.reshape(POOL_G, POOL_GD, POOL_GD), gq, gkv, ps, lg.reshape(1, D_MODEL), lb.reshape(1, D_MODEL), last)


def kernel(x, positions, w_in, q_norm_g, w_uq, kv_norm_g, w_ukv, pool_w, pool_scale, w_out, ln_g, ln_b, loss_target, m_w_in, m_q_norm_g, m_w_uq, m_kv_norm_g, m_w_ukv, m_pool_w, m_pool_scale, m_w_out, m_ln_g, m_ln_b, v_w_in, v_q_norm_g, v_w_uq, v_kv_norm_g, v_w_ukv, v_pool_w, v_pool_scale, v_w_out, v_ln_g, v_ln_b):
    nb, S, _ = x.shape
    T = nb * S
    tm = min(256, S)
    tq = min(256, S)
    tk = min(256, S)
    assert S % tm == 0 and tm % HALO == 0 and S % tq == 0 and S % tk == 0

    cx, cy, cc = lax.axis_index("x"), lax.axis_index("y"), lax.axis_index("c")
    c_arr = jnp.reshape(cc, (1,)).astype(jnp.int32)
    me_arr = jnp.reshape(2 * cx + cy, (1,)).astype(jnp.int32)

    slab = _pack_shard(w_in, w_uq, w_ukv, w_out).astype(BF16).reshape(2, HALF_ROWS, 128)
    gathered = _weight_gather(slab).reshape(N_CHIPS, SHARD_ROWS, 128)
    parts = [_unpack_shard(gathered[k]) for k in range(N_CHIPS)]
    w_in_f = jnp.concatenate([p[0] for p in parts], axis=1)
    w_uq_f = jnp.concatenate([p[1] for p in parts], axis=1)
    w_ukv_f = jnp.concatenate([p[2] for p in parts], axis=1)
    w_out_f = jnp.concatenate([p[3] for p in parts], axis=0)
    w_in_e = jnp.concatenate([w_in_f[:, :832], jnp.zeros((D_MODEL, 64), BF16), w_in_f[:, 832:]], axis=1)
    w_uq_e = jnp.pad(w_uq_f.reshape(Q_LORA, HEADS, NOPE + ROPE), ((0, 0), (0, 0), (0, 64))).reshape(Q_LORA, HEADS * HEAD_PAD)
    pool_w_b = pool_w.astype(BF16)
    pool_w_t = jnp.swapaxes(pool_w_b, 1, 2)
    gq2 = q_norm_g.reshape(1, Q_LORA)
    gkv2 = kv_norm_g.reshape(1, KV_LORA)
    ps2 = pool_scale.reshape(1, POOL_W)

    half = ROPE // 2
    inv_freq = ROPE_THETA ** (-jnp.arange(half, dtype=F32) / half)
    ang = positions.astype(F32).reshape(T, 1) * inv_freq
    cos, sin = jnp.cos(ang), jnp.sin(ang)
    z32, z64 = jnp.zeros((T, 32), F32), jnp.zeros((T, 64), F32)
    rc = jnp.concatenate([cos, cos, z64], axis=1)
    rsa = jnp.concatenate([sin, z32, z64], axis=1)
    rsb = jnp.concatenate([z32, sin, z64], axis=1)
    pos_col = positions.reshape(T, 1)
    pos_row = positions.reshape(nb, 1, S)
    qmax = jnp.max(positions.reshape(nb, S // tq, tq), axis=2).reshape(-1)
    kmin = jnp.min(positions.reshape(nb, S // tk, tk), axis=2).reshape(-1)

    xf = x.reshape(T, D_MODEL)
    tgt = loss_target.reshape(T, D_MODEL)

    xq, xkv, ga, u, gb, q, k, v = _fwd_proj(xf, w_in_e, w_uq_e, w_ukv_f, gq2, gkv2, rc, rsa, rsb, tm)
    o, lse = _attn_fwd(q, k, v, pos_col, pos_row, qmax, kmin, nb, S, tq, tk)

    (dz, do, delta, dga, dgb, dpc, d_w_out, d_pool_w, d_pool_scale, d_ln_g, d_ln_b, loss_part) = _mid(
        xf, tgt, o, ga, u, gb, w_out_f, w_out_f.T, pool_w_b, pool_w_t, ps2, ln_g, ln_b, S, tm)

    dq, dk, dv = _attn_bwd(q, k, v, do, lse, delta, pos_col, pos_row, qmax, kmin, nb, S, tq, tk)
    dx, d_w_in_e, d_w_uq_e, d_w_ukv, d_gq, d_gkv = _bwd_proj(
        dq, dk, dv, xq, xkv, xf, dz, dga, dgb, dpc, rc, rsa, rsb,
        w_uq_e.T, w_ukv_f.T, w_in_e.T, gq2, gkv2, S, tm)
    grad_x = dx.reshape(nb, S, D_MODEL)

    d_w_in = jnp.concatenate([d_w_in_e[:, :832], d_w_in_e[:, 896:]], axis=1)
    d_w_uq = d_w_uq_e.reshape(Q_LORA, HEADS, HEAD_PAD)[:, :, :NOPE + ROPE].reshape(Q_LORA, HEADS * (NOPE + ROPE))
    per_chip = [_pack_shard(d_w_in[:, 592 * kk:592 * (kk + 1)], d_w_uq[:, 192 * kk:192 * (kk + 1)],
                            d_w_ukv[:, 256 * kk:256 * (kk + 1)], d_w_out[256 * kk:256 * (kk + 1), :])
                for kk in range(N_CHIPS)]
    g_all = jnp.stack(per_chip).reshape(N_CHIPS, 2, HALF_ROWS, 128).transpose(1, 0, 2, 3)
    from_sibling = _grad_to_sibling(g_all)
    chip_sum = _add_sibling_half(g_all, from_sibling, c_arr)
    from_chips = _grad_to_chips(chip_sum)
    my_half = _add_chip_parts(chip_sum, from_chips, me_arr)
    g_shard = _halves_exchange(my_half).reshape(SHARD_ROWS, 128)

    small = _pack_small(d_pool_w, d_gq, d_gkv, d_pool_scale, d_ln_g, d_ln_b, loss_part)
    small_sum = _small_allreduce(small)

    d_s, m_s, v_s = _adamw(g_shard, _pack_shard(w_in, w_uq, w_ukv, w_out), _pack_shard(m_w_in, m_w_uq, m_w_ukv, m_w_out),
                           _pack_shard(v_w_in, v_w_uq, v_w_ukv, v_w_out), ROW_TILE)
    zero_tail = jnp.zeros((128,), F32)
    d_r, m_r, v_r = _adamw(
        small_sum,
        _pack_small(pool_w, q_norm_g, kv_norm_g, pool_scale, ln_g, ln_b, zero_tail),
        _pack_small(m_pool_w, m_q_norm_g, m_kv_norm_g, m_pool_scale, m_ln_g, m_ln_b, zero_tail),
        _pack_small(v_pool_w, v_q_norm_g, v_kv_norm_g, v_pool_scale, v_ln_g, v_ln_b, zero_tail + 1.0),
        SMALL_ROWS)

    big = [_unpack_shard(a) for a in (g_shard, d_s, m_s, v_s)]
    sm = [_unpack_small(a) for a in (small_sum, d_r, m_r, v_r)]
    loss = sm[0][6][0]

    def leaves(b, s):
        return (b[0], s[1], b[1], s[2], b[2], s[0], s[3], b[3], s[4], s[5])

    return (loss, grad_x) + leaves(big[0], sm[0]) + leaves(big[1], sm[1]) + leaves(big[2], sm[2]) + leaves(big[3], sm[3])
```

```python
import functools

import jax
import jax.numpy as jnp
from jax import lax
from jax.experimental import pallas as pl
from jax.experimental.pallas import tpu as pltpu

F32 = jnp.float32
BF16 = jnp.bfloat16
MESH = pl.DeviceIdType.MESH

HEADS = 4
NOPE = 128
ROPE = 64
HEAD_PAD = 256
Q_LORA = 512
KV_LORA = 256
MLA_W = 512
POOL_W = 512
POOL_G = 4
POOL_GD = 128
D_MODEL = 1024
IN_W = 2368
IN_EXT = 2432
ROPE_THETA = 10000.0
RMS_EPS = 1e-6
LN_EPS = 1e-5
ALPHA = 2.0 ** 0.25
SCALE = 192.0 ** -0.5
LOG2E = 1.4426950408889634
LN2 = 0.6931471805599453
QSCALE = SCALE * LOG2E
NEG = float(jnp.finfo(jnp.float32).min)
HALO = 16

ADAM_LR = 0.001
ADAM_B1 = 0.9
ADAM_B2 = 0.999
ADAM_EPS = 1e-08
ADAM_WD = 0.01
ADAM_STEP = 10

N_CHIPS = 4
N_BIG = 4
VEC_ROWS = 16
VEC_HALF = VEC_ROWS // 2

VMEM_LIMIT = 56 * 1024 * 1024


def _cparams(n_grid_dims=0, **kw):
    sem = ("arbitrary",) * n_grid_dims if n_grid_dims else None
    return pltpu.CompilerParams(dimension_semantics=sem, vmem_limit_bytes=VMEM_LIMIT, **kw)


def _full(shape):
    nd = len(shape)
    return pl.BlockSpec(shape, lambda *_: (0,) * nd)


def _dot(a, b):
    return jnp.dot(a, b, preferred_element_type=F32)


def _dot_nt(a, b):
    return lax.dot_general(a, b, (((1,), (1,)), ((), ())), preferred_element_type=F32)


def _dot_tn(a, b):
    return lax.dot_general(a, b, (((0,), (0,)), ((), ())), preferred_element_type=F32)


def _rope(g, c, sa, sb, sign):
    return g * c + sign * (pltpu.roll(g, 32, 1) * sb - pltpu.roll(g, 96, 1) * sa)


def _place():
    x, y, c = lax.axis_index("x"), lax.axis_index("y"), lax.axis_index("c")
    chips = [(1 - x, y), (x, 1 - y), (1 - x, 1 - y)]
    return x, y, c, chips


def _half_rows(ref, half_index, lead=None):
    axis = 0 if lead is None else 1
    hr = ref.shape[axis] // 2
    rows = pl.ds(half_index * hr, hr)
    return ref.at[rows] if lead is None else ref.at[lead, rows]


ANY = pl.BlockSpec(memory_space=pl.ANY)


def _weight_gather(shards):
    n = len(shards)

    def body(*refs):
        ins, outs = refs[:n], refs[n:2 * n]
        send_sems, recv_sems, local_sems = refs[2 * n:]
        x, y, c, chips = _place()
        me = 2 * x + y
        local = [pltpu.make_async_copy(ins[t], outs[t].at[me], local_sems.at[t]) for t in range(n)]
        for cp in local:
            cp.start()

        def copy(t, k, chip_idx, half, to, src=None):
            dst = _half_rows(outs[t], half, lead=chip_idx)
            return pltpu.make_async_remote_copy(
                src_ref=dst if src is None else src, dst_ref=dst,
                send_sem=send_sems.at[6 * t + k], recv_sem=recv_sems.at[6 * t + k],
                device_id=to, device_id_type=MESH)

        first = [copy(t, j, me, c, (cx, cy, c), src=_half_rows(ins[t], c))
                 for t in range(n) for j, (cx, cy) in enumerate(chips)]
        for cp in first:
            cp.start()
        passed = []
        for j, (cx, cy) in enumerate(chips):
            for t in range(n):
                copy(t, j, 2 * cx + cy, c, (x, y, c)).wait_recv()
                fwd = copy(t, 3 + j, 2 * cx + cy, c, (x, y, 1 - c))
                fwd.start()
                passed.append(fwd)
        for j, (cx, cy) in enumerate(chips):
            for t in range(n):
                copy(t, 3 + j, 2 * cx + cy, 1 - c, (x, y, c)).wait_recv()
        for cp in first + passed:
            cp.wait_send()
        for cp in local:
            cp.wait()

    return pl.pallas_call(
        body, name="weight_gather",
        out_shape=tuple(jax.ShapeDtypeStruct((N_CHIPS,) + a.shape, a.dtype) for a in shards),
        in_specs=[ANY] * n, out_specs=(ANY,) * n,
        scratch_shapes=[pltpu.SemaphoreType.DMA((6 * n,)), pltpu.SemaphoreType.DMA((6 * n,)),
                        pltpu.SemaphoreType.DMA((n,))],
    )(*shards)


def _grad_to_sibling(gs):
    n = len(gs)

    def body(*refs):
        g_refs, r_refs = refs[:n], refs[n:2 * n]
        send_sems, recv_sems = refs[2 * n:]
        x, y, c, _ = _place()
        cps = []
        for t in range(n):
            hr = gs[t].shape[1] // 2
            src = g_refs[t].at[pl.ds(0, N_CHIPS), pl.ds((1 - c) * hr, hr)]
            cp = pltpu.make_async_remote_copy(
                src_ref=src, dst_ref=r_refs[t], send_sem=send_sems.at[t], recv_sem=recv_sems.at[t],
                device_id=(x, y, 1 - c), device_id_type=MESH)
            cp.start()
            cps.append(cp)
        for cp in cps:
            cp.wait()

    return pl.pallas_call(
        body, name="grad_to_sibling",
        out_shape=tuple(jax.ShapeDtypeStruct((N_CHIPS, g.shape[1] // 2, g.shape[2]), F32) for g in gs),
        in_specs=[ANY] * n, out_specs=(ANY,) * n,
        scratch_shapes=[pltpu.SemaphoreType.DMA((n,)), pltpu.SemaphoreType.DMA((n,))],
    )(*gs)


def _grad_to_chips(ss):
    n = len(ss)

    def body(*refs):
        s_refs, r_refs = refs[:n], refs[n:2 * n]
        send_sems, recv_sems = refs[2 * n:]
        x, y, c, chips = _place()
        cps = []
        for t in range(n):
            for j, (cx, cy) in enumerate(chips):
                cp = pltpu.make_async_remote_copy(
                    src_ref=s_refs[t].at[2 * cx + cy], dst_ref=r_refs[t].at[j],
                    send_sem=send_sems.at[3 * t + j], recv_sem=recv_sems.at[3 * t + j],
                    device_id=(cx, cy, c), device_id_type=MESH)
                cp.start()
                cps.append(cp)
        for cp in cps:
            cp.wait()

    return pl.pallas_call(
        body, name="grad_to_chips",
        out_shape=tuple(jax.ShapeDtypeStruct((3,) + s.shape[1:], F32) for s in ss),
        in_specs=[ANY] * n, out_specs=(ANY,) * n,
        scratch_shapes=[pltpu.SemaphoreType.DMA((3 * n,)), pltpu.SemaphoreType.DMA((3 * n,))],
    )(*ss)


def _halves_exchange(hs):
    n = len(hs)

    def body(*refs):
        h_refs, o_refs = refs[:n], refs[n:2 * n]
        send_sems, recv_sems, local_sems = refs[2 * n:]
        x, y, c, _ = _place()
        sib = (x, y, 1 - c)
        local, cps = [], []
        for t in range(n):
            mine = pltpu.make_async_copy(h_refs[t], _half_rows(o_refs[t], c), local_sems.at[t])
            mine.start()
            local.append(mine)
            cp = pltpu.make_async_remote_copy(
                src_ref=h_refs[t], dst_ref=_half_rows(o_refs[t], c),
                send_sem=send_sems.at[t], recv_sem=recv_sems.at[t], device_id=sib, device_id_type=MESH)
            cp.start()
            cps.append(cp)
        for t in range(n):
            pltpu.make_async_remote_copy(
                src_ref=h_refs[t], dst_ref=_half_rows(o_refs[t], 1 - c),
                send_sem=send_sems.at[t], recv_sem=recv_sems.at[t], device_id=sib, device_id_type=MESH).wait_recv()
        for cp in cps:
            cp.wait_send()
        for cp in local:
            cp.wait()

    return pl.pallas_call(
        body, name="halves_exchange",
        out_shape=tuple(jax.ShapeDtypeStruct((2 * h.shape[0], h.shape[1]), F32) for h in hs),
        in_specs=[ANY] * n, out_specs=(ANY,) * n,
        scratch_shapes=[pltpu.SemaphoreType.DMA((n,)), pltpu.SemaphoreType.DMA((n,)), pltpu.SemaphoreType.DMA((n,))],
    )(*hs)


def _small_allreduce(d_pool_w, d_ln_g, d_ln_b, d_ps, d_gq, d_gkv, loss_part):
    pw_rows = POOL_G * POOL_GD

    def body(pw_in, lng_in, lnb_in, ps_in, gq_in, gkv_in, loss_in, pw_out, vec_out,
             vec_in, pw_sib, vec_sib, pw_sum, vec_sum, pw_chip, vec_chip, send_sems, recv_sems):
        x, y, c, chips = _place()
        sib = (x, y, 1 - c)
        vec_in[...] = jnp.zeros_like(vec_in)
        vec_in[0:1, :] = lng_in[...]
        vec_in[1:2, :] = lnb_in[...]
        vec_in[2:3, 0:POOL_W] = ps_in[...]
        vec_in[3:4, 0:Q_LORA] = gq_in[...]
        vec_in[8:9, 0:KV_LORA] = gkv_in[...]
        vec_in[9:10, 0:128] = loss_in[...]

        def rdma(k, src, dst, to):
            return pltpu.make_async_remote_copy(src_ref=src, dst_ref=dst, send_sem=send_sems.at[k],
                                                recv_sem=recv_sems.at[k], device_id=to, device_id_type=MESH)

        a = [rdma(0, pw_in, pw_sib, sib), rdma(1, vec_in, vec_sib, sib)]
        for cp in a:
            cp.start()
        for cp in a:
            cp.wait()
        pw_sum[...] = pw_in[...] + pw_sib[...]
        vec_sum[...] = vec_in[...] + vec_sib[...]

        bufs = [(pw_sum, pw_chip, pw_out, pw_rows // 2), (vec_sum, vec_chip, vec_out, VEC_HALF)]
        cps = []
        for t, (sm, chip_buf, _, hr) in enumerate(bufs):
            rows = pl.ds(pl.multiple_of(c * hr, 8), hr)
            for j, (cx, cy) in enumerate(chips):
                cp = rdma(2 + 3 * t + j, sm.at[rows], chip_buf.at[j], (cx, cy, c))
                cp.start()
                cps.append(cp)
        for cp in cps:
            cp.wait()
        last = []
        for t, (sm, chip_buf, out, hr) in enumerate(bufs):
            rows = pl.ds(pl.multiple_of(c * hr, 8), hr)
            other = pl.ds(pl.multiple_of((1 - c) * hr, 8), hr)
            out[rows, :] = (sm[rows, :] + chip_buf[0]) + (chip_buf[1] + chip_buf[2])
            cp = rdma(8 + t, out.at[rows], out.at[rows], sib)
            cp.start()
            last.append((cp, rdma(8 + t, out.at[other], out.at[other], sib)))
        for cp, recv in last:
            recv.wait_recv()
            cp.wait_send()

    vm = pl.BlockSpec(memory_space=pltpu.VMEM)
    vec_shape = (VEC_ROWS, D_MODEL)
    return pl.pallas_call(
        body, name="small_allreduce",
        out_shape=(jax.ShapeDtypeStruct((pw_rows, POOL_GD), F32), jax.ShapeDtypeStruct(vec_shape, F32)),
        in_specs=[vm] * 7, out_specs=(vm, vm),
        scratch_shapes=[pltpu.VMEM(vec_shape, F32), pltpu.VMEM((pw_rows, POOL_GD), F32), pltpu.VMEM(vec_shape, F32),
                        pltpu.VMEM((pw_rows, POOL_GD), F32), pltpu.VMEM(vec_shape, F32),
                        pltpu.VMEM((3, pw_rows // 2, POOL_GD), F32), pltpu.VMEM((3, VEC_HALF, D_MODEL), F32),
                        pltpu.SemaphoreType.DMA((10,)), pltpu.SemaphoreType.DMA((10,))],
    )(d_pool_w.reshape(pw_rows, POOL_GD), d_ln_g, d_ln_b, d_ps, d_gq, d_gkv, loss_part)


ADD_STEPS = 2


def _add_sibling_half(gs, rs, c_arr):
    n = len(gs)

    def body(c_ref, *refs):
        for t in range(n):
            refs[2 * n + t][...] = refs[t][...] + refs[n + t][...]

    in_specs, out_specs = [], []
    for g in gs:
        br = g.shape[1] // 2 // ADD_STEPS
        in_specs.append(pl.BlockSpec((None, br, g.shape[2]), lambda k, i, c: (k, c[0] * ADD_STEPS + i, 0)))
    for r in rs:
        br = r.shape[1] // ADD_STEPS
        spec = pl.BlockSpec((None, br, r.shape[2]), lambda k, i, c: (k, i, 0))
        in_specs.append(spec)
        out_specs.append(spec)
    return pl.pallas_call(
        body, name="add_sibling_half",
        out_shape=tuple(jax.ShapeDtypeStruct(r.shape, F32) for r in rs),
        grid_spec=pltpu.PrefetchScalarGridSpec(num_scalar_prefetch=1, grid=(N_CHIPS, ADD_STEPS),
                                               in_specs=in_specs, out_specs=tuple(out_specs)),
        compiler_params=_cparams(2),
    )(c_arr, *gs, *rs)


def _add_chip_parts(ss, rs, me_arr):
    n = len(ss)

    def body(me_ref, *refs):
        for t in range(n):
            s_ref = refs[t]
            r0, r1, r2 = refs[n + 3 * t: n + 3 * t + 3]
            refs[4 * n + t][...] = (s_ref[...] + r0[...]) + (r1[...] + r2[...])

    in_specs, out_specs, operands = [], [], []
    for s in ss:
        br = s.shape[1] // ADD_STEPS
        in_specs.append(pl.BlockSpec((None, br, s.shape[2]), lambda i, me: (me[0], i, 0)))
    for r in rs:
        br = r.shape[1] // ADD_STEPS
        for j in range(3):
            in_specs.append(pl.BlockSpec((None, br, r.shape[2]), functools.partial(lambda i, me, j: (j, i, 0), j=j)))
            operands.append(r)
        out_specs.append(pl.BlockSpec((br, r.shape[2]), lambda i, me: (i, 0)))
    return pl.pallas_call(
        body, name="add_chip_parts",
        out_shape=tuple(jax.ShapeDtypeStruct(s.shape[1:], F32) for s in ss),
        grid_spec=pltpu.PrefetchScalarGridSpec(num_scalar_prefetch=1, grid=(ADD_STEPS,),
                                               in_specs=in_specs, out_specs=tuple(out_specs)),
        compiler_params=_cparams(1),
    )(me_arr, *ss, *operands)


def _adamw_math(g, w, m, v):
    nm = ADAM_B1 * m + (1.0 - ADAM_B1) * g
    nv = ADAM_B2 * v + (1.0 - ADAM_B2) * (g * g)
    m_hat = nm / (1.0 - ADAM_B1 ** ADAM_STEP)
    v_hat = nv / (1.0 - ADAM_B2 ** ADAM_STEP)
    return -ADAM_LR * (m_hat / (jnp.sqrt(v_hat) + ADAM_EPS) + ADAM_WD * w), nm, nv


ADAM_STEPS = 8


def _adamw_big(gs, ws, ms, vs):
    n = len(gs)

    def body(*refs):
        for t in range(n):
            d, nm, nv = _adamw_math(refs[t][...], refs[n + t][...], refs[2 * n + t][...], refs[3 * n + t][...])
            refs[4 * n + 3 * t][...] = d
            refs[4 * n + 3 * t + 1][...] = nm
            refs[4 * n + 3 * t + 2][...] = nv

    specs = [pl.BlockSpec((g.shape[0] // ADAM_STEPS, g.shape[1]), lambda i: (i, 0)) for g in gs]
    out_specs, out_shape = [], []
    for t in range(n):
        out_specs += [specs[t]] * 3
        out_shape += [jax.ShapeDtypeStruct(gs[t].shape, F32)] * 3
    outs = pl.pallas_call(
        body, name="adamw_big", grid=(ADAM_STEPS,),
        in_specs=specs * 4, out_specs=tuple(out_specs), out_shape=tuple(out_shape),
        compiler_params=_cparams(1),
    )(*gs, *ws, *ms, *vs)
    return [outs[3 * t: 3 * t + 3] for t in range(n)]


def _adamw_small(pw_sum, vec_sum, ws, ms, vs):
    rows = (None, 0, 1, 2, 3, 8)
    n = len(ws)

    def body(pw_ref, vec_ref, *refs):
        outs = refs[3 * n:]
        for t in range(n):
            w_ref, m_ref, v_ref = refs[t], refs[n + t], refs[2 * n + t]
            if rows[t] is None:
                g = pw_ref[...]
            else:
                g = vec_ref[rows[t]:rows[t] + 1, 0:w_ref.shape[1]]
            d, nm, nv = _adamw_math(g, w_ref[...], m_ref[...], v_ref[...])
            outs[4 * t][...] = g
            outs[4 * t + 1][...] = d
            outs[4 * t + 2][...] = nm
            outs[4 * t + 3][...] = nv
        outs[4 * n][...] = vec_ref[9:10, 0:128]

    vm = pl.BlockSpec(memory_space=pltpu.VMEM)
    out_shape = []
    for w in ws:
        out_shape += [jax.ShapeDtypeStruct(w.shape, F32)] * 4
    out_shape.append(jax.ShapeDtypeStruct((1, 128), F32))
    outs = pl.pallas_call(
        body, name="adamw_small", in_specs=[vm] * (2 + 3 * n), out_specs=(vm,) * (4 * n + 1),
        out_shape=tuple(out_shape),
    )(pw_sum, vec_sum, *ws, *ms, *vs)
    return [outs[4 * t: 4 * t + 4] for t in range(n)], outs[4 * n]


def _fwd_proj(x, w_in_e, w_uq_e, w_ukv, gq, gkv, rc, rsa, rsb, tm):
    T = x.shape[0]

    def body(x_ref, win_ref, wuq_ref, wukv_ref, gq_ref, gkv_ref, c_ref, sa_ref, sb_ref,
             xq_ref, xkv_ref, ga_ref, u_ref, gb_ref, q_ref, k_ref, v_ref):
        h = _dot(x_ref[...].astype(BF16), win_ref[...])
        xq = h[:, 0:512]
        xkv = h[:, 512:768]
        xq_ref[...] = xq
        xkv_ref[...] = xkv
        ga_ref[...] = h[:, 896:1408]
        u_ref[...] = h[:, 1408:1920]
        gb_ref[...] = h[:, 1920:2432]
        c, sa, sb = c_ref[...], sa_ref[...], sb_ref[...]
        rq = lax.rsqrt(jnp.mean(xq * xq, axis=-1, keepdims=True) + RMS_EPS)
        q = _dot(((xq * rq) * gq_ref[...]).astype(BF16), wuq_ref[...]) * QSCALE
        rkv = lax.rsqrt(jnp.mean(xkv * xkv, axis=-1, keepdims=True) + RMS_EPS)
        kv = _dot(((xkv * rkv) * gkv_ref[...]).astype(BF16), wukv_ref[...])
        kr = _rope(h[:, 768:896], c, sa, sb, 1.0).astype(BF16)
        for hh in range(HEADS):
            b0 = hh * HEAD_PAD
            q_ref[:, b0:b0 + 128] = q[:, b0:b0 + 128].astype(BF16)
            q_ref[:, b0 + 128:b0 + 256] = _rope(q[:, b0 + 128:b0 + 256], c, sa, sb, 1.0).astype(BF16)
            k_ref[:, b0:b0 + 128] = kv[:, b0:b0 + 128].astype(BF16)
            k_ref[:, b0 + 128:b0 + 256] = kr
            v_ref[:, hh * 128:(hh + 1) * 128] = kv[:, b0 + 128:b0 + 256].astype(BF16)

    row = lambda w: pl.BlockSpec((tm, w), lambda i: (i, 0))
    f = lambda w, dt: jax.ShapeDtypeStruct((T, w), dt)
    return pl.pallas_call(
        body, name="fwd_proj", grid=(T // tm,),
        in_specs=[row(D_MODEL), _full(w_in_e.shape), _full(w_uq_e.shape), _full(w_ukv.shape),
                  _full(gq.shape), _full(gkv.shape), row(128), row(128), row(128)],
        out_specs=(row(512), row(256), row(512), row(512), row(512), row(1024), row(1024), row(512)),
        out_shape=(f(512, F32), f(256, F32), f(512, F32), f(512, F32), f(512, F32),
                   f(1024, BF16), f(1024, BF16), f(512, BF16)),
        compiler_params=_cparams(1),
    )(x, w_in_e, w_uq_e, w_ukv, gq, gkv, rc, rsa, rsb)


def _attn_fwd(q, k, v, pos_col, pos_row, bounds, nb, S, tq, tk):
    T = q.shape[0]
    nq, nk = S // tq, S // tk
    reps = tk // 128

    def body(qmin_ref, qmax_ref, kmin_ref, kmax_ref, q_ref, k_ref, v_ref, pc_ref, pr_ref, o_ref, lse_ref,
             m_sc, l_sc, acc_sc):
        b, i = pl.program_id(0), pl.program_id(2)
        m_sc[...] = jnp.full(m_sc.shape, NEG, F32)
        l_sc[...] = jnp.zeros_like(l_sc)
        acc_sc[...] = jnp.zeros_like(acc_sc)
        q_lo = qmin_ref[b * nq + i]
        q_hi = qmax_ref[b * nq + i]

        def tile(j, masked):
            off = pl.multiple_of(j * tk, tk)
            s = _dot_nt(q_ref[...], k_ref[pl.ds(off, tk), :])
            if masked:
                s = jnp.where(pc_ref[...] >= pr_ref[pl.ds(j, 1), :], s, NEG)
            m_prev = m_sc[...]
            m_new = jnp.maximum(m_prev, jnp.max(s, axis=1, keepdims=True))
            p = jnp.exp2(s - jnp.concatenate([m_new] * reps, axis=1))
            a = jnp.exp2(m_prev - m_new)
            l_sc[...] = a * l_sc[...] + jnp.sum(p, axis=1, keepdims=True)
            acc_sc[...] = a * acc_sc[...] + _dot(p.astype(BF16), v_ref[pl.ds(off, tk), :])
            m_sc[...] = m_new

        def step(j, carry):
            visible = kmin_ref[b * nk + j] <= q_hi
            clear = q_lo >= kmax_ref[b * nk + j]

            @pl.when(jnp.logical_and(visible, clear))
            def _():
                tile(j, False)

            @pl.when(jnp.logical_and(visible, jnp.logical_not(clear)))
            def _():
                tile(j, True)
            return carry

        lax.fori_loop(0, nk, step, 0)
        l = l_sc[...]
        o_ref[...] = acc_sc[...] / l
        lse_ref[...] = m_sc[...] + jnp.log2(l)

    return pl.pallas_call(
        body, name="attn_fwd",
        grid_spec=pltpu.PrefetchScalarGridSpec(
            num_scalar_prefetch=4, grid=(nb, HEADS, nq),
            in_specs=[pl.BlockSpec((tq, HEAD_PAD), lambda b, h, i, *_: (b * nq + i, h)),
                      pl.BlockSpec((S, HEAD_PAD), lambda b, h, i, *_: (b, h)),
                      pl.BlockSpec((S, 128), lambda b, h, i, *_: (b, h)),
                      pl.BlockSpec((tq, 1), lambda b, h, i, *_: (b * nq + i, 0)),
                      pl.BlockSpec((None, nk, tk), lambda b, h, i, *_: (b, 0, 0))],
            out_specs=(pl.BlockSpec((tq, 128), lambda b, h, i, *_: (b * nq + i, h)),
                       pl.BlockSpec((tq, 128), lambda b, h, i, *_: (b * nq + i, h))),
            scratch_shapes=[pltpu.VMEM((tq, 128), F32), pltpu.VMEM((tq, 128), F32), pltpu.VMEM((tq, 128), F32)]),
        out_shape=(jax.ShapeDtypeStruct((T, MLA_W), F32), jax.ShapeDtypeStruct((T, MLA_W), F32)),
        compiler_params=_cparams(3),
    )(*bounds, q, k, v, pos_col, pos_row.reshape(nb, nk, tk))


def _mid(x, tgt, o, ga, u, gb, w_out, pool_w, pool_scale, ln_g, ln_b, S, tm):
    T = x.shape[0]
    tps = S // tm
    hb = tm // HALO

    def body(x_ref, tgt_ref, o_ref, ga_ref, u_ref, uh_ref, gb_ref, wout_ref, pw_ref,
             ps_ref, lng_ref, lnb_ref,
             dz_ref, do_ref, delta_ref, dga_ref, dgb_ref, dpc_ref,
             dwout_ref, dpw_ref, dps_ref, dlng_ref, dlnb_ref, loss_ref):
        i = pl.program_id(0)

        @pl.when(i == 0)
        def _():
            dwout_ref[...] = jnp.zeros_like(dwout_ref)
            dpw_ref[...] = jnp.zeros_like(dpw_ref)
            dps_ref[...] = jnp.zeros_like(dps_ref)
            dlng_ref[...] = jnp.zeros_like(dlng_ref)
            dlnb_ref[...] = jnp.zeros_like(dlnb_ref)
            loss_ref[...] = jnp.zeros_like(loss_ref)

        seq_tile = i % tps
        tpos = seq_tile * tm + lax.broadcasted_iota(jnp.int32, (tm, 1), 0)
        ga_v = ga_ref[...]
        sig_a = jax.nn.sigmoid(ga_v)
        silu_a = ga_v * sig_a
        o_v = o_ref[...]
        ya = o_v * silu_a

        u_v = u_ref[...]
        halo = jnp.where(seq_tile == 0, 0.0, uh_ref[...])
        pooled, cnts, mixed = [], [], []
        for g in range(POOL_G):
            lanes = slice(g * POOL_GD, (g + 1) * POOL_GD)
            w = jnp.concatenate([halo[:, lanes], u_v[:, lanes]], axis=0)
            for st in range(g + 1):
                w = w + pltpu.roll(w, 1 << st, 0)
            cnt = jnp.minimum(tpos + 1, 2 << g).astype(F32)
            pg = (w[HALO:, :] / cnt - u_v[:, lanes]).astype(BF16)
            pooled.append(pg)
            cnts.append(cnt)
            mixed.append(_dot(pg, pw_ref[g]))
        mixed = jnp.concatenate(mixed, axis=1)
        ps = ps_ref[...]
        ybp = mixed * ps
        gb_v = gb_ref[...]
        sig_b = jax.nn.sigmoid(gb_v)
        silu_b = gb_v * sig_b
        yb = ybp * silu_b

        cat = jnp.concatenate([ya, yb], axis=1).astype(BF16)
        z = ALPHA * x_ref[...] + _dot(cat, wout_ref[...])
        mu = jnp.mean(z, axis=-1, keepdims=True)
        zc = z - mu
        rstd = lax.rsqrt(jnp.mean(zc * zc, axis=-1, keepdims=True) + LN_EPS)
        zhat = zc * rstd
        lng = lng_ref[...]
        err = (zhat * lng + lnb_ref[...]) - tgt_ref[...]
        row_loss = jnp.sum(err * err, axis=1, keepdims=True)
        loss_ref[...] += jnp.broadcast_to(jnp.sum(row_loss, axis=0, keepdims=True) * (0.5 / D_MODEL), (1, 128))
        dy = err * (1.0 / D_MODEL)
        dlng_ref[...] += jnp.sum(dy * zhat, axis=0, keepdims=True)
        dlnb_ref[...] += jnp.sum(dy, axis=0, keepdims=True)
        dzh = dy * lng
        dz = rstd * (dzh - jnp.mean(dzh, axis=-1, keepdims=True)
                     - zhat * jnp.mean(dzh * zhat, axis=-1, keepdims=True))
        dz_ref[...] = dz
        dzb = dz.astype(BF16)
        dwout_ref[...] += _dot_tn(cat, dzb)
        dcat = _dot_nt(dzb, wout_ref[...])
        dya = dcat[:, :MLA_W]
        dyb = dcat[:, MLA_W:]

        do = dya * silu_a
        do_ref[...] = do.astype(BF16)
        prod = do * o_v
        for hh in range(HEADS):
            lanes = slice(hh * 128, (hh + 1) * 128)
            delta_ref[:, lanes] = jnp.broadcast_to(jnp.sum(prod[:, lanes], axis=1, keepdims=True), (tm, 128))
        dga_ref[...] = (dya * o_v * (sig_a * (1.0 + ga_v * (1.0 - sig_a)))).astype(BF16)
        dgb_ref[...] = (dyb * ybp * (sig_b * (1.0 + gb_v * (1.0 - sig_b)))).astype(BF16)
        dybp = dyb * silu_b
        dps_ref[...] += jnp.sum(dybp * mixed, axis=0, keepdims=True)
        dmixed = (dybp * ps).astype(BF16)
        for g in range(POOL_G):
            lanes = slice(g * POOL_GD, (g + 1) * POOL_GD)
            dpw_ref[g] += _dot_tn(pooled[g], dmixed[:, lanes])
            dpc_ref[:, lanes] = _dot_nt(dmixed[:, lanes], pw_ref[g]) / cnts[g]

    row = lambda w: pl.BlockSpec((tm, w), lambda i: (i, 0))
    f = lambda w, dt: jax.ShapeDtypeStruct((T, w), dt)
    halo_spec = pl.BlockSpec((HALO, POOL_W), lambda i: (jnp.maximum(i * hb - 1, 0), 0))
    return pl.pallas_call(
        body, name="mid", grid=(T // tm,),
        in_specs=[row(D_MODEL), row(D_MODEL), row(MLA_W), row(MLA_W), row(POOL_W), halo_spec, row(POOL_W),
                  _full(w_out.shape), _full(pool_w.shape),
                  _full(pool_scale.shape), _full(ln_g.shape), _full(ln_b.shape)],
        out_specs=(row(D_MODEL), row(MLA_W), row(MLA_W), row(MLA_W), row(POOL_W), row(POOL_W),
                   _full((D_MODEL, D_MODEL)), _full(pool_w.shape), _full((1, POOL_W)),
                   _full((1, D_MODEL)), _full((1, D_MODEL)), _full((1, 128))),
        out_shape=(f(D_MODEL, F32), f(MLA_W, BF16), f(MLA_W, F32), f(MLA_W, BF16), f(POOL_W, BF16), f(POOL_W, F32),
                   jax.ShapeDtypeStruct((D_MODEL, D_MODEL), F32), jax.ShapeDtypeStruct(pool_w.shape, F32),
                   jax.ShapeDtypeStruct((1, POOL_W), F32), jax.ShapeDtypeStruct((1, D_MODEL), F32),
                   jax.ShapeDtypeStruct((1, D_MODEL), F32), jax.ShapeDtypeStruct((1, 128), F32)),
        compiler_params=_cparams(1),
    )(x, tgt, o, ga, u, u, gb, w_out, pool_w, pool_scale, ln_g, ln_b)


def _attn_bwd(q, k, v, do, lse, delta, pos_col, pos_row, bounds, nb, S, tq, tk):
    T = q.shape[0]
    nq, nk = S // tq, S // tk
    reps = tk // 128

    def body(qmin_ref, qmax_ref, kmin_ref, kmax_ref, q_ref, k_ref, v_ref, do_ref, lse_ref, dl_ref, pc_ref, pr_ref,
             dq_ref, dk_ref, dv_ref):
        b, j = pl.program_id(0), pl.program_id(2)

        @pl.when(j == 0)
        def _():
            dq_ref[...] = jnp.zeros_like(dq_ref)

        dk_ref[...] = jnp.zeros_like(dk_ref)
        dv_ref[...] = jnp.zeros_like(dv_ref)
        k_lo = kmin_ref[b * nk + j]
        k_hi = kmax_ref[b * nk + j]

        def tile(i, masked):
            kb = k_ref[...]
            rows = pl.ds(pl.multiple_of(i * tq, tq), tq)
            qb = q_ref[rows, :]
            dob = do_ref[rows, :]
            s = _dot_nt(qb, kb)
            if masked:
                s = jnp.where(pc_ref[rows, :] >= pr_ref[...], s, NEG)
            p = jnp.exp2(s - jnp.concatenate([lse_ref[rows, :]] * reps, axis=1))
            dv_ref[...] += _dot_tn(p.astype(BF16), dob)
            dp = _dot_nt(dob, v_ref[...])
            ds = (p * (dp - jnp.concatenate([dl_ref[rows, :]] * reps, axis=1))).astype(BF16)
            dq_ref[rows, :] += _dot(ds, kb)
            dk_ref[...] += _dot_tn(ds, qb)

        def step(i, carry):
            visible = k_lo <= qmax_ref[b * nq + i]
            clear = qmin_ref[b * nq + i] >= k_hi

            @pl.when(jnp.logical_and(visible, clear))
            def _():
                tile(i, False)

            @pl.when(jnp.logical_and(visible, jnp.logical_not(clear)))
            def _():
                tile(i, True)
            return carry

        lax.fori_loop(0, nq, step, 0)

    seq = lambda w: pl.BlockSpec((S, w), lambda b, h, j, *_: (b, h))
    return pl.pallas_call(
        body, name="attn_bwd",
        grid_spec=pltpu.PrefetchScalarGridSpec(
            num_scalar_prefetch=4, grid=(nb, HEADS, nk),
            in_specs=[seq(HEAD_PAD),
                      pl.BlockSpec((tk, HEAD_PAD), lambda b, h, j, *_: (b * nk + j, h)),
                      pl.BlockSpec((tk, 128), lambda b, h, j, *_: (b * nk + j, h)),
                      seq(128), seq(128), seq(128),
                      pl.BlockSpec((S, 1), lambda b, h, j, *_: (b, 0)),
                      pl.BlockSpec((None, 1, tk), lambda b, h, j, *_: (b, 0, j))],
            out_specs=(seq(HEAD_PAD),
                       pl.BlockSpec((tk, HEAD_PAD), lambda b, h, j, *_: (b * nk + j, h)),
                       pl.BlockSpec((tk, 128), lambda b, h, j, *_: (b * nk + j, h)))),
        out_shape=(jax.ShapeDtypeStruct((T, HEADS * HEAD_PAD), F32),
                   jax.ShapeDtypeStruct((T, HEADS * HEAD_PAD), F32),
                   jax.ShapeDtypeStruct((T, MLA_W), F32)),
        compiler_params=_cparams(3),
    )(*bounds, q, k, v, do, lse, delta, pos_col, pos_row)


def _bwd_proj(dq, dk, dv, xq, xkv, x, dz, dga, dgb, dpc, rc, rsa, rsb, w_uq_e, w_ukv, w_in_e, gq, gkv, S, tm):
    T = x.shape[0]
    tps = S // tm
    hb = tm // HALO
    n_steps = T // tm

    def body(dq_ref, dk_ref, dv_ref, xq_ref, xkv_ref, x_ref, dz_ref, dga_ref, dgb_ref, dpc_ref, dph_ref,
             c_ref, sa_ref, sb_ref, wuq_ref, wukv_ref, win_ref, gq_ref, gkv_ref,
             dx_ref, dwin_hbm, dwuq_hbm, dwukv_hbm, dgq_ref, dgkv_ref,
             acc_win, acc_wuq, acc_wukv):
        i = pl.program_id(0)

        @pl.when(i == 0)
        def _():
            acc_win[...] = jnp.zeros_like(acc_win)
            acc_wuq[...] = jnp.zeros_like(acc_wuq)
            acc_wukv[...] = jnp.zeros_like(acc_wukv)
            dgq_ref[...] = jnp.zeros_like(dgq_ref)
            dgkv_ref[...] = jnp.zeros_like(dgkv_ref)

        c, sa, sb = c_ref[...], sa_ref[...], sb_ref[...]
        dq_v = dq_ref[...] * SCALE
        dk_v = dk_ref[...] * LN2
        dv_v = dv_ref[...]
        dq_parts, dkv_parts = [], []
        dkr = jnp.zeros((tm, 128), F32)
        for hh in range(HEADS):
            b0 = hh * HEAD_PAD
            dq_parts.append(dq_v[:, b0:b0 + 128].astype(BF16))
            dq_parts.append(_rope(dq_v[:, b0 + 128:b0 + 256], c, sa, sb, -1.0).astype(BF16))
            dkv_parts.append(dk_v[:, b0:b0 + 128].astype(BF16))
            dkv_parts.append(dv_v[:, hh * 128:(hh + 1) * 128].astype(BF16))
            dkr = dkr + dk_v[:, b0 + 128:b0 + 256]
        dqp = jnp.concatenate(dq_parts, axis=1)
        dkvp = jnp.concatenate(dkv_parts, axis=1)
        dkrr = _rope(dkr, c, sa, sb, -1.0)

        def rms_bwd(xv, g, dyn, dg_ref):
            r = lax.rsqrt(jnp.mean(xv * xv, axis=-1, keepdims=True) + RMS_EPS)
            xhat = xv * r
            dg_ref[...] += jnp.sum(dyn * xhat, axis=0, keepdims=True)
            dxh = dyn * g
            return r * (dxh - xhat * jnp.mean(dxh * xhat, axis=-1, keepdims=True))

        xq_v = xq_ref[...]
        gq_v = gq_ref[...]
        rq = lax.rsqrt(jnp.mean(xq_v * xq_v, axis=-1, keepdims=True) + RMS_EPS)
        acc_wuq[...] += _dot_tn(((xq_v * rq) * gq_v).astype(BF16), dqp)
        dxq = rms_bwd(xq_v, gq_v, _dot_nt(dqp, wuq_ref[...]), dgq_ref)

        xkv_v = xkv_ref[...]
        gkv_v = gkv_ref[...]
        rkv = lax.rsqrt(jnp.mean(xkv_v * xkv_v, axis=-1, keepdims=True) + RMS_EPS)
        acc_wukv[...] += _dot_tn(((xkv_v * rkv) * gkv_v).astype(BF16), dkvp)
        dxkv = rms_bwd(xkv_v, gkv_v, _dot_nt(dkvp, wukv_ref[...]), dgkv_ref)

        seq_tile = i % tps
        tpos = seq_tile * tm + lax.broadcasted_iota(jnp.int32, (tm, 1), 0)
        dpc_v = dpc_ref[...]
        halo = jnp.where(seq_tile == tps - 1, 0.0, dph_ref[...])
        n = tm + HALO
        du = []
        for g in range(POOL_G):
            lanes = slice(g * POOL_GD, (g + 1) * POOL_GD)
            f = jnp.concatenate([dpc_v[:, lanes], halo[:, lanes]], axis=0)
            for st in range(g + 1):
                f = f + pltpu.roll(f, n - (1 << st), 0)
            cnt = jnp.minimum(tpos + 1, 2 << g).astype(F32)
            du.append((f[:tm, :] - dpc_v[:, lanes] * cnt).astype(BF16))

        dh = jnp.concatenate([dxq.astype(BF16), dxkv.astype(BF16), dkrr.astype(BF16), dga_ref[...]]
                             + du + [dgb_ref[...]], axis=1)
        dx_ref[...] = ALPHA * dz_ref[...] + _dot_nt(dh, win_ref[...])
        acc_win[...] += _dot_tn(x_ref[...].astype(BF16), dh)

        @pl.when(i == n_steps - 1)
        def _():
            pltpu.sync_copy(acc_win, dwin_hbm)
            pltpu.sync_copy(acc_wuq, dwuq_hbm)
            pltpu.sync_copy(acc_wukv, dwukv_hbm)

    row = lambda w: pl.BlockSpec((tm, w), lambda i: (i, 0))
    halo_spec = pl.BlockSpec((HALO, POOL_W), lambda i: (jnp.minimum((i + 1) * hb, T // HALO - 1), 0))
    return pl.pallas_call(
        body, name="bwd_proj", grid=(n_steps,),
        in_specs=[row(1024), row(1024), row(512), row(512), row(256), row(D_MODEL), row(D_MODEL),
                  row(512), row(512), row(512), halo_spec, row(128), row(128), row(128),
                  _full(w_uq_e.shape), _full(w_ukv.shape), _full(w_in_e.shape), _full(gq.shape), _full(gkv.shape)],
        out_specs=(row(D_MODEL), ANY, ANY, ANY, _full((1, Q_LORA)), _full((1, KV_LORA))),
        out_shape=(jax.ShapeDtypeStruct((T, D_MODEL), F32),
                   jax.ShapeDtypeStruct((D_MODEL, IN_EXT), F32),
                   jax.ShapeDtypeStruct((Q_LORA, HEADS * HEAD_PAD), F32),
                   jax.ShapeDtypeStruct((KV_LORA, 1024), F32),
                   jax.ShapeDtypeStruct((1, Q_LORA), F32), jax.ShapeDtypeStruct((1, KV_LORA), F32)),
        scratch_shapes=[pltpu.VMEM((D_MODEL, IN_EXT), F32), pltpu.VMEM((Q_LORA, HEADS * HEAD_PAD), F32),
                        pltpu.VMEM((KV_LORA, 1024), F32)],
        compiler_params=_cparams(1),
    )(dq, dk, dv, xq, xkv, x, dz, dga, dgb, dpc, dpc, rc, rsa, rsb, w_uq_e, w_ukv, w_in_e, gq, gkv)


def kernel(x, positions, w_in, q_norm_g, w_uq, kv_norm_g, w_ukv, pool_w, pool_scale, w_out, ln_g, ln_b, loss_target, m_w_in, m_q_norm_g, m_w_uq, m_kv_norm_g, m_w_ukv, m_pool_w, m_pool_scale, m_w_out, m_ln_g, m_ln_b, v_w_in, v_q_norm_g, v_w_uq, v_kv_norm_g, v_w_ukv, v_pool_w, v_pool_scale, v_w_out, v_ln_g, v_ln_b):
    nb, S, _ = x.shape
    T = nb * S
    tm = min(256, S)
    tq = min(512, S)
    tk = min(512, S)
    assert S % tm == 0 and tm % HALO == 0 and S % tq == 0 and S % tk == 0

    cx, cy, cc = lax.axis_index("x"), lax.axis_index("y"), lax.axis_index("c")
    c_arr = jnp.reshape(cc, (1,)).astype(jnp.int32)
    me_arr = jnp.reshape(2 * cx + cy, (1,)).astype(jnp.int32)

    w_in_g, w_uq_g, w_ukv_g, w_out_g = _weight_gather(
        [w_in.astype(BF16), w_uq.astype(BF16), w_ukv.astype(BF16), w_out.astype(BF16)])
    w_in_f = w_in_g.transpose(1, 0, 2).reshape(D_MODEL, IN_W)
    w_in_e = jnp.concatenate([w_in_f[:, :832], jnp.zeros((D_MODEL, 64), BF16), w_in_f[:, 832:]], axis=1)
    w_uq_e = jnp.pad(w_uq_g.transpose(1, 0, 2), ((0, 0), (0, 0), (0, 64))).reshape(Q_LORA, HEADS * HEAD_PAD)
    w_ukv_f = w_ukv_g.transpose(1, 0, 2).reshape(KV_LORA, 1024)
    w_out_f = w_out_g.reshape(D_MODEL, D_MODEL)
    pool_w_b = pool_w.astype(BF16)
    gq2 = q_norm_g.reshape(1, Q_LORA)
    gkv2 = kv_norm_g.reshape(1, KV_LORA)
    ps2 = pool_scale.reshape(1, POOL_W)

    half = ROPE // 2
    inv_freq = ROPE_THETA ** (-jnp.arange(half, dtype=F32) / half)
    ang = positions.astype(F32).reshape(T, 1) * inv_freq
    cos, sin = jnp.cos(ang), jnp.sin(ang)
    z32, z64 = jnp.zeros((T, 32), F32), jnp.zeros((T, 64), F32)
    rc = jnp.concatenate([cos, cos, z64], axis=1)
    rsa = jnp.concatenate([sin, z32, z64], axis=1)
    rsb = jnp.concatenate([z32, sin, z64], axis=1)
    pos_col = positions.reshape(T, 1)
    pos_row = positions.reshape(nb, 1, S)
    pos_q = positions.reshape(nb, S // tq, tq)
    pos_k = positions.reshape(nb, S // tk, tk)
    bounds = (jnp.min(pos_q, axis=2).reshape(-1), jnp.max(pos_q, axis=2).reshape(-1),
              jnp.min(pos_k, axis=2).reshape(-1), jnp.max(pos_k, axis=2).reshape(-1))

    xf = x.reshape(T, D_MODEL)
    tgt = loss_target.reshape(T, D_MODEL)

    xq, xkv, ga, u, gb, q, k, v = _fwd_proj(xf, w_in_e, w_uq_e, w_ukv_f, gq2, gkv2, rc, rsa, rsb, tm)
    o, lse = _attn_fwd(q, k, v, pos_col, pos_row, bounds, nb, S, tq, tk)

    (dz, do, delta, dga, dgb, dpc, d_w_out, d_pool_w, d_pool_scale, d_ln_g, d_ln_b, loss_part) = _mid(
        xf, tgt, o, ga, u, gb, w_out_f, pool_w_b, ps2, ln_g, ln_b, S, tm)

    dq, dk, dv = _attn_bwd(q, k, v, do, lse, delta, pos_col, pos_row, bounds, nb, S, tq, tk)
    dx, d_w_in_e, d_w_uq_e, d_w_ukv, d_gq, d_gkv = _bwd_proj(
        dq, dk, dv, xq, xkv, xf, dz, dga, dgb, dpc, rc, rsa, rsb, w_uq_e, w_ukv_f, w_in_e, gq2, gkv2, S, tm)
    grad_x = dx.reshape(nb, S, D_MODEL)

    g_in = jnp.concatenate([d_w_in_e[:, :832], d_w_in_e[:, 896:]], axis=1).reshape(D_MODEL, N_CHIPS, 592).transpose(1, 0, 2)
    g_uq = d_w_uq_e.reshape(Q_LORA, HEADS, HEAD_PAD)[:, :, :NOPE + ROPE].transpose(1, 0, 2)
    g_ukv = d_w_ukv.reshape(KV_LORA, N_CHIPS, 256).transpose(1, 0, 2)
    g_out = d_w_out.reshape(N_CHIPS, 256, D_MODEL)
    gs = [g_in, g_uq, g_ukv, g_out]
    from_sibling = _grad_to_sibling(gs)
    chip_sums = _add_sibling_half(gs, from_sibling, c_arr)
    from_chips = _grad_to_chips(chip_sums)
    my_halves = _add_chip_parts(chip_sums, from_chips, me_arr)
    g_big = _halves_exchange(my_halves)

    pw_sum, vec_sum = _small_allreduce(d_pool_w, d_ln_g, d_ln_b, d_pool_scale, d_gq, d_gkv, loss_part)

    big = _adamw_big(g_big, [w_in, w_uq, w_ukv, w_out], [m_w_in, m_w_uq, m_w_ukv, m_w_out],
                     [v_w_in, v_w_uq, v_w_ukv, v_w_out])
    two_d = lambda a: a.reshape(-1, a.shape[-1])
    small_names = lambda pw, lg, lb, ps, gq, gkv: [two_d(pw), lg, lb, ps.reshape(1, -1), gq.reshape(1, -1), gkv.reshape(1, -1)]
    small, loss_row = _adamw_small(
        pw_sum, vec_sum,
        small_names(pool_w, ln_g, ln_b, pool_scale, q_norm_g, kv_norm_g),
        small_names(m_pool_w, m_ln_g, m_ln_b, m_pool_scale, m_q_norm_g, m_kv_norm_g),
        small_names(v_pool_w, v_ln_g, v_ln_b, v_pool_scale, v_q_norm_g, v_kv_norm_g))
    loss = loss_row[0, 0]

    def leaves(kind):
        b = [g_big[t] if kind == 0 else big[t][kind - 1] for t in range(N_BIG)]
        s = [small[t][kind] for t in range(6)]
        return (b[0], s[4].reshape(Q_LORA), b[1], s[5].reshape(KV_LORA), b[2],
                s[0].reshape(POOL_G, POOL_GD, POOL_GD), s[3].reshape(POOL_W), b[3], s[1], s[2])

    return (loss, grad_x) + leaves(0) + leaves(1) + leaves(2) + leaves(3)
```

```python
import functools

import jax
import jax.numpy as jnp
from jax import lax
from jax.experimental import pallas as pl
from jax.experimental.pallas import tpu as pltpu

F32 = jnp.float32
BF16 = jnp.bfloat16
MESH = pl.DeviceIdType.MESH

HEADS = 4
NOPE = 128
ROPE = 64
HEAD_PAD = 256
Q_LORA = 512
KV_LORA = 256
MLA_W = 512
POOL_W = 512
POOL_G = 4
POOL_GD = 128
D_MODEL = 1024
IN_W = 2368
IN_EXT = 2432
ROPE_THETA = 10000.0
RMS_EPS = 1e-6
LN_EPS = 1e-5
ALPHA = 2.0 ** 0.25
SCALE = 192.0 ** -0.5
LOG2E = 1.4426950408889634
LN2 = 0.6931471805599453
QSCALE = SCALE * LOG2E
NEG = float(jnp.finfo(jnp.float32).min)
HALO = 16

ADAM_LR = 0.001
ADAM_B1 = 0.9
ADAM_B2 = 0.999
ADAM_EPS = 1e-08
ADAM_WD = 0.01
ADAM_STEP = 10

N_CHIPS = 4
N_BIG = 4
VEC_ROWS = 16
VEC_HALF = VEC_ROWS // 2

VMEM_LIMIT = 56 * 1024 * 1024


def _cparams(n_grid_dims=0, **kw):
    sem = ("arbitrary",) * n_grid_dims if n_grid_dims else None
    return pltpu.CompilerParams(dimension_semantics=sem, vmem_limit_bytes=VMEM_LIMIT, **kw)


def _full(shape):
    nd = len(shape)
    return pl.BlockSpec(shape, lambda *_: (0,) * nd)


def _dot(a, b):
    return jnp.dot(a, b, preferred_element_type=F32)


def _dot_nt(a, b):
    return lax.dot_general(a, b, (((1,), (1,)), ((), ())), preferred_element_type=F32)


def _dot_tn(a, b):
    return lax.dot_general(a, b, (((0,), (0,)), ((), ())), preferred_element_type=F32)


def _rope(g, c, sa, sb, sign):
    return g * c + sign * (pltpu.roll(g, 32, 1) * sb - pltpu.roll(g, 96, 1) * sa)


def _place():
    x, y, c = lax.axis_index("x"), lax.axis_index("y"), lax.axis_index("c")
    chips = [(1 - x, y), (x, 1 - y), (1 - x, 1 - y)]
    return x, y, c, chips


def _half_rows(ref, half_index, lead=None):
    axis = 0 if lead is None else 1
    hr = ref.shape[axis] // 2
    rows = pl.ds(half_index * hr, hr)
    return ref.at[rows] if lead is None else ref.at[lead, rows]


ANY = pl.BlockSpec(memory_space=pl.ANY)


def _weight_gather(slots):
    n = len(slots)

    def body(*refs):
        outs = refs[n:2 * n]
        send_sems, recv_sems = refs[2 * n:]
        x, y, c, chips = _place()
        me = 2 * x + y

        def copy(t, k, chip_idx, half, to):
            blk = _half_rows(outs[t], half, lead=chip_idx)
            return pltpu.make_async_remote_copy(
                src_ref=blk, dst_ref=blk, send_sem=send_sems.at[6 * t + k], recv_sem=recv_sems.at[6 * t + k],
                device_id=to, device_id_type=MESH)

        first = [copy(t, j, me, c, (cx, cy, c)) for t in range(n) for j, (cx, cy) in enumerate(chips)]
        for cp in first:
            cp.start()
        passed = []
        for j, (cx, cy) in enumerate(chips):
            for t in range(n):
                copy(t, j, 2 * cx + cy, c, (x, y, c)).wait_recv()
                fwd = copy(t, 3 + j, 2 * cx + cy, c, (x, y, 1 - c))
                fwd.start()
                passed.append(fwd)
        for j, (cx, cy) in enumerate(chips):
            for t in range(n):
                copy(t, 3 + j, 2 * cx + cy, 1 - c, (x, y, c)).wait_recv()
        for cp in first + passed:
            cp.wait_send()

    return pl.pallas_call(
        body, name="weight_gather",
        out_shape=tuple(jax.ShapeDtypeStruct(a.shape, a.dtype) for a in slots),
        in_specs=[ANY] * n, out_specs=(ANY,) * n, input_output_aliases={t: t for t in range(n)},
        scratch_shapes=[pltpu.SemaphoreType.DMA((6 * n,)), pltpu.SemaphoreType.DMA((6 * n,))],
    )(*slots)


def _grad_to_sibling(gs):
    n = len(gs)

    def body(*refs):
        g_refs, r_refs = refs[:n], refs[n:2 * n]
        send_sems, recv_sems = refs[2 * n:]
        x, y, c, _ = _place()
        cps = []
        for t in range(n):
            hr = gs[t].shape[1] // 2
            src = g_refs[t].at[pl.ds(0, N_CHIPS), pl.ds((1 - c) * hr, hr)]
            cp = pltpu.make_async_remote_copy(
                src_ref=src, dst_ref=r_refs[t], send_sem=send_sems.at[t], recv_sem=recv_sems.at[t],
                device_id=(x, y, 1 - c), device_id_type=MESH)
            cp.start()
            cps.append(cp)
        for cp in cps:
            cp.wait()

    return pl.pallas_call(
        body, name="grad_to_sibling",
        out_shape=tuple(jax.ShapeDtypeStruct((N_CHIPS, g.shape[1] // 2, g.shape[2]), F32) for g in gs),
        in_specs=[ANY] * n, out_specs=(ANY,) * n,
        scratch_shapes=[pltpu.SemaphoreType.DMA((n,)), pltpu.SemaphoreType.DMA((n,))],
    )(*gs)


def _grad_to_chips(ss):
    n = len(ss)

    def body(*refs):
        s_refs, r_refs = refs[:n], refs[n:2 * n]
        send_sems, recv_sems = refs[2 * n:]
        x, y, c, chips = _place()
        cps = []
        for t in range(n):
            for j, (cx, cy) in enumerate(chips):
                cp = pltpu.make_async_remote_copy(
                    src_ref=s_refs[t].at[2 * cx + cy], dst_ref=r_refs[t].at[j],
                    send_sem=send_sems.at[3 * t + j], recv_sem=recv_sems.at[3 * t + j],
                    device_id=(cx, cy, c), device_id_type=MESH)
                cp.start()
                cps.append(cp)
        for cp in cps:
            cp.wait()

    return pl.pallas_call(
        body, name="grad_to_chips",
        out_shape=tuple(jax.ShapeDtypeStruct((3,) + s.shape[1:], s.dtype) for s in ss),
        in_specs=[ANY] * n, out_specs=(ANY,) * n,
        scratch_shapes=[pltpu.SemaphoreType.DMA((3 * n,)), pltpu.SemaphoreType.DMA((3 * n,))],
    )(*ss)


def _halves_exchange(fs):
    n = len(fs)

    def body(*refs):
        o_refs = refs[n:2 * n]
        send_sems, recv_sems = refs[2 * n:]
        x, y, c, _ = _place()
        sib = (x, y, 1 - c)
        cps = []
        for t in range(n):
            mine = _half_rows(o_refs[t], c)
            cp = pltpu.make_async_remote_copy(
                src_ref=mine, dst_ref=mine, send_sem=send_sems.at[t], recv_sem=recv_sems.at[t],
                device_id=sib, device_id_type=MESH)
            cp.start()
            cps.append(cp)
        for t in range(n):
            theirs = _half_rows(o_refs[t], 1 - c)
            pltpu.make_async_remote_copy(
                src_ref=theirs, dst_ref=theirs, send_sem=send_sems.at[t], recv_sem=recv_sems.at[t],
                device_id=sib, device_id_type=MESH).wait_recv()
        for cp in cps:
            cp.wait_send()

    return pl.pallas_call(
        body, name="halves_exchange",
        out_shape=tuple(jax.ShapeDtypeStruct(f.shape, f.dtype) for f in fs),
        in_specs=[ANY] * n, out_specs=(ANY,) * n, input_output_aliases={t: t for t in range(n)},
        scratch_shapes=[pltpu.SemaphoreType.DMA((n,)), pltpu.SemaphoreType.DMA((n,))],
    )(*fs)


def _small_allreduce(d_pool_w, d_ln_g, d_ln_b, d_ps, d_gq, d_gkv, loss_part):
    pw_rows = POOL_G * POOL_GD

    def body(pw_in, lng_in, lnb_in, ps_in, gq_in, gkv_in, loss_in, pw_out, vec_out,
             vec_in, pw_sib, vec_sib, pw_sum, vec_sum, pw_chip, vec_chip, send_sems, recv_sems):
        x, y, c, chips = _place()
        sib = (x, y, 1 - c)
        vec_in[...] = jnp.zeros_like(vec_in)
        vec_in[0:1, :] = lng_in[...]
        vec_in[1:2, :] = lnb_in[...]
        vec_in[2:3, 0:POOL_W] = ps_in[...]
        vec_in[3:4, 0:Q_LORA] = gq_in[...]
        vec_in[8:9, 0:KV_LORA] = gkv_in[...]
        vec_in[9:10, 0:128] = loss_in[...]

        def rdma(k, src, dst, to):
            return pltpu.make_async_remote_copy(src_ref=src, dst_ref=dst, send_sem=send_sems.at[k],
                                                recv_sem=recv_sems.at[k], device_id=to, device_id_type=MESH)

        a = [rdma(0, pw_in, pw_sib, sib), rdma(1, vec_in, vec_sib, sib)]
        for cp in a:
            cp.start()
        for cp in a:
            cp.wait()
        pw_sum[...] = pw_in[...] + pw_sib[...]
        vec_sum[...] = vec_in[...] + vec_sib[...]

        bufs = [(pw_sum, pw_chip, pw_out, pw_rows // 2), (vec_sum, vec_chip, vec_out, VEC_HALF)]
        cps = []
        for t, (sm, chip_buf, _, hr) in enumerate(bufs):
            rows = pl.ds(pl.multiple_of(c * hr, 8), hr)
            for j, (cx, cy) in enumerate(chips):
                cp = rdma(2 + 3 * t + j, sm.at[rows], chip_buf.at[j], (cx, cy, c))
                cp.start()
                cps.append(cp)
        for cp in cps:
            cp.wait()
        last = []
        for t, (sm, chip_buf, out, hr) in enumerate(bufs):
            rows = pl.ds(pl.multiple_of(c * hr, 8), hr)
            other = pl.ds(pl.multiple_of((1 - c) * hr, 8), hr)
            out[rows, :] = (sm[rows, :] + chip_buf[0]) + (chip_buf[1] + chip_buf[2])
            cp = rdma(8 + t, out.at[rows], out.at[rows], sib)
            cp.start()
            last.append((cp, rdma(8 + t, out.at[other], out.at[other], sib)))
        for cp, recv in last:
            recv.wait_recv()
            cp.wait_send()

    vm = pl.BlockSpec(memory_space=pltpu.VMEM)
    vec_shape = (VEC_ROWS, D_MODEL)
    return pl.pallas_call(
        body, name="small_allreduce",
        out_shape=(jax.ShapeDtypeStruct((pw_rows, POOL_GD), F32), jax.ShapeDtypeStruct(vec_shape, F32)),
        in_specs=[vm] * 7, out_specs=(vm, vm),
        scratch_shapes=[pltpu.VMEM(vec_shape, F32), pltpu.VMEM((pw_rows, POOL_GD), F32), pltpu.VMEM(vec_shape, F32),
                        pltpu.VMEM((pw_rows, POOL_GD), F32), pltpu.VMEM(vec_shape, F32),
                        pltpu.VMEM((3, pw_rows // 2, POOL_GD), F32), pltpu.VMEM((3, VEC_HALF, D_MODEL), F32),
                        pltpu.SemaphoreType.DMA((10,)), pltpu.SemaphoreType.DMA((10,))],
    )(d_pool_w.reshape(pw_rows, POOL_GD), d_ln_g, d_ln_b, d_ps, d_gq, d_gkv, loss_part)


ADD_STEPS = 2


def _add_sibling_half(gs, rs, place_arr):
    n = len(gs)

    def body(place_ref, *refs):
        me = place_ref[0]
        for t in range(n):
            g_ref, r_ref = refs[t], refs[n + t]
            refs[2 * n + 2 * t][...] = (g_ref[...] + r_ref[...]).astype(BF16)
            refs[2 * n + 2 * t + 1][...] = g_ref[me] + r_ref[me]

    in_specs, out_specs, out_shape = [], [], []
    for g in gs:
        br = g.shape[1] // 2 // ADD_STEPS
        in_specs.append(pl.BlockSpec((N_CHIPS, br, g.shape[2]), lambda i, p: (0, p[1] * ADD_STEPS + i, 0)))
    for r in rs:
        br = r.shape[1] // ADD_STEPS
        in_specs.append(pl.BlockSpec((N_CHIPS, br, r.shape[2]), lambda i, p: (0, i, 0)))
        out_specs += [pl.BlockSpec((N_CHIPS, br, r.shape[2]), lambda i, p: (0, i, 0)),
                      pl.BlockSpec((br, r.shape[2]), lambda i, p: (i, 0))]
        out_shape += [jax.ShapeDtypeStruct(r.shape, BF16), jax.ShapeDtypeStruct(r.shape[1:], F32)]
    outs = pl.pallas_call(
        body, name="add_sibling_half", out_shape=tuple(out_shape),
        grid_spec=pltpu.PrefetchScalarGridSpec(num_scalar_prefetch=1, grid=(ADD_STEPS,),
                                               in_specs=in_specs, out_specs=tuple(out_specs)),
        compiler_params=_cparams(1),
    )(place_arr, *gs, *rs)
    return list(outs[0::2]), list(outs[1::2])


def _add_chip_parts(owns, rs, place_arr):
    n = len(owns)

    def body(place_ref, *refs):
        for t in range(n):
            r_ref = refs[n + t]
            refs[2 * n + t][...] = ((refs[t][...] + r_ref[0].astype(F32))
                                    + (r_ref[1].astype(F32) + r_ref[2].astype(F32)))

    in_specs, out_specs = [], []
    for o in owns:
        br = o.shape[0] // ADD_STEPS
        in_specs.append(pl.BlockSpec((br, o.shape[1]), lambda i, p: (i, 0)))
    for r in rs:
        br = r.shape[1] // ADD_STEPS
        in_specs.append(pl.BlockSpec((3, br, r.shape[2]), lambda i, p: (0, i, 0)))
        out_specs.append(pl.BlockSpec((br, r.shape[2]), lambda i, p: (p[1] * ADD_STEPS + i, 0)))
    return pl.pallas_call(
        body, name="add_chip_parts",
        out_shape=tuple(jax.ShapeDtypeStruct((2 * o.shape[0], o.shape[1]), F32) for o in owns),
        grid_spec=pltpu.PrefetchScalarGridSpec(num_scalar_prefetch=1, grid=(ADD_STEPS,),
                                               in_specs=in_specs, out_specs=tuple(out_specs)),
        compiler_params=_cparams(1),
    )(place_arr, *owns, *rs)


def _adamw_math(g, w, m, v):
    nm = ADAM_B1 * m + (1.0 - ADAM_B1) * g
    nv = ADAM_B2 * v + (1.0 - ADAM_B2) * (g * g)
    m_hat = nm / (1.0 - ADAM_B1 ** ADAM_STEP)
    v_hat = nv / (1.0 - ADAM_B2 ** ADAM_STEP)
    return -ADAM_LR * (m_hat / (jnp.sqrt(v_hat) + ADAM_EPS) + ADAM_WD * w), nm, nv


ADAM_STEPS = 8


def _adamw_big(gs, ws, ms, vs):
    n = len(gs)

    def body(*refs):
        for t in range(n):
            d, nm, nv = _adamw_math(refs[t][...], refs[n + t][...], refs[2 * n + t][...], refs[3 * n + t][...])
            refs[4 * n + 3 * t][...] = d
            refs[4 * n + 3 * t + 1][...] = nm
            refs[4 * n + 3 * t + 2][...] = nv

    specs = [pl.BlockSpec((g.shape[0] // ADAM_STEPS, g.shape[1]), lambda i: (i, 0)) for g in gs]
    out_specs, out_shape = [], []
    for t in range(n):
        out_specs += [specs[t]] * 3
        out_shape += [jax.ShapeDtypeStruct(gs[t].shape, F32)] * 3
    outs = pl.pallas_call(
        body, name="adamw_big", grid=(ADAM_STEPS,),
        in_specs=specs * 4, out_specs=tuple(out_specs), out_shape=tuple(out_shape),
        compiler_params=_cparams(1),
    )(*gs, *ws, *ms, *vs)
    return [outs[3 * t: 3 * t + 3] for t in range(n)]


def _adamw_small(pw_sum, vec_sum, ws, ms, vs):
    rows = (None, 0, 1, 2, 3, 8)
    n = len(ws)

    def body(pw_ref, vec_ref, *refs):
        outs = refs[3 * n:]
        for t in range(n):
            w_ref, m_ref, v_ref = refs[t], refs[n + t], refs[2 * n + t]
            if rows[t] is None:
                g = pw_ref[...]
            else:
                g = vec_ref[rows[t]:rows[t] + 1, 0:w_ref.shape[1]]
            d, nm, nv = _adamw_math(g, w_ref[...], m_ref[...], v_ref[...])
            outs[4 * t][...] = g
            outs[4 * t + 1][...] = d
            outs[4 * t + 2][...] = nm
            outs[4 * t + 3][...] = nv
        outs[4 * n][...] = vec_ref[9:10, 0:128]

    vm = pl.BlockSpec(memory_space=pltpu.VMEM)
    out_shape = []
    for w in ws:
        out_shape += [jax.ShapeDtypeStruct(w.shape, F32)] * 4
    out_shape.append(jax.ShapeDtypeStruct((1, 128), F32))
    outs = pl.pallas_call(
        body, name="adamw_small", in_specs=[vm] * (2 + 3 * n), out_specs=(vm,) * (4 * n + 1),
        out_shape=tuple(out_shape),
    )(pw_sum, vec_sum, *ws, *ms, *vs)
    return [outs[4 * t: 4 * t + 4] for t in range(n)], outs[4 * n]


def _fwd_proj(x, w_in_e, w_uq_e, w_ukv, gq, gkv, rc, rsa, rsb, tm):
    T = x.shape[0]

    def body(x_ref, win_ref, wuq_ref, wukv_ref, gq_ref, gkv_ref, c_ref, sa_ref, sb_ref,
             xq_ref, xkv_ref, ga_ref, u_ref, gb_ref, q_ref, k_ref, v_ref):
        h = _dot(x_ref[...].astype(BF16), win_ref[...])
        xq = h[:, 0:512]
        xkv = h[:, 512:768]
        xq_ref[...] = xq
        xkv_ref[...] = xkv
        ga_ref[...] = h[:, 896:1408]
        u_ref[...] = h[:, 1408:1920]
        gb_ref[...] = h[:, 1920:2432]
        c, sa, sb = c_ref[...], sa_ref[...], sb_ref[...]
        rq = lax.rsqrt(jnp.mean(xq * xq, axis=-1, keepdims=True) + RMS_EPS)
        q = _dot(((xq * rq) * gq_ref[...]).astype(BF16), wuq_ref[...]) * QSCALE
        rkv = lax.rsqrt(jnp.mean(xkv * xkv, axis=-1, keepdims=True) + RMS_EPS)
        kv = _dot(((xkv * rkv) * gkv_ref[...]).astype(BF16), wukv_ref[...])
        kr = _rope(h[:, 768:896], c, sa, sb, 1.0).astype(BF16)
        for hh in range(HEADS):
            b0 = hh * HEAD_PAD
            q_ref[:, b0:b0 + 128] = q[:, b0:b0 + 128].astype(BF16)
            q_ref[:, b0 + 128:b0 + 256] = _rope(q[:, b0 + 128:b0 + 256], c, sa, sb, 1.0).astype(BF16)
            k_ref[:, b0:b0 + 128] = kv[:, b0:b0 + 128].astype(BF16)
            k_ref[:, b0 + 128:b0 + 256] = kr
            v_ref[:, hh * 128:(hh + 1) * 128] = kv[:, b0 + 128:b0 + 256].astype(BF16)

    row = lambda w: pl.BlockSpec((tm, w), lambda i: (i, 0))
    f = lambda w, dt: jax.ShapeDtypeStruct((T, w), dt)
    return pl.pallas_call(
        body, name="fwd_proj", grid=(T // tm,),
        in_specs=[row(D_MODEL), _full(w_in_e.shape), _full(w_uq_e.shape), _full(w_ukv.shape),
                  _full(gq.shape), _full(gkv.shape), row(128), row(128), row(128)],
        out_specs=(row(512), row(256), row(512), row(512), row(512), row(1024), row(1024), row(512)),
        out_shape=(f(512, F32), f(256, F32), f(512, F32), f(512, F32), f(512, F32),
                   f(1024, BF16), f(1024, BF16), f(512, BF16)),
        compiler_params=_cparams(1),
    )(x, w_in_e, w_uq_e, w_ukv, gq, gkv, rc, rsa, rsb)


def _attn_fwd(q, k, v, pos_col, pos_row, bounds, nb, S, tq, tk):
    T = q.shape[0]
    nq, nk = S // tq, S // tk
    reps = tk // 128

    def body(qmin_ref, qmax_ref, kmin_ref, kmax_ref, q_ref, k_ref, v_ref, pc_ref, pr_ref, o_ref, lse_ref,
             m_sc, l_sc, acc_sc):
        b, i = pl.program_id(0), pl.program_id(2)
        m_sc[...] = jnp.full(m_sc.shape, NEG, F32)
        l_sc[...] = jnp.zeros_like(l_sc)
        acc_sc[...] = jnp.zeros_like(acc_sc)
        q_lo = qmin_ref[b * nq + i]
        q_hi = qmax_ref[b * nq + i]

        def tile(j, masked):
            off = pl.multiple_of(j * tk, tk)
            s = _dot_nt(q_ref[...], k_ref[pl.ds(off, tk), :])
            if masked:
                s = jnp.where(pc_ref[...] >= pr_ref[pl.ds(j, 1), :], s, NEG)
            m_prev = m_sc[...]
            m_new = jnp.maximum(m_prev, jnp.max(s, axis=1, keepdims=True))
            p = jnp.exp2(s - jnp.concatenate([m_new] * reps, axis=1))
            a = jnp.exp2(m_prev - m_new)
            l_sc[...] = a * l_sc[...] + jnp.sum(p, axis=1, keepdims=True)
            acc_sc[...] = a * acc_sc[...] + _dot(p.astype(BF16), v_ref[pl.ds(off, tk), :])
            m_sc[...] = m_new

        def step(j, carry):
            visible = kmin_ref[b * nk + j] <= q_hi
            clear = q_lo >= kmax_ref[b * nk + j]

            @pl.when(jnp.logical_and(visible, clear))
            def _():
                tile(j, False)

            @pl.when(jnp.logical_and(visible, jnp.logical_not(clear)))
            def _():
                tile(j, True)
            return carry

        lax.fori_loop(0, nk, step, 0)
        l = l_sc[...]
        o_ref[...] = acc_sc[...] / l
        lse_ref[...] = m_sc[...] + jnp.log2(l)

    return pl.pallas_call(
        body, name="attn_fwd",
        grid_spec=pltpu.PrefetchScalarGridSpec(
            num_scalar_prefetch=4, grid=(nb, HEADS, nq),
            in_specs=[pl.BlockSpec((tq, HEAD_PAD), lambda b, h, i, *_: (b * nq + i, h)),
                      pl.BlockSpec((S, HEAD_PAD), lambda b, h, i, *_: (b, h)),
                      pl.BlockSpec((S, 128), lambda b, h, i, *_: (b, h)),
                      pl.BlockSpec((tq, 1), lambda b, h, i, *_: (b * nq + i, 0)),
                      pl.BlockSpec((None, nk, tk), lambda b, h, i, *_: (b, 0, 0))],
            out_specs=(pl.BlockSpec((tq, 128), lambda b, h, i, *_: (b * nq + i, h)),
                       pl.BlockSpec((tq, 128), lambda b, h, i, *_: (b * nq + i, h))),
            scratch_shapes=[pltpu.VMEM((tq, 128), F32), pltpu.VMEM((tq, 128), F32), pltpu.VMEM((tq, 128), F32)]),
        out_shape=(jax.ShapeDtypeStruct((T, MLA_W), F32), jax.ShapeDtypeStruct((T, MLA_W), F32)),
        compiler_params=_cparams(3),
    )(*bounds, q, k, v, pos_col, pos_row.reshape(nb, nk, tk))


def _mid(x, tgt, o, ga, u, gb, w_out, pool_w, pool_scale, ln_g, ln_b, S, tm):
    T = x.shape[0]
    tps = S // tm
    hb = tm // HALO

    def body(x_ref, tgt_ref, o_ref, ga_ref, u_ref, uh_ref, gb_ref, wout_ref, pw_ref,
             ps_ref, lng_ref, lnb_ref,
             dz_ref, do_ref, delta_ref, dga_ref, dgb_ref, dpc_ref,
             dwout_ref, dpw_ref, dps_ref, dlng_ref, dlnb_ref, loss_ref):
        i = pl.program_id(0)

        @pl.when(i == 0)
        def _():
            dwout_ref[...] = jnp.zeros_like(dwout_ref)
            dpw_ref[...] = jnp.zeros_like(dpw_ref)
            dps_ref[...] = jnp.zeros_like(dps_ref)
            dlng_ref[...] = jnp.zeros_like(dlng_ref)
            dlnb_ref[...] = jnp.zeros_like(dlnb_ref)
            loss_ref[...] = jnp.zeros_like(loss_ref)

        seq_tile = i % tps
        tpos = seq_tile * tm + lax.broadcasted_iota(jnp.int32, (tm, 1), 0)
        ga_v = ga_ref[...]
        sig_a = jax.nn.sigmoid(ga_v)
        silu_a = ga_v * sig_a
        o_v = o_ref[...]
        ya = o_v * silu_a

        u_v = u_ref[...]
        halo = jnp.where(seq_tile == 0, 0.0, uh_ref[...])
        pooled, cnts, mixed = [], [], []
        for g in range(POOL_G):
            lanes = slice(g * POOL_GD, (g + 1) * POOL_GD)
            w = jnp.concatenate([halo[:, lanes], u_v[:, lanes]], axis=0)
            for st in range(g + 1):
                w = w + pltpu.roll(w, 1 << st, 0)
            cnt = jnp.minimum(tpos + 1, 2 << g).astype(F32)
            pg = (w[HALO:, :] / cnt - u_v[:, lanes]).astype(BF16)
            pooled.append(pg)
            cnts.append(cnt)
            mixed.append(_dot(pg, pw_ref[g]))
        mixed = jnp.concatenate(mixed, axis=1)
        ps = ps_ref[...]
        ybp = mixed * ps
        gb_v = gb_ref[...]
        sig_b = jax.nn.sigmoid(gb_v)
        silu_b = gb_v * sig_b
        yb = ybp * silu_b

        cat = jnp.concatenate([ya, yb], axis=1).astype(BF16)
        z = ALPHA * x_ref[...] + _dot(cat, wout_ref[...])
        mu = jnp.mean(z, axis=-1, keepdims=True)
        zc = z - mu
        rstd = lax.rsqrt(jnp.mean(zc * zc, axis=-1, keepdims=True) + LN_EPS)
        zhat = zc * rstd
        lng = lng_ref[...]
        err = (zhat * lng + lnb_ref[...]) - tgt_ref[...]
        row_loss = jnp.sum(err * err, axis=1, keepdims=True)
        loss_ref[...] += jnp.broadcast_to(jnp.sum(row_loss, axis=0, keepdims=True) * (0.5 / D_MODEL), (1, 128))
        dy = err * (1.0 / D_MODEL)
        dlng_ref[...] += jnp.sum(dy * zhat, axis=0, keepdims=True)
        dlnb_ref[...] += jnp.sum(dy, axis=0, keepdims=True)
        dzh = dy * lng
        dz = rstd * (dzh - jnp.mean(dzh, axis=-1, keepdims=True)
                     - zhat * jnp.mean(dzh * zhat, axis=-1, keepdims=True))
        dz_ref[...] = dz
        dzb = dz.astype(BF16)
        dwout_ref[...] += _dot_tn(cat, dzb)
        dcat = _dot_nt(dzb, wout_ref[...])
        dya = dcat[:, :MLA_W]
        dyb = dcat[:, MLA_W:]

        do = dya * silu_a
        do_ref[...] = do.astype(BF16)
        prod = do * o_v
        for hh in range(HEADS):
            lanes = slice(hh * 128, (hh + 1) * 128)
            delta_ref[:, lanes] = jnp.broadcast_to(jnp.sum(prod[:, lanes], axis=1, keepdims=True), (tm, 128))
        dga_ref[...] = (dya * o_v * (sig_a * (1.0 + ga_v * (1.0 - sig_a)))).astype(BF16)
        dgb_ref[...] = (dyb * ybp * (sig_b * (1.0 + gb_v * (1.0 - sig_b)))).astype(BF16)
        dybp = dyb * silu_b
        dps_ref[...] += jnp.sum(dybp * mixed, axis=0, keepdims=True)
        dmixed = (dybp * ps).astype(BF16)
        for g in range(POOL_G):
            lanes = slice(g * POOL_GD, (g + 1) * POOL_GD)
            dpw_ref[g] += _dot_tn(pooled[g], dmixed[:, lanes])
            dpc_ref[:, lanes] = _dot_nt(dmixed[:, lanes], pw_ref[g]) / cnts[g]

    row = lambda w: pl.BlockSpec((tm, w), lambda i: (i, 0))
    f = lambda w, dt: jax.ShapeDtypeStruct((T, w), dt)
    halo_spec = pl.BlockSpec((HALO, POOL_W), lambda i: (jnp.maximum(i * hb - 1, 0), 0))
    return pl.pallas_call(
        body, name="mid", grid=(T // tm,),
        in_specs=[row(D_MODEL), row(D_MODEL), row(MLA_W), row(MLA_W), row(POOL_W), halo_spec, row(POOL_W),
                  _full(w_out.shape), _full(pool_w.shape),
                  _full(pool_scale.shape), _full(ln_g.shape), _full(ln_b.shape)],
        out_specs=(row(D_MODEL), row(MLA_W), row(MLA_W), row(MLA_W), row(POOL_W), row(POOL_W),
                   _full((D_MODEL, D_MODEL)), _full(pool_w.shape), _full((1, POOL_W)),
                   _full((1, D_MODEL)), _full((1, D_MODEL)), _full((1, 128))),
        out_shape=(f(D_MODEL, F32), f(MLA_W, BF16), f(MLA_W, F32), f(MLA_W, BF16), f(POOL_W, BF16), f(POOL_W, F32),
                   jax.ShapeDtypeStruct((D_MODEL, D_MODEL), F32), jax.ShapeDtypeStruct(pool_w.shape, F32),
                   jax.ShapeDtypeStruct((1, POOL_W), F32), jax.ShapeDtypeStruct((1, D_MODEL), F32),
                   jax.ShapeDtypeStruct((1, D_MODEL), F32), jax.ShapeDtypeStruct((1, 128), F32)),
        compiler_params=_cparams(1),
    )(x, tgt, o, ga, u, u, gb, w_out, pool_w, pool_scale, ln_g, ln_b)


def _attn_bwd(q, k, v, do, lse, delta, pos_col, pos_row, bounds, nb, S, tq, tk):
    T = q.shape[0]
    nq, nk = S // tq, S // tk
    reps = tk // 128

    def body(qmin_ref, qmax_ref, kmin_ref, kmax_ref, q_ref, k_ref, v_ref, do_ref, lse_ref, dl_ref, pc_ref, pr_ref,
             dq_ref, dk_ref, dv_ref):
        b, j = pl.program_id(0), pl.program_id(2)

        @pl.when(j == 0)
        def _():
            dq_ref[...] = jnp.zeros_like(dq_ref)

        dk_ref[...] = jnp.zeros_like(dk_ref)
        dv_ref[...] = jnp.zeros_like(dv_ref)
        k_lo = kmin_ref[b * nk + j]
        k_hi = kmax_ref[b * nk + j]

        def tile(i, masked):
            kb = k_ref[...]
            rows = pl.ds(pl.multiple_of(i * tq, tq), tq)
            qb = q_ref[rows, :]
            dob = do_ref[rows, :]
            s = _dot_nt(qb, kb)
            if masked:
                s = jnp.where(pc_ref[rows, :] >= pr_ref[...], s, NEG)
            p = jnp.exp2(s - jnp.concatenate([lse_ref[rows, :]] * reps, axis=1))
            dv_ref[...] += _dot_tn(p.astype(BF16), dob)
            dp = _dot_nt(dob, v_ref[...])
            ds = (p * (dp - jnp.concatenate([dl_ref[rows, :]] * reps, axis=1))).astype(BF16)
            dq_ref[rows, :] += _dot(ds, kb)
            dk_ref[...] += _dot_tn(ds, qb)

        def step(i, carry):
            visible = k_lo <= qmax_ref[b * nq + i]
            clear = qmin_ref[b * nq + i] >= k_hi

            @pl.when(jnp.logical_and(visible, clear))
            def _():
                tile(i, False)

            @pl.when(jnp.logical_and(visible, jnp.logical_not(clear)))
            def _():
                tile(i, True)
            return carry

        lax.fori_loop(0, nq, step, 0)

    seq = lambda w: pl.BlockSpec((S, w), lambda b, h, j, *_: (b, h))
    return pl.pallas_call(
        body, name="attn_bwd",
        grid_spec=pltpu.PrefetchScalarGridSpec(
            num_scalar_prefetch=4, grid=(nb, HEADS, nk),
            in_specs=[seq(HEAD_PAD),
                      pl.BlockSpec((tk, HEAD_PAD), lambda b, h, j, *_: (b * nk + j, h)),
                      pl.BlockSpec((tk, 128), lambda b, h, j, *_: (b * nk + j, h)),
                      seq(128), seq(128), seq(128),
                      pl.BlockSpec((S, 1), lambda b, h, j, *_: (b, 0)),
                      pl.BlockSpec((None, 1, tk), lambda b, h, j, *_: (b, 0, j))],
            out_specs=(seq(HEAD_PAD),
                       pl.BlockSpec((tk, HEAD_PAD), lambda b, h, j, *_: (b * nk + j, h)),
                       pl.BlockSpec((tk, 128), lambda b, h, j, *_: (b * nk + j, h)))),
        out_shape=(jax.ShapeDtypeStruct((T, HEADS * HEAD_PAD), F32),
                   jax.ShapeDtypeStruct((T, HEADS * HEAD_PAD), F32),
                   jax.ShapeDtypeStruct((T, MLA_W), F32)),
        compiler_params=_cparams(3),
    )(*bounds, q, k, v, do, lse, delta, pos_col, pos_row)


def _bwd_proj(dq, dk, dv, xq, xkv, x, dz, dga, dgb, dpc, rc, rsa, rsb, w_uq_e, w_ukv, w_in_e, gq, gkv, S, tm):
    T = x.shape[0]
    tps = S // tm
    hb = tm // HALO
    n_steps = T // tm

    def body(dq_ref, dk_ref, dv_ref, xq_ref, xkv_ref, x_ref, dz_ref, dga_ref, dgb_ref, dpc_ref, dph_ref,
             c_ref, sa_ref, sb_ref, wuq_ref, wukv_ref, win_ref, gq_ref, gkv_ref,
             dx_ref, dwin_hbm, dwuq_hbm, dwukv_hbm, dgq_ref, dgkv_ref,
             acc_win, acc_wuq, acc_wukv):
        i = pl.program_id(0)

        @pl.when(i == 0)
        def _():
            acc_win[...] = jnp.zeros_like(acc_win)
            acc_wuq[...] = jnp.zeros_like(acc_wuq)
            acc_wukv[...] = jnp.zeros_like(acc_wukv)
            dgq_ref[...] = jnp.zeros_like(dgq_ref)
            dgkv_ref[...] = jnp.zeros_like(dgkv_ref)

        c, sa, sb = c_ref[...], sa_ref[...], sb_ref[...]
        dq_v = dq_ref[...] * SCALE
        dk_v = dk_ref[...] * LN2
        dv_v = dv_ref[...]
        dq_parts, dkv_parts = [], []
        dkr = jnp.zeros((tm, 128), F32)
        for hh in range(HEADS):
            b0 = hh * HEAD_PAD
            dq_parts.append(dq_v[:, b0:b0 + 128].astype(BF16))
            dq_parts.append(_rope(dq_v[:, b0 + 128:b0 + 256], c, sa, sb, -1.0).astype(BF16))
            dkv_parts.append(dk_v[:, b0:b0 + 128].astype(BF16))
            dkv_parts.append(dv_v[:, hh * 128:(hh + 1) * 128].astype(BF16))
            dkr = dkr + dk_v[:, b0 + 128:b0 + 256]
        dqp = jnp.concatenate(dq_parts, axis=1)
        dkvp = jnp.concatenate(dkv_parts, axis=1)
        dkrr = _rope(dkr, c, sa, sb, -1.0)

        def rms_bwd(xv, g, dyn, dg_ref):
            r = lax.rsqrt(jnp.mean(xv * xv, axis=-1, keepdims=True) + RMS_EPS)
            xhat = xv * r
            dg_ref[...] += jnp.sum(dyn * xhat, axis=0, keepdims=True)
            dxh = dyn * g
            return r * (dxh - xhat * jnp.mean(dxh * xhat, axis=-1, keepdims=True))

        xq_v = xq_ref[...]
        gq_v = gq_ref[...]
        rq = lax.rsqrt(jnp.mean(xq_v * xq_v, axis=-1, keepdims=True) + RMS_EPS)
        acc_wuq[...] += _dot_tn(((xq_v * rq) * gq_v).astype(BF16), dqp)
        dxq = rms_bwd(xq_v, gq_v, _dot_nt(dqp, wuq_ref[...]), dgq_ref)

        xkv_v = xkv_ref[...]
        gkv_v = gkv_ref[...]
        rkv = lax.rsqrt(jnp.mean(xkv_v * xkv_v, axis=-1, keepdims=True) + RMS_EPS)
        acc_wukv[...] += _dot_tn(((xkv_v * rkv) * gkv_v).astype(BF16), dkvp)
        dxkv = rms_bwd(xkv_v, gkv_v, _dot_nt(dkvp, wukv_ref[...]), dgkv_ref)

        seq_tile = i % tps
        tpos = seq_tile * tm + lax.broadcasted_iota(jnp.int32, (tm, 1), 0)
        dpc_v = dpc_ref[...]
        halo = jnp.where(seq_tile == tps - 1, 0.0, dph_ref[...])
        n = tm + HALO
        du = []
        for g in range(POOL_G):
            lanes = slice(g * POOL_GD, (g + 1) * POOL_GD)
            f = jnp.concatenate([dpc_v[:, lanes], halo[:, lanes]], axis=0)
            for st in range(g + 1):
                f = f + pltpu.roll(f, n - (1 << st), 0)
            cnt = jnp.minimum(tpos + 1, 2 << g).astype(F32)
            du.append((f[:tm, :] - dpc_v[:, lanes] * cnt).astype(BF16))

        dh = jnp.concatenate([dxq.astype(BF16), dxkv.astype(BF16), dkrr.astype(BF16), dga_ref[...]]
                             + du + [dgb_ref[...]], axis=1)
        dx_ref[...] = ALPHA * dz_ref[...] + _dot_nt(dh, win_ref[...])
        acc_win[...] += _dot_tn(x_ref[...].astype(BF16), dh)

        @pl.when(i == n_steps - 1)
        def _():
            pltpu.sync_copy(acc_win, dwin_hbm)
            pltpu.sync_copy(acc_wuq, dwuq_hbm)
            pltpu.sync_copy(acc_wukv, dwukv_hbm)

    row = lambda w: pl.BlockSpec((tm, w), lambda i: (i, 0))
    halo_spec = pl.BlockSpec((HALO, POOL_W), lambda i: (jnp.minimum((i + 1) * hb, T // HALO - 1), 0))
    return pl.pallas_call(
        body, name="bwd_proj", grid=(n_steps,),
        in_specs=[row(1024), row(1024), row(512), row(512), row(256), row(D_MODEL), row(D_MODEL),
                  row(512), row(512), row(512), halo_spec, row(128), row(128), row(128),
                  _full(w_uq_e.shape), _full(w_ukv.shape), _full(w_in_e.shape), _full(gq.shape), _full(gkv.shape)],
        out_specs=(row(D_MODEL), ANY, ANY, ANY, _full((1, Q_LORA)), _full((1, KV_LORA))),
        out_shape=(jax.ShapeDtypeStruct((T, D_MODEL), F32),
                   jax.ShapeDtypeStruct((D_MODEL, IN_EXT), F32),
                   jax.ShapeDtypeStruct((Q_LORA, HEADS * HEAD_PAD), F32),
                   jax.ShapeDtypeStruct((KV_LORA, 1024), F32),
                   jax.ShapeDtypeStruct((1, Q_LORA), F32), jax.ShapeDtypeStruct((1, KV_LORA), F32)),
        scratch_shapes=[pltpu.VMEM((D_MODEL, IN_EXT), F32), pltpu.VMEM((Q_LORA, HEADS * HEAD_PAD), F32),
                        pltpu.VMEM((KV_LORA, 1024), F32)],
        compiler_params=_cparams(1),
    )(dq, dk, dv, xq, xkv, x, dz, dga, dgb, dpc, dpc, rc, rsa, rsb, w_uq_e, w_ukv, w_in_e, gq, gkv)


def kernel(x, positions, w_in, q_norm_g, w_uq, kv_norm_g, w_ukv, pool_w, pool_scale, w_out, ln_g, ln_b, loss_target, m_w_in, m_q_norm_g, m_w_uq, m_kv_norm_g, m_w_ukv, m_pool_w, m_pool_scale, m_w_out, m_ln_g, m_ln_b, v_w_in, v_q_norm_g, v_w_uq, v_kv_norm_g, v_w_ukv, v_pool_w, v_pool_scale, v_w_out, v_ln_g, v_ln_b):
    nb, S, _ = x.shape
    T = nb * S
    tm = min(256, S)
    tq = min(512, S)
    tk = min(512, S)
    assert S % tm == 0 and tm % HALO == 0 and S % tq == 0 and S % tk == 0

    cx, cy, cc = lax.axis_index("x"), lax.axis_index("y"), lax.axis_index("c")
    me = 2 * cx + cy
    place_arr = jnp.stack([me, cc]).astype(jnp.int32)

    def own_slot(w):
        return lax.dynamic_update_slice(jnp.zeros((N_CHIPS,) + w.shape, BF16), w.astype(BF16)[None], (me, 0, 0))

    w_in_g, w_uq_g, w_ukv_g, w_out_g = _weight_gather([own_slot(w_in), own_slot(w_uq), own_slot(w_ukv), own_slot(w_out)])
    w_in_f = w_in_g.transpose(1, 0, 2).reshape(D_MODEL, IN_W)
    w_in_e = jnp.concatenate([w_in_f[:, :832], jnp.zeros((D_MODEL, 64), BF16), w_in_f[:, 832:]], axis=1)
    w_uq_e = jnp.pad(w_uq_g.transpose(1, 0, 2), ((0, 0), (0, 0), (0, 64))).reshape(Q_LORA, HEADS * HEAD_PAD)
    w_ukv_f = w_ukv_g.transpose(1, 0, 2).reshape(KV_LORA, 1024)
    w_out_f = w_out_g.reshape(D_MODEL, D_MODEL)
    pool_w_b = pool_w.astype(BF16)
    gq2 = q_norm_g.reshape(1, Q_LORA)
    gkv2 = kv_norm_g.reshape(1, KV_LORA)
    ps2 = pool_scale.reshape(1, POOL_W)

    half = ROPE // 2
    inv_freq = ROPE_THETA ** (-jnp.arange(half, dtype=F32) / half)
    ang = positions.astype(F32).reshape(T, 1) * inv_freq
    cos, sin = jnp.cos(ang), jnp.sin(ang)
    z32, z64 = jnp.zeros((T, 32), F32), jnp.zeros((T, 64), F32)
    rc = jnp.concatenate([cos, cos, z64], axis=1)
    rsa = jnp.concatenate([sin, z32, z64], axis=1)
    rsb = jnp.concatenate([z32, sin, z64], axis=1)
    pos_col = positions.reshape(T, 1)
    pos_row = positions.reshape(nb, 1, S)
    pos_q = positions.reshape(nb, S // tq, tq)
    pos_k = positions.reshape(nb, S // tk, tk)
    bounds = (jnp.min(pos_q, axis=2).reshape(-1), jnp.max(pos_q, axis=2).reshape(-1),
              jnp.min(pos_k, axis=2).reshape(-1), jnp.max(pos_k, axis=2).reshape(-1))

    xf = x.reshape(T, D_MODEL)
    tgt = loss_target.reshape(T, D_MODEL)

    xq, xkv, ga, u, gb, q, k, v = _fwd_proj(xf, w_in_e, w_uq_e, w_ukv_f, gq2, gkv2, rc, rsa, rsb, tm)
    o, lse = _attn_fwd(q, k, v, pos_col, pos_row, bounds, nb, S, tq, tk)

    (dz, do, delta, dga, dgb, dpc, d_w_out, d_pool_w, d_pool_scale, d_ln_g, d_ln_b, loss_part) = _mid(
        xf, tgt, o, ga, u, gb, w_out_f, pool_w_b, ps2, ln_g, ln_b, S, tm)

    dq, dk, dv = _attn_bwd(q, k, v, do, lse, delta, pos_col, pos_row, bounds, nb, S, tq, tk)
    dx, d_w_in_e, d_w_uq_e, d_w_ukv, d_gq, d_gkv = _bwd_proj(
        dq, dk, dv, xq, xkv, xf, dz, dga, dgb, dpc, rc, rsa, rsb, w_uq_e, w_ukv_f, w_in_e, gq2, gkv2, S, tm)
    grad_x = dx.reshape(nb, S, D_MODEL)

    g_in = jnp.concatenate([d_w_in_e[:, :832], d_w_in_e[:, 896:]], axis=1).reshape(D_MODEL, N_CHIPS, 592).transpose(1, 0, 2)
    g_uq = d_w_uq_e.reshape(Q_LORA, HEADS, HEAD_PAD)[:, :, :NOPE + ROPE].transpose(1, 0, 2)
    g_ukv = d_w_ukv.reshape(KV_LORA, N_CHIPS, 256).transpose(1, 0, 2)
    g_out = d_w_out.reshape(N_CHIPS, 256, D_MODEL)
    gs = [g_in, g_uq, g_ukv, g_out]
    from_sibling = _grad_to_sibling(gs)
    chip_sums, own_sums = _add_sibling_half(gs, from_sibling, place_arr)
    from_chips = _grad_to_chips(chip_sums)
    g_big = _halves_exchange(_add_chip_parts(own_sums, from_chips, place_arr))

    pw_sum, vec_sum = _small_allreduce(d_pool_w, d_ln_g, d_ln_b, d_pool_scale, d_gq, d_gkv, loss_part)

    big = _adamw_big(g_big, [w_in, w_uq, w_ukv, w_out], [m_w_in, m_w_uq, m_w_ukv, m_w_out],
                     [v_w_in, v_w_uq, v_w_ukv, v_w_out])
    two_d = lambda a: a.reshape(-1, a.shape[-1])
    small_names = lambda pw, lg, lb, ps, gq, gkv: [two_d(pw), lg, lb, ps.reshape(1, -1), gq.reshape(1, -1), gkv.reshape(1, -1)]
    small, loss_row = _adamw_small(
        pw_sum, vec_sum,
        small_names(pool_w, ln_g, ln_b, pool_scale, q_norm_g, kv_norm_g),
        small_names(m_pool_w, m_ln_g, m_ln_b, m_pool_scale, m_q_norm_g, m_kv_norm_g),
        small_names(v_pool_w, v_ln_g, v_ln_b, v_pool_scale, v_q_norm_g, v_kv_norm_g))
    loss = loss_row[0, 0]

    def leaves(kind):
        b = [g_big[t] if kind == 0 else big[t][kind - 1] for t in range(N_BIG)]
        s = [small[t][kind] for t in range(6)]
        return (b[0], s[4].reshape(Q_LORA), b[1], s[5].reshape(KV_LORA), b[2],
                s[0].reshape(POOL_G, POOL_GD, POOL_GD), s[3].reshape(POOL_W), b[3], s[1], s[2])

    return (loss, grad_x) + leaves(0) + leaves(1) + leaves(2) + leaves(3)
```

```python
import functools

import jax
import jax.numpy as jnp
from jax import lax
from jax.experimental import pallas as pl
from jax.experimental.pallas import tpu as pltpu

F32 = jnp.float32
BF16 = jnp.bfloat16
MESH = pl.DeviceIdType.MESH

HEADS = 4
NOPE = 128
ROPE = 64
HEAD_PAD = 256
Q_LORA = 512
KV_LORA = 256
MLA_W = 512
POOL_W = 512
POOL_G = 4
POOL_GD = 128
D_MODEL = 1024
IN_W = 2368
IN_EXT = 2432
ROPE_THETA = 10000.0
RMS_EPS = 1e-6
LN_EPS = 1e-5
ALPHA = 2.0 ** 0.25
SCALE = 192.0 ** -0.5
LOG2E = 1.4426950408889634
LN2 = 0.6931471805599453
QSCALE = SCALE * LOG2E
NEG = float(jnp.finfo(jnp.float32).min)
HEAD_GROUP = 2
HALO = 16

ADAM_LR = 0.001
ADAM_B1 = 0.9
ADAM_B2 = 0.999
ADAM_EPS = 1e-08
ADAM_WD = 0.01
ADAM_STEP = 10

N_CHIPS = 4
N_BIG = 4
VEC_ROWS = 16
VEC_HALF = VEC_ROWS // 2

VMEM_LIMIT = 56 * 1024 * 1024


def _cparams(n_grid_dims=0, **kw):
    sem = ("arbitrary",) * n_grid_dims if n_grid_dims else None
    return pltpu.CompilerParams(dimension_semantics=sem, vmem_limit_bytes=VMEM_LIMIT, **kw)


def _full(shape):
    nd = len(shape)
    return pl.BlockSpec(shape, lambda *_: (0,) * nd)


def _dot(a, b):
    return jnp.dot(a, b, preferred_element_type=F32)


def _dot_nt(a, b):
    return lax.dot_general(a, b, (((1,), (1,)), ((), ())), preferred_element_type=F32)


def _dot_tn(a, b):
    return lax.dot_general(a, b, (((0,), (0,)), ((), ())), preferred_element_type=F32)


def _rope(g, c, sa, sb, sign):
    return g * c + sign * (pltpu.roll(g, 32, 1) * sb - pltpu.roll(g, 96, 1) * sa)


def _place():
    x, y, c = lax.axis_index("x"), lax.axis_index("y"), lax.axis_index("c")
    chips = [(1 - x, y), (x, 1 - y), (1 - x, 1 - y)]
    return x, y, c, chips


def _half_rows(ref, half_index, lead=None):
    axis = 0 if lead is None else 1
    hr = ref.shape[axis] // 2
    rows = pl.ds(half_index * hr, hr)
    return ref.at[rows] if lead is None else ref.at[lead, rows]


ANY = pl.BlockSpec(memory_space=pl.ANY)


def _weight_gather(slots):
    n = len(slots)

    def body(*refs):
        outs = refs[n:2 * n]
        send_sems, recv_sems = refs[2 * n:]
        x, y, c, chips = _place()
        me = 2 * x + y

        def copy(t, k, chip_idx, half, to):
            blk = _half_rows(outs[t], half, lead=chip_idx)
            return pltpu.make_async_remote_copy(
                src_ref=blk, dst_ref=blk, send_sem=send_sems.at[6 * t + k], recv_sem=recv_sems.at[6 * t + k],
                device_id=to, device_id_type=MESH)

        first = [copy(t, j, me, c, (cx, cy, c)) for t in range(n) for j, (cx, cy) in enumerate(chips)]
        for cp in first:
            cp.start()
        passed = []
        for j, (cx, cy) in enumerate(chips):
            for t in range(n):
                copy(t, j, 2 * cx + cy, c, (x, y, c)).wait_recv()
                fwd = copy(t, 3 + j, 2 * cx + cy, c, (x, y, 1 - c))
                fwd.start()
                passed.append(fwd)
        for j, (cx, cy) in enumerate(chips):
            for t in range(n):
                copy(t, 3 + j, 2 * cx + cy, 1 - c, (x, y, c)).wait_recv()
        for cp in first + passed:
            cp.wait_send()

    return pl.pallas_call(
        body, name="weight_gather",
        out_shape=tuple(jax.ShapeDtypeStruct(a.shape, a.dtype) for a in slots),
        in_specs=[ANY] * n, out_specs=(ANY,) * n, input_output_aliases={t: t for t in range(n)},
        scratch_shapes=[pltpu.SemaphoreType.DMA((6 * n,)), pltpu.SemaphoreType.DMA((6 * n,))],
    )(*slots)


def _grad_to_sibling(gs):
    n = len(gs)

    def body(*refs):
        g_refs, r_refs = refs[:n], refs[n:2 * n]
        send_sems, recv_sems = refs[2 * n:]
        x, y, c, _ = _place()
        cps = []
        for t in range(n):
            hr = gs[t].shape[1] // 2
            src = g_refs[t].at[pl.ds(0, N_CHIPS), pl.ds((1 - c) * hr, hr)]
            cp = pltpu.make_async_remote_copy(
                src_ref=src, dst_ref=r_refs[t], send_sem=send_sems.at[t], recv_sem=recv_sems.at[t],
                device_id=(x, y, 1 - c), device_id_type=MESH)
            cp.start()
            cps.append(cp)
        for cp in cps:
            cp.wait()

    return pl.pallas_call(
        body, name="grad_to_sibling",
        out_shape=tuple(jax.ShapeDtypeStruct((N_CHIPS, g.shape[1] // 2, g.shape[2]), F32) for g in gs),
        in_specs=[ANY] * n, out_specs=(ANY,) * n,
        scratch_shapes=[pltpu.SemaphoreType.DMA((n,)), pltpu.SemaphoreType.DMA((n,))],
    )(*gs)


def _grad_to_chips(ss):
    n = len(ss)

    def body(*refs):
        s_refs, r_refs = refs[:n], refs[n:2 * n]
        send_sems, recv_sems = refs[2 * n:]
        x, y, c, chips = _place()
        cps = []
        for t in range(n):
            for j, (cx, cy) in enumerate(chips):
                cp = pltpu.make_async_remote_copy(
                    src_ref=s_refs[t].at[2 * cx + cy], dst_ref=r_refs[t].at[j],
                    send_sem=send_sems.at[3 * t + j], recv_sem=recv_sems.at[3 * t + j],
                    device_id=(cx, cy, c), device_id_type=MESH)
                cp.start()
                cps.append(cp)
        for cp in cps:
            cp.wait()

    return pl.pallas_call(
        body, name="grad_to_chips",
        out_shape=tuple(jax.ShapeDtypeStruct((3,) + s.shape[1:], s.dtype) for s in ss),
        in_specs=[ANY] * n, out_specs=(ANY,) * n,
        scratch_shapes=[pltpu.SemaphoreType.DMA((3 * n,)), pltpu.SemaphoreType.DMA((3 * n,))],
    )(*ss)


def _halves_exchange(fs):
    n = len(fs)

    def body(*refs):
        o_refs = refs[n:2 * n]
        send_sems, recv_sems = refs[2 * n:]
        x, y, c, _ = _place()
        sib = (x, y, 1 - c)
        cps = []
        for t in range(n):
            mine = _half_rows(o_refs[t], c)
            cp = pltpu.make_async_remote_copy(
                src_ref=mine, dst_ref=mine, send_sem=send_sems.at[t], recv_sem=recv_sems.at[t],
                device_id=sib, device_id_type=MESH)
            cp.start()
            cps.append(cp)
        for t in range(n):
            theirs = _half_rows(o_refs[t], 1 - c)
            pltpu.make_async_remote_copy(
                src_ref=theirs, dst_ref=theirs, send_sem=send_sems.at[t], recv_sem=recv_sems.at[t],
                device_id=sib, device_id_type=MESH).wait_recv()
        for cp in cps:
            cp.wait_send()

    return pl.pallas_call(
        body, name="halves_exchange",
        out_shape=tuple(jax.ShapeDtypeStruct(f.shape, f.dtype) for f in fs),
        in_specs=[ANY] * n, out_specs=(ANY,) * n, input_output_aliases={t: t for t in range(n)},
        scratch_shapes=[pltpu.SemaphoreType.DMA((n,)), pltpu.SemaphoreType.DMA((n,))],
    )(*fs)


def _small_allreduce(d_pool_w, d_ln_g, d_ln_b, d_ps, d_gq, d_gkv, loss_part):
    pw_rows = POOL_G * POOL_GD

    def body(pw_in, lng_in, lnb_in, ps_in, gq_in, gkv_in, loss_in, pw_out, vec_out,
             vec_in, pw_sib, vec_sib, pw_sum, vec_sum, pw_chip, vec_chip, send_sems, recv_sems):
        x, y, c, chips = _place()
        sib = (x, y, 1 - c)
        vec_in[...] = jnp.zeros_like(vec_in)
        vec_in[0:1, :] = lng_in[...]
        vec_in[1:2, :] = lnb_in[...]
        vec_in[2:3, 0:POOL_W] = ps_in[...]
        vec_in[3:4, 0:Q_LORA] = gq_in[...]
        vec_in[8:9, 0:KV_LORA] = gkv_in[...]
        vec_in[9:10, 0:128] = loss_in[...]

        def rdma(k, src, dst, to):
            return pltpu.make_async_remote_copy(src_ref=src, dst_ref=dst, send_sem=send_sems.at[k],
                                                recv_sem=recv_sems.at[k], device_id=to, device_id_type=MESH)

        a = [rdma(0, pw_in, pw_sib, sib), rdma(1, vec_in, vec_sib, sib)]
        for cp in a:
            cp.start()
        for cp in a:
            cp.wait()
        pw_sum[...] = pw_in[...] + pw_sib[...]
        vec_sum[...] = vec_in[...] + vec_sib[...]

        bufs = [(pw_sum, pw_chip, pw_out, pw_rows // 2), (vec_sum, vec_chip, vec_out, VEC_HALF)]
        cps = []
        for t, (sm, chip_buf, _, hr) in enumerate(bufs):
            rows = pl.ds(pl.multiple_of(c * hr, 8), hr)
            for j, (cx, cy) in enumerate(chips):
                cp = rdma(2 + 3 * t + j, sm.at[rows], chip_buf.at[j], (cx, cy, c))
                cp.start()
                cps.append(cp)
        for cp in cps:
            cp.wait()
        last = []
        for t, (sm, chip_buf, out, hr) in enumerate(bufs):
            rows = pl.ds(pl.multiple_of(c * hr, 8), hr)
            other = pl.ds(pl.multiple_of((1 - c) * hr, 8), hr)
            out[rows, :] = (sm[rows, :] + chip_buf[0]) + (chip_buf[1] + chip_buf[2])
            cp = rdma(8 + t, out.at[rows], out.at[rows], sib)
            cp.start()
            last.append((cp, rdma(8 + t, out.at[other], out.at[other], sib)))
        for cp, recv in last:
            recv.wait_recv()
            cp.wait_send()

    vm = pl.BlockSpec(memory_space=pltpu.VMEM)
    vec_shape = (VEC_ROWS, D_MODEL)
    return pl.pallas_call(
        body, name="small_allreduce",
        out_shape=(jax.ShapeDtypeStruct((pw_rows, POOL_GD), F32), jax.ShapeDtypeStruct(vec_shape, F32)),
        in_specs=[vm] * 7, out_specs=(vm, vm),
        scratch_shapes=[pltpu.VMEM(vec_shape, F32), pltpu.VMEM((pw_rows, POOL_GD), F32), pltpu.VMEM(vec_shape, F32),
                        pltpu.VMEM((pw_rows, POOL_GD), F32), pltpu.VMEM(vec_shape, F32),
                        pltpu.VMEM((3, pw_rows // 2, POOL_GD), F32), pltpu.VMEM((3, VEC_HALF, D_MODEL), F32),
                        pltpu.SemaphoreType.DMA((10,)), pltpu.SemaphoreType.DMA((10,))],
    )(d_pool_w.reshape(pw_rows, POOL_GD), d_ln_g, d_ln_b, d_ps, d_gq, d_gkv, loss_part)


ADD_STEPS = 2


def _add_sibling_half(gs, rs, place_arr):
    n = len(gs)

    def body(place_ref, *refs):
        me = place_ref[0]
        for t in range(n):
            g_ref, r_ref = refs[t], refs[n + t]
            refs[2 * n + 2 * t][...] = (g_ref[...] + r_ref[...]).astype(BF16)
            refs[2 * n + 2 * t + 1][...] = g_ref[me] + r_ref[me]

    in_specs, out_specs, out_shape = [], [], []
    for g in gs:
        br = g.shape[1] // 2 // ADD_STEPS
        in_specs.append(pl.BlockSpec((N_CHIPS, br, g.shape[2]), lambda i, p: (0, p[1] * ADD_STEPS + i, 0)))
    for r in rs:
        br = r.shape[1] // ADD_STEPS
        in_specs.append(pl.BlockSpec((N_CHIPS, br, r.shape[2]), lambda i, p: (0, i, 0)))
        out_specs += [pl.BlockSpec((N_CHIPS, br, r.shape[2]), lambda i, p: (0, i, 0)),
                      pl.BlockSpec((br, r.shape[2]), lambda i, p: (i, 0))]
        out_shape += [jax.ShapeDtypeStruct(r.shape, BF16), jax.ShapeDtypeStruct(r.shape[1:], F32)]
    outs = pl.pallas_call(
        body, name="add_sibling_half", out_shape=tuple(out_shape),
        grid_spec=pltpu.PrefetchScalarGridSpec(num_scalar_prefetch=1, grid=(ADD_STEPS,),
                                               in_specs=in_specs, out_specs=tuple(out_specs)),
        compiler_params=_cparams(1),
    )(place_arr, *gs, *rs)
    return list(outs[0::2]), list(outs[1::2])


def _add_chip_parts(owns, rs, place_arr):
    n = len(owns)

    def body(place_ref, *refs):
        for t in range(n):
            r_ref = refs[n + t]
            refs[2 * n + t][...] = ((refs[t][...] + r_ref[0].astype(F32))
                                    + (r_ref[1].astype(F32) + r_ref[2].astype(F32)))

    in_specs, out_specs = [], []
    for o in owns:
        br = o.shape[0] // ADD_STEPS
        in_specs.append(pl.BlockSpec((br, o.shape[1]), lambda i, p: (i, 0)))
    for r in rs:
        br = r.shape[1] // ADD_STEPS
        in_specs.append(pl.BlockSpec((3, br, r.shape[2]), lambda i, p: (0, i, 0)))
        out_specs.append(pl.BlockSpec((br, r.shape[2]), lambda i, p: (p[1] * ADD_STEPS + i, 0)))
    return pl.pallas_call(
        body, name="add_chip_parts",
        out_shape=tuple(jax.ShapeDtypeStruct((2 * o.shape[0], o.shape[1]), F32) for o in owns),
        grid_spec=pltpu.PrefetchScalarGridSpec(num_scalar_prefetch=1, grid=(ADD_STEPS,),
                                               in_specs=in_specs, out_specs=tuple(out_specs)),
        compiler_params=_cparams(1),
    )(place_arr, *owns, *rs)


def _adamw_math(g, w, m, v):
    nm = ADAM_B1 * m + (1.0 - ADAM_B1) * g
    nv = ADAM_B2 * v + (1.0 - ADAM_B2) * (g * g)
    m_hat = nm / (1.0 - ADAM_B1 ** ADAM_STEP)
    v_hat = nv / (1.0 - ADAM_B2 ** ADAM_STEP)
    return -ADAM_LR * (m_hat / (jnp.sqrt(v_hat) + ADAM_EPS) + ADAM_WD * w), nm, nv


ADAM_STEPS = 8


def _adamw_big(gs, ws, ms, vs):
    n = len(gs)

    def body(*refs):
        for t in range(n):
            d, nm, nv = _adamw_math(refs[t][...], refs[n + t][...], refs[2 * n + t][...], refs[3 * n + t][...])
            refs[4 * n + 3 * t][...] = d
            refs[4 * n + 3 * t + 1][...] = nm
            refs[4 * n + 3 * t + 2][...] = nv

    specs = [pl.BlockSpec((g.shape[0] // ADAM_STEPS, g.shape[1]), lambda i: (i, 0)) for g in gs]
    out_specs, out_shape = [], []
    for t in range(n):
        out_specs += [specs[t]] * 3
        out_shape += [jax.ShapeDtypeStruct(gs[t].shape, F32)] * 3
    outs = pl.pallas_call(
        body, name="adamw_big", grid=(ADAM_STEPS,),
        in_specs=specs * 4, out_specs=tuple(out_specs), out_shape=tuple(out_shape),
        compiler_params=_cparams(1),
    )(*gs, *ws, *ms, *vs)
    return [outs[3 * t: 3 * t + 3] for t in range(n)]


def _adamw_small(pw_sum, vec_sum, ws, ms, vs):
    rows = (None, 0, 1, 2, 3, 8)
    n = len(ws)

    def body(pw_ref, vec_ref, *refs):
        outs = refs[3 * n:]
        for t in range(n):
            w_ref, m_ref, v_ref = refs[t], refs[n + t], refs[2 * n + t]
            if rows[t] is None:
                g = pw_ref[...]
            else:
                g = vec_ref[rows[t]:rows[t] + 1, 0:w_ref.shape[1]]
            d, nm, nv = _adamw_math(g, w_ref[...], m_ref[...], v_ref[...])
            outs[4 * t][...] = g
            outs[4 * t + 1][...] = d
            outs[4 * t + 2][...] = nm
            outs[4 * t + 3][...] = nv
        outs[4 * n][...] = vec_ref[9:10, 0:128]

    vm = pl.BlockSpec(memory_space=pltpu.VMEM)
    out_shape = []
    for w in ws:
        out_shape += [jax.ShapeDtypeStruct(w.shape, F32)] * 4
    out_shape.append(jax.ShapeDtypeStruct((1, 128), F32))
    outs = pl.pallas_call(
        body, name="adamw_small", in_specs=[vm] * (2 + 3 * n), out_specs=(vm,) * (4 * n + 1),
        out_shape=tuple(out_shape),
    )(pw_sum, vec_sum, *ws, *ms, *vs)
    return [outs[4 * t: 4 * t + 4] for t in range(n)], outs[4 * n]


def _fwd_proj(x, w_in_e, w_uq_e, w_ukv, gq, gkv, rc, rsa, rsb, tm):
    T = x.shape[0]

    def body(x_ref, win_ref, wuq_ref, wukv_ref, gq_ref, gkv_ref, c_ref, sa_ref, sb_ref,
             xq_ref, xkv_ref, ga_ref, u_ref, gb_ref, q_ref, k_ref, v_ref):
        h = _dot(x_ref[...].astype(BF16), win_ref[...])
        xq = h[:, 0:512]
        xkv = h[:, 512:768]
        xq_ref[...] = xq
        xkv_ref[...] = xkv
        ga_ref[...] = h[:, 896:1408]
        u_ref[...] = h[:, 1408:1920]
        gb_ref[...] = h[:, 1920:2432]
        c, sa, sb = c_ref[...], sa_ref[...], sb_ref[...]
        rq = lax.rsqrt(jnp.mean(xq * xq, axis=-1, keepdims=True) + RMS_EPS)
        q = _dot(((xq * rq) * gq_ref[...]).astype(BF16), wuq_ref[...]) * QSCALE
        rkv = lax.rsqrt(jnp.mean(xkv * xkv, axis=-1, keepdims=True) + RMS_EPS)
        kv = _dot(((xkv * rkv) * gkv_ref[...]).astype(BF16), wukv_ref[...])
        kr = _rope(h[:, 768:896], c, sa, sb, 1.0).astype(BF16)
        for hh in range(HEADS):
            b0 = hh * HEAD_PAD
            q_ref[:, b0:b0 + 128] = q[:, b0:b0 + 128].astype(BF16)
            q_ref[:, b0 + 128:b0 + 256] = _rope(q[:, b0 + 128:b0 + 256], c, sa, sb, 1.0).astype(BF16)
            k_ref[:, b0:b0 + 128] = kv[:, b0:b0 + 128].astype(BF16)
            k_ref[:, b0 + 128:b0 + 256] = kr
            v_ref[:, hh * 128:(hh + 1) * 128] = kv[:, b0 + 128:b0 + 256].astype(BF16)

    row = lambda w: pl.BlockSpec((tm, w), lambda i: (i, 0))
    f = lambda w, dt: jax.ShapeDtypeStruct((T, w), dt)
    return pl.pallas_call(
        body, name="fwd_proj", grid=(T // tm,),
        in_specs=[row(D_MODEL), _full(w_in_e.shape), _full(w_uq_e.shape), _full(w_ukv.shape),
                  _full(gq.shape), _full(gkv.shape), row(128), row(128), row(128)],
        out_specs=(row(512), row(256), row(512), row(512), row(512), row(1024), row(1024), row(512)),
        out_shape=(f(512, F32), f(256, F32), f(512, F32), f(512, F32), f(512, F32),
                   f(1024, BF16), f(1024, BF16), f(512, BF16)),
        compiler_params=_cparams(1),
    )(x, w_in_e, w_uq_e, w_ukv, gq, gkv, rc, rsa, rsb)


def _attn_fwd(q, k, v, pos_col, pos_row, bounds, nb, S, tq, tk):
    T = q.shape[0]
    nq, nk = S // tq, S // tk
    reps = tk // 128
    hg = HEAD_GROUP

    def body(qmin_ref, qmax_ref, kmin_ref, kmax_ref, q_ref, k_ref, v_ref, pc_ref, pr_ref, o_ref, lse_ref,
             m_sc, l_sc, acc_sc):
        b, i = pl.program_id(0), pl.program_id(2)
        m_sc[...] = jnp.full(m_sc.shape, NEG, F32)
        l_sc[...] = jnp.zeros_like(l_sc)
        acc_sc[...] = jnp.zeros_like(acc_sc)
        q_lo = qmin_ref[b * nq + i]
        q_hi = qmax_ref[b * nq + i]

        def tile(j, masked):
            off = pl.multiple_of(j * tk, tk)
            if masked:
                keep = pc_ref[...] >= pr_ref[pl.ds(j, 1), :]
            logits = []
            for g in range(hg):
                qk = slice(g * HEAD_PAD, (g + 1) * HEAD_PAD)
                s = _dot_nt(q_ref[:, qk], k_ref[pl.ds(off, tk), qk])
                if masked:
                    s = jnp.where(keep, s, NEG)
                logits.append(s)
            probs = []
            for g in range(hg):
                hv = slice(g * 128, (g + 1) * 128)
                s = logits[g]
                m_prev = m_sc[:, hv]
                m_new = jnp.maximum(m_prev, jnp.max(s, axis=1, keepdims=True))
                p = jnp.exp2(s - jnp.concatenate([m_new] * reps, axis=1))
                a = jnp.exp2(m_prev - m_new)
                l_sc[:, hv] = a * l_sc[:, hv] + jnp.sum(p, axis=1, keepdims=True)
                m_sc[:, hv] = m_new
                probs.append((p.astype(BF16), a))
            for g in range(hg):
                hv = slice(g * 128, (g + 1) * 128)
                p, a = probs[g]
                acc_sc[:, hv] = a * acc_sc[:, hv] + _dot(p, v_ref[pl.ds(off, tk), hv])

        def step(j, carry):
            visible = kmin_ref[b * nk + j] <= q_hi
            clear = q_lo >= kmax_ref[b * nk + j]

            @pl.when(jnp.logical_and(visible, clear))
            def _():
                tile(j, False)

            @pl.when(jnp.logical_and(visible, jnp.logical_not(clear)))
            def _():
                tile(j, True)
            return carry

        lax.fori_loop(0, nk, step, 0)
        l = l_sc[...]
        o_ref[...] = acc_sc[...] / l
        lse_ref[...] = m_sc[...] + jnp.log2(l)

    ng = HEADS // hg
    stat = pltpu.VMEM((tq, hg * 128), F32)
    return pl.pallas_call(
        body, name="attn_fwd",
        grid_spec=pltpu.PrefetchScalarGridSpec(
            num_scalar_prefetch=4, grid=(nb, ng, nq),
            in_specs=[pl.BlockSpec((tq, hg * HEAD_PAD), lambda b, h, i, *_: (b * nq + i, h)),
                      pl.BlockSpec((S, hg * HEAD_PAD), lambda b, h, i, *_: (b, h)),
                      pl.BlockSpec((S, hg * 128), lambda b, h, i, *_: (b, h)),
                      pl.BlockSpec((tq, 1), lambda b, h, i, *_: (b * nq + i, 0)),
                      pl.BlockSpec((None, nk, tk), lambda b, h, i, *_: (b, 0, 0))],
            out_specs=(pl.BlockSpec((tq, hg * 128), lambda b, h, i, *_: (b * nq + i, h)),
                       pl.BlockSpec((tq, hg * 128), lambda b, h, i, *_: (b * nq + i, h))),
            scratch_shapes=[stat, stat, stat]),
        out_shape=(jax.ShapeDtypeStruct((T, MLA_W), F32), jax.ShapeDtypeStruct((T, MLA_W), F32)),
        compiler_params=_cparams(3),
    )(*bounds, q, k, v, pos_col, pos_row.reshape(nb, nk, tk))


def _mid(x, tgt, o, ga, u, gb, w_out, pool_w, pool_scale, ln_g, ln_b, S, tm):
    T = x.shape[0]
    tps = S // tm
    hb = tm // HALO

    def body(x_ref, tgt_ref, o_ref, ga_ref, u_ref, uh_ref, gb_ref, wout_ref, pw_ref,
             ps_ref, lng_ref, lnb_ref,
             dz_ref, do_ref, delta_ref, dga_ref, dgb_ref, dpc_ref,
             dwout_ref, dpw_ref, dps_ref, dlng_ref, dlnb_ref, loss_ref):
        i = pl.program_id(0)

        @pl.when(i == 0)
        def _():
            dwout_ref[...] = jnp.zeros_like(dwout_ref)
            dpw_ref[...] = jnp.zeros_like(dpw_ref)
            dps_ref[...] = jnp.zeros_like(dps_ref)
            dlng_ref[...] = jnp.zeros_like(dlng_ref)
            dlnb_ref[...] = jnp.zeros_like(dlnb_ref)
            loss_ref[...] = jnp.zeros_like(loss_ref)

        seq_tile = i % tps
        tpos = seq_tile * tm + lax.broadcasted_iota(jnp.int32, (tm, 1), 0)
        ga_v = ga_ref[...]
        sig_a = jax.nn.sigmoid(ga_v)
        silu_a = ga_v * sig_a
        o_v = o_ref[...]
        ya = o_v * silu_a

        u_v = u_ref[...]
        halo = jnp.where(seq_tile == 0, 0.0, uh_ref[...])
        pooled, cnts, mixed = [], [], []
        for g in range(POOL_G):
            lanes = slice(g * POOL_GD, (g + 1) * POOL_GD)
            w = jnp.concatenate([halo[:, lanes], u_v[:, lanes]], axis=0)
            for st in range(g + 1):
                w = w + pltpu.roll(w, 1 << st, 0)
            cnt = jnp.minimum(tpos + 1, 2 << g).astype(F32)
            pg = (w[HALO:, :] / cnt - u_v[:, lanes]).astype(BF16)
            pooled.append(pg)
            cnts.append(cnt)
            mixed.append(_dot(pg, pw_ref[g]))
        mixed = jnp.concatenate(mixed, axis=1)
        ps = ps_ref[...]
        ybp = mixed * ps
        gb_v = gb_ref[...]
        sig_b = jax.nn.sigmoid(gb_v)
        silu_b = gb_v * sig_b
        yb = ybp * silu_b

        cat = jnp.concatenate([ya, yb], axis=1).astype(BF16)
        z = ALPHA * x_ref[...] + _dot(cat, wout_ref[...])
        mu = jnp.mean(z, axis=-1, keepdims=True)
        zc = z - mu
        rstd = lax.rsqrt(jnp.mean(zc * zc, axis=-1, keepdims=True) + LN_EPS)
        zhat = zc * rstd
        lng = lng_ref[...]
        err = (zhat * lng + lnb_ref[...]) - tgt_ref[...]
        row_loss = jnp.sum(err * err, axis=1, keepdims=True)
        loss_ref[...] += jnp.broadcast_to(jnp.sum(row_loss, axis=0, keepdims=True) * (0.5 / D_MODEL), (1, 128))
        dy = err * (1.0 / D_MODEL)
        dlng_ref[...] += jnp.sum(dy * zhat, axis=0, keepdims=True)
        dlnb_ref[...] += jnp.sum(dy, axis=0, keepdims=True)
        dzh = dy * lng
        dz = rstd * (dzh - jnp.mean(dzh, axis=-1, keepdims=True)
                     - zhat * jnp.mean(dzh * zhat, axis=-1, keepdims=True))
        dz_ref[...] = dz
        dzb = dz.astype(BF16)
        dwout_ref[...] += _dot_tn(cat, dzb)
        dcat = _dot_nt(dzb, wout_ref[...])
        dya = dcat[:, :MLA_W]
        dyb = dcat[:, MLA_W:]

        do = dya * silu_a
        do_ref[...] = do.astype(BF16)
        prod = do * o_v
        for hh in range(HEADS):
            lanes = slice(hh * 128, (hh + 1) * 128)
            delta_ref[:, lanes] = jnp.broadcast_to(jnp.sum(prod[:, lanes], axis=1, keepdims=True), (tm, 128))
        dga_ref[...] = (dya * o_v * (sig_a * (1.0 + ga_v * (1.0 - sig_a)))).astype(BF16)
        dgb_ref[...] = (dyb * ybp * (sig_b * (1.0 + gb_v * (1.0 - sig_b)))).astype(BF16)
        dybp = dyb * silu_b
        dps_ref[...] += jnp.sum(dybp * mixed, axis=0, keepdims=True)
        dmixed = (dybp * ps).astype(BF16)
        for g in range(POOL_G):
            lanes = slice(g * POOL_GD, (g + 1) * POOL_GD)
            dpw_ref[g] += _dot_tn(pooled[g], dmixed[:, lanes])
            dpc_ref[:, lanes] = _dot_nt(dmixed[:, lanes], pw_ref[g]) / cnts[g]

    row = lambda w: pl.BlockSpec((tm, w), lambda i: (i, 0))
    f = lambda w, dt: jax.ShapeDtypeStruct((T, w), dt)
    halo_spec = pl.BlockSpec((HALO, POOL_W), lambda i: (jnp.maximum(i * hb - 1, 0), 0))
    return pl.pallas_call(
        body, name="mid", grid=(T // tm,),
        in_specs=[row(D_MODEL), row(D_MODEL), row(MLA_W), row(MLA_W), row(POOL_W), halo_spec, row(POOL_W),
                  _full(w_out.shape), _full(pool_w.shape),
                  _full(pool_scale.shape), _full(ln_g.shape), _full(ln_b.shape)],
        out_specs=(row(D_MODEL), row(MLA_W), row(MLA_W), row(MLA_W), row(POOL_W), row(POOL_W),
                   _full((D_MODEL, D_MODEL)), _full(pool_w.shape), _full((1, POOL_W)),
                   _full((1, D_MODEL)), _full((1, D_MODEL)), _full((1, 128))),
        out_shape=(f(D_MODEL, F32), f(MLA_W, BF16), f(MLA_W, F32), f(MLA_W, BF16), f(POOL_W, BF16), f(POOL_W, F32),
                   jax.ShapeDtypeStruct((D_MODEL, D_MODEL), F32), jax.ShapeDtypeStruct(pool_w.shape, F32),
                   jax.ShapeDtypeStruct((1, POOL_W), F32), jax.ShapeDtypeStruct((1, D_MODEL), F32),
                   jax.ShapeDtypeStruct((1, D_MODEL), F32), jax.ShapeDtypeStruct((1, 128), F32)),
        compiler_params=_cparams(1),
    )(x, tgt, o, ga, u, u, gb, w_out, pool_w, pool_scale, ln_g, ln_b)


def _attn_bwd(q, k, v, do, lse, delta, pos_col, pos_row, bounds, nb, S, tq, tk):
    T = q.shape[0]
    nq, nk = S // tq, S // tk
    reps = tk // 128
    hg = HEAD_GROUP

    def body(qmin_ref, qmax_ref, kmin_ref, kmax_ref, q_ref, k_ref, v_ref, do_ref, lse_ref, dl_ref, pc_ref, pr_ref,
             dq_ref, dk_ref, dv_ref):
        b, j = pl.program_id(0), pl.program_id(2)

        @pl.when(j == 0)
        def _():
            dq_ref[...] = jnp.zeros_like(dq_ref)

        dk_ref[...] = jnp.zeros_like(dk_ref)
        dv_ref[...] = jnp.zeros_like(dv_ref)
        k_lo = kmin_ref[b * nk + j]
        k_hi = kmax_ref[b * nk + j]

        def tile(i, masked):
            rows = pl.ds(pl.multiple_of(i * tq, tq), tq)
            if masked:
                keep = pc_ref[rows, :] >= pr_ref[...]
            stage = []
            for g in range(hg):
                qk = slice(g * HEAD_PAD, (g + 1) * HEAD_PAD)
                hv = slice(g * 128, (g + 1) * 128)
                s = _dot_nt(q_ref[rows, qk], k_ref[:, qk])
                if masked:
                    s = jnp.where(keep, s, NEG)
                stage.append((s, _dot_nt(do_ref[rows, hv], v_ref[:, hv])))
            grads = []
            for g in range(hg):
                hv = slice(g * 128, (g + 1) * 128)
                s, dp = stage[g]
                p = jnp.exp2(s - jnp.concatenate([lse_ref[rows, hv]] * reps, axis=1))
                ds = (p * (dp - jnp.concatenate([dl_ref[rows, hv]] * reps, axis=1))).astype(BF16)
                grads.append((p.astype(BF16), ds))
            for g in range(hg):
                qk = slice(g * HEAD_PAD, (g + 1) * HEAD_PAD)
                hv = slice(g * 128, (g + 1) * 128)
                p, ds = grads[g]
                dv_ref[:, hv] += _dot_tn(p, do_ref[rows, hv])
                dq_ref[rows, qk] += _dot(ds, k_ref[:, qk])
                dk_ref[:, qk] += _dot_tn(ds, q_ref[rows, qk])

        def step(i, carry):
            visible = k_lo <= qmax_ref[b * nq + i]
            clear = qmin_ref[b * nq + i] >= k_hi

            @pl.when(jnp.logical_and(visible, clear))
            def _():
                tile(i, False)

            @pl.when(jnp.logical_and(visible, jnp.logical_not(clear)))
            def _():
                tile(i, True)
            return carry

        lax.fori_loop(0, nq, step, 0)

    ng = HEADS // hg
    seq = lambda w: pl.BlockSpec((S, w), lambda b, h, j, *_: (b, h))
    blk = lambda w: pl.BlockSpec((tk, w), lambda b, h, j, *_: (b * nk + j, h))
    return pl.pallas_call(
        body, name="attn_bwd",
        grid_spec=pltpu.PrefetchScalarGridSpec(
            num_scalar_prefetch=4, grid=(nb, ng, nk),
            in_specs=[seq(hg * HEAD_PAD), blk(hg * HEAD_PAD), blk(hg * 128),
                      seq(hg * 128), seq(hg * 128), seq(hg * 128),
                      pl.BlockSpec((S, 1), lambda b, h, j, *_: (b, 0)),
                      pl.BlockSpec((None, 1, tk), lambda b, h, j, *_: (b, 0, j))],
            out_specs=(seq(hg * HEAD_PAD), blk(hg * HEAD_PAD), blk(hg * 128))),
        out_shape=(jax.ShapeDtypeStruct((T, HEADS * HEAD_PAD), F32),
                   jax.ShapeDtypeStruct((T, HEADS * HEAD_PAD), F32),
                   jax.ShapeDtypeStruct((T, MLA_W), F32)),
        compiler_params=_cparams(3),
    )(*bounds, q, k, v, do, lse, delta, pos_col, pos_row)


def _bwd_proj(dq, dk, dv, xq, xkv, x, dz, dga, dgb, dpc, rc, rsa, rsb, w_uq_e, w_ukv, w_in_e, gq, gkv, S, tm):
    T = x.shape[0]
    tps = S // tm
    hb = tm // HALO
    n_steps = T // tm

    def body(dq_ref, dk_ref, dv_ref, xq_ref, xkv_ref, x_ref, dz_ref, dga_ref, dgb_ref, dpc_ref, dph_ref,
             c_ref, sa_ref, sb_ref, wuq_ref, wukv_ref, win_ref, gq_ref, gkv_ref,
             dx_ref, dwin_hbm, dwuq_hbm, dwukv_hbm, dgq_ref, dgkv_ref,
             acc_win, acc_wuq, acc_wukv):
        i = pl.program_id(0)

        @pl.when(i == 0)
        def _():
            acc_win[...] = jnp.zeros_like(acc_win)
            acc_wuq[...] = jnp.zeros_like(acc_wuq)
            acc_wukv[...] = jnp.zeros_like(acc_wukv)
            dgq_ref[...] = jnp.zeros_like(dgq_ref)
            dgkv_ref[...] = jnp.zeros_like(dgkv_ref)

        c, sa, sb = c_ref[...], sa_ref[...], sb_ref[...]
        dq_v = dq_ref[...] * SCALE
        dk_v = dk_ref[...] * LN2
        dv_v = dv_ref[...]
        dq_parts, dkv_parts = [], []
        dkr = jnp.zeros((tm, 128), F32)
        for hh in range(HEADS):
            b0 = hh * HEAD_PAD
            dq_parts.append(dq_v[:, b0:b0 + 128].astype(BF16))
            dq_parts.append(_rope(dq_v[:, b0 + 128:b0 + 256], c, sa, sb, -1.0).astype(BF16))
            dkv_parts.append(dk_v[:, b0:b0 + 128].astype(BF16))
            dkv_parts.append(dv_v[:, hh * 128:(hh + 1) * 128].astype(BF16))
            dkr = dkr + dk_v[:, b0 + 128:b0 + 256]
        dqp = jnp.concatenate(dq_parts, axis=1)
        dkvp = jnp.concatenate(dkv_parts, axis=1)
        dkrr = _rope(dkr, c, sa, sb, -1.0)

        def rms_bwd(xv, g, dyn, dg_ref):
            r = lax.rsqrt(jnp.mean(xv * xv, axis=-1, keepdims=True) + RMS_EPS)
            xhat = xv * r
            dg_ref[...] += jnp.sum(dyn * xhat, axis=0, keepdims=True)
            dxh = dyn * g
            return r * (dxh - xhat * jnp.mean(dxh * xhat, axis=-1, keepdims=True))

        xq_v = xq_ref[...]
        gq_v = gq_ref[...]
        rq = lax.rsqrt(jnp.mean(xq_v * xq_v, axis=-1, keepdims=True) + RMS_EPS)
        acc_wuq[...] += _dot_tn(((xq_v * rq) * gq_v).astype(BF16), dqp)
        dxq = rms_bwd(xq_v, gq_v, _dot_nt(dqp, wuq_ref[...]), dgq_ref)

        xkv_v = xkv_ref[...]
        gkv_v = gkv_ref[...]
        rkv = lax.rsqrt(jnp.mean(xkv_v * xkv_v, axis=-1, keepdims=True) + RMS_EPS)
        acc_wukv[...] += _dot_tn(((xkv_v * rkv) * gkv_v).astype(BF16), dkvp)
        dxkv = rms_bwd(xkv_v, gkv_v, _dot_nt(dkvp, wukv_ref[...]), dgkv_ref)

        seq_tile = i % tps
        tpos = seq_tile * tm + lax.broadcasted_iota(jnp.int32, (tm, 1), 0)
        dpc_v = dpc_ref[...]
        halo = jnp.where(seq_tile == tps - 1, 0.0, dph_ref[...])
        n = tm + HALO
        du = []
        for g in range(POOL_G):
            lanes = slice(g * POOL_GD, (g + 1) * POOL_GD)
            f = jnp.concatenate([dpc_v[:, lanes], halo[:, lanes]], axis=0)
            for st in range(g + 1):
                f = f + pltpu.roll(f, n - (1 << st), 0)
            cnt = jnp.minimum(tpos + 1, 2 << g).astype(F32)
            du.append((f[:tm, :] - dpc_v[:, lanes] * cnt).astype(BF16))

        dh = jnp.concatenate([dxq.astype(BF16), dxkv.astype(BF16), dkrr.astype(BF16), dga_ref[...]]
                             + du + [dgb_ref[...]], axis=1)
        dx_ref[...] = ALPHA * dz_ref[...] + _dot_nt(dh, win_ref[...])
        acc_win[...] += _dot_tn(x_ref[...].astype(BF16), dh)

        @pl.when(i == n_steps - 1)
        def _():
            pltpu.sync_copy(acc_win, dwin_hbm)
            pltpu.sync_copy(acc_wuq, dwuq_hbm)
            pltpu.sync_copy(acc_wukv, dwukv_hbm)

    row = lambda w: pl.BlockSpec((tm, w), lambda i: (i, 0))
    halo_spec = pl.BlockSpec((HALO, POOL_W), lambda i: (jnp.minimum((i + 1) * hb, T // HALO - 1), 0))
    return pl.pallas_call(
        body, name="bwd_proj", grid=(n_steps,),
        in_specs=[row(1024), row(1024), row(512), row(512), row(256), row(D_MODEL), row(D_MODEL),
                  row(512), row(512), row(512), halo_spec, row(128), row(128), row(128),
                  _full(w_uq_e.shape), _full(w_ukv.shape), _full(w_in_e.shape), _full(gq.shape), _full(gkv.shape)],
        out_specs=(row(D_MODEL), ANY, ANY, ANY, _full((1, Q_LORA)), _full((1, KV_LORA))),
        out_shape=(jax.ShapeDtypeStruct((T, D_MODEL), F32),
                   jax.ShapeDtypeStruct((D_MODEL, IN_EXT), F32),
                   jax.ShapeDtypeStruct((Q_LORA, HEADS * HEAD_PAD), F32),
                   jax.ShapeDtypeStruct((KV_LORA, 1024), F32),
                   jax.ShapeDtypeStruct((1, Q_LORA), F32), jax.ShapeDtypeStruct((1, KV_LORA), F32)),
        scratch_shapes=[pltpu.VMEM((D_MODEL, IN_EXT), F32), pltpu.VMEM((Q_LORA, HEADS * HEAD_PAD), F32),
                        pltpu.VMEM((KV_LORA, 1024), F32)],
        compiler_params=_cparams(1),
    )(dq, dk, dv, xq, xkv, x, dz, dga, dgb, dpc, dpc, rc, rsa, rsb, w_uq_e, w_ukv, w_in_e, gq, gkv)


def kernel(x, positions, w_in, q_norm_g, w_uq, kv_norm_g, w_ukv, pool_w, pool_scale, w_out, ln_g, ln_b, loss_target, m_w_in, m_q_norm_g, m_w_uq, m_kv_norm_g, m_w_ukv, m_pool_w, m_pool_scale, m_w_out, m_ln_g, m_ln_b, v_w_in, v_q_norm_g, v_w_uq, v_kv_norm_g, v_w_ukv, v_pool_w, v_pool_scale, v_w_out, v_ln_g, v_ln_b):
    nb, S, _ = x.shape
    T = nb * S
    tm = min(256, S)
    tq = min(512, S)
    tk = min(512, S)
    assert S % tm == 0 and tm % HALO == 0 and S % tq == 0 and S % tk == 0

    cx, cy, cc = lax.axis_index("x"), lax.axis_index("y"), lax.axis_index("c")
    me = 2 * cx + cy
    place_arr = jnp.stack([me, cc]).astype(jnp.int32)

    def own_slot(w):
        return lax.dynamic_update_slice(jnp.zeros((N_CHIPS,) + w.shape, BF16), w.astype(BF16)[None], (me, 0, 0))

    w_in_g, w_uq_g, w_ukv_g, w_out_g = _weight_gather([own_slot(w_in), own_slot(w_uq), own_slot(w_ukv), own_slot(w_out)])
    w_in_f = w_in_g.transpose(1, 0, 2).reshape(D_MODEL, IN_W)
    w_in_e = jnp.concatenate([w_in_f[:, :832], jnp.zeros((D_MODEL, 64), BF16), w_in_f[:, 832:]], axis=1)
    w_uq_e = jnp.pad(w_uq_g.transpose(1, 0, 2), ((0, 0), (0, 0), (0, 64))).reshape(Q_LORA, HEADS * HEAD_PAD)
    w_ukv_f = w_ukv_g.transpose(1, 0, 2).reshape(KV_LORA, 1024)
    w_out_f = w_out_g.reshape(D_MODEL, D_MODEL)
    pool_w_b = pool_w.astype(BF16)
    gq2 = q_norm_g.reshape(1, Q_LORA)
    gkv2 = kv_norm_g.reshape(1, KV_LORA)
    ps2 = pool_scale.reshape(1, POOL_W)

    half = ROPE // 2
    inv_freq = ROPE_THETA ** (-jnp.arange(half, dtype=F32) / half)
    ang = positions.astype(F32).reshape(T, 1) * inv_freq
    cos, sin = jnp.cos(ang), jnp.sin(ang)
    z32, z64 = jnp.zeros((T, 32), F32), jnp.zeros((T, 64), F32)
    rc = jnp.concatenate([cos, cos, z64], axis=1)
    rsa = jnp.concatenate([sin, z32, z64], axis=1)
    rsb = jnp.concatenate([z32, sin, z64], axis=1)
    pos_col = positions.reshape(T, 1)
    pos_row = positions.reshape(nb, 1, S)
    pos_q = positions.reshape(nb, S // tq, tq)
    pos_k = positions.reshape(nb, S // tk, tk)
    bounds = (jnp.min(pos_q, axis=2).reshape(-1), jnp.max(pos_q, axis=2).reshape(-1),
              jnp.min(pos_k, axis=2).reshape(-1), jnp.max(pos_k, axis=2).reshape(-1))

    xf = x.reshape(T, D_MODEL)
    tgt = loss_target.reshape(T, D_MODEL)

    xq, xkv, ga, u, gb, q, k, v = _fwd_proj(xf, w_in_e, w_uq_e, w_ukv_f, gq2, gkv2, rc, rsa, rsb, tm)
    o, lse = _attn_fwd(q, k, v, pos_col, pos_row, bounds, nb, S, tq, tk)

    (dz, do, delta, dga, dgb, dpc, d_w_out, d_pool_w, d_pool_scale, d_ln_g, d_ln_b, loss_part) = _mid(
        xf, tgt, o, ga, u, gb, w_out_f, pool_w_b, ps2, ln_g, ln_b, S, tm)

    dq, dk, dv = _attn_bwd(q, k, v, do, lse, delta, pos_col, pos_row, bounds, nb, S, tq, tk)
    dx, d_w_in_e, d_w_uq_e, d_w_ukv, d_gq, d_gkv = _bwd_proj(
        dq, dk, dv, xq, xkv, xf, dz, dga, dgb, dpc, rc, rsa, rsb, w_uq_e, w_ukv_f, w_in_e, gq2, gkv2, S, tm)
    grad_x = dx.reshape(nb, S, D_MODEL)

    g_in = jnp.concatenate([d_w_in_e[:, :832], d_w_in_e[:, 896:]], axis=1).reshape(D_MODEL, N_CHIPS, 592).transpose(1, 0, 2)
    g_uq = d_w_uq_e.reshape(Q_LORA, HEADS, HEAD_PAD)[:, :, :NOPE + ROPE].transpose(1, 0, 2)
    g_ukv = d_w_ukv.reshape(KV_LORA, N_CHIPS, 256).transpose(1, 0, 2)
    g_out = d_w_out.reshape(N_CHIPS, 256, D_MODEL)
    gs = [g_in, g_uq, g_ukv, g_out]
    from_sibling = _grad_to_sibling(gs)
    chip_sums, own_sums = _add_sibling_half(gs, from_sibling, place_arr)
    from_chips = _grad_to_chips(chip_sums)
    g_big = _halves_exchange(_add_chip_parts(own_sums, from_chips, place_arr))

    pw_sum, vec_sum = _small_allreduce(d_pool_w, d_ln_g, d_ln_b, d_pool_scale, d_gq, d_gkv, loss_part)

    big = _adamw_big(g_big, [w_in, w_uq, w_ukv, w_out], [m_w_in, m_w_uq, m_w_ukv, m_w_out],
                     [v_w_in, v_w_uq, v_w_ukv, v_w_out])
    two_d = lambda a: a.reshape(-1, a.shape[-1])
    small_names = lambda pw, lg, lb, ps, gq, gkv: [two_d(pw), lg, lb, ps.reshape(1, -1), gq.reshape(1, -1), gkv.reshape(1, -1)]
    small, loss_row = _adamw_small(
        pw_sum, vec_sum,
        small_names(pool_w, ln_g, ln_b, pool_scale, q_norm_g, kv_norm_g),
        small_names(m_pool_w, m_ln_g, m_ln_b, m_pool_scale, m_q_norm_g, m_kv_norm_g),
        small_names(v_pool_w, v_ln_g, v_ln_b, v_pool_scale, v_q_norm_g, v_kv_norm_g))
    loss = loss_row[0, 0]

    def leaves(kind):
        b = [g_big[t] if kind == 0 else big[t][kind - 1] for t in range(N_BIG)]
        s = [small[t][kind] for t in range(6)]
        return (b[0], s[4].reshape(Q_LORA), b[1], s[5].reshape(KV_LORA), b[2],
                s[0].reshape(POOL_G, POOL_GD, POOL_GD), s[3].reshape(POOL_W), b[3], s[1], s[2])

    return (loss, grad_x) + leaves(0) + leaves(1) + leaves(2) + leaves(3)
```

```python
import functools

import jax
import jax.numpy as jnp
from jax import lax
from jax.experimental import pallas as pl
from jax.experimental.pallas import tpu as pltpu

F32 = jnp.float32
BF16 = jnp.bfloat16
MESH = pl.DeviceIdType.MESH

HEADS = 4
NOPE = 128
ROPE = 64
HEAD_PAD = 256
Q_LORA = 512
KV_LORA = 256
MLA_W = 512
POOL_W = 512
POOL_G = 4
POOL_GD = 128
D_MODEL = 1024
IN_W = 2368
IN_EXT = 2432
ROPE_THETA = 10000.0
RMS_EPS = 1e-6
LN_EPS = 1e-5
ALPHA = 2.0 ** 0.25
SCALE = 192.0 ** -0.5
LOG2E = 1.4426950408889634
LN2 = 0.6931471805599453
QSCALE = SCALE * LOG2E
NEG = float(jnp.finfo(jnp.float32).min)
HEAD_GROUP = 2
HALO = 16

ADAM_LR = 0.001
ADAM_B1 = 0.9
ADAM_B2 = 0.999
ADAM_EPS = 1e-08
ADAM_WD = 0.01
ADAM_STEP = 10

N_CHIPS = 4
N_BIG = 4
VEC_ROWS = 16
VEC_HALF = VEC_ROWS // 2

VMEM_LIMIT = 56 * 1024 * 1024


def _cparams(n_grid_dims=0, **kw):
    sem = ("arbitrary",) * n_grid_dims if n_grid_dims else None
    return pltpu.CompilerParams(dimension_semantics=sem, vmem_limit_bytes=VMEM_LIMIT, **kw)


def _full(shape):
    nd = len(shape)
    return pl.BlockSpec(shape, lambda *_: (0,) * nd)


def _dot(a, b):
    return jnp.dot(a, b, preferred_element_type=F32)


def _dot_nt(a, b):
    return lax.dot_general(a, b, (((1,), (1,)), ((), ())), preferred_element_type=F32)


def _dot_tn(a, b):
    return lax.dot_general(a, b, (((0,), (0,)), ((), ())), preferred_element_type=F32)


def _rope(g, c, sa, sb, sign):
    return g * c + sign * (pltpu.roll(g, 32, 1) * sb - pltpu.roll(g, 96, 1) * sa)


def _place():
    x, y, c = lax.axis_index("x"), lax.axis_index("y"), lax.axis_index("c")
    chips = [(1 - x, y), (x, 1 - y), (1 - x, 1 - y)]
    return x, y, c, chips


def _half_cols(ref, half_index):
    hc = ref.shape[-1] // 2
    lead = tuple(pl.ds(0, n) for n in ref.shape[:-1])
    return ref.at[lead + (pl.ds(half_index * hc, hc),)]


ANY = pl.BlockSpec(memory_space=pl.ANY)


def _weight_gather(slots, valid_rows):
    n = len(slots)

    def body(*refs):
        outs = refs[n:2 * n]
        send_sems, recv_sems = refs[2 * n:]
        x, y, c, chips = _place()
        me = 2 * x + y

        def copy(t, k, chip_idx, half, to):
            hc = slots[t].shape[2] // 2
            blk = outs[t].at[chip_idx, pl.ds(0, valid_rows[t]), pl.ds(half * hc, hc)]
            return pltpu.make_async_remote_copy(
                src_ref=blk, dst_ref=blk, send_sem=send_sems.at[6 * t + k], recv_sem=recv_sems.at[6 * t + k],
                device_id=to, device_id_type=MESH)

        first = [copy(t, j, me, c, (cx, cy, c)) for t in range(n) for j, (cx, cy) in enumerate(chips)]
        for cp in first:
            cp.start()
        passed = []
        for j, (cx, cy) in enumerate(chips):
            for t in range(n):
                copy(t, j, 2 * cx + cy, c, (x, y, c)).wait_recv()
                fwd = copy(t, 3 + j, 2 * cx + cy, c, (x, y, 1 - c))
                fwd.start()
                passed.append(fwd)
        for j, (cx, cy) in enumerate(chips):
            for t in range(n):
                copy(t, 3 + j, 2 * cx + cy, 1 - c, (x, y, c)).wait_recv()
        for cp in first + passed:
            cp.wait_send()

    return pl.pallas_call(
        body, name="weight_gather",
        out_shape=tuple(jax.ShapeDtypeStruct(a.shape, a.dtype) for a in slots),
        in_specs=[ANY] * n, out_specs=(ANY,) * n, input_output_aliases={t: t for t in range(n)},
        scratch_shapes=[pltpu.SemaphoreType.DMA((6 * n,)), pltpu.SemaphoreType.DMA((6 * n,))],
    )(*slots)


def _grad_to_sibling(gs):
    n = len(gs)

    def body(*refs):
        g_refs, r_refs = refs[:n], refs[n:2 * n]
        send_sems, recv_sems = refs[2 * n:]
        x, y, c, _ = _place()
        cps = []
        for t in range(n):
            cp = pltpu.make_async_remote_copy(
                src_ref=_half_cols(g_refs[t], 1 - c), dst_ref=r_refs[t], send_sem=send_sems.at[t], recv_sem=recv_sems.at[t],
                device_id=(x, y, 1 - c), device_id_type=MESH)
            cp.start()
            cps.append(cp)
        for cp in cps:
            cp.wait()

    return pl.pallas_call(
        body, name="grad_to_sibling",
        out_shape=tuple(jax.ShapeDtypeStruct((N_CHIPS, g.shape[1], g.shape[2] // 2), F32) for g in gs),
        in_specs=[ANY] * n, out_specs=(ANY,) * n,
        scratch_shapes=[pltpu.SemaphoreType.DMA((n,)), pltpu.SemaphoreType.DMA((n,))],
    )(*gs)


def _grad_to_chips(ss):
    n = len(ss)

    def body(*refs):
        s_refs, r_refs = refs[:n], refs[n:2 * n]
        send_sems, recv_sems = refs[2 * n:]
        x, y, c, chips = _place()
        cps = []
        for t in range(n):
            for j, (cx, cy) in enumerate(chips):
                cp = pltpu.make_async_remote_copy(
                    src_ref=s_refs[t].at[2 * cx + cy], dst_ref=r_refs[t].at[j],
                    send_sem=send_sems.at[3 * t + j], recv_sem=recv_sems.at[3 * t + j],
                    device_id=(cx, cy, c), device_id_type=MESH)
                cp.start()
                cps.append(cp)
        for cp in cps:
            cp.wait()

    return pl.pallas_call(
        body, name="grad_to_chips",
        out_shape=tuple(jax.ShapeDtypeStruct((3,) + s.shape[1:], s.dtype) for s in ss),
        in_specs=[ANY] * n, out_specs=(ANY,) * n,
        scratch_shapes=[pltpu.SemaphoreType.DMA((3 * n,)), pltpu.SemaphoreType.DMA((3 * n,))],
    )(*ss)


def _halves_exchange(fs):
    n = len(fs)

    def body(*refs):
        o_refs = refs[n:2 * n]
        send_sems, recv_sems = refs[2 * n:]
        x, y, c, _ = _place()
        sib = (x, y, 1 - c)
        cps = []
        for t in range(n):
            mine = _half_cols(o_refs[t], c)
            cp = pltpu.make_async_remote_copy(
                src_ref=mine, dst_ref=mine, send_sem=send_sems.at[t], recv_sem=recv_sems.at[t],
                device_id=sib, device_id_type=MESH)
            cp.start()
            cps.append(cp)
        for t in range(n):
            theirs = _half_cols(o_refs[t], 1 - c)
            pltpu.make_async_remote_copy(
                src_ref=theirs, dst_ref=theirs, send_sem=send_sems.at[t], recv_sem=recv_sems.at[t],
                device_id=sib, device_id_type=MESH).wait_recv()
        for cp in cps:
            cp.wait_send()

    return pl.pallas_call(
        body, name="halves_exchange",
        out_shape=tuple(jax.ShapeDtypeStruct(f.shape, f.dtype) for f in fs),
        in_specs=[ANY] * n, out_specs=(ANY,) * n, input_output_aliases={t: t for t in range(n)},
        scratch_shapes=[pltpu.SemaphoreType.DMA((n,)), pltpu.SemaphoreType.DMA((n,))],
    )(*fs)


def _small_allreduce(d_pool_w, d_ln_g, d_ln_b, d_ps, d_gq, d_gkv, loss_part):
    pw_rows = POOL_G * POOL_GD

    def body(pw_in, lng_in, lnb_in, ps_in, gq_in, gkv_in, loss_in, pw_out, vec_out,
             vec_in, pw_sib, vec_sib, pw_sum, vec_sum, pw_chip, vec_chip, send_sems, recv_sems):
        x, y, c, chips = _place()
        sib = (x, y, 1 - c)
        vec_in[...] = jnp.zeros_like(vec_in)
        vec_in[0:1, :] = lng_in[...]
        vec_in[1:2, :] = lnb_in[...]
        vec_in[2:3, 0:POOL_W] = ps_in[...]
        vec_in[3:4, 0:Q_LORA] = gq_in[...]
        vec_in[8:9, 0:KV_LORA] = gkv_in[...]
        vec_in[9:10, 0:128] = loss_in[...]

        def rdma(k, src, dst, to):
            return pltpu.make_async_remote_copy(src_ref=src, dst_ref=dst, send_sem=send_sems.at[k],
                                                recv_sem=recv_sems.at[k], device_id=to, device_id_type=MESH)

        a = [rdma(0, pw_in, pw_sib, sib), rdma(1, vec_in, vec_sib, sib)]
        for cp in a:
            cp.start()
        for cp in a:
            cp.wait()
        pw_sum[...] = pw_in[...] + pw_sib[...]
        vec_sum[...] = vec_in[...] + vec_sib[...]

        bufs = [(pw_sum, pw_chip, pw_out, pw_rows // 2), (vec_sum, vec_chip, vec_out, VEC_HALF)]
        cps = []
        for t, (sm, chip_buf, _, hr) in enumerate(bufs):
            rows = pl.ds(pl.multiple_of(c * hr, 8), hr)
            for j, (cx, cy) in enumerate(chips):
                cp = rdma(2 + 3 * t + j, sm.at[rows], chip_buf.at[j], (cx, cy, c))
                cp.start()
                cps.append(cp)
        for cp in cps:
            cp.wait()
        last = []
        for t, (sm, chip_buf, out, hr) in enumerate(bufs):
            rows = pl.ds(pl.multiple_of(c * hr, 8), hr)
            other = pl.ds(pl.multiple_of((1 - c) * hr, 8), hr)
            out[rows, :] = (sm[rows, :] + chip_buf[0]) + (chip_buf[1] + chip_buf[2])
            cp = rdma(8 + t, out.at[rows], out.at[rows], sib)
            cp.start()
            last.append((cp, rdma(8 + t, out.at[other], out.at[other], sib)))
        for cp, recv in last:
            recv.wait_recv()
            cp.wait_send()

    vm = pl.BlockSpec(memory_space=pltpu.VMEM)
    vec_shape = (VEC_ROWS, D_MODEL)
    return pl.pallas_call(
        body, name="small_allreduce",
        out_shape=(jax.ShapeDtypeStruct((pw_rows, POOL_GD), F32), jax.ShapeDtypeStruct(vec_shape, F32)),
        in_specs=[vm] * 7, out_specs=(vm, vm),
        scratch_shapes=[pltpu.VMEM(vec_shape, F32), pltpu.VMEM((pw_rows, POOL_GD), F32), pltpu.VMEM(vec_shape, F32),
                        pltpu.VMEM((pw_rows, POOL_GD), F32), pltpu.VMEM(vec_shape, F32),
                        pltpu.VMEM((3, pw_rows // 2, POOL_GD), F32), pltpu.VMEM((3, VEC_HALF, D_MODEL), F32),
                        pltpu.SemaphoreType.DMA((10,)), pltpu.SemaphoreType.DMA((10,))],
    )(d_pool_w.reshape(pw_rows, POOL_GD), d_ln_g, d_ln_b, d_ps, d_gq, d_gkv, loss_part)


def _add_sibling_half(gs, rs, place_arr):
    n = len(gs)

    def body(place_ref, *refs):
        k = pl.program_id(0)
        for t in range(n):
            total = refs[t][...] + refs[n + t][...]
            refs[2 * n + 2 * t][...] = total.astype(BF16)

            @pl.when(k == place_ref[0])
            def _():
                refs[2 * n + 2 * t + 1][...] = total

    in_specs, out_specs, out_shape = [], [], []
    for g in gs:
        in_specs.append(pl.BlockSpec((None, g.shape[1], g.shape[2] // 2), lambda k, p: (k, 0, p[1])))
    for r in rs:
        blk = pl.BlockSpec((None,) + r.shape[1:], lambda k, p: (k, 0, 0))
        in_specs.append(blk)
        out_specs += [blk, pl.BlockSpec(r.shape[1:], lambda k, p: (0, 0))]
        out_shape += [jax.ShapeDtypeStruct(r.shape, BF16), jax.ShapeDtypeStruct(r.shape[1:], F32)]
    outs = pl.pallas_call(
        body, name="add_sibling_half", out_shape=tuple(out_shape),
        grid_spec=pltpu.PrefetchScalarGridSpec(num_scalar_prefetch=1, grid=(N_CHIPS,),
                                               in_specs=in_specs, out_specs=tuple(out_specs)),
        compiler_params=_cparams(1),
    )(place_arr, *gs, *rs)
    return list(outs[0::2]), list(outs[1::2])


def _add_chip_parts(owns, rs, place_arr):
    n = len(owns)

    def body(place_ref, *refs):
        for t in range(n):
            r_ref = refs[n + t]
            refs[2 * n + t][...] = ((refs[t][...] + r_ref[0].astype(F32))
                                    + (r_ref[1].astype(F32) + r_ref[2].astype(F32)))

    in_specs, out_specs = [], []
    for o in owns:
        in_specs.append(pl.BlockSpec(o.shape, lambda i, p: (0, 0)))
    for r in rs:
        in_specs.append(pl.BlockSpec(r.shape, lambda i, p: (0, 0, 0)))
        out_specs.append(pl.BlockSpec(r.shape[1:], lambda i, p: (0, p[1])))
    return pl.pallas_call(
        body, name="add_chip_parts",
        out_shape=tuple(jax.ShapeDtypeStruct((o.shape[0], 2 * o.shape[1]), F32) for o in owns),
        grid_spec=pltpu.PrefetchScalarGridSpec(num_scalar_prefetch=1, grid=(1,),
                                               in_specs=in_specs, out_specs=tuple(out_specs)),
        compiler_params=_cparams(1),
    )(place_arr, *owns, *rs)


def _adamw_math(g, w, m, v):
    nm = ADAM_B1 * m + (1.0 - ADAM_B1) * g
    nv = ADAM_B2 * v + (1.0 - ADAM_B2) * (g * g)
    m_hat = nm / (1.0 - ADAM_B1 ** ADAM_STEP)
    v_hat = nv / (1.0 - ADAM_B2 ** ADAM_STEP)
    return -ADAM_LR * (m_hat / (jnp.sqrt(v_hat) + ADAM_EPS) + ADAM_WD * w), nm, nv


ADAM_STEPS = 8


def _adamw_big(gs, ws, ms, vs):
    n = len(gs)

    def body(*refs):
        for t in range(n):
            d, nm, nv = _adamw_math(refs[t][...], refs[n + t][...], refs[2 * n + t][...], refs[3 * n + t][...])
            refs[4 * n + 3 * t][...] = d
            refs[4 * n + 3 * t + 1][...] = nm
            refs[4 * n + 3 * t + 2][...] = nv

    def tile_spec(shape):
        rows, cols = shape
        if rows % (8 * ADAM_STEPS) == 0:
            return pl.BlockSpec((rows // ADAM_STEPS, cols), lambda i: (i, 0))
        return pl.BlockSpec((rows, cols // ADAM_STEPS), lambda i: (0, i))

    specs = [tile_spec(g.shape) for g in gs]
    out_specs, out_shape = [], []
    for t in range(n):
        out_specs += [specs[t]] * 3
        out_shape += [jax.ShapeDtypeStruct(gs[t].shape, F32)] * 3
    outs = pl.pallas_call(
        body, name="adamw_big", grid=(ADAM_STEPS,),
        in_specs=specs * 4, out_specs=tuple(out_specs), out_shape=tuple(out_shape),
        compiler_params=_cparams(1),
    )(*gs, *ws, *ms, *vs)
    return [outs[3 * t: 3 * t + 3] for t in range(n)]


def _adamw_small(pw_sum, vec_sum, ws, ms, vs):
    rows = (None, 0, 1, 2, 3, 8)
    n = len(ws)

    def body(pw_ref, vec_ref, *refs):
        outs = refs[3 * n:]
        for t in range(n):
            w_ref, m_ref, v_ref = refs[t], refs[n + t], refs[2 * n + t]
            if rows[t] is None:
                g = pw_ref[...]
            else:
                g = vec_ref[rows[t]:rows[t] + 1, 0:w_ref.shape[1]]
            d, nm, nv = _adamw_math(g, w_ref[...], m_ref[...], v_ref[...])
            outs[4 * t][...] = g
            outs[4 * t + 1][...] = d
            outs[4 * t + 2][...] = nm
            outs[4 * t + 3][...] = nv
        outs[4 * n][...] = vec_ref[9:10, 0:128]

    vm = pl.BlockSpec(memory_space=pltpu.VMEM)
    out_shape = []
    for w in ws:
        out_shape += [jax.ShapeDtypeStruct(w.shape, F32)] * 4
    out_shape.append(jax.ShapeDtypeStruct((1, 128), F32))
    outs = pl.pallas_call(
        body, name="adamw_small", in_specs=[vm] * (2 + 3 * n), out_specs=(vm,) * (4 * n + 1),
        out_shape=tuple(out_shape),
    )(pw_sum, vec_sum, *ws, *ms, *vs)
    return [outs[4 * t: 4 * t + 4] for t in range(n)], outs[4 * n]


def _fwd_proj(x, w_in_t, w_uq_t, w_ukv, gq, gkv, rc, rsa, rsb, tm):
    T = x.shape[0]

    def body(x_ref, win_ref, wuq_ref, wukv_ref, gq_ref, gkv_ref, c_ref, sa_ref, sb_ref,
             xq_ref, xkv_ref, ga_ref, u_ref, gb_ref, q_ref, k_ref, v_ref):
        h = _dot_nt(x_ref[...].astype(BF16), win_ref[...])
        xq = h[:, 0:512]
        xkv = h[:, 512:768]
        xq_ref[...] = xq
        xkv_ref[...] = xkv
        ga_ref[...] = h[:, 896:1408]
        u_ref[...] = h[:, 1408:1920]
        gb_ref[...] = h[:, 1920:2432]
        c, sa, sb = c_ref[...], sa_ref[...], sb_ref[...]
        rq = lax.rsqrt(jnp.mean(xq * xq, axis=-1, keepdims=True) + RMS_EPS)
        q = _dot_nt(((xq * rq) * gq_ref[...]).astype(BF16), wuq_ref[...]) * QSCALE
        rkv = lax.rsqrt(jnp.mean(xkv * xkv, axis=-1, keepdims=True) + RMS_EPS)
        kv = _dot(((xkv * rkv) * gkv_ref[...]).astype(BF16), wukv_ref[...])
        kr = _rope(h[:, 768:896], c, sa, sb, 1.0).astype(BF16)
        for hh in range(HEADS):
            b0 = hh * HEAD_PAD
            q_ref[:, b0:b0 + 128] = q[:, b0:b0 + 128].astype(BF16)
            q_ref[:, b0 + 128:b0 + 256] = _rope(q[:, b0 + 128:b0 + 256], c, sa, sb, 1.0).astype(BF16)
            k_ref[:, b0:b0 + 128] = kv[:, b0:b0 + 128].astype(BF16)
            k_ref[:, b0 + 128:b0 + 256] = kr
            v_ref[:, hh * 128:(hh + 1) * 128] = kv[:, b0 + 128:b0 + 256].astype(BF16)

    row = lambda w: pl.BlockSpec((tm, w), lambda i: (i, 0))
    f = lambda w, dt: jax.ShapeDtypeStruct((T, w), dt)
    return pl.pallas_call(
        body, name="fwd_proj", grid=(T // tm,),
        in_specs=[row(D_MODEL), _full(w_in_t.shape), _full(w_uq_t.shape), _full(w_ukv.shape),
                  _full(gq.shape), _full(gkv.shape), row(128), row(128), row(128)],
        out_specs=(row(512), row(256), row(512), row(512), row(512), row(1024), row(1024), row(512)),
        out_shape=(f(512, F32), f(256, F32), f(512, F32), f(512, F32), f(512, F32),
                   f(1024, BF16), f(1024, BF16), f(512, BF16)),
        compiler_params=_cparams(1),
    )(x, w_in_t, w_uq_t, w_ukv, gq, gkv, rc, rsa, rsb)


def _attn_fwd(q, k, v, pos_col, pos_row, bounds, nb, S, tq, tk):
    T = q.shape[0]
    nq, nk = S // tq, S // tk
    reps = tk // 128
    hg = HEAD_GROUP

    def body(qmin_ref, qmax_ref, kmin_ref, kmax_ref, q_ref, k_ref, v_ref, pc_ref, pr_ref, o_ref, lse_ref,
             m_sc, l_sc, acc_sc):
        b, i = pl.program_id(0), pl.program_id(2)
        m_sc[...] = jnp.full(m_sc.shape, NEG, F32)
        l_sc[...] = jnp.zeros_like(l_sc)
        acc_sc[...] = jnp.zeros_like(acc_sc)
        q_lo = qmin_ref[b * nq + i]
        q_hi = qmax_ref[b * nq + i]

        def tile(j, masked):
            off = pl.multiple_of(j * tk, tk)
            if masked:
                keep = pc_ref[...] >= pr_ref[pl.ds(j, 1), :]
            logits = []
            for g in range(hg):
                qk = slice(g * HEAD_PAD, (g + 1) * HEAD_PAD)
                s = _dot_nt(q_ref[:, qk], k_ref[pl.ds(off, tk), qk])
                if masked:
                    s = jnp.where(keep, s, NEG)
                logits.append(s)
            probs = []
            for g in range(hg):
                hv = slice(g * 128, (g + 1) * 128)
                s = logits[g]
                m_prev = m_sc[:, hv]
                m_new = jnp.maximum(m_prev, jnp.max(s, axis=1, keepdims=True))
                p = jnp.exp2(s - jnp.concatenate([m_new] * reps, axis=1))
                a = jnp.exp2(m_prev - m_new)
                l_sc[:, hv] = a * l_sc[:, hv] + jnp.sum(p, axis=1, keepdims=True)
                m_sc[:, hv] = m_new
                probs.append((p.astype(BF16), a))
            for g in range(hg):
                hv = slice(g * 128, (g + 1) * 128)
                p, a = probs[g]
                acc_sc[:, hv] = a * acc_sc[:, hv] + _dot(p, v_ref[pl.ds(off, tk), hv])

        def step(j, carry):
            visible = kmin_ref[b * nk + j] <= q_hi
            clear = q_lo >= kmax_ref[b * nk + j]

            @pl.when(jnp.logical_and(visible, clear))
            def _():
                tile(j, False)

            @pl.when(jnp.logical_and(visible, jnp.logical_not(clear)))
            def _():
                tile(j, True)
            return carry

        lax.fori_loop(0, nk, step, 0)
        l = l_sc[...]
        o_ref[...] = acc_sc[...] / l
        lse_ref[...] = m_sc[...] + jnp.log2(l)

    ng = HEADS // hg
    stat = pltpu.VMEM((tq, hg * 128), F32)
    return pl.pallas_call(
        body, name="attn_fwd",
        grid_spec=pltpu.PrefetchScalarGridSpec(
            num_scalar_prefetch=4, grid=(nb, ng, nq),
            in_specs=[pl.BlockSpec((tq, hg * HEAD_PAD), lambda b, h, i, *_: (b * nq + i, h)),
                      pl.BlockSpec((S, hg * HEAD_PAD), lambda b, h, i, *_: (b, h)),
                      pl.BlockSpec((S, hg * 128), lambda b, h, i, *_: (b, h)),
                      pl.BlockSpec((tq, 1), lambda b, h, i, *_: (b * nq + i, 0)),
                      pl.BlockSpec((None, nk, tk), lambda b, h, i, *_: (b, 0, 0))],
            out_specs=(pl.BlockSpec((tq, hg * 128), lambda b, h, i, *_: (b * nq + i, h)),
                       pl.BlockSpec((tq, hg * 128), lambda b, h, i, *_: (b * nq + i, h))),
            scratch_shapes=[stat, stat, stat]),
        out_shape=(jax.ShapeDtypeStruct((T, MLA_W), F32), jax.ShapeDtypeStruct((T, MLA_W), F32)),
        compiler_params=_cparams(3),
    )(*bounds, q, k, v, pos_col, pos_row.reshape(nb, nk, tk))


def _mid(x, tgt, o, ga, u, gb, w_out, pool_w, pool_scale, ln_g, ln_b, S, tm):
    T = x.shape[0]
    tps = S // tm
    hb = tm // HALO

    def body(x_ref, tgt_ref, o_ref, ga_ref, u_ref, uh_ref, gb_ref, wout_ref, pw_ref,
             ps_ref, lng_ref, lnb_ref,
             dz_ref, do_ref, delta_ref, dga_ref, dgb_ref, dpc_ref,
             dwout_ref, dpw_ref, dps_ref, dlng_ref, dlnb_ref, loss_ref):
        i = pl.program_id(0)

        @pl.when(i == 0)
        def _():
            dwout_ref[...] = jnp.zeros_like(dwout_ref)
            dpw_ref[...] = jnp.zeros_like(dpw_ref)
            dps_ref[...] = jnp.zeros_like(dps_ref)
            dlng_ref[...] = jnp.zeros_like(dlng_ref)
            dlnb_ref[...] = jnp.zeros_like(dlnb_ref)
            loss_ref[...] = jnp.zeros_like(loss_ref)

        seq_tile = i % tps
        tpos = seq_tile * tm + lax.broadcasted_iota(jnp.int32, (tm, 1), 0)
        ga_v = ga_ref[...]
        sig_a = jax.nn.sigmoid(ga_v)
        silu_a = ga_v * sig_a
        o_v = o_ref[...]
        ya = o_v * silu_a

        u_v = u_ref[...]
        halo = jnp.where(seq_tile == 0, 0.0, uh_ref[...])
        pooled, cnts, mixed = [], [], []
        for g in range(POOL_G):
            lanes = slice(g * POOL_GD, (g + 1) * POOL_GD)
            w = jnp.concatenate([halo[:, lanes], u_v[:, lanes]], axis=0)
            for st in range(g + 1):
                w = w + pltpu.roll(w, 1 << st, 0)
            cnt = jnp.minimum(tpos + 1, 2 << g).astype(F32)
            pg = (w[HALO:, :] / cnt - u_v[:, lanes]).astype(BF16)
            pooled.append(pg)
            cnts.append(cnt)
            mixed.append(_dot(pg, pw_ref[g]))
        mixed = jnp.concatenate(mixed, axis=1)
        ps = ps_ref[...]
        ybp = mixed * ps
        gb_v = gb_ref[...]
        sig_b = jax.nn.sigmoid(gb_v)
        silu_b = gb_v * sig_b
        yb = ybp * silu_b

        cat = jnp.concatenate([ya, yb], axis=1).astype(BF16)
        z = ALPHA * x_ref[...] + _dot(cat, wout_ref[...])
        mu = jnp.mean(z, axis=-1, keepdims=True)
        zc = z - mu
        rstd = lax.rsqrt(jnp.mean(zc * zc, axis=-1, keepdims=True) + LN_EPS)
        zhat = zc * rstd
        lng = lng_ref[...]
        err = (zhat * lng + lnb_ref[...]) - tgt_ref[...]
        row_loss = jnp.sum(err * err, axis=1, keepdims=True)
        loss_ref[...] += jnp.broadcast_to(jnp.sum(row_loss, axis=0, keepdims=True) * (0.5 / D_MODEL), (1, 128))
        dy = err * (1.0 / D_MODEL)
        dlng_ref[...] += jnp.sum(dy * zhat, axis=0, keepdims=True)
        dlnb_ref[...] += jnp.sum(dy, axis=0, keepdims=True)
        dzh = dy * lng
        dz = rstd * (dzh - jnp.mean(dzh, axis=-1, keepdims=True)
                     - zhat * jnp.mean(dzh * zhat, axis=-1, keepdims=True))
        dz_ref[...] = dz
        dzb = dz.astype(BF16)
        dwout_ref[...] += _dot_tn(cat, dzb)
        dcat = _dot_nt(dzb, wout_ref[...])
        dya = dcat[:, :MLA_W]
        dyb = dcat[:, MLA_W:]

        do = dya * silu_a
        do_ref[...] = do.astype(BF16)
        prod = do * o_v
        for hh in range(HEADS):
            lanes = slice(hh * 128, (hh + 1) * 128)
            delta_ref[:, lanes] = jnp.broadcast_to(jnp.sum(prod[:, lanes], axis=1, keepdims=True), (tm, 128))
        dga_ref[...] = (dya * o_v * (sig_a * (1.0 + ga_v * (1.0 - sig_a)))).astype(BF16)
        dgb_ref[...] = (dyb * ybp * (sig_b * (1.0 + gb_v * (1.0 - sig_b)))).astype(BF16)
        dybp = dyb * silu_b
        dps_ref[...] += jnp.sum(dybp * mixed, axis=0, keepdims=True)
        dmixed = (dybp * ps).astype(BF16)
        for g in range(POOL_G):
            lanes = slice(g * POOL_GD, (g + 1) * POOL_GD)
            dpw_ref[g] += _dot_tn(pooled[g], dmixed[:, lanes])
            dpc_ref[:, lanes] = _dot_nt(dmixed[:, lanes], pw_ref[g]) / cnts[g]

    row = lambda w: pl.BlockSpec((tm, w), lambda i: (i, 0))
    f = lambda w, dt: jax.ShapeDtypeStruct((T, w), dt)
    halo_spec = pl.BlockSpec((HALO, POOL_W), lambda i: (jnp.maximum(i * hb - 1, 0), 0))
    return pl.pallas_call(
        body, name="mid", grid=(T // tm,),
        in_specs=[row(D_MODEL), row(D_MODEL), row(MLA_W), row(MLA_W), row(POOL_W), halo_spec, row(POOL_W),
                  _full(w_out.shape), _full(pool_w.shape),
                  _full(pool_scale.shape), _full(ln_g.shape), _full(ln_b.shape)],
        out_specs=(row(D_MODEL), row(MLA_W), row(MLA_W), row(MLA_W), row(POOL_W), row(POOL_W),
                   _full((D_MODEL, D_MODEL)), _full(pool_w.shape), _full((1, POOL_W)),
                   _full((1, D_MODEL)), _full((1, D_MODEL)), _full((1, 128))),
        out_shape=(f(D_MODEL, F32), f(MLA_W, BF16), f(MLA_W, F32), f(MLA_W, BF16), f(POOL_W, BF16), f(POOL_W, F32),
                   jax.ShapeDtypeStruct((D_MODEL, D_MODEL), F32), jax.ShapeDtypeStruct(pool_w.shape, F32),
                   jax.ShapeDtypeStruct((1, POOL_W), F32), jax.ShapeDtypeStruct((1, D_MODEL), F32),
                   jax.ShapeDtypeStruct((1, D_MODEL), F32), jax.ShapeDtypeStruct((1, 128), F32)),
        compiler_params=_cparams(1),
    )(x, tgt, o, ga, u, u, gb, w_out, pool_w, pool_scale, ln_g, ln_b)


def _attn_bwd(q, k, v, do, lse, delta, pos_col, pos_row, bounds, nb, S, tq, tk):
    T = q.shape[0]
    nq, nk = S // tq, S // tk
    reps = tk // 128
    hg = HEAD_GROUP

    def body(qmin_ref, qmax_ref, kmin_ref, kmax_ref, q_ref, k_ref, v_ref, do_ref, lse_ref, dl_ref, pc_ref, pr_ref,
             dq_ref, dk_ref, dv_ref):
        b, j = pl.program_id(0), pl.program_id(2)

        @pl.when(j == 0)
        def _():
            dq_ref[...] = jnp.zeros_like(dq_ref)

        dk_ref[...] = jnp.zeros_like(dk_ref)
        dv_ref[...] = jnp.zeros_like(dv_ref)
        k_lo = kmin_ref[b * nk + j]
        k_hi = kmax_ref[b * nk + j]

        def tile(i, masked):
            rows = pl.ds(pl.multiple_of(i * tq, tq), tq)
            if masked:
                keep = pc_ref[rows, :] >= pr_ref[...]
            stage = []
            for g in range(hg):
                qk = slice(g * HEAD_PAD, (g + 1) * HEAD_PAD)
                hv = slice(g * 128, (g + 1) * 128)
                s = _dot_nt(q_ref[rows, qk], k_ref[:, qk])
                if masked:
                    s = jnp.where(keep, s, NEG)
                stage.append((s, _dot_nt(do_ref[rows, hv], v_ref[:, hv])))
            grads = []
            for g in range(hg):
                hv = slice(g * 128, (g + 1) * 128)
                s, dp = stage[g]
                p = jnp.exp2(s - jnp.concatenate([lse_ref[rows, hv]] * reps, axis=1))
                ds = (p * (dp - jnp.concatenate([dl_ref[rows, hv]] * reps, axis=1))).astype(BF16)
                grads.append((p.astype(BF16), ds))
            for g in range(hg):
                qk = slice(g * HEAD_PAD, (g + 1) * HEAD_PAD)
                hv = slice(g * 128, (g + 1) * 128)
                p, ds = grads[g]
                dv_ref[:, hv] += _dot_tn(p, do_ref[rows, hv])
                dq_ref[rows, qk] += _dot(ds, k_ref[:, qk])
                dk_ref[:, qk] += _dot_tn(ds, q_ref[rows, qk])

        def step(i, carry):
            visible = k_lo <= qmax_ref[b * nq + i]
            clear = qmin_ref[b * nq + i] >= k_hi

            @pl.when(jnp.logical_and(visible, clear))
            def _():
                tile(i, False)

            @pl.when(jnp.logical_and(visible, jnp.logical_not(clear)))
            def _():
                tile(i, True)
            return carry

        lax.fori_loop(0, nq, step, 0)

    ng = HEADS // hg
    seq = lambda w: pl.BlockSpec((S, w), lambda b, h, j, *_: (b, h))
    blk = lambda w: pl.BlockSpec((tk, w), lambda b, h, j, *_: (b * nk + j, h))
    return pl.pallas_call(
        body, name="attn_bwd",
        grid_spec=pltpu.PrefetchScalarGridSpec(
            num_scalar_prefetch=4, grid=(nb, ng, nk),
            in_specs=[seq(hg * HEAD_PAD), blk(hg * HEAD_PAD), blk(hg * 128),
                      seq(hg * 128), seq(hg * 128), seq(hg * 128),
                      pl.BlockSpec((S, 1), lambda b, h, j, *_: (b, 0)),
                      pl.BlockSpec((None, 1, tk), lambda b, h, j, *_: (b, 0, j))],
            out_specs=(seq(hg * HEAD_PAD), blk(hg * HEAD_PAD), blk(hg * 128))),
        out_shape=(jax.ShapeDtypeStruct((T, HEADS * HEAD_PAD), F32),
                   jax.ShapeDtypeStruct((T, HEADS * HEAD_PAD), F32),
                   jax.ShapeDtypeStruct((T, MLA_W), F32)),
        compiler_params=_cparams(3),
    )(*bounds, q, k, v, do, lse, delta, pos_col, pos_row)


def _bwd_proj(dq, dk, dv, xq, xkv, x, dz, dga, dgb, dpc, rc, rsa, rsb, w_uq_t, w_ukv, w_in_t, gq, gkv, S, tm):
    T = x.shape[0]
    tps = S // tm
    hb = tm // HALO
    n_steps = T // tm

    def body(dq_ref, dk_ref, dv_ref, xq_ref, xkv_ref, x_ref, dz_ref, dga_ref, dgb_ref, dpc_ref, dph_ref,
             c_ref, sa_ref, sb_ref, wuq_ref, wukv_ref, win_ref, gq_ref, gkv_ref,
             dx_ref, dwin_hbm, dwuq_hbm, dwukv_hbm, dgq_ref, dgkv_ref,
             acc_win, acc_wuq, acc_wukv):
        i = pl.program_id(0)

        @pl.when(i == 0)
        def _():
            acc_win[...] = jnp.zeros_like(acc_win)
            acc_wuq[...] = jnp.zeros_like(acc_wuq)
            acc_wukv[...] = jnp.zeros_like(acc_wukv)
            dgq_ref[...] = jnp.zeros_like(dgq_ref)
            dgkv_ref[...] = jnp.zeros_like(dgkv_ref)

        c, sa, sb = c_ref[...], sa_ref[...], sb_ref[...]
        dq_v = dq_ref[...] * SCALE
        dk_v = dk_ref[...] * LN2
        dv_v = dv_ref[...]
        dq_parts, dkv_parts = [], []
        dkr = jnp.zeros((tm, 128), F32)
        for hh in range(HEADS):
            b0 = hh * HEAD_PAD
            dq_parts.append(dq_v[:, b0:b0 + 128].astype(BF16))
            dq_parts.append(_rope(dq_v[:, b0 + 128:b0 + 256], c, sa, sb, -1.0).astype(BF16))
            dkv_parts.append(dk_v[:, b0:b0 + 128].astype(BF16))
            dkv_parts.append(dv_v[:, hh * 128:(hh + 1) * 128].astype(BF16))
            dkr = dkr + dk_v[:, b0 + 128:b0 + 256]
        dqp = jnp.concatenate(dq_parts, axis=1)
        dkvp = jnp.concatenate(dkv_parts, axis=1)
        dkrr = _rope(dkr, c, sa, sb, -1.0)

        def rms_bwd(xv, g, dyn, dg_ref):
            r = lax.rsqrt(jnp.mean(xv * xv, axis=-1, keepdims=True) + RMS_EPS)
            xhat = xv * r
            dg_ref[...] += jnp.sum(dyn * xhat, axis=0, keepdims=True)
            dxh = dyn * g
            return r * (dxh - xhat * jnp.mean(dxh * xhat, axis=-1, keepdims=True))

        xq_v = xq_ref[...]
        gq_v = gq_ref[...]
        rq = lax.rsqrt(jnp.mean(xq_v * xq_v, axis=-1, keepdims=True) + RMS_EPS)
        acc_wuq[...] += _dot_tn(dqp, ((xq_v * rq) * gq_v).astype(BF16))
        dxq = rms_bwd(xq_v, gq_v, _dot(dqp, wuq_ref[...]), dgq_ref)

        xkv_v = xkv_ref[...]
        gkv_v = gkv_ref[...]
        rkv = lax.rsqrt(jnp.mean(xkv_v * xkv_v, axis=-1, keepdims=True) + RMS_EPS)
        acc_wukv[...] += _dot_tn(((xkv_v * rkv) * gkv_v).astype(BF16), dkvp)
        dxkv = rms_bwd(xkv_v, gkv_v, _dot_nt(dkvp, wukv_ref[...]), dgkv_ref)

        seq_tile = i % tps
        tpos = seq_tile * tm + lax.broadcasted_iota(jnp.int32, (tm, 1), 0)
        dpc_v = dpc_ref[...]
        halo = jnp.where(seq_tile == tps - 1, 0.0, dph_ref[...])
        n = tm + HALO
        du = []
        for g in range(POOL_G):
            lanes = slice(g * POOL_GD, (g + 1) * POOL_GD)
            f = jnp.concatenate([dpc_v[:, lanes], halo[:, lanes]], axis=0)
            for st in range(g + 1):
                f = f + pltpu.roll(f, n - (1 << st), 0)
            cnt = jnp.minimum(tpos + 1, 2 << g).astype(F32)
            du.append((f[:tm, :] - dpc_v[:, lanes] * cnt).astype(BF16))

        dh = jnp.concatenate([dxq.astype(BF16), dxkv.astype(BF16), dkrr.astype(BF16), dga_ref[...]]
                             + du + [dgb_ref[...]], axis=1)
        dx_ref[...] = ALPHA * dz_ref[...] + _dot(dh, win_ref[...])
        acc_win[...] += _dot_tn(dh, x_ref[...].astype(BF16))

        @pl.when(i == n_steps - 1)
        def _():
            pltpu.sync_copy(acc_win.at[pl.ds(0, 832)], dwin_hbm.at[pl.ds(0, 832)])
            pltpu.sync_copy(acc_win.at[pl.ds(896, IN_EXT - 896)], dwin_hbm.at[pl.ds(832, IN_W - 832)])
            for hh in range(HEADS):
                pltpu.sync_copy(acc_wuq.at[pl.ds(hh * HEAD_PAD, NOPE + ROPE)], dwuq_hbm.at[hh])
            pltpu.sync_copy(acc_wukv, dwukv_hbm)

    row = lambda w: pl.BlockSpec((tm, w), lambda i: (i, 0))
    halo_spec = pl.BlockSpec((HALO, POOL_W), lambda i: (jnp.minimum((i + 1) * hb, T // HALO - 1), 0))
    return pl.pallas_call(
        body, name="bwd_proj", grid=(n_steps,),
        in_specs=[row(1024), row(1024), row(512), row(512), row(256), row(D_MODEL), row(D_MODEL),
                  row(512), row(512), row(512), halo_spec, row(128), row(128), row(128),
                  _full(w_uq_t.shape), _full(w_ukv.shape), _full(w_in_t.shape), _full(gq.shape), _full(gkv.shape)],
        out_specs=(row(D_MODEL), ANY, ANY, ANY, _full((1, Q_LORA)), _full((1, KV_LORA))),
        out_shape=(jax.ShapeDtypeStruct((T, D_MODEL), F32),
                   jax.ShapeDtypeStruct((IN_W, D_MODEL), F32),
                   jax.ShapeDtypeStruct((HEADS, NOPE + ROPE, Q_LORA), F32),
                   jax.ShapeDtypeStruct((KV_LORA, 1024), F32),
                   jax.ShapeDtypeStruct((1, Q_LORA), F32), jax.ShapeDtypeStruct((1, KV_LORA), F32)),
        scratch_shapes=[pltpu.VMEM((IN_EXT, D_MODEL), F32), pltpu.VMEM((HEADS * HEAD_PAD, Q_LORA), F32),
                        pltpu.VMEM((KV_LORA, 1024), F32)],
        compiler_params=_cparams(1),
    )(dq, dk, dv, xq, xkv, x, dz, dga, dgb, dpc, dpc, rc, rsa, rsb, w_uq_t, w_ukv, w_in_t, gq, gkv)


def kernel(x, positions, w_in, q_norm_g, w_uq, kv_norm_g, w_ukv, pool_w, pool_scale, w_out, ln_g, ln_b, loss_target, m_w_in, m_q_norm_g, m_w_uq, m_kv_norm_g, m_w_ukv, m_pool_w, m_pool_scale, m_w_out, m_ln_g, m_ln_b, v_w_in, v_q_norm_g, v_w_uq, v_kv_norm_g, v_w_ukv, v_pool_w, v_pool_scale, v_w_out, v_ln_g, v_ln_b):
    nb, S, _ = x.shape
    T = nb * S
    tm = min(256, S)
    tq = min(512, S)
    tk = min(512, S)
    assert S % tm == 0 and tm % HALO == 0 and S % tq == 0 and S % tk == 0

    cx, cy, cc = lax.axis_index("x"), lax.axis_index("y"), lax.axis_index("c")
    me = 2 * cx + cy
    place_arr = jnp.stack([me, cc]).astype(jnp.int32)

    def own_slot(w, slot_rows):
        blk = jnp.pad(w.astype(BF16), ((0, slot_rows - w.shape[0]), (0, 0)))
        return lax.dynamic_update_slice(jnp.zeros((N_CHIPS,) + blk.shape, BF16), blk[None], (me, 0, 0))

    w_in_g, w_uq_g, w_ukv_g, w_out_g = _weight_gather(
        [own_slot(w_in.T, 592), own_slot(w_uq.T, HEAD_PAD), own_slot(w_ukv, KV_LORA), own_slot(w_out, 256)],
        (592, NOPE + ROPE, KV_LORA, 256))
    w_in_f = w_in_g.reshape(IN_W, D_MODEL)
    w_in_t = jnp.concatenate([w_in_f[:832], jnp.zeros((64, D_MODEL), BF16), w_in_f[832:]], axis=0)
    w_uq_t = w_uq_g.reshape(HEADS * HEAD_PAD, Q_LORA)
    w_ukv_f = w_ukv_g.transpose(1, 0, 2).reshape(KV_LORA, 1024)
    w_out_f = w_out_g.reshape(D_MODEL, D_MODEL)
    pool_w_b = pool_w.astype(BF16)
    gq2 = q_norm_g.reshape(1, Q_LORA)
    gkv2 = kv_norm_g.reshape(1, KV_LORA)
    ps2 = pool_scale.reshape(1, POOL_W)

    half = ROPE // 2
    inv_freq = ROPE_THETA ** (-jnp.arange(half, dtype=F32) / half)
    ang = positions.astype(F32).reshape(T, 1) * inv_freq
    cos, sin = jnp.cos(ang), jnp.sin(ang)
    z32, z64 = jnp.zeros((T, 32), F32), jnp.zeros((T, 64), F32)
    rc = jnp.concatenate([cos, cos, z64], axis=1)
    rsa = jnp.concatenate([sin, z32, z64], axis=1)
    rsb = jnp.concatenate([z32, sin, z64], axis=1)
    pos_col = positions.reshape(T, 1)
    pos_row = positions.reshape(nb, 1, S)
    pos_q = positions.reshape(nb, S // tq, tq)
    pos_k = positions.reshape(nb, S // tk, tk)
    bounds = (jnp.min(pos_q, axis=2).reshape(-1), jnp.max(pos_q, axis=2).reshape(-1),
              jnp.min(pos_k, axis=2).reshape(-1), jnp.max(pos_k, axis=2).reshape(-1))

    xf = x.reshape(T, D_MODEL)
    tgt = loss_target.reshape(T, D_MODEL)

    xq, xkv, ga, u, gb, q, k, v = _fwd_proj(xf, w_in_t, w_uq_t, w_ukv_f, gq2, gkv2, rc, rsa, rsb, tm)
    o, lse = _attn_fwd(q, k, v, pos_col, pos_row, bounds, nb, S, tq, tk)

    (dz, do, delta, dga, dgb, dpc, d_w_out, d_pool_w, d_pool_scale, d_ln_g, d_ln_b, loss_part) = _mid(
        xf, tgt, o, ga, u, gb, w_out_f, pool_w_b, ps2, ln_g, ln_b, S, tm)

    dq, dk, dv = _attn_bwd(q, k, v, do, lse, delta, pos_col, pos_row, bounds, nb, S, tq, tk)
    dx, d_w_in_t, d_w_uq_t, d_w_ukv, d_gq, d_gkv = _bwd_proj(
        dq, dk, dv, xq, xkv, xf, dz, dga, dgb, dpc, rc, rsa, rsb, w_uq_t, w_ukv_f, w_in_t, gq2, gkv2, S, tm)
    grad_x = dx.reshape(nb, S, D_MODEL)

    g_in = d_w_in_t.reshape(N_CHIPS, 592, D_MODEL)
    g_uq = d_w_uq_t
    g_ukv = d_w_ukv.reshape(KV_LORA, N_CHIPS, 256).transpose(1, 0, 2)
    g_out = d_w_out.reshape(N_CHIPS, 256, D_MODEL)
    gs = [g_in, g_uq, g_ukv, g_out]
    from_sibling = _grad_to_sibling(gs)
    chip_sums, own_sums = _add_sibling_half(gs, from_sibling, place_arr)
    from_chips = _grad_to_chips(chip_sums)
    g_big = _halves_exchange(_add_chip_parts(own_sums, from_chips, place_arr))

    pw_sum, vec_sum = _small_allreduce(d_pool_w, d_ln_g, d_ln_b, d_pool_scale, d_gq, d_gkv, loss_part)

    big = _adamw_big(g_big, [w_in.T, w_uq.T, w_ukv, w_out], [m_w_in.T, m_w_uq.T, m_w_ukv, m_w_out],
                     [v_w_in.T, v_w_uq.T, v_w_ukv, v_w_out])
    two_d = lambda a: a.reshape(-1, a.shape[-1])
    small_names = lambda pw, lg, lb, ps, gq, gkv: [two_d(pw), lg, lb, ps.reshape(1, -1), gq.reshape(1, -1), gkv.reshape(1, -1)]
    small, loss_row = _adamw_small(
        pw_sum, vec_sum,
        small_names(pool_w, ln_g, ln_b, pool_scale, q_norm_g, kv_norm_g),
        small_names(m_pool_w, m_ln_g, m_ln_b, m_pool_scale, m_q_norm_g, m_kv_norm_g),
        small_names(v_pool_w, v_ln_g, v_ln_b, v_pool_scale, v_q_norm_g, v_kv_norm_g))
    loss = loss_row[0, 0]

    def leaves(kind):
        b = [g_big[t] if kind == 0 else big[t][kind - 1] for t in range(N_BIG)]
        b = [b[0].T, b[1].T, b[2], b[3]]
        s = [small[t][kind] for t in range(6)]
        return (b[0], s[4].reshape(Q_LORA), b[1], s[5].reshape(KV_LORA), b[2],
                s[0].reshape(POOL_G, POOL_GD, POOL_GD), s[3].reshape(POOL_W), b[3], s[1], s[2])

    return (loss, grad_x) + leaves(0) + leaves(1) + leaves(2) + leaves(3)
```

```python
import functools

import jax
import jax.numpy as jnp
from jax import lax
from jax.experimental import pallas as pl
from jax.experimental.pallas import tpu as pltpu

F32 = jnp.float32
BF16 = jnp.bfloat16
MESH = pl.DeviceIdType.MESH

HEADS = 4
NOPE = 128
ROPE = 64
HEAD_PAD = 256
Q_LORA = 512
KV_LORA = 256
MLA_W = 512
POOL_W = 512
POOL_G = 4
POOL_GD = 128
D_MODEL = 1024
IN_W = 2368
IN_EXT = 2432
ROPE_THETA = 10000.0
RMS_EPS = 1e-6
LN_EPS = 1e-5
ALPHA = 2.0 ** 0.25
SCALE = 192.0 ** -0.5
LOG2E = 1.4426950408889634
LN2 = 0.6931471805599453
QSCALE = SCALE * LOG2E
NEG = float(jnp.finfo(jnp.float32).min)
HEAD_GROUP = 2
HALO = 16

ADAM_LR = 0.001
ADAM_B1 = 0.9
ADAM_B2 = 0.999
ADAM_EPS = 1e-08
ADAM_WD = 0.01
ADAM_STEP = 10

N_CHIPS = 4
N_BIG = 4
VEC_ROWS = 16
VEC_HALF = VEC_ROWS // 2

VMEM_LIMIT = 56 * 1024 * 1024


def _cparams(n_grid_dims=0, **kw):
    sem = ("arbitrary",) * n_grid_dims if n_grid_dims else None
    return pltpu.CompilerParams(dimension_semantics=sem, vmem_limit_bytes=VMEM_LIMIT, **kw)


def _full(shape):
    nd = len(shape)
    return pl.BlockSpec(shape, lambda *_: (0,) * nd)


def _dot(a, b):
    return jnp.dot(a, b, preferred_element_type=F32)


def _dot_nt(a, b):
    return lax.dot_general(a, b, (((1,), (1,)), ((), ())), preferred_element_type=F32)


def _dot_tn(a, b):
    return lax.dot_general(a, b, (((0,), (0,)), ((), ())), preferred_element_type=F32)


def _rope_tables(pos_col, freq_row):
    lane = lax.broadcasted_iota(jnp.int32, (1, 128), 1)
    ang = pos_col.astype(F32) * freq_row
    cos, sin = jnp.cos(ang), jnp.sin(ang)
    zero = jnp.zeros_like(cos)
    return (jnp.where(lane < 64, cos, zero), jnp.where(lane < 32, sin, zero),
            jnp.where(jnp.logical_and(lane >= 32, lane < 64), sin, zero))


def _rope(g, c, sa, sb, sign):
    return g * c + sign * (pltpu.roll(g, 32, 1) * sb - pltpu.roll(g, 96, 1) * sa)


def _place():
    x, y, c = lax.axis_index("x"), lax.axis_index("y"), lax.axis_index("c")
    chips = [(1 - x, y), (x, 1 - y), (1 - x, 1 - y)]
    return x, y, c, chips


def _half_cols(ref, half_index):
    hc = ref.shape[-1] // 2
    lead = tuple(pl.ds(0, n) for n in ref.shape[:-1])
    return ref.at[lead + (pl.ds(half_index * hc, hc),)]


ANY = pl.BlockSpec(memory_space=pl.ANY)


def _weight_gather(slots, valid_rows):
    n = len(slots)

    def body(*refs):
        outs = refs[n:2 * n]
        send_sems, recv_sems = refs[2 * n:]
        x, y, c, chips = _place()
        me = 2 * x + y

        def copy(t, k, chip_idx, half, to):
            hc = slots[t].shape[2] // 2
            blk = outs[t].at[chip_idx, pl.ds(0, valid_rows[t]), pl.ds(half * hc, hc)]
            return pltpu.make_async_remote_copy(
                src_ref=blk, dst_ref=blk, send_sem=send_sems.at[6 * t + k], recv_sem=recv_sems.at[6 * t + k],
                device_id=to, device_id_type=MESH)

        first = [copy(t, j, me, c, (cx, cy, c)) for t in range(n) for j, (cx, cy) in enumerate(chips)]
        for cp in first:
            cp.start()
        passed = []
        for j, (cx, cy) in enumerate(chips):
            for t in range(n):
                copy(t, j, 2 * cx + cy, c, (x, y, c)).wait_recv()
                fwd = copy(t, 3 + j, 2 * cx + cy, c, (x, y, 1 - c))
                fwd.start()
                passed.append(fwd)
        for j, (cx, cy) in enumerate(chips):
            for t in range(n):
                copy(t, 3 + j, 2 * cx + cy, 1 - c, (x, y, c)).wait_recv()
        for cp in first + passed:
            cp.wait_send()

    return pl.pallas_call(
        body, name="weight_gather",
        out_shape=tuple(jax.ShapeDtypeStruct(a.shape, a.dtype) for a in slots),
        in_specs=[ANY] * n, out_specs=(ANY,) * n, input_output_aliases={t: t for t in range(n)},
        scratch_shapes=[pltpu.SemaphoreType.DMA((6 * n,)), pltpu.SemaphoreType.DMA((6 * n,))],
    )(*slots)


def _grad_to_sibling(gs):
    n = len(gs)

    def body(*refs):
        g_refs, r_refs = refs[:n], refs[n:2 * n]
        send_sems, recv_sems = refs[2 * n:]
        x, y, c, _ = _place()
        cps = []
        for t in range(n):
            cp = pltpu.make_async_remote_copy(
                src_ref=_half_cols(g_refs[t], 1 - c), dst_ref=r_refs[t], send_sem=send_sems.at[t], recv_sem=recv_sems.at[t],
                device_id=(x, y, 1 - c), device_id_type=MESH)
            cp.start()
            cps.append(cp)
        for cp in cps:
            cp.wait()

    return pl.pallas_call(
        body, name="grad_to_sibling",
        out_shape=tuple(jax.ShapeDtypeStruct((N_CHIPS, g.shape[1], g.shape[2] // 2), F32) for g in gs),
        in_specs=[ANY] * n, out_specs=(ANY,) * n,
        scratch_shapes=[pltpu.SemaphoreType.DMA((n,)), pltpu.SemaphoreType.DMA((n,))],
    )(*gs)


def _grad_to_chips(ss):
    n = len(ss)

    def body(*refs):
        s_refs, r_refs = refs[:n], refs[n:2 * n]
        send_sems, recv_sems = refs[2 * n:]
        x, y, c, chips = _place()
        cps = []
        for t in range(n):
            for j, (cx, cy) in enumerate(chips):
                cp = pltpu.make_async_remote_copy(
                    src_ref=s_refs[t].at[2 * cx + cy], dst_ref=r_refs[t].at[j],
                    send_sem=send_sems.at[3 * t + j], recv_sem=recv_sems.at[3 * t + j],
                    device_id=(cx, cy, c), device_id_type=MESH)
                cp.start()
                cps.append(cp)
        for cp in cps:
            cp.wait()

    return pl.pallas_call(
        body, name="grad_to_chips",
        out_shape=tuple(jax.ShapeDtypeStruct((3,) + s.shape[1:], s.dtype) for s in ss),
        in_specs=[ANY] * n, out_specs=(ANY,) * n,
        scratch_shapes=[pltpu.SemaphoreType.DMA((3 * n,)), pltpu.SemaphoreType.DMA((3 * n,))],
    )(*ss)


def _halves_exchange(fs):
    n = len(fs)

    def body(*refs):
        o_refs = refs[n:2 * n]
        send_sems, recv_sems = refs[2 * n:]
        x, y, c, _ = _place()
        sib = (x, y, 1 - c)
        cps = []
        for t in range(n):
            mine = _half_cols(o_refs[t], c)
            cp = pltpu.make_async_remote_copy(
                src_ref=mine, dst_ref=mine, send_sem=send_sems.at[t], recv_sem=recv_sems.at[t],
                device_id=sib, device_id_type=MESH)
            cp.start()
            cps.append(cp)
        for t in range(n):
            theirs = _half_cols(o_refs[t], 1 - c)
            pltpu.make_async_remote_copy(
                src_ref=theirs, dst_ref=theirs, send_sem=send_sems.at[t], recv_sem=recv_sems.at[t],
                device_id=sib, device_id_type=MESH).wait_recv()
        for cp in cps:
            cp.wait_send()

    return pl.pallas_call(
        body, name="halves_exchange",
        out_shape=tuple(jax.ShapeDtypeStruct(f.shape, f.dtype) for f in fs),
        in_specs=[ANY] * n, out_specs=(ANY,) * n, input_output_aliases={t: t for t in range(n)},
        scratch_shapes=[pltpu.SemaphoreType.DMA((n,)), pltpu.SemaphoreType.DMA((n,))],
    )(*fs)


def _small_allreduce(d_pool_w, d_ln_g, d_ln_b, d_ps, d_gq, d_gkv, loss_part):
    pw_rows = POOL_G * POOL_GD

    def body(pw_in, lng_in, lnb_in, ps_in, gq_in, gkv_in, loss_in, pw_out, vec_out,
             vec_in, pw_sib, vec_sib, pw_sum, vec_sum, pw_chip, vec_chip, send_sems, recv_sems):
        x, y, c, chips = _place()
        sib = (x, y, 1 - c)
        vec_in[...] = jnp.zeros_like(vec_in)
        vec_in[0:1, :] = lng_in[...]
        vec_in[1:2, :] = lnb_in[...]
        vec_in[2:3, 0:POOL_W] = ps_in[...]
        vec_in[3:4, 0:Q_LORA] = gq_in[...]
        vec_in[8:9, 0:KV_LORA] = gkv_in[...]
        vec_in[9:10, 0:128] = loss_in[...]

        def rdma(k, src, dst, to):
            return pltpu.make_async_remote_copy(src_ref=src, dst_ref=dst, send_sem=send_sems.at[k],
                                                recv_sem=recv_sems.at[k], device_id=to, device_id_type=MESH)

        a = [rdma(0, pw_in, pw_sib, sib), rdma(1, vec_in, vec_sib, sib)]
        for cp in a:
            cp.start()
        for cp in a:
            cp.wait()
        pw_sum[...] = pw_in[...] + pw_sib[...]
        vec_sum[...] = vec_in[...] + vec_sib[...]

        bufs = [(pw_sum, pw_chip, pw_out, pw_rows // 2), (vec_sum, vec_chip, vec_out, VEC_HALF)]
        cps = []
        for t, (sm, chip_buf, _, hr) in enumerate(bufs):
            rows = pl.ds(pl.multiple_of(c * hr, 8), hr)
            for j, (cx, cy) in enumerate(chips):
                cp = rdma(2 + 3 * t + j, sm.at[rows], chip_buf.at[j], (cx, cy, c))
                cp.start()
                cps.append(cp)
        for cp in cps:
            cp.wait()
        last = []
        for t, (sm, chip_buf, out, hr) in enumerate(bufs):
            rows = pl.ds(pl.multiple_of(c * hr, 8), hr)
            other = pl.ds(pl.multiple_of((1 - c) * hr, 8), hr)
            out[rows, :] = (sm[rows, :] + chip_buf[0]) + (chip_buf[1] + chip_buf[2])
            cp = rdma(8 + t, out.at[rows], out.at[rows], sib)
            cp.start()
            last.append((cp, rdma(8 + t, out.at[other], out.at[other], sib)))
        for cp, recv in last:
            recv.wait_recv()
            cp.wait_send()

    vm = pl.BlockSpec(memory_space=pltpu.VMEM)
    vec_shape = (VEC_ROWS, D_MODEL)
    return pl.pallas_call(
        body, name="small_allreduce",
        out_shape=(jax.ShapeDtypeStruct((pw_rows, POOL_GD), F32), jax.ShapeDtypeStruct(vec_shape, F32)),
        in_specs=[vm] * 7, out_specs=(vm, vm),
        scratch_shapes=[pltpu.VMEM(vec_shape, F32), pltpu.VMEM((pw_rows, POOL_GD), F32), pltpu.VMEM(vec_shape, F32),
                        pltpu.VMEM((pw_rows, POOL_GD), F32), pltpu.VMEM(vec_shape, F32),
                        pltpu.VMEM((3, pw_rows // 2, POOL_GD), F32), pltpu.VMEM((3, VEC_HALF, D_MODEL), F32),
                        pltpu.SemaphoreType.DMA((10,)), pltpu.SemaphoreType.DMA((10,))],
    )(d_pool_w.reshape(pw_rows, POOL_GD), d_ln_g, d_ln_b, d_ps, d_gq, d_gkv, loss_part)


def _add_sibling_half(gs, rs, place_arr):
    n = len(gs)

    def body(place_ref, *refs):
        k = pl.program_id(0)
        for t in range(n):
            total = refs[t][...] + refs[n + t][...]
            refs[2 * n + 2 * t][...] = total.astype(BF16)

            @pl.when(k == place_ref[0])
            def _():
                refs[2 * n + 2 * t + 1][...] = total

    in_specs, out_specs, out_shape = [], [], []
    for g in gs:
        in_specs.append(pl.BlockSpec((None, g.shape[1], g.shape[2] // 2), lambda k, p: (k, 0, p[1])))
    for r in rs:
        blk = pl.BlockSpec((None,) + r.shape[1:], lambda k, p: (k, 0, 0))
        in_specs.append(blk)
        out_specs += [blk, pl.BlockSpec(r.shape[1:], lambda k, p: (0, 0))]
        out_shape += [jax.ShapeDtypeStruct(r.shape, BF16), jax.ShapeDtypeStruct(r.shape[1:], F32)]
    outs = pl.pallas_call(
        body, name="add_sibling_half", out_shape=tuple(out_shape),
        grid_spec=pltpu.PrefetchScalarGridSpec(num_scalar_prefetch=1, grid=(N_CHIPS,),
                                               in_specs=in_specs, out_specs=tuple(out_specs)),
        compiler_params=_cparams(1),
    )(place_arr, *gs, *rs)
    return list(outs[0::2]), list(outs[1::2])


def _add_chip_parts(owns, rs, place_arr):
    n = len(owns)

    def body(place_ref, *refs):
        for t in range(n):
            r_ref = refs[n + t]
            refs[2 * n + t][...] = ((refs[t][...] + r_ref[0].astype(F32))
                                    + (r_ref[1].astype(F32) + r_ref[2].astype(F32)))

    in_specs, out_specs = [], []
    for o in owns:
        in_specs.append(pl.BlockSpec(o.shape, lambda i, p: (0, 0)))
    for r in rs:
        in_specs.append(pl.BlockSpec(r.shape, lambda i, p: (0, 0, 0)))
        out_specs.append(pl.BlockSpec(r.shape[1:], lambda i, p: (0, p[1])))
    return pl.pallas_call(
        body, name="add_chip_parts",
        out_shape=tuple(jax.ShapeDtypeStruct((o.shape[0], 2 * o.shape[1]), F32) for o in owns),
        grid_spec=pltpu.PrefetchScalarGridSpec(num_scalar_prefetch=1, grid=(1,),
                                               in_specs=in_specs, out_specs=tuple(out_specs)),
        compiler_params=_cparams(1),
    )(place_arr, *owns, *rs)


def _adamw_math(g, w, m, v):
    nm = ADAM_B1 * m + (1.0 - ADAM_B1) * g
    nv = ADAM_B2 * v + (1.0 - ADAM_B2) * (g * g)
    m_hat = nm / (1.0 - ADAM_B1 ** ADAM_STEP)
    v_hat = nv / (1.0 - ADAM_B2 ** ADAM_STEP)
    return -ADAM_LR * (m_hat / (jnp.sqrt(v_hat) + ADAM_EPS) + ADAM_WD * w), nm, nv


ADAM_STEPS = 8


def _adamw_big(gs, ws, ms, vs):
    n = len(gs)

    def body(*refs):
        for t in range(n):
            d, nm, nv = _adamw_math(refs[t][...], refs[n + t][...], refs[2 * n + t][...], refs[3 * n + t][...])
            refs[4 * n + 3 * t][...] = d
            refs[4 * n + 3 * t + 1][...] = nm
            refs[4 * n + 3 * t + 2][...] = nv

    def tile_spec(shape):
        rows, cols = shape
        if rows % (8 * ADAM_STEPS) == 0:
            return pl.BlockSpec((rows // ADAM_STEPS, cols), lambda i: (i, 0))
        return pl.BlockSpec((rows, cols // ADAM_STEPS), lambda i: (0, i))

    specs = [tile_spec(g.shape) for g in gs]
    out_specs, out_shape = [], []
    for t in range(n):
        out_specs += [specs[t]] * 3
        out_shape += [jax.ShapeDtypeStruct(gs[t].shape, F32)] * 3
    outs = pl.pallas_call(
        body, name="adamw_big", grid=(ADAM_STEPS,),
        in_specs=specs * 4, out_specs=tuple(out_specs), out_shape=tuple(out_shape),
        compiler_params=_cparams(1),
    )(*gs, *ws, *ms, *vs)
    return [outs[3 * t: 3 * t + 3] for t in range(n)]


def _adamw_small(pw_sum, vec_sum, ws, ms, vs):
    rows = (None, 0, 1, 2, 3, 8)
    n = len(ws)

    def body(pw_ref, vec_ref, *refs):
        outs = refs[3 * n:]
        for t in range(n):
            w_ref, m_ref, v_ref = refs[t], refs[n + t], refs[2 * n + t]
            if rows[t] is None:
                g = pw_ref[...]
            else:
                g = vec_ref[rows[t]:rows[t] + 1, 0:w_ref.shape[1]]
            d, nm, nv = _adamw_math(g, w_ref[...], m_ref[...], v_ref[...])
            outs[4 * t][...] = g
            outs[4 * t + 1][...] = d
            outs[4 * t + 2][...] = nm
            outs[4 * t + 3][...] = nv
        outs[4 * n][...] = vec_ref[9:10, 0:128]

    vm = pl.BlockSpec(memory_space=pltpu.VMEM)
    out_shape = []
    for w in ws:
        out_shape += [jax.ShapeDtypeStruct(w.shape, F32)] * 4
    out_shape.append(jax.ShapeDtypeStruct((1, 128), F32))
    outs = pl.pallas_call(
        body, name="adamw_small", in_specs=[vm] * (2 + 3 * n), out_specs=(vm,) * (4 * n + 1),
        out_shape=tuple(out_shape),
    )(pw_sum, vec_sum, *ws, *ms, *vs)
    return [outs[4 * t: 4 * t + 4] for t in range(n)], outs[4 * n]


def _fwd_proj(x, w_in_t, w_uq_t, w_ukv, gq, gkv, pos_col, freq_row, tm):
    T = x.shape[0]

    def body(x_ref, win_ref, wuq_ref, wukv_ref, gq_ref, gkv_ref, pos_ref, freq_ref,
             xq_ref, xkv_ref, ga_ref, u_ref, gb_ref, q_ref, k_ref, v_ref, c_ref, sa_ref, sb_ref):
        h = _dot_nt(x_ref[...].astype(BF16), win_ref[...])
        xq = h[:, 0:512]
        xkv = h[:, 512:768]
        xq_ref[...] = xq
        xkv_ref[...] = xkv
        ga_ref[...] = h[:, 896:1408]
        u_ref[...] = h[:, 1408:1920]
        gb_ref[...] = h[:, 1920:2432]
        c, sa, sb = _rope_tables(pos_ref[...], freq_ref[...])
        c_ref[...] = c
        sa_ref[...] = sa
        sb_ref[...] = sb
        rq = lax.rsqrt(jnp.mean(xq * xq, axis=-1, keepdims=True) + RMS_EPS)
        q = _dot_nt(((xq * rq) * gq_ref[...]).astype(BF16), wuq_ref[...]) * QSCALE
        rkv = lax.rsqrt(jnp.mean(xkv * xkv, axis=-1, keepdims=True) + RMS_EPS)
        kv = _dot(((xkv * rkv) * gkv_ref[...]).astype(BF16), wukv_ref[...])
        kr = _rope(h[:, 768:896], c, sa, sb, 1.0).astype(BF16)
        for hh in range(HEADS):
            b0 = hh * HEAD_PAD
            q_ref[:, b0:b0 + 128] = q[:, b0:b0 + 128].astype(BF16)
            q_ref[:, b0 + 128:b0 + 256] = _rope(q[:, b0 + 128:b0 + 256], c, sa, sb, 1.0).astype(BF16)
            k_ref[:, b0:b0 + 128] = kv[:, b0:b0 + 128].astype(BF16)
            k_ref[:, b0 + 128:b0 + 256] = kr
            v_ref[:, hh * 128:(hh + 1) * 128] = kv[:, b0 + 128:b0 + 256].astype(BF16)

    row = lambda w: pl.BlockSpec((tm, w), lambda i: (i, 0))
    f = lambda w, dt: jax.ShapeDtypeStruct((T, w), dt)
    return pl.pallas_call(
        body, name="fwd_proj", grid=(T // tm,),
        in_specs=[row(D_MODEL), _full(w_in_t.shape), _full(w_uq_t.shape), _full(w_ukv.shape),
                  _full(gq.shape), _full(gkv.shape), row(1), _full(freq_row.shape)],
        out_specs=(row(512), row(256), row(512), row(512), row(512), row(1024), row(1024), row(512),
                   row(128), row(128), row(128)),
        out_shape=(f(512, F32), f(256, F32), f(512, F32), f(512, F32), f(512, F32),
                   f(1024, BF16), f(1024, BF16), f(512, BF16), f(128, F32), f(128, F32), f(128, F32)),
        compiler_params=_cparams(1),
    )(x, w_in_t, w_uq_t, w_ukv, gq, gkv, pos_col, freq_row)


def _attn_fwd(q, k, v, pos_col, pos_row, bounds, nb, S, tq, tk):
    T = q.shape[0]
    nq, nk = S // tq, S // tk
    reps = tk // 128
    hg = HEAD_GROUP

    def body(qmin_ref, qmax_ref, kmin_ref, kmax_ref, q_ref, k_ref, v_ref, pc_ref, pr_ref, o_ref, lse_ref,
             m_sc, l_sc, acc_sc):
        b, i = pl.program_id(0), pl.program_id(2)
        m_sc[...] = jnp.full(m_sc.shape, NEG, F32)
        l_sc[...] = jnp.zeros_like(l_sc)
        acc_sc[...] = jnp.zeros_like(acc_sc)
        q_lo = qmin_ref[b * nq + i]
        q_hi = qmax_ref[b * nq + i]

        def tile(j, masked):
            off = pl.multiple_of(j * tk, tk)
            if masked:
                keep = pc_ref[...] >= pr_ref[pl.ds(j, 1), :]
            logits = []
            for g in range(hg):
                qk = slice(g * HEAD_PAD, (g + 1) * HEAD_PAD)
                s = _dot_nt(q_ref[:, qk], k_ref[pl.ds(off, tk), qk])
                if masked:
                    s = jnp.where(keep, s, NEG)
                logits.append(s)
            probs = []
            for g in range(hg):
                hv = slice(g * 128, (g + 1) * 128)
                s = logits[g]
                m_prev = m_sc[:, hv]
                m_new = jnp.maximum(m_prev, jnp.max(s, axis=1, keepdims=True))
                p = jnp.exp2(s - jnp.concatenate([m_new] * reps, axis=1))
                a = jnp.exp2(m_prev - m_new)
                l_sc[:, hv] = a * l_sc[:, hv] + jnp.sum(p, axis=1, keepdims=True)
                m_sc[:, hv] = m_new
                probs.append((p.astype(BF16), a))
            for g in range(hg):
                hv = slice(g * 128, (g + 1) * 128)
                p, a = probs[g]
                acc_sc[:, hv] = a * acc_sc[:, hv] + _dot(p, v_ref[pl.ds(off, tk), hv])

        def step(j, carry):
            visible = kmin_ref[b * nk + j] <= q_hi
            clear = q_lo >= kmax_ref[b * nk + j]

            @pl.when(jnp.logical_and(visible, clear))
            def _():
                tile(j, False)

            @pl.when(jnp.logical_and(visible, jnp.logical_not(clear)))
            def _():
                tile(j, True)
            return carry

        lax.fori_loop(0, nk, step, 0)
        l = l_sc[...]
        o_ref[...] = acc_sc[...] / l
        lse_ref[...] = m_sc[...] + jnp.log2(l)

    ng = HEADS // hg
    stat = pltpu.VMEM((tq, hg * 128), F32)
    return pl.pallas_call(
        body, name="attn_fwd",
        grid_spec=pltpu.PrefetchScalarGridSpec(
            num_scalar_prefetch=4, grid=(nb, ng, nq),
            in_specs=[pl.BlockSpec((tq, hg * HEAD_PAD), lambda b, h, i, *_: (b * nq + i, h)),
                      pl.BlockSpec((S, hg * HEAD_PAD), lambda b, h, i, *_: (b, h)),
                      pl.BlockSpec((S, hg * 128), lambda b, h, i, *_: (b, h)),
                      pl.BlockSpec((tq, 1), lambda b, h, i, *_: (b * nq + i, 0)),
                      pl.BlockSpec((None, nk, tk), lambda b, h, i, *_: (b, 0, 0))],
            out_specs=(pl.BlockSpec((tq, hg * 128), lambda b, h, i, *_: (b * nq + i, h)),
                       pl.BlockSpec((tq, hg * 128), lambda b, h, i, *_: (b * nq + i, h))),
            scratch_shapes=[stat, stat, stat]),
        out_shape=(jax.ShapeDtypeStruct((T, MLA_W), F32), jax.ShapeDtypeStruct((T, MLA_W), F32)),
        compiler_params=_cparams(3),
    )(*bounds, q, k, v, pos_col, pos_row.reshape(nb, nk, tk))


def _mid(x, tgt, o, ga, u, gb, w_out, pool_w, pool_scale, ln_g, ln_b, S, tm):
    T = x.shape[0]
    tps = S // tm
    hb = tm // HALO

    def body(x_ref, tgt_ref, o_ref, ga_ref, u_ref, uh_ref, gb_ref, wout_ref, pw_ref,
             ps_ref, lng_ref, lnb_ref,
             dz_ref, do_ref, delta_ref, dga_ref, dgb_ref, dpc_ref,
             dwout_ref, dpw_ref, dps_ref, dlng_ref, dlnb_ref, loss_ref):
        i = pl.program_id(0)

        @pl.when(i == 0)
        def _():
            dwout_ref[...] = jnp.zeros_like(dwout_ref)
            dpw_ref[...] = jnp.zeros_like(dpw_ref)
            dps_ref[...] = jnp.zeros_like(dps_ref)
            dlng_ref[...] = jnp.zeros_like(dlng_ref)
            dlnb_ref[...] = jnp.zeros_like(dlnb_ref)
            loss_ref[...] = jnp.zeros_like(loss_ref)

        seq_tile = i % tps
        tpos = seq_tile * tm + lax.broadcasted_iota(jnp.int32, (tm, 1), 0)
        ga_v = ga_ref[...]
        sig_a = jax.nn.sigmoid(ga_v)
        silu_a = ga_v * sig_a
        o_v = o_ref[...]
        ya = o_v * silu_a

        u_v = u_ref[...]
        halo = jnp.where(seq_tile == 0, 0.0, uh_ref[...])
        pooled, cnts, mixed = [], [], []
        for g in range(POOL_G):
            lanes = slice(g * POOL_GD, (g + 1) * POOL_GD)
            w = jnp.concatenate([halo[:, lanes], u_v[:, lanes]], axis=0)
            for st in range(g + 1):
                w = w + pltpu.roll(w, 1 << st, 0)
            cnt = jnp.minimum(tpos + 1, 2 << g).astype(F32)
            pg = (w[HALO:, :] / cnt - u_v[:, lanes]).astype(BF16)
            pooled.append(pg)
            cnts.append(cnt)
            mixed.append(_dot(pg, pw_ref[g]))
        mixed = jnp.concatenate(mixed, axis=1)
        ps = ps_ref[...]
        ybp = mixed * ps
        gb_v = gb_ref[...]
        sig_b = jax.nn.sigmoid(gb_v)
        silu_b = gb_v * sig_b
        yb = ybp * silu_b

        cat = jnp.concatenate([ya, yb], axis=1).astype(BF16)
        z = ALPHA * x_ref[...] + _dot(cat, wout_ref[...])
        mu = jnp.mean(z, axis=-1, keepdims=True)
        zc = z - mu
        rstd = lax.rsqrt(jnp.mean(zc * zc, axis=-1, keepdims=True) + LN_EPS)
        zhat = zc * rstd
        lng = lng_ref[...]
        err = (zhat * lng + lnb_ref[...]) - tgt_ref[...]
        row_loss = jnp.sum(err * err, axis=1, keepdims=True)
        loss_ref[...] += jnp.broadcast_to(jnp.sum(row_loss, axis=0, keepdims=True) * (0.5 / D_MODEL), (1, 128))
        dy = err * (1.0 / D_MODEL)
        dlng_ref[...] += jnp.sum(dy * zhat, axis=0, keepdims=True)
        dlnb_ref[...] += jnp.sum(dy, axis=0, keepdims=True)
        dzh = dy * lng
        dz = rstd * (dzh - jnp.mean(dzh, axis=-1, keepdims=True)
                     - zhat * jnp.mean(dzh * zhat, axis=-1, keepdims=True))
        dz_ref[...] = dz
        dzb = dz.astype(BF16)
        dwout_ref[...] += _dot_tn(cat, dzb)
        dcat = _dot_nt(dzb, wout_ref[...])
        dya = dcat[:, :MLA_W]
        dyb = dcat[:, MLA_W:]

        do = dya * silu_a
        do_ref[...] = do.astype(BF16)
        prod = do * o_v
        for hh in range(HEADS):
            lanes = slice(hh * 128, (hh + 1) * 128)
            delta_ref[:, lanes] = jnp.broadcast_to(jnp.sum(prod[:, lanes], axis=1, keepdims=True), (tm, 128))
        dga_ref[...] = (dya * o_v * (sig_a * (1.0 + ga_v * (1.0 - sig_a)))).astype(BF16)
        dgb_ref[...] = (dyb * ybp * (sig_b * (1.0 + gb_v * (1.0 - sig_b)))).astype(BF16)
        dybp = dyb * silu_b
        dps_ref[...] += jnp.sum(dybp * mixed, axis=0, keepdims=True)
        dmixed = (dybp * ps).astype(BF16)
        for g in range(POOL_G):
            lanes = slice(g * POOL_GD, (g + 1) * POOL_GD)
            dpw_ref[g] += _dot_tn(pooled[g], dmixed[:, lanes])
            dpc_ref[:, lanes] = _dot_nt(dmixed[:, lanes], pw_ref[g]) / cnts[g]

    row = lambda w: pl.BlockSpec((tm, w), lambda i: (i, 0))
    f = lambda w, dt: jax.ShapeDtypeStruct((T, w), dt)
    halo_spec = pl.BlockSpec((HALO, POOL_W), lambda i: (jnp.maximum(i * hb - 1, 0), 0))
    return pl.pallas_call(
        body, name="mid", grid=(T // tm,),
        in_specs=[row(D_MODEL), row(D_MODEL), row(MLA_W), row(MLA_W), row(POOL_W), halo_spec, row(POOL_W),
                  _full(w_out.shape), _full(pool_w.shape),
                  _full(pool_scale.shape), _full(ln_g.shape), _full(ln_b.shape)],
        out_specs=(row(D_MODEL), row(MLA_W), row(MLA_W), row(MLA_W), row(POOL_W), row(POOL_W),
                   _full((D_MODEL, D_MODEL)), _full(pool_w.shape), _full((1, POOL_W)),
                   _full((1, D_MODEL)), _full((1, D_MODEL)), _full((1, 128))),
        out_shape=(f(D_MODEL, F32), f(MLA_W, BF16), f(MLA_W, F32), f(MLA_W, BF16), f(POOL_W, BF16), f(POOL_W, F32),
                   jax.ShapeDtypeStruct((D_MODEL, D_MODEL), F32), jax.ShapeDtypeStruct(pool_w.shape, F32),
                   jax.ShapeDtypeStruct((1, POOL_W), F32), jax.ShapeDtypeStruct((1, D_MODEL), F32),
                   jax.ShapeDtypeStruct((1, D_MODEL), F32), jax.ShapeDtypeStruct((1, 128), F32)),
        compiler_params=_cparams(1),
    )(x, tgt, o, ga, u, u, gb, w_out, pool_w, pool_scale, ln_g, ln_b)


def _attn_bwd(q, k, v, do, lse, delta, pos_col, pos_row, bounds, nb, S, tq, tk):
    T = q.shape[0]
    nq, nk = S // tq, S // tk
    reps = tk // 128
    hg = HEAD_GROUP

    def body(qmin_ref, qmax_ref, kmin_ref, kmax_ref, q_ref, k_ref, v_ref, do_ref, lse_ref, dl_ref, pc_ref, pr_ref,
             dq_ref, dk_ref, dv_ref):
        b, j = pl.program_id(0), pl.program_id(2)

        @pl.when(j == 0)
        def _():
            dq_ref[...] = jnp.zeros_like(dq_ref)

        dk_ref[...] = jnp.zeros_like(dk_ref)
        dv_ref[...] = jnp.zeros_like(dv_ref)
        k_lo = kmin_ref[b * nk + j]
        k_hi = kmax_ref[b * nk + j]

        def tile(i, masked):
            rows = pl.ds(pl.multiple_of(i * tq, tq), tq)
            if masked:
                keep = pc_ref[rows, :] >= pr_ref[...]
            stage = []
            for g in range(hg):
                qk = slice(g * HEAD_PAD, (g + 1) * HEAD_PAD)
                hv = slice(g * 128, (g + 1) * 128)
                s = _dot_nt(q_ref[rows, qk], k_ref[:, qk])
                if masked:
                    s = jnp.where(keep, s, NEG)
                stage.append((s, _dot_nt(do_ref[rows, hv], v_ref[:, hv])))
            grads = []
            for g in range(hg):
                hv = slice(g * 128, (g + 1) * 128)
                s, dp = stage[g]
                p = jnp.exp2(s - jnp.concatenate([lse_ref[rows, hv]] * reps, axis=1))
                ds = (p * (dp - jnp.concatenate([dl_ref[rows, hv]] * reps, axis=1))).astype(BF16)
                grads.append((p.astype(BF16), ds))
            for g in range(hg):
                qk = slice(g * HEAD_PAD, (g + 1) * HEAD_PAD)
                hv = slice(g * 128, (g + 1) * 128)
                p, ds = grads[g]
                dv_ref[:, hv] += _dot_tn(p, do_ref[rows, hv])
                dq_ref[rows, qk] += _dot(ds, k_ref[:, qk])
                dk_ref[:, qk] += _dot_tn(ds, q_ref[rows, qk])

        def step(i, carry):
            visible = k_lo <= qmax_ref[b * nq + i]
            clear = qmin_ref[b * nq + i] >= k_hi

            @pl.when(jnp.logical_and(visible, clear))
            def _():
                tile(i, False)

            @pl.when(jnp.logical_and(visible, jnp.logical_not(clear)))
            def _():
                tile(i, True)
            return carry

        lax.fori_loop(0, nq, step, 0)

    ng = HEADS // hg
    seq = lambda w: pl.BlockSpec((S, w), lambda b, h, j, *_: (b, h))
    blk = lambda w: pl.BlockSpec((tk, w), lambda b, h, j, *_: (b * nk + j, h))
    return pl.pallas_call(
        body, name="attn_bwd",
        grid_spec=pltpu.PrefetchScalarGridSpec(
            num_scalar_prefetch=4, grid=(nb, ng, nk),
            in_specs=[seq(hg * HEAD_PAD), blk(hg * HEAD_PAD), blk(hg * 128),
                      seq(hg * 128), seq(hg * 128), seq(hg * 128),
                      pl.BlockSpec((S, 1), lambda b, h, j, *_: (b, 0)),
                      pl.BlockSpec((None, 1, tk), lambda b, h, j, *_: (b, 0, j))],
            out_specs=(seq(hg * HEAD_PAD), blk(hg * HEAD_PAD), blk(hg * 128))),
        out_shape=(jax.ShapeDtypeStruct((T, HEADS * HEAD_PAD), F32),
                   jax.ShapeDtypeStruct((T, HEADS * HEAD_PAD), F32),
                   jax.ShapeDtypeStruct((T, MLA_W), F32)),
        compiler_params=_cparams(3),
    )(*bounds, q, k, v, do, lse, delta, pos_col, pos_row)


def _bwd_proj(dq, dk, dv, xq, xkv, x, dz, dga, dgb, dpc, rc, rsa, rsb, w_uq_t, w_ukv, w_in_t, gq, gkv, S, tm):
    T = x.shape[0]
    tps = S // tm
    hb = tm // HALO
    n_tiles = T // tm

    def body(dq_ref, dk_ref, dv_ref, xq_ref, xkv_ref, x_ref, dz_ref, dga_ref, dgb_ref, dpc_ref, dph_ref,
             c_ref, sa_ref, sb_ref, wuq_ref, wukv_ref, win_ref, gq_ref, gkv_ref,
             dx_ref, dwin_hbm, dwuq_hbm, dwukv_hbm, dgq_ref, dgkv_ref,
             acc_win, acc_wuq, acc_wukv, dh_sc):
        i = pl.program_id(0)

        @pl.when(i == 0)
        def _():
            acc_win[...] = jnp.zeros_like(acc_win)
            acc_wuq[...] = jnp.zeros_like(acc_wuq)
            acc_wukv[...] = jnp.zeros_like(acc_wukv)
            dgq_ref[...] = jnp.zeros_like(dgq_ref)
            dgkv_ref[...] = jnp.zeros_like(dgkv_ref)
            dh_sc[...] = jnp.zeros_like(dh_sc)

        dh_prev = dh_sc[...]
        dx_ref[...] = ALPHA * dz_ref[...] + _dot(dh_prev, win_ref[...])
        acc_win[...] += _dot_tn(dh_prev, x_ref[...].astype(BF16))

        live = jnp.where(i < n_tiles, 1.0, 0.0)
        c, sa, sb = c_ref[...], sa_ref[...], sb_ref[...]
        dq_v = dq_ref[...] * (SCALE * live)
        dk_v = dk_ref[...] * (LN2 * live)
        dv_v = dv_ref[...] * live
        dq_parts, dkv_parts = [], []
        dkr = jnp.zeros((tm, 128), F32)
        for hh in range(HEADS):
            b0 = hh * HEAD_PAD
            dq_parts.append(dq_v[:, b0:b0 + 128].astype(BF16))
            dq_parts.append(_rope(dq_v[:, b0 + 128:b0 + 256], c, sa, sb, -1.0).astype(BF16))
            dkv_parts.append(dk_v[:, b0:b0 + 128].astype(BF16))
            dkv_parts.append(dv_v[:, hh * 128:(hh + 1) * 128].astype(BF16))
            dkr = dkr + dk_v[:, b0 + 128:b0 + 256]
        dqp = jnp.concatenate(dq_parts, axis=1)
        dkvp = jnp.concatenate(dkv_parts, axis=1)
        dkrr = _rope(dkr, c, sa, sb, -1.0)

        def rms_bwd(xv, g, dyn, dg_ref):
            r = lax.rsqrt(jnp.mean(xv * xv, axis=-1, keepdims=True) + RMS_EPS)
            xhat = xv * r
            dg_ref[...] += jnp.sum(dyn * xhat, axis=0, keepdims=True)
            dxh = dyn * g
            return r * (dxh - xhat * jnp.mean(dxh * xhat, axis=-1, keepdims=True))

        xq_v = xq_ref[...]
        gq_v = gq_ref[...]
        rq = lax.rsqrt(jnp.mean(xq_v * xq_v, axis=-1, keepdims=True) + RMS_EPS)
        acc_wuq[...] += _dot_tn(dqp, ((xq_v * rq) * gq_v).astype(BF16))
        dxq = rms_bwd(xq_v, gq_v, _dot(dqp, wuq_ref[...]), dgq_ref)

        xkv_v = xkv_ref[...]
        gkv_v = gkv_ref[...]
        rkv = lax.rsqrt(jnp.mean(xkv_v * xkv_v, axis=-1, keepdims=True) + RMS_EPS)
        acc_wukv[...] += _dot_tn(((xkv_v * rkv) * gkv_v).astype(BF16), dkvp)
        dxkv = rms_bwd(xkv_v, gkv_v, _dot_nt(dkvp, wukv_ref[...]), dgkv_ref)

        seq_tile = i % tps
        tpos = seq_tile * tm + lax.broadcasted_iota(jnp.int32, (tm, 1), 0)
        dpc_v = dpc_ref[...]
        halo = jnp.where(seq_tile == tps - 1, 0.0, dph_ref[...])
        n = tm + HALO
        du = []
        for g in range(POOL_G):
            lanes = slice(g * POOL_GD, (g + 1) * POOL_GD)
            f = jnp.concatenate([dpc_v[:, lanes], halo[:, lanes]], axis=0)
            for st in range(g + 1):
                f = f + pltpu.roll(f, n - (1 << st), 0)
            cnt = jnp.minimum(tpos + 1, 2 << g).astype(F32)
            du.append((f[:tm, :] - dpc_v[:, lanes] * cnt).astype(BF16))

        dh_sc[...] = jnp.concatenate([dxq.astype(BF16), dxkv.astype(BF16), dkrr.astype(BF16), dga_ref[...]]
                                     + du + [dgb_ref[...]], axis=1)

        @pl.when(i == n_tiles)
        def _():
            pltpu.sync_copy(acc_win.at[pl.ds(0, 832)], dwin_hbm.at[pl.ds(0, 832)])
            pltpu.sync_copy(acc_win.at[pl.ds(896, IN_EXT - 896)], dwin_hbm.at[pl.ds(832, IN_W - 832)])
            for hh in range(HEADS):
                pltpu.sync_copy(acc_wuq.at[pl.ds(hh * HEAD_PAD, NOPE + ROPE)], dwuq_hbm.at[hh])
            pltpu.sync_copy(acc_wukv, dwukv_hbm)

    cur = lambda w: pl.BlockSpec((tm, w), lambda i: (jnp.minimum(i, n_tiles - 1), 0))
    prev = lambda w: pl.BlockSpec((tm, w), lambda i: (jnp.maximum(i - 1, 0), 0))
    halo_spec = pl.BlockSpec((HALO, POOL_W), lambda i: (jnp.minimum((i + 1) * hb, T // HALO - 1), 0))
    return pl.pallas_call(
        body, name="bwd_proj", grid=(n_tiles + 1,),
        in_specs=[cur(1024), cur(1024), cur(512), cur(512), cur(256), prev(D_MODEL), prev(D_MODEL),
                  cur(512), cur(512), cur(512), halo_spec, cur(128), cur(128), cur(128),
                  _full(w_uq_t.shape), _full(w_ukv.shape), _full(w_in_t.shape), _full(gq.shape), _full(gkv.shape)],
        out_specs=(prev(D_MODEL), ANY, ANY, ANY, _full((1, Q_LORA)), _full((1, KV_LORA))),
        out_shape=(jax.ShapeDtypeStruct((T, D_MODEL), F32),
                   jax.ShapeDtypeStruct((IN_W, D_MODEL), F32),
                   jax.ShapeDtypeStruct((HEADS, NOPE + ROPE, Q_LORA), F32),
                   jax.ShapeDtypeStruct((KV_LORA, 1024), F32),
                   jax.ShapeDtypeStruct((1, Q_LORA), F32), jax.ShapeDtypeStruct((1, KV_LORA), F32)),
        scratch_shapes=[pltpu.VMEM((IN_EXT, D_MODEL), F32), pltpu.VMEM((HEADS * HEAD_PAD, Q_LORA), F32),
                        pltpu.VMEM((KV_LORA, 1024), F32), pltpu.VMEM((tm, IN_EXT), BF16)],
        compiler_params=_cparams(1),
    )(dq, dk, dv, xq, xkv, x, dz, dga, dgb, dpc, dpc, rc, rsa, rsb, w_uq_t, w_ukv, w_in_t, gq, gkv)


def kernel(x, positions, w_in, q_norm_g, w_uq, kv_norm_g, w_ukv, pool_w, pool_scale, w_out, ln_g, ln_b, loss_target, m_w_in, m_q_norm_g, m_w_uq, m_kv_norm_g, m_w_ukv, m_pool_w, m_pool_scale, m_w_out, m_ln_g, m_ln_b, v_w_in, v_q_norm_g, v_w_uq, v_kv_norm_g, v_w_ukv, v_pool_w, v_pool_scale, v_w_out, v_ln_g, v_ln_b):
    nb, S, _ = x.shape
    T = nb * S
    tm = min(256, S)
    tq = min(512, S)
    tk = min(512, S)
    assert S % tm == 0 and tm % HALO == 0 and S % tq == 0 and S % tk == 0

    cx, cy, cc = lax.axis_index("x"), lax.axis_index("y"), lax.axis_index("c")
    me = 2 * cx + cy
    place_arr = jnp.stack([me, cc]).astype(jnp.int32)

    def own_slot(w, slot_rows):
        blk = jnp.pad(w.astype(BF16), ((0, slot_rows - w.shape[0]), (0, 0)))
        return lax.dynamic_update_slice(jnp.zeros((N_CHIPS,) + blk.shape, BF16), blk[None], (me, 0, 0))

    w_in_g, w_uq_g, w_ukv_g, w_out_g = _weight_gather(
        [own_slot(w_in.T, 592), own_slot(w_uq.T, HEAD_PAD), own_slot(w_ukv, KV_LORA), own_slot(w_out, 256)],
        (592, NOPE + ROPE, KV_LORA, 256))
    w_in_f = w_in_g.reshape(IN_W, D_MODEL)
    w_in_t = jnp.concatenate([w_in_f[:832], jnp.zeros((64, D_MODEL), BF16), w_in_f[832:]], axis=0)
    w_uq_t = w_uq_g.reshape(HEADS * HEAD_PAD, Q_LORA)
    w_ukv_f = w_ukv_g.transpose(1, 0, 2).reshape(KV_LORA, 1024)
    w_out_f = w_out_g.reshape(D_MODEL, D_MODEL)
    pool_w_b = pool_w.astype(BF16)
    gq2 = q_norm_g.reshape(1, Q_LORA)
    gkv2 = kv_norm_g.reshape(1, KV_LORA)
    ps2 = pool_scale.reshape(1, POOL_W)

    half = ROPE // 2
    inv_freq = ROPE_THETA ** (-jnp.arange(half, dtype=F32) / half)
    freq_row = jnp.concatenate([inv_freq, inv_freq, jnp.zeros((2 * half,), F32)]).reshape(1, 128)
    pos_col = positions.reshape(T, 1)
    pos_row = positions.reshape(nb, 1, S)
    pos_q = positions.reshape(nb, S // tq, tq)
    pos_k = positions.reshape(nb, S // tk, tk)
    bounds = (jnp.min(pos_q, axis=2).reshape(-1), jnp.max(pos_q, axis=2).reshape(-1),
              jnp.min(pos_k, axis=2).reshape(-1), jnp.max(pos_k, axis=2).reshape(-1))

    xf = x.reshape(T, D_MODEL)
    tgt = loss_target.reshape(T, D_MODEL)

    xq, xkv, ga, u, gb, q, k, v, rc, rsa, rsb = _fwd_proj(xf, w_in_t, w_uq_t, w_ukv_f, gq2, gkv2, pos_col, freq_row, tm)
    o, lse = _attn_fwd(q, k, v, pos_col, pos_row, bounds, nb, S, tq, tk)

    (dz, do, delta, dga, dgb, dpc, d_w_out, d_pool_w, d_pool_scale, d_ln_g, d_ln_b, loss_part) = _mid(
        xf, tgt, o, ga, u, gb, w_out_f, pool_w_b, ps2, ln_g, ln_b, S, tm)

    dq, dk, dv = _attn_bwd(q, k, v, do, lse, delta, pos_col, pos_row, bounds, nb, S, tq, tk)
    dx, d_w_in_t, d_w_uq_t, d_w_ukv, d_gq, d_gkv = _bwd_proj(
        dq, dk, dv, xq, xkv, xf, dz, dga, dgb, dpc, rc, rsa, rsb, w_uq_t, w_ukv_f, w_in_t, gq2, gkv2, S, tm)
    grad_x = dx.reshape(nb, S, D_MODEL)

    g_in = d_w_in_t.reshape(N_CHIPS, 592, D_MODEL)
    g_uq = d_w_uq_t
    g_ukv = d_w_ukv.reshape(KV_LORA, N_CHIPS, 256).transpose(1, 0, 2)
    g_out = d_w_out.reshape(N_CHIPS, 256, D_MODEL)
    gs = [g_in, g_uq, g_ukv, g_out]
    from_sibling = _grad_to_sibling(gs)
    chip_sums, own_sums = _add_sibling_half(gs, from_sibling, place_arr)
    from_chips = _grad_to_chips(chip_sums)
    g_big = _halves_exchange(_add_chip_parts(own_sums, from_chips, place_arr))

    pw_sum, vec_sum = _small_allreduce(d_pool_w, d_ln_g, d_ln_b, d_pool_scale, d_gq, d_gkv, loss_part)

    big = _adamw_big(g_big, [w_in.T, w_uq.T, w_ukv, w_out], [m_w_in.T, m_w_uq.T, m_w_ukv, m_w_out],
                     [v_w_in.T, v_w_uq.T, v_w_ukv, v_w_out])
    two_d = lambda a: a.reshape(-1, a.shape[-1])
    small_names = lambda pw, lg, lb, ps, gq, gkv: [two_d(pw), lg, lb, ps.reshape(1, -1), gq.reshape(1, -1), gkv.reshape(1, -1)]
    small, loss_row = _adamw_small(
        pw_sum, vec_sum,
        small_names(pool_w, ln_g, ln_b, pool_scale, q_norm_g, kv_norm_g),
        small_names(m_pool_w, m_ln_g, m_ln_b, m_pool_scale, m_q_norm_g, m_kv_norm_g),
        small_names(v_pool_w, v_ln_g, v_ln_b, v_pool_scale, v_q_norm_g, v_kv_norm_g))
    loss = loss_row[0, 0]

    def leaves(kind):
        b = [g_big[t] if kind == 0 else big[t][kind - 1] for t in range(N_BIG)]
        b = [b[0].T, b[1].T, b[2], b[3]]
        s = [small[t][kind] for t in range(6)]
        return (b[0], s[4].reshape(Q_LORA), b[1], s[5].reshape(KV_LORA), b[2],
                s[0].reshape(POOL_G, POOL_GD, POOL_GD), s[3].reshape(POOL_W), b[3], s[1], s[2])

    return (loss, grad_x) + leaves(0) + leaves(1) + leaves(2) + leaves(3)
```

```python
import functools

import jax
import jax.numpy as jnp
from jax import lax
from jax.experimental import pallas as pl
from jax.experimental.pallas import tpu as pltpu

F32 = jnp.float32
BF16 = jnp.bfloat16
MESH = pl.DeviceIdType.MESH

HEADS = 4
NOPE = 128
ROPE = 64
HEAD_PAD = 256
Q_LORA = 512
KV_LORA = 256
MLA_W = 512
POOL_W = 512
POOL_G = 4
POOL_GD = 128
D_MODEL = 1024
IN_W = 2368
IN_EXT = 2432
ROPE_THETA = 10000.0
RMS_EPS = 1e-6
LN_EPS = 1e-5
ALPHA = 2.0 ** 0.25
SCALE = 192.0 ** -0.5
LOG2E = 1.4426950408889634
LN2 = 0.6931471805599453
QSCALE = SCALE * LOG2E
NEG = float(jnp.finfo(jnp.float32).min)
HEAD_GROUP = 2
HALO = 16

ADAM_LR = 0.001
ADAM_B1 = 0.9
ADAM_B2 = 0.999
ADAM_EPS = 1e-08
ADAM_WD = 0.01
ADAM_STEP = 10

N_CHIPS = 4
N_BIG = 4
VEC_ROWS = 16
VEC_HALF = VEC_ROWS // 2

VMEM_LIMIT = 56 * 1024 * 1024


def _cparams(n_grid_dims=0, **kw):
    sem = ("arbitrary",) * n_grid_dims if n_grid_dims else None
    return pltpu.CompilerParams(dimension_semantics=sem, vmem_limit_bytes=VMEM_LIMIT, **kw)


def _full(shape):
    nd = len(shape)
    return pl.BlockSpec(shape, lambda *_: (0,) * nd)


def _dot(a, b):
    return jnp.dot(a, b, preferred_element_type=F32)


def _dot_nt(a, b):
    return lax.dot_general(a, b, (((1,), (1,)), ((), ())), preferred_element_type=F32)


def _dot_tn(a, b):
    return lax.dot_general(a, b, (((0,), (0,)), ((), ())), preferred_element_type=F32)


def _rope_table(pos_col, freq_row):
    lane = lax.broadcasted_iota(jnp.int32, (1, 128), 1)
    ang = pos_col.astype(F32) * freq_row
    return jnp.where(lane < 32, jnp.cos(ang), jnp.where(lane < 64, jnp.sin(ang), 0.0))


def _expand_rope_table(tab):
    lane = lax.broadcasted_iota(jnp.int32, (1, 128), 1)
    second = jnp.logical_and(lane >= 32, lane < 64)
    c = jnp.where(lane < 32, tab, jnp.where(second, pltpu.roll(tab, 32, 1), 0.0))
    sa = jnp.where(lane < 32, pltpu.roll(tab, 96, 1), 0.0)
    sb = jnp.where(second, tab, 0.0)
    return c, sa, sb


def _rope(g, c, sa, sb, sign):
    return g * c + sign * (pltpu.roll(g, 32, 1) * sb - pltpu.roll(g, 96, 1) * sa)


def _place():
    x, y, c = lax.axis_index("x"), lax.axis_index("y"), lax.axis_index("c")
    chips = [(1 - x, y), (x, 1 - y), (1 - x, 1 - y)]
    return x, y, c, chips


def _half_cols(ref, half_index):
    hc = ref.shape[-1] // 2
    lead = tuple(pl.ds(0, n) for n in ref.shape[:-1])
    return ref.at[lead + (pl.ds(half_index * hc, hc),)]


ANY = pl.BlockSpec(memory_space=pl.ANY)


def _weight_gather(slots, valid_rows):
    n = len(slots)

    def body(*refs):
        outs = refs[n:2 * n]
        send_sems, recv_sems = refs[2 * n:]
        x, y, c, chips = _place()
        me = 2 * x + y

        def copy(t, k, chip_idx, half, to):
            hc = slots[t].shape[2] // 2
            blk = outs[t].at[chip_idx, pl.ds(0, valid_rows[t]), pl.ds(half * hc, hc)]
            return pltpu.make_async_remote_copy(
                src_ref=blk, dst_ref=blk, send_sem=send_sems.at[6 * t + k], recv_sem=recv_sems.at[6 * t + k],
                device_id=to, device_id_type=MESH)

        first = [copy(t, j, me, c, (cx, cy, c)) for t in range(n) for j, (cx, cy) in enumerate(chips)]
        for cp in first:
            cp.start()
        passed = []
        for j, (cx, cy) in enumerate(chips):
            for t in range(n):
                copy(t, j, 2 * cx + cy, c, (x, y, c)).wait_recv()
                fwd = copy(t, 3 + j, 2 * cx + cy, c, (x, y, 1 - c))
                fwd.start()
                passed.append(fwd)
        for j, (cx, cy) in enumerate(chips):
            for t in range(n):
                copy(t, 3 + j, 2 * cx + cy, 1 - c, (x, y, c)).wait_recv()
        for cp in first + passed:
            cp.wait_send()

    return pl.pallas_call(
        body, name="weight_gather",
        out_shape=tuple(jax.ShapeDtypeStruct(a.shape, a.dtype) for a in slots),
        in_specs=[ANY] * n, out_specs=(ANY,) * n, input_output_aliases={t: t for t in range(n)},
        scratch_shapes=[pltpu.SemaphoreType.DMA((6 * n,)), pltpu.SemaphoreType.DMA((6 * n,))],
    )(*slots)


def _grad_to_sibling(gs):
    n = len(gs)

    def body(*refs):
        g_refs, r_refs = refs[:n], refs[n:2 * n]
        send_sems, recv_sems = refs[2 * n:]
        x, y, c, _ = _place()
        cps = []
        for t in range(n):
            cp = pltpu.make_async_remote_copy(
                src_ref=_half_cols(g_refs[t], 1 - c), dst_ref=r_refs[t], send_sem=send_sems.at[t], recv_sem=recv_sems.at[t],
                device_id=(x, y, 1 - c), device_id_type=MESH)
            cp.start()
            cps.append(cp)
        for cp in cps:
            cp.wait()

    return pl.pallas_call(
        body, name="grad_to_sibling",
        out_shape=tuple(jax.ShapeDtypeStruct((N_CHIPS, g.shape[1], g.shape[2] // 2), F32) for g in gs),
        in_specs=[ANY] * n, out_specs=(ANY,) * n,
        scratch_shapes=[pltpu.SemaphoreType.DMA((n,)), pltpu.SemaphoreType.DMA((n,))],
    )(*gs)


def _grad_to_chips(ss):
    n = len(ss)

    def body(*refs):
        s_refs, r_refs = refs[:n], refs[n:2 * n]
        send_sems, recv_sems = refs[2 * n:]
        x, y, c, chips = _place()
        cps = []
        for t in range(n):
            for j, (cx, cy) in enumerate(chips):
                cp = pltpu.make_async_remote_copy(
                    src_ref=s_refs[t].at[2 * cx + cy], dst_ref=r_refs[t].at[j],
                    send_sem=send_sems.at[3 * t + j], recv_sem=recv_sems.at[3 * t + j],
                    device_id=(cx, cy, c), device_id_type=MESH)
                cp.start()
                cps.append(cp)
        for cp in cps:
            cp.wait()

    return pl.pallas_call(
        body, name="grad_to_chips",
        out_shape=tuple(jax.ShapeDtypeStruct((3,) + s.shape[1:], s.dtype) for s in ss),
        in_specs=[ANY] * n, out_specs=(ANY,) * n,
        scratch_shapes=[pltpu.SemaphoreType.DMA((3 * n,)), pltpu.SemaphoreType.DMA((3 * n,))],
    )(*ss)


def _halves_exchange(fs):
    n = len(fs)

    def body(*refs):
        o_refs = refs[n:2 * n]
        send_sems, recv_sems = refs[2 * n:]
        x, y, c, _ = _place()
        sib = (x, y, 1 - c)
        cps = []
        for t in range(n):
            mine = _half_cols(o_refs[t], c)
            cp = pltpu.make_async_remote_copy(
                src_ref=mine, dst_ref=mine, send_sem=send_sems.at[t], recv_sem=recv_sems.at[t],
                device_id=sib, device_id_type=MESH)
            cp.start()
            cps.append(cp)
        for t in range(n):
            theirs = _half_cols(o_refs[t], 1 - c)
            pltpu.make_async_remote_copy(
                src_ref=theirs, dst_ref=theirs, send_sem=send_sems.at[t], recv_sem=recv_sems.at[t],
                device_id=sib, device_id_type=MESH).wait_recv()
        for cp in cps:
            cp.wait_send()

    return pl.pallas_call(
        body, name="halves_exchange",
        out_shape=tuple(jax.ShapeDtypeStruct(f.shape, f.dtype) for f in fs),
        in_specs=[ANY] * n, out_specs=(ANY,) * n, input_output_aliases={t: t for t in range(n)},
        scratch_shapes=[pltpu.SemaphoreType.DMA((n,)), pltpu.SemaphoreType.DMA((n,))],
    )(*fs)


def _small_allreduce(d_pool_w, d_ln_g, d_ln_b, d_ps, d_gq, d_gkv, loss_part):
    pw_rows = POOL_G * POOL_GD

    def body(pw_in, lng_in, lnb_in, ps_in, gq_in, gkv_in, loss_in, pw_out, vec_out,
             vec_in, pw_sib, vec_sib, pw_sum, vec_sum, pw_chip, vec_chip, send_sems, recv_sems):
        x, y, c, chips = _place()
        sib = (x, y, 1 - c)
        vec_in[...] = jnp.zeros_like(vec_in)
        vec_in[0:1, :] = lng_in[...]
        vec_in[1:2, :] = lnb_in[...]
        vec_in[2:3, 0:POOL_W] = ps_in[...]
        vec_in[3:4, 0:Q_LORA] = gq_in[...]
        vec_in[8:9, 0:KV_LORA] = gkv_in[...]
        vec_in[9:10, 0:128] = loss_in[...]

        def rdma(k, src, dst, to):
            return pltpu.make_async_remote_copy(src_ref=src, dst_ref=dst, send_sem=send_sems.at[k],
                                                recv_sem=recv_sems.at[k], device_id=to, device_id_type=MESH)

        a = [rdma(0, pw_in, pw_sib, sib), rdma(1, vec_in, vec_sib, sib)]
        for cp in a:
            cp.start()
        for cp in a:
            cp.wait()
        pw_sum[...] = pw_in[...] + pw_sib[...]
        vec_sum[...] = vec_in[...] + vec_sib[...]

        bufs = [(pw_sum, pw_chip, pw_out, pw_rows // 2), (vec_sum, vec_chip, vec_out, VEC_HALF)]
        cps = []
        for t, (sm, chip_buf, _, hr) in enumerate(bufs):
            rows = pl.ds(pl.multiple_of(c * hr, 8), hr)
            for j, (cx, cy) in enumerate(chips):
                cp = rdma(2 + 3 * t + j, sm.at[rows], chip_buf.at[j], (cx, cy, c))
                cp.start()
                cps.append(cp)
        for cp in cps:
            cp.wait()
        last = []
        for t, (sm, chip_buf, out, hr) in enumerate(bufs):
            rows = pl.ds(pl.multiple_of(c * hr, 8), hr)
            other = pl.ds(pl.multiple_of((1 - c) * hr, 8), hr)
            out[rows, :] = (sm[rows, :] + chip_buf[0]) + (chip_buf[1] + chip_buf[2])
            cp = rdma(8 + t, out.at[rows], out.at[rows], sib)
            cp.start()
            last.append((cp, rdma(8 + t, out.at[other], out.at[other], sib)))
        for cp, recv in last:
            recv.wait_recv()
            cp.wait_send()

    vm = pl.BlockSpec(memory_space=pltpu.VMEM)
    vec_shape = (VEC_ROWS, D_MODEL)
    return pl.pallas_call(
        body, name="small_allreduce",
        out_shape=(jax.ShapeDtypeStruct((pw_rows, POOL_GD), F32), jax.ShapeDtypeStruct(vec_shape, F32)),
        in_specs=[vm] * 7, out_specs=(vm, vm),
        scratch_shapes=[pltpu.VMEM(vec_shape, F32), pltpu.VMEM((pw_rows, POOL_GD), F32), pltpu.VMEM(vec_shape, F32),
                        pltpu.VMEM((pw_rows, POOL_GD), F32), pltpu.VMEM(vec_shape, F32),
                        pltpu.VMEM((3, pw_rows // 2, POOL_GD), F32), pltpu.VMEM((3, VEC_HALF, D_MODEL), F32),
                        pltpu.SemaphoreType.DMA((10,)), pltpu.SemaphoreType.DMA((10,))],
    )(d_pool_w.reshape(pw_rows, POOL_GD), d_ln_g, d_ln_b, d_ps, d_gq, d_gkv, loss_part)


def _add_sibling_half(gs, rs, place_arr):
    n = len(gs)

    def body(place_ref, *refs):
        k = pl.program_id(0)
        for t in range(n):
            total = refs[t][...] + refs[n + t][...]
            refs[2 * n + 2 * t][...] = total.astype(BF16)

            @pl.when(k == place_ref[0])
            def _():
                refs[2 * n + 2 * t + 1][...] = total

    in_specs, out_specs, out_shape = [], [], []
    for g in gs:
        in_specs.append(pl.BlockSpec((None, g.shape[1], g.shape[2] // 2), lambda k, p: (k, 0, p[1])))
    for r in rs:
        blk = pl.BlockSpec((None,) + r.shape[1:], lambda k, p: (k, 0, 0))
        in_specs.append(blk)
        out_specs += [blk, pl.BlockSpec(r.shape[1:], lambda k, p: (0, 0))]
        out_shape += [jax.ShapeDtypeStruct(r.shape, BF16), jax.ShapeDtypeStruct(r.shape[1:], F32)]
    outs = pl.pallas_call(
        body, name="add_sibling_half", out_shape=tuple(out_shape),
        grid_spec=pltpu.PrefetchScalarGridSpec(num_scalar_prefetch=1, grid=(N_CHIPS,),
                                               in_specs=in_specs, out_specs=tuple(out_specs)),
        compiler_params=_cparams(1),
    )(place_arr, *gs, *rs)
    return list(outs[0::2]), list(outs[1::2])


def _add_chip_parts(owns, rs, place_arr):
    n = len(owns)

    def body(place_ref, *refs):
        for t in range(n):
            r_ref = refs[n + t]
            refs[2 * n + t][...] = ((refs[t][...] + r_ref[0].astype(F32))
                                    + (r_ref[1].astype(F32) + r_ref[2].astype(F32)))

    in_specs, out_specs = [], []
    for o in owns:
        in_specs.append(pl.BlockSpec(o.shape, lambda i, p: (0, 0)))
    for r in rs:
        in_specs.append(pl.BlockSpec(r.shape, lambda i, p: (0, 0, 0)))
        out_specs.append(pl.BlockSpec(r.shape[1:], lambda i, p: (0, p[1])))
    return pl.pallas_call(
        body, name="add_chip_parts",
        out_shape=tuple(jax.ShapeDtypeStruct((o.shape[0], 2 * o.shape[1]), F32) for o in owns),
        grid_spec=pltpu.PrefetchScalarGridSpec(num_scalar_prefetch=1, grid=(1,),
                                               in_specs=in_specs, out_specs=tuple(out_specs)),
        compiler_params=_cparams(1),
    )(place_arr, *owns, *rs)


def _adamw_math(g, w, m, v):
    nm = ADAM_B1 * m + (1.0 - ADAM_B1) * g
    nv = ADAM_B2 * v + (1.0 - ADAM_B2) * (g * g)
    m_hat = nm / (1.0 - ADAM_B1 ** ADAM_STEP)
    v_hat = nv / (1.0 - ADAM_B2 ** ADAM_STEP)
    return -ADAM_LR * (m_hat / (jnp.sqrt(v_hat) + ADAM_EPS) + ADAM_WD * w), nm, nv


ADAM_STEPS = 8


def _adamw_big(gs, ws, ms, vs):
    n = len(gs)

    def body(*refs):
        for t in range(n):
            d, nm, nv = _adamw_math(refs[t][...], refs[n + t][...], refs[2 * n + t][...], refs[3 * n + t][...])
            refs[4 * n + 3 * t][...] = d
            refs[4 * n + 3 * t + 1][...] = nm
            refs[4 * n + 3 * t + 2][...] = nv

    def tile_spec(shape):
        rows, cols = shape
        if rows % (8 * ADAM_STEPS) == 0:
            return pl.BlockSpec((rows // ADAM_STEPS, cols), lambda i: (i, 0))
        return pl.BlockSpec((rows, cols // ADAM_STEPS), lambda i: (0, i))

    specs = [tile_spec(g.shape) for g in gs]
    out_specs, out_shape = [], []
    for t in range(n):
        out_specs += [specs[t]] * 3
        out_shape += [jax.ShapeDtypeStruct(gs[t].shape, F32)] * 3
    outs = pl.pallas_call(
        body, name="adamw_big", grid=(ADAM_STEPS,),
        in_specs=specs * 4, out_specs=tuple(out_specs), out_shape=tuple(out_shape),
        compiler_params=_cparams(1),
    )(*gs, *ws, *ms, *vs)
    return [outs[3 * t: 3 * t + 3] for t in range(n)]


def _adamw_small(pw_sum, vec_sum, ws, ms, vs):
    rows = (None, 0, 1, 2, 3, 8)
    n = len(ws)

    def body(pw_ref, vec_ref, *refs):
        outs = refs[3 * n:]
        for t in range(n):
            w_ref, m_ref, v_ref = refs[t], refs[n + t], refs[2 * n + t]
            if rows[t] is None:
                g = pw_ref[...]
            else:
                g = vec_ref[rows[t]:rows[t] + 1, 0:w_ref.shape[1]]
            d, nm, nv = _adamw_math(g, w_ref[...], m_ref[...], v_ref[...])
            outs[4 * t][...] = g
            outs[4 * t + 1][...] = d
            outs[4 * t + 2][...] = nm
            outs[4 * t + 3][...] = nv
        outs[4 * n][...] = vec_ref[9:10, 0:128]

    vm = pl.BlockSpec(memory_space=pltpu.VMEM)
    out_shape = []
    for w in ws:
        out_shape += [jax.ShapeDtypeStruct(w.shape, F32)] * 4
    out_shape.append(jax.ShapeDtypeStruct((1, 128), F32))
    outs = pl.pallas_call(
        body, name="adamw_small", in_specs=[vm] * (2 + 3 * n), out_specs=(vm,) * (4 * n + 1),
        out_shape=tuple(out_shape),
    )(pw_sum, vec_sum, *ws, *ms, *vs)
    return [outs[4 * t: 4 * t + 4] for t in range(n)], outs[4 * n]


def _fwd_proj(x, w_in_t, w_uq_t, w_ukv, gq, gkv, pos_col, freq_row, tm):
    T = x.shape[0]

    def body(x_ref, win_ref, wuq_ref, wukv_ref, gq_ref, gkv_ref, pos_ref, freq_ref,
             xq_ref, xkv_ref, ga_ref, u_ref, gb_ref, q_ref, k_ref, v_ref, tab_ref):
        h = _dot_nt(x_ref[...].astype(BF16), win_ref[...])
        xq = h[:, 0:512]
        xkv = h[:, 512:768]
        xq_ref[...] = xq.astype(BF16)
        xkv_ref[...] = xkv.astype(BF16)
        ga_ref[...] = h[:, 896:1408].astype(BF16)
        u_ref[...] = h[:, 1408:1920].astype(BF16)
        gb_ref[...] = h[:, 1920:2432].astype(BF16)
        tab = _rope_table(pos_ref[...], freq_ref[...])
        tab_ref[...] = tab
        c, sa, sb = _expand_rope_table(tab)
        rq = lax.rsqrt(jnp.mean(xq * xq, axis=-1, keepdims=True) + RMS_EPS)
        q = _dot_nt(((xq * rq) * gq_ref[...]).astype(BF16), wuq_ref[...]) * QSCALE
        rkv = lax.rsqrt(jnp.mean(xkv * xkv, axis=-1, keepdims=True) + RMS_EPS)
        kv = _dot(((xkv * rkv) * gkv_ref[...]).astype(BF16), wukv_ref[...])
        kr = _rope(h[:, 768:896], c, sa, sb, 1.0).astype(BF16)
        for hh in range(HEADS):
            b0 = hh * HEAD_PAD
            q_ref[:, b0:b0 + 128] = q[:, b0:b0 + 128].astype(BF16)
            q_ref[:, b0 + 128:b0 + 256] = _rope(q[:, b0 + 128:b0 + 256], c, sa, sb, 1.0).astype(BF16)
            k_ref[:, b0:b0 + 128] = kv[:, b0:b0 + 128].astype(BF16)
            k_ref[:, b0 + 128:b0 + 256] = kr
            v_ref[:, hh * 128:(hh + 1) * 128] = kv[:, b0 + 128:b0 + 256].astype(BF16)

    row = lambda w: pl.BlockSpec((tm, w), lambda i: (i, 0))
    f = lambda w, dt: jax.ShapeDtypeStruct((T, w), dt)
    return pl.pallas_call(
        body, name="fwd_proj", grid=(T // tm,),
        in_specs=[row(D_MODEL), _full(w_in_t.shape), _full(w_uq_t.shape), _full(w_ukv.shape),
                  _full(gq.shape), _full(gkv.shape), row(1), _full(freq_row.shape)],
        out_specs=(row(512), row(256), row(512), row(512), row(512), row(1024), row(1024), row(512), row(128)),
        out_shape=(f(512, BF16), f(256, BF16), f(512, BF16), f(512, BF16), f(512, BF16),
                   f(1024, BF16), f(1024, BF16), f(512, BF16), f(128, F32)),
        compiler_params=_cparams(1),
    )(x, w_in_t, w_uq_t, w_ukv, gq, gkv, pos_col, freq_row)


def _attn_fwd(q, k, v, pos_col, pos_row, bounds, nb, S, tq, tk):
    T = q.shape[0]
    nq, nk = S // tq, S // tk
    reps = tk // 128
    hg = HEAD_GROUP

    def body(qmin_ref, qmax_ref, kmin_ref, kmax_ref, q_ref, k_ref, v_ref, pc_ref, pr_ref, o_ref, lse_ref,
             m_sc, l_sc, acc_sc):
        b, i = pl.program_id(0), pl.program_id(2)
        m_sc[...] = jnp.full(m_sc.shape, NEG, F32)
        l_sc[...] = jnp.zeros_like(l_sc)
        acc_sc[...] = jnp.zeros_like(acc_sc)
        q_lo = qmin_ref[b * nq + i]
        q_hi = qmax_ref[b * nq + i]

        def tile(j, masked):
            off = pl.multiple_of(j * tk, tk)
            if masked:
                keep = pc_ref[...] >= pr_ref[pl.ds(j, 1), :]
            logits = []
            for g in range(hg):
                qk = slice(g * HEAD_PAD, (g + 1) * HEAD_PAD)
                s = _dot_nt(q_ref[:, qk], k_ref[pl.ds(off, tk), qk])
                if masked:
                    s = jnp.where(keep, s, NEG)
                logits.append(s)
            probs = []
            for g in range(hg):
                hv = slice(g * 128, (g + 1) * 128)
                s = logits[g]
                m_prev = m_sc[:, hv]
                m_new = jnp.maximum(m_prev, jnp.max(s, axis=1, keepdims=True))
                p = jnp.exp2(s - jnp.concatenate([m_new] * reps, axis=1))
                a = jnp.exp2(m_prev - m_new)
                l_sc[:, hv] = a * l_sc[:, hv] + jnp.sum(p, axis=1, keepdims=True)
                m_sc[:, hv] = m_new
                probs.append((p.astype(BF16), a))
            for g in range(hg):
                hv = slice(g * 128, (g + 1) * 128)
                p, a = probs[g]
                acc_sc[:, hv] = a * acc_sc[:, hv] + _dot(p, v_ref[pl.ds(off, tk), hv])

        def step(j, carry):
            visible = kmin_ref[b * nk + j] <= q_hi
            clear = q_lo >= kmax_ref[b * nk + j]

            @pl.when(jnp.logical_and(visible, clear))
            def _():
                tile(j, False)

            @pl.when(jnp.logical_and(visible, jnp.logical_not(clear)))
            def _():
                tile(j, True)
            return carry

        lax.fori_loop(0, nk, step, 0)
        l = l_sc[...]
        o_ref[...] = acc_sc[...] / l
        lse_ref[...] = m_sc[...] + jnp.log2(l)

    ng = HEADS // hg
    stat = pltpu.VMEM((tq, hg * 128), F32)
    return pl.pallas_call(
        body, name="attn_fwd",
        grid_spec=pltpu.PrefetchScalarGridSpec(
            num_scalar_prefetch=4, grid=(nb, ng, nq),
            in_specs=[pl.BlockSpec((tq, hg * HEAD_PAD), lambda b, h, i, *_: (b * nq + i, h)),
                      pl.BlockSpec((S, hg * HEAD_PAD), lambda b, h, i, *_: (b, h)),
                      pl.BlockSpec((S, hg * 128), lambda b, h, i, *_: (b, h)),
                      pl.BlockSpec((tq, 1), lambda b, h, i, *_: (b * nq + i, 0)),
                      pl.BlockSpec((None, nk, tk), lambda b, h, i, *_: (b, 0, 0))],
            out_specs=(pl.BlockSpec((tq, hg * 128), lambda b, h, i, *_: (b * nq + i, h)),
                       pl.BlockSpec((tq, hg * 128), lambda b, h, i, *_: (b * nq + i, h))),
            scratch_shapes=[stat, stat, stat]),
        out_shape=(jax.ShapeDtypeStruct((T, MLA_W), F32), jax.ShapeDtypeStruct((T, MLA_W), F32)),
        compiler_params=_cparams(3),
    )(*bounds, q, k, v, pos_col, pos_row.reshape(nb, nk, tk))


def _mid(x, tgt, o, ga, u, gb, w_out, pool_w, pool_scale, ln_g, ln_b, S, tm):
    T = x.shape[0]
    tps = S // tm
    hb = tm // HALO

    def body(x_ref, tgt_ref, o_ref, ga_ref, u_ref, uh_ref, gb_ref, wout_ref, pw_ref,
             ps_ref, lng_ref, lnb_ref,
             dz_ref, do_ref, delta_ref, dga_ref, dgb_ref, dpc_ref,
             dwout_ref, dpw_ref, dps_ref, dlng_ref, dlnb_ref, loss_ref):
        i = pl.program_id(0)

        @pl.when(i == 0)
        def _():
            dwout_ref[...] = jnp.zeros_like(dwout_ref)
            dpw_ref[...] = jnp.zeros_like(dpw_ref)
            dps_ref[...] = jnp.zeros_like(dps_ref)
            dlng_ref[...] = jnp.zeros_like(dlng_ref)
            dlnb_ref[...] = jnp.zeros_like(dlnb_ref)
            loss_ref[...] = jnp.zeros_like(loss_ref)

        seq_tile = i % tps
        tpos = seq_tile * tm + lax.broadcasted_iota(jnp.int32, (tm, 1), 0)
        ga_v = ga_ref[...].astype(F32)
        sig_a = jax.nn.sigmoid(ga_v)
        silu_a = ga_v * sig_a
        o_v = o_ref[...]
        ya = o_v * silu_a

        u_v = u_ref[...].astype(F32)
        halo = jnp.where(seq_tile == 0, 0.0, uh_ref[...].astype(F32))
        pooled, cnts, mixed = [], [], []
        for g in range(POOL_G):
            lanes = slice(g * POOL_GD, (g + 1) * POOL_GD)
            w = jnp.concatenate([halo[:, lanes], u_v[:, lanes]], axis=0)
            for st in range(g + 1):
                w = w + pltpu.roll(w, 1 << st, 0)
            cnt = jnp.minimum(tpos + 1, 2 << g).astype(F32)
            pg = (w[HALO:, :] / cnt - u_v[:, lanes]).astype(BF16)
            pooled.append(pg)
            cnts.append(cnt)
            mixed.append(_dot(pg, pw_ref[g]))
        mixed = jnp.concatenate(mixed, axis=1)
        ps = ps_ref[...]
        ybp = mixed * ps
        gb_v = gb_ref[...].astype(F32)
        sig_b = jax.nn.sigmoid(gb_v)
        silu_b = gb_v * sig_b
        yb = ybp * silu_b

        cat = jnp.concatenate([ya, yb], axis=1).astype(BF16)
        z = ALPHA * x_ref[...] + _dot(cat, wout_ref[...])
        mu = jnp.mean(z, axis=-1, keepdims=True)
        zc = z - mu
        rstd = lax.rsqrt(jnp.mean(zc * zc, axis=-1, keepdims=True) + LN_EPS)
        zhat = zc * rstd
        lng = lng_ref[...]
        err = (zhat * lng + lnb_ref[...]) - tgt_ref[...]
        row_loss = jnp.sum(err * err, axis=1, keepdims=True)
        loss_ref[...] += jnp.broadcast_to(jnp.sum(row_loss, axis=0, keepdims=True) * (0.5 / D_MODEL), (1, 128))
        dy = err * (1.0 / D_MODEL)
        dlng_ref[...] += jnp.sum(dy * zhat, axis=0, keepdims=True)
        dlnb_ref[...] += jnp.sum(dy, axis=0, keepdims=True)
        dzh = dy * lng
        dz = rstd * (dzh - jnp.mean(dzh, axis=-1, keepdims=True)
                     - zhat * jnp.mean(dzh * zhat, axis=-1, keepdims=True))
        dz_ref[...] = dz
        dzb = dz.astype(BF16)
        dwout_ref[...] += _dot_tn(cat, dzb)
        dcat = _dot_nt(dzb, wout_ref[...])
        dya = dcat[:, :MLA_W]
        dyb = dcat[:, MLA_W:]

        do = dya * silu_a
        do_ref[...] = do.astype(BF16)
        prod = do * o_v
        for hh in range(HEADS):
            lanes = slice(hh * 128, (hh + 1) * 128)
            delta_ref[:, lanes] = jnp.broadcast_to(jnp.sum(prod[:, lanes], axis=1, keepdims=True), (tm, 128))
        dga_ref[...] = (dya * o_v * (sig_a * (1.0 + ga_v * (1.0 - sig_a)))).astype(BF16)
        dgb_ref[...] = (dyb * ybp * (sig_b * (1.0 + gb_v * (1.0 - sig_b)))).astype(BF16)
        dybp = dyb * silu_b
        dps_ref[...] += jnp.sum(dybp * mixed, axis=0, keepdims=True)
        dmixed = (dybp * ps).astype(BF16)
        for g in range(POOL_G):
            lanes = slice(g * POOL_GD, (g + 1) * POOL_GD)
            dpw_ref[g] += _dot_tn(pooled[g], dmixed[:, lanes])
            dpc_ref[:, lanes] = (_dot_nt(dmixed[:, lanes], pw_ref[g]) / cnts[g]).astype(BF16)

    row = lambda w: pl.BlockSpec((tm, w), lambda i: (i, 0))
    f = lambda w, dt: jax.ShapeDtypeStruct((T, w), dt)
    halo_spec = pl.BlockSpec((HALO, POOL_W), lambda i: (jnp.maximum(i * hb - 1, 0), 0))
    return pl.pallas_call(
        body, name="mid", grid=(T // tm,),
        in_specs=[row(D_MODEL), row(D_MODEL), row(MLA_W), row(MLA_W), row(POOL_W), halo_spec, row(POOL_W),
                  _full(w_out.shape), _full(pool_w.shape),
                  _full(pool_scale.shape), _full(ln_g.shape), _full(ln_b.shape)],
        out_specs=(row(D_MODEL), row(MLA_W), row(MLA_W), row(MLA_W), row(POOL_W), row(POOL_W),
                   _full((D_MODEL, D_MODEL)), _full(pool_w.shape), _full((1, POOL_W)),
                   _full((1, D_MODEL)), _full((1, D_MODEL)), _full((1, 128))),
        out_shape=(f(D_MODEL, F32), f(MLA_W, BF16), f(MLA_W, F32), f(MLA_W, BF16), f(POOL_W, BF16), f(POOL_W, BF16),
                   jax.ShapeDtypeStruct((D_MODEL, D_MODEL), F32), jax.ShapeDtypeStruct(pool_w.shape, F32),
                   jax.ShapeDtypeStruct((1, POOL_W), F32), jax.ShapeDtypeStruct((1, D_MODEL), F32),
                   jax.ShapeDtypeStruct((1, D_MODEL), F32), jax.ShapeDtypeStruct((1, 128), F32)),
        compiler_params=_cparams(1),
    )(x, tgt, o, ga, u, u, gb, w_out, pool_w, pool_scale, ln_g, ln_b)


def _attn_bwd(q, k, v, do, lse, delta, pos_col, pos_row, bounds, nb, S, tq, tk):
    T = q.shape[0]
    nq, nk = S // tq, S // tk
    reps = tk // 128
    hg = HEAD_GROUP

    def body(qmin_ref, qmax_ref, kmin_ref, kmax_ref, q_ref, k_ref, v_ref, do_ref, lse_ref, dl_ref, pc_ref, pr_ref,
             dq_out, dk_out, dv_out, dq_ref, dk_ref, dv_ref):
        b, j = pl.program_id(0), pl.program_id(2)

        @pl.when(j == 0)
        def _():
            dq_ref[...] = jnp.zeros_like(dq_ref)

        dk_ref[...] = jnp.zeros_like(dk_ref)
        dv_ref[...] = jnp.zeros_like(dv_ref)
        k_lo = kmin_ref[b * nk + j]
        k_hi = kmax_ref[b * nk + j]

        def tile(i, masked):
            rows = pl.ds(pl.multiple_of(i * tq, tq), tq)
            if masked:
                keep = pc_ref[rows, :] >= pr_ref[...]
            stage = []
            for g in range(hg):
                qk = slice(g * HEAD_PAD, (g + 1) * HEAD_PAD)
                hv = slice(g * 128, (g + 1) * 128)
                s = _dot_nt(q_ref[rows, qk], k_ref[:, qk])
                if masked:
                    s = jnp.where(keep, s, NEG)
                stage.append((s, _dot_nt(do_ref[rows, hv], v_ref[:, hv])))
            grads = []
            for g in range(hg):
                hv = slice(g * 128, (g + 1) * 128)
                s, dp = stage[g]
                p = jnp.exp2(s - jnp.concatenate([lse_ref[rows, hv]] * reps, axis=1))
                ds = (p * (dp - jnp.concatenate([dl_ref[rows, hv]] * reps, axis=1))).astype(BF16)
                grads.append((p.astype(BF16), ds))
            for g in range(hg):
                qk = slice(g * HEAD_PAD, (g + 1) * HEAD_PAD)
                hv = slice(g * 128, (g + 1) * 128)
                p, ds = grads[g]
                dv_ref[:, hv] += _dot_tn(p, do_ref[rows, hv])
                dq_ref[rows, qk] += _dot(ds, k_ref[:, qk])
                dk_ref[:, qk] += _dot_tn(ds, q_ref[rows, qk])

        def step(i, carry):
            visible = k_lo <= qmax_ref[b * nq + i]
            clear = qmin_ref[b * nq + i] >= k_hi

            @pl.when(jnp.logical_and(visible, clear))
            def _():
                tile(i, False)

            @pl.when(jnp.logical_and(visible, jnp.logical_not(clear)))
            def _():
                tile(i, True)
            return carry

        lax.fori_loop(0, nq, step, 0)
        dk_out[...] = dk_ref[...].astype(BF16)
        dv_out[...] = dv_ref[...].astype(BF16)

        @pl.when(j == nk - 1)
        def _():
            dq_out[...] = dq_ref[...].astype(BF16)

    ng = HEADS // hg
    seq = lambda w: pl.BlockSpec((S, w), lambda b, h, j, *_: (b, h))
    blk = lambda w: pl.BlockSpec((tk, w), lambda b, h, j, *_: (b * nk + j, h))
    return pl.pallas_call(
        body, name="attn_bwd",
        grid_spec=pltpu.PrefetchScalarGridSpec(
            num_scalar_prefetch=4, grid=(nb, ng, nk),
            in_specs=[seq(hg * HEAD_PAD), blk(hg * HEAD_PAD), blk(hg * 128),
                      seq(hg * 128), seq(hg * 128), seq(hg * 128),
                      pl.BlockSpec((S, 1), lambda b, h, j, *_: (b, 0)),
                      pl.BlockSpec((None, 1, tk), lambda b, h, j, *_: (b, 0, j))],
            out_specs=(seq(hg * HEAD_PAD), blk(hg * HEAD_PAD), blk(hg * 128)),
            scratch_shapes=[pltpu.VMEM((S, hg * HEAD_PAD), F32), pltpu.VMEM((tk, hg * HEAD_PAD), F32),
                            pltpu.VMEM((tk, hg * 128), F32)]),
        out_shape=(jax.ShapeDtypeStruct((T, HEADS * HEAD_PAD), BF16),
                   jax.ShapeDtypeStruct((T, HEADS * HEAD_PAD), BF16),
                   jax.ShapeDtypeStruct((T, MLA_W), BF16)),
        compiler_params=_cparams(3),
    )(*bounds, q, k, v, do, lse, delta, pos_col, pos_row)


def _bwd_proj(dq, dk, dv, xq, xkv, x, dz, dga, dgb, dpc, rope_tab, w_uq_t, w_ukv, w_in_t, gq, gkv, S, tm):
    T = x.shape[0]
    tps = S // tm
    hb = tm // HALO
    n_tiles = T // tm

    def body(dq_ref, dk_ref, dv_ref, xq_ref, xkv_ref, x_ref, dz_ref, dga_ref, dgb_ref, dpc_ref, dph_ref,
             tab_ref, wuq_ref, wukv_ref, win_ref, gq_ref, gkv_ref,
             dx_ref, dwin_hbm, dwuq_hbm, dwukv_hbm, dgq_ref, dgkv_ref,
             acc_win, acc_wuq, acc_wukv, dh_sc):
        i = pl.program_id(0)

        @pl.when(i == 0)
        def _():
            acc_win[...] = jnp.zeros_like(acc_win)
            acc_wuq[...] = jnp.zeros_like(acc_wuq)
            acc_wukv[...] = jnp.zeros_like(acc_wukv)
            dgq_ref[...] = jnp.zeros_like(dgq_ref)
            dgkv_ref[...] = jnp.zeros_like(dgkv_ref)
            dh_sc[...] = jnp.zeros_like(dh_sc)

        dh_prev = dh_sc[...]
        dx_ref[...] = ALPHA * dz_ref[...] + _dot(dh_prev, win_ref[...])
        acc_win[...] += _dot_tn(dh_prev, x_ref[...].astype(BF16))

        live = jnp.where(i < n_tiles, 1.0, 0.0)
        c, sa, sb = _expand_rope_table(tab_ref[...])
        dq_v = dq_ref[...].astype(F32) * (SCALE * live)
        dk_v = dk_ref[...].astype(F32) * (LN2 * live)
        dv_v = dv_ref[...].astype(F32) * live
        dq_parts, dkv_parts = [], []
        dkr = jnp.zeros((tm, 128), F32)
        for hh in range(HEADS):
            b0 = hh * HEAD_PAD
            dq_parts.append(dq_v[:, b0:b0 + 128].astype(BF16))
            dq_parts.append(_rope(dq_v[:, b0 + 128:b0 + 256], c, sa, sb, -1.0).astype(BF16))
            dkv_parts.append(dk_v[:, b0:b0 + 128].astype(BF16))
            dkv_parts.append(dv_v[:, hh * 128:(hh + 1) * 128].astype(BF16))
            dkr = dkr + dk_v[:, b0 + 128:b0 + 256]
        dqp = jnp.concatenate(dq_parts, axis=1)
        dkvp = jnp.concatenate(dkv_parts, axis=1)
        dkrr = _rope(dkr, c, sa, sb, -1.0)

        def rms_bwd(xv, g, dyn, dg_ref):
            r = lax.rsqrt(jnp.mean(xv * xv, axis=-1, keepdims=True) + RMS_EPS)
            xhat = xv * r
            dg_ref[...] += jnp.sum(dyn * xhat, axis=0, keepdims=True)
            dxh = dyn * g
            return r * (dxh - xhat * jnp.mean(dxh * xhat, axis=-1, keepdims=True))

        xq_v = xq_ref[...].astype(F32)
        gq_v = gq_ref[...]
        rq = lax.rsqrt(jnp.mean(xq_v * xq_v, axis=-1, keepdims=True) + RMS_EPS)
        acc_wuq[...] += _dot_tn(dqp, ((xq_v * rq) * gq_v).astype(BF16))
        dxq = rms_bwd(xq_v, gq_v, _dot(dqp, wuq_ref[...]), dgq_ref)

        xkv_v = xkv_ref[...].astype(F32)
        gkv_v = gkv_ref[...]
        rkv = lax.rsqrt(jnp.mean(xkv_v * xkv_v, axis=-1, keepdims=True) + RMS_EPS)
        acc_wukv[...] += _dot_tn(((xkv_v * rkv) * gkv_v).astype(BF16), dkvp)
        dxkv = rms_bwd(xkv_v, gkv_v, _dot_nt(dkvp, wukv_ref[...]), dgkv_ref)

        seq_tile = i % tps
        tpos = seq_tile * tm + lax.broadcasted_iota(jnp.int32, (tm, 1), 0)
        dpc_v = dpc_ref[...].astype(F32)
        halo = jnp.where(seq_tile == tps - 1, 0.0, dph_ref[...].astype(F32))
        n = tm + HALO
        du = []
        for g in range(POOL_G):
            lanes = slice(g * POOL_GD, (g + 1) * POOL_GD)
            f = jnp.concatenate([dpc_v[:, lanes], halo[:, lanes]], axis=0)
            for st in range(g + 1):
                f = f + pltpu.roll(f, n - (1 << st), 0)
            cnt = jnp.minimum(tpos + 1, 2 << g).astype(F32)
            du.append((f[:tm, :] - dpc_v[:, lanes] * cnt).astype(BF16))

        dh_sc[...] = jnp.concatenate([dxq.astype(BF16), dxkv.astype(BF16), dkrr.astype(BF16), dga_ref[...]]
                                     + du + [dgb_ref[...]], axis=1)

        @pl.when(i == n_tiles)
        def _():
            pltpu.sync_copy(acc_win.at[pl.ds(0, 832)], dwin_hbm.at[pl.ds(0, 832)])
            pltpu.sync_copy(acc_win.at[pl.ds(896, IN_EXT - 896)], dwin_hbm.at[pl.ds(832, IN_W - 832)])
            for hh in range(HEADS):
                pltpu.sync_copy(acc_wuq.at[pl.ds(hh * HEAD_PAD, NOPE + ROPE)], dwuq_hbm.at[hh])
            pltpu.sync_copy(acc_wukv, dwukv_hbm)

    cur = lambda w: pl.BlockSpec((tm, w), lambda i: (jnp.minimum(i, n_tiles - 1), 0))
    prev = lambda w: pl.BlockSpec((tm, w), lambda i: (jnp.maximum(i - 1, 0), 0))
    halo_spec = pl.BlockSpec((HALO, POOL_W), lambda i: (jnp.minimum((i + 1) * hb, T // HALO - 1), 0))
    return pl.pallas_call(
        body, name="bwd_proj", grid=(n_tiles + 1,),
        in_specs=[cur(1024), cur(1024), cur(512), cur(512), cur(256), prev(D_MODEL), prev(D_MODEL),
                  cur(512), cur(512), cur(512), halo_spec, cur(128),
                  _full(w_uq_t.shape), _full(w_ukv.shape), _full(w_in_t.shape), _full(gq.shape), _full(gkv.shape)],
        out_specs=(prev(D_MODEL), ANY, ANY, ANY, _full((1, Q_LORA)), _full((1, KV_LORA))),
        out_shape=(jax.ShapeDtypeStruct((T, D_MODEL), F32),
                   jax.ShapeDtypeStruct((IN_W, D_MODEL), F32),
                   jax.ShapeDtypeStruct((HEADS, NOPE + ROPE, Q_LORA), F32),
                   jax.ShapeDtypeStruct((KV_LORA, 1024), F32),
                   jax.ShapeDtypeStruct((1, Q_LORA), F32), jax.ShapeDtypeStruct((1, KV_LORA), F32)),
        scratch_shapes=[pltpu.VMEM((IN_EXT, D_MODEL), F32), pltpu.VMEM((HEADS * HEAD_PAD, Q_LORA), F32),
                        pltpu.VMEM((KV_LORA, 1024), F32), pltpu.VMEM((tm, IN_EXT), BF16)],
        compiler_params=_cparams(1),
    )(dq, dk, dv, xq, xkv, x, dz, dga, dgb, dpc, dpc, rope_tab, w_uq_t, w_ukv, w_in_t, gq, gkv)


def kernel(x, positions, w_in, q_norm_g, w_uq, kv_norm_g, w_ukv, pool_w, pool_scale, w_out, ln_g, ln_b, loss_target, m_w_in, m_q_norm_g, m_w_uq, m_kv_norm_g, m_w_ukv, m_pool_w, m_pool_scale, m_w_out, m_ln_g, m_ln_b, v_w_in, v_q_norm_g, v_w_uq, v_kv_norm_g, v_w_ukv, v_pool_w, v_pool_scale, v_w_out, v_ln_g, v_ln_b):
    nb, S, _ = x.shape
    T = nb * S
    tm = min(256, S)
    tq = min(512, S)
    tk = min(512, S)
    assert S % tm == 0 and tm % HALO == 0 and S % tq == 0 and S % tk == 0

    cx, cy, cc = lax.axis_index("x"), lax.axis_index("y"), lax.axis_index("c")
    me = 2 * cx + cy
    place_arr = jnp.stack([me, cc]).astype(jnp.int32)

    def own_slot(w, slot_rows):
        blk = jnp.pad(w.astype(BF16), ((0, slot_rows - w.shape[0]), (0, 0)))
        return lax.dynamic_update_slice(jnp.zeros((N_CHIPS,) + blk.shape, BF16), blk[None], (me, 0, 0))

    w_in_g, w_uq_g, w_ukv_g, w_out_g = _weight_gather(
        [own_slot(w_in.T, 592), own_slot(w_uq.T, HEAD_PAD), own_slot(w_ukv, KV_LORA), own_slot(w_out, 256)],
        (592, NOPE + ROPE, KV_LORA, 256))
    w_in_f = w_in_g.reshape(IN_W, D_MODEL)
    w_in_t = jnp.concatenate([w_in_f[:832], jnp.zeros((64, D_MODEL), BF16), w_in_f[832:]], axis=0)
    w_uq_t = w_uq_g.reshape(HEADS * HEAD_PAD, Q_LORA)
    w_ukv_f = w_ukv_g.transpose(1, 0, 2).reshape(KV_LORA, 1024)
    w_out_f = w_out_g.reshape(D_MODEL, D_MODEL)
    pool_w_b = pool_w.astype(BF16)
    gq2 = q_norm_g.reshape(1, Q_LORA)
    gkv2 = kv_norm_g.reshape(1, KV_LORA)
    ps2 = pool_scale.reshape(1, POOL_W)

    half = ROPE // 2
    inv_freq = ROPE_THETA ** (-jnp.arange(half, dtype=F32) / half)
    freq_row = jnp.concatenate([inv_freq, inv_freq, jnp.zeros((2 * half,), F32)]).reshape(1, 128)
    pos_col = positions.reshape(T, 1)
    pos_row = positions.reshape(nb, 1, S)
    pos_q = positions.reshape(nb, S // tq, tq)
    pos_k = positions.reshape(nb, S // tk, tk)
    bounds = (jnp.min(pos_q, axis=2).reshape(-1), jnp.max(pos_q, axis=2).reshape(-1),
              jnp.min(pos_k, axis=2).reshape(-1), jnp.max(pos_k, axis=2).reshape(-1))

    xf = x.reshape(T, D_MODEL)
    tgt = loss_target.reshape(T, D_MODEL)

    xq, xkv, ga, u, gb, q, k, v, rope_tab = _fwd_proj(xf, w_in_t, w_uq_t, w_ukv_f, gq2, gkv2, pos_col, freq_row, tm)
    o, lse = _attn_fwd(q, k, v, pos_col, pos_row, bounds, nb, S, tq, tk)

    (dz, do, delta, dga, dgb, dpc, d_w_out, d_pool_w, d_pool_scale, d_ln_g, d_ln_b, loss_part) = _mid(
        xf, tgt, o, ga, u, gb, w_out_f, pool_w_b, ps2, ln_g, ln_b, S, tm)

    dq, dk, dv = _attn_bwd(q, k, v, do, lse, delta, pos_col, pos_row, bounds, nb, S, tq, tk)
    dx, d_w_in_t, d_w_uq_t, d_w_ukv, d_gq, d_gkv = _bwd_proj(
        dq, dk, dv, xq, xkv, xf, dz, dga, dgb, dpc, rope_tab, w_uq_t, w_ukv_f, w_in_t, gq2, gkv2, S, tm)
    grad_x = dx.reshape(nb, S, D_MODEL)

    g_in = d_w_in_t.reshape(N_CHIPS, 592, D_MODEL)
    g_uq = d_w_uq_t
    g_ukv = d_w_ukv.reshape(KV_LORA, N_CHIPS, 256).transpose(1, 0, 2)
    g_out = d_w_out.reshape(N_CHIPS, 256, D_MODEL)
    gs = [g_in, g_uq, g_ukv, g_out]
    from_sibling = _grad_to_sibling(gs)
    chip_sums, own_sums = _add_sibling_half(gs, from_sibling, place_arr)
    from_chips = _grad_to_chips(chip_sums)
    g_big = _halves_exchange(_add_chip_parts(own_sums, from_chips, place_arr))

    pw_sum, vec_sum = _small_allreduce(d_pool_w, d_ln_g, d_ln_b, d_pool_scale, d_gq, d_gkv, loss_part)

    big = _adamw_big(g_big, [w_in.T, w_uq.T, w_ukv, w_out], [m_w_in.T, m_w_uq.T, m_w_ukv, m_w_out],
                     [v_w_in.T, v_w_uq.T, v_w_ukv, v_w_out])
    two_d = lambda a: a.reshape(-1, a.shape[-1])
    small_names = lambda pw, lg, lb, ps, gq, gkv: [two_d(pw), lg, lb, ps.reshape(1, -1), gq.reshape(1, -1), gkv.reshape(1, -1)]
    small, loss_row = _adamw_small(
        pw_sum, vec_sum,
        small_names(pool_w, ln_g, ln_b, pool_scale, q_norm_g, kv_norm_g),
        small_names(m_pool_w, m_ln_g, m_ln_b, m_pool_scale, m_q_norm_g, m_kv_norm_g),
        small_names(v_pool_w, v_ln_g, v_ln_b, v_pool_scale, v_q_norm_g, v_kv_norm_g))
    loss = loss_row[0, 0]

    def leaves(kind):
        b = [g_big[t] if kind == 0 else big[t][kind - 1] for t in range(N_BIG)]
        b = [b[0].T, b[1].T, b[2], b[3]]
        s = [small[t][kind] for t in range(6)]
        return (b[0], s[4].reshape(Q_LORA), b[1], s[5].reshape(KV_LORA), b[2],
                s[0].reshape(POOL_G, POOL_GD, POOL_GD), s[3].reshape(POOL_W), b[3], s[1], s[2])

    return (loss, grad_x) + leaves(0) + leaves(1) + leaves(2) + leaves(3)
```

```python
import functools

import jax
import jax.numpy as jnp
from jax import lax
from jax.experimental import pallas as pl
from jax.experimental.pallas import tpu as pltpu

F32 = jnp.float32
BF16 = jnp.bfloat16
MESH = pl.DeviceIdType.MESH

HEADS = 4
NOPE = 128
ROPE = 64
HEAD_PAD = 256
Q_LORA = 512
KV_LORA = 256
MLA_W = 512
POOL_W = 512
POOL_G = 4
POOL_GD = 128
D_MODEL = 1024
IN_W = 2368
IN_EXT = 2432
ROPE_THETA = 10000.0
RMS_EPS = 1e-6
LN_EPS = 1e-5
ALPHA = 2.0 ** 0.25
SCALE = 192.0 ** -0.5
LOG2E = 1.4426950408889634
LN2 = 0.6931471805599453
QSCALE = SCALE * LOG2E
NEG = float(jnp.finfo(jnp.float32).min)
HEAD_GROUP = 2
HALO = 16

ADAM_LR = 0.001
ADAM_B1 = 0.9
ADAM_B2 = 0.999
ADAM_EPS = 1e-08
ADAM_WD = 0.01
ADAM_STEP = 10

N_CHIPS = 4
N_BIG = 4
VEC_ROWS = 16

VMEM_LIMIT = 56 * 1024 * 1024


def _cparams(n_grid_dims=0, **kw):
    sem = ("arbitrary",) * n_grid_dims if n_grid_dims else None
    return pltpu.CompilerParams(dimension_semantics=sem, vmem_limit_bytes=VMEM_LIMIT, **kw)


def _full(shape):
    nd = len(shape)
    return pl.BlockSpec(shape, lambda *_: (0,) * nd)


def _dot(a, b):
    return jnp.dot(a, b, preferred_element_type=F32)


def _dot_nt(a, b):
    return lax.dot_general(a, b, (((1,), (1,)), ((), ())), preferred_element_type=F32)


def _dot_tn(a, b):
    return lax.dot_general(a, b, (((0,), (0,)), ((), ())), preferred_element_type=F32)


def _rope_table(pos_col, freq_row):
    lane = lax.broadcasted_iota(jnp.int32, (1, 128), 1)
    ang = pos_col.astype(F32) * freq_row
    return jnp.where(lane < 32, jnp.cos(ang), jnp.where(lane < 64, jnp.sin(ang), 0.0))


def _expand_rope_table(tab):
    lane = lax.broadcasted_iota(jnp.int32, (1, 128), 1)
    second = jnp.logical_and(lane >= 32, lane < 64)
    c = jnp.where(lane < 32, tab, jnp.where(second, pltpu.roll(tab, 32, 1), 0.0))
    sa = jnp.where(lane < 32, pltpu.roll(tab, 96, 1), 0.0)
    sb = jnp.where(second, tab, 0.0)
    return c, sa, sb


def _rope(g, c, sa, sb, sign):
    return g * c + sign * (pltpu.roll(g, 32, 1) * sb - pltpu.roll(g, 96, 1) * sa)


def _place():
    x, y, c = lax.axis_index("x"), lax.axis_index("y"), lax.axis_index("c")
    chips = [(1 - x, y), (x, 1 - y), (1 - x, 1 - y)]
    return x, y, c, chips


def _half_cols(ref, half_index):
    hc = ref.shape[-1] // 2
    lead = tuple(pl.ds(0, n) for n in ref.shape[:-1])
    return ref.at[lead + (pl.ds(half_index * hc, hc),)]


ANY = pl.BlockSpec(memory_space=pl.ANY)


def _weight_gather(slots, valid_rows):
    n = len(slots)

    def body(*refs):
        outs = refs[n:2 * n]
        send_sems, recv_sems = refs[2 * n:]
        x, y, c, chips = _place()
        me = 2 * x + y

        def copy(t, k, chip_idx, half, to):
            hc = slots[t].shape[2] // 2
            blk = outs[t].at[chip_idx, pl.ds(0, valid_rows[t]), pl.ds(half * hc, hc)]
            return pltpu.make_async_remote_copy(
                src_ref=blk, dst_ref=blk, send_sem=send_sems.at[6 * t + k], recv_sem=recv_sems.at[6 * t + k],
                device_id=to, device_id_type=MESH)

        first = [copy(t, j, me, c, (cx, cy, c)) for t in range(n) for j, (cx, cy) in enumerate(chips)]
        for cp in first:
            cp.start()
        passed = []
        for j, (cx, cy) in enumerate(chips):
            for t in range(n):
                copy(t, j, 2 * cx + cy, c, (x, y, c)).wait_recv()
                fwd = copy(t, 3 + j, 2 * cx + cy, c, (x, y, 1 - c))
                fwd.start()
                passed.append(fwd)
        for j, (cx, cy) in enumerate(chips):
            for t in range(n):
                copy(t, 3 + j, 2 * cx + cy, 1 - c, (x, y, c)).wait_recv()
        for cp in first + passed:
            cp.wait_send()

    return pl.pallas_call(
        body, name="weight_gather",
        out_shape=tuple(jax.ShapeDtypeStruct(a.shape, a.dtype) for a in slots),
        in_specs=[ANY] * n, out_specs=(ANY,) * n, input_output_aliases={t: t for t in range(n)},
        scratch_shapes=[pltpu.SemaphoreType.DMA((6 * n,)), pltpu.SemaphoreType.DMA((6 * n,))],
    )(*slots)


def _grad_to_sibling(gs):
    n = len(gs)

    def body(*refs):
        g_refs, r_refs = refs[:n], refs[n:2 * n]
        send_sems, recv_sems = refs[2 * n:]
        x, y, c, _ = _place()
        cps = []
        for t in range(n):
            cp = pltpu.make_async_remote_copy(
                src_ref=_half_cols(g_refs[t], 1 - c), dst_ref=r_refs[t], send_sem=send_sems.at[t], recv_sem=recv_sems.at[t],
                device_id=(x, y, 1 - c), device_id_type=MESH)
            cp.start()
            cps.append(cp)
        for cp in cps:
            cp.wait()

    return pl.pallas_call(
        body, name="grad_to_sibling",
        out_shape=tuple(jax.ShapeDtypeStruct((N_CHIPS, g.shape[1], g.shape[2] // 2), F32) for g in gs),
        in_specs=[ANY] * n, out_specs=(ANY,) * n,
        scratch_shapes=[pltpu.SemaphoreType.DMA((n,)), pltpu.SemaphoreType.DMA((n,))],
    )(*gs)


def _grad_to_chips(ss):
    n = len(ss)

    def body(*refs):
        s_refs, r_refs = refs[:n], refs[n:2 * n]
        send_sems, recv_sems = refs[2 * n:]
        x, y, c, chips = _place()
        cps = []
        for t in range(n):
            for j, (cx, cy) in enumerate(chips):
                cp = pltpu.make_async_remote_copy(
                    src_ref=s_refs[t].at[2 * cx + cy], dst_ref=r_refs[t].at[j],
                    send_sem=send_sems.at[3 * t + j], recv_sem=recv_sems.at[3 * t + j],
                    device_id=(cx, cy, c), device_id_type=MESH)
                cp.start()
                cps.append(cp)
        for cp in cps:
            cp.wait()

    return pl.pallas_call(
        body, name="grad_to_chips",
        out_shape=tuple(jax.ShapeDtypeStruct((3,) + s.shape[1:], s.dtype) for s in ss),
        in_specs=[ANY] * n, out_specs=(ANY,) * n,
        scratch_shapes=[pltpu.SemaphoreType.DMA((3 * n,)), pltpu.SemaphoreType.DMA((3 * n,))],
    )(*ss)


def _halves_exchange(fs):
    n = len(fs)

    def body(*refs):
        o_refs = refs[n:2 * n]
        send_sems, recv_sems = refs[2 * n:]
        x, y, c, _ = _place()
        sib = (x, y, 1 - c)
        cps = []
        for t in range(n):
            mine = _half_cols(o_refs[t], c)
            cp = pltpu.make_async_remote_copy(
                src_ref=mine, dst_ref=mine, send_sem=send_sems.at[t], recv_sem=recv_sems.at[t],
                device_id=sib, device_id_type=MESH)
            cp.start()
            cps.append(cp)
        for t in range(n):
            theirs = _half_cols(o_refs[t], 1 - c)
            pltpu.make_async_remote_copy(
                src_ref=theirs, dst_ref=theirs, send_sem=send_sems.at[t], recv_sem=recv_sems.at[t],
                device_id=sib, device_id_type=MESH).wait_recv()
        for cp in cps:
            cp.wait_send()

    return pl.pallas_call(
        body, name="halves_exchange",
        out_shape=tuple(jax.ShapeDtypeStruct(f.shape, f.dtype) for f in fs),
        in_specs=[ANY] * n, out_specs=(ANY,) * n, input_output_aliases={t: t for t in range(n)},
        scratch_shapes=[pltpu.SemaphoreType.DMA((n,)), pltpu.SemaphoreType.DMA((n,))],
    )(*fs)


def _add_sibling_half(gs, rs, place_arr, wire_dtypes):
    n = len(gs)

    def body(place_ref, *refs):
        k = pl.program_id(0)
        for t in range(n):
            total = refs[t][...] + refs[n + t][...]
            refs[2 * n + 2 * t][...] = total.astype(wire_dtypes[t])

            @pl.when(k == place_ref[0])
            def _():
                refs[2 * n + 2 * t + 1][...] = total

    in_specs, out_specs, out_shape = [], [], []
    for g in gs:
        in_specs.append(pl.BlockSpec((None, g.shape[1], g.shape[2] // 2), lambda k, p: (k, 0, p[1])))
    for r, wire in zip(rs, wire_dtypes):
        blk = pl.BlockSpec((None,) + r.shape[1:], lambda k, p: (k, 0, 0))
        in_specs.append(blk)
        out_specs += [blk, pl.BlockSpec(r.shape[1:], lambda k, p: (0, 0))]
        out_shape += [jax.ShapeDtypeStruct(r.shape, wire), jax.ShapeDtypeStruct(r.shape[1:], F32)]
    outs = pl.pallas_call(
        body, name="add_sibling_half", out_shape=tuple(out_shape),
        grid_spec=pltpu.PrefetchScalarGridSpec(num_scalar_prefetch=1, grid=(N_CHIPS,),
                                               in_specs=in_specs, out_specs=tuple(out_specs)),
        compiler_params=_cparams(1),
    )(place_arr, *gs, *rs)
    return list(outs[0::2]), list(outs[1::2])


def _add_chip_parts(owns, rs, place_arr):
    n = len(owns)

    def body(place_ref, *refs):
        for t in range(n):
            r_ref = refs[n + t]
            refs[2 * n + t][...] = ((refs[t][...] + r_ref[0].astype(F32))
                                    + (r_ref[1].astype(F32) + r_ref[2].astype(F32)))

    in_specs, out_specs = [], []
    for o in owns:
        in_specs.append(pl.BlockSpec(o.shape, lambda i, p: (0, 0)))
    for r in rs:
        in_specs.append(pl.BlockSpec(r.shape, lambda i, p: (0, 0, 0)))
        out_specs.append(pl.BlockSpec(r.shape[1:], lambda i, p: (0, p[1])))
    return pl.pallas_call(
        body, name="add_chip_parts",
        out_shape=tuple(jax.ShapeDtypeStruct((o.shape[0], 2 * o.shape[1]), F32) for o in owns),
        grid_spec=pltpu.PrefetchScalarGridSpec(num_scalar_prefetch=1, grid=(1,),
                                               in_specs=in_specs, out_specs=tuple(out_specs)),
        compiler_params=_cparams(1),
    )(place_arr, *owns, *rs)


def _adamw_math(g, w, m, v):
    nm = ADAM_B1 * m + (1.0 - ADAM_B1) * g
    nv = ADAM_B2 * v + (1.0 - ADAM_B2) * (g * g)
    m_hat = nm / (1.0 - ADAM_B1 ** ADAM_STEP)
    v_hat = nv / (1.0 - ADAM_B2 ** ADAM_STEP)
    return -ADAM_LR * (m_hat / (jnp.sqrt(v_hat) + ADAM_EPS) + ADAM_WD * w), nm, nv


ADAM_STEPS = 8


def _adamw_big(gs, ws, ms, vs):
    n = len(gs)

    def body(*refs):
        for t in range(n):
            d, nm, nv = _adamw_math(refs[t][...], refs[n + t][...], refs[2 * n + t][...], refs[3 * n + t][...])
            refs[4 * n + 3 * t][...] = d
            refs[4 * n + 3 * t + 1][...] = nm
            refs[4 * n + 3 * t + 2][...] = nv

    def tile_spec(shape):
        rows, cols = shape
        if rows % (8 * ADAM_STEPS) == 0:
            return pl.BlockSpec((rows // ADAM_STEPS, cols), lambda i: (i, 0))
        return pl.BlockSpec((rows, cols // ADAM_STEPS), lambda i: (0, i))

    specs = [tile_spec(g.shape) for g in gs]
    out_specs, out_shape = [], []
    for t in range(n):
        out_specs += [specs[t]] * 3
        out_shape += [jax.ShapeDtypeStruct(gs[t].shape, F32)] * 3
    outs = pl.pallas_call(
        body, name="adamw_big", grid=(ADAM_STEPS,),
        in_specs=specs * 4, out_specs=tuple(out_specs), out_shape=tuple(out_shape),
        compiler_params=_cparams(1),
    )(*gs, *ws, *ms, *vs)
    return [outs[3 * t: 3 * t + 3] for t in range(n)]


def _adamw_small(pw_sum, vec_sum, ws, ms, vs):
    rows = (None, 0, 1, 2, 3, 8)
    n = len(ws)

    def body(pw_ref, vec_ref, *refs):
        outs = refs[3 * n:]
        for t in range(n):
            w_ref, m_ref, v_ref = refs[t], refs[n + t], refs[2 * n + t]
            if rows[t] is None:
                g = pw_ref[...]
            else:
                g = vec_ref[rows[t]:rows[t] + 1, 0:w_ref.shape[1]]
            d, nm, nv = _adamw_math(g, w_ref[...], m_ref[...], v_ref[...])
            outs[4 * t][...] = g
            outs[4 * t + 1][...] = d
            outs[4 * t + 2][...] = nm
            outs[4 * t + 3][...] = nv
        outs[4 * n][...] = vec_ref[9:10, 0:128]

    vm = pl.BlockSpec(memory_space=pltpu.VMEM)
    out_shape = []
    for w in ws:
        out_shape += [jax.ShapeDtypeStruct(w.shape, F32)] * 4
    out_shape.append(jax.ShapeDtypeStruct((1, 128), F32))
    outs = pl.pallas_call(
        body, name="adamw_small", in_specs=[vm] * (2 + 3 * n), out_specs=(vm,) * (4 * n + 1),
        out_shape=tuple(out_shape),
    )(pw_sum, vec_sum, *ws, *ms, *vs)
    return [outs[4 * t: 4 * t + 4] for t in range(n)], outs[4 * n]


def _fwd_proj(x, w_in_t, w_uq_t, w_ukv, gq, gkv, pos_col, freq_row, tm):
    T = x.shape[0]

    def body(x_ref, win_ref, wuq_ref, wukv_ref, gq_ref, gkv_ref, pos_ref, freq_ref,
             xq_ref, xkv_ref, ga_ref, u_ref, gb_ref, q_ref, k_ref, v_ref, tab_ref):
        h = _dot_nt(x_ref[...].astype(BF16), win_ref[...])
        xq = h[:, 0:512]
        xkv = h[:, 512:768]
        xq_ref[...] = xq.astype(BF16)
        xkv_ref[...] = xkv.astype(BF16)
        ga_ref[...] = h[:, 896:1408].astype(BF16)
        u_ref[...] = h[:, 1408:1920].astype(BF16)
        gb_ref[...] = h[:, 1920:2432].astype(BF16)
        tab = _rope_table(pos_ref[...], freq_ref[...])
        tab_ref[...] = tab
        c, sa, sb = _expand_rope_table(tab)
        rq = lax.rsqrt(jnp.mean(xq * xq, axis=-1, keepdims=True) + RMS_EPS)
        q = _dot_nt(((xq * rq) * gq_ref[...]).astype(BF16), wuq_ref[...]) * QSCALE
        rkv = lax.rsqrt(jnp.mean(xkv * xkv, axis=-1, keepdims=True) + RMS_EPS)
        kv = _dot(((xkv * rkv) * gkv_ref[...]).astype(BF16), wukv_ref[...])
        kr = _rope(h[:, 768:896], c, sa, sb, 1.0).astype(BF16)
        for hh in range(HEADS):
            b0 = hh * HEAD_PAD
            q_ref[:, b0:b0 + 128] = q[:, b0:b0 + 128].astype(BF16)
            q_ref[:, b0 + 128:b0 + 256] = _rope(q[:, b0 + 128:b0 + 256], c, sa, sb, 1.0).astype(BF16)
            k_ref[:, b0:b0 + 128] = kv[:, b0:b0 + 128].astype(BF16)
            k_ref[:, b0 + 128:b0 + 256] = kr
            v_ref[:, hh * 128:(hh + 1) * 128] = kv[:, b0 + 128:b0 + 256].astype(BF16)

    row = lambda w: pl.BlockSpec((tm, w), lambda i: (i, 0))
    f = lambda w, dt: jax.ShapeDtypeStruct((T, w), dt)
    return pl.pallas_call(
        body, name="fwd_proj", grid=(T // tm,),
        in_specs=[row(D_MODEL), _full(w_in_t.shape), _full(w_uq_t.shape), _full(w_ukv.shape),
                  _full(gq.shape), _full(gkv.shape), row(1), _full(freq_row.shape)],
        out_specs=(row(512), row(256), row(512), row(512), row(512), row(1024), row(1024), row(512), row(128)),
        out_shape=(f(512, BF16), f(256, BF16), f(512, BF16), f(512, BF16), f(512, BF16),
                   f(1024, BF16), f(1024, BF16), f(512, BF16), f(128, F32)),
        compiler_params=_cparams(1),
    )(x, w_in_t, w_uq_t, w_ukv, gq, gkv, pos_col, freq_row)


def _attn_fwd(q, k, v, pos_col, pos_row, bounds, nb, S, tq, tk):
    T = q.shape[0]
    nq, nk = S // tq, S // tk
    reps = tk // 128
    hg = HEAD_GROUP

    def body(qmin_ref, qmax_ref, kmin_ref, kmax_ref, q_ref, k_ref, v_ref, pc_ref, pr_ref, o_ref, lse_ref,
             m_sc, l_sc, acc_sc):
        b, i = pl.program_id(0), pl.program_id(2)
        m_sc[...] = jnp.full(m_sc.shape, NEG, F32)
        l_sc[...] = jnp.zeros_like(l_sc)
        acc_sc[...] = jnp.zeros_like(acc_sc)
        q_lo = qmin_ref[b * nq + i]
        q_hi = qmax_ref[b * nq + i]

        def tile(j, masked):
            off = pl.multiple_of(j * tk, tk)
            if masked:
                keep = pc_ref[...] >= pr_ref[pl.ds(j, 1), :]
            logits = []
            for g in range(hg):
                qk = slice(g * HEAD_PAD, (g + 1) * HEAD_PAD)
                s = _dot_nt(q_ref[:, qk], k_ref[pl.ds(off, tk), qk])
                if masked:
                    s = jnp.where(keep, s, NEG)
                logits.append(s)
            probs = []
            for g in range(hg):
                hv = slice(g * 128, (g + 1) * 128)
                s = logits[g]
                m_prev = m_sc[:, hv]
                m_new = jnp.maximum(m_prev, jnp.max(s, axis=1, keepdims=True))
                p = jnp.exp2(s - jnp.concatenate([m_new] * reps, axis=1))
                a = jnp.exp2(m_prev - m_new)
                l_sc[:, hv] = a * l_sc[:, hv] + jnp.sum(p, axis=1, keepdims=True)
                m_sc[:, hv] = m_new
                probs.append((p.astype(BF16), a))
            for g in range(hg):
                hv = slice(g * 128, (g + 1) * 128)
                p, a = probs[g]
                acc_sc[:, hv] = a * acc_sc[:, hv] + _dot(p, v_ref[pl.ds(off, tk), hv])

        def step(j, carry):
            visible = kmin_ref[b * nk + j] <= q_hi
            clear = q_lo >= kmax_ref[b * nk + j]

            @pl.when(jnp.logical_and(visible, clear))
            def _():
                tile(j, False)

            @pl.when(jnp.logical_and(visible, jnp.logical_not(clear)))
            def _():
                tile(j, True)
            return carry

        lax.fori_loop(0, nk, step, 0)
        l = l_sc[...]
        o_ref[...] = acc_sc[...] / l
        lse_ref[...] = m_sc[...] + jnp.log2(l)

    ng = HEADS // hg
    stat = pltpu.VMEM((tq, hg * 128), F32)
    return pl.pallas_call(
        body, name="attn_fwd",
        grid_spec=pltpu.PrefetchScalarGridSpec(
            num_scalar_prefetch=4, grid=(nb, ng, nq),
            in_specs=[pl.BlockSpec((tq, hg * HEAD_PAD), lambda b, h, i, *_: (b * nq + i, h)),
                      pl.BlockSpec((S, hg * HEAD_PAD), lambda b, h, i, *_: (b, h)),
                      pl.BlockSpec((S, hg * 128), lambda b, h, i, *_: (b, h)),
                      pl.BlockSpec((tq, 1), lambda b, h, i, *_: (b * nq + i, 0)),
                      pl.BlockSpec((None, nk, tk), lambda b, h, i, *_: (b, 0, 0))],
            out_specs=(pl.BlockSpec((tq, hg * 128), lambda b, h, i, *_: (b * nq + i, h)),
                       pl.BlockSpec((tq, hg * 128), lambda b, h, i, *_: (b * nq + i, h))),
            scratch_shapes=[stat, stat, stat]),
        out_shape=(jax.ShapeDtypeStruct((T, MLA_W), F32), jax.ShapeDtypeStruct((T, MLA_W), F32)),
        compiler_params=_cparams(3),
    )(*bounds, q, k, v, pos_col, pos_row.reshape(nb, nk, tk))


def _mid(x, tgt, o, ga, u, gb, w_out, pool_w, pool_scale, ln_g, ln_b, S, tm):
    T = x.shape[0]
    tps = S // tm
    hb = tm // HALO

    def body(x_ref, tgt_ref, o_ref, ga_ref, u_ref, uh_ref, gb_ref, wout_ref, pw_ref,
             ps_ref, lng_ref, lnb_ref,
             dz_ref, do_ref, delta_ref, dga_ref, dgb_ref, dpc_ref,
             dwout_ref, dpw_ref, dps_ref, dlng_ref, dlnb_ref, loss_ref):
        i = pl.program_id(0)

        @pl.when(i == 0)
        def _():
            dwout_ref[...] = jnp.zeros_like(dwout_ref)
            dpw_ref[...] = jnp.zeros_like(dpw_ref)
            dps_ref[...] = jnp.zeros_like(dps_ref)
            dlng_ref[...] = jnp.zeros_like(dlng_ref)
            dlnb_ref[...] = jnp.zeros_like(dlnb_ref)
            loss_ref[...] = jnp.zeros_like(loss_ref)

        seq_tile = i % tps
        tpos = seq_tile * tm + lax.broadcasted_iota(jnp.int32, (tm, 1), 0)
        ga_v = ga_ref[...].astype(F32)
        sig_a = jax.nn.sigmoid(ga_v)
        silu_a = ga_v * sig_a
        o_v = o_ref[...]
        ya = o_v * silu_a

        u_v = u_ref[...].astype(F32)
        halo = jnp.where(seq_tile == 0, 0.0, uh_ref[...].astype(F32))
        pooled, cnts, mixed = [], [], []
        for g in range(POOL_G):
            lanes = slice(g * POOL_GD, (g + 1) * POOL_GD)
            w = jnp.concatenate([halo[:, lanes], u_v[:, lanes]], axis=0)
            for st in range(g + 1):
                w = w + pltpu.roll(w, 1 << st, 0)
            cnt = jnp.minimum(tpos + 1, 2 << g).astype(F32)
            pg = (w[HALO:, :] / cnt - u_v[:, lanes]).astype(BF16)
            pooled.append(pg)
            cnts.append(cnt)
            mixed.append(_dot(pg, pw_ref[g]))
        mixed = jnp.concatenate(mixed, axis=1)
        ps = ps_ref[...]
        ybp = mixed * ps
        gb_v = gb_ref[...].astype(F32)
        sig_b = jax.nn.sigmoid(gb_v)
        silu_b = gb_v * sig_b
        yb = ybp * silu_b

        cat = jnp.concatenate([ya, yb], axis=1).astype(BF16)
        z = ALPHA * x_ref[...] + _dot(cat, wout_ref[...])
        mu = jnp.mean(z, axis=-1, keepdims=True)
        zc = z - mu
        rstd = lax.rsqrt(jnp.mean(zc * zc, axis=-1, keepdims=True) + LN_EPS)
        zhat = zc * rstd
        lng = lng_ref[...]
        err = (zhat * lng + lnb_ref[...]) - tgt_ref[...]
        row_loss = jnp.sum(err * err, axis=1, keepdims=True)
        loss_ref[...] += jnp.broadcast_to(jnp.sum(row_loss, axis=0, keepdims=True) * (0.5 / D_MODEL), (1, 128))
        dy = err * (1.0 / D_MODEL)
        dlng_ref[...] += jnp.sum(dy * zhat, axis=0, keepdims=True)
        dlnb_ref[...] += jnp.sum(dy, axis=0, keepdims=True)
        dzh = dy * lng
        dz = rstd * (dzh - jnp.mean(dzh, axis=-1, keepdims=True)
                     - zhat * jnp.mean(dzh * zhat, axis=-1, keepdims=True))
        dz_ref[...] = dz
        dzb = dz.astype(BF16)
        dwout_ref[...] += _dot_tn(cat, dzb)
        dcat = _dot_nt(dzb, wout_ref[...])
        dya = dcat[:, :MLA_W]
        dyb = dcat[:, MLA_W:]

        do = dya * silu_a
        do_ref[...] = do.astype(BF16)
        prod = do * o_v
        for hh in range(HEADS):
            lanes = slice(hh * 128, (hh + 1) * 128)
            delta_ref[:, lanes] = jnp.broadcast_to(jnp.sum(prod[:, lanes], axis=1, keepdims=True), (tm, 128))
        dga_ref[...] = (dya * o_v * (sig_a * (1.0 + ga_v * (1.0 - sig_a)))).astype(BF16)
        dgb_ref[...] = (dyb * ybp * (sig_b * (1.0 + gb_v * (1.0 - sig_b)))).astype(BF16)
        dybp = dyb * silu_b
        dps_ref[...] += jnp.sum(dybp * mixed, axis=0, keepdims=True)
        dmixed = (dybp * ps).astype(BF16)
        for g in range(POOL_G):
            lanes = slice(g * POOL_GD, (g + 1) * POOL_GD)
            dpw_ref[g] += _dot_tn(pooled[g], dmixed[:, lanes])
            dpc_ref[:, lanes] = (_dot_nt(dmixed[:, lanes], pw_ref[g]) / cnts[g]).astype(BF16)

    row = lambda w: pl.BlockSpec((tm, w), lambda i: (i, 0))
    f = lambda w, dt: jax.ShapeDtypeStruct((T, w), dt)
    halo_spec = pl.BlockSpec((HALO, POOL_W), lambda i: (jnp.maximum(i * hb - 1, 0), 0))
    return pl.pallas_call(
        body, name="mid", grid=(T // tm,),
        in_specs=[row(D_MODEL), row(D_MODEL), row(MLA_W), row(MLA_W), row(POOL_W), halo_spec, row(POOL_W),
                  _full(w_out.shape), _full(pool_w.shape),
                  _full(pool_scale.shape), _full(ln_g.shape), _full(ln_b.shape)],
        out_specs=(row(D_MODEL), row(MLA_W), row(MLA_W), row(MLA_W), row(POOL_W), row(POOL_W),
                   _full((D_MODEL, D_MODEL)), _full(pool_w.shape), _full((1, POOL_W)),
                   _full((1, D_MODEL)), _full((1, D_MODEL)), _full((1, 128))),
        out_shape=(f(D_MODEL, F32), f(MLA_W, BF16), f(MLA_W, F32), f(MLA_W, BF16), f(POOL_W, BF16), f(POOL_W, BF16),
                   jax.ShapeDtypeStruct((D_MODEL, D_MODEL), F32), jax.ShapeDtypeStruct(pool_w.shape, F32),
                   jax.ShapeDtypeStruct((1, POOL_W), F32), jax.ShapeDtypeStruct((1, D_MODEL), F32),
                   jax.ShapeDtypeStruct((1, D_MODEL), F32), jax.ShapeDtypeStruct((1, 128), F32)),
        compiler_params=_cparams(1),
    )(x, tgt, o, ga, u, u, gb, w_out, pool_w, pool_scale, ln_g, ln_b)


def _attn_bwd(q, k, v, do, lse, delta, pos_col, pos_row, bounds, nb, S, tq, tk):
    T = q.shape[0]
    nq, nk = S // tq, S // tk
    reps = tk // 128
    hg = HEAD_GROUP

    def body(qmin_ref, qmax_ref, kmin_ref, kmax_ref, q_ref, k_ref, v_ref, do_ref, lse_ref, dl_ref, pc_ref, pr_ref,
             dq_out, dk_out, dv_out, dq_ref, dk_ref, dv_ref):
        b, j = pl.program_id(0), pl.program_id(2)

        @pl.when(j == 0)
        def _():
            dq_ref[...] = jnp.zeros_like(dq_ref)

        dk_ref[...] = jnp.zeros_like(dk_ref)
        dv_ref[...] = jnp.zeros_like(dv_ref)
        k_lo = kmin_ref[b * nk + j]
        k_hi = kmax_ref[b * nk + j]

        def tile(i, masked):
            rows = pl.ds(pl.multiple_of(i * tq, tq), tq)
            if masked:
                keep = pc_ref[rows, :] >= pr_ref[...]
            stage = []
            for g in range(hg):
                qk = slice(g * HEAD_PAD, (g + 1) * HEAD_PAD)
                hv = slice(g * 128, (g + 1) * 128)
                s = _dot_nt(q_ref[rows, qk], k_ref[:, qk])
                if masked:
                    s = jnp.where(keep, s, NEG)
                stage.append((s, _dot_nt(do_ref[rows, hv], v_ref[:, hv])))
            grads = []
            for g in range(hg):
                hv = slice(g * 128, (g + 1) * 128)
                s, dp = stage[g]
                p = jnp.exp2(s - jnp.concatenate([lse_ref[rows, hv]] * reps, axis=1))
                ds = (p * (dp - jnp.concatenate([dl_ref[rows, hv]] * reps, axis=1))).astype(BF16)
                grads.append((p.astype(BF16), ds))
            for g in range(hg):
                qk = slice(g * HEAD_PAD, (g + 1) * HEAD_PAD)
                hv = slice(g * 128, (g + 1) * 128)
                p, ds = grads[g]
                dv_ref[:, hv] += _dot_tn(p, do_ref[rows, hv])
                dq_ref[rows, qk] += _dot(ds, k_ref[:, qk])
                dk_ref[:, qk] += _dot_tn(ds, q_ref[rows, qk])

        def step(i, carry):
            visible = k_lo <= qmax_ref[b * nq + i]
            clear = qmin_ref[b * nq + i] >= k_hi

            @pl.when(jnp.logical_and(visible, clear))
            def _():
                tile(i, False)

            @pl.when(jnp.logical_and(visible, jnp.logical_not(clear)))
            def _():
                tile(i, True)
            return carry

        lax.fori_loop(0, nq, step, 0)
        dk_out[...] = dk_ref[...].astype(BF16)
        dv_out[...] = dv_ref[...].astype(BF16)

        @pl.when(j == nk - 1)
        def _():
            dq_out[...] = dq_ref[...].astype(BF16)

    ng = HEADS // hg
    seq = lambda w: pl.BlockSpec((S, w), lambda b, h, j, *_: (b, h))
    blk = lambda w: pl.BlockSpec((tk, w), lambda b, h, j, *_: (b * nk + j, h))
    return pl.pallas_call(
        body, name="attn_bwd",
        grid_spec=pltpu.PrefetchScalarGridSpec(
            num_scalar_prefetch=4, grid=(nb, ng, nk),
            in_specs=[seq(hg * HEAD_PAD), blk(hg * HEAD_PAD), blk(hg * 128),
                      seq(hg * 128), seq(hg * 128), seq(hg * 128),
                      pl.BlockSpec((S, 1), lambda b, h, j, *_: (b, 0)),
                      pl.BlockSpec((None, 1, tk), lambda b, h, j, *_: (b, 0, j))],
            out_specs=(seq(hg * HEAD_PAD), blk(hg * HEAD_PAD), blk(hg * 128)),
            scratch_shapes=[pltpu.VMEM((S, hg * HEAD_PAD), F32), pltpu.VMEM((tk, hg * HEAD_PAD), F32),
                            pltpu.VMEM((tk, hg * 128), F32)]),
        out_shape=(jax.ShapeDtypeStruct((T, HEADS * HEAD_PAD), BF16),
                   jax.ShapeDtypeStruct((T, HEADS * HEAD_PAD), BF16),
                   jax.ShapeDtypeStruct((T, MLA_W), BF16)),
        compiler_params=_cparams(3),
    )(*bounds, q, k, v, do, lse, delta, pos_col, pos_row)


def _bwd_proj(dq, dk, dv, xq, xkv, x, dz, dga, dgb, dpc, rope_tab, w_uq_t, w_ukv, w_in_t, gq, gkv, S, tm):
    T = x.shape[0]
    tps = S // tm
    hb = tm // HALO
    n_tiles = T // tm

    def body(dq_ref, dk_ref, dv_ref, xq_ref, xkv_ref, x_ref, dz_ref, dga_ref, dgb_ref, dpc_ref, dph_ref,
             tab_ref, wuq_ref, wukv_ref, win_ref, gq_ref, gkv_ref,
             dx_ref, dwin_hbm, dwuq_hbm, dwukv_hbm, dgq_ref, dgkv_ref,
             acc_win, acc_wuq, acc_wukv, dh_sc):
        i = pl.program_id(0)

        @pl.when(i == 0)
        def _():
            acc_win[...] = jnp.zeros_like(acc_win)
            acc_wuq[...] = jnp.zeros_like(acc_wuq)
            acc_wukv[...] = jnp.zeros_like(acc_wukv)
            dgq_ref[...] = jnp.zeros_like(dgq_ref)
            dgkv_ref[...] = jnp.zeros_like(dgkv_ref)
            dh_sc[...] = jnp.zeros_like(dh_sc)

        dh_prev = dh_sc[...]
        dx_ref[...] = ALPHA * dz_ref[...] + _dot(dh_prev, win_ref[...])
        acc_win[...] += _dot_tn(dh_prev, x_ref[...].astype(BF16))

        live = jnp.where(i < n_tiles, 1.0, 0.0)
        c, sa, sb = _expand_rope_table(tab_ref[...])
        dq_v = dq_ref[...].astype(F32) * (SCALE * live)
        dk_v = dk_ref[...].astype(F32) * (LN2 * live)
        dv_v = dv_ref[...].astype(F32) * live
        dq_parts, dkv_parts = [], []
        dkr = jnp.zeros((tm, 128), F32)
        for hh in range(HEADS):
            b0 = hh * HEAD_PAD
            dq_parts.append(dq_v[:, b0:b0 + 128].astype(BF16))
            dq_parts.append(_rope(dq_v[:, b0 + 128:b0 + 256], c, sa, sb, -1.0).astype(BF16))
            dkv_parts.append(dk_v[:, b0:b0 + 128].astype(BF16))
            dkv_parts.append(dv_v[:, hh * 128:(hh + 1) * 128].astype(BF16))
            dkr = dkr + dk_v[:, b0 + 128:b0 + 256]
        dqp = jnp.concatenate(dq_parts, axis=1)
        dkvp = jnp.concatenate(dkv_parts, axis=1)
        dkrr = _rope(dkr, c, sa, sb, -1.0)

        def rms_bwd(xv, g, dyn, dg_ref):
            r = lax.rsqrt(jnp.mean(xv * xv, axis=-1, keepdims=True) + RMS_EPS)
            xhat = xv * r
            dg_ref[...] += jnp.sum(dyn * xhat, axis=0, keepdims=True)
            dxh = dyn * g
            return r * (dxh - xhat * jnp.mean(dxh * xhat, axis=-1, keepdims=True))

        xq_v = xq_ref[...].astype(F32)
        gq_v = gq_ref[...]
        rq = lax.rsqrt(jnp.mean(xq_v * xq_v, axis=-1, keepdims=True) + RMS_EPS)
        acc_wuq[...] += _dot_tn(dqp, ((xq_v * rq) * gq_v).astype(BF16))
        dxq = rms_bwd(xq_v, gq_v, _dot(dqp, wuq_ref[...]), dgq_ref)

        xkv_v = xkv_ref[...].astype(F32)
        gkv_v = gkv_ref[...]
        rkv = lax.rsqrt(jnp.mean(xkv_v * xkv_v, axis=-1, keepdims=True) + RMS_EPS)
        acc_wukv[...] += _dot_tn(((xkv_v * rkv) * gkv_v).astype(BF16), dkvp)
        dxkv = rms_bwd(xkv_v, gkv_v, _dot_nt(dkvp, wukv_ref[...]), dgkv_ref)

        seq_tile = i % tps
        tpos = seq_tile * tm + lax.broadcasted_iota(jnp.int32, (tm, 1), 0)
        dpc_v = dpc_ref[...].astype(F32)
        halo = jnp.where(seq_tile == tps - 1, 0.0, dph_ref[...].astype(F32))
        n = tm + HALO
        du = []
        for g in range(POOL_G):
            lanes = slice(g * POOL_GD, (g + 1) * POOL_GD)
            f = jnp.concatenate([dpc_v[:, lanes], halo[:, lanes]], axis=0)
            for st in range(g + 1):
                f = f + pltpu.roll(f, n - (1 << st), 0)
            cnt = jnp.minimum(tpos + 1, 2 << g).astype(F32)
            du.append((f[:tm, :] - dpc_v[:, lanes] * cnt).astype(BF16))

        dh_sc[...] = jnp.concatenate([dxq.astype(BF16), dxkv.astype(BF16), dkrr.astype(BF16), dga_ref[...]]
                                     + du + [dgb_ref[...]], axis=1)

        @pl.when(i == n_tiles)
        def _():
            pltpu.sync_copy(acc_win.at[pl.ds(0, 832)], dwin_hbm.at[pl.ds(0, 832)])
            pltpu.sync_copy(acc_win.at[pl.ds(896, IN_EXT - 896)], dwin_hbm.at[pl.ds(832, IN_W - 832)])
            for hh in range(HEADS):
                pltpu.sync_copy(acc_wuq.at[pl.ds(hh * HEAD_PAD, NOPE + ROPE)], dwuq_hbm.at[hh])
            pltpu.sync_copy(acc_wukv, dwukv_hbm)

    cur = lambda w: pl.BlockSpec((tm, w), lambda i: (jnp.minimum(i, n_tiles - 1), 0))
    prev = lambda w: pl.BlockSpec((tm, w), lambda i: (jnp.maximum(i - 1, 0), 0))
    halo_spec = pl.BlockSpec((HALO, POOL_W), lambda i: (jnp.minimum((i + 1) * hb, T // HALO - 1), 0))
    return pl.pallas_call(
        body, name="bwd_proj", grid=(n_tiles + 1,),
        in_specs=[cur(1024), cur(1024), cur(512), cur(512), cur(256), prev(D_MODEL), prev(D_MODEL),
                  cur(512), cur(512), cur(512), halo_spec, cur(128),
                  _full(w_uq_t.shape), _full(w_ukv.shape), _full(w_in_t.shape), _full(gq.shape), _full(gkv.shape)],
        out_specs=(prev(D_MODEL), ANY, ANY, ANY, _full((1, Q_LORA)), _full((1, KV_LORA))),
        out_shape=(jax.ShapeDtypeStruct((T, D_MODEL), F32),
                   jax.ShapeDtypeStruct((IN_W, D_MODEL), F32),
                   jax.ShapeDtypeStruct((HEADS, NOPE + ROPE, Q_LORA), F32),
                   jax.ShapeDtypeStruct((KV_LORA, 1024), F32),
                   jax.ShapeDtypeStruct((1, Q_LORA), F32), jax.ShapeDtypeStruct((1, KV_LORA), F32)),
        scratch_shapes=[pltpu.VMEM((IN_EXT, D_MODEL), F32), pltpu.VMEM((HEADS * HEAD_PAD, Q_LORA), F32),
                        pltpu.VMEM((KV_LORA, 1024), F32), pltpu.VMEM((tm, IN_EXT), BF16)],
        compiler_params=_cparams(1),
    )(dq, dk, dv, xq, xkv, x, dz, dga, dgb, dpc, dpc, rope_tab, w_uq_t, w_ukv, w_in_t, gq, gkv)


def kernel(x, positions, w_in, q_norm_g, w_uq, kv_norm_g, w_ukv, pool_w, pool_scale, w_out, ln_g, ln_b, loss_target, m_w_in, m_q_norm_g, m_w_uq, m_kv_norm_g, m_w_ukv, m_pool_w, m_pool_scale, m_w_out, m_ln_g, m_ln_b, v_w_in, v_q_norm_g, v_w_uq, v_kv_norm_g, v_w_ukv, v_pool_w, v_pool_scale, v_w_out, v_ln_g, v_ln_b):
    nb, S, _ = x.shape
    T = nb * S
    tm = min(256, S)
    tq = min(512, S)
    tk = min(512, S)
    assert S % tm == 0 and tm % HALO == 0 and S % tq == 0 and S % tk == 0

    cx, cy, cc = lax.axis_index("x"), lax.axis_index("y"), lax.axis_index("c")
    me = 2 * cx + cy
    place_arr = jnp.stack([me, cc]).astype(jnp.int32)

    def own_slot(w, slot_rows):
        blk = jnp.pad(w.astype(BF16), ((0, slot_rows - w.shape[0]), (0, 0)))
        return lax.dynamic_update_slice(jnp.zeros((N_CHIPS,) + blk.shape, BF16), blk[None], (me, 0, 0))

    w_in_g, w_uq_g, w_ukv_g, w_out_g = _weight_gather(
        [own_slot(w_in.T, 592), own_slot(w_uq.T, HEAD_PAD), own_slot(w_ukv, KV_LORA), own_slot(w_out, 256)],
        (592, NOPE + ROPE, KV_LORA, 256))
    w_in_f = w_in_g.reshape(IN_W, D_MODEL)
    w_in_t = jnp.concatenate([w_in_f[:832], jnp.zeros((64, D_MODEL), BF16), w_in_f[832:]], axis=0)
    w_uq_t = w_uq_g.reshape(HEADS * HEAD_PAD, Q_LORA)
    w_ukv_f = w_ukv_g.transpose(1, 0, 2).reshape(KV_LORA, 1024)
    w_out_f = w_out_g.reshape(D_MODEL, D_MODEL)
    pool_w_b = pool_w.astype(BF16)
    gq2 = q_norm_g.reshape(1, Q_LORA)
    gkv2 = kv_norm_g.reshape(1, KV_LORA)
    ps2 = pool_scale.reshape(1, POOL_W)

    half = ROPE // 2
    inv_freq = ROPE_THETA ** (-jnp.arange(half, dtype=F32) / half)
    freq_row = jnp.concatenate([inv_freq, inv_freq, jnp.zeros((2 * half,), F32)]).reshape(1, 128)
    pos_col = positions.reshape(T, 1)
    pos_row = positions.reshape(nb, 1, S)
    pos_q = positions.reshape(nb, S // tq, tq)
    pos_k = positions.reshape(nb, S // tk, tk)
    bounds = (jnp.min(pos_q, axis=2).reshape(-1), jnp.max(pos_q, axis=2).reshape(-1),
              jnp.min(pos_k, axis=2).reshape(-1), jnp.max(pos_k, axis=2).reshape(-1))

    xf = x.reshape(T, D_MODEL)
    tgt = loss_target.reshape(T, D_MODEL)

    xq, xkv, ga, u, gb, q, k, v, rope_tab = _fwd_proj(xf, w_in_t, w_uq_t, w_ukv_f, gq2, gkv2, pos_col, freq_row, tm)
    o, lse = _attn_fwd(q, k, v, pos_col, pos_row, bounds, nb, S, tq, tk)

    (dz, do, delta, dga, dgb, dpc, d_w_out, d_pool_w, d_pool_scale, d_ln_g, d_ln_b, loss_part) = _mid(
        xf, tgt, o, ga, u, gb, w_out_f, pool_w_b, ps2, ln_g, ln_b, S, tm)

    dq, dk, dv = _attn_bwd(q, k, v, do, lse, delta, pos_col, pos_row, bounds, nb, S, tq, tk)
    dx, d_w_in_t, d_w_uq_t, d_w_ukv, d_gq, d_gkv = _bwd_proj(
        dq, dk, dv, xq, xkv, xf, dz, dga, dgb, dpc, rope_tab, w_uq_t, w_ukv_f, w_in_t, gq2, gkv2, S, tm)
    grad_x = dx.reshape(nb, S, D_MODEL)

    g_in = d_w_in_t.reshape(N_CHIPS, 592, D_MODEL)
    g_uq = d_w_uq_t
    g_ukv = d_w_ukv.reshape(KV_LORA, N_CHIPS, 256).transpose(1, 0, 2)
    g_out = d_w_out.reshape(N_CHIPS, 256, D_MODEL)
    wide = lambda a: jnp.pad(a.reshape(1, -1), ((0, 0), (0, D_MODEL - a.size)))
    vec = jnp.concatenate([d_ln_g, d_ln_b, wide(d_pool_scale), wide(d_gq), jnp.zeros((4, D_MODEL), F32),
                           wide(d_gkv), wide(loss_part), jnp.zeros((VEC_ROWS - 10, D_MODEL), F32)], axis=0)
    to_all = lambda a: jnp.broadcast_to(a[None], (N_CHIPS,) + a.shape)
    gs = [g_in, g_uq, g_ukv, g_out, to_all(d_pool_w.reshape(-1, D_MODEL)), to_all(vec)]
    from_sibling = _grad_to_sibling(gs)
    chip_sums, own_sums = _add_sibling_half(gs, from_sibling, place_arr, (BF16,) * N_BIG + (F32, F32))
    from_chips = _grad_to_chips(chip_sums)
    g_all = _halves_exchange(_add_chip_parts(own_sums, from_chips, place_arr))
    g_big = g_all[:N_BIG]
    pw_sum = g_all[N_BIG].reshape(POOL_G * POOL_GD, POOL_GD)
    vec_sum = g_all[N_BIG + 1]

    big = _adamw_big(g_big, [w_in.T, w_uq.T, w_ukv, w_out], [m_w_in.T, m_w_uq.T, m_w_ukv, m_w_out],
                     [v_w_in.T, v_w_uq.T, v_w_ukv, v_w_out])
    two_d = lambda a: a.reshape(-1, a.shape[-1])
    small_names = lambda pw, lg, lb, ps, gq, gkv: [two_d(pw), lg, lb, ps.reshape(1, -1), gq.reshape(1, -1), gkv.reshape(1, -1)]
    small, loss_row = _adamw_small(
        pw_sum, vec_sum,
        small_names(pool_w, ln_g, ln_b, pool_scale, q_norm_g, kv_norm_g),
        small_names(m_pool_w, m_ln_g, m_ln_b, m_pool_scale, m_q_norm_g, m_kv_norm_g),
        small_names(v_pool_w, v_ln_g, v_ln_b, v_pool_scale, v_q_norm_g, v_kv_norm_g))
    loss = loss_row[0, 0]

    def leaves(kind):
        b = [g_big[t] if kind == 0 else big[t][kind - 1] for t in range(N_BIG)]
        b = [b[0].T, b[1].T, b[2], b[3]]
        s = [small[t][kind] for t in range(6)]
        return (b[0], s[4].reshape(Q_LORA), b[1], s[5].reshape(KV_LORA), b[2],
                s[0].reshape(POOL_G, POOL_GD, POOL_GD), s[3].reshape(POOL_W), b[3], s[1], s[2])

    return (loss, grad_x) + leaves(0) + leaves(1) + leaves(2) + leaves(3)
```

```python
import functools

import jax
import jax.numpy as jnp
from jax import lax
from jax.experimental import pallas as pl
from jax.experimental.pallas import tpu as pltpu

F32 = jnp.float32
BF16 = jnp.bfloat16
MESH = pl.DeviceIdType.MESH

HEADS = 4
NOPE = 128
ROPE = 64
HEAD_PAD = 256
Q_LORA = 512
KV_LORA = 256
MLA_W = 512
POOL_W = 512
POOL_G = 4
POOL_GD = 128
D_MODEL = 1024
IN_W = 2368
IN_EXT = 2432
ROPE_THETA = 10000.0
RMS_EPS = 1e-6
LN_EPS = 1e-5
ALPHA = 2.0 ** 0.25
SCALE = 192.0 ** -0.5
LOG2E = 1.4426950408889634
LN2 = 0.6931471805599453
QSCALE = SCALE * LOG2E
NEG = float(jnp.finfo(jnp.float32).min)
HEAD_GROUP = 2
HALO = 16

ADAM_LR = 0.001
ADAM_B1 = 0.9
ADAM_B2 = 0.999
ADAM_EPS = 1e-08
ADAM_WD = 0.01
ADAM_STEP = 10

N_CHIPS = 4
N_BIG = 4
VEC_ROWS = 16

VMEM_LIMIT = 56 * 1024 * 1024


def _cparams(n_grid_dims=0, **kw):
    sem = ("arbitrary",) * n_grid_dims if n_grid_dims else None
    return pltpu.CompilerParams(dimension_semantics=sem, vmem_limit_bytes=VMEM_LIMIT, **kw)


def _full(shape):
    nd = len(shape)
    return pl.BlockSpec(shape, lambda *_: (0,) * nd)


def _dot(a, b):
    return jnp.dot(a, b, preferred_element_type=F32)


def _dot_nt(a, b):
    return lax.dot_general(a, b, (((1,), (1,)), ((), ())), preferred_element_type=F32)


def _dot_tn(a, b):
    return lax.dot_general(a, b, (((0,), (0,)), ((), ())), preferred_element_type=F32)


def _rope_table(pos_col, freq_row):
    lane = lax.broadcasted_iota(jnp.int32, (1, 128), 1)
    ang = pos_col.astype(F32) * freq_row
    return jnp.where(lane < 32, jnp.cos(ang), jnp.where(lane < 64, jnp.sin(ang), 0.0))


def _expand_rope_table(tab):
    lane = lax.broadcasted_iota(jnp.int32, (1, 128), 1)
    second = jnp.logical_and(lane >= 32, lane < 64)
    c = jnp.where(lane < 32, tab, jnp.where(second, pltpu.roll(tab, 32, 1), 0.0))
    sa = jnp.where(lane < 32, pltpu.roll(tab, 96, 1), 0.0)
    sb = jnp.where(second, tab, 0.0)
    return c, sa, sb


def _rope(g, c, sa, sb, sign):
    return g * c + sign * (pltpu.roll(g, 32, 1) * sb - pltpu.roll(g, 96, 1) * sa)


def _place():
    x, y, c = lax.axis_index("x"), lax.axis_index("y"), lax.axis_index("c")
    chips = [(1 - x, y), (x, 1 - y), (1 - x, 1 - y)]
    return x, y, c, chips


def _half_cols(ref, half_index):
    hc = ref.shape[-1] // 2
    lead = tuple(pl.ds(0, n) for n in ref.shape[:-1])
    return ref.at[lead + (pl.ds(half_index * hc, hc),)]


ANY = pl.BlockSpec(memory_space=pl.ANY)


def _weight_gather(slots, valid_rows):
    n = len(slots)

    def body(*refs):
        outs = refs[n:2 * n]
        send_sems, recv_sems = refs[2 * n:]
        x, y, c, chips = _place()
        me = 2 * x + y

        def copy(t, k, chip_idx, half, to):
            hc = slots[t].shape[2] // 2
            blk = outs[t].at[chip_idx, pl.ds(0, valid_rows[t]), pl.ds(half * hc, hc)]
            return pltpu.make_async_remote_copy(
                src_ref=blk, dst_ref=blk, send_sem=send_sems.at[6 * t + k], recv_sem=recv_sems.at[6 * t + k],
                device_id=to, device_id_type=MESH)

        first = [copy(t, j, me, c, (cx, cy, c)) for t in range(n) for j, (cx, cy) in enumerate(chips)]
        for cp in first:
            cp.start()
        passed = []
        for j, (cx, cy) in enumerate(chips):
            for t in range(n):
                copy(t, j, 2 * cx + cy, c, (x, y, c)).wait_recv()
                fwd = copy(t, 3 + j, 2 * cx + cy, c, (x, y, 1 - c))
                fwd.start()
                passed.append(fwd)
        for j, (cx, cy) in enumerate(chips):
            for t in range(n):
                copy(t, 3 + j, 2 * cx + cy, 1 - c, (x, y, c)).wait_recv()
        for cp in first + passed:
            cp.wait_send()

    return pl.pallas_call(
        body, name="weight_gather",
        out_shape=tuple(jax.ShapeDtypeStruct(a.shape, a.dtype) for a in slots),
        in_specs=[ANY] * n, out_specs=(ANY,) * n, input_output_aliases={t: t for t in range(n)},
        scratch_shapes=[pltpu.SemaphoreType.DMA((6 * n,)), pltpu.SemaphoreType.DMA((6 * n,))],
    )(*slots)


def _grad_reduce(gs, wire_dtypes):
    n = len(gs)
    hcs = [g.shape[2] // 2 for g in gs]

    def body(*refs):
        g_refs, out_refs = refs[:n], refs[n:2 * n]
        scr = refs[2 * n:]
        own, sib, wire, got, fin = (scr[i * n:(i + 1) * n] for i in range(5))
        d2d_send, d2d_recv, ici_send, ici_recv, fin_send, fin_recv, loc_in, loc_out = scr[5 * n:]
        x, y, c, chips = _place()
        sibling = (x, y, 1 - c)
        dests = [2 * cx + cy for cx, cy in chips] + [2 * x + y]

        def mine(t):
            return pl.ds(c * hcs[t], hcs[t])

        def theirs(t):
            return pl.ds((1 - c) * hcs[t], hcs[t])

        def rows(t):
            return pl.ds(0, gs[t].shape[1])

        def remote(src, dst, send, recv, to):
            return pltpu.make_async_remote_copy(src_ref=src, dst_ref=dst, send_sem=send, recv_sem=recv,
                                                device_id=to, device_id_type=MESH)

        loads, d2d = [], []
        for j, k in enumerate(dests):
            for t in range(n):
                ld = pltpu.make_async_copy(g_refs[t].at[k, rows(t), mine(t)], own[t].at[j], loc_in.at[4 * t + j])
                ld.start()
                loads.append(ld)
                cp = remote(g_refs[t].at[k, rows(t), theirs(t)], sib[t].at[j],
                            d2d_send.at[4 * t + j], d2d_recv.at[4 * t + j], sibling)
                cp.start()
                d2d.append(cp)

        ici = []
        for j, (cx, cy) in enumerate(chips):
            for t in range(n):
                loads[j * n + t].wait()
                d2d[j * n + t].wait_recv()
                wire[t][j] = (own[t][j] + sib[t][j]).astype(wire_dtypes[t])
                cp = remote(wire[t].at[j], got[t].at[j], ici_send.at[3 * t + j], ici_recv.at[3 * t + j], (cx, cy, c))
                cp.start()
                ici.append(cp)

        last = []
        for t in range(n):
            loads[3 * n + t].wait()
            d2d[3 * n + t].wait_recv()
            for j in range(3):
                ici[j * n + t].wait_recv()
            fin[t][...] = (((own[t][3] + sib[t][3]) + got[t][0].astype(F32))
                           + (got[t][1].astype(F32) + got[t][2].astype(F32)))
            st = pltpu.make_async_copy(fin[t], out_refs[t].at[rows(t), mine(t)], loc_out.at[t])
            st.start()
            cp = remote(fin[t], out_refs[t].at[rows(t), mine(t)], fin_send.at[t], fin_recv.at[t], sibling)
            cp.start()
            last.append((st, cp))
        for t in range(n):
            remote(fin[t], out_refs[t].at[rows(t), theirs(t)], fin_send.at[t], fin_recv.at[t], sibling).wait_recv()
        for cp in d2d + ici:
            cp.wait_send()
        for st, cp in last:
            cp.wait_send()
            st.wait()

    scratch = ([pltpu.VMEM((4, g.shape[1], hc), F32) for g, hc in zip(gs, hcs)]
               + [pltpu.VMEM((4, g.shape[1], hc), F32) for g, hc in zip(gs, hcs)]
               + [pltpu.VMEM((3, g.shape[1], hc), w) for g, hc, w in zip(gs, hcs, wire_dtypes)]
               + [pltpu.VMEM((3, g.shape[1], hc), w) for g, hc, w in zip(gs, hcs, wire_dtypes)]
               + [pltpu.VMEM((g.shape[1], hc), F32) for g, hc in zip(gs, hcs)]
               + [pltpu.SemaphoreType.DMA((4 * n,)), pltpu.SemaphoreType.DMA((4 * n,)),
                  pltpu.SemaphoreType.DMA((3 * n,)), pltpu.SemaphoreType.DMA((3 * n,)),
                  pltpu.SemaphoreType.DMA((n,)), pltpu.SemaphoreType.DMA((n,)),
                  pltpu.SemaphoreType.DMA((4 * n,)), pltpu.SemaphoreType.DMA((n,))])
    return pl.pallas_call(
        body, name="grad_reduce",
        out_shape=tuple(jax.ShapeDtypeStruct(g.shape[1:], F32) for g in gs),
        in_specs=[ANY] * n, out_specs=(ANY,) * n, scratch_shapes=scratch,
        compiler_params=_cparams(),
    )(*gs)


def _adamw_math(g, w, m, v):
    nm = ADAM_B1 * m + (1.0 - ADAM_B1) * g
    nv = ADAM_B2 * v + (1.0 - ADAM_B2) * (g * g)
    m_hat = nm / (1.0 - ADAM_B1 ** ADAM_STEP)
    v_hat = nv / (1.0 - ADAM_B2 ** ADAM_STEP)
    return -ADAM_LR * (m_hat / (jnp.sqrt(v_hat) + ADAM_EPS) + ADAM_WD * w), nm, nv


ADAM_STEPS = 8


def _adamw_big(gs, ws, ms, vs):
    n = len(gs)

    def body(*refs):
        for t in range(n):
            d, nm, nv = _adamw_math(refs[t][...], refs[n + t][...], refs[2 * n + t][...], refs[3 * n + t][...])
            refs[4 * n + 3 * t][...] = d
            refs[4 * n + 3 * t + 1][...] = nm
            refs[4 * n + 3 * t + 2][...] = nv

    def tile_spec(shape):
        rows, cols = shape
        if rows % (8 * ADAM_STEPS) == 0:
            return pl.BlockSpec((rows // ADAM_STEPS, cols), lambda i: (i, 0))
        return pl.BlockSpec((rows, cols // ADAM_STEPS), lambda i: (0, i))

    specs = [tile_spec(g.shape) for g in gs]
    out_specs, out_shape = [], []
    for t in range(n):
        out_specs += [specs[t]] * 3
        out_shape += [jax.ShapeDtypeStruct(gs[t].shape, F32)] * 3
    outs = pl.pallas_call(
        body, name="adamw_big", grid=(ADAM_STEPS,),
        in_specs=specs * 4, out_specs=tuple(out_specs), out_shape=tuple(out_shape),
        compiler_params=_cparams(1),
    )(*gs, *ws, *ms, *vs)
    return [outs[3 * t: 3 * t + 3] for t in range(n)]


def _adamw_small(pw_sum, vec_sum, ws, ms, vs):
    rows = (None, 0, 1, 2, 3, 8)
    n = len(ws)

    def body(pw_ref, vec_ref, *refs):
        outs = refs[3 * n:]
        for t in range(n):
            w_ref, m_ref, v_ref = refs[t], refs[n + t], refs[2 * n + t]
            if rows[t] is None:
                g = pw_ref[...]
            else:
                g = vec_ref[rows[t]:rows[t] + 1, 0:w_ref.shape[1]]
            d, nm, nv = _adamw_math(g, w_ref[...], m_ref[...], v_ref[...])
            outs[4 * t][...] = g
            outs[4 * t + 1][...] = d
            outs[4 * t + 2][...] = nm
            outs[4 * t + 3][...] = nv
        outs[4 * n][...] = vec_ref[9:10, 0:128]

    vm = pl.BlockSpec(memory_space=pltpu.VMEM)
    out_shape = []
    for w in ws:
        out_shape += [jax.ShapeDtypeStruct(w.shape, F32)] * 4
    out_shape.append(jax.ShapeDtypeStruct((1, 128), F32))
    outs = pl.pallas_call(
        body, name="adamw_small", in_specs=[vm] * (2 + 3 * n), out_specs=(vm,) * (4 * n + 1),
        out_shape=tuple(out_shape),
    )(pw_sum, vec_sum, *ws, *ms, *vs)
    return [outs[4 * t: 4 * t + 4] for t in range(n)], outs[4 * n]


def _fwd_proj(x, w_in_t, w_uq_t, w_ukv, gq, gkv, pos_col, freq_row, tm):
    T = x.shape[0]

    def body(x_ref, win_ref, wuq_ref, wukv_ref, gq_ref, gkv_ref, pos_ref, freq_ref,
             xq_ref, xkv_ref, ga_ref, u_ref, gb_ref, q_ref, k_ref, v_ref, tab_ref):
        h = _dot_nt(x_ref[...].astype(BF16), win_ref[...])
        xq = h[:, 0:512]
        xkv = h[:, 512:768]
        xq_ref[...] = xq.astype(BF16)
        xkv_ref[...] = xkv.astype(BF16)
        ga_ref[...] = h[:, 896:1408].astype(BF16)
        u_ref[...] = h[:, 1408:1920].astype(BF16)
        gb_ref[...] = h[:, 1920:2432].astype(BF16)
        tab = _rope_table(pos_ref[...], freq_ref[...])
        tab_ref[...] = tab
        c, sa, sb = _expand_rope_table(tab)
        rq = lax.rsqrt(jnp.mean(xq * xq, axis=-1, keepdims=True) + RMS_EPS)
        q = _dot_nt(((xq * rq) * gq_ref[...]).astype(BF16), wuq_ref[...]) * QSCALE
        rkv = lax.rsqrt(jnp.mean(xkv * xkv, axis=-1, keepdims=True) + RMS_EPS)
        kv = _dot(((xkv * rkv) * gkv_ref[...]).astype(BF16), wukv_ref[...])
        kr = _rope(h[:, 768:896], c, sa, sb, 1.0).astype(BF16)
        for hh in range(HEADS):
            b0 = hh * HEAD_PAD
            q_ref[:, b0:b0 + 128] = q[:, b0:b0 + 128].astype(BF16)
            q_ref[:, b0 + 128:b0 + 256] = _rope(q[:, b0 + 128:b0 + 256], c, sa, sb, 1.0).astype(BF16)
            k_ref[:, b0:b0 + 128] = kv[:, b0:b0 + 128].astype(BF16)
            k_ref[:, b0 + 128:b0 + 256] = kr
            v_ref[:, hh * 128:(hh + 1) * 128] = kv[:, b0 + 128:b0 + 256].astype(BF16)

    row = lambda w: pl.BlockSpec((tm, w), lambda i: (i, 0))
    f = lambda w, dt: jax.ShapeDtypeStruct((T, w), dt)
    return pl.pallas_call(
        body, name="fwd_proj", grid=(T // tm,),
        in_specs=[row(D_MODEL), _full(w_in_t.shape), _full(w_uq_t.shape), _full(w_ukv.shape),
                  _full(gq.shape), _full(gkv.shape), row(1), _full(freq_row.shape)],
        out_specs=(row(512), row(256), row(512), row(512), row(512), row(1024), row(1024), row(512), row(128)),
        out_shape=(f(512, BF16), f(256, BF16), f(512, BF16), f(512, BF16), f(512, BF16),
                   f(1024, BF16), f(1024, BF16), f(512, BF16), f(128, F32)),
        compiler_params=_cparams(1),
    )(x, w_in_t, w_uq_t, w_ukv, gq, gkv, pos_col, freq_row)


def _attn_fwd(q, k, v, pos_col, pos_row, bounds, nb, S, tq, tk):
    T = q.shape[0]
    nq, nk = S // tq, S // tk
    reps = tk // 128
    hg = HEAD_GROUP

    def body(qmin_ref, qmax_ref, kmin_ref, kmax_ref, q_ref, k_ref, v_ref, pc_ref, pr_ref, o_ref, lse_ref,
             m_sc, l_sc, acc_sc):
        b, i = pl.program_id(0), pl.program_id(2)
        m_sc[...] = jnp.full(m_sc.shape, NEG, F32)
        l_sc[...] = jnp.zeros_like(l_sc)
        acc_sc[...] = jnp.zeros_like(acc_sc)
        q_lo = qmin_ref[b * nq + i]
        q_hi = qmax_ref[b * nq + i]

        def tile(j, masked):
            off = pl.multiple_of(j * tk, tk)
            if masked:
                keep = pc_ref[...] >= pr_ref[pl.ds(j, 1), :]
            logits = []
            for g in range(hg):
                qk = slice(g * HEAD_PAD, (g + 1) * HEAD_PAD)
                s = _dot_nt(q_ref[:, qk], k_ref[pl.ds(off, tk), qk])
                if masked:
                    s = jnp.where(keep, s, NEG)
                logits.append(s)
            probs = []
            for g in range(hg):
                hv = slice(g * 128, (g + 1) * 128)
                s = logits[g]
                m_prev = m_sc[:, hv]
                m_new = jnp.maximum(m_prev, jnp.max(s, axis=1, keepdims=True))
                p = jnp.exp2(s - jnp.concatenate([m_new] * reps, axis=1))
                a = jnp.exp2(m_prev - m_new)
                l_sc[:, hv] = a * l_sc[:, hv] + jnp.sum(p, axis=1, keepdims=True)
                m_sc[:, hv] = m_new
                probs.append((p.astype(BF16), a))
            for g in range(hg):
                hv = slice(g * 128, (g + 1) * 128)
                p, a = probs[g]
                acc_sc[:, hv] = a * acc_sc[:, hv] + _dot(p, v_ref[pl.ds(off, tk), hv])

        def step(j, carry):
            visible = kmin_ref[b * nk + j] <= q_hi
            clear = q_lo >= kmax_ref[b * nk + j]

            @pl.when(jnp.logical_and(visible, clear))
            def _():
                tile(j, False)

            @pl.when(jnp.logical_and(visible, jnp.logical_not(clear)))
            def _():
                tile(j, True)
            return carry

        lax.fori_loop(0, nk, step, 0)
        l = l_sc[...]
        o_ref[...] = acc_sc[...] / l
        lse_ref[...] = m_sc[...] + jnp.log2(l)

    ng = HEADS // hg
    stat = pltpu.VMEM((tq, hg * 128), F32)
    return pl.pallas_call(
        body, name="attn_fwd",
        grid_spec=pltpu.PrefetchScalarGridSpec(
            num_scalar_prefetch=4, grid=(nb, ng, nq),
            in_specs=[pl.BlockSpec((tq, hg * HEAD_PAD), lambda b, h, i, *_: (b * nq + i, h)),
                      pl.BlockSpec((S, hg * HEAD_PAD), lambda b, h, i, *_: (b, h)),
                      pl.BlockSpec((S, hg * 128), lambda b, h, i, *_: (b, h)),
                      pl.BlockSpec((tq, 1), lambda b, h, i, *_: (b * nq + i, 0)),
                      pl.BlockSpec((None, nk, tk), lambda b, h, i, *_: (b, 0, 0))],
            out_specs=(pl.BlockSpec((tq, hg * 128), lambda b, h, i, *_: (b * nq + i, h)),
                       pl.BlockSpec((tq, hg * 128), lambda b, h, i, *_: (b * nq + i, h))),
            scratch_shapes=[stat, stat, stat]),
        out_shape=(jax.ShapeDtypeStruct((T, MLA_W), F32), jax.ShapeDtypeStruct((T, MLA_W), F32)),
        compiler_params=_cparams(3),
    )(*bounds, q, k, v, pos_col, pos_row.reshape(nb, nk, tk))


def _mid(x, tgt, o, ga, u, gb, w_out, pool_w, pool_scale, ln_g, ln_b, S, tm):
    T = x.shape[0]
    tps = S // tm
    hb = tm // HALO

    def body(x_ref, tgt_ref, o_ref, ga_ref, u_ref, uh_ref, gb_ref, wout_ref, pw_ref,
             ps_ref, lng_ref, lnb_ref,
             dz_ref, do_ref, delta_ref, dga_ref, dgb_ref, dpc_ref,
             dwout_ref, dpw_ref, dps_ref, dlng_ref, dlnb_ref, loss_ref):
        i = pl.program_id(0)

        @pl.when(i == 0)
        def _():
            dwout_ref[...] = jnp.zeros_like(dwout_ref)
            dpw_ref[...] = jnp.zeros_like(dpw_ref)
            dps_ref[...] = jnp.zeros_like(dps_ref)
            dlng_ref[...] = jnp.zeros_like(dlng_ref)
            dlnb_ref[...] = jnp.zeros_like(dlnb_ref)
            loss_ref[...] = jnp.zeros_like(loss_ref)

        seq_tile = i % tps
        tpos = seq_tile * tm + lax.broadcasted_iota(jnp.int32, (tm, 1), 0)
        ga_v = ga_ref[...].astype(F32)
        sig_a = jax.nn.sigmoid(ga_v)
        silu_a = ga_v * sig_a
        o_v = o_ref[...]
        ya = o_v * silu_a

        u_v = u_ref[...].astype(F32)
        halo = jnp.where(seq_tile == 0, 0.0, uh_ref[...].astype(F32))
        pooled, cnts, mixed = [], [], []
        for g in range(POOL_G):
            lanes = slice(g * POOL_GD, (g + 1) * POOL_GD)
            w = jnp.concatenate([halo[:, lanes], u_v[:, lanes]], axis=0)
            for st in range(g + 1):
                w = w + pltpu.roll(w, 1 << st, 0)
            cnt = jnp.minimum(tpos + 1, 2 << g).astype(F32)
            pg = (w[HALO:, :] / cnt - u_v[:, lanes]).astype(BF16)
            pooled.append(pg)
            cnts.append(cnt)
            mixed.append(_dot(pg, pw_ref[g]))
        mixed = jnp.concatenate(mixed, axis=1)
        ps = ps_ref[...]
        ybp = mixed * ps
        gb_v = gb_ref[...].astype(F32)
        sig_b = jax.nn.sigmoid(gb_v)
        silu_b = gb_v * sig_b
        yb = ybp * silu_b

        cat = jnp.concatenate([ya, yb], axis=1).astype(BF16)
        z = ALPHA * x_ref[...] + _dot(cat, wout_ref[...])
        mu = jnp.mean(z, axis=-1, keepdims=True)
        zc = z - mu
        rstd = lax.rsqrt(jnp.mean(zc * zc, axis=-1, keepdims=True) + LN_EPS)
        zhat = zc * rstd
        lng = lng_ref[...]
        err = (zhat * lng + lnb_ref[...]) - tgt_ref[...]
        row_loss = jnp.sum(err * err, axis=1, keepdims=True)
        loss_ref[...] += jnp.broadcast_to(jnp.sum(row_loss, axis=0, keepdims=True) * (0.5 / D_MODEL), (1, 128))
        dy = err * (1.0 / D_MODEL)
        dlng_ref[...] += jnp.sum(dy * zhat, axis=0, keepdims=True)
        dlnb_ref[...] += jnp.sum(dy, axis=0, keepdims=True)
        dzh = dy * lng
        dz = rstd * (dzh - jnp.mean(dzh, axis=-1, keepdims=True)
                     - zhat * jnp.mean(dzh * zhat, axis=-1, keepdims=True))
        dz_ref[...] = dz
        dzb = dz.astype(BF16)
        dwout_ref[...] += _dot_tn(cat, dzb)
        dcat = _dot_nt(dzb, wout_ref[...])
        dya = dcat[:, :MLA_W]
        dyb = dcat[:, MLA_W:]

        do = dya * silu_a
        do_ref[...] = do.astype(BF16)
        prod = do * o_v
        for hh in range(HEADS):
            lanes = slice(hh * 128, (hh + 1) * 128)
            delta_ref[:, lanes] = jnp.broadcast_to(jnp.sum(prod[:, lanes], axis=1, keepdims=True), (tm, 128))
        dga_ref[...] = (dya * o_v * (sig_a * (1.0 + ga_v * (1.0 - sig_a)))).astype(BF16)
        dgb_ref[...] = (dyb * ybp * (sig_b * (1.0 + gb_v * (1.0 - sig_b)))).astype(BF16)
        dybp = dyb * silu_b
        dps_ref[...] += jnp.sum(dybp * mixed, axis=0, keepdims=True)
        dmixed = (dybp * ps).astype(BF16)
        for g in range(POOL_G):
            lanes = slice(g * POOL_GD, (g + 1) * POOL_GD)
            dpw_ref[g] += _dot_tn(pooled[g], dmixed[:, lanes])
            dpc_ref[:, lanes] = (_dot_nt(dmixed[:, lanes], pw_ref[g]) / cnts[g]).astype(BF16)

    row = lambda w: pl.BlockSpec((tm, w), lambda i: (i, 0))
    f = lambda w, dt: jax.ShapeDtypeStruct((T, w), dt)
    halo_spec = pl.BlockSpec((HALO, POOL_W), lambda i: (jnp.maximum(i * hb - 1, 0), 0))
    return pl.pallas_call(
        body, name="mid", grid=(T // tm,),
        in_specs=[row(D_MODEL), row(D_MODEL), row(MLA_W), row(MLA_W), row(POOL_W), halo_spec, row(POOL_W),
                  _full(w_out.shape), _full(pool_w.shape),
                  _full(pool_scale.shape), _full(ln_g.shape), _full(ln_b.shape)],
        out_specs=(row(D_MODEL), row(MLA_W), row(MLA_W), row(MLA_W), row(POOL_W), row(POOL_W),
                   _full((D_MODEL, D_MODEL)), _full(pool_w.shape), _full((1, POOL_W)),
                   _full((1, D_MODEL)), _full((1, D_MODEL)), _full((1, 128))),
        out_shape=(f(D_MODEL, F32), f(MLA_W, BF16), f(MLA_W, F32), f(MLA_W, BF16), f(POOL_W, BF16), f(POOL_W, BF16),
                   jax.ShapeDtypeStruct((D_MODEL, D_MODEL), F32), jax.ShapeDtypeStruct(pool_w.shape, F32),
                   jax.ShapeDtypeStruct((1, POOL_W), F32), jax.ShapeDtypeStruct((1, D_MODEL), F32),
                   jax.ShapeDtypeStruct((1, D_MODEL), F32), jax.ShapeDtypeStruct((1, 128), F32)),
        compiler_params=_cparams(1),
    )(x, tgt, o, ga, u, u, gb, w_out, pool_w, pool_scale, ln_g, ln_b)


def _attn_bwd(q, k, v, do, lse, delta, pos_col, pos_row, bounds, nb, S, tq, tk):
    T = q.shape[0]
    nq, nk = S // tq, S // tk
    reps = tk // 128
    hg = HEAD_GROUP

    def body(qmin_ref, qmax_ref, kmin_ref, kmax_ref, q_ref, k_ref, v_ref, do_ref, lse_ref, dl_ref, pc_ref, pr_ref,
             dq_out, dk_out, dv_out, dq_ref, dk_ref, dv_ref):
        b, j = pl.program_id(0), pl.program_id(2)

        @pl.when(j == 0)
        def _():
            dq_ref[...] = jnp.zeros_like(dq_ref)

        dk_ref[...] = jnp.zeros_like(dk_ref)
        dv_ref[...] = jnp.zeros_like(dv_ref)
        k_lo = kmin_ref[b * nk + j]
        k_hi = kmax_ref[b * nk + j]

        def tile(i, masked):
            rows = pl.ds(pl.multiple_of(i * tq, tq), tq)
            if masked:
                keep = pc_ref[rows, :] >= pr_ref[...]
            stage = []
            for g in range(hg):
                qk = slice(g * HEAD_PAD, (g + 1) * HEAD_PAD)
                hv = slice(g * 128, (g + 1) * 128)
                s = _dot_nt(q_ref[rows, qk], k_ref[:, qk])
                if masked:
                    s = jnp.where(keep, s, NEG)
                stage.append((s, _dot_nt(do_ref[rows, hv], v_ref[:, hv])))
            grads = []
            for g in range(hg):
                hv = slice(g * 128, (g + 1) * 128)
                s, dp = stage[g]
                p = jnp.exp2(s - jnp.concatenate([lse_ref[rows, hv]] * reps, axis=1))
                ds = (p * (dp - jnp.concatenate([dl_ref[rows, hv]] * reps, axis=1))).astype(BF16)
                grads.append((p.astype(BF16), ds))
            for g in range(hg):
                qk = slice(g * HEAD_PAD, (g + 1) * HEAD_PAD)
                hv = slice(g * 128, (g + 1) * 128)
                p, ds = grads[g]
                dv_ref[:, hv] += _dot_tn(p, do_ref[rows, hv])
                dq_ref[rows, qk] += _dot(ds, k_ref[:, qk])
                dk_ref[:, qk] += _dot_tn(ds, q_ref[rows, qk])

        def step(i, carry):
            visible = k_lo <= qmax_ref[b * nq + i]
            clear = qmin_ref[b * nq + i] >= k_hi

            @pl.when(jnp.logical_and(visible, clear))
            def _():
                tile(i, False)

            @pl.when(jnp.logical_and(visible, jnp.logical_not(clear)))
            def _():
                tile(i, True)
            return carry

        lax.fori_loop(0, nq, step, 0)
        dk_out[...] = dk_ref[...].astype(BF16)
        dv_out[...] = dv_ref[...].astype(BF16)

        @pl.when(j == nk - 1)
        def _():
            dq_out[...] = dq_ref[...].astype(BF16)

    ng = HEADS // hg
    seq = lambda w: pl.BlockSpec((S, w), lambda b, h, j, *_: (b, h))
    blk = lambda w: pl.BlockSpec((tk, w), lambda b, h, j, *_: (b * nk + j, h))
    return pl.pallas_call(
        body, name="attn_bwd",
        grid_spec=pltpu.PrefetchScalarGridSpec(
            num_scalar_prefetch=4, grid=(nb, ng, nk),
            in_specs=[seq(hg * HEAD_PAD), blk(hg * HEAD_PAD), blk(hg * 128),
                      seq(hg * 128), seq(hg * 128), seq(hg * 128),
                      pl.BlockSpec((S, 1), lambda b, h, j, *_: (b, 0)),
                      pl.BlockSpec((None, 1, tk), lambda b, h, j, *_: (b, 0, j))],
            out_specs=(seq(hg * HEAD_PAD), blk(hg * HEAD_PAD), blk(hg * 128)),
            scratch_shapes=[pltpu.VMEM((S, hg * HEAD_PAD), F32), pltpu.VMEM((tk, hg * HEAD_PAD), F32),
                            pltpu.VMEM((tk, hg * 128), F32)]),
        out_shape=(jax.ShapeDtypeStruct((T, HEADS * HEAD_PAD), BF16),
                   jax.ShapeDtypeStruct((T, HEADS * HEAD_PAD), BF16),
                   jax.ShapeDtypeStruct((T, MLA_W), BF16)),
        compiler_params=_cparams(3),
    )(*bounds, q, k, v, do, lse, delta, pos_col, pos_row)


def _bwd_proj(dq, dk, dv, xq, xkv, x, dz, dga, dgb, dpc, rope_tab, w_uq_t, w_ukv, w_in_t, gq, gkv, S, tm):
    T = x.shape[0]
    tps = S // tm
    hb = tm // HALO
    n_tiles = T // tm

    def body(dq_ref, dk_ref, dv_ref, xq_ref, xkv_ref, x_ref, dz_ref, dga_ref, dgb_ref, dpc_ref, dph_ref,
             tab_ref, wuq_ref, wukv_ref, win_ref, gq_ref, gkv_ref,
             dx_ref, dwin_hbm, dwuq_hbm, dwukv_hbm, dgq_ref, dgkv_ref,
             acc_win, acc_wuq, acc_wukv, dh_sc):
        i = pl.program_id(0)

        @pl.when(i == 0)
        def _():
            acc_win[...] = jnp.zeros_like(acc_win)
            acc_wuq[...] = jnp.zeros_like(acc_wuq)
            acc_wukv[...] = jnp.zeros_like(acc_wukv)
            dgq_ref[...] = jnp.zeros_like(dgq_ref)
            dgkv_ref[...] = jnp.zeros_like(dgkv_ref)
            dh_sc[...] = jnp.zeros_like(dh_sc)

        dh_prev = dh_sc[...]
        dx_ref[...] = ALPHA * dz_ref[...] + _dot(dh_prev, win_ref[...])
        acc_win[...] += _dot_tn(dh_prev, x_ref[...].astype(BF16))

        live = jnp.where(i < n_tiles, 1.0, 0.0)
        c, sa, sb = _expand_rope_table(tab_ref[...])
        dq_v = dq_ref[...].astype(F32) * (SCALE * live)
        dk_v = dk_ref[...].astype(F32) * (LN2 * live)
        dv_v = dv_ref[...].astype(F32) * live
        dq_parts, dkv_parts = [], []
        dkr = jnp.zeros((tm, 128), F32)
        for hh in range(HEADS):
            b0 = hh * HEAD_PAD
            dq_parts.append(dq_v[:, b0:b0 + 128].astype(BF16))
            dq_parts.append(_rope(dq_v[:, b0 + 128:b0 + 256], c, sa, sb, -1.0).astype(BF16))
            dkv_parts.append(dk_v[:, b0:b0 + 128].astype(BF16))
            dkv_parts.append(dv_v[:, hh * 128:(hh + 1) * 128].astype(BF16))
            dkr = dkr + dk_v[:, b0 + 128:b0 + 256]
        dqp = jnp.concatenate(dq_parts, axis=1)
        dkvp = jnp.concatenate(dkv_parts, axis=1)
        dkrr = _rope(dkr, c, sa, sb, -1.0)

        def rms_bwd(xv, g, dyn, dg_ref):
            r = lax.rsqrt(jnp.mean(xv * xv, axis=-1, keepdims=True) + RMS_EPS)
            xhat = xv * r
            dg_ref[...] += jnp.sum(dyn * xhat, axis=0, keepdims=True)
            dxh = dyn * g
            return r * (dxh - xhat * jnp.mean(dxh * xhat, axis=-1, keepdims=True))

        xq_v = xq_ref[...].astype(F32)
        gq_v = gq_ref[...]
        rq = lax.rsqrt(jnp.mean(xq_v * xq_v, axis=-1, keepdims=True) + RMS_EPS)
        acc_wuq[...] += _dot_tn(dqp, ((xq_v * rq) * gq_v).astype(BF16))
        dxq = rms_bwd(xq_v, gq_v, _dot(dqp, wuq_ref[...]), dgq_ref)

        xkv_v = xkv_ref[...].astype(F32)
        gkv_v = gkv_ref[...]
        rkv = lax.rsqrt(jnp.mean(xkv_v * xkv_v, axis=-1, keepdims=True) + RMS_EPS)
        acc_wukv[...] += _dot_tn(((xkv_v * rkv) * gkv_v).astype(BF16), dkvp)
        dxkv = rms_bwd(xkv_v, gkv_v, _dot_nt(dkvp, wukv_ref[...]), dgkv_ref)

        seq_tile = i % tps
        tpos = seq_tile * tm + lax.broadcasted_iota(jnp.int32, (tm, 1), 0)
        dpc_v = dpc_ref[...].astype(F32)
        halo = jnp.where(seq_tile == tps - 1, 0.0, dph_ref[...].astype(F32))
        n = tm + HALO
        du = []
        for g in range(POOL_G):
            lanes = slice(g * POOL_GD, (g + 1) * POOL_GD)
            f = jnp.concatenate([dpc_v[:, lanes], halo[:, lanes]], axis=0)
            for st in range(g + 1):
                f = f + pltpu.roll(f, n - (1 << st), 0)
            cnt = jnp.minimum(tpos + 1, 2 << g).astype(F32)
            du.append((f[:tm, :] - dpc_v[:, lanes] * cnt).astype(BF16))

        dh_sc[...] = jnp.concatenate([dxq.astype(BF16), dxkv.astype(BF16), dkrr.astype(BF16), dga_ref[...]]
                                     + du + [dgb_ref[...]], axis=1)

        @pl.when(i == n_tiles)
        def _():
            pltpu.sync_copy(acc_win.at[pl.ds(0, 832)], dwin_hbm.at[pl.ds(0, 832)])
            pltpu.sync_copy(acc_win.at[pl.ds(896, IN_EXT - 896)], dwin_hbm.at[pl.ds(832, IN_W - 832)])
            for hh in range(HEADS):
                pltpu.sync_copy(acc_wuq.at[pl.ds(hh * HEAD_PAD, NOPE + ROPE)], dwuq_hbm.at[hh])
            pltpu.sync_copy(acc_wukv, dwukv_hbm)

    cur = lambda w: pl.BlockSpec((tm, w), lambda i: (jnp.minimum(i, n_tiles - 1), 0))
    prev = lambda w: pl.BlockSpec((tm, w), lambda i: (jnp.maximum(i - 1, 0), 0))
    halo_spec = pl.BlockSpec((HALO, POOL_W), lambda i: (jnp.minimum((i + 1) * hb, T // HALO - 1), 0))
    return pl.pallas_call(
        body, name="bwd_proj", grid=(n_tiles + 1,),
        in_specs=[cur(1024), cur(1024), cur(512), cur(512), cur(256), prev(D_MODEL), prev(D_MODEL),
                  cur(512), cur(512), cur(512), halo_spec, cur(128),
                  _full(w_uq_t.shape), _full(w_ukv.shape), _full(w_in_t.shape), _full(gq.shape), _full(gkv.shape)],
        out_specs=(prev(D_MODEL), ANY, ANY, ANY, _full((1, Q_LORA)), _full((1, KV_LORA))),
        out_shape=(jax.ShapeDtypeStruct((T, D_MODEL), F32),
                   jax.ShapeDtypeStruct((IN_W, D_MODEL), F32),
                   jax.ShapeDtypeStruct((HEADS, NOPE + ROPE, Q_LORA), F32),
                   jax.ShapeDtypeStruct((KV_LORA, 1024), F32),
                   jax.ShapeDtypeStruct((1, Q_LORA), F32), jax.ShapeDtypeStruct((1, KV_LORA), F32)),
        scratch_shapes=[pltpu.VMEM((IN_EXT, D_MODEL), F32), pltpu.VMEM((HEADS * HEAD_PAD, Q_LORA), F32),
                        pltpu.VMEM((KV_LORA, 1024), F32), pltpu.VMEM((tm, IN_EXT), BF16)],
        compiler_params=_cparams(1),
    )(dq, dk, dv, xq, xkv, x, dz, dga, dgb, dpc, dpc, rope_tab, w_uq_t, w_ukv, w_in_t, gq, gkv)


def kernel(x, positions, w_in, q_norm_g, w_uq, kv_norm_g, w_ukv, pool_w, pool_scale, w_out, ln_g, ln_b, loss_target, m_w_in, m_q_norm_g, m_w_uq, m_kv_norm_g, m_w_ukv, m_pool_w, m_pool_scale, m_w_out, m_ln_g, m_ln_b, v_w_in, v_q_norm_g, v_w_uq, v_kv_norm_g, v_w_ukv, v_pool_w, v_pool_scale, v_w_out, v_ln_g, v_ln_b):
    nb, S, _ = x.shape
    T = nb * S
    tm = min(256, S)
    tq = min(512, S)
    tk = min(512, S)
    assert S % tm == 0 and tm % HALO == 0 and S % tq == 0 and S % tk == 0

    cx, cy, cc = lax.axis_index("x"), lax.axis_index("y"), lax.axis_index("c")
    me = 2 * cx + cy

    def own_slot(w, slot_rows):
        blk = jnp.pad(w.astype(BF16), ((0, slot_rows - w.shape[0]), (0, 0)))
        return lax.dynamic_update_slice(jnp.zeros((N_CHIPS,) + blk.shape, BF16), blk[None], (me, 0, 0))

    w_in_g, w_uq_g, w_ukv_g, w_out_g = _weight_gather(
        [own_slot(w_in.T, 592), own_slot(w_uq.T, HEAD_PAD), own_slot(w_ukv, KV_LORA), own_slot(w_out, 256)],
        (592, NOPE + ROPE, KV_LORA, 256))
    w_in_f = w_in_g.reshape(IN_W, D_MODEL)
    w_in_t = jnp.concatenate([w_in_f[:832], jnp.zeros((64, D_MODEL), BF16), w_in_f[832:]], axis=0)
    w_uq_t = w_uq_g.reshape(HEADS * HEAD_PAD, Q_LORA)
    w_ukv_f = w_ukv_g.transpose(1, 0, 2).reshape(KV_LORA, 1024)
    w_out_f = w_out_g.reshape(D_MODEL, D_MODEL)
    pool_w_b = pool_w.astype(BF16)
    gq2 = q_norm_g.reshape(1, Q_LORA)
    gkv2 = kv_norm_g.reshape(1, KV_LORA)
    ps2 = pool_scale.reshape(1, POOL_W)

    half = ROPE // 2
    inv_freq = ROPE_THETA ** (-jnp.arange(half, dtype=F32) / half)
    freq_row = jnp.concatenate([inv_freq, inv_freq, jnp.zeros((2 * half,), F32)]).reshape(1, 128)
    pos_col = positions.reshape(T, 1)
    pos_row = positions.reshape(nb, 1, S)
    pos_q = positions.reshape(nb, S // tq, tq)
    pos_k = positions.reshape(nb, S // tk, tk)
    bounds = (jnp.min(pos_q, axis=2).reshape(-1), jnp.max(pos_q, axis=2).reshape(-1),
              jnp.min(pos_k, axis=2).reshape(-1), jnp.max(pos_k, axis=2).reshape(-1))

    xf = x.reshape(T, D_MODEL)
    tgt = loss_target.reshape(T, D_MODEL)

    xq, xkv, ga, u, gb, q, k, v, rope_tab = _fwd_proj(xf, w_in_t, w_uq_t, w_ukv_f, gq2, gkv2, pos_col, freq_row, tm)
    o, lse = _attn_fwd(q, k, v, pos_col, pos_row, bounds, nb, S, tq, tk)

    (dz, do, delta, dga, dgb, dpc, d_w_out, d_pool_w, d_pool_scale, d_ln_g, d_ln_b, loss_part) = _mid(
        xf, tgt, o, ga, u, gb, w_out_f, pool_w_b, ps2, ln_g, ln_b, S, tm)

    dq, dk, dv = _attn_bwd(q, k, v, do, lse, delta, pos_col, pos_row, bounds, nb, S, tq, tk)
    dx, d_w_in_t, d_w_uq_t, d_w_ukv, d_gq, d_gkv = _bwd_proj(
        dq, dk, dv, xq, xkv, xf, dz, dga, dgb, dpc, rope_tab, w_uq_t, w_ukv_f, w_in_t, gq2, gkv2, S, tm)
    grad_x = dx.reshape(nb, S, D_MODEL)

    g_in = d_w_in_t.reshape(N_CHIPS, 592, D_MODEL)
    g_uq = d_w_uq_t
    g_ukv = d_w_ukv.reshape(KV_LORA, N_CHIPS, 256).transpose(1, 0, 2)
    g_out = d_w_out.reshape(N_CHIPS, 256, D_MODEL)
    wide = lambda a: jnp.pad(a.reshape(1, -1), ((0, 0), (0, D_MODEL - a.size)))
    vec = jnp.concatenate([d_ln_g, d_ln_b, wide(d_pool_scale), wide(d_gq), jnp.zeros((4, D_MODEL), F32),
                           wide(d_gkv), wide(loss_part), jnp.zeros((VEC_ROWS - 10, D_MODEL), F32)], axis=0)
    to_all = lambda a: jnp.broadcast_to(a[None], (N_CHIPS,) + a.shape)
    gs = [g_in, g_uq, g_ukv, g_out, to_all(d_pool_w.reshape(-1, D_MODEL)), to_all(vec)]
    g_all = _grad_reduce(gs, (BF16,) * N_BIG + (F32, F32))
    g_big = g_all[:N_BIG]
    pw_sum = g_all[N_BIG].reshape(POOL_G * POOL_GD, POOL_GD)
    vec_sum = g_all[N_BIG + 1]

    big = _adamw_big(g_big, [w_in.T, w_uq.T, w_ukv, w_out], [m_w_in.T, m_w_uq.T, m_w_ukv, m_w_out],
                     [v_w_in.T, v_w_uq.T, v_w_ukv, v_w_out])
    two_d = lambda a: a.reshape(-1, a.shape[-1])
    small_names = lambda pw, lg, lb, ps, gq, gkv: [two_d(pw), lg, lb, ps.reshape(1, -1), gq.reshape(1, -1), gkv.reshape(1, -1)]
    small, loss_row = _adamw_small(
        pw_sum, vec_sum,
        small_names(pool_w, ln_g, ln_b, pool_scale, q_norm_g, kv_norm_g),
        small_names(m_pool_w, m_ln_g, m_ln_b, m_pool_scale, m_q_norm_g, m_kv_norm_g),
        small_names(v_pool_w, v_ln_g, v_ln_b, v_pool_scale, v_q_norm_g, v_kv_norm_g))
    loss = loss_row[0, 0]

    def leaves(kind):
        b = [g_big[t] if kind == 0 else big[t][kind - 1] for t in range(N_BIG)]
        b = [b[0].T, b[1].T, b[2], b[3]]
        s = [small[t][kind] for t in range(6)]
        return (b[0], s[4].reshape(Q_LORA), b[1], s[5].reshape(KV_LORA), b[2],
                s[0].reshape(POOL_G, POOL_GD, POOL_GD), s[3].reshape(POOL_W), b[3], s[1], s[2])

    return (loss, grad_x) + leaves(0) + leaves(1) + leaves(2) + leaves(3)
```

```python
import jax
import jax.numpy as jnp
from jax import lax
from jax.experimental import pallas as pl
from jax.experimental.pallas import tpu as pltpu

F32 = jnp.float32
BF16 = jnp.bfloat16
MESH = pl.DeviceIdType.MESH

HEADS = 4
NOPE = 128
ROPE = 64
HEAD_PAD = 256
Q_LORA = 512
KV_LORA = 256
MLA_W = 512
POOL_W = 512
POOL_G = 4
POOL_GD = 128
D_MODEL = 1024
IN_W = 2368
IN_EXT = 2432
COL_KV, COL_KR, COL_GA, COL_U, COL_GB = 512, 768, 896, 1408, 1920
ROPE_END = COL_KR + 64
IN_SHARD = IN_W // 4
ROPE_THETA = 10000.0
RMS_EPS = 1e-6
LN_EPS = 1e-5
ALPHA = 2.0 ** 0.25
SCALE = 192.0 ** -0.5
LOG2E = 1.4426950408889634
LN2 = 0.6931471805599453
QSCALE = SCALE * LOG2E
NEG = float(jnp.finfo(jnp.float32).min)
HEAD_GROUP = 2
HALO = 16

ADAM_LR = 0.001
ADAM_B1 = 0.9
ADAM_B2 = 0.999
ADAM_EPS = 1e-08
ADAM_WD = 0.01
ADAM_STEP = 10

N_CHIPS = 4
N_BIG = 4
VEC_ROWS = 16

VMEM_LIMIT = 56 * 1024 * 1024


def _cparams(n_grid_dims=0, **kw):
    sem = ("arbitrary",) * n_grid_dims if n_grid_dims else None
    return pltpu.CompilerParams(dimension_semantics=sem, vmem_limit_bytes=VMEM_LIMIT, **kw)


def _full(shape):
    nd = len(shape)
    return pl.BlockSpec(shape, lambda *_: (0,) * nd)


def _dot(a, b):
    return jnp.dot(a, b, preferred_element_type=F32)


def _dot_nt(a, b):
    return lax.dot_general(a, b, (((1,), (1,)), ((), ())), preferred_element_type=F32)


def _dot_tn(a, b):
    return lax.dot_general(a, b, (((0,), (0,)), ((), ())), preferred_element_type=F32)


def _rope_table(pos_col, freq_row):
    lane = lax.broadcasted_iota(jnp.int32, (1, 128), 1)
    ang = pos_col.astype(F32) * freq_row
    return jnp.where(lane < 32, jnp.cos(ang), jnp.where(lane < 64, jnp.sin(ang), 0.0))


def _expand_rope_table(tab):
    lane = lax.broadcasted_iota(jnp.int32, (1, 128), 1)
    second = jnp.logical_and(lane >= 32, lane < 64)
    c = jnp.where(lane < 32, tab, jnp.where(second, pltpu.roll(tab, 32, 1), 0.0))
    sa = jnp.where(lane < 32, pltpu.roll(tab, 96, 1), 0.0)
    sb = jnp.where(second, tab, 0.0)
    return c, sa, sb


def _rope(g, c, sa, sb, sign):
    return g * c + sign * (pltpu.roll(g, 32, 1) * sb - pltpu.roll(g, 96, 1) * sa)


def _place():
    x, y, c = lax.axis_index("x"), lax.axis_index("y"), lax.axis_index("c")
    chips = [(1 - x, y), (x, 1 - y), (1 - x, 1 - y)]
    return x, y, c, chips


ANY = pl.BlockSpec(memory_space=pl.ANY)


def _weight_gather(slots, valid_rows):
    n = len(slots)

    def body(*refs):
        outs = refs[n:2 * n]
        send_sems, recv_sems = refs[2 * n:]
        x, y, c, chips = _place()
        me = 2 * x + y

        def copy(t, k, chip_idx, half, to):
            hc = slots[t].shape[2] // 2
            blk = outs[t].at[chip_idx, pl.ds(0, valid_rows[t]), pl.ds(half * hc, hc)]
            return pltpu.make_async_remote_copy(
                src_ref=blk, dst_ref=blk, send_sem=send_sems.at[6 * t + k], recv_sem=recv_sems.at[6 * t + k],
                device_id=to, device_id_type=MESH)

        first = [copy(t, j, me, c, (cx, cy, c)) for t in range(n) for j, (cx, cy) in enumerate(chips)]
        for cp in first:
            cp.start()
        passed = []
        for j, (cx, cy) in enumerate(chips):
            for t in range(n):
                copy(t, j, 2 * cx + cy, c, (x, y, c)).wait_recv()
                fwd = copy(t, 3 + j, 2 * cx + cy, c, (x, y, 1 - c))
                fwd.start()
                passed.append(fwd)
        for j, (cx, cy) in enumerate(chips):
            for t in range(n):
                copy(t, 3 + j, 2 * cx + cy, 1 - c, (x, y, c)).wait_recv()
        for cp in first + passed:
            cp.wait_send()

    return pl.pallas_call(
        body, name="weight_gather",
        out_shape=tuple(jax.ShapeDtypeStruct(a.shape, a.dtype) for a in slots),
        in_specs=[ANY] * n, out_specs=(ANY,) * n, input_output_aliases={t: t for t in range(n)},
        scratch_shapes=[pltpu.SemaphoreType.DMA((6 * n,)), pltpu.SemaphoreType.DMA((6 * n,))],
    )(*slots)


def _grad_reduce(gs, wire_dtypes):
    n = len(gs)
    hcs = [g.shape[2] // 2 for g in gs]

    def body(*refs):
        g_refs, out_refs = refs[:n], refs[n:2 * n]
        scr = refs[2 * n:]
        own, sib, wire, got, fin = (scr[i * n:(i + 1) * n] for i in range(5))
        d2d_send, d2d_recv, ici_send, ici_recv, fin_send, fin_recv, loc_in, loc_out = scr[5 * n:]
        x, y, c, chips = _place()
        sibling = (x, y, 1 - c)
        dests = [2 * cx + cy for cx, cy in chips] + [2 * x + y]

        def mine(t):
            return pl.ds(c * hcs[t], hcs[t])

        def theirs(t):
            return pl.ds((1 - c) * hcs[t], hcs[t])

        def rows(t):
            return pl.ds(0, gs[t].shape[1])

        def remote(src, dst, send, recv, to):
            return pltpu.make_async_remote_copy(src_ref=src, dst_ref=dst, send_sem=send, recv_sem=recv,
                                                device_id=to, device_id_type=MESH)

        loads, d2d = [], []
        for j, k in enumerate(dests):
            for t in range(n):
                ld = pltpu.make_async_copy(g_refs[t].at[k, rows(t), mine(t)], own[t].at[j], loc_in.at[4 * t + j])
                ld.start()
                loads.append(ld)
                cp = remote(g_refs[t].at[k, rows(t), theirs(t)], sib[t].at[j],
                            d2d_send.at[4 * t + j], d2d_recv.at[4 * t + j], sibling)
                cp.start()
                d2d.append(cp)

        ici = []
        for j, (cx, cy) in enumerate(chips):
            for t in range(n):
                loads[j * n + t].wait()
                d2d[j * n + t].wait_recv()
                wire[t][j] = (own[t][j] + sib[t][j]).astype(wire_dtypes[t])
                cp = remote(wire[t].at[j], got[t].at[j], ici_send.at[3 * t + j], ici_recv.at[3 * t + j], (cx, cy, c))
                cp.start()
                ici.append(cp)

        last = []
        for t in range(n):
            loads[3 * n + t].wait()
            d2d[3 * n + t].wait_recv()
            for j in range(3):
                ici[j * n + t].wait_recv()
            fin[t][...] = (((own[t][3] + sib[t][3]) + got[t][0].astype(F32))
                           + (got[t][1].astype(F32) + got[t][2].astype(F32)))
            st = pltpu.make_async_copy(fin[t], out_refs[t].at[rows(t), mine(t)], loc_out.at[t])
            st.start()
            cp = remote(fin[t], out_refs[t].at[rows(t), mine(t)], fin_send.at[t], fin_recv.at[t], sibling)
            cp.start()
            last.append((st, cp))
        for t in range(n):
            remote(fin[t], out_refs[t].at[rows(t), theirs(t)], fin_send.at[t], fin_recv.at[t], sibling).wait_recv()
        for cp in d2d + ici:
            cp.wait_send()
        for st, cp in last:
            cp.wait_send()
            st.wait()

    scratch = ([pltpu.VMEM((4, g.shape[1], hc), F32) for g, hc in zip(gs, hcs)]
               + [pltpu.VMEM((4, g.shape[1], hc), F32) for g, hc in zip(gs, hcs)]
               + [pltpu.VMEM((3, g.shape[1], hc), w) for g, hc, w in zip(gs, hcs, wire_dtypes)]
               + [pltpu.VMEM((3, g.shape[1], hc), w) for g, hc, w in zip(gs, hcs, wire_dtypes)]
               + [pltpu.VMEM((g.shape[1], hc), F32) for g, hc in zip(gs, hcs)]
               + [pltpu.SemaphoreType.DMA((4 * n,)), pltpu.SemaphoreType.DMA((4 * n,)),
                  pltpu.SemaphoreType.DMA((3 * n,)), pltpu.SemaphoreType.DMA((3 * n,)),
                  pltpu.SemaphoreType.DMA((n,)), pltpu.SemaphoreType.DMA((n,)),
                  pltpu.SemaphoreType.DMA((4 * n,)), pltpu.SemaphoreType.DMA((n,))])
    return pl.pallas_call(
        body, name="grad_reduce",
        out_shape=tuple(jax.ShapeDtypeStruct(g.shape[1:], F32) for g in gs),
        in_specs=[ANY] * n, out_specs=(ANY,) * n, scratch_shapes=scratch,
        compiler_params=_cparams(),
    )(*gs)


def _adamw_math(g, w, m, v):
    nm = ADAM_B1 * m + (1.0 - ADAM_B1) * g
    nv = ADAM_B2 * v + (1.0 - ADAM_B2) * (g * g)
    m_hat = nm / (1.0 - ADAM_B1 ** ADAM_STEP)
    v_hat = nv / (1.0 - ADAM_B2 ** ADAM_STEP)
    return -ADAM_LR * (m_hat / (jnp.sqrt(v_hat) + ADAM_EPS) + ADAM_WD * w), nm, nv


ADAM_STEPS = 8


def _adamw_big(gs, ws, ms, vs):
    n = len(gs)

    def body(*refs):
        for t in range(n):
            d, nm, nv = _adamw_math(refs[t][...], refs[n + t][...], refs[2 * n + t][...], refs[3 * n + t][...])
            refs[4 * n + 3 * t][...] = d
            refs[4 * n + 3 * t + 1][...] = nm
            refs[4 * n + 3 * t + 2][...] = nv

    def tile_spec(shape):
        rows, cols = shape
        if rows % (8 * ADAM_STEPS) == 0:
            return pl.BlockSpec((rows // ADAM_STEPS, cols), lambda i: (i, 0))
        return pl.BlockSpec((rows, cols // ADAM_STEPS), lambda i: (0, i))

    specs = [tile_spec(g.shape) for g in gs]
    out_specs, out_shape = [], []
    for t in range(n):
        out_specs += [specs[t]] * 3
        out_shape += [jax.ShapeDtypeStruct(gs[t].shape, F32)] * 3
    outs = pl.pallas_call(
        body, name="adamw_big", grid=(ADAM_STEPS,),
        in_specs=specs * 4, out_specs=tuple(out_specs), out_shape=tuple(out_shape),
        compiler_params=_cparams(1),
    )(*gs, *ws, *ms, *vs)
    return [outs[3 * t: 3 * t + 3] for t in range(n)]


def _adamw_small(pw_sum, vec_sum, ws, ms, vs):
    rows = (None, 0, 1, 2, 3, 8)
    n = len(ws)

    def body(pw_ref, vec_ref, *refs):
        outs = refs[3 * n:]
        for t in range(n):
            w_ref, m_ref, v_ref = refs[t], refs[n + t], refs[2 * n + t]
            if rows[t] is None:
                g = pw_ref[...]
            else:
                g = vec_ref[rows[t]:rows[t] + 1, 0:w_ref.shape[1]]
            d, nm, nv = _adamw_math(g, w_ref[...], m_ref[...], v_ref[...])
            outs[4 * t][...] = g
            outs[4 * t + 1][...] = d
            outs[4 * t + 2][...] = nm
            outs[4 * t + 3][...] = nv
        outs[4 * n][...] = vec_ref[9:10, 0:128]

    vm = pl.BlockSpec(memory_space=pltpu.VMEM)
    out_shape = []
    for w in ws:
        out_shape += [jax.ShapeDtypeStruct(w.shape, F32)] * 4
    out_shape.append(jax.ShapeDtypeStruct((1, 128), F32))
    outs = pl.pallas_call(
        body, name="adamw_small", in_specs=[vm] * (2 + 3 * n), out_specs=(vm,) * (4 * n + 1),
        out_shape=tuple(out_shape),
    )(pw_sum, vec_sum, *ws, *ms, *vs)
    return [outs[4 * t: 4 * t + 4] for t in range(n)], outs[4 * n]


def _fwd_proj(x, w_in_t, w_uq_t, w_ukv, gq, gkv, pos_col, freq_row, w_out_slots, tm):
    T = x.shape[0]
    n_steps = T // tm
    fwd_step = n_steps // 2

    def body(x_ref, win_ref, wuq_ref, wukv_ref, gq_ref, gkv_ref, pos_ref, freq_ref, wo_in,
             xq_ref, xkv_ref, ga_ref, u_ref, gb_ref, q_ref, k_ref, v_ref, tab_ref, wo_ref, send_sems, recv_sems):
        i = pl.program_id(0)
        px, py, pc, chips = _place()
        hc = D_MODEL // 2

        def wo_copy(k, chip_idx, half, to):
            blk = wo_ref.at[chip_idx, pl.ds(0, 256), pl.ds(half * hc, hc)]
            return pltpu.make_async_remote_copy(src_ref=blk, dst_ref=blk, send_sem=send_sems.at[k],
                                                recv_sem=recv_sems.at[k], device_id=to, device_id_type=MESH)

        @pl.when(i == 0)
        def _():
            for j, (cx, cy) in enumerate(chips):
                wo_copy(j, 2 * px + py, pc, (cx, cy, pc)).start()

        @pl.when(i == fwd_step)
        def _():
            for j, (cx, cy) in enumerate(chips):
                wo_copy(j, 2 * cx + cy, pc, (px, py, pc)).wait_recv()
                wo_copy(3 + j, 2 * cx + cy, pc, (px, py, 1 - pc)).start()

        @pl.when(i == n_steps - 1)
        def _():
            for j, (cx, cy) in enumerate(chips):
                wo_copy(3 + j, 2 * cx + cy, 1 - pc, (px, py, pc)).wait_recv()
            for j, (cx, cy) in enumerate(chips):
                wo_copy(j, 2 * px + py, pc, (cx, cy, pc)).wait_send()
                wo_copy(3 + j, 2 * cx + cy, pc, (px, py, 1 - pc)).wait_send()

        h = _dot_nt(x_ref[...].astype(BF16), win_ref[...])
        xq = h[:, 0:COL_KV]
        xkv = h[:, COL_KV:COL_KR]
        xq_ref[...] = xq.astype(BF16)
        xkv_ref[...] = xkv.astype(BF16)
        ga_ref[...] = h[:, COL_GA:COL_U].astype(BF16)
        u_ref[...] = h[:, COL_U:COL_GB].astype(BF16)
        gb_ref[...] = h[:, COL_GB:IN_EXT].astype(BF16)
        tab = _rope_table(pos_ref[...], freq_ref[...])
        tab_ref[...] = tab
        c, sa, sb = _expand_rope_table(tab)
        rq = lax.rsqrt(jnp.mean(xq * xq, axis=-1, keepdims=True) + RMS_EPS)
        q = _dot_nt(((xq * rq) * gq_ref[...]).astype(BF16), wuq_ref[...]) * QSCALE
        rkv = lax.rsqrt(jnp.mean(xkv * xkv, axis=-1, keepdims=True) + RMS_EPS)
        kv = _dot(((xkv * rkv) * gkv_ref[...]).astype(BF16), wukv_ref[...])
        kr = _rope(h[:, COL_KR:COL_GA], c, sa, sb, 1.0).astype(BF16)
        for hh in range(HEADS):
            b0 = hh * HEAD_PAD
            q_ref[:, b0:b0 + 128] = q[:, b0:b0 + 128].astype(BF16)
            q_ref[:, b0 + 128:b0 + 256] = _rope(q[:, b0 + 128:b0 + 256], c, sa, sb, 1.0).astype(BF16)
            k_ref[:, b0:b0 + 128] = kv[:, b0:b0 + 128].astype(BF16)
            k_ref[:, b0 + 128:b0 + 256] = kr
            v_ref[:, hh * 128:(hh + 1) * 128] = kv[:, b0 + 128:b0 + 256].astype(BF16)

    row = lambda w: pl.BlockSpec((tm, w), lambda i: (i, 0))
    f = lambda w, dt: jax.ShapeDtypeStruct((T, w), dt)
    return pl.pallas_call(
        body, name="fwd_proj", grid=(n_steps,),
        in_specs=[row(D_MODEL), _full(w_in_t.shape), _full(w_uq_t.shape), _full(w_ukv.shape),
                  _full(gq.shape), _full(gkv.shape), row(1), _full(freq_row.shape), ANY],
        out_specs=(row(512), row(256), row(512), row(512), row(512), row(1024), row(1024), row(512), row(128), ANY),
        out_shape=(f(512, BF16), f(256, BF16), f(512, BF16), f(512, BF16), f(512, BF16),
                   f(1024, BF16), f(1024, BF16), f(512, BF16), f(128, F32),
                   jax.ShapeDtypeStruct(w_out_slots.shape, BF16)),
        input_output_aliases={8: 9},
        scratch_shapes=[pltpu.SemaphoreType.DMA((6,)), pltpu.SemaphoreType.DMA((6,))],
        compiler_params=_cparams(1),
    )(x, w_in_t, w_uq_t, w_ukv, gq, gkv, pos_col, freq_row, w_out_slots)


def _attn_fwd(q, k, v, pos_col, pos_row, bounds, nb, S, tq, tk):
    T = q.shape[0]
    nq, nk = S // tq, S // tk
    reps = tk // 128
    hg = HEAD_GROUP

    def body(qmin_ref, qmax_ref, kmin_ref, kmax_ref, q_ref, k_ref, v_ref, pc_ref, pr_ref, o_ref, lse_ref,
             m_sc, l_sc, acc_sc):
        b, i = pl.program_id(0), pl.program_id(2)
        m_sc[...] = jnp.full(m_sc.shape, NEG, F32)
        l_sc[...] = jnp.zeros_like(l_sc)
        acc_sc[...] = jnp.zeros_like(acc_sc)
        q_lo = qmin_ref[b * nq + i]
        q_hi = qmax_ref[b * nq + i]

        def tile(j, masked):
            off = pl.multiple_of(j * tk, tk)
            if masked:
                keep = pc_ref[...] >= pr_ref[pl.ds(j, 1), :]
            logits = []
            for g in range(hg):
                qk = slice(g * HEAD_PAD, (g + 1) * HEAD_PAD)
                s = _dot_nt(q_ref[:, qk], k_ref[pl.ds(off, tk), qk])
                if masked:
                    s = jnp.where(keep, s, NEG)
                logits.append(s)
            probs = []
            for g in range(hg):
                hv = slice(g * 128, (g + 1) * 128)
                s = logits[g]
                m_prev = m_sc[:, hv]
                m_new = jnp.maximum(m_prev, jnp.max(s, axis=1, keepdims=True))
                p = jnp.exp2(s - jnp.concatenate([m_new] * reps, axis=1))
                a = jnp.exp2(m_prev - m_new)
                l_sc[:, hv] = a * l_sc[:, hv] + jnp.sum(p, axis=1, keepdims=True)
                m_sc[:, hv] = m_new
                probs.append((p.astype(BF16), a))
            for g in range(hg):
                hv = slice(g * 128, (g + 1) * 128)
                p, a = probs[g]
                acc_sc[:, hv] = a * acc_sc[:, hv] + _dot(p, v_ref[pl.ds(off, tk), hv])

        def step(j, carry):
            visible = kmin_ref[b * nk + j] <= q_hi
            clear = q_lo >= kmax_ref[b * nk + j]

            @pl.when(jnp.logical_and(visible, clear))
            def _():
                tile(j, False)

            @pl.when(jnp.logical_and(visible, jnp.logical_not(clear)))
            def _():
                tile(j, True)
            return carry

        lax.fori_loop(0, nk, step, 0)
        l = l_sc[...]
        o_ref[...] = acc_sc[...] / l
        lse_ref[...] = m_sc[...] + jnp.log2(l)

    ng = HEADS // hg
    stat = pltpu.VMEM((tq, hg * 128), F32)
    return pl.pallas_call(
        body, name="attn_fwd",
        grid_spec=pltpu.PrefetchScalarGridSpec(
            num_scalar_prefetch=4, grid=(nb, ng, nq),
            in_specs=[pl.BlockSpec((tq, hg * HEAD_PAD), lambda b, h, i, *_: (b * nq + i, h)),
                      pl.BlockSpec((S, hg * HEAD_PAD), lambda b, h, i, *_: (b, h)),
                      pl.BlockSpec((S, hg * 128), lambda b, h, i, *_: (b, h)),
                      pl.BlockSpec((tq, 1), lambda b, h, i, *_: (b * nq + i, 0)),
                      pl.BlockSpec((None, nk, tk), lambda b, h, i, *_: (b, 0, 0))],
            out_specs=(pl.BlockSpec((tq, hg * 128), lambda b, h, i, *_: (b * nq + i, h)),
                       pl.BlockSpec((tq, hg * 128), lambda b, h, i, *_: (b * nq + i, h))),
            scratch_shapes=[stat, stat, stat]),
        out_shape=(jax.ShapeDtypeStruct((T, MLA_W), F32), jax.ShapeDtypeStruct((T, MLA_W), F32)),
        compiler_params=_cparams(3),
    )(*bounds, q, k, v, pos_col, pos_row.reshape(nb, nk, tk))


def _mid(x, tgt, o, ga, u, gb, w_out, pool_w, pool_scale, ln_g, ln_b, S, tm):
    T = x.shape[0]
    tps = S // tm
    hb = tm // HALO

    def body(x_ref, tgt_ref, o_ref, ga_ref, u_ref, uh_ref, gb_ref, wout_ref, pw_ref,
             ps_ref, lng_ref, lnb_ref,
             dz_ref, do_ref, delta_ref, dga_ref, dgb_ref, dpc_ref,
             dwout_ref, dpw_ref, dps_ref, dlng_ref, dlnb_ref, loss_ref):
        i = pl.program_id(0)

        @pl.when(i == 0)
        def _():
            dwout_ref[...] = jnp.zeros_like(dwout_ref)
            dpw_ref[...] = jnp.zeros_like(dpw_ref)
            dps_ref[...] = jnp.zeros_like(dps_ref)
            dlng_ref[...] = jnp.zeros_like(dlng_ref)
            dlnb_ref[...] = jnp.zeros_like(dlnb_ref)
            loss_ref[...] = jnp.zeros_like(loss_ref)

        seq_tile = i % tps
        tpos = seq_tile * tm + lax.broadcasted_iota(jnp.int32, (tm, 1), 0)
        ga_v = ga_ref[...].astype(F32)
        sig_a = jax.nn.sigmoid(ga_v)
        silu_a = ga_v * sig_a
        o_v = o_ref[...]
        ya = o_v * silu_a

        u_v = u_ref[...].astype(F32)
        halo = jnp.where(seq_tile == 0, 0.0, uh_ref[...].astype(F32))
        pooled, cnts, mixed = [], [], []
        for g in range(POOL_G):
            lanes = slice(g * POOL_GD, (g + 1) * POOL_GD)
            w = jnp.concatenate([halo[:, lanes], u_v[:, lanes]], axis=0)
            for st in range(g + 1):
                w = w + pltpu.roll(w, 1 << st, 0)
            cnt = jnp.minimum(tpos + 1, 2 << g).astype(F32)
            pg = (w[HALO:, :] / cnt - u_v[:, lanes]).astype(BF16)
            pooled.append(pg)
            cnts.append(cnt)
            mixed.append(_dot(pg, pw_ref[g]))
        mixed = jnp.concatenate(mixed, axis=1)
        ps = ps_ref[...]
        ybp = mixed * ps
        gb_v = gb_ref[...].astype(F32)
        sig_b = jax.nn.sigmoid(gb_v)
        silu_b = gb_v * sig_b
        yb = ybp * silu_b

        cat = jnp.concatenate([ya, yb], axis=1).astype(BF16)
        z = ALPHA * x_ref[...] + _dot(cat, wout_ref[...])
        mu = jnp.mean(z, axis=-1, keepdims=True)
        zc = z - mu
        rstd = lax.rsqrt(jnp.mean(zc * zc, axis=-1, keepdims=True) + LN_EPS)
        zhat = zc * rstd
        lng = lng_ref[...]
        err = (zhat * lng + lnb_ref[...]) - tgt_ref[...]
        row_loss = jnp.sum(err * err, axis=1, keepdims=True)
        loss_ref[...] += jnp.broadcast_to(jnp.sum(row_loss, axis=0, keepdims=True) * (0.5 / D_MODEL), (1, 128))
        dy = err * (1.0 / D_MODEL)
        dlng_ref[...] += jnp.sum(dy * zhat, axis=0, keepdims=True)
        dlnb_ref[...] += jnp.sum(dy, axis=0, keepdims=True)
        dzh = dy * lng
        dz = rstd * (dzh - jnp.mean(dzh, axis=-1, keepdims=True)
                     - zhat * jnp.mean(dzh * zhat, axis=-1, keepdims=True))
        dz_ref[...] = dz
        dzb = dz.astype(BF16)
        dwout_ref[...] += _dot_tn(cat, dzb)
        dcat = _dot_nt(dzb, wout_ref[...])
        dya = dcat[:, :MLA_W]
        dyb = dcat[:, MLA_W:]

        do = dya * silu_a
        do_ref[...] = do.astype(BF16)
        prod = do * o_v
        for hh in range(HEADS):
            lanes = slice(hh * 128, (hh + 1) * 128)
            delta_ref[:, lanes] = jnp.broadcast_to(jnp.sum(prod[:, lanes], axis=1, keepdims=True), (tm, 128))
        dga_ref[...] = (dya * o_v * (sig_a * (1.0 + ga_v * (1.0 - sig_a)))).astype(BF16)
        dgb_ref[...] = (dyb * ybp * (sig_b * (1.0 + gb_v * (1.0 - sig_b)))).astype(BF16)
        dybp = dyb * silu_b
        dps_ref[...] += jnp.sum(dybp * mixed, axis=0, keepdims=True)
        dmixed = (dybp * ps).astype(BF16)
        for g in range(POOL_G):
            lanes = slice(g * POOL_GD, (g + 1) * POOL_GD)
            dpw_ref[g] += _dot_tn(pooled[g], dmixed[:, lanes])
            dpc_ref[:, lanes] = (_dot_nt(dmixed[:, lanes], pw_ref[g]) / cnts[g]).astype(BF16)

    row = lambda w: pl.BlockSpec((tm, w), lambda i: (i, 0))
    f = lambda w, dt: jax.ShapeDtypeStruct((T, w), dt)
    halo_spec = pl.BlockSpec((HALO, POOL_W), lambda i: (jnp.maximum(i * hb - 1, 0), 0))
    return pl.pallas_call(
        body, name="mid", grid=(T // tm,),
        in_specs=[row(D_MODEL), row(D_MODEL), row(MLA_W), row(MLA_W), row(POOL_W), halo_spec, row(POOL_W),
                  _full(w_out.shape), _full(pool_w.shape),
                  _full(pool_scale.shape), _full(ln_g.shape), _full(ln_b.shape)],
        out_specs=(row(D_MODEL), row(MLA_W), row(MLA_W), row(MLA_W), row(POOL_W), row(POOL_W),
                   _full((D_MODEL, D_MODEL)), _full(pool_w.shape), _full((1, POOL_W)),
                   _full((1, D_MODEL)), _full((1, D_MODEL)), _full((1, 128))),
        out_shape=(f(D_MODEL, F32), f(MLA_W, BF16), f(MLA_W, F32), f(MLA_W, BF16), f(POOL_W, BF16), f(POOL_W, BF16),
                   jax.ShapeDtypeStruct((D_MODEL, D_MODEL), F32), jax.ShapeDtypeStruct(pool_w.shape, F32),
                   jax.ShapeDtypeStruct((1, POOL_W), F32), jax.ShapeDtypeStruct((1, D_MODEL), F32),
                   jax.ShapeDtypeStruct((1, D_MODEL), F32), jax.ShapeDtypeStruct((1, 128), F32)),
        compiler_params=_cparams(1),
    )(x, tgt, o, ga, u, u, gb, w_out, pool_w, pool_scale, ln_g, ln_b)


def _attn_bwd(q, k, v, do, lse, delta, pos_col, pos_row, bounds, nb, S, tq, tk):
    T = q.shape[0]
    nq, nk = S // tq, S // tk
    reps = tk // 128
    hg = HEAD_GROUP

    def body(qmin_ref, qmax_ref, kmin_ref, kmax_ref, q_ref, k_ref, v_ref, do_ref, lse_ref, dl_ref, pc_ref, pr_ref,
             dq_out, dk_out, dv_out, dq_ref, dk_ref, dv_ref):
        b, j = pl.program_id(0), pl.program_id(2)

        @pl.when(j == 0)
        def _():
            dq_ref[...] = jnp.zeros_like(dq_ref)

        dk_ref[...] = jnp.zeros_like(dk_ref)
        dv_ref[...] = jnp.zeros_like(dv_ref)
        k_lo = kmin_ref[b * nk + j]
        k_hi = kmax_ref[b * nk + j]

        def tile(i, masked):
            rows = pl.ds(pl.multiple_of(i * tq, tq), tq)
            if masked:
                keep = pc_ref[rows, :] >= pr_ref[...]
            stage = []
            for g in range(hg):
                qk = slice(g * HEAD_PAD, (g + 1) * HEAD_PAD)
                hv = slice(g * 128, (g + 1) * 128)
                s = _dot_nt(q_ref[rows, qk], k_ref[:, qk])
                if masked:
                    s = jnp.where(keep, s, NEG)
                stage.append((s, _dot_nt(do_ref[rows, hv], v_ref[:, hv])))
            grads = []
            for g in range(hg):
                hv = slice(g * 128, (g + 1) * 128)
                s, dp = stage[g]
                p = jnp.exp2(s - jnp.concatenate([lse_ref[rows, hv]] * reps, axis=1))
                ds = (p * (dp - jnp.concatenate([dl_ref[rows, hv]] * reps, axis=1))).astype(BF16)
                grads.append((p.astype(BF16), ds))
            for g in range(hg):
                qk = slice(g * HEAD_PAD, (g + 1) * HEAD_PAD)
                hv = slice(g * 128, (g + 1) * 128)
                p, ds = grads[g]
                dv_ref[:, hv] += _dot_tn(p, do_ref[rows, hv])
                dq_ref[rows, qk] += _dot(ds, k_ref[:, qk])
                dk_ref[:, qk] += _dot_tn(ds, q_ref[rows, qk])

        def step(i, carry):
            visible = k_lo <= qmax_ref[b * nq + i]
            clear = qmin_ref[b * nq + i] >= k_hi

            @pl.when(jnp.logical_and(visible, clear))
            def _():
                tile(i, False)

            @pl.when(jnp.logical_and(visible, jnp.logical_not(clear)))
            def _():
                tile(i, True)
            return carry

        lax.fori_loop(0, nq, step, 0)
        dk_out[...] = dk_ref[...].astype(BF16)
        dv_out[...] = dv_ref[...].astype(BF16)

        @pl.when(j == nk - 1)
        def _():
            dq_out[...] = dq_ref[...].astype(BF16)

    ng = HEADS // hg
    seq = lambda w: pl.BlockSpec((S, w), lambda b, h, j, *_: (b, h))
    blk = lambda w: pl.BlockSpec((tk, w), lambda b, h, j, *_: (b * nk + j, h))
    return pl.pallas_call(
        body, name="attn_bwd",
        grid_spec=pltpu.PrefetchScalarGridSpec(
            num_scalar_prefetch=4, grid=(nb, ng, nk),
            in_specs=[seq(hg * HEAD_PAD), blk(hg * HEAD_PAD), blk(hg * 128),
                      seq(hg * 128), seq(hg * 128), seq(hg * 128),
                      pl.BlockSpec((S, 1), lambda b, h, j, *_: (b, 0)),
                      pl.BlockSpec((None, 1, tk), lambda b, h, j, *_: (b, 0, j))],
            out_specs=(seq(hg * HEAD_PAD), blk(hg * HEAD_PAD), blk(hg * 128)),
            scratch_shapes=[pltpu.VMEM((S, hg * HEAD_PAD), F32), pltpu.VMEM((tk, hg * HEAD_PAD), F32),
                            pltpu.VMEM((tk, hg * 128), F32)]),
        out_shape=(jax.ShapeDtypeStruct((T, HEADS * HEAD_PAD), BF16),
                   jax.ShapeDtypeStruct((T, HEADS * HEAD_PAD), BF16),
                   jax.ShapeDtypeStruct((T, MLA_W), BF16)),
        compiler_params=_cparams(3),
    )(*bounds, q, k, v, do, lse, delta, pos_col, pos_row)


def _bwd_proj(dq, dk, dv, xq, xkv, x, dz, dga, dgb, dpc, rope_tab, w_uq_t, w_ukv, w_in_t, gq, gkv, S, tm):
    T = x.shape[0]
    tps = S // tm
    hb = tm // HALO
    n_tiles = T // tm

    def body(dq_ref, dk_ref, dv_ref, xq_ref, xkv_ref, x_ref, dz_ref, dga_ref, dgb_ref, dpc_ref, dph_ref,
             tab_ref, wuq_ref, wukv_ref, win_ref, gq_ref, gkv_ref,
             dx_ref, dwin_hbm, dwuq_hbm, dwukv_hbm, dgq_ref, dgkv_ref,
             acc_win, acc_wuq, acc_wukv, dh_sc):
        i = pl.program_id(0)

        @pl.when(i == 0)
        def _():
            acc_win[...] = jnp.zeros_like(acc_win)
            acc_wuq[...] = jnp.zeros_like(acc_wuq)
            acc_wukv[...] = jnp.zeros_like(acc_wukv)
            dgq_ref[...] = jnp.zeros_like(dgq_ref)
            dgkv_ref[...] = jnp.zeros_like(dgkv_ref)
            dh_sc[...] = jnp.zeros_like(dh_sc)

        dh_prev = dh_sc[...]
        dx_ref[...] = ALPHA * dz_ref[...] + _dot(dh_prev, win_ref[...])
        acc_win[...] += _dot_tn(dh_prev, x_ref[...].astype(BF16))

        live = jnp.where(i < n_tiles, 1.0, 0.0)
        c, sa, sb = _expand_rope_table(tab_ref[...])
        dq_v = dq_ref[...].astype(F32) * (SCALE * live)
        dk_v = dk_ref[...].astype(F32) * (LN2 * live)
        dv_v = dv_ref[...].astype(F32) * live
        dq_parts, dkv_parts = [], []
        dkr = jnp.zeros((tm, 128), F32)
        for hh in range(HEADS):
            b0 = hh * HEAD_PAD
            dq_parts.append(dq_v[:, b0:b0 + 128].astype(BF16))
            dq_parts.append(_rope(dq_v[:, b0 + 128:b0 + 256], c, sa, sb, -1.0).astype(BF16))
            dkv_parts.append(dk_v[:, b0:b0 + 128].astype(BF16))
            dkv_parts.append(dv_v[:, hh * 128:(hh + 1) * 128].astype(BF16))
            dkr = dkr + dk_v[:, b0 + 128:b0 + 256]
        dqp = jnp.concatenate(dq_parts, axis=1)
        dkvp = jnp.concatenate(dkv_parts, axis=1)
        dkrr = _rope(dkr, c, sa, sb, -1.0)

        def rms_bwd(xv, g, dyn, dg_ref):
            r = lax.rsqrt(jnp.mean(xv * xv, axis=-1, keepdims=True) + RMS_EPS)
            xhat = xv * r
            dg_ref[...] += jnp.sum(dyn * xhat, axis=0, keepdims=True)
            dxh = dyn * g
            return r * (dxh - xhat * jnp.mean(dxh * xhat, axis=-1, keepdims=True))

        xq_v = xq_ref[...].astype(F32)
        gq_v = gq_ref[...]
        rq = lax.rsqrt(jnp.mean(xq_v * xq_v, axis=-1, keepdims=True) + RMS_EPS)
        acc_wuq[...] += _dot_tn(dqp, ((xq_v * rq) * gq_v).astype(BF16))
        dxq = rms_bwd(xq_v, gq_v, _dot(dqp, wuq_ref[...]), dgq_ref)

        xkv_v = xkv_ref[...].astype(F32)
        gkv_v = gkv_ref[...]
        rkv = lax.rsqrt(jnp.mean(xkv_v * xkv_v, axis=-1, keepdims=True) + RMS_EPS)
        acc_wukv[...] += _dot_tn(((xkv_v * rkv) * gkv_v).astype(BF16), dkvp)
        dxkv = rms_bwd(xkv_v, gkv_v, _dot_nt(dkvp, wukv_ref[...]), dgkv_ref)

        seq_tile = i % tps
        tpos = seq_tile * tm + lax.broadcasted_iota(jnp.int32, (tm, 1), 0)
        dpc_v = dpc_ref[...].astype(F32)
        halo = jnp.where(seq_tile == tps - 1, 0.0, dph_ref[...].astype(F32))
        n = tm + HALO
        du = []
        for g in range(POOL_G):
            lanes = slice(g * POOL_GD, (g + 1) * POOL_GD)
            f = jnp.concatenate([dpc_v[:, lanes], halo[:, lanes]], axis=0)
            for st in range(g + 1):
                f = f + pltpu.roll(f, n - (1 << st), 0)
            cnt = jnp.minimum(tpos + 1, 2 << g).astype(F32)
            du.append((f[:tm, :] - dpc_v[:, lanes] * cnt).astype(BF16))

        dh_sc[...] = jnp.concatenate([dxq.astype(BF16), dxkv.astype(BF16), dkrr.astype(BF16), dga_ref[...]]
                                     + du + [dgb_ref[...]], axis=1)

        @pl.when(i == n_tiles)
        def _():
            pltpu.sync_copy(acc_win.at[pl.ds(0, ROPE_END)], dwin_hbm.at[pl.ds(0, ROPE_END)])
            pltpu.sync_copy(acc_win.at[pl.ds(COL_GA, IN_EXT - COL_GA)], dwin_hbm.at[pl.ds(ROPE_END, IN_W - ROPE_END)])
            for hh in range(HEADS):
                pltpu.sync_copy(acc_wuq.at[pl.ds(hh * HEAD_PAD, NOPE + ROPE)], dwuq_hbm.at[hh])
            pltpu.sync_copy(acc_wukv, dwukv_hbm)

    cur = lambda w: pl.BlockSpec((tm, w), lambda i: (jnp.minimum(i, n_tiles - 1), 0))
    prev = lambda w: pl.BlockSpec((tm, w), lambda i: (jnp.maximum(i - 1, 0), 0))
    halo_spec = pl.BlockSpec((HALO, POOL_W), lambda i: (jnp.minimum((i + 1) * hb, T // HALO - 1), 0))
    return pl.pallas_call(
        body, name="bwd_proj", grid=(n_tiles + 1,),
        in_specs=[cur(1024), cur(1024), cur(512), cur(512), cur(256), prev(D_MODEL), prev(D_MODEL),
                  cur(512), cur(512), cur(512), halo_spec, cur(128),
                  _full(w_uq_t.shape), _full(w_ukv.shape), _full(w_in_t.shape), _full(gq.shape), _full(gkv.shape)],
        out_specs=(prev(D_MODEL), ANY, ANY, ANY, _full((1, Q_LORA)), _full((1, KV_LORA))),
        out_shape=(jax.ShapeDtypeStruct((T, D_MODEL), F32),
                   jax.ShapeDtypeStruct((IN_W, D_MODEL), F32),
                   jax.ShapeDtypeStruct((HEADS, NOPE + ROPE, Q_LORA), F32),
                   jax.ShapeDtypeStruct((KV_LORA, 1024), F32),
                   jax.ShapeDtypeStruct((1, Q_LORA), F32), jax.ShapeDtypeStruct((1, KV_LORA), F32)),
        scratch_shapes=[pltpu.VMEM((IN_EXT, D_MODEL), F32), pltpu.VMEM((HEADS * HEAD_PAD, Q_LORA), F32),
                        pltpu.VMEM((KV_LORA, 1024), F32), pltpu.VMEM((tm, IN_EXT), BF16)],
        compiler_params=_cparams(1),
    )(dq, dk, dv, xq, xkv, x, dz, dga, dgb, dpc, dpc, rope_tab, w_uq_t, w_ukv, w_in_t, gq, gkv)


def kernel(x, positions, w_in, q_norm_g, w_uq, kv_norm_g, w_ukv, pool_w, pool_scale, w_out, ln_g, ln_b, loss_target, m_w_in, m_q_norm_g, m_w_uq, m_kv_norm_g, m_w_ukv, m_pool_w, m_pool_scale, m_w_out, m_ln_g, m_ln_b, v_w_in, v_q_norm_g, v_w_uq, v_kv_norm_g, v_w_ukv, v_pool_w, v_pool_scale, v_w_out, v_ln_g, v_ln_b):
    nb, S, _ = x.shape
    T = nb * S
    tm = min(256, S)
    tq = min(512, S)
    tk = min(512, S)
    assert S % tm == 0 and tm % HALO == 0 and S % tq == 0 and S % tk == 0

    cx, cy, cc = lax.axis_index("x"), lax.axis_index("y"), lax.axis_index("c")
    me = 2 * cx + cy

    def own_slot(w, slot_rows):
        blk = jnp.pad(w.astype(BF16), ((0, slot_rows - w.shape[0]), (0, 0)))
        return lax.dynamic_update_slice(jnp.zeros((N_CHIPS,) + blk.shape, BF16), blk[None], (me, 0, 0))

    w_in_g, w_uq_g, w_ukv_g = _weight_gather(
        [own_slot(w_in.T, IN_SHARD), own_slot(w_uq.T, HEAD_PAD), own_slot(w_ukv, KV_LORA)], (IN_SHARD, NOPE + ROPE, KV_LORA))
    w_in_f = w_in_g.reshape(IN_W, D_MODEL)
    w_in_t = jnp.concatenate([w_in_f[:ROPE_END], jnp.zeros((COL_GA - ROPE_END, D_MODEL), BF16), w_in_f[ROPE_END:]], axis=0)
    w_uq_t = w_uq_g.reshape(HEADS * HEAD_PAD, Q_LORA)
    w_ukv_f = w_ukv_g.transpose(1, 0, 2).reshape(KV_LORA, 1024)
    pool_w_b = pool_w.astype(BF16)
    gq2 = q_norm_g.reshape(1, Q_LORA)
    gkv2 = kv_norm_g.reshape(1, KV_LORA)
    ps2 = pool_scale.reshape(1, POOL_W)

    half = ROPE // 2
    inv_freq = ROPE_THETA ** (-jnp.arange(half, dtype=F32) / half)
    freq_row = jnp.concatenate([inv_freq, inv_freq, jnp.zeros((2 * half,), F32)]).reshape(1, 128)
    pos_col = positions.reshape(T, 1)
    pos_row = positions.reshape(nb, 1, S)
    pos_q = positions.reshape(nb, S // tq, tq)
    pos_k = positions.reshape(nb, S // tk, tk)
    bounds = (jnp.min(pos_q, axis=2).reshape(-1), jnp.max(pos_q, axis=2).reshape(-1),
              jnp.min(pos_k, axis=2).reshape(-1), jnp.max(pos_k, axis=2).reshape(-1))

    xf = x.reshape(T, D_MODEL)
    tgt = loss_target.reshape(T, D_MODEL)

    xq, xkv, ga, u, gb, q, k, v, rope_tab, w_out_g = _fwd_proj(
        xf, w_in_t, w_uq_t, w_ukv_f, gq2, gkv2, pos_col, freq_row, own_slot(w_out, 256), tm)
    w_out_f = w_out_g.reshape(D_MODEL, D_MODEL)
    o, lse = _attn_fwd(q, k, v, pos_col, pos_row, bounds, nb, S, tq, tk)

    (dz, do, delta, dga, dgb, dpc, d_w_out, d_pool_w, d_pool_scale, d_ln_g, d_ln_b, loss_part) = _mid(
        xf, tgt, o, ga, u, gb, w_out_f, pool_w_b, ps2, ln_g, ln_b, S, tm)

    dq, dk, dv = _attn_bwd(q, k, v, do, lse, delta, pos_col, pos_row, bounds, nb, S, tq, tk)
    dx, d_w_in_t, d_w_uq_t, d_w_ukv, d_gq, d_gkv = _bwd_proj(
        dq, dk, dv, xq, xkv, xf, dz, dga, dgb, dpc, rope_tab, w_uq_t, w_ukv_f, w_in_t, gq2, gkv2, S, tm)
    grad_x = dx.reshape(nb, S, D_MODEL)

    g_in = d_w_in_t.reshape(N_CHIPS, IN_SHARD, D_MODEL)
    g_uq = d_w_uq_t
    g_ukv = d_w_ukv.reshape(KV_LORA, N_CHIPS, 256).transpose(1, 0, 2)
    g_out = d_w_out.reshape(N_CHIPS, 256, D_MODEL)
    wide = lambda a: jnp.pad(a.reshape(1, -1), ((0, 0), (0, D_MODEL - a.size)))
    vec = jnp.concatenate([d_ln_g, d_ln_b, wide(d_pool_scale), wide(d_gq), jnp.zeros((4, D_MODEL), F32),
                           wide(d_gkv), wide(loss_part), jnp.zeros((VEC_ROWS - 10, D_MODEL), F32)], axis=0)
    to_all = lambda a: jnp.broadcast_to(a[None], (N_CHIPS,) + a.shape)
    gs = [g_in, g_uq, g_ukv, g_out, to_all(d_pool_w.reshape(-1, D_MODEL)), to_all(vec)]
    g_all = _grad_reduce(gs, (BF16,) * N_BIG + (F32, F32))
    g_big = g_all[:N_BIG]
    pw_sum = g_all[N_BIG].reshape(POOL_G * POOL_GD, POOL_GD)
    vec_sum = g_all[N_BIG + 1]

    big = _adamw_big(g_big, [w_in.T, w_uq.T, w_ukv, w_out], [m_w_in.T, m_w_uq.T, m_w_ukv, m_w_out],
                     [v_w_in.T, v_w_uq.T, v_w_ukv, v_w_out])
    two_d = lambda a: a.reshape(-1, a.shape[-1])
    small_names = lambda pw, lg, lb, ps, gq, gkv: [two_d(pw), lg, lb, ps.reshape(1, -1), gq.reshape(1, -1), gkv.reshape(1, -1)]
    small, loss_row = _adamw_small(
        pw_sum, vec_sum,
        small_names(pool_w, ln_g, ln_b, pool_scale, q_norm_g, kv_norm_g),
        small_names(m_pool_w, m_ln_g, m_ln_b, m_pool_scale, m_q_norm_g, m_kv_norm_g),
        small_names(v_pool_w, v_ln_g, v_ln_b, v_pool_scale, v_q_norm_g, v_kv_norm_g))
    loss = loss_row[0, 0]

    def leaves(kind):
        b = [g_big[t] if kind == 0 else big[t][kind - 1] for t in range(N_BIG)]
        b = [b[0].T, b[1].T, b[2], b[3]]
        s = [small[t][kind] for t in range(6)]
        return (b[0], s[4].reshape(Q_LORA), b[1], s[5].reshape(KV_LORA), b[2],
                s[0].reshape(POOL_G, POOL_GD, POOL_GD), s[3].reshape(POOL_W), b[3], s[1], s[2])

    return (loss, grad_x) + leaves(0) + leaves(1) + leaves(2) + leaves(3)
```

```python
import jax
import jax.numpy as jnp
from jax import lax
from jax.experimental import pallas as pl
from jax.experimental.pallas import tpu as pltpu

F32 = jnp.float32
BF16 = jnp.bfloat16
MESH = pl.DeviceIdType.MESH

HEADS = 4
NOPE = 128
ROPE = 64
HEAD_PAD = 256
Q_LORA = 512
KV_LORA = 256
MLA_W = 512
POOL_W = 512
POOL_G = 4
POOL_GD = 128
D_MODEL = 1024
IN_W = 2368
IN_EXT = 2432
COL_KV, COL_KR, COL_GA, COL_U, COL_GB = 512, 768, 896, 1408, 1920
ROPE_END = COL_KR + 64
IN_SHARD = IN_W // 4
ROPE_THETA = 10000.0
RMS_EPS = 1e-6
LN_EPS = 1e-5
ALPHA = 2.0 ** 0.25
SCALE = 192.0 ** -0.5
LOG2E = 1.4426950408889634
LN2 = 0.6931471805599453
QSCALE = SCALE * LOG2E
NEG = float(jnp.finfo(jnp.float32).min)
HEAD_GROUP = 2
HALO = 16

ADAM_LR = 0.001
ADAM_B1 = 0.9
ADAM_B2 = 0.999
ADAM_EPS = 1e-08
ADAM_WD = 0.01
ADAM_STEP = 10

N_CHIPS = 4
N_BIG = 4
VEC_ROWS = 16

VMEM_LIMIT = 56 * 1024 * 1024


def _cparams(n_grid_dims=0, **kw):
    sem = ("arbitrary",) * n_grid_dims if n_grid_dims else None
    return pltpu.CompilerParams(dimension_semantics=sem, vmem_limit_bytes=VMEM_LIMIT, **kw)


def _full(shape):
    nd = len(shape)
    return pl.BlockSpec(shape, lambda *_: (0,) * nd)


def _dot(a, b):
    return jnp.dot(a, b, preferred_element_type=F32)


def _dot_nt(a, b):
    return lax.dot_general(a, b, (((1,), (1,)), ((), ())), preferred_element_type=F32)


def _dot_tn(a, b):
    return lax.dot_general(a, b, (((0,), (0,)), ((), ())), preferred_element_type=F32)


def _rope_table(pos_col, freq_row):
    lane = lax.broadcasted_iota(jnp.int32, (1, 128), 1)
    ang = pos_col.astype(F32) * freq_row
    return jnp.where(lane < 32, jnp.cos(ang), jnp.where(lane < 64, jnp.sin(ang), 0.0))


def _expand_rope_table(tab):
    lane = lax.broadcasted_iota(jnp.int32, (1, 128), 1)
    second = jnp.logical_and(lane >= 32, lane < 64)
    c = jnp.where(lane < 32, tab, jnp.where(second, pltpu.roll(tab, 32, 1), 0.0))
    sa = jnp.where(lane < 32, pltpu.roll(tab, 96, 1), 0.0)
    sb = jnp.where(second, tab, 0.0)
    return c, sa, sb


def _rope(g, c, sa, sb, sign):
    return g * c + sign * (pltpu.roll(g, 32, 1) * sb - pltpu.roll(g, 96, 1) * sa)


def _place():
    x, y, c = lax.axis_index("x"), lax.axis_index("y"), lax.axis_index("c")
    chips = [(1 - x, y), (x, 1 - y), (1 - x, 1 - y)]
    return x, y, c, chips


ANY = pl.BlockSpec(memory_space=pl.ANY)


ROPE_CHUNK = 2048


def _weight_gather(slots, valid_rows, pos_col, freq_row):
    n = len(slots)
    T = pos_col.shape[0]
    chunk = min(ROPE_CHUNK, T)
    assert T % chunk == 0

    def body(*refs):
        pos_hbm, freq_ref = refs[n:n + 2]
        outs = refs[n + 2:2 * n + 2]
        tab_hbm = refs[2 * n + 2]
        send_sems, recv_sems, pos_buf, tab_buf = refs[2 * n + 3:]
        x, y, c, chips = _place()
        me = 2 * x + y

        def copy(t, k, chip_idx, half, to):
            hc = slots[t].shape[2] // 2
            blk = outs[t].at[chip_idx, pl.ds(0, valid_rows[t]), pl.ds(half * hc, hc)]
            return pltpu.make_async_remote_copy(
                src_ref=blk, dst_ref=blk, send_sem=send_sems.at[6 * t + k], recv_sem=recv_sems.at[6 * t + k],
                device_id=to, device_id_type=MESH)

        first = [copy(t, j, me, c, (cx, cy, c)) for t in range(n) for j, (cx, cy) in enumerate(chips)]
        for cp in first:
            cp.start()

        def table_chunk(r, carry):
            rows = pl.ds(pl.multiple_of(r * chunk, chunk), chunk)
            pltpu.sync_copy(pos_hbm.at[rows], pos_buf)
            tab_buf[...] = _rope_table(pos_buf[...], freq_ref[...])
            pltpu.sync_copy(tab_buf, tab_hbm.at[rows])
            return carry

        lax.fori_loop(0, T // chunk, table_chunk, 0)
        passed = []
        for j, (cx, cy) in enumerate(chips):
            for t in range(n):
                copy(t, j, 2 * cx + cy, c, (x, y, c)).wait_recv()
                fwd = copy(t, 3 + j, 2 * cx + cy, c, (x, y, 1 - c))
                fwd.start()
                passed.append(fwd)
        for j, (cx, cy) in enumerate(chips):
            for t in range(n):
                copy(t, 3 + j, 2 * cx + cy, 1 - c, (x, y, c)).wait_recv()
        for cp in first + passed:
            cp.wait_send()

    outs = pl.pallas_call(
        body, name="weight_gather",
        out_shape=tuple(jax.ShapeDtypeStruct(a.shape, a.dtype) for a in slots) + (jax.ShapeDtypeStruct((T, 128), F32),),
        in_specs=[ANY] * n + [ANY, pl.BlockSpec(memory_space=pltpu.VMEM)], out_specs=(ANY,) * (n + 1),
        input_output_aliases={t: t for t in range(n)},
        scratch_shapes=[pltpu.SemaphoreType.DMA((6 * n,)), pltpu.SemaphoreType.DMA((6 * n,)),
                        pltpu.VMEM((chunk, 1), jnp.int32), pltpu.VMEM((chunk, 128), F32)],
    )(*slots, pos_col, freq_row)
    return outs[:n], outs[n]


def _grad_reduce(gs, wire_dtypes):
    n = len(gs)
    hcs = [g.shape[2] // 2 for g in gs]

    def body(*refs):
        g_refs, out_refs = refs[:n], refs[n:2 * n]
        scr = refs[2 * n:]
        own, sib, wire, got, fin = (scr[i * n:(i + 1) * n] for i in range(5))
        d2d_send, d2d_recv, ici_send, ici_recv, fin_send, fin_recv, loc_in, loc_out = scr[5 * n:]
        x, y, c, chips = _place()
        sibling = (x, y, 1 - c)
        dests = [2 * cx + cy for cx, cy in chips] + [2 * x + y]

        def mine(t):
            return pl.ds(c * hcs[t], hcs[t])

        def theirs(t):
            return pl.ds((1 - c) * hcs[t], hcs[t])

        def rows(t):
            return pl.ds(0, gs[t].shape[1])

        def remote(src, dst, send, recv, to):
            return pltpu.make_async_remote_copy(src_ref=src, dst_ref=dst, send_sem=send, recv_sem=recv,
                                                device_id=to, device_id_type=MESH)

        loads, d2d = [], []
        for j, k in enumerate(dests):
            for t in range(n):
                ld = pltpu.make_async_copy(g_refs[t].at[k, rows(t), mine(t)], own[t].at[j], loc_in.at[4 * t + j])
                ld.start()
                loads.append(ld)
                cp = remote(g_refs[t].at[k, rows(t), theirs(t)], sib[t].at[j],
                            d2d_send.at[4 * t + j], d2d_recv.at[4 * t + j], sibling)
                cp.start()
                d2d.append(cp)

        ici = []
        for j, (cx, cy) in enumerate(chips):
            for t in range(n):
                loads[j * n + t].wait()
                d2d[j * n + t].wait_recv()
                wire[t][j] = (own[t][j] + sib[t][j]).astype(wire_dtypes[t])
                cp = remote(wire[t].at[j], got[t].at[j], ici_send.at[3 * t + j], ici_recv.at[3 * t + j], (cx, cy, c))
                cp.start()
                ici.append(cp)

        last = []
        for t in range(n):
            loads[3 * n + t].wait()
            d2d[3 * n + t].wait_recv()
            for j in range(3):
                ici[j * n + t].wait_recv()
            fin[t][...] = (((own[t][3] + sib[t][3]) + got[t][0].astype(F32))
                           + (got[t][1].astype(F32) + got[t][2].astype(F32)))
            st = pltpu.make_async_copy(fin[t], out_refs[t].at[rows(t), mine(t)], loc_out.at[t])
            st.start()
            cp = remote(fin[t], out_refs[t].at[rows(t), mine(t)], fin_send.at[t], fin_recv.at[t], sibling)
            cp.start()
            last.append((st, cp))
        for t in range(n):
            remote(fin[t], out_refs[t].at[rows(t), theirs(t)], fin_send.at[t], fin_recv.at[t], sibling).wait_recv()
        for cp in d2d + ici:
            cp.wait_send()
        for st, cp in last:
            cp.wait_send()
            st.wait()

    scratch = ([pltpu.VMEM((4, g.shape[1], hc), F32) for g, hc in zip(gs, hcs)]
               + [pltpu.VMEM((4, g.shape[1], hc), F32) for g, hc in zip(gs, hcs)]
               + [pltpu.VMEM((3, g.shape[1], hc), w) for g, hc, w in zip(gs, hcs, wire_dtypes)]
               + [pltpu.VMEM((3, g.shape[1], hc), w) for g, hc, w in zip(gs, hcs, wire_dtypes)]
               + [pltpu.VMEM((g.shape[1], hc), F32) for g, hc in zip(gs, hcs)]
               + [pltpu.SemaphoreType.DMA((4 * n,)), pltpu.SemaphoreType.DMA((4 * n,)),
                  pltpu.SemaphoreType.DMA((3 * n,)), pltpu.SemaphoreType.DMA((3 * n,)),
                  pltpu.SemaphoreType.DMA((n,)), pltpu.SemaphoreType.DMA((n,)),
                  pltpu.SemaphoreType.DMA((4 * n,)), pltpu.SemaphoreType.DMA((n,))])
    return pl.pallas_call(
        body, name="grad_reduce",
        out_shape=tuple(jax.ShapeDtypeStruct(g.shape[1:], F32) for g in gs),
        in_specs=[ANY] * n, out_specs=(ANY,) * n, scratch_shapes=scratch,
        compiler_params=_cparams(),
    )(*gs)


def _adamw_math(g, w, m, v):
    nm = ADAM_B1 * m + (1.0 - ADAM_B1) * g
    nv = ADAM_B2 * v + (1.0 - ADAM_B2) * (g * g)
    m_hat = nm / (1.0 - ADAM_B1 ** ADAM_STEP)
    v_hat = nv / (1.0 - ADAM_B2 ** ADAM_STEP)
    return -ADAM_LR * (m_hat / (jnp.sqrt(v_hat) + ADAM_EPS) + ADAM_WD * w), nm, nv


ADAM_STEPS = 8


def _adamw_big(gs, ws, ms, vs):
    n = len(gs)

    def body(*refs):
        for t in range(n):
            d, nm, nv = _adamw_math(refs[t][...], refs[n + t][...], refs[2 * n + t][...], refs[3 * n + t][...])
            refs[4 * n + 3 * t][...] = d
            refs[4 * n + 3 * t + 1][...] = nm
            refs[4 * n + 3 * t + 2][...] = nv

    def tile_spec(shape):
        rows, cols = shape
        if rows % (8 * ADAM_STEPS) == 0:
            return pl.BlockSpec((rows // ADAM_STEPS, cols), lambda i: (i, 0))
        return pl.BlockSpec((rows, cols // ADAM_STEPS), lambda i: (0, i))

    specs = [tile_spec(g.shape) for g in gs]
    out_specs, out_shape = [], []
    for t in range(n):
        out_specs += [specs[t]] * 3
        out_shape += [jax.ShapeDtypeStruct(gs[t].shape, F32)] * 3
    outs = pl.pallas_call(
        body, name="adamw_big", grid=(ADAM_STEPS,),
        in_specs=specs * 4, out_specs=tuple(out_specs), out_shape=tuple(out_shape),
        compiler_params=_cparams(1),
    )(*gs, *ws, *ms, *vs)
    return [outs[3 * t: 3 * t + 3] for t in range(n)]


def _adamw_small(pw_sum, vec_sum, ws, ms, vs):
    rows = (None, 0, 1, 2, 3, 8)
    n = len(ws)

    def body(pw_ref, vec_ref, *refs):
        outs = refs[3 * n:]
        for t in range(n):
            w_ref, m_ref, v_ref = refs[t], refs[n + t], refs[2 * n + t]
            if rows[t] is None:
                g = pw_ref[...]
            else:
                g = vec_ref[rows[t]:rows[t] + 1, 0:w_ref.shape[1]]
            d, nm, nv = _adamw_math(g, w_ref[...], m_ref[...], v_ref[...])
            outs[4 * t][...] = g
            outs[4 * t + 1][...] = d
            outs[4 * t + 2][...] = nm
            outs[4 * t + 3][...] = nv
        outs[4 * n][...] = vec_ref[9:10, 0:128]

    vm = pl.BlockSpec(memory_space=pltpu.VMEM)
    out_shape = []
    for w in ws:
        out_shape += [jax.ShapeDtypeStruct(w.shape, F32)] * 4
    out_shape.append(jax.ShapeDtypeStruct((1, 128), F32))
    outs = pl.pallas_call(
        body, name="adamw_small", in_specs=[vm] * (2 + 3 * n), out_specs=(vm,) * (4 * n + 1),
        out_shape=tuple(out_shape),
    )(pw_sum, vec_sum, *ws, *ms, *vs)
    return [outs[4 * t: 4 * t + 4] for t in range(n)], outs[4 * n]


def _fwd_proj(x, w_in_t, w_uq_t, w_ukv, gq, gkv, rope_tab, w_out_slots, tm):
    T = x.shape[0]
    n_steps = T // tm
    fwd_step = n_steps // 2

    def body(x_ref, win_ref, wuq_ref, wukv_ref, gq_ref, gkv_ref, tab_ref, wo_in,
             xq_ref, xkv_ref, ga_ref, u_ref, gb_ref, q_ref, k_ref, v_ref, wo_ref, send_sems, recv_sems):
        i = pl.program_id(0)
        px, py, pc, chips = _place()
        hc = D_MODEL // 2

        def wo_copy(k, chip_idx, half, to):
            blk = wo_ref.at[chip_idx, pl.ds(0, 256), pl.ds(half * hc, hc)]
            return pltpu.make_async_remote_copy(src_ref=blk, dst_ref=blk, send_sem=send_sems.at[k],
                                                recv_sem=recv_sems.at[k], device_id=to, device_id_type=MESH)

        @pl.when(i == 0)
        def _():
            for j, (cx, cy) in enumerate(chips):
                wo_copy(j, 2 * px + py, pc, (cx, cy, pc)).start()

        @pl.when(i == fwd_step)
        def _():
            for j, (cx, cy) in enumerate(chips):
                wo_copy(j, 2 * cx + cy, pc, (px, py, pc)).wait_recv()
                wo_copy(3 + j, 2 * cx + cy, pc, (px, py, 1 - pc)).start()

        @pl.when(i == n_steps - 1)
        def _():
            for j, (cx, cy) in enumerate(chips):
                wo_copy(3 + j, 2 * cx + cy, 1 - pc, (px, py, pc)).wait_recv()
            for j, (cx, cy) in enumerate(chips):
                wo_copy(j, 2 * px + py, pc, (cx, cy, pc)).wait_send()
                wo_copy(3 + j, 2 * cx + cy, pc, (px, py, 1 - pc)).wait_send()

        h = _dot_nt(x_ref[...].astype(BF16), win_ref[...])
        xq = h[:, 0:COL_KV]
        xkv = h[:, COL_KV:COL_KR]
        xq_ref[...] = xq.astype(BF16)
        xkv_ref[...] = xkv.astype(BF16)
        ga_ref[...] = h[:, COL_GA:COL_U].astype(BF16)
        u_ref[...] = h[:, COL_U:COL_GB].astype(BF16)
        gb_ref[...] = h[:, COL_GB:IN_EXT].astype(BF16)
        c, sa, sb = _expand_rope_table(tab_ref[...])
        rq = lax.rsqrt(jnp.mean(xq * xq, axis=-1, keepdims=True) + RMS_EPS)
        q = _dot_nt(((xq * rq) * gq_ref[...]).astype(BF16), wuq_ref[...]) * QSCALE
        rkv = lax.rsqrt(jnp.mean(xkv * xkv, axis=-1, keepdims=True) + RMS_EPS)
        kv = _dot(((xkv * rkv) * gkv_ref[...]).astype(BF16), wukv_ref[...])
        kr = _rope(h[:, COL_KR:COL_GA], c, sa, sb, 1.0).astype(BF16)
        for hh in range(HEADS):
            b0 = hh * HEAD_PAD
            q_ref[:, b0:b0 + 128] = q[:, b0:b0 + 128].astype(BF16)
            q_ref[:, b0 + 128:b0 + 256] = _rope(q[:, b0 + 128:b0 + 256], c, sa, sb, 1.0).astype(BF16)
            k_ref[:, b0:b0 + 128] = kv[:, b0:b0 + 128].astype(BF16)
            k_ref[:, b0 + 128:b0 + 256] = kr
            v_ref[:, hh * 128:(hh + 1) * 128] = kv[:, b0 + 128:b0 + 256].astype(BF16)

    row = lambda w: pl.BlockSpec((tm, w), lambda i: (i, 0))
    f = lambda w, dt: jax.ShapeDtypeStruct((T, w), dt)
    return pl.pallas_call(
        body, name="fwd_proj", grid=(n_steps,),
        in_specs=[row(D_MODEL), _full(w_in_t.shape), _full(w_uq_t.shape), _full(w_ukv.shape),
                  _full(gq.shape), _full(gkv.shape), row(128), ANY],
        out_specs=(row(512), row(256), row(512), row(512), row(512), row(1024), row(1024), row(512), ANY),
        out_shape=(f(512, BF16), f(256, BF16), f(512, BF16), f(512, BF16), f(512, BF16),
                   f(1024, BF16), f(1024, BF16), f(512, BF16),
                   jax.ShapeDtypeStruct(w_out_slots.shape, BF16)),
        input_output_aliases={7: 8},
        scratch_shapes=[pltpu.SemaphoreType.DMA((6,)), pltpu.SemaphoreType.DMA((6,))],
        compiler_params=_cparams(1),
    )(x, w_in_t, w_uq_t, w_ukv, gq, gkv, rope_tab, w_out_slots)


def _attn_fwd(q, k, v, pos_col, pos_row, bounds, nb, S, tq, tk):
    T = q.shape[0]
    nq, nk = S // tq, S // tk
    reps = tk // 128
    hg = HEAD_GROUP

    def body(qmin_ref, qmax_ref, kmin_ref, kmax_ref, q_ref, k_ref, v_ref, pc_ref, pr_ref, o_ref, lse_ref,
             m_sc, l_sc, acc_sc):
        b, i = pl.program_id(0), pl.program_id(2)
        m_sc[...] = jnp.full(m_sc.shape, NEG, F32)
        l_sc[...] = jnp.zeros_like(l_sc)
        acc_sc[...] = jnp.zeros_like(acc_sc)
        q_lo = qmin_ref[b * nq + i]
        q_hi = qmax_ref[b * nq + i]

        def tile(j, masked):
            off = pl.multiple_of(j * tk, tk)
            if masked:
                keep = pc_ref[...] >= pr_ref[pl.ds(j, 1), :]
            logits = []
            for g in range(hg):
                qk = slice(g * HEAD_PAD, (g + 1) * HEAD_PAD)
                s = _dot_nt(q_ref[:, qk], k_ref[pl.ds(off, tk), qk])
                if masked:
                    s = jnp.where(keep, s, NEG)
                logits.append(s)
            probs = []
            for g in range(hg):
                hv = slice(g * 128, (g + 1) * 128)
                s = logits[g]
                m_prev = m_sc[:, hv]
                m_new = jnp.maximum(m_prev, jnp.max(s, axis=1, keepdims=True))
                p = jnp.exp2(s - jnp.concatenate([m_new] * reps, axis=1))
                a = jnp.exp2(m_prev - m_new)
                l_sc[:, hv] = a * l_sc[:, hv] + jnp.sum(p, axis=1, keepdims=True)
                m_sc[:, hv] = m_new
                probs.append((p.astype(BF16), a))
            for g in range(hg):
                hv = slice(g * 128, (g + 1) * 128)
                p, a = probs[g]
                acc_sc[:, hv] = a * acc_sc[:, hv] + _dot(p, v_ref[pl.ds(off, tk), hv])

        def step(j, carry):
            visible = kmin_ref[b * nk + j] <= q_hi
            clear = q_lo >= kmax_ref[b * nk + j]

            @pl.when(jnp.logical_and(visible, clear))
            def _():
                tile(j, False)

            @pl.when(jnp.logical_and(visible, jnp.logical_not(clear)))
            def _():
                tile(j, True)
            return carry

        lax.fori_loop(0, nk, step, 0)
        l = l_sc[...]
        o_ref[...] = acc_sc[...] / l
        lse_ref[...] = m_sc[...] + jnp.log2(l)

    ng = HEADS // hg
    stat = pltpu.VMEM((tq, hg * 128), F32)
    return pl.pallas_call(
        body, name="attn_fwd",
        grid_spec=pltpu.PrefetchScalarGridSpec(
            num_scalar_prefetch=4, grid=(nb, ng, nq),
            in_specs=[pl.BlockSpec((tq, hg * HEAD_PAD), lambda b, h, i, *_: (b * nq + i, h)),
                      pl.BlockSpec((S, hg * HEAD_PAD), lambda b, h, i, *_: (b, h)),
                      pl.BlockSpec((S, hg * 128), lambda b, h, i, *_: (b, h)),
                      pl.BlockSpec((tq, 1), lambda b, h, i, *_: (b * nq + i, 0)),
                      pl.BlockSpec((None, nk, tk), lambda b, h, i, *_: (b, 0, 0))],
            out_specs=(pl.BlockSpec((tq, hg * 128), lambda b, h, i, *_: (b * nq + i, h)),
                       pl.BlockSpec((tq, hg * 128), lambda b, h, i, *_: (b * nq + i, h))),
            scratch_shapes=[stat, stat, stat]),
        out_shape=(jax.ShapeDtypeStruct((T, MLA_W), F32), jax.ShapeDtypeStruct((T, MLA_W), F32)),
        compiler_params=_cparams(3),
    )(*bounds, q, k, v, pos_col, pos_row.reshape(nb, nk, tk))


def _mid(x, tgt, o, ga, u, gb, w_out, pool_w, pool_scale, ln_g, ln_b, S, tm):
    T = x.shape[0]
    tps = S // tm
    hb = tm // HALO

    def body(x_ref, tgt_ref, o_ref, ga_ref, u_ref, uh_ref, gb_ref, wout_ref, pw_ref,
             ps_ref, lng_ref, lnb_ref,
             dz_ref, do_ref, delta_ref, dga_ref, dgb_ref, dpc_ref,
             dwout_ref, dpw_ref, dps_ref, dlng_ref, dlnb_ref, loss_ref):
        i = pl.program_id(0)

        @pl.when(i == 0)
        def _():
            dwout_ref[...] = jnp.zeros_like(dwout_ref)
            dpw_ref[...] = jnp.zeros_like(dpw_ref)
            dps_ref[...] = jnp.zeros_like(dps_ref)
            dlng_ref[...] = jnp.zeros_like(dlng_ref)
            dlnb_ref[...] = jnp.zeros_like(dlnb_ref)
            loss_ref[...] = jnp.zeros_like(loss_ref)

        seq_tile = i % tps
        tpos = seq_tile * tm + lax.broadcasted_iota(jnp.int32, (tm, 1), 0)
        ga_v = ga_ref[...].astype(F32)
        sig_a = jax.nn.sigmoid(ga_v)
        silu_a = ga_v * sig_a
        o_v = o_ref[...]
        ya = o_v * silu_a

        u_v = u_ref[...].astype(F32)
        halo = jnp.where(seq_tile == 0, 0.0, uh_ref[...].astype(F32))
        pooled, cnts, mixed = [], [], []
        for g in range(POOL_G):
            lanes = slice(g * POOL_GD, (g + 1) * POOL_GD)
            w = jnp.concatenate([halo[:, lanes], u_v[:, lanes]], axis=0)
            for st in range(g + 1):
                w = w + pltpu.roll(w, 1 << st, 0)
            cnt = jnp.minimum(tpos + 1, 2 << g).astype(F32)
            pg = (w[HALO:, :] / cnt - u_v[:, lanes]).astype(BF16)
            pooled.append(pg)
            cnts.append(cnt)
            mixed.append(_dot(pg, pw_ref[g]))
        mixed = jnp.concatenate(mixed, axis=1)
        ps = ps_ref[...]
        ybp = mixed * ps
        gb_v = gb_ref[...].astype(F32)
        sig_b = jax.nn.sigmoid(gb_v)
        silu_b = gb_v * sig_b
        yb = ybp * silu_b

        cat = jnp.concatenate([ya, yb], axis=1).astype(BF16)
        z = ALPHA * x_ref[...] + _dot(cat, wout_ref[...])
        mu = jnp.mean(z, axis=-1, keepdims=True)
        zc = z - mu
        rstd = lax.rsqrt(jnp.mean(zc * zc, axis=-1, keepdims=True) + LN_EPS)
        zhat = zc * rstd
        lng = lng_ref[...]
        err = (zhat * lng + lnb_ref[...]) - tgt_ref[...]
        row_loss = jnp.sum(err * err, axis=1, keepdims=True)
        loss_ref[...] += jnp.broadcast_to(jnp.sum(row_loss, axis=0, keepdims=True) * (0.5 / D_MODEL), (1, 128))
        dy = err * (1.0 / D_MODEL)
        dlng_ref[...] += jnp.sum(dy * zhat, axis=0, keepdims=True)
        dlnb_ref[...] += jnp.sum(dy, axis=0, keepdims=True)
        dzh = dy * lng
        dz = rstd * (dzh - jnp.mean(dzh, axis=-1, keepdims=True)
                     - zhat * jnp.mean(dzh * zhat, axis=-1, keepdims=True))
        dz_ref[...] = dz
        dzb = dz.astype(BF16)
        dwout_ref[...] += _dot_tn(cat, dzb)
        dcat = _dot_nt(dzb, wout_ref[...])
        dya = dcat[:, :MLA_W]
        dyb = dcat[:, MLA_W:]

        do = dya * silu_a
        do_ref[...] = do.astype(BF16)
        prod = do * o_v
        for hh in range(HEADS):
            lanes = slice(hh * 128, (hh + 1) * 128)
            delta_ref[:, lanes] = jnp.broadcast_to(jnp.sum(prod[:, lanes], axis=1, keepdims=True), (tm, 128))
        dga_ref[...] = (dya * o_v * (sig_a * (1.0 + ga_v * (1.0 - sig_a)))).astype(BF16)
        dgb_ref[...] = (dyb * ybp * (sig_b * (1.0 + gb_v * (1.0 - sig_b)))).astype(BF16)
        dybp = dyb * silu_b
        dps_ref[...] += jnp.sum(dybp * mixed, axis=0, keepdims=True)
        dmixed = (dybp * ps).astype(BF16)
        for g in range(POOL_G):
            lanes = slice(g * POOL_GD, (g + 1) * POOL_GD)
            dpw_ref[g] += _dot_tn(pooled[g], dmixed[:, lanes])
            dpc_ref[:, lanes] = (_dot_nt(dmixed[:, lanes], pw_ref[g]) / cnts[g]).astype(BF16)

    row = lambda w: pl.BlockSpec((tm, w), lambda i: (i, 0))
    f = lambda w, dt: jax.ShapeDtypeStruct((T, w), dt)
    halo_spec = pl.BlockSpec((HALO, POOL_W), lambda i: (jnp.maximum(i * hb - 1, 0), 0))
    return pl.pallas_call(
        body, name="mid", grid=(T // tm,),
        in_specs=[row(D_MODEL), row(D_MODEL), row(MLA_W), row(MLA_W), row(POOL_W), halo_spec, row(POOL_W),
                  _full(w_out.shape), _full(pool_w.shape),
                  _full(pool_scale.shape), _full(ln_g.shape), _full(ln_b.shape)],
        out_specs=(row(D_MODEL), row(MLA_W), row(MLA_W), row(MLA_W), row(POOL_W), row(POOL_W),
                   _full((D_MODEL, D_MODEL)), _full(pool_w.shape), _full((1, POOL_W)),
                   _full((1, D_MODEL)), _full((1, D_MODEL)), _full((1, 128))),
        out_shape=(f(D_MODEL, F32), f(MLA_W, BF16), f(MLA_W, F32), f(MLA_W, BF16), f(POOL_W, BF16), f(POOL_W, BF16),
                   jax.ShapeDtypeStruct((D_MODEL, D_MODEL), F32), jax.ShapeDtypeStruct(pool_w.shape, F32),
                   jax.ShapeDtypeStruct((1, POOL_W), F32), jax.ShapeDtypeStruct((1, D_MODEL), F32),
                   jax.ShapeDtypeStruct((1, D_MODEL), F32), jax.ShapeDtypeStruct((1, 128), F32)),
        compiler_params=_cparams(1),
    )(x, tgt, o, ga, u, u, gb, w_out, pool_w, pool_scale, ln_g, ln_b)


def _attn_bwd(q, k, v, do, lse, delta, pos_col, pos_row, bounds, nb, S, tq, tk):
    T = q.shape[0]
    nq, nk = S // tq, S // tk
    reps = tk // 128
    hg = HEAD_GROUP

    def body(qmin_ref, qmax_ref, kmin_ref, kmax_ref, q_ref, k_ref, v_ref, do_ref, lse_ref, dl_ref, pc_ref, pr_ref,
             dq_out, dk_out, dv_out, dq_ref, dk_ref, dv_ref):
        b, j = pl.program_id(0), pl.program_id(2)

        @pl.when(j == 0)
        def _():
            dq_ref[...] = jnp.zeros_like(dq_ref)

        dk_ref[...] = jnp.zeros_like(dk_ref)
        dv_ref[...] = jnp.zeros_like(dv_ref)
        k_lo = kmin_ref[b * nk + j]
        k_hi = kmax_ref[b * nk + j]

        def tile(i, masked):
            rows = pl.ds(pl.multiple_of(i * tq, tq), tq)
            if masked:
                keep = pc_ref[rows, :] >= pr_ref[...]
            stage = []
            for g in range(hg):
                qk = slice(g * HEAD_PAD, (g + 1) * HEAD_PAD)
                hv = slice(g * 128, (g + 1) * 128)
                s = _dot_nt(q_ref[rows, qk], k_ref[:, qk])
                if masked:
                    s = jnp.where(keep, s, NEG)
                stage.append((s, _dot_nt(do_ref[rows, hv], v_ref[:, hv])))
            grads = []
            for g in range(hg):
                hv = slice(g * 128, (g + 1) * 128)
                s, dp = stage[g]
                p = jnp.exp2(s - jnp.concatenate([lse_ref[rows, hv]] * reps, axis=1))
                ds = (p * (dp - jnp.concatenate([dl_ref[rows, hv]] * reps, axis=1))).astype(BF16)
                grads.append((p.astype(BF16), ds))
            for g in range(hg):
                qk = slice(g * HEAD_PAD, (g + 1) * HEAD_PAD)
                hv = slice(g * 128, (g + 1) * 128)
                p, ds = grads[g]
                dv_ref[:, hv] += _dot_tn(p, do_ref[rows, hv])
                dq_ref[rows, qk] += _dot(ds, k_ref[:, qk])
                dk_ref[:, qk] += _dot_tn(ds, q_ref[rows, qk])

        def step(i, carry):
            visible = k_lo <= qmax_ref[b * nq + i]
            clear = qmin_ref[b * nq + i] >= k_hi

            @pl.when(jnp.logical_and(visible, clear))
            def _():
                tile(i, False)

            @pl.when(jnp.logical_and(visible, jnp.logical_not(clear)))
            def _():
                tile(i, True)
            return carry

        lax.fori_loop(0, nq, step, 0)
        dk_out[...] = dk_ref[...].astype(BF16)
        dv_out[...] = dv_ref[...].astype(BF16)

        @pl.when(j == nk - 1)
        def _():
            dq_out[...] = dq_ref[...].astype(BF16)

    ng = HEADS // hg
    seq = lambda w: pl.BlockSpec((S, w), lambda b, h, j, *_: (b, h))
    blk = lambda w: pl.BlockSpec((tk, w), lambda b, h, j, *_: (b * nk + j, h))
    return pl.pallas_call(
        body, name="attn_bwd",
        grid_spec=pltpu.PrefetchScalarGridSpec(
            num_scalar_prefetch=4, grid=(nb, ng, nk),
            in_specs=[seq(hg * HEAD_PAD), blk(hg * HEAD_PAD), blk(hg * 128),
                      seq(hg * 128), seq(hg * 128), seq(hg * 128),
                      pl.BlockSpec((S, 1), lambda b, h, j, *_: (b, 0)),
                      pl.BlockSpec((None, 1, tk), lambda b, h, j, *_: (b, 0, j))],
            out_specs=(seq(hg * HEAD_PAD), blk(hg * HEAD_PAD), blk(hg * 128)),
            scratch_shapes=[pltpu.VMEM((S, hg * HEAD_PAD), F32), pltpu.VMEM((tk, hg * HEAD_PAD), F32),
                            pltpu.VMEM((tk, hg * 128), F32)]),
        out_shape=(jax.ShapeDtypeStruct((T, HEADS * HEAD_PAD), BF16),
                   jax.ShapeDtypeStruct((T, HEADS * HEAD_PAD), BF16),
                   jax.ShapeDtypeStruct((T, MLA_W), BF16)),
        compiler_params=_cparams(3),
    )(*bounds, q, k, v, do, lse, delta, pos_col, pos_row)


def _bwd_proj(dq, dk, dv, xq, xkv, x, dz, dga, dgb, dpc, rope_tab, w_uq_t, w_ukv, w_in_t, gq, gkv, S, tm):
    T = x.shape[0]
    tps = S // tm
    hb = tm // HALO
    n_tiles = T // tm

    def body(dq_ref, dk_ref, dv_ref, xq_ref, xkv_ref, x_ref, dz_ref, dga_ref, dgb_ref, dpc_ref, dph_ref,
             tab_ref, wuq_ref, wukv_ref, win_ref, gq_ref, gkv_ref,
             dx_ref, dwin_hbm, dwuq_hbm, dwukv_hbm, dgq_ref, dgkv_ref,
             acc_win, acc_wuq, acc_wukv, dh_sc):
        i = pl.program_id(0)

        @pl.when(i == 0)
        def _():
            acc_win[...] = jnp.zeros_like(acc_win)
            acc_wuq[...] = jnp.zeros_like(acc_wuq)
            acc_wukv[...] = jnp.zeros_like(acc_wukv)
            dgq_ref[...] = jnp.zeros_like(dgq_ref)
            dgkv_ref[...] = jnp.zeros_like(dgkv_ref)
            dh_sc[...] = jnp.zeros_like(dh_sc)

        dh_prev = dh_sc[...]
        dx_ref[...] = ALPHA * dz_ref[...] + _dot(dh_prev, win_ref[...])
        acc_win[...] += _dot_tn(dh_prev, x_ref[...].astype(BF16))

        live = jnp.where(i < n_tiles, 1.0, 0.0)
        c, sa, sb = _expand_rope_table(tab_ref[...])
        dq_v = dq_ref[...].astype(F32) * (SCALE * live)
        dk_v = dk_ref[...].astype(F32) * (LN2 * live)
        dv_v = dv_ref[...].astype(F32) * live
        dq_parts, dkv_parts = [], []
        dkr = jnp.zeros((tm, 128), F32)
        for hh in range(HEADS):
            b0 = hh * HEAD_PAD
            dq_parts.append(dq_v[:, b0:b0 + 128].astype(BF16))
            dq_parts.append(_rope(dq_v[:, b0 + 128:b0 + 256], c, sa, sb, -1.0).astype(BF16))
            dkv_parts.append(dk_v[:, b0:b0 + 128].astype(BF16))
            dkv_parts.append(dv_v[:, hh * 128:(hh + 1) * 128].astype(BF16))
            dkr = dkr + dk_v[:, b0 + 128:b0 + 256]
        dqp = jnp.concatenate(dq_parts, axis=1)
        dkvp = jnp.concatenate(dkv_parts, axis=1)
        dkrr = _rope(dkr, c, sa, sb, -1.0)

        def rms_bwd(xv, g, dyn, dg_ref):
            r = lax.rsqrt(jnp.mean(xv * xv, axis=-1, keepdims=True) + RMS_EPS)
            xhat = xv * r
            dg_ref[...] += jnp.sum(dyn * xhat, axis=0, keepdims=True)
            dxh = dyn * g
            return r * (dxh - xhat * jnp.mean(dxh * xhat, axis=-1, keepdims=True))

        xq_v = xq_ref[...].astype(F32)
        gq_v = gq_ref[...]
        rq = lax.rsqrt(jnp.mean(xq_v * xq_v, axis=-1, keepdims=True) + RMS_EPS)
        acc_wuq[...] += _dot_tn(dqp, ((xq_v * rq) * gq_v).astype(BF16))
        dxq = rms_bwd(xq_v, gq_v, _dot(dqp, wuq_ref[...]), dgq_ref)

        xkv_v = xkv_ref[...].astype(F32)
        gkv_v = gkv_ref[...]
        rkv = lax.rsqrt(jnp.mean(xkv_v * xkv_v, axis=-1, keepdims=True) + RMS_EPS)
        acc_wukv[...] += _dot_tn(((xkv_v * rkv) * gkv_v).astype(BF16), dkvp)
        dxkv = rms_bwd(xkv_v, gkv_v, _dot_nt(dkvp, wukv_ref[...]), dgkv_ref)

        seq_tile = i % tps
        tpos = seq_tile * tm + lax.broadcasted_iota(jnp.int32, (tm, 1), 0)
        dpc_v = dpc_ref[...].astype(F32)
        halo = jnp.where(seq_tile == tps - 1, 0.0, dph_ref[...].astype(F32))
        n = tm + HALO
        du = []
        for g in range(POOL_G):
            lanes = slice(g * POOL_GD, (g + 1) * POOL_GD)
            f = jnp.concatenate([dpc_v[:, lanes], halo[:, lanes]], axis=0)
            for st in range(g + 1):
                f = f + pltpu.roll(f, n - (1 << st), 0)
            cnt = jnp.minimum(tpos + 1, 2 << g).astype(F32)
            du.append((f[:tm, :] - dpc_v[:, lanes] * cnt).astype(BF16))

        dh_sc[...] = jnp.concatenate([dxq.astype(BF16), dxkv.astype(BF16), dkrr.astype(BF16), dga_ref[...]]
                                     + du + [dgb_ref[...]], axis=1)

        @pl.when(i == n_tiles)
        def _():
            pltpu.sync_copy(acc_win.at[pl.ds(0, ROPE_END)], dwin_hbm.at[pl.ds(0, ROPE_END)])
            pltpu.sync_copy(acc_win.at[pl.ds(COL_GA, IN_EXT - COL_GA)], dwin_hbm.at[pl.ds(ROPE_END, IN_W - ROPE_END)])
            for hh in range(HEADS):
                pltpu.sync_copy(acc_wuq.at[pl.ds(hh * HEAD_PAD, NOPE + ROPE)], dwuq_hbm.at[hh])
            pltpu.sync_copy(acc_wukv, dwukv_hbm)

    cur = lambda w: pl.BlockSpec((tm, w), lambda i: (jnp.minimum(i, n_tiles - 1), 0))
    prev = lambda w: pl.BlockSpec((tm, w), lambda i: (jnp.maximum(i - 1, 0), 0))
    halo_spec = pl.BlockSpec((HALO, POOL_W), lambda i: (jnp.minimum((i + 1) * hb, T // HALO - 1), 0))
    return pl.pallas_call(
        body, name="bwd_proj", grid=(n_tiles + 1,),
        in_specs=[cur(1024), cur(1024), cur(512), cur(512), cur(256), prev(D_MODEL), prev(D_MODEL),
                  cur(512), cur(512), cur(512), halo_spec, cur(128),
                  _full(w_uq_t.shape), _full(w_ukv.shape), _full(w_in_t.shape), _full(gq.shape), _full(gkv.shape)],
        out_specs=(prev(D_MODEL), ANY, ANY, ANY, _full((1, Q_LORA)), _full((1, KV_LORA))),
        out_shape=(jax.ShapeDtypeStruct((T, D_MODEL), F32),
                   jax.ShapeDtypeStruct((IN_W, D_MODEL), F32),
                   jax.ShapeDtypeStruct((HEADS, NOPE + ROPE, Q_LORA), F32),
                   jax.ShapeDtypeStruct((KV_LORA, 1024), F32),
                   jax.ShapeDtypeStruct((1, Q_LORA), F32), jax.ShapeDtypeStruct((1, KV_LORA), F32)),
        scratch_shapes=[pltpu.VMEM((IN_EXT, D_MODEL), F32), pltpu.VMEM((HEADS * HEAD_PAD, Q_LORA), F32),
                        pltpu.VMEM((KV_LORA, 1024), F32), pltpu.VMEM((tm, IN_EXT), BF16)],
        compiler_params=_cparams(1),
    )(dq, dk, dv, xq, xkv, x, dz, dga, dgb, dpc, dpc, rope_tab, w_uq_t, w_ukv, w_in_t, gq, gkv)


def kernel(x, positions, w_in, q_norm_g, w_uq, kv_norm_g, w_ukv, pool_w, pool_scale, w_out, ln_g, ln_b, loss_target, m_w_in, m_q_norm_g, m_w_uq, m_kv_norm_g, m_w_ukv, m_pool_w, m_pool_scale, m_w_out, m_ln_g, m_ln_b, v_w_in, v_q_norm_g, v_w_uq, v_kv_norm_g, v_w_ukv, v_pool_w, v_pool_scale, v_w_out, v_ln_g, v_ln_b):
    nb, S, _ = x.shape
    T = nb * S
    tm = min(256, S)
    tq = min(512, S)
    tk = min(512, S)
    assert S % tm == 0 and tm % HALO == 0 and S % tq == 0 and S % tk == 0

    cx, cy, cc = lax.axis_index("x"), lax.axis_index("y"), lax.axis_index("c")
    me = 2 * cx + cy

    half = ROPE // 2
    inv_freq = ROPE_THETA ** (-jnp.arange(half, dtype=F32) / half)
    freq_row = jnp.concatenate([inv_freq, inv_freq, jnp.zeros((2 * half,), F32)]).reshape(1, 128)
    pos_col = positions.reshape(T, 1)
    pos_row = positions.reshape(nb, 1, S)
    pos_q = positions.reshape(nb, S // tq, tq)
    pos_k = positions.reshape(nb, S // tk, tk)
    bounds = (jnp.min(pos_q, axis=2).reshape(-1), jnp.max(pos_q, axis=2).reshape(-1),
              jnp.min(pos_k, axis=2).reshape(-1), jnp.max(pos_k, axis=2).reshape(-1))

    def own_slot(w, slot_rows):
        blk = jnp.pad(w.astype(BF16), ((0, slot_rows - w.shape[0]), (0, 0)))
        return lax.dynamic_update_slice(jnp.zeros((N_CHIPS,) + blk.shape, BF16), blk[None], (me, 0, 0))

    (w_in_g, w_uq_g, w_ukv_g), rope_tab = _weight_gather(
        [own_slot(w_in.T, IN_SHARD), own_slot(w_uq.T, HEAD_PAD), own_slot(w_ukv, KV_LORA)], (IN_SHARD, NOPE + ROPE, KV_LORA),
        pos_col, freq_row)
    w_in_f = w_in_g.reshape(IN_W, D_MODEL)
    w_in_t = jnp.concatenate([w_in_f[:ROPE_END], jnp.zeros((COL_GA - ROPE_END, D_MODEL), BF16), w_in_f[ROPE_END:]], axis=0)
    w_uq_t = w_uq_g.reshape(HEADS * HEAD_PAD, Q_LORA)
    w_ukv_f = w_ukv_g.transpose(1, 0, 2).reshape(KV_LORA, 1024)
    pool_w_b = pool_w.astype(BF16)
    gq2 = q_norm_g.reshape(1, Q_LORA)
    gkv2 = kv_norm_g.reshape(1, KV_LORA)
    ps2 = pool_scale.reshape(1, POOL_W)

    xf = x.reshape(T, D_MODEL)
    tgt = loss_target.reshape(T, D_MODEL)

    xq, xkv, ga, u, gb, q, k, v, w_out_g = _fwd_proj(
        xf, w_in_t, w_uq_t, w_ukv_f, gq2, gkv2, rope_tab, own_slot(w_out, 256), tm)
    w_out_f = w_out_g.reshape(D_MODEL, D_MODEL)
    o, lse = _attn_fwd(q, k, v, pos_col, pos_row, bounds, nb, S, tq, tk)

    (dz, do, delta, dga, dgb, dpc, d_w_out, d_pool_w, d_pool_scale, d_ln_g, d_ln_b, loss_part) = _mid(
        xf, tgt, o, ga, u, gb, w_out_f, pool_w_b, ps2, ln_g, ln_b, S, tm)

    dq, dk, dv = _attn_bwd(q, k, v, do, lse, delta, pos_col, pos_row, bounds, nb, S, tq, tk)
    dx, d_w_in_t, d_w_uq_t, d_w_ukv, d_gq, d_gkv = _bwd_proj(
        dq, dk, dv, xq, xkv, xf, dz, dga, dgb, dpc, rope_tab, w_uq_t, w_ukv_f, w_in_t, gq2, gkv2, S, tm)
    grad_x = dx.reshape(nb, S, D_MODEL)

    g_in = d_w_in_t.reshape(N_CHIPS, IN_SHARD, D_MODEL)
    g_uq = d_w_uq_t
    g_ukv = d_w_ukv.reshape(KV_LORA, N_CHIPS, 256).transpose(1, 0, 2)
    g_out = d_w_out.reshape(N_CHIPS, 256, D_MODEL)
    wide = lambda a: jnp.pad(a.reshape(1, -1), ((0, 0), (0, D_MODEL - a.size)))
    vec = jnp.concatenate([d_ln_g, d_ln_b, wide(d_pool_scale), wide(d_gq), jnp.zeros((4, D_MODEL), F32),
                           wide(d_gkv), wide(loss_part), jnp.zeros((VEC_ROWS - 10, D_MODEL), F32)], axis=0)
    to_all = lambda a: jnp.broadcast_to(a[None], (N_CHIPS,) + a.shape)
    gs = [g_in, g_uq, g_ukv, g_out, to_all(d_pool_w.reshape(-1, D_MODEL)), to_all(vec)]
    g_all = _grad_reduce(gs, (BF16,) * N_BIG + (F32, F32))
    g_big = g_all[:N_BIG]
    pw_sum = g_all[N_BIG].reshape(POOL_G * POOL_GD, POOL_GD)
    vec_sum = g_all[N_BIG + 1]

    big = _adamw_big(g_big, [w_in.T, w_uq.T, w_ukv, w_out], [m_w_in.T, m_w_uq.T, m_w_ukv, m_w_out],
                     [v_w_in.T, v_w_uq.T, v_w_ukv, v_w_out])
    two_d = lambda a: a.reshape(-1, a.shape[-1])
    small_names = lambda pw, lg, lb, ps, gq, gkv: [two_d(pw), lg, lb, ps.reshape(1, -1), gq.reshape(1, -1), gkv.reshape(1, -1)]
    small, loss_row = _adamw_small(
        pw_sum, vec_sum,
        small_names(pool_w, ln_g, ln_b, pool_scale, q_norm_g, kv_norm_g),
        small_names(m_pool_w, m_ln_g, m_ln_b, m_pool_scale, m_q_norm_g, m_kv_norm_g),
        small_names(v_pool_w, v_ln_g, v_ln_b, v_pool_scale, v_q_norm_g, v_kv_norm_g))
    loss = loss_row[0, 0]

    def leaves(kind):
        b = [g_big[t] if kind == 0 else big[t][kind - 1] for t in range(N_BIG)]
        b = [b[0].T, b[1].T, b[2], b[3]]
        s = [small[t][kind] for t in range(6)]
        return (b[0], s[4].reshape(Q_LORA), b[1], s[5].reshape(KV_LORA), b[2],
                s[0].reshape(POOL_G, POOL_GD, POOL_GD), s[3].reshape(POOL_W), b[3], s[1], s[2])

    return (loss, grad_x) + leaves(0) + leaves(1) + leaves(2) + leaves(3)
```

```python
import jax
import jax.numpy as jnp
from jax import lax
from jax.experimental import pallas as pl
from jax.experimental.pallas import tpu as pltpu

F32 = jnp.float32
BF16 = jnp.bfloat16
MESH = pl.DeviceIdType.MESH

HEADS = 4
NOPE = 128
ROPE = 64
HEAD_PAD = 256
Q_LORA = 512
KV_LORA = 256
MLA_W = 512
POOL_W = 512
POOL_G = 4
POOL_GD = 128
D_MODEL = 1024
IN_W = 2368
IN_EXT = 2432
COL_KV, COL_KR, COL_GA, COL_U, COL_GB = 512, 768, 896, 1408, 1920
ROPE_END = COL_KR + 64
IN_SHARD = IN_W // 4
ROPE_THETA = 10000.0
RMS_EPS = 1e-6
LN_EPS = 1e-5
ALPHA = 2.0 ** 0.25
SCALE = 192.0 ** -0.5
LOG2E = 1.4426950408889634
LN2 = 0.6931471805599453
QSCALE = SCALE * LOG2E
NEG = float(jnp.finfo(jnp.float32).min)
HEAD_GROUP = 2
HALO = 16

ADAM_LR = 0.001
ADAM_B1 = 0.9
ADAM_B2 = 0.999
ADAM_EPS = 1e-08
ADAM_WD = 0.01
ADAM_STEP = 10

N_CHIPS = 4
N_BIG = 4
VEC_ROWS = 16

VMEM_LIMIT = 56 * 1024 * 1024


def _cparams(n_grid_dims=0, **kw):
    sem = ("arbitrary",) * n_grid_dims if n_grid_dims else None
    return pltpu.CompilerParams(dimension_semantics=sem, vmem_limit_bytes=VMEM_LIMIT, **kw)


def _full(shape):
    nd = len(shape)
    return pl.BlockSpec(shape, lambda *_: (0,) * nd)


def _dot(a, b):
    return jnp.dot(a, b, preferred_element_type=F32)


def _dot_nt(a, b):
    return lax.dot_general(a, b, (((1,), (1,)), ((), ())), preferred_element_type=F32)


def _dot_tn(a, b):
    return lax.dot_general(a, b, (((0,), (0,)), ((), ())), preferred_element_type=F32)


def _rope_table(pos_col, freq_row):
    lane = lax.broadcasted_iota(jnp.int32, (1, 128), 1)
    ang = pos_col.astype(F32) * freq_row
    return jnp.where(lane < 32, jnp.cos(ang), jnp.where(lane < 64, jnp.sin(ang), 0.0))


def _expand_rope_table(tab):
    lane = lax.broadcasted_iota(jnp.int32, (1, 128), 1)
    second = jnp.logical_and(lane >= 32, lane < 64)
    c = jnp.where(lane < 32, tab, jnp.where(second, pltpu.roll(tab, 32, 1), 0.0))
    sa = jnp.where(lane < 32, pltpu.roll(tab, 96, 1), 0.0)
    sb = jnp.where(second, tab, 0.0)
    return c, sa, sb


def _rope(g, c, sa, sb, sign):
    return g * c + sign * (pltpu.roll(g, 32, 1) * sb - pltpu.roll(g, 96, 1) * sa)


def _place():
    x, y, c = lax.axis_index("x"), lax.axis_index("y"), lax.axis_index("c")
    chips = [(1 - x, y), (x, 1 - y), (1 - x, 1 - y)]
    return x, y, c, chips


ANY = pl.BlockSpec(memory_space=pl.ANY)


ROPE_CHUNK = 2048


def _weight_gather(slots, valid_rows, pos_col, freq_row):
    n = len(slots)
    T = pos_col.shape[0]
    chunk = min(ROPE_CHUNK, T)
    assert T % chunk == 0

    def body(*refs):
        pos_hbm, freq_ref = refs[n:n + 2]
        outs = refs[n + 2:2 * n + 2]
        tab_hbm = refs[2 * n + 2]
        send_sems, recv_sems, pos_buf, tab_buf = refs[2 * n + 3:]
        x, y, c, chips = _place()
        me = 2 * x + y

        def copy(t, k, chip_idx, half, to):
            hc = slots[t].shape[2] // 2
            blk = outs[t].at[chip_idx, pl.ds(0, valid_rows[t]), pl.ds(half * hc, hc)]
            return pltpu.make_async_remote_copy(
                src_ref=blk, dst_ref=blk, send_sem=send_sems.at[6 * t + k], recv_sem=recv_sems.at[6 * t + k],
                device_id=to, device_id_type=MESH)

        first = [copy(t, j, me, c, (cx, cy, c)) for t in range(n) for j, (cx, cy) in enumerate(chips)]
        for cp in first:
            cp.start()

        def table_chunk(r, carry):
            rows = pl.ds(pl.multiple_of(r * chunk, chunk), chunk)
            pltpu.sync_copy(pos_hbm.at[rows], pos_buf)
            tab_buf[...] = _rope_table(pos_buf[...], freq_ref[...])
            pltpu.sync_copy(tab_buf, tab_hbm.at[rows])
            return carry

        lax.fori_loop(0, T // chunk, table_chunk, 0)
        passed = []
        for j, (cx, cy) in enumerate(chips):
            for t in range(n):
                copy(t, j, 2 * cx + cy, c, (x, y, c)).wait_recv()
                fwd = copy(t, 3 + j, 2 * cx + cy, c, (x, y, 1 - c))
                fwd.start()
                passed.append(fwd)
        for j, (cx, cy) in enumerate(chips):
            for t in range(n):
                copy(t, 3 + j, 2 * cx + cy, 1 - c, (x, y, c)).wait_recv()
        for cp in first + passed:
            cp.wait_send()

    outs = pl.pallas_call(
        body, name="weight_gather",
        out_shape=tuple(jax.ShapeDtypeStruct(a.shape, a.dtype) for a in slots) + (jax.ShapeDtypeStruct((T, 128), F32),),
        in_specs=[ANY] * n + [ANY, pl.BlockSpec(memory_space=pltpu.VMEM)], out_specs=(ANY,) * (n + 1),
        input_output_aliases={t: t for t in range(n)},
        scratch_shapes=[pltpu.SemaphoreType.DMA((6 * n,)), pltpu.SemaphoreType.DMA((6 * n,)),
                        pltpu.VMEM((chunk, 1), jnp.int32), pltpu.VMEM((chunk, 128), F32)],
    )(*slots, pos_col, freq_row)
    return outs[:n], outs[n]


def _reduce_scratch(gs, wire_dtypes):
    n = len(gs)
    half = [(g.shape[1], g.shape[2] // 2) for g in gs]
    return ([pltpu.VMEM((4,) + h, F32) for h in half] + [pltpu.VMEM((4,) + h, F32) for h in half]
            + [pltpu.VMEM((3,) + h, w) for h, w in zip(half, wire_dtypes)]
            + [pltpu.VMEM((3,) + h, w) for h, w in zip(half, wire_dtypes)]
            + [pltpu.VMEM(h, F32) for h in half]
            + [pltpu.SemaphoreType.DMA((4 * n,)), pltpu.SemaphoreType.DMA((4 * n,)),
               pltpu.SemaphoreType.DMA((3 * n,)), pltpu.SemaphoreType.DMA((3 * n,)),
               pltpu.SemaphoreType.DMA((n,)), pltpu.SemaphoreType.DMA((n,)),
               pltpu.SemaphoreType.DMA((4 * n,)), pltpu.SemaphoreType.DMA((n,))])


def _reduce_phases(gs, wire_dtypes, g_refs, out_refs, scr):
    n = len(gs)
    hcs = [g.shape[2] // 2 for g in gs]
    own, sib, wire, got, fin = (scr[i * n:(i + 1) * n] for i in range(5))
    d2d_send, d2d_recv, ici_send, ici_recv, fin_send, fin_recv, loc_in, loc_out = scr[5 * n:]

    def place():
        x, y, c, chips = _place()
        return c, chips, (x, y, 1 - c), [2 * cx + cy for cx, cy in chips] + [2 * x + y]

    def remote(src, dst, send, recv, to):
        return pltpu.make_async_remote_copy(src_ref=src, dst_ref=dst, send_sem=send, recv_sem=recv,
                                            device_id=to, device_id_type=MESH)

    def block(ref, t, half, lead=None):
        cols = pl.ds(half * hcs[t], hcs[t])
        rows = pl.ds(0, gs[t].shape[1])
        return ref.at[rows, cols] if lead is None else ref.at[lead, rows, cols]

    def load(t, j):
        c, _, _, dests = place()
        return pltpu.make_async_copy(block(g_refs[t], t, c, dests[j]), own[t].at[j], loc_in.at[4 * t + j])

    def d2d(t, j):
        c, _, sibling, dests = place()
        return remote(block(g_refs[t], t, 1 - c, dests[j]), sib[t].at[j],
                      d2d_send.at[4 * t + j], d2d_recv.at[4 * t + j], sibling)

    def ici(t, j):
        c, chips, _, _ = place()
        return remote(wire[t].at[j], got[t].at[j], ici_send.at[3 * t + j], ici_recv.at[3 * t + j], chips[j] + (c,))

    def store(t):
        c = place()[0]
        return pltpu.make_async_copy(fin[t], block(out_refs[t], t, c), loc_out.at[t])

    def final(t, half_of):
        c, _, sibling, _ = place()
        return remote(fin[t], block(out_refs[t], t, c if half_of == "mine" else 1 - c),
                      fin_send.at[t], fin_recv.at[t], sibling)

    def start():
        for j in range(4):
            for t in range(n):
                load(t, j).start()
                d2d(t, j).start()

    def exchange():
        for j in range(3):
            for t in range(n):
                load(t, j).wait()
                d2d(t, j).wait_recv()
                wire[t][j] = (own[t][j] + sib[t][j]).astype(wire_dtypes[t])
                ici(t, j).start()

    def finish():
        for t in range(n):
            load(t, 3).wait()
            d2d(t, 3).wait_recv()
            for j in range(3):
                ici(t, j).wait_recv()
            fin[t][...] = (((own[t][3] + sib[t][3]) + got[t][0].astype(F32))
                           + (got[t][1].astype(F32) + got[t][2].astype(F32)))
            store(t).start()
            final(t, "mine").start()

    def drain():
        for t in range(n):
            final(t, "theirs").wait_recv()
        for t in range(n):
            for j in range(4):
                d2d(t, j).wait_send()
            for j in range(3):
                ici(t, j).wait_send()
            final(t, "mine").wait_send()
            store(t).wait()

    return start, exchange, finish, drain


def _grad_reduce(gs, wire_dtypes):
    n = len(gs)

    def body(*refs):
        for phase in _reduce_phases(gs, wire_dtypes, refs[:n], refs[n:2 * n], refs[2 * n:]):
            phase()

    return pl.pallas_call(
        body, name="grad_reduce",
        out_shape=tuple(jax.ShapeDtypeStruct(g.shape[1:], F32) for g in gs),
        in_specs=[ANY] * n, out_specs=(ANY,) * n, scratch_shapes=_reduce_scratch(gs, wire_dtypes),
        compiler_params=_cparams(),
    )(*gs)


def _adamw_math(g, w, m, v):
    nm = ADAM_B1 * m + (1.0 - ADAM_B1) * g
    nv = ADAM_B2 * v + (1.0 - ADAM_B2) * (g * g)
    m_hat = nm / (1.0 - ADAM_B1 ** ADAM_STEP)
    v_hat = nv / (1.0 - ADAM_B2 ** ADAM_STEP)
    return -ADAM_LR * (m_hat / (jnp.sqrt(v_hat) + ADAM_EPS) + ADAM_WD * w), nm, nv


ADAM_STEPS = 8


def _adamw_big(gs, ws, ms, vs):
    n = len(gs)

    def body(*refs):
        for t in range(n):
            d, nm, nv = _adamw_math(refs[t][...], refs[n + t][...], refs[2 * n + t][...], refs[3 * n + t][...])
            refs[4 * n + 3 * t][...] = d
            refs[4 * n + 3 * t + 1][...] = nm
            refs[4 * n + 3 * t + 2][...] = nv

    def tile_spec(shape):
        rows, cols = shape
        if rows % (8 * ADAM_STEPS) == 0:
            return pl.BlockSpec((rows // ADAM_STEPS, cols), lambda i: (i, 0))
        return pl.BlockSpec((rows, cols // ADAM_STEPS), lambda i: (0, i))

    specs = [tile_spec(g.shape) for g in gs]
    out_specs, out_shape = [], []
    for t in range(n):
        out_specs += [specs[t]] * 3
        out_shape += [jax.ShapeDtypeStruct(gs[t].shape, F32)] * 3
    outs = pl.pallas_call(
        body, name="adamw_big", grid=(ADAM_STEPS,),
        in_specs=specs * 4, out_specs=tuple(out_specs), out_shape=tuple(out_shape),
        compiler_params=_cparams(1),
    )(*gs, *ws, *ms, *vs)
    return [outs[3 * t: 3 * t + 3] for t in range(n)]


def _adamw_small(pw_sum, vec_sum, ws, ms, vs):
    rows = (None, 0, 1, 2, 3, 8)
    n = len(ws)

    def body(pw_ref, vec_ref, *refs):
        outs = refs[3 * n:]
        for t in range(n):
            w_ref, m_ref, v_ref = refs[t], refs[n + t], refs[2 * n + t]
            if rows[t] is None:
                g = pw_ref[...]
            else:
                g = vec_ref[rows[t]:rows[t] + 1, 0:w_ref.shape[1]]
            d, nm, nv = _adamw_math(g, w_ref[...], m_ref[...], v_ref[...])
            outs[4 * t][...] = g
            outs[4 * t + 1][...] = d
            outs[4 * t + 2][...] = nm
            outs[4 * t + 3][...] = nv
        outs[4 * n][...] = vec_ref[9:10, 0:128]

    vm = pl.BlockSpec(memory_space=pltpu.VMEM)
    out_shape = []
    for w in ws:
        out_shape += [jax.ShapeDtypeStruct(w.shape, F32)] * 4
    out_shape.append(jax.ShapeDtypeStruct((1, 128), F32))
    outs = pl.pallas_call(
        body, name="adamw_small", in_specs=[vm] * (2 + 3 * n), out_specs=(vm,) * (4 * n + 1),
        out_shape=tuple(out_shape),
    )(pw_sum, vec_sum, *ws, *ms, *vs)
    return [outs[4 * t: 4 * t + 4] for t in range(n)], outs[4 * n]


def _fwd_proj(x, w_in_t, w_uq_t, w_ukv, gq, gkv, rope_tab, w_out_slots, tm):
    T = x.shape[0]
    n_steps = T // tm
    fwd_step = n_steps // 2

    def body(x_ref, win_ref, wuq_ref, wukv_ref, gq_ref, gkv_ref, tab_ref, wo_in,
             xq_ref, xkv_ref, ga_ref, u_ref, gb_ref, q_ref, k_ref, v_ref, wo_ref, send_sems, recv_sems):
        i = pl.program_id(0)
        px, py, pc, chips = _place()
        hc = D_MODEL // 2

        def wo_copy(k, chip_idx, half, to):
            blk = wo_ref.at[chip_idx, pl.ds(0, 256), pl.ds(half * hc, hc)]
            return pltpu.make_async_remote_copy(src_ref=blk, dst_ref=blk, send_sem=send_sems.at[k],
                                                recv_sem=recv_sems.at[k], device_id=to, device_id_type=MESH)

        @pl.when(i == 0)
        def _():
            for j, (cx, cy) in enumerate(chips):
                wo_copy(j, 2 * px + py, pc, (cx, cy, pc)).start()

        @pl.when(i == fwd_step)
        def _():
            for j, (cx, cy) in enumerate(chips):
                wo_copy(j, 2 * cx + cy, pc, (px, py, pc)).wait_recv()
                wo_copy(3 + j, 2 * cx + cy, pc, (px, py, 1 - pc)).start()

        @pl.when(i == n_steps - 1)
        def _():
            for j, (cx, cy) in enumerate(chips):
                wo_copy(3 + j, 2 * cx + cy, 1 - pc, (px, py, pc)).wait_recv()
            for j, (cx, cy) in enumerate(chips):
                wo_copy(j, 2 * px + py, pc, (cx, cy, pc)).wait_send()
                wo_copy(3 + j, 2 * cx + cy, pc, (px, py, 1 - pc)).wait_send()

        h = _dot_nt(x_ref[...].astype(BF16), win_ref[...])
        xq = h[:, 0:COL_KV]
        xkv = h[:, COL_KV:COL_KR]
        xq_ref[...] = xq.astype(BF16)
        xkv_ref[...] = xkv.astype(BF16)
        ga_ref[...] = h[:, COL_GA:COL_U].astype(BF16)
        u_ref[...] = h[:, COL_U:COL_GB].astype(BF16)
        gb_ref[...] = h[:, COL_GB:IN_EXT].astype(BF16)
        c, sa, sb = _expand_rope_table(tab_ref[...])
        rq = lax.rsqrt(jnp.mean(xq * xq, axis=-1, keepdims=True) + RMS_EPS)
        q = _dot_nt(((xq * rq) * gq_ref[...]).astype(BF16), wuq_ref[...]) * QSCALE
        rkv = lax.rsqrt(jnp.mean(xkv * xkv, axis=-1, keepdims=True) + RMS_EPS)
        kv = _dot(((xkv * rkv) * gkv_ref[...]).astype(BF16), wukv_ref[...])
        kr = _rope(h[:, COL_KR:COL_GA], c, sa, sb, 1.0).astype(BF16)
        for hh in range(HEADS):
            b0 = hh * HEAD_PAD
            q_ref[:, b0:b0 + 128] = q[:, b0:b0 + 128].astype(BF16)
            q_ref[:, b0 + 128:b0 + 256] = _rope(q[:, b0 + 128:b0 + 256], c, sa, sb, 1.0).astype(BF16)
            k_ref[:, b0:b0 + 128] = kv[:, b0:b0 + 128].astype(BF16)
            k_ref[:, b0 + 128:b0 + 256] = kr
            v_ref[:, hh * 128:(hh + 1) * 128] = kv[:, b0 + 128:b0 + 256].astype(BF16)

    row = lambda w: pl.BlockSpec((tm, w), lambda i: (i, 0))
    f = lambda w, dt: jax.ShapeDtypeStruct((T, w), dt)
    return pl.pallas_call(
        body, name="fwd_proj", grid=(n_steps,),
        in_specs=[row(D_MODEL), _full(w_in_t.shape), _full(w_uq_t.shape), _full(w_ukv.shape),
                  _full(gq.shape), _full(gkv.shape), row(128), ANY],
        out_specs=(row(512), row(256), row(512), row(512), row(512), row(1024), row(1024), row(512), ANY),
        out_shape=(f(512, BF16), f(256, BF16), f(512, BF16), f(512, BF16), f(512, BF16),
                   f(1024, BF16), f(1024, BF16), f(512, BF16),
                   jax.ShapeDtypeStruct(w_out_slots.shape, BF16)),
        input_output_aliases={7: 8},
        scratch_shapes=[pltpu.SemaphoreType.DMA((6,)), pltpu.SemaphoreType.DMA((6,))],
        compiler_params=_cparams(1),
    )(x, w_in_t, w_uq_t, w_ukv, gq, gkv, rope_tab, w_out_slots)


def _attn_fwd(q, k, v, pos_col, pos_row, bounds, nb, S, tq, tk):
    T = q.shape[0]
    nq, nk = S // tq, S // tk
    reps = tk // 128
    hg = HEAD_GROUP

    def body(qmin_ref, qmax_ref, kmin_ref, kmax_ref, q_ref, k_ref, v_ref, pc_ref, pr_ref, o_ref, lse_ref,
             m_sc, l_sc, acc_sc):
        b, i = pl.program_id(0), pl.program_id(2)
        m_sc[...] = jnp.full(m_sc.shape, NEG, F32)
        l_sc[...] = jnp.zeros_like(l_sc)
        acc_sc[...] = jnp.zeros_like(acc_sc)
        q_lo = qmin_ref[b * nq + i]
        q_hi = qmax_ref[b * nq + i]

        def tile(j, masked):
            off = pl.multiple_of(j * tk, tk)
            if masked:
                keep = pc_ref[...] >= pr_ref[pl.ds(j, 1), :]
            logits = []
            for g in range(hg):
                qk = slice(g * HEAD_PAD, (g + 1) * HEAD_PAD)
                s = _dot_nt(q_ref[:, qk], k_ref[pl.ds(off, tk), qk])
                if masked:
                    s = jnp.where(keep, s, NEG)
                logits.append(s)
            probs = []
            for g in range(hg):
                hv = slice(g * 128, (g + 1) * 128)
                s = logits[g]
                m_prev = m_sc[:, hv]
                m_new = jnp.maximum(m_prev, jnp.max(s, axis=1, keepdims=True))
                p = jnp.exp2(s - jnp.concatenate([m_new] * reps, axis=1))
                a = jnp.exp2(m_prev - m_new)
                l_sc[:, hv] = a * l_sc[:, hv] + jnp.sum(p, axis=1, keepdims=True)
                m_sc[:, hv] = m_new
                probs.append((p.astype(BF16), a))
            for g in range(hg):
                hv = slice(g * 128, (g + 1) * 128)
                p, a = probs[g]
                acc_sc[:, hv] = a * acc_sc[:, hv] + _dot(p, v_ref[pl.ds(off, tk), hv])

        def step(j, carry):
            visible = kmin_ref[b * nk + j] <= q_hi
            clear = q_lo >= kmax_ref[b * nk + j]

            @pl.when(jnp.logical_and(visible, clear))
            def _():
                tile(j, False)

            @pl.when(jnp.logical_and(visible, jnp.logical_not(clear)))
            def _():
                tile(j, True)
            return carry

        lax.fori_loop(0, nk, step, 0)
        l = l_sc[...]
        o_ref[...] = acc_sc[...] / l
        lse_ref[...] = m_sc[...] + jnp.log2(l)

    ng = HEADS // hg
    stat = pltpu.VMEM((tq, hg * 128), F32)
    return pl.pallas_call(
        body, name="attn_fwd",
        grid_spec=pltpu.PrefetchScalarGridSpec(
            num_scalar_prefetch=4, grid=(nb, ng, nq),
            in_specs=[pl.BlockSpec((tq, hg * HEAD_PAD), lambda b, h, i, *_: (b * nq + i, h)),
                      pl.BlockSpec((S, hg * HEAD_PAD), lambda b, h, i, *_: (b, h)),
                      pl.BlockSpec((S, hg * 128), lambda b, h, i, *_: (b, h)),
                      pl.BlockSpec((tq, 1), lambda b, h, i, *_: (b * nq + i, 0)),
                      pl.BlockSpec((None, nk, tk), lambda b, h, i, *_: (b, 0, 0))],
            out_specs=(pl.BlockSpec((tq, hg * 128), lambda b, h, i, *_: (b * nq + i, h)),
                       pl.BlockSpec((tq, hg * 128), lambda b, h, i, *_: (b * nq + i, h))),
            scratch_shapes=[stat, stat, stat]),
        out_shape=(jax.ShapeDtypeStruct((T, MLA_W), F32), jax.ShapeDtypeStruct((T, MLA_W), F32)),
        compiler_params=_cparams(3),
    )(*bounds, q, k, v, pos_col, pos_row.reshape(nb, nk, tk))


def _mid(x, tgt, o, ga, u, gb, w_out, pool_w, pool_scale, ln_g, ln_b, S, tm):
    T = x.shape[0]
    tps = S // tm
    hb = tm // HALO

    def body(x_ref, tgt_ref, o_ref, ga_ref, u_ref, uh_ref, gb_ref, wout_ref, pw_ref,
             ps_ref, lng_ref, lnb_ref,
             dz_ref, do_ref, delta_ref, dga_ref, dgb_ref, dpc_ref,
             dwout_ref, dpw_ref, dps_ref, dlng_ref, dlnb_ref, loss_ref):
        i = pl.program_id(0)

        @pl.when(i == 0)
        def _():
            dwout_ref[...] = jnp.zeros_like(dwout_ref)
            dpw_ref[...] = jnp.zeros_like(dpw_ref)
            dps_ref[...] = jnp.zeros_like(dps_ref)
            dlng_ref[...] = jnp.zeros_like(dlng_ref)
            dlnb_ref[...] = jnp.zeros_like(dlnb_ref)
            loss_ref[...] = jnp.zeros_like(loss_ref)

        seq_tile = i % tps
        tpos = seq_tile * tm + lax.broadcasted_iota(jnp.int32, (tm, 1), 0)
        ga_v = ga_ref[...].astype(F32)
        sig_a = jax.nn.sigmoid(ga_v)
        silu_a = ga_v * sig_a
        o_v = o_ref[...]
        ya = o_v * silu_a

        u_v = u_ref[...].astype(F32)
        halo = jnp.where(seq_tile == 0, 0.0, uh_ref[...].astype(F32))
        pooled, cnts, mixed = [], [], []
        for g in range(POOL_G):
            lanes = slice(g * POOL_GD, (g + 1) * POOL_GD)
            w = jnp.concatenate([halo[:, lanes], u_v[:, lanes]], axis=0)
            for st in range(g + 1):
                w = w + pltpu.roll(w, 1 << st, 0)
            cnt = jnp.minimum(tpos + 1, 2 << g).astype(F32)
            pg = (w[HALO:, :] / cnt - u_v[:, lanes]).astype(BF16)
            pooled.append(pg)
            cnts.append(cnt)
            mixed.append(_dot(pg, pw_ref[g]))
        mixed = jnp.concatenate(mixed, axis=1)
        ps = ps_ref[...]
        ybp = mixed * ps
        gb_v = gb_ref[...].astype(F32)
        sig_b = jax.nn.sigmoid(gb_v)
        silu_b = gb_v * sig_b
        yb = ybp * silu_b

        cat = jnp.concatenate([ya, yb], axis=1).astype(BF16)
        z = ALPHA * x_ref[...] + _dot(cat, wout_ref[...])
        mu = jnp.mean(z, axis=-1, keepdims=True)
        zc = z - mu
        rstd = lax.rsqrt(jnp.mean(zc * zc, axis=-1, keepdims=True) + LN_EPS)
        zhat = zc * rstd
        lng = lng_ref[...]
        err = (zhat * lng + lnb_ref[...]) - tgt_ref[...]
        row_loss = jnp.sum(err * err, axis=1, keepdims=True)
        loss_ref[...] += jnp.broadcast_to(jnp.sum(row_loss, axis=0, keepdims=True) * (0.5 / D_MODEL), (1, 128))
        dy = err * (1.0 / D_MODEL)
        dlng_ref[...] += jnp.sum(dy * zhat, axis=0, keepdims=True)
        dlnb_ref[...] += jnp.sum(dy, axis=0, keepdims=True)
        dzh = dy * lng
        dz = rstd * (dzh - jnp.mean(dzh, axis=-1, keepdims=True)
                     - zhat * jnp.mean(dzh * zhat, axis=-1, keepdims=True))
        dz_ref[...] = dz
        dzb = dz.astype(BF16)
        dwout_ref[...] += _dot_tn(cat, dzb)
        dcat = _dot_nt(dzb, wout_ref[...])
        dya = dcat[:, :MLA_W]
        dyb = dcat[:, MLA_W:]

        do = dya * silu_a
        do_ref[...] = do.astype(BF16)
        prod = do * o_v
        for hh in range(HEADS):
            lanes = slice(hh * 128, (hh + 1) * 128)
            delta_ref[:, lanes] = jnp.broadcast_to(jnp.sum(prod[:, lanes], axis=1, keepdims=True), (tm, 128))
        dga_ref[...] = (dya * o_v * (sig_a * (1.0 + ga_v * (1.0 - sig_a)))).astype(BF16)
        dgb_ref[...] = (dyb * ybp * (sig_b * (1.0 + gb_v * (1.0 - sig_b)))).astype(BF16)
        dybp = dyb * silu_b
        dps_ref[...] += jnp.sum(dybp * mixed, axis=0, keepdims=True)
        dmixed = (dybp * ps).astype(BF16)
        for g in range(POOL_G):
            lanes = slice(g * POOL_GD, (g + 1) * POOL_GD)
            dpw_ref[g] += _dot_tn(pooled[g], dmixed[:, lanes])
            dpc_ref[:, lanes] = (_dot_nt(dmixed[:, lanes], pw_ref[g]) / cnts[g]).astype(BF16)

    row = lambda w: pl.BlockSpec((tm, w), lambda i: (i, 0))
    f = lambda w, dt: jax.ShapeDtypeStruct((T, w), dt)
    halo_spec = pl.BlockSpec((HALO, POOL_W), lambda i: (jnp.maximum(i * hb - 1, 0), 0))
    return pl.pallas_call(
        body, name="mid", grid=(T // tm,),
        in_specs=[row(D_MODEL), row(D_MODEL), row(MLA_W), row(MLA_W), row(POOL_W), halo_spec, row(POOL_W),
                  _full(w_out.shape), _full(pool_w.shape),
                  _full(pool_scale.shape), _full(ln_g.shape), _full(ln_b.shape)],
        out_specs=(row(D_MODEL), row(MLA_W), row(MLA_W), row(MLA_W), row(POOL_W), row(POOL_W),
                   _full((D_MODEL, D_MODEL)), _full(pool_w.shape), _full((1, POOL_W)),
                   _full((1, D_MODEL)), _full((1, D_MODEL)), _full((1, 128))),
        out_shape=(f(D_MODEL, F32), f(MLA_W, BF16), f(MLA_W, F32), f(MLA_W, BF16), f(POOL_W, BF16), f(POOL_W, BF16),
                   jax.ShapeDtypeStruct((D_MODEL, D_MODEL), F32), jax.ShapeDtypeStruct(pool_w.shape, F32),
                   jax.ShapeDtypeStruct((1, POOL_W), F32), jax.ShapeDtypeStruct((1, D_MODEL), F32),
                   jax.ShapeDtypeStruct((1, D_MODEL), F32), jax.ShapeDtypeStruct((1, 128), F32)),
        compiler_params=_cparams(1),
    )(x, tgt, o, ga, u, u, gb, w_out, pool_w, pool_scale, ln_g, ln_b)


def _attn_bwd(q, k, v, do, lse, delta, pos_col, pos_row, bounds, early, early_wire, nb, S, tq, tk):
    T = q.shape[0]
    ne = len(early)
    nq, nk = S // tq, S // tk
    reps = tk // 128
    hg = HEAD_GROUP
    ng = HEADS // hg

    def body(qmin_ref, qmax_ref, kmin_ref, kmax_ref, q_ref, k_ref, v_ref, do_ref, lse_ref, dl_ref, pc_ref, pr_ref,
             *rest):
        early_in, rest = rest[:ne], rest[ne:]
        dq_out, dk_out, dv_out = rest[:3]
        early_out, rest = rest[3:3 + ne], rest[3 + ne:]
        dq_ref, dk_ref, dv_ref = rest[:3]
        b, j = pl.program_id(0), pl.program_id(2)
        flat = (b * ng + pl.program_id(1)) * nk + j
        last = nb * ng * nk - 1
        when = [0, min(3, last), min(max(5 * (last + 1) // 8, 3), last), last]
        for at, phase in zip(when, _reduce_phases(early, early_wire, early_in, early_out, rest[3:])):
            pl.when(flat == at)(phase)

        @pl.when(j == 0)
        def _():
            dq_ref[...] = jnp.zeros_like(dq_ref)

        dk_ref[...] = jnp.zeros_like(dk_ref)
        dv_ref[...] = jnp.zeros_like(dv_ref)
        k_lo = kmin_ref[b * nk + j]
        k_hi = kmax_ref[b * nk + j]

        def tile(i, masked):
            rows = pl.ds(pl.multiple_of(i * tq, tq), tq)
            if masked:
                keep = pc_ref[rows, :] >= pr_ref[...]
            stage = []
            for g in range(hg):
                qk = slice(g * HEAD_PAD, (g + 1) * HEAD_PAD)
                hv = slice(g * 128, (g + 1) * 128)
                s = _dot_nt(q_ref[rows, qk], k_ref[:, qk])
                if masked:
                    s = jnp.where(keep, s, NEG)
                stage.append((s, _dot_nt(do_ref[rows, hv], v_ref[:, hv])))
            grads = []
            for g in range(hg):
                hv = slice(g * 128, (g + 1) * 128)
                s, dp = stage[g]
                p = jnp.exp2(s - jnp.concatenate([lse_ref[rows, hv]] * reps, axis=1))
                ds = (p * (dp - jnp.concatenate([dl_ref[rows, hv]] * reps, axis=1))).astype(BF16)
                grads.append((p.astype(BF16), ds))
            for g in range(hg):
                qk = slice(g * HEAD_PAD, (g + 1) * HEAD_PAD)
                hv = slice(g * 128, (g + 1) * 128)
                p, ds = grads[g]
                dv_ref[:, hv] += _dot_tn(p, do_ref[rows, hv])
                dq_ref[rows, qk] += _dot(ds, k_ref[:, qk])
                dk_ref[:, qk] += _dot_tn(ds, q_ref[rows, qk])

        def step(i, carry):
            visible = k_lo <= qmax_ref[b * nq + i]
            clear = qmin_ref[b * nq + i] >= k_hi

            @pl.when(jnp.logical_and(visible, clear))
            def _():
                tile(i, False)

            @pl.when(jnp.logical_and(visible, jnp.logical_not(clear)))
            def _():
                tile(i, True)
            return carry

        lax.fori_loop(0, nq, step, 0)
        dk_out[...] = dk_ref[...].astype(BF16)
        dv_out[...] = dv_ref[...].astype(BF16)

        @pl.when(j == nk - 1)
        def _():
            dq_out[...] = dq_ref[...].astype(BF16)

    seq = lambda w: pl.BlockSpec((S, w), lambda b, h, j, *_: (b, h))
    blk = lambda w: pl.BlockSpec((tk, w), lambda b, h, j, *_: (b * nk + j, h))
    outs = pl.pallas_call(
        body, name="attn_bwd",
        grid_spec=pltpu.PrefetchScalarGridSpec(
            num_scalar_prefetch=4, grid=(nb, ng, nk),
            in_specs=[seq(hg * HEAD_PAD), blk(hg * HEAD_PAD), blk(hg * 128),
                      seq(hg * 128), seq(hg * 128), seq(hg * 128),
                      pl.BlockSpec((S, 1), lambda b, h, j, *_: (b, 0)),
                      pl.BlockSpec((None, 1, tk), lambda b, h, j, *_: (b, 0, j))] + [ANY] * ne,
            out_specs=(seq(hg * HEAD_PAD), blk(hg * HEAD_PAD), blk(hg * 128)) + (ANY,) * ne,
            scratch_shapes=[pltpu.VMEM((S, hg * HEAD_PAD), F32), pltpu.VMEM((tk, hg * HEAD_PAD), F32),
                            pltpu.VMEM((tk, hg * 128), F32)] + _reduce_scratch(early, early_wire)),
        out_shape=(jax.ShapeDtypeStruct((T, HEADS * HEAD_PAD), BF16),
                   jax.ShapeDtypeStruct((T, HEADS * HEAD_PAD), BF16),
                   jax.ShapeDtypeStruct((T, MLA_W), BF16)) + tuple(jax.ShapeDtypeStruct(g.shape[1:], F32) for g in early),
        compiler_params=_cparams(3),
    )(*bounds, q, k, v, do, lse, delta, pos_col, pos_row, *early)
    return outs[:3], outs[3:]


def _bwd_proj(dq, dk, dv, xq, xkv, x, dz, dga, dgb, dpc, rope_tab, w_uq_t, w_ukv, w_in_t, gq, gkv, S, tm):
    T = x.shape[0]
    tps = S // tm
    hb = tm // HALO
    n_tiles = T // tm

    def body(dq_ref, dk_ref, dv_ref, xq_ref, xkv_ref, x_ref, dz_ref, dga_ref, dgb_ref, dpc_ref, dph_ref,
             tab_ref, wuq_ref, wukv_ref, win_ref, gq_ref, gkv_ref,
             dx_ref, dwin_hbm, dwuq_hbm, dwukv_hbm, dgq_ref, dgkv_ref,
             acc_win, acc_wuq, acc_wukv, dh_sc):
        i = pl.program_id(0)

        @pl.when(i == 0)
        def _():
            acc_win[...] = jnp.zeros_like(acc_win)
            acc_wuq[...] = jnp.zeros_like(acc_wuq)
            acc_wukv[...] = jnp.zeros_like(acc_wukv)
            dgq_ref[...] = jnp.zeros_like(dgq_ref)
            dgkv_ref[...] = jnp.zeros_like(dgkv_ref)
            dh_sc[...] = jnp.zeros_like(dh_sc)

        dh_prev = dh_sc[...]
        dx_ref[...] = ALPHA * dz_ref[...] + _dot(dh_prev, win_ref[...])
        acc_win[...] += _dot_tn(dh_prev, x_ref[...].astype(BF16))

        live = jnp.where(i < n_tiles, 1.0, 0.0)
        c, sa, sb = _expand_rope_table(tab_ref[...])
        dq_v = dq_ref[...].astype(F32) * (SCALE * live)
        dk_v = dk_ref[...].astype(F32) * (LN2 * live)
        dv_v = dv_ref[...].astype(F32) * live
        dq_parts, dkv_parts = [], []
        dkr = jnp.zeros((tm, 128), F32)
        for hh in range(HEADS):
            b0 = hh * HEAD_PAD
            dq_parts.append(dq_v[:, b0:b0 + 128].astype(BF16))
            dq_parts.append(_rope(dq_v[:, b0 + 128:b0 + 256], c, sa, sb, -1.0).astype(BF16))
            dkv_parts.append(dk_v[:, b0:b0 + 128].astype(BF16))
            dkv_parts.append(dv_v[:, hh * 128:(hh + 1) * 128].astype(BF16))
            dkr = dkr + dk_v[:, b0 + 128:b0 + 256]
        dqp = jnp.concatenate(dq_parts, axis=1)
        dkvp = jnp.concatenate(dkv_parts, axis=1)
        dkrr = _rope(dkr, c, sa, sb, -1.0)

        def rms_bwd(xv, g, dyn, dg_ref):
            r = lax.rsqrt(jnp.mean(xv * xv, axis=-1, keepdims=True) + RMS_EPS)
            xhat = xv * r
            dg_ref[...] += jnp.sum(dyn * xhat, axis=0, keepdims=True)
            dxh = dyn * g
            return r * (dxh - xhat * jnp.mean(dxh * xhat, axis=-1, keepdims=True))

        xq_v = xq_ref[...].astype(F32)
        gq_v = gq_ref[...]
        rq = lax.rsqrt(jnp.mean(xq_v * xq_v, axis=-1, keepdims=True) + RMS_EPS)
        acc_wuq[...] += _dot_tn(dqp, ((xq_v * rq) * gq_v).astype(BF16))
        dxq = rms_bwd(xq_v, gq_v, _dot(dqp, wuq_ref[...]), dgq_ref)

        xkv_v = xkv_ref[...].astype(F32)
        gkv_v = gkv_ref[...]
        rkv = lax.rsqrt(jnp.mean(xkv_v * xkv_v, axis=-1, keepdims=True) + RMS_EPS)
        acc_wukv[...] += _dot_tn(((xkv_v * rkv) * gkv_v).astype(BF16), dkvp)
        dxkv = rms_bwd(xkv_v, gkv_v, _dot_nt(dkvp, wukv_ref[...]), dgkv_ref)

        seq_tile = i % tps
        tpos = seq_tile * tm + lax.broadcasted_iota(jnp.int32, (tm, 1), 0)
        dpc_v = dpc_ref[...].astype(F32)
        halo = jnp.where(seq_tile == tps - 1, 0.0, dph_ref[...].astype(F32))
        n = tm + HALO
        du = []
        for g in range(POOL_G):
            lanes = slice(g * POOL_GD, (g + 1) * POOL_GD)
            f = jnp.concatenate([dpc_v[:, lanes], halo[:, lanes]], axis=0)
            for st in range(g + 1):
                f = f + pltpu.roll(f, n - (1 << st), 0)
            cnt = jnp.minimum(tpos + 1, 2 << g).astype(F32)
            du.append((f[:tm, :] - dpc_v[:, lanes] * cnt).astype(BF16))

        dh_sc[...] = jnp.concatenate([dxq.astype(BF16), dxkv.astype(BF16), dkrr.astype(BF16), dga_ref[...]]
                                     + du + [dgb_ref[...]], axis=1)

        @pl.when(i == n_tiles)
        def _():
            pltpu.sync_copy(acc_win.at[pl.ds(0, ROPE_END)], dwin_hbm.at[pl.ds(0, ROPE_END)])
            pltpu.sync_copy(acc_win.at[pl.ds(COL_GA, IN_EXT - COL_GA)], dwin_hbm.at[pl.ds(ROPE_END, IN_W - ROPE_END)])
            for hh in range(HEADS):
                pltpu.sync_copy(acc_wuq.at[pl.ds(hh * HEAD_PAD, NOPE + ROPE)], dwuq_hbm.at[hh])
            pltpu.sync_copy(acc_wukv, dwukv_hbm)

    cur = lambda w: pl.BlockSpec((tm, w), lambda i: (jnp.minimum(i, n_tiles - 1), 0))
    prev = lambda w: pl.BlockSpec((tm, w), lambda i: (jnp.maximum(i - 1, 0), 0))
    halo_spec = pl.BlockSpec((HALO, POOL_W), lambda i: (jnp.minimum((i + 1) * hb, T // HALO - 1), 0))
    return pl.pallas_call(
        body, name="bwd_proj", grid=(n_tiles + 1,),
        in_specs=[cur(1024), cur(1024), cur(512), cur(512), cur(256), prev(D_MODEL), prev(D_MODEL),
                  cur(512), cur(512), cur(512), halo_spec, cur(128),
                  _full(w_uq_t.shape), _full(w_ukv.shape), _full(w_in_t.shape), _full(gq.shape), _full(gkv.shape)],
        out_specs=(prev(D_MODEL), ANY, ANY, ANY, _full((1, Q_LORA)), _full((1, KV_LORA))),
        out_shape=(jax.ShapeDtypeStruct((T, D_MODEL), F32),
                   jax.ShapeDtypeStruct((IN_W, D_MODEL), F32),
                   jax.ShapeDtypeStruct((HEADS, NOPE + ROPE, Q_LORA), F32),
                   jax.ShapeDtypeStruct((KV_LORA, 1024), F32),
                   jax.ShapeDtypeStruct((1, Q_LORA), F32), jax.ShapeDtypeStruct((1, KV_LORA), F32)),
        scratch_shapes=[pltpu.VMEM((IN_EXT, D_MODEL), F32), pltpu.VMEM((HEADS * HEAD_PAD, Q_LORA), F32),
                        pltpu.VMEM((KV_LORA, 1024), F32), pltpu.VMEM((tm, IN_EXT), BF16)],
        compiler_params=_cparams(1),
    )(dq, dk, dv, xq, xkv, x, dz, dga, dgb, dpc, dpc, rope_tab, w_uq_t, w_ukv, w_in_t, gq, gkv)


def kernel(x, positions, w_in, q_norm_g, w_uq, kv_norm_g, w_ukv, pool_w, pool_scale, w_out, ln_g, ln_b, loss_target, m_w_in, m_q_norm_g, m_w_uq, m_kv_norm_g, m_w_ukv, m_pool_w, m_pool_scale, m_w_out, m_ln_g, m_ln_b, v_w_in, v_q_norm_g, v_w_uq, v_kv_norm_g, v_w_ukv, v_pool_w, v_pool_scale, v_w_out, v_ln_g, v_ln_b):
    nb, S, _ = x.shape
    T = nb * S
    tm = min(256, S)
    tq = min(512, S)
    tk = min(512, S)
    assert S % tm == 0 and tm % HALO == 0 and S % tq == 0 and S % tk == 0

    cx, cy, cc = lax.axis_index("x"), lax.axis_index("y"), lax.axis_index("c")
    me = 2 * cx + cy

    half = ROPE // 2
    inv_freq = ROPE_THETA ** (-jnp.arange(half, dtype=F32) / half)
    freq_row = jnp.concatenate([inv_freq, inv_freq, jnp.zeros((2 * half,), F32)]).reshape(1, 128)
    pos_col = positions.reshape(T, 1)
    pos_row = positions.reshape(nb, 1, S)
    pos_q = positions.reshape(nb, S // tq, tq)
    pos_k = positions.reshape(nb, S // tk, tk)
    bounds = (jnp.min(pos_q, axis=2).reshape(-1), jnp.max(pos_q, axis=2).reshape(-1),
              jnp.min(pos_k, axis=2).reshape(-1), jnp.max(pos_k, axis=2).reshape(-1))

    def own_slot(w, slot_rows):
        blk = jnp.pad(w.astype(BF16), ((0, slot_rows - w.shape[0]), (0, 0)))
        return lax.dynamic_update_slice(jnp.zeros((N_CHIPS,) + blk.shape, BF16), blk[None], (me, 0, 0))

    (w_in_g, w_uq_g, w_ukv_g), rope_tab = _weight_gather(
        [own_slot(w_in.T, IN_SHARD), own_slot(w_uq.T, HEAD_PAD), own_slot(w_ukv, KV_LORA)], (IN_SHARD, NOPE + ROPE, KV_LORA),
        pos_col, freq_row)
    w_in_f = w_in_g.reshape(IN_W, D_MODEL)
    w_in_t = jnp.concatenate([w_in_f[:ROPE_END], jnp.zeros((COL_GA - ROPE_END, D_MODEL), BF16), w_in_f[ROPE_END:]], axis=0)
    w_uq_t = w_uq_g.reshape(HEADS * HEAD_PAD, Q_LORA)
    w_ukv_f = w_ukv_g.transpose(1, 0, 2).reshape(KV_LORA, 1024)
    pool_w_b = pool_w.astype(BF16)
    gq2 = q_norm_g.reshape(1, Q_LORA)
    gkv2 = kv_norm_g.reshape(1, KV_LORA)
    ps2 = pool_scale.reshape(1, POOL_W)

    xf = x.reshape(T, D_MODEL)
    tgt = loss_target.reshape(T, D_MODEL)

    xq, xkv, ga, u, gb, q, k, v, w_out_g = _fwd_proj(
        xf, w_in_t, w_uq_t, w_ukv_f, gq2, gkv2, rope_tab, own_slot(w_out, 256), tm)
    w_out_f = w_out_g.reshape(D_MODEL, D_MODEL)
    o, lse = _attn_fwd(q, k, v, pos_col, pos_row, bounds, nb, S, tq, tk)

    (dz, do, delta, dga, dgb, dpc, d_w_out, d_pool_w, d_pool_scale, d_ln_g, d_ln_b, loss_part) = _mid(
        xf, tgt, o, ga, u, gb, w_out_f, pool_w_b, ps2, ln_g, ln_b, S, tm)

    wide = lambda a: jnp.pad(a.reshape(1, -1), ((0, 0), (0, D_MODEL - a.size)))
    blank = lambda r: jnp.zeros((r, D_MODEL), F32)
    to_all = lambda a: jnp.broadcast_to(a[None], (N_CHIPS,) + a.shape)
    vec_early = jnp.concatenate([d_ln_g, d_ln_b, wide(d_pool_scale), blank(6), wide(loss_part), blank(VEC_ROWS - 10)], axis=0)
    early = [d_w_out.reshape(N_CHIPS, 256, D_MODEL), to_all(d_pool_w.reshape(-1, D_MODEL)), to_all(vec_early)]
    (dq, dk, dv), (g_out, pw_sum, vec_early_sum) = _attn_bwd(
        q, k, v, do, lse, delta, pos_col, pos_row, bounds, early, (BF16, F32, F32), nb, S, tq, tk)
    dx, d_w_in_t, d_w_uq_t, d_w_ukv, d_gq, d_gkv = _bwd_proj(
        dq, dk, dv, xq, xkv, xf, dz, dga, dgb, dpc, rope_tab, w_uq_t, w_ukv_f, w_in_t, gq2, gkv2, S, tm)
    grad_x = dx.reshape(nb, S, D_MODEL)

    g_in = d_w_in_t.reshape(N_CHIPS, IN_SHARD, D_MODEL)
    g_uq = d_w_uq_t
    g_ukv = d_w_ukv.reshape(KV_LORA, N_CHIPS, 256).transpose(1, 0, 2)
    vec_late = jnp.concatenate([blank(3), wide(d_gq), blank(4), wide(d_gkv), blank(VEC_ROWS - 9)], axis=0)
    g_in, g_uq, g_ukv, vec_late_sum = _grad_reduce([g_in, g_uq, g_ukv, to_all(vec_late)], (BF16, BF16, BF16, F32))
    g_big = [g_in, g_uq, g_ukv, g_out]
    pw_sum = pw_sum.reshape(POOL_G * POOL_GD, POOL_GD)
    vec_sum = vec_early_sum + vec_late_sum

    big = _adamw_big(g_big, [w_in.T, w_uq.T, w_ukv, w_out], [m_w_in.T, m_w_uq.T, m_w_ukv, m_w_out],
                     [v_w_in.T, v_w_uq.T, v_w_ukv, v_w_out])
    two_d = lambda a: a.reshape(-1, a.shape[-1])
    small_names = lambda pw, lg, lb, ps, gq, gkv: [two_d(pw), lg, lb, ps.reshape(1, -1), gq.reshape(1, -1), gkv.reshape(1, -1)]
    small, loss_row = _adamw_small(
        pw_sum, vec_sum,
        small_names(pool_w, ln_g, ln_b, pool_scale, q_norm_g, kv_norm_g),
        small_names(m_pool_w, m_ln_g, m_ln_b, m_pool_scale, m_q_norm_g, m_kv_norm_g),
        small_names(v_pool_w, v_ln_g, v_ln_b, v_pool_scale, v_q_norm_g, v_kv_norm_g))
    loss = loss_row[0, 0]

    def leaves(kind):
        b = [g_big[t] if kind == 0 else big[t][kind - 1] for t in range(N_BIG)]
        b = [b[0].T, b[1].T, b[2], b[3]]
        s = [small[t][kind] for t in range(6)]
        return (b[0], s[4].reshape(Q_LORA), b[1], s[5].reshape(KV_LORA), b[2],
                s[0].reshape(POOL_G, POOL_GD, POOL_GD), s[3].reshape(POOL_W), b[3], s[1], s[2])

    return (loss, grad_x) + leaves(0) + leaves(1) + leaves(2) + leaves(3)
```

```python
import jax
import jax.numpy as jnp
from jax import lax
from jax.experimental import pallas as pl
from jax.experimental.pallas import tpu as pltpu

F32 = jnp.float32
BF16 = jnp.bfloat16
MESH = pl.DeviceIdType.MESH

HEADS = 4
NOPE = 128
ROPE = 64
HEAD_PAD = 256
Q_LORA = 512
KV_LORA = 256
MLA_W = 512
POOL_W = 512
POOL_G = 4
POOL_GD = 128
D_MODEL = 1024
IN_W = 2368
IN_EXT = 2432
COL_KV, COL_KR, COL_GA, COL_U, COL_GB = 512, 768, 896, 1408, 1920
ROPE_END = COL_KR + 64
IN_SHARD = IN_W // 4
ROPE_THETA = 10000.0
RMS_EPS = 1e-6
LN_EPS = 1e-5
ALPHA = 2.0 ** 0.25
SCALE = 192.0 ** -0.5
LOG2E = 1.4426950408889634
LN2 = 0.6931471805599453
QSCALE = SCALE * LOG2E
NEG = float(jnp.finfo(jnp.float32).min)
HEAD_GROUP = 2
HALO = 16

ADAM_LR = 0.001
ADAM_B1 = 0.9
ADAM_B2 = 0.999
ADAM_EPS = 1e-08
ADAM_WD = 0.01
ADAM_STEP = 10

N_CHIPS = 4
N_BIG = 4
VEC_ROWS = 16

VMEM_LIMIT = 56 * 1024 * 1024


def _cparams(n_grid_dims=0, **kw):
    sem = ("arbitrary",) * n_grid_dims if n_grid_dims else None
    return pltpu.CompilerParams(dimension_semantics=sem, vmem_limit_bytes=VMEM_LIMIT, **kw)


def _full(shape):
    nd = len(shape)
    return pl.BlockSpec(shape, lambda *_: (0,) * nd)


def _dot(a, b):
    return jnp.dot(a, b, preferred_element_type=F32)


def _dot_nt(a, b):
    return lax.dot_general(a, b, (((1,), (1,)), ((), ())), preferred_element_type=F32)


def _dot_tn(a, b):
    return lax.dot_general(a, b, (((0,), (0,)), ((), ())), preferred_element_type=F32)


def _rope_table(pos_col, freq_row):
    lane = lax.broadcasted_iota(jnp.int32, (1, 128), 1)
    ang = pos_col.astype(F32) * freq_row
    return jnp.where(lane < 32, jnp.cos(ang), jnp.where(lane < 64, jnp.sin(ang), 0.0))


def _expand_rope_table(tab):
    lane = lax.broadcasted_iota(jnp.int32, (1, 128), 1)
    second = jnp.logical_and(lane >= 32, lane < 64)
    c = jnp.where(lane < 32, tab, jnp.where(second, pltpu.roll(tab, 32, 1), 0.0))
    sa = jnp.where(lane < 32, pltpu.roll(tab, 96, 1), 0.0)
    sb = jnp.where(second, tab, 0.0)
    return c, sa, sb


def _rope(g, c, sa, sb, sign):
    return g * c + sign * (pltpu.roll(g, 32, 1) * sb - pltpu.roll(g, 96, 1) * sa)


def _place():
    x, y, c = lax.axis_index("x"), lax.axis_index("y"), lax.axis_index("c")
    chips = [(1 - x, y), (x, 1 - y), (1 - x, 1 - y)]
    return x, y, c, chips


ANY = pl.BlockSpec(memory_space=pl.ANY)


ROPE_CHUNK = 2048


def _weight_gather(slots, valid_rows, pos_col, freq_row):
    n = len(slots)
    T = pos_col.shape[0]
    chunk = min(ROPE_CHUNK, T)
    assert T % chunk == 0

    def body(*refs):
        pos_hbm, freq_ref = refs[n:n + 2]
        outs = refs[n + 2:2 * n + 2]
        tab_hbm = refs[2 * n + 2]
        send_sems, recv_sems, pos_buf, tab_buf = refs[2 * n + 3:]
        x, y, c, chips = _place()
        me, sibling = 2 * x + y, (x, y, 1 - c)
        near = [chips[0] + (c,), chips[1] + (c,)]
        idx = [2 * cx + cy for cx, cy in chips]

        def copy(t, k, chip_idx, half, quarter, to):
            hc, rows = slots[t].shape[2] // 2, valid_rows[t]
            row0, n_rows, col0, n_cols = 0, rows, half * hc, hc
            if quarter is not None and (hc // 2) % 128 == 0:
                col0, n_cols = col0 + quarter * (hc // 2), hc // 2
            elif quarter is not None:
                row0, n_rows = quarter * (rows // 2), rows // 2
            blk = outs[t].at[chip_idx, pl.ds(row0, n_rows), pl.ds(col0, n_cols)]
            return pltpu.make_async_remote_copy(
                src_ref=blk, dst_ref=blk, send_sem=send_sems.at[8 * t + k], recv_sem=recv_sems.at[8 * t + k],
                device_id=to, device_id_type=MESH)

        def own_half(t, a):
            return copy(t, a, me, c, None, near[a])

        def from_near(t, a, to_sibling=False, half=None):
            half = c if half is None else half
            return copy(t, 2 + a if to_sibling else a, idx[a], half, None, sibling if to_sibling else (x, y, c))

        def quarter_out(t, a):
            return copy(t, 4 + a, idx[a], c, a, near[1 - a])

        def quarter_in(t, a, to_sibling=False, half=None):
            half = c if half is None else half
            return copy(t, 6 + a if to_sibling else 4 + a, idx[2], half, a, sibling if to_sibling else (x, y, c))

        for t in range(n):
            for a in range(2):
                own_half(t, a).start()

        def table_chunk(r, carry):
            rows = pl.ds(pl.multiple_of(r * chunk, chunk), chunk)
            pltpu.sync_copy(pos_hbm.at[rows], pos_buf)
            tab_buf[...] = _rope_table(pos_buf[...], freq_ref[...])
            pltpu.sync_copy(tab_buf, tab_hbm.at[rows])
            return carry

        lax.fori_loop(0, T // chunk, table_chunk, 0)
        for a in range(2):
            for t in range(n):
                from_near(t, a).wait_recv()
                quarter_out(t, a).start()
                from_near(t, a, to_sibling=True).start()
        for a in range(2):
            for t in range(n):
                quarter_in(t, a).wait_recv()
                quarter_in(t, a, to_sibling=True).start()
        for a in range(2):
            for t in range(n):
                from_near(t, a, to_sibling=True, half=1 - c).wait_recv()
                quarter_in(t, a, to_sibling=True, half=1 - c).wait_recv()
        for a in range(2):
            for t in range(n):
                own_half(t, a).wait_send()
                quarter_out(t, a).wait_send()
                from_near(t, a, to_sibling=True).wait_send()
                quarter_in(t, a, to_sibling=True).wait_send()

    outs = pl.pallas_call(
        body, name="weight_gather",
        out_shape=tuple(jax.ShapeDtypeStruct(a.shape, a.dtype) for a in slots) + (jax.ShapeDtypeStruct((T, 128), F32),),
        in_specs=[ANY] * n + [ANY, pl.BlockSpec(memory_space=pltpu.VMEM)], out_specs=(ANY,) * (n + 1),
        input_output_aliases={t: t for t in range(n)},
        scratch_shapes=[pltpu.SemaphoreType.DMA((8 * n,)), pltpu.SemaphoreType.DMA((8 * n,)),
                        pltpu.VMEM((chunk, 1), jnp.int32), pltpu.VMEM((chunk, 128), F32)],
    )(*slots, pos_col, freq_row)
    return outs[:n], outs[n]


def _reduce_scratch(gs, wire_dtypes):
    n = len(gs)
    half = [(g.shape[1], g.shape[2] // 2) for g in gs]
    return ([pltpu.VMEM((4,) + h, F32) for h in half] + [pltpu.VMEM((4,) + h, F32) for h in half]
            + [pltpu.VMEM((3,) + h, w) for h, w in zip(half, wire_dtypes)]
            + [pltpu.VMEM((3,) + h, w) for h, w in zip(half, wire_dtypes)]
            + [pltpu.VMEM(h, F32) for h in half]
            + [pltpu.SemaphoreType.DMA((4 * n,)), pltpu.SemaphoreType.DMA((4 * n,)),
               pltpu.SemaphoreType.DMA((3 * n,)), pltpu.SemaphoreType.DMA((3 * n,)),
               pltpu.SemaphoreType.DMA((n,)), pltpu.SemaphoreType.DMA((n,)),
               pltpu.SemaphoreType.DMA((4 * n,)), pltpu.SemaphoreType.DMA((n,))])


def _reduce_phases(gs, wire_dtypes, g_refs, out_refs, scr):
    n = len(gs)
    hcs = [g.shape[2] // 2 for g in gs]
    own, sib, wire, got, fin = (scr[i * n:(i + 1) * n] for i in range(5))
    d2d_send, d2d_recv, ici_send, ici_recv, fin_send, fin_recv, loc_in, loc_out = scr[5 * n:]

    def place():
        x, y, c, chips = _place()
        return c, chips, (x, y, 1 - c), [2 * cx + cy for cx, cy in chips] + [2 * x + y]

    def remote(src, dst, send, recv, to):
        return pltpu.make_async_remote_copy(src_ref=src, dst_ref=dst, send_sem=send, recv_sem=recv,
                                            device_id=to, device_id_type=MESH)

    def block(ref, t, half, lead=None):
        cols = pl.ds(half * hcs[t], hcs[t])
        rows = pl.ds(0, gs[t].shape[1])
        return ref.at[rows, cols] if lead is None else ref.at[lead, rows, cols]

    def source(t, j, half):
        dests = place()[3]
        return block(g_refs[t], t, half, dests[j] if gs[t].shape[0] == N_CHIPS else 0)

    def load(t, j):
        return pltpu.make_async_copy(source(t, j, place()[0]), own[t].at[j], loc_in.at[4 * t + j])

    def d2d(t, j):
        c, _, sibling, _ = place()
        return remote(source(t, j, 1 - c), sib[t].at[j], d2d_send.at[4 * t + j], d2d_recv.at[4 * t + j], sibling)

    def ici(t, j):
        c, chips, _, _ = place()
        return remote(wire[t].at[j], got[t].at[j], ici_send.at[3 * t + j], ici_recv.at[3 * t + j], chips[j] + (c,))

    def store(t):
        c = place()[0]
        return pltpu.make_async_copy(fin[t], block(out_refs[t], t, c), loc_out.at[t])

    def final(t, half_of):
        c, _, sibling, _ = place()
        return remote(fin[t], block(out_refs[t], t, c if half_of == "mine" else 1 - c),
                      fin_send.at[t], fin_recv.at[t], sibling)

    def start():
        for j in range(4):
            for t in range(n):
                load(t, j).start()
                d2d(t, j).start()

    def exchange():
        for j in range(3):
            for t in range(n):
                load(t, j).wait()
                d2d(t, j).wait_recv()
                wire[t][j] = (own[t][j] + sib[t][j]).astype(wire_dtypes[t])
                ici(t, j).start()

    def finish():
        for t in range(n):
            load(t, 3).wait()
            d2d(t, 3).wait_recv()
            for j in range(3):
                ici(t, j).wait_recv()
            fin[t][...] = (((own[t][3] + sib[t][3]) + got[t][0].astype(F32))
                           + (got[t][1].astype(F32) + got[t][2].astype(F32)))
            store(t).start()
            final(t, "mine").start()

    def drain():
        for t in range(n):
            final(t, "theirs").wait_recv()
        for t in range(n):
            for j in range(4):
                d2d(t, j).wait_send()
            for j in range(3):
                ici(t, j).wait_send()
            final(t, "mine").wait_send()
            store(t).wait()

    return start, exchange, finish, drain


def _grad_reduce(gs, wire_dtypes):
    n = len(gs)

    def body(*refs):
        for phase in _reduce_phases(gs, wire_dtypes, refs[:n], refs[n:2 * n], refs[2 * n:]):
            phase()

    return pl.pallas_call(
        body, name="grad_reduce",
        out_shape=tuple(jax.ShapeDtypeStruct(g.shape[1:], F32) for g in gs),
        in_specs=[ANY] * n, out_specs=(ANY,) * n, scratch_shapes=_reduce_scratch(gs, wire_dtypes),
        compiler_params=_cparams(),
    )(*gs)


def _adamw_math(g, w, m, v):
    nm = ADAM_B1 * m + (1.0 - ADAM_B1) * g
    nv = ADAM_B2 * v + (1.0 - ADAM_B2) * (g * g)
    m_hat = nm / (1.0 - ADAM_B1 ** ADAM_STEP)
    v_hat = nv / (1.0 - ADAM_B2 ** ADAM_STEP)
    return -ADAM_LR * (m_hat / (jnp.sqrt(v_hat) + ADAM_EPS) + ADAM_WD * w), nm, nv


ADAM_STEPS = 8


def _adamw_big(gs, ws, ms, vs):
    n = len(gs)

    def body(*refs):
        for t in range(n):
            d, nm, nv = _adamw_math(refs[t][...], refs[n + t][...], refs[2 * n + t][...], refs[3 * n + t][...])
            refs[4 * n + 3 * t][...] = d
            refs[4 * n + 3 * t + 1][...] = nm
            refs[4 * n + 3 * t + 2][...] = nv

    def tile_spec(shape):
        rows, cols = shape
        if rows % (8 * ADAM_STEPS) == 0:
            return pl.BlockSpec((rows // ADAM_STEPS, cols), lambda i: (i, 0))
        return pl.BlockSpec((rows, cols // ADAM_STEPS), lambda i: (0, i))

    specs = [tile_spec(g.shape) for g in gs]
    out_specs, out_shape = [], []
    for t in range(n):
        out_specs += [specs[t]] * 3
        out_shape += [jax.ShapeDtypeStruct(gs[t].shape, F32)] * 3
    outs = pl.pallas_call(
        body, name="adamw_big", grid=(ADAM_STEPS,),
        in_specs=specs * 4, out_specs=tuple(out_specs), out_shape=tuple(out_shape),
        compiler_params=_cparams(1),
    )(*gs, *ws, *ms, *vs)
    return [outs[3 * t: 3 * t + 3] for t in range(n)]


def _adamw_small(pw_sum, vec_sum, ws, ms, vs):
    rows = (None, 0, 1, 2, 3, 8)
    n = len(ws)

    def body(pw_ref, vec_ref, *refs):
        outs = refs[3 * n:]
        for t in range(n):
            w_ref, m_ref, v_ref = refs[t], refs[n + t], refs[2 * n + t]
            if rows[t] is None:
                g = pw_ref[...]
            else:
                g = vec_ref[rows[t]:rows[t] + 1, 0:w_ref.shape[1]]
            d, nm, nv = _adamw_math(g, w_ref[...], m_ref[...], v_ref[...])
            outs[4 * t][...] = g
            outs[4 * t + 1][...] = d
            outs[4 * t + 2][...] = nm
            outs[4 * t + 3][...] = nv
        outs[4 * n][...] = vec_ref[9:10, 0:128]

    vm = pl.BlockSpec(memory_space=pltpu.VMEM)
    out_shape = []
    for w in ws:
        out_shape += [jax.ShapeDtypeStruct(w.shape, F32)] * 4
    out_shape.append(jax.ShapeDtypeStruct((1, 128), F32))
    outs = pl.pallas_call(
        body, name="adamw_small", in_specs=[vm] * (2 + 3 * n), out_specs=(vm,) * (4 * n + 1),
        out_shape=tuple(out_shape),
    )(pw_sum, vec_sum, *ws, *ms, *vs)
    return [outs[4 * t: 4 * t + 4] for t in range(n)], outs[4 * n]


def _fwd_proj(x, w_in_t, w_uq_t, w_ukv, gq, gkv, rope_tab, w_out_slots, tm):
    T = x.shape[0]
    n_steps = T // tm
    fwd_step = n_steps // 2

    def body(x_ref, win_ref, wuq_ref, wukv_ref, gq_ref, gkv_ref, tab_ref, wo_in,
             xq_ref, xkv_ref, ga_ref, u_ref, gb_ref, q_ref, k_ref, v_ref, wo_ref, send_sems, recv_sems):
        i = pl.program_id(0)
        px, py, pc, chips = _place()
        hc = D_MODEL // 2

        def wo_copy(k, chip_idx, half, to):
            blk = wo_ref.at[chip_idx, pl.ds(0, 256), pl.ds(half * hc, hc)]
            return pltpu.make_async_remote_copy(src_ref=blk, dst_ref=blk, send_sem=send_sems.at[k],
                                                recv_sem=recv_sems.at[k], device_id=to, device_id_type=MESH)

        @pl.when(i == 0)
        def _():
            for j, (cx, cy) in enumerate(chips):
                wo_copy(j, 2 * px + py, pc, (cx, cy, pc)).start()

        @pl.when(i == fwd_step)
        def _():
            for j, (cx, cy) in enumerate(chips):
                wo_copy(j, 2 * cx + cy, pc, (px, py, pc)).wait_recv()
                wo_copy(3 + j, 2 * cx + cy, pc, (px, py, 1 - pc)).start()

        @pl.when(i == n_steps - 1)
        def _():
            for j, (cx, cy) in enumerate(chips):
                wo_copy(3 + j, 2 * cx + cy, 1 - pc, (px, py, pc)).wait_recv()
            for j, (cx, cy) in enumerate(chips):
                wo_copy(j, 2 * px + py, pc, (cx, cy, pc)).wait_send()
                wo_copy(3 + j, 2 * cx + cy, pc, (px, py, 1 - pc)).wait_send()

        h = _dot_nt(x_ref[...].astype(BF16), win_ref[...])
        xq = h[:, 0:COL_KV]
        xkv = h[:, COL_KV:COL_KR]
        xq_ref[...] = xq.astype(BF16)
        xkv_ref[...] = xkv.astype(BF16)
        ga_ref[...] = h[:, COL_GA:COL_U].astype(BF16)
        u_ref[...] = h[:, COL_U:COL_GB].astype(BF16)
        gb_ref[...] = h[:, COL_GB:IN_EXT].astype(BF16)
        c, sa, sb = _expand_rope_table(tab_ref[...])
        rq = lax.rsqrt(jnp.mean(xq * xq, axis=-1, keepdims=True) + RMS_EPS)
        q = _dot_nt(((xq * rq) * gq_ref[...]).astype(BF16), wuq_ref[...]) * QSCALE
        rkv = lax.rsqrt(jnp.mean(xkv * xkv, axis=-1, keepdims=True) + RMS_EPS)
        kv = _dot(((xkv * rkv) * gkv_ref[...]).astype(BF16), wukv_ref[...])
        kr = _rope(h[:, COL_KR:COL_GA], c, sa, sb, 1.0).astype(BF16)
        for hh in range(HEADS):
            b0 = hh * HEAD_PAD
            q_ref[:, b0:b0 + 128] = q[:, b0:b0 + 128].astype(BF16)
            q_ref[:, b0 + 128:b0 + 256] = _rope(q[:, b0 + 128:b0 + 256], c, sa, sb, 1.0).astype(BF16)
            k_ref[:, b0:b0 + 128] = kv[:, b0:b0 + 128].astype(BF16)
            k_ref[:, b0 + 128:b0 + 256] = kr
            v_ref[:, hh * 128:(hh + 1) * 128] = kv[:, b0 + 128:b0 + 256].astype(BF16)

    row = lambda w: pl.BlockSpec((tm, w), lambda i: (i, 0))
    f = lambda w, dt: jax.ShapeDtypeStruct((T, w), dt)
    return pl.pallas_call(
        body, name="fwd_proj", grid=(n_steps,),
        in_specs=[row(D_MODEL), _full(w_in_t.shape), _full(w_uq_t.shape), _full(w_ukv.shape),
                  _full(gq.shape), _full(gkv.shape), row(128), ANY],
        out_specs=(row(512), row(256), row(512), row(512), row(512), row(1024), row(1024), row(512), ANY),
        out_shape=(f(512, BF16), f(256, BF16), f(512, BF16), f(512, BF16), f(512, BF16),
                   f(1024, BF16), f(1024, BF16), f(512, BF16),
                   jax.ShapeDtypeStruct(w_out_slots.shape, BF16)),
        input_output_aliases={7: 8},
        scratch_shapes=[pltpu.SemaphoreType.DMA((6,)), pltpu.SemaphoreType.DMA((6,))],
        compiler_params=_cparams(1),
    )(x, w_in_t, w_uq_t, w_ukv, gq, gkv, rope_tab, w_out_slots)


def _attn_fwd(q, k, v, pos_col, pos_row, bounds, nb, S, tq, tk):
    T = q.shape[0]
    nq, nk = S // tq, S // tk
    reps = tk // 128
    hg = HEAD_GROUP

    def body(qmin_ref, qmax_ref, kmin_ref, kmax_ref, q_ref, k_ref, v_ref, pc_ref, pr_ref, o_ref, lse_ref,
             m_sc, l_sc, acc_sc):
        b, i = pl.program_id(0), pl.program_id(2)
        m_sc[...] = jnp.full(m_sc.shape, NEG, F32)
        l_sc[...] = jnp.zeros_like(l_sc)
        acc_sc[...] = jnp.zeros_like(acc_sc)
        q_lo = qmin_ref[b * nq + i]
        q_hi = qmax_ref[b * nq + i]

        def tile(j, masked):
            off = pl.multiple_of(j * tk, tk)
            if masked:
                keep = pc_ref[...] >= pr_ref[pl.ds(j, 1), :]
            logits = []
            for g in range(hg):
                qk = slice(g * HEAD_PAD, (g + 1) * HEAD_PAD)
                s = _dot_nt(q_ref[:, qk], k_ref[pl.ds(off, tk), qk])
                if masked:
                    s = jnp.where(keep, s, NEG)
                logits.append(s)
            probs = []
            for g in range(hg):
                hv = slice(g * 128, (g + 1) * 128)
                s = logits[g]
                m_prev = m_sc[:, hv]
                m_new = jnp.maximum(m_prev, jnp.max(s, axis=1, keepdims=True))
                p = jnp.exp2(s - jnp.concatenate([m_new] * reps, axis=1))
                a = jnp.exp2(m_prev - m_new)
                l_sc[:, hv] = a * l_sc[:, hv] + jnp.sum(p, axis=1, keepdims=True)
                m_sc[:, hv] = m_new
                probs.append((p.astype(BF16), a))
            for g in range(hg):
                hv = slice(g * 128, (g + 1) * 128)
                p, a = probs[g]
                acc_sc[:, hv] = a * acc_sc[:, hv] + _dot(p, v_ref[pl.ds(off, tk), hv])

        def step(j, carry):
            visible = kmin_ref[b * nk + j] <= q_hi
            clear = q_lo >= kmax_ref[b * nk + j]

            @pl.when(jnp.logical_and(visible, clear))
            def _():
                tile(j, False)

            @pl.when(jnp.logical_and(visible, jnp.logical_not(clear)))
            def _():
                tile(j, True)
            return carry

        lax.fori_loop(0, nk, step, 0)
        l = l_sc[...]
        o_ref[...] = acc_sc[...] / l
        lse_ref[...] = m_sc[...] + jnp.log2(l)

    ng = HEADS // hg
    stat = pltpu.VMEM((tq, hg * 128), F32)
    return pl.pallas_call(
        body, name="attn_fwd",
        grid_spec=pltpu.PrefetchScalarGridSpec(
            num_scalar_prefetch=4, grid=(nb, ng, nq),
            in_specs=[pl.BlockSpec((tq, hg * HEAD_PAD), lambda b, h, i, *_: (b * nq + i, h)),
                      pl.BlockSpec((S, hg * HEAD_PAD), lambda b, h, i, *_: (b, h)),
                      pl.BlockSpec((S, hg * 128), lambda b, h, i, *_: (b, h)),
                      pl.BlockSpec((tq, 1), lambda b, h, i, *_: (b * nq + i, 0)),
                      pl.BlockSpec((None, nk, tk), lambda b, h, i, *_: (b, 0, 0))],
            out_specs=(pl.BlockSpec((tq, hg * 128), lambda b, h, i, *_: (b * nq + i, h)),
                       pl.BlockSpec((tq, hg * 128), lambda b, h, i, *_: (b * nq + i, h))),
            scratch_shapes=[stat, stat, stat]),
        out_shape=(jax.ShapeDtypeStruct((T, MLA_W), F32), jax.ShapeDtypeStruct((T, MLA_W), F32)),
        compiler_params=_cparams(3),
    )(*bounds, q, k, v, pos_col, pos_row.reshape(nb, nk, tk))


def _mid(x, tgt, o, ga, u, gb, w_out, pool_w, pool_scale, ln_g, ln_b, S, tm):
    T = x.shape[0]
    tps = S // tm
    hb = tm // HALO

    def body(x_ref, tgt_ref, o_ref, ga_ref, u_ref, uh_ref, gb_ref, wout_ref, pw_ref,
             ps_ref, lng_ref, lnb_ref,
             dz_ref, do_ref, delta_ref, dga_ref, dgb_ref, dpc_ref,
             dwout_ref, dpw_ref, dps_ref, dlng_ref, dlnb_ref, loss_ref):
        i = pl.program_id(0)

        @pl.when(i == 0)
        def _():
            dwout_ref[...] = jnp.zeros_like(dwout_ref)
            dpw_ref[...] = jnp.zeros_like(dpw_ref)
            dps_ref[...] = jnp.zeros_like(dps_ref)
            dlng_ref[...] = jnp.zeros_like(dlng_ref)
            dlnb_ref[...] = jnp.zeros_like(dlnb_ref)
            loss_ref[...] = jnp.zeros_like(loss_ref)

        seq_tile = i % tps
        tpos = seq_tile * tm + lax.broadcasted_iota(jnp.int32, (tm, 1), 0)
        ga_v = ga_ref[...].astype(F32)
        sig_a = jax.nn.sigmoid(ga_v)
        silu_a = ga_v * sig_a
        o_v = o_ref[...]
        ya = o_v * silu_a

        u_v = u_ref[...].astype(F32)
        halo = jnp.where(seq_tile == 0, 0.0, uh_ref[...].astype(F32))
        pooled, cnts, mixed = [], [], []
        for g in range(POOL_G):
            lanes = slice(g * POOL_GD, (g + 1) * POOL_GD)
            w = jnp.concatenate([halo[:, lanes], u_v[:, lanes]], axis=0)
            for st in range(g + 1):
                w = w + pltpu.roll(w, 1 << st, 0)
            cnt = jnp.minimum(tpos + 1, 2 << g).astype(F32)
            pg = (w[HALO:, :] / cnt - u_v[:, lanes]).astype(BF16)
            pooled.append(pg)
            cnts.append(cnt)
            mixed.append(_dot(pg, pw_ref[g]))
        mixed = jnp.concatenate(mixed, axis=1)
        ps = ps_ref[...]
        ybp = mixed * ps
        gb_v = gb_ref[...].astype(F32)
        sig_b = jax.nn.sigmoid(gb_v)
        silu_b = gb_v * sig_b
        yb = ybp * silu_b

        cat = jnp.concatenate([ya, yb], axis=1).astype(BF16)
        z = ALPHA * x_ref[...] + _dot(cat, wout_ref[...])
        mu = jnp.mean(z, axis=-1, keepdims=True)
        zc = z - mu
        rstd = lax.rsqrt(jnp.mean(zc * zc, axis=-1, keepdims=True) + LN_EPS)
        zhat = zc * rstd
        lng = lng_ref[...]
        err = (zhat * lng + lnb_ref[...]) - tgt_ref[...]
        row_loss = jnp.sum(err * err, axis=1, keepdims=True)
        loss_ref[...] += jnp.broadcast_to(jnp.sum(row_loss, axis=0, keepdims=True) * (0.5 / D_MODEL), (1, 128))
        dy = err * (1.0 / D_MODEL)
        dlng_ref[...] += jnp.sum(dy * zhat, axis=0, keepdims=True)
        dlnb_ref[...] += jnp.sum(dy, axis=0, keepdims=True)
        dzh = dy * lng
        dz = rstd * (dzh - jnp.mean(dzh, axis=-1, keepdims=True)
                     - zhat * jnp.mean(dzh * zhat, axis=-1, keepdims=True))
        dz_ref[...] = dz
        dzb = dz.astype(BF16)
        dwout_ref[...] += _dot_tn(cat, dzb)
        dcat = _dot_nt(dzb, wout_ref[...])
        dya = dcat[:, :MLA_W]
        dyb = dcat[:, MLA_W:]

        do = dya * silu_a
        do_ref[...] = do.astype(BF16)
        prod = do * o_v
        for hh in range(HEADS):
            lanes = slice(hh * 128, (hh + 1) * 128)
            delta_ref[:, lanes] = jnp.broadcast_to(jnp.sum(prod[:, lanes], axis=1, keepdims=True), (tm, 128))
        dga_ref[...] = (dya * o_v * (sig_a * (1.0 + ga_v * (1.0 - sig_a)))).astype(BF16)
        dgb_ref[...] = (dyb * ybp * (sig_b * (1.0 + gb_v * (1.0 - sig_b)))).astype(BF16)
        dybp = dyb * silu_b
        dps_ref[...] += jnp.sum(dybp * mixed, axis=0, keepdims=True)
        dmixed = (dybp * ps).astype(BF16)
        for g in range(POOL_G):
            lanes = slice(g * POOL_GD, (g + 1) * POOL_GD)
            dpw_ref[g] += _dot_tn(pooled[g], dmixed[:, lanes])
            dpc_ref[:, lanes] = (_dot_nt(dmixed[:, lanes], pw_ref[g]) / cnts[g]).astype(BF16)

    row = lambda w: pl.BlockSpec((tm, w), lambda i: (i, 0))
    f = lambda w, dt: jax.ShapeDtypeStruct((T, w), dt)
    halo_spec = pl.BlockSpec((HALO, POOL_W), lambda i: (jnp.maximum(i * hb - 1, 0), 0))
    return pl.pallas_call(
        body, name="mid", grid=(T // tm,),
        in_specs=[row(D_MODEL), row(D_MODEL), row(MLA_W), row(MLA_W), row(POOL_W), halo_spec, row(POOL_W),
                  _full(w_out.shape), _full(pool_w.shape),
                  _full(pool_scale.shape), _full(ln_g.shape), _full(ln_b.shape)],
        out_specs=(row(D_MODEL), row(MLA_W), row(MLA_W), row(MLA_W), row(POOL_W), row(POOL_W),
                   _full((D_MODEL, D_MODEL)), _full(pool_w.shape), _full((1, POOL_W)),
                   _full((1, D_MODEL)), _full((1, D_MODEL)), _full((1, 128))),
        out_shape=(f(D_MODEL, F32), f(MLA_W, BF16), f(MLA_W, F32), f(MLA_W, BF16), f(POOL_W, BF16), f(POOL_W, BF16),
                   jax.ShapeDtypeStruct((D_MODEL, D_MODEL), F32), jax.ShapeDtypeStruct(pool_w.shape, F32),
                   jax.ShapeDtypeStruct((1, POOL_W), F32), jax.ShapeDtypeStruct((1, D_MODEL), F32),
                   jax.ShapeDtypeStruct((1, D_MODEL), F32), jax.ShapeDtypeStruct((1, 128), F32)),
        compiler_params=_cparams(1),
    )(x, tgt, o, ga, u, u, gb, w_out, pool_w, pool_scale, ln_g, ln_b)


def _attn_bwd(q, k, v, do, lse, delta, pos_col, pos_row, bounds, early, early_wire, nb, S, tq, tk):
    T = q.shape[0]
    ne = len(early)
    nq, nk = S // tq, S // tk
    reps = tk // 128
    hg = HEAD_GROUP
    ng = HEADS // hg

    def body(qmin_ref, qmax_ref, kmin_ref, kmax_ref, q_ref, k_ref, v_ref, do_ref, lse_ref, dl_ref, pc_ref, pr_ref,
             *rest):
        early_in, rest = rest[:ne], rest[ne:]
        dq_out, dk_out, dv_out = rest[:3]
        early_out, rest = rest[3:3 + ne], rest[3 + ne:]
        dq_ref, dk_ref, dv_ref = rest[:3]
        b, j = pl.program_id(0), pl.program_id(2)
        flat = (b * ng + pl.program_id(1)) * nk + j
        last = nb * ng * nk - 1
        when = [0, min(3, last), min(max(5 * (last + 1) // 8, 3), last), last]
        for at, phase in zip(when, _reduce_phases(early, early_wire, early_in, early_out, rest[3:])):
            pl.when(flat == at)(phase)

        @pl.when(j == 0)
        def _():
            dq_ref[...] = jnp.zeros_like(dq_ref)

        dk_ref[...] = jnp.zeros_like(dk_ref)
        dv_ref[...] = jnp.zeros_like(dv_ref)
        k_lo = kmin_ref[b * nk + j]
        k_hi = kmax_ref[b * nk + j]

        def tile(i, masked):
            rows = pl.ds(pl.multiple_of(i * tq, tq), tq)
            if masked:
                keep = pc_ref[rows, :] >= pr_ref[...]
            stage = []
            for g in range(hg):
                qk = slice(g * HEAD_PAD, (g + 1) * HEAD_PAD)
                hv = slice(g * 128, (g + 1) * 128)
                s = _dot_nt(q_ref[rows, qk], k_ref[:, qk])
                if masked:
                    s = jnp.where(keep, s, NEG)
                stage.append((s, _dot_nt(do_ref[rows, hv], v_ref[:, hv])))
            grads = []
            for g in range(hg):
                hv = slice(g * 128, (g + 1) * 128)
                s, dp = stage[g]
                p = jnp.exp2(s - jnp.concatenate([lse_ref[rows, hv]] * reps, axis=1))
                ds = (p * (dp - jnp.concatenate([dl_ref[rows, hv]] * reps, axis=1))).astype(BF16)
                grads.append((p.astype(BF16), ds))
            for g in range(hg):
                qk = slice(g * HEAD_PAD, (g + 1) * HEAD_PAD)
                hv = slice(g * 128, (g + 1) * 128)
                p, ds = grads[g]
                dv_ref[:, hv] += _dot_tn(p, do_ref[rows, hv])
                dq_ref[rows, qk] += _dot(ds, k_ref[:, qk])
                dk_ref[:, qk] += _dot_tn(ds, q_ref[rows, qk])

        def step(i, carry):
            visible = k_lo <= qmax_ref[b * nq + i]
            clear = qmin_ref[b * nq + i] >= k_hi

            @pl.when(jnp.logical_and(visible, clear))
            def _():
                tile(i, False)

            @pl.when(jnp.logical_and(visible, jnp.logical_not(clear)))
            def _():
                tile(i, True)
            return carry

        lax.fori_loop(0, nq, step, 0)
        dk_out[...] = dk_ref[...].astype(BF16)
        dv_out[...] = dv_ref[...].astype(BF16)

        @pl.when(j == nk - 1)
        def _():
            dq_out[...] = dq_ref[...].astype(BF16)

    seq = lambda w: pl.BlockSpec((S, w), lambda b, h, j, *_: (b, h))
    blk = lambda w: pl.BlockSpec((tk, w), lambda b, h, j, *_: (b * nk + j, h))
    outs = pl.pallas_call(
        body, name="attn_bwd",
        grid_spec=pltpu.PrefetchScalarGridSpec(
            num_scalar_prefetch=4, grid=(nb, ng, nk),
            in_specs=[seq(hg * HEAD_PAD), blk(hg * HEAD_PAD), blk(hg * 128),
                      seq(hg * 128), seq(hg * 128), seq(hg * 128),
                      pl.BlockSpec((S, 1), lambda b, h, j, *_: (b, 0)),
                      pl.BlockSpec((None, 1, tk), lambda b, h, j, *_: (b, 0, j))] + [ANY] * ne,
            out_specs=(seq(hg * HEAD_PAD), blk(hg * HEAD_PAD), blk(hg * 128)) + (ANY,) * ne,
            scratch_shapes=[pltpu.VMEM((S, hg * HEAD_PAD), F32), pltpu.VMEM((tk, hg * HEAD_PAD), F32),
                            pltpu.VMEM((tk, hg * 128), F32)] + _reduce_scratch(early, early_wire)),
        out_shape=(jax.ShapeDtypeStruct((T, HEADS * HEAD_PAD), BF16),
                   jax.ShapeDtypeStruct((T, HEADS * HEAD_PAD), BF16),
                   jax.ShapeDtypeStruct((T, MLA_W), BF16)) + tuple(jax.ShapeDtypeStruct(g.shape[1:], F32) for g in early),
        compiler_params=_cparams(3),
    )(*bounds, q, k, v, do, lse, delta, pos_col, pos_row, *early)
    return outs[:3], outs[3:]


def _bwd_proj(dq, dk, dv, xq, xkv, x, dz, dga, dgb, dpc, rope_tab, w_uq_t, w_ukv, w_in_t, gq, gkv, S, tm):
    T = x.shape[0]
    tps = S // tm
    hb = tm // HALO
    n_tiles = T // tm

    def body(dq_ref, dk_ref, dv_ref, xq_ref, xkv_ref, x_ref, dz_ref, dga_ref, dgb_ref, dpc_ref, dph_ref,
             tab_ref, wuq_ref, wukv_ref, win_ref, gq_ref, gkv_ref,
             dx_ref, dwin_hbm, dwuq_hbm, dwukv_hbm, dgq_ref, dgkv_ref,
             acc_win, acc_wuq, acc_wukv, dh_sc):
        i = pl.program_id(0)

        @pl.when(i == 0)
        def _():
            acc_win[...] = jnp.zeros_like(acc_win)
            acc_wuq[...] = jnp.zeros_like(acc_wuq)
            acc_wukv[...] = jnp.zeros_like(acc_wukv)
            dgq_ref[...] = jnp.zeros_like(dgq_ref)
            dgkv_ref[...] = jnp.zeros_like(dgkv_ref)
            dh_sc[...] = jnp.zeros_like(dh_sc)

        dh_prev = dh_sc[...]
        dx_ref[...] = ALPHA * dz_ref[...] + _dot(dh_prev, win_ref[...])
        acc_win[...] += _dot_tn(dh_prev, x_ref[...].astype(BF16))

        live = jnp.where(i < n_tiles, 1.0, 0.0)
        c, sa, sb = _expand_rope_table(tab_ref[...])
        dq_v = dq_ref[...].astype(F32) * (SCALE * live)
        dk_v = dk_ref[...].astype(F32) * (LN2 * live)
        dv_v = dv_ref[...].astype(F32) * live
        dq_parts, dkv_parts = [], []
        dkr = jnp.zeros((tm, 128), F32)
        for hh in range(HEADS):
            b0 = hh * HEAD_PAD
            dq_parts.append(dq_v[:, b0:b0 + 128].astype(BF16))
            dq_parts.append(_rope(dq_v[:, b0 + 128:b0 + 256], c, sa, sb, -1.0).astype(BF16))
            dkv_parts.append(dk_v[:, b0:b0 + 128].astype(BF16))
            dkv_parts.append(dv_v[:, hh * 128:(hh + 1) * 128].astype(BF16))
            dkr = dkr + dk_v[:, b0 + 128:b0 + 256]
        dqp = jnp.concatenate(dq_parts, axis=1)
        dkvp = jnp.concatenate(dkv_parts, axis=1)
        dkrr = _rope(dkr, c, sa, sb, -1.0)

        def rms_bwd(xv, g, dyn, dg_ref):
            r = lax.rsqrt(jnp.mean(xv * xv, axis=-1, keepdims=True) + RMS_EPS)
            xhat = xv * r
            dg_ref[...] += jnp.sum(dyn * xhat, axis=0, keepdims=True)
            dxh = dyn * g
            return r * (dxh - xhat * jnp.mean(dxh * xhat, axis=-1, keepdims=True))

        xq_v = xq_ref[...].astype(F32)
        gq_v = gq_ref[...]
        rq = lax.rsqrt(jnp.mean(xq_v * xq_v, axis=-1, keepdims=True) + RMS_EPS)
        acc_wuq[...] += _dot_tn(dqp, ((xq_v * rq) * gq_v).astype(BF16))
        dxq = rms_bwd(xq_v, gq_v, _dot(dqp, wuq_ref[...]), dgq_ref)

        xkv_v = xkv_ref[...].astype(F32)
        gkv_v = gkv_ref[...]
        rkv = lax.rsqrt(jnp.mean(xkv_v * xkv_v, axis=-1, keepdims=True) + RMS_EPS)
        acc_wukv[...] += _dot_tn(((xkv_v * rkv) * gkv_v).astype(BF16), dkvp)
        dxkv = rms_bwd(xkv_v, gkv_v, _dot_nt(dkvp, wukv_ref[...]), dgkv_ref)

        seq_tile = i % tps
        tpos = seq_tile * tm + lax.broadcasted_iota(jnp.int32, (tm, 1), 0)
        dpc_v = dpc_ref[...].astype(F32)
        halo = jnp.where(seq_tile == tps - 1, 0.0, dph_ref[...].astype(F32))
        n = tm + HALO
        du = []
        for g in range(POOL_G):
            lanes = slice(g * POOL_GD, (g + 1) * POOL_GD)
            f = jnp.concatenate([dpc_v[:, lanes], halo[:, lanes]], axis=0)
            for st in range(g + 1):
                f = f + pltpu.roll(f, n - (1 << st), 0)
            cnt = jnp.minimum(tpos + 1, 2 << g).astype(F32)
            du.append((f[:tm, :] - dpc_v[:, lanes] * cnt).astype(BF16))

        dh_sc[...] = jnp.concatenate([dxq.astype(BF16), dxkv.astype(BF16), dkrr.astype(BF16), dga_ref[...]]
                                     + du + [dgb_ref[...]], axis=1)

        @pl.when(i == n_tiles)
        def _():
            pltpu.sync_copy(acc_win.at[pl.ds(0, ROPE_END)], dwin_hbm.at[pl.ds(0, ROPE_END)])
            pltpu.sync_copy(acc_win.at[pl.ds(COL_GA, IN_EXT - COL_GA)], dwin_hbm.at[pl.ds(ROPE_END, IN_W - ROPE_END)])
            for hh in range(HEADS):
                pltpu.sync_copy(acc_wuq.at[pl.ds(hh * HEAD_PAD, NOPE + ROPE)], dwuq_hbm.at[hh])
            pltpu.sync_copy(acc_wukv, dwukv_hbm)

    cur = lambda w: pl.BlockSpec((tm, w), lambda i: (jnp.minimum(i, n_tiles - 1), 0))
    prev = lambda w: pl.BlockSpec((tm, w), lambda i: (jnp.maximum(i - 1, 0), 0))
    halo_spec = pl.BlockSpec((HALO, POOL_W), lambda i: (jnp.minimum((i + 1) * hb, T // HALO - 1), 0))
    return pl.pallas_call(
        body, name="bwd_proj", grid=(n_tiles + 1,),
        in_specs=[cur(1024), cur(1024), cur(512), cur(512), cur(256), prev(D_MODEL), prev(D_MODEL),
                  cur(512), cur(512), cur(512), halo_spec, cur(128),
                  _full(w_uq_t.shape), _full(w_ukv.shape), _full(w_in_t.shape), _full(gq.shape), _full(gkv.shape)],
        out_specs=(prev(D_MODEL), ANY, ANY, ANY, _full((1, Q_LORA)), _full((1, KV_LORA))),
        out_shape=(jax.ShapeDtypeStruct((T, D_MODEL), F32),
                   jax.ShapeDtypeStruct((IN_W, D_MODEL), F32),
                   jax.ShapeDtypeStruct((HEADS, NOPE + ROPE, Q_LORA), F32),
                   jax.ShapeDtypeStruct((KV_LORA, 1024), F32),
                   jax.ShapeDtypeStruct((1, Q_LORA), F32), jax.ShapeDtypeStruct((1, KV_LORA), F32)),
        scratch_shapes=[pltpu.VMEM((IN_EXT, D_MODEL), F32), pltpu.VMEM((HEADS * HEAD_PAD, Q_LORA), F32),
                        pltpu.VMEM((KV_LORA, 1024), F32), pltpu.VMEM((tm, IN_EXT), BF16)],
        compiler_params=_cparams(1),
    )(dq, dk, dv, xq, xkv, x, dz, dga, dgb, dpc, dpc, rope_tab, w_uq_t, w_ukv, w_in_t, gq, gkv)


def kernel(x, positions, w_in, q_norm_g, w_uq, kv_norm_g, w_ukv, pool_w, pool_scale, w_out, ln_g, ln_b, loss_target, m_w_in, m_q_norm_g, m_w_uq, m_kv_norm_g, m_w_ukv, m_pool_w, m_pool_scale, m_w_out, m_ln_g, m_ln_b, v_w_in, v_q_norm_g, v_w_uq, v_kv_norm_g, v_w_ukv, v_pool_w, v_pool_scale, v_w_out, v_ln_g, v_ln_b):
    nb, S, _ = x.shape
    T = nb * S
    tm = min(256, S)
    tq = min(512, S)
    tk = min(512, S)
    assert S % tm == 0 and tm % HALO == 0 and S % tq == 0 and S % tk == 0

    cx, cy, cc = lax.axis_index("x"), lax.axis_index("y"), lax.axis_index("c")
    me = 2 * cx + cy

    half = ROPE // 2
    inv_freq = ROPE_THETA ** (-jnp.arange(half, dtype=F32) / half)
    freq_row = jnp.concatenate([inv_freq, inv_freq, jnp.zeros((2 * half,), F32)]).reshape(1, 128)
    pos_col = positions.reshape(T, 1)
    pos_row = positions.reshape(nb, 1, S)
    pos_q = positions.reshape(nb, S // tq, tq)
    pos_k = positions.reshape(nb, S // tk, tk)
    bounds = (jnp.min(pos_q, axis=2).reshape(-1), jnp.max(pos_q, axis=2).reshape(-1),
              jnp.min(pos_k, axis=2).reshape(-1), jnp.max(pos_k, axis=2).reshape(-1))

    def own_slot(w, slot_rows):
        blk = jnp.pad(w.astype(BF16), ((0, slot_rows - w.shape[0]), (0, 0)))
        return lax.dynamic_update_slice(jnp.zeros((N_CHIPS,) + blk.shape, BF16), blk[None], (me, 0, 0))

    (w_in_g, w_uq_g, w_ukv_g), rope_tab = _weight_gather(
        [own_slot(w_in.T, IN_SHARD), own_slot(w_uq.T, HEAD_PAD), own_slot(w_ukv, KV_LORA)], (IN_SHARD, NOPE + ROPE, KV_LORA),
        pos_col, freq_row)
    w_in_f = w_in_g.reshape(IN_W, D_MODEL)
    w_in_t = jnp.concatenate([w_in_f[:ROPE_END], jnp.zeros((COL_GA - ROPE_END, D_MODEL), BF16), w_in_f[ROPE_END:]], axis=0)
    w_uq_t = w_uq_g.reshape(HEADS * HEAD_PAD, Q_LORA)
    w_ukv_f = w_ukv_g.transpose(1, 0, 2).reshape(KV_LORA, 1024)
    pool_w_b = pool_w.astype(BF16)
    gq2 = q_norm_g.reshape(1, Q_LORA)
    gkv2 = kv_norm_g.reshape(1, KV_LORA)
    ps2 = pool_scale.reshape(1, POOL_W)

    xf = x.reshape(T, D_MODEL)
    tgt = loss_target.reshape(T, D_MODEL)

    xq, xkv, ga, u, gb, q, k, v, w_out_g = _fwd_proj(
        xf, w_in_t, w_uq_t, w_ukv_f, gq2, gkv2, rope_tab, own_slot(w_out, 256), tm)
    w_out_f = w_out_g.reshape(D_MODEL, D_MODEL)
    o, lse = _attn_fwd(q, k, v, pos_col, pos_row, bounds, nb, S, tq, tk)

    (dz, do, delta, dga, dgb, dpc, d_w_out, d_pool_w, d_pool_scale, d_ln_g, d_ln_b, loss_part) = _mid(
        xf, tgt, o, ga, u, gb, w_out_f, pool_w_b, ps2, ln_g, ln_b, S, tm)

    wide = lambda a: jnp.pad(a.reshape(1, -1), ((0, 0), (0, D_MODEL - a.size)))
    blank = lambda r: jnp.zeros((r, D_MODEL), F32)
    vec_early = jnp.concatenate([d_ln_g, d_ln_b, wide(d_pool_scale), blank(6), wide(loss_part), blank(VEC_ROWS - 10)], axis=0)
    early = [d_w_out.reshape(N_CHIPS, 256, D_MODEL), d_pool_w.reshape(1, -1, D_MODEL), vec_early[None]]
    (dq, dk, dv), (g_out, pw_sum, vec_early_sum) = _attn_bwd(
        q, k, v, do, lse, delta, pos_col, pos_row, bounds, early, (BF16, F32, F32), nb, S, tq, tk)
    dx, d_w_in_t, d_w_uq_t, d_w_ukv, d_gq, d_gkv = _bwd_proj(
        dq, dk, dv, xq, xkv, xf, dz, dga, dgb, dpc, rope_tab, w_uq_t, w_ukv_f, w_in_t, gq2, gkv2, S, tm)
    grad_x = dx.reshape(nb, S, D_MODEL)

    g_in = d_w_in_t.reshape(N_CHIPS, IN_SHARD, D_MODEL)
    g_uq = d_w_uq_t
    g_ukv = d_w_ukv.reshape(KV_LORA, N_CHIPS, 256).transpose(1, 0, 2)
    vec_late = jnp.concatenate([blank(3), wide(d_gq), blank(4), wide(d_gkv), blank(VEC_ROWS - 9)], axis=0)
    g_in, g_uq, g_ukv, vec_late_sum = _grad_reduce([g_in, g_uq, g_ukv, vec_late[None]], (BF16, BF16, BF16, F32))
    g_big = [g_in, g_uq, g_ukv, g_out]
    pw_sum = pw_sum.reshape(POOL_G * POOL_GD, POOL_GD)
    vec_sum = vec_early_sum + vec_late_sum

    big = _adamw_big(g_big, [w_in.T, w_uq.T, w_ukv, w_out], [m_w_in.T, m_w_uq.T, m_w_ukv, m_w_out],
                     [v_w_in.T, v_w_uq.T, v_w_ukv, v_w_out])
    two_d = lambda a: a.reshape(-1, a.shape[-1])
    small_names = lambda pw, lg, lb, ps, gq, gkv: [two_d(pw), lg, lb, ps.reshape(1, -1), gq.reshape(1, -1), gkv.reshape(1, -1)]
    small, loss_row = _adamw_small(
        pw_sum, vec_sum,
        small_names(pool_w, ln_g, ln_b, pool_scale, q_norm_g, kv_norm_g),
        small_names(m_pool_w, m_ln_g, m_ln_b, m_pool_scale, m_q_norm_g, m_kv_norm_g),
        small_names(v_pool_w, v_ln_g, v_ln_b, v_pool_scale, v_q_norm_g, v_kv_norm_g))
    loss = loss_row[0, 0]

    def leaves(kind):
        b = [g_big[t] if kind == 0 else big[t][kind - 1] for t in range(N_BIG)]
        b = [b[0].T, b[1].T, b[2], b[3]]
        s = [small[t][kind] for t in range(6)]
        return (b[0], s[4].reshape(Q_LORA), b[1], s[5].reshape(KV_LORA), b[2],
                s[0].reshape(POOL_G, POOL_GD, POOL_GD), s[3].reshape(POOL_W), b[3], s[1], s[2])

    return (loss, grad_x) + leaves(0) + leaves(1) + leaves(2) + leaves(3)
```

```python
import jax
import jax.numpy as jnp
from jax import lax
from jax.experimental import pallas as pl
from jax.experimental.pallas import tpu as pltpu

F32 = jnp.float32
BF16 = jnp.bfloat16
MESH = pl.DeviceIdType.MESH

HEADS = 4
NOPE = 128
ROPE = 64
HEAD_PAD = 256
Q_LORA = 512
KV_LORA = 256
MLA_W = 512
POOL_W = 512
POOL_G = 4
POOL_GD = 128
D_MODEL = 1024
IN_W = 2368
IN_EXT = 2432
COL_KV, COL_KR, COL_GA, COL_U, COL_GB = 512, 768, 896, 1408, 1920
ROPE_END = COL_KR + 64
IN_SHARD = IN_W // 4
ROPE_THETA = 10000.0
RMS_EPS = 1e-6
LN_EPS = 1e-5
ALPHA = 2.0 ** 0.25
SCALE = 192.0 ** -0.5
LOG2E = 1.4426950408889634
LN2 = 0.6931471805599453
QSCALE = SCALE * LOG2E
NEG = float(jnp.finfo(jnp.float32).min)
HEAD_GROUP = 2
HEAD_GROUP_FWD = 4
HALO = 16

ADAM_LR = 0.001
ADAM_B1 = 0.9
ADAM_B2 = 0.999
ADAM_EPS = 1e-08
ADAM_WD = 0.01
ADAM_STEP = 10

N_CHIPS = 4
N_BIG = 4
VEC_ROWS = 16

VMEM_LIMIT = 56 * 1024 * 1024


def _cparams(n_grid_dims=0, **kw):
    sem = ("arbitrary",) * n_grid_dims if n_grid_dims else None
    return pltpu.CompilerParams(dimension_semantics=sem, vmem_limit_bytes=VMEM_LIMIT, **kw)


def _full(shape):
    nd = len(shape)
    return pl.BlockSpec(shape, lambda *_: (0,) * nd)


def _dot(a, b):
    return jnp.dot(a, b, preferred_element_type=F32)


def _dot_nt(a, b):
    return lax.dot_general(a, b, (((1,), (1,)), ((), ())), preferred_element_type=F32)


def _dot_tn(a, b):
    return lax.dot_general(a, b, (((0,), (0,)), ((), ())), preferred_element_type=F32)


def _rope_table(pos_col, freq_row):
    lane = lax.broadcasted_iota(jnp.int32, (1, 128), 1)
    ang = pos_col.astype(F32) * freq_row
    return jnp.where(lane < 32, jnp.cos(ang), jnp.where(lane < 64, jnp.sin(ang), 0.0))


def _expand_rope_table(tab):
    lane = lax.broadcasted_iota(jnp.int32, (1, 128), 1)
    second = jnp.logical_and(lane >= 32, lane < 64)
    c = jnp.where(lane < 32, tab, jnp.where(second, pltpu.roll(tab, 32, 1), 0.0))
    sa = jnp.where(lane < 32, pltpu.roll(tab, 96, 1), 0.0)
    sb = jnp.where(second, tab, 0.0)
    return c, sa, sb


def _rope(g, c, sa, sb, sign):
    return g * c + sign * (pltpu.roll(g, 32, 1) * sb - pltpu.roll(g, 96, 1) * sa)


def _place():
    x, y, c = lax.axis_index("x"), lax.axis_index("y"), lax.axis_index("c")
    chips = [(1 - x, y), (x, 1 - y), (1 - x, 1 - y)]
    return x, y, c, chips


ANY = pl.BlockSpec(memory_space=pl.ANY)


ROPE_CHUNK = 2048


def _weight_gather(slots, valid_rows, pos_col, freq_row):
    n = len(slots)
    T = pos_col.shape[0]
    chunk = min(ROPE_CHUNK, T)
    assert T % chunk == 0

    def body(*refs):
        pos_hbm, freq_ref = refs[n:n + 2]
        outs = refs[n + 2:2 * n + 2]
        tab_hbm = refs[2 * n + 2]
        send_sems, recv_sems, pos_buf, tab_buf = refs[2 * n + 3:]
        x, y, c, chips = _place()
        me = 2 * x + y

        def copy(t, k, chip_idx, half, to):
            hc = slots[t].shape[2] // 2
            blk = outs[t].at[chip_idx, pl.ds(0, valid_rows[t]), pl.ds(half * hc, hc)]
            return pltpu.make_async_remote_copy(
                src_ref=blk, dst_ref=blk, send_sem=send_sems.at[6 * t + k], recv_sem=recv_sems.at[6 * t + k],
                device_id=to, device_id_type=MESH)

        first = [copy(t, j, me, c, (cx, cy, c)) for t in range(n) for j, (cx, cy) in enumerate(chips)]
        for cp in first:
            cp.start()

        def table_chunk(r, carry):
            rows = pl.ds(pl.multiple_of(r * chunk, chunk), chunk)
            pltpu.sync_copy(pos_hbm.at[rows], pos_buf)
            tab_buf[...] = _rope_table(pos_buf[...], freq_ref[...])
            pltpu.sync_copy(tab_buf, tab_hbm.at[rows])
            return carry

        lax.fori_loop(0, T // chunk, table_chunk, 0)
        passed = []
        for j, (cx, cy) in enumerate(chips):
            for t in range(n):
                copy(t, j, 2 * cx + cy, c, (x, y, c)).wait_recv()
                fwd = copy(t, 3 + j, 2 * cx + cy, c, (x, y, 1 - c))
                fwd.start()
                passed.append(fwd)
        for j, (cx, cy) in enumerate(chips):
            for t in range(n):
                copy(t, 3 + j, 2 * cx + cy, 1 - c, (x, y, c)).wait_recv()
        for cp in first + passed:
            cp.wait_send()

    outs = pl.pallas_call(
        body, name="weight_gather",
        out_shape=tuple(jax.ShapeDtypeStruct(a.shape, a.dtype) for a in slots) + (jax.ShapeDtypeStruct((T, 128), F32),),
        in_specs=[ANY] * n + [ANY, pl.BlockSpec(memory_space=pltpu.VMEM)], out_specs=(ANY,) * (n + 1),
        input_output_aliases={t: t for t in range(n)},
        scratch_shapes=[pltpu.SemaphoreType.DMA((6 * n,)), pltpu.SemaphoreType.DMA((6 * n,)),
                        pltpu.VMEM((chunk, 1), jnp.int32), pltpu.VMEM((chunk, 128), F32)],
    )(*slots, pos_col, freq_row)
    return outs[:n], outs[n]


def _reduce_scratch(gs, wire_dtypes):
    n = len(gs)
    half = [(g.shape[1], g.shape[2] // 2) for g in gs]
    return ([pltpu.VMEM((4,) + h, F32) for h in half] + [pltpu.VMEM((4,) + h, F32) for h in half]
            + [pltpu.VMEM((3,) + h, w) for h, w in zip(half, wire_dtypes)]
            + [pltpu.VMEM((3,) + h, w) for h, w in zip(half, wire_dtypes)]
            + [pltpu.VMEM(h, F32) for h in half]
            + [pltpu.SemaphoreType.DMA((4 * n,)), pltpu.SemaphoreType.DMA((4 * n,)),
               pltpu.SemaphoreType.DMA((3 * n,)), pltpu.SemaphoreType.DMA((3 * n,)),
               pltpu.SemaphoreType.DMA((n,)), pltpu.SemaphoreType.DMA((n,)),
               pltpu.SemaphoreType.DMA((4 * n,)), pltpu.SemaphoreType.DMA((n,))])


def _reduce_phases(gs, wire_dtypes, g_refs, out_refs, scr):
    n = len(gs)
    hcs = [g.shape[2] // 2 for g in gs]
    own, sib, wire, got, fin = (scr[i * n:(i + 1) * n] for i in range(5))
    d2d_send, d2d_recv, ici_send, ici_recv, fin_send, fin_recv, loc_in, loc_out = scr[5 * n:]

    def place():
        x, y, c, chips = _place()
        return c, chips, (x, y, 1 - c), [2 * cx + cy for cx, cy in chips] + [2 * x + y]

    def remote(src, dst, send, recv, to):
        return pltpu.make_async_remote_copy(src_ref=src, dst_ref=dst, send_sem=send, recv_sem=recv,
                                            device_id=to, device_id_type=MESH)

    def block(ref, t, half, lead=None):
        cols = pl.ds(half * hcs[t], hcs[t])
        rows = pl.ds(0, gs[t].shape[1])
        return ref.at[rows, cols] if lead is None else ref.at[lead, rows, cols]

    def load(t, j):
        c, _, _, dests = place()
        return pltpu.make_async_copy(block(g_refs[t], t, c, dests[j]), own[t].at[j], loc_in.at[4 * t + j])

    def d2d(t, j):
        c, _, sibling, dests = place()
        return remote(block(g_refs[t], t, 1 - c, dests[j]), sib[t].at[j],
                      d2d_send.at[4 * t + j], d2d_recv.at[4 * t + j], sibling)

    def ici(t, j):
        c, chips, _, _ = place()
        return remote(wire[t].at[j], got[t].at[j], ici_send.at[3 * t + j], ici_recv.at[3 * t + j], chips[j] + (c,))

    def store(t):
        c = place()[0]
        return pltpu.make_async_copy(fin[t], block(out_refs[t], t, c), loc_out.at[t])

    def final(t, half_of):
        c, _, sibling, _ = place()
        return remote(fin[t], block(out_refs[t], t, c if half_of == "mine" else 1 - c),
                      fin_send.at[t], fin_recv.at[t], sibling)

    def start():
        for j in range(4):
            for t in range(n):
                load(t, j).start()
                d2d(t, j).start()

    def exchange():
        for j in range(3):
            for t in range(n):
                load(t, j).wait()
                d2d(t, j).wait_recv()
                wire[t][j] = (own[t][j] + sib[t][j]).astype(wire_dtypes[t])
                ici(t, j).start()

    def finish():
        for t in range(n):
            load(t, 3).wait()
            d2d(t, 3).wait_recv()
            for j in range(3):
                ici(t, j).wait_recv()
            fin[t][...] = (((own[t][3] + sib[t][3]) + got[t][0].astype(F32))
                           + (got[t][1].astype(F32) + got[t][2].astype(F32)))
            store(t).start()
            final(t, "mine").start()

    def drain():
        for t in range(n):
            final(t, "theirs").wait_recv()
        for t in range(n):
            for j in range(4):
                d2d(t, j).wait_send()
            for j in range(3):
                ici(t, j).wait_send()
            final(t, "mine").wait_send()
            store(t).wait()

    return start, exchange, finish, drain


def _grad_reduce(gs, wire_dtypes):
    n = len(gs)

    def body(*refs):
        for phase in _reduce_phases(gs, wire_dtypes, refs[:n], refs[n:2 * n], refs[2 * n:]):
            phase()

    return pl.pallas_call(
        body, name="grad_reduce",
        out_shape=tuple(jax.ShapeDtypeStruct(g.shape[1:], F32) for g in gs),
        in_specs=[ANY] * n, out_specs=(ANY,) * n, scratch_shapes=_reduce_scratch(gs, wire_dtypes),
        compiler_params=_cparams(),
    )(*gs)


def _adamw_math(g, w, m, v):
    nm = ADAM_B1 * m + (1.0 - ADAM_B1) * g
    nv = ADAM_B2 * v + (1.0 - ADAM_B2) * (g * g)
    m_hat = nm / (1.0 - ADAM_B1 ** ADAM_STEP)
    v_hat = nv / (1.0 - ADAM_B2 ** ADAM_STEP)
    return -ADAM_LR * (m_hat / (jnp.sqrt(v_hat) + ADAM_EPS) + ADAM_WD * w), nm, nv


ADAM_STEPS = 8


def _adamw_big(gs, ws, ms, vs):
    n = len(gs)

    def body(*refs):
        for t in range(n):
            d, nm, nv = _adamw_math(refs[t][...], refs[n + t][...], refs[2 * n + t][...], refs[3 * n + t][...])
            refs[4 * n + 3 * t][...] = d
            refs[4 * n + 3 * t + 1][...] = nm
            refs[4 * n + 3 * t + 2][...] = nv

    def tile_spec(shape):
        rows, cols = shape
        if rows % (8 * ADAM_STEPS) == 0:
            return pl.BlockSpec((rows // ADAM_STEPS, cols), lambda i: (i, 0))
        return pl.BlockSpec((rows, cols // ADAM_STEPS), lambda i: (0, i))

    specs = [tile_spec(g.shape) for g in gs]
    out_specs, out_shape = [], []
    for t in range(n):
        out_specs += [specs[t]] * 3
        out_shape += [jax.ShapeDtypeStruct(gs[t].shape, F32)] * 3
    outs = pl.pallas_call(
        body, name="adamw_big", grid=(ADAM_STEPS,),
        in_specs=specs * 4, out_specs=tuple(out_specs), out_shape=tuple(out_shape),
        compiler_params=_cparams(1),
    )(*gs, *ws, *ms, *vs)
    return [outs[3 * t: 3 * t + 3] for t in range(n)]


def _adamw_small(pw_sum, vec_sum, ws, ms, vs):
    rows = (None, 0, 1, 2, 3, 8)
    n = len(ws)

    def body(pw_ref, vec_ref, *refs):
        outs = refs[3 * n:]
        for t in range(n):
            w_ref, m_ref, v_ref = refs[t], refs[n + t], refs[2 * n + t]
            if rows[t] is None:
                g = pw_ref[...]
            else:
                g = vec_ref[rows[t]:rows[t] + 1, 0:w_ref.shape[1]]
            d, nm, nv = _adamw_math(g, w_ref[...], m_ref[...], v_ref[...])
            outs[4 * t][...] = g
            outs[4 * t + 1][...] = d
            outs[4 * t + 2][...] = nm
            outs[4 * t + 3][...] = nv
        outs[4 * n][...] = vec_ref[9:10, 0:128]

    vm = pl.BlockSpec(memory_space=pltpu.VMEM)
    out_shape = []
    for w in ws:
        out_shape += [jax.ShapeDtypeStruct(w.shape, F32)] * 4
    out_shape.append(jax.ShapeDtypeStruct((1, 128), F32))
    outs = pl.pallas_call(
        body, name="adamw_small", in_specs=[vm] * (2 + 3 * n), out_specs=(vm,) * (4 * n + 1),
        out_shape=tuple(out_shape),
    )(pw_sum, vec_sum, *ws, *ms, *vs)
    return [outs[4 * t: 4 * t + 4] for t in range(n)], outs[4 * n]


def _fwd_proj(x, w_in_t, w_uq_t, w_ukv, gq, gkv, rope_tab, w_out_slots, tm):
    T = x.shape[0]
    n_steps = T // tm
    fwd_step = n_steps // 2

    def body(x_ref, win_ref, wuq_ref, wukv_ref, gq_ref, gkv_ref, tab_ref, wo_in,
             xq_ref, xkv_ref, ga_ref, u_ref, gb_ref, q_ref, k_ref, v_ref, wo_ref, send_sems, recv_sems):
        i = pl.program_id(0)
        px, py, pc, chips = _place()
        hc = D_MODEL // 2

        def wo_copy(k, chip_idx, half, to):
            blk = wo_ref.at[chip_idx, pl.ds(0, 256), pl.ds(half * hc, hc)]
            return pltpu.make_async_remote_copy(src_ref=blk, dst_ref=blk, send_sem=send_sems.at[k],
                                                recv_sem=recv_sems.at[k], device_id=to, device_id_type=MESH)

        @pl.when(i == 0)
        def _():
            for j, (cx, cy) in enumerate(chips):
                wo_copy(j, 2 * px + py, pc, (cx, cy, pc)).start()

        @pl.when(i == fwd_step)
        def _():
            for j, (cx, cy) in enumerate(chips):
                wo_copy(j, 2 * cx + cy, pc, (px, py, pc)).wait_recv()
                wo_copy(3 + j, 2 * cx + cy, pc, (px, py, 1 - pc)).start()

        @pl.when(i == n_steps - 1)
        def _():
            for j, (cx, cy) in enumerate(chips):
                wo_copy(3 + j, 2 * cx + cy, 1 - pc, (px, py, pc)).wait_recv()
            for j, (cx, cy) in enumerate(chips):
                wo_copy(j, 2 * px + py, pc, (cx, cy, pc)).wait_send()
                wo_copy(3 + j, 2 * cx + cy, pc, (px, py, 1 - pc)).wait_send()

        h = _dot_nt(x_ref[...].astype(BF16), win_ref[...])
        xq = h[:, 0:COL_KV]
        xkv = h[:, COL_KV:COL_KR]
        xq_ref[...] = xq.astype(BF16)
        xkv_ref[...] = xkv.astype(BF16)
        ga_ref[...] = h[:, COL_GA:COL_U].astype(BF16)
        u_ref[...] = h[:, COL_U:COL_GB].astype(BF16)
        gb_ref[...] = h[:, COL_GB:IN_EXT].astype(BF16)
        c, sa, sb = _expand_rope_table(tab_ref[...])
        rq = lax.rsqrt(jnp.mean(xq * xq, axis=-1, keepdims=True) + RMS_EPS)
        q = _dot_nt(((xq * rq) * gq_ref[...]).astype(BF16), wuq_ref[...]) * QSCALE
        rkv = lax.rsqrt(jnp.mean(xkv * xkv, axis=-1, keepdims=True) + RMS_EPS)
        kv = _dot(((xkv * rkv) * gkv_ref[...]).astype(BF16), wukv_ref[...])
        kr = _rope(h[:, COL_KR:COL_GA], c, sa, sb, 1.0).astype(BF16)
        for hh in range(HEADS):
            b0 = hh * HEAD_PAD
            q_ref[:, b0:b0 + 128] = q[:, b0:b0 + 128].astype(BF16)
            q_ref[:, b0 + 128:b0 + 256] = _rope(q[:, b0 + 128:b0 + 256], c, sa, sb, 1.0).astype(BF16)
            k_ref[:, b0:b0 + 128] = kv[:, b0:b0 + 128].astype(BF16)
            k_ref[:, b0 + 128:b0 + 256] = kr
            v_ref[:, hh * 128:(hh + 1) * 128] = kv[:, b0 + 128:b0 + 256].astype(BF16)

    row = lambda w: pl.BlockSpec((tm, w), lambda i: (i, 0))
    f = lambda w, dt: jax.ShapeDtypeStruct((T, w), dt)
    return pl.pallas_call(
        body, name="fwd_proj", grid=(n_steps,),
        in_specs=[row(D_MODEL), _full(w_in_t.shape), _full(w_uq_t.shape), _full(w_ukv.shape),
                  _full(gq.shape), _full(gkv.shape), row(128), ANY],
        out_specs=(row(512), row(256), row(512), row(512), row(512), row(1024), row(1024), row(512), ANY),
        out_shape=(f(512, BF16), f(256, BF16), f(512, BF16), f(512, BF16), f(512, BF16),
                   f(1024, BF16), f(1024, BF16), f(512, BF16),
                   jax.ShapeDtypeStruct(w_out_slots.shape, BF16)),
        input_output_aliases={7: 8},
        scratch_shapes=[pltpu.SemaphoreType.DMA((6,)), pltpu.SemaphoreType.DMA((6,))],
        compiler_params=_cparams(1),
    )(x, w_in_t, w_uq_t, w_ukv, gq, gkv, rope_tab, w_out_slots)


def _attn_fwd(q, k, v, pos_col, pos_row, bounds, nb, S, tq, tk):
    T = q.shape[0]
    nq, nk = S // tq, S // tk
    reps = tk // 128
    hg = HEAD_GROUP_FWD

    def body(qmin_ref, qmax_ref, kmin_ref, kmax_ref, q_ref, k_ref, v_ref, pc_ref, pr_ref, o_ref, lse_ref,
             m_sc, l_sc, acc_sc):
        b, i = pl.program_id(0), pl.program_id(2)
        m_sc[...] = jnp.full(m_sc.shape, NEG, F32)
        l_sc[...] = jnp.zeros_like(l_sc)
        acc_sc[...] = jnp.zeros_like(acc_sc)
        q_lo = qmin_ref[b * nq + i]
        q_hi = qmax_ref[b * nq + i]

        def tile(j, masked):
            off = pl.multiple_of(j * tk, tk)
            if masked:
                keep = pc_ref[...] >= pr_ref[pl.ds(j, 1), :]
            logits = []
            for g in range(hg):
                qk = slice(g * HEAD_PAD, (g + 1) * HEAD_PAD)
                s = _dot_nt(q_ref[:, qk], k_ref[pl.ds(off, tk), qk])
                if masked:
                    s = jnp.where(keep, s, NEG)
                logits.append(s)
            probs = []
            for g in range(hg):
                hv = slice(g * 128, (g + 1) * 128)
                s = logits[g]
                m_prev = m_sc[:, hv]
                m_new = jnp.maximum(m_prev, jnp.max(s, axis=1, keepdims=True))
                p = jnp.exp2(s - jnp.concatenate([m_new] * reps, axis=1))
                a = jnp.exp2(m_prev - m_new)
                l_sc[:, hv] = a * l_sc[:, hv] + jnp.sum(p, axis=1, keepdims=True)
                m_sc[:, hv] = m_new
                probs.append((p.astype(BF16), a))
            for g in range(hg):
                hv = slice(g * 128, (g + 1) * 128)
                p, a = probs[g]
                acc_sc[:, hv] = a * acc_sc[:, hv] + _dot(p, v_ref[pl.ds(off, tk), hv])

        def step(j, carry):
            visible = kmin_ref[b * nk + j] <= q_hi
            clear = q_lo >= kmax_ref[b * nk + j]

            @pl.when(jnp.logical_and(visible, clear))
            def _():
                tile(j, False)

            @pl.when(jnp.logical_and(visible, jnp.logical_not(clear)))
            def _():
                tile(j, True)
            return carry

        lax.fori_loop(0, nk, step, 0)
        l = l_sc[...]
        o_ref[...] = acc_sc[...] / l
        lse_ref[...] = m_sc[...] + jnp.log2(l)

    ng = HEADS // hg
    stat = pltpu.VMEM((tq, hg * 128), F32)
    return pl.pallas_call(
        body, name="attn_fwd",
        grid_spec=pltpu.PrefetchScalarGridSpec(
            num_scalar_prefetch=4, grid=(nb, ng, nq),
            in_specs=[pl.BlockSpec((tq, hg * HEAD_PAD), lambda b, h, i, *_: (b * nq + i, h)),
                      pl.BlockSpec((S, hg * HEAD_PAD), lambda b, h, i, *_: (b, h)),
                      pl.BlockSpec((S, hg * 128), lambda b, h, i, *_: (b, h)),
                      pl.BlockSpec((tq, 1), lambda b, h, i, *_: (b * nq + i, 0)),
                      pl.BlockSpec((None, nk, tk), lambda b, h, i, *_: (b, 0, 0))],
            out_specs=(pl.BlockSpec((tq, hg * 128), lambda b, h, i, *_: (b * nq + i, h)),
                       pl.BlockSpec((tq, hg * 128), lambda b, h, i, *_: (b * nq + i, h))),
            scratch_shapes=[stat, stat, stat]),
        out_shape=(jax.ShapeDtypeStruct((T, MLA_W), F32), jax.ShapeDtypeStruct((T, MLA_W), F32)),
        compiler_params=_cparams(3),
    )(*bounds, q, k, v, pos_col, pos_row.reshape(nb, nk, tk))


def _mid(x, tgt, o, ga, u, gb, w_out, pool_w, pool_scale, ln_g, ln_b, S, tm):
    T = x.shape[0]
    tps = S // tm
    hb = tm // HALO

    def body(x_ref, tgt_ref, o_ref, ga_ref, u_ref, uh_ref, gb_ref, wout_ref, pw_ref,
             ps_ref, lng_ref, lnb_ref,
             dz_ref, do_ref, delta_ref, dga_ref, dgb_ref, dpc_ref,
             dwout_ref, dpw_ref, dps_ref, dlng_ref, dlnb_ref, loss_ref):
        i = pl.program_id(0)

        @pl.when(i == 0)
        def _():
            dwout_ref[...] = jnp.zeros_like(dwout_ref)
            dpw_ref[...] = jnp.zeros_like(dpw_ref)
            dps_ref[...] = jnp.zeros_like(dps_ref)
            dlng_ref[...] = jnp.zeros_like(dlng_ref)
            dlnb_ref[...] = jnp.zeros_like(dlnb_ref)
            loss_ref[...] = jnp.zeros_like(loss_ref)

        seq_tile = i % tps
        tpos = seq_tile * tm + lax.broadcasted_iota(jnp.int32, (tm, 1), 0)
        ga_v = ga_ref[...].astype(F32)
        sig_a = jax.nn.sigmoid(ga_v)
        silu_a = ga_v * sig_a
        o_v = o_ref[...]
        ya = o_v * silu_a

        u_v = u_ref[...].astype(F32)
        halo = jnp.where(seq_tile == 0, 0.0, uh_ref[...].astype(F32))
        pooled, cnts, mixed = [], [], []
        for g in range(POOL_G):
            lanes = slice(g * POOL_GD, (g + 1) * POOL_GD)
            w = jnp.concatenate([halo[:, lanes], u_v[:, lanes]], axis=0)
            for st in range(g + 1):
                w = w + pltpu.roll(w, 1 << st, 0)
            cnt = jnp.minimum(tpos + 1, 2 << g).astype(F32)
            pg = (w[HALO:, :] / cnt - u_v[:, lanes]).astype(BF16)
            pooled.append(pg)
            cnts.append(cnt)
            mixed.append(_dot(pg, pw_ref[g]))
        mixed = jnp.concatenate(mixed, axis=1)
        ps = ps_ref[...]
        ybp = mixed * ps
        gb_v = gb_ref[...].astype(F32)
        sig_b = jax.nn.sigmoid(gb_v)
        silu_b = gb_v * sig_b
        yb = ybp * silu_b

        cat = jnp.concatenate([ya, yb], axis=1).astype(BF16)
        z = ALPHA * x_ref[...] + _dot(cat, wout_ref[...])
        mu = jnp.mean(z, axis=-1, keepdims=True)
        zc = z - mu
        rstd = lax.rsqrt(jnp.mean(zc * zc, axis=-1, keepdims=True) + LN_EPS)
        zhat = zc * rstd
        lng = lng_ref[...]
        err = (zhat * lng + lnb_ref[...]) - tgt_ref[...]
        row_loss = jnp.sum(err * err, axis=1, keepdims=True)
        loss_ref[...] += jnp.broadcast_to(jnp.sum(row_loss, axis=0, keepdims=True) * (0.5 / D_MODEL), (1, 128))
        dy = err * (1.0 / D_MODEL)
        dlng_ref[...] += jnp.sum(dy * zhat, axis=0, keepdims=True)
        dlnb_ref[...] += jnp.sum(dy, axis=0, keepdims=True)
        dzh = dy * lng
        dz = rstd * (dzh - jnp.mean(dzh, axis=-1, keepdims=True)
                     - zhat * jnp.mean(dzh * zhat, axis=-1, keepdims=True))
        dz_ref[...] = dz
        dzb = dz.astype(BF16)
        dwout_ref[...] += _dot_tn(cat, dzb)
        dcat = _dot_nt(dzb, wout_ref[...])
        dya = dcat[:, :MLA_W]
        dyb = dcat[:, MLA_W:]

        do = dya * silu_a
        do_ref[...] = do.astype(BF16)
        prod = do * o_v
        for hh in range(HEADS):
            lanes = slice(hh * 128, (hh + 1) * 128)
            delta_ref[:, lanes] = jnp.broadcast_to(jnp.sum(prod[:, lanes], axis=1, keepdims=True), (tm, 128))
        dga_ref[...] = (dya * o_v * (sig_a * (1.0 + ga_v * (1.0 - sig_a)))).astype(BF16)
        dgb_ref[...] = (dyb * ybp * (sig_b * (1.0 + gb_v * (1.0 - sig_b)))).astype(BF16)
        dybp = dyb * silu_b
        dps_ref[...] += jnp.sum(dybp * mixed, axis=0, keepdims=True)
        dmixed = (dybp * ps).astype(BF16)
        for g in range(POOL_G):
            lanes = slice(g * POOL_GD, (g + 1) * POOL_GD)
            dpw_ref[g] += _dot_tn(pooled[g], dmixed[:, lanes])
            dpc_ref[:, lanes] = (_dot_nt(dmixed[:, lanes], pw_ref[g]) / cnts[g]).astype(BF16)

    row = lambda w: pl.BlockSpec((tm, w), lambda i: (i, 0))
    f = lambda w, dt: jax.ShapeDtypeStruct((T, w), dt)
    halo_spec = pl.BlockSpec((HALO, POOL_W), lambda i: (jnp.maximum(i * hb - 1, 0), 0))
    return pl.pallas_call(
        body, name="mid", grid=(T // tm,),
        in_specs=[row(D_MODEL), row(D_MODEL), row(MLA_W), row(MLA_W), row(POOL_W), halo_spec, row(POOL_W),
                  _full(w_out.shape), _full(pool_w.shape),
                  _full(pool_scale.shape), _full(ln_g.shape), _full(ln_b.shape)],
        out_specs=(row(D_MODEL), row(MLA_W), row(MLA_W), row(MLA_W), row(POOL_W), row(POOL_W),
                   _full((D_MODEL, D_MODEL)), _full(pool_w.shape), _full((1, POOL_W)),
                   _full((1, D_MODEL)), _full((1, D_MODEL)), _full((1, 128))),
        out_shape=(f(D_MODEL, F32), f(MLA_W, BF16), f(MLA_W, F32), f(MLA_W, BF16), f(POOL_W, BF16), f(POOL_W, BF16),
                   jax.ShapeDtypeStruct((D_MODEL, D_MODEL), F32), jax.ShapeDtypeStruct(pool_w.shape, F32),
                   jax.ShapeDtypeStruct((1, POOL_W), F32), jax.ShapeDtypeStruct((1, D_MODEL), F32),
                   jax.ShapeDtypeStruct((1, D_MODEL), F32), jax.ShapeDtypeStruct((1, 128), F32)),
        compiler_params=_cparams(1),
    )(x, tgt, o, ga, u, u, gb, w_out, pool_w, pool_scale, ln_g, ln_b)


def _attn_bwd(q, k, v, do, lse, delta, pos_col, pos_row, bounds, early, early_wire, nb, S, tq, tk):
    T = q.shape[0]
    ne = len(early)
    nq, nk = S // tq, S // tk
    reps = tk // 128
    hg = HEAD_GROUP
    ng = HEADS // hg

    def body(qmin_ref, qmax_ref, kmin_ref, kmax_ref, q_ref, k_ref, v_ref, do_ref, lse_ref, dl_ref, pc_ref, pr_ref,
             *rest):
        early_in, rest = rest[:ne], rest[ne:]
        dq_out, dk_out, dv_out = rest[:3]
        early_out, rest = rest[3:3 + ne], rest[3 + ne:]
        dq_ref, dk_ref, dv_ref = rest[:3]
        b, j = pl.program_id(0), pl.program_id(2)
        flat = (b * ng + pl.program_id(1)) * nk + j
        last = nb * ng * nk - 1
        when = [0, min(3, last), min(max(5 * (last + 1) // 8, 3), last), last]
        for at, phase in zip(when, _reduce_phases(early, early_wire, early_in, early_out, rest[3:])):
            pl.when(flat == at)(phase)

        @pl.when(j == 0)
        def _():
            dq_ref[...] = jnp.zeros_like(dq_ref)

        dk_ref[...] = jnp.zeros_like(dk_ref)
        dv_ref[...] = jnp.zeros_like(dv_ref)
        k_lo = kmin_ref[b * nk + j]
        k_hi = kmax_ref[b * nk + j]

        def tile(i, masked):
            rows = pl.ds(pl.multiple_of(i * tq, tq), tq)
            if masked:
                keep = pc_ref[rows, :] >= pr_ref[...]
            stage = []
            for g in range(hg):
                qk = slice(g * HEAD_PAD, (g + 1) * HEAD_PAD)
                hv = slice(g * 128, (g + 1) * 128)
                s = _dot_nt(q_ref[rows, qk], k_ref[:, qk])
                if masked:
                    s = jnp.where(keep, s, NEG)
                stage.append((s, _dot_nt(do_ref[rows, hv], v_ref[:, hv])))
            grads = []
            for g in range(hg):
                hv = slice(g * 128, (g + 1) * 128)
                s, dp = stage[g]
                p = jnp.exp2(s - jnp.concatenate([lse_ref[rows, hv]] * reps, axis=1))
                ds = (p * (dp - jnp.concatenate([dl_ref[rows, hv]] * reps, axis=1))).astype(BF16)
                grads.append((p.astype(BF16), ds))
            for g in range(hg):
                qk = slice(g * HEAD_PAD, (g + 1) * HEAD_PAD)
                hv = slice(g * 128, (g + 1) * 128)
                p, ds = grads[g]
                dv_ref[:, hv] += _dot_tn(p, do_ref[rows, hv])
                dq_ref[rows, qk] += _dot(ds, k_ref[:, qk])
                dk_ref[:, qk] += _dot_tn(ds, q_ref[rows, qk])

        def step(i, carry):
            visible = k_lo <= qmax_ref[b * nq + i]
            clear = qmin_ref[b * nq + i] >= k_hi

            @pl.when(jnp.logical_and(visible, clear))
            def _():
                tile(i, False)

            @pl.when(jnp.logical_and(visible, jnp.logical_not(clear)))
            def _():
                tile(i, True)
            return carry

        lax.fori_loop(0, nq, step, 0)
        dk_out[...] = dk_ref[...].astype(BF16)
        dv_out[...] = dv_ref[...].astype(BF16)

        @pl.when(j == nk - 1)
        def _():
            dq_out[...] = dq_ref[...].astype(BF16)

    seq = lambda w: pl.BlockSpec((S, w), lambda b, h, j, *_: (b, h))
    blk = lambda w: pl.BlockSpec((tk, w), lambda b, h, j, *_: (b * nk + j, h))
    outs = pl.pallas_call(
        body, name="attn_bwd",
        grid_spec=pltpu.PrefetchScalarGridSpec(
            num_scalar_prefetch=4, grid=(nb, ng, nk),
            in_specs=[seq(hg * HEAD_PAD), blk(hg * HEAD_PAD), blk(hg * 128),
                      seq(hg * 128), seq(hg * 128), seq(hg * 128),
                      pl.BlockSpec((S, 1), lambda b, h, j, *_: (b, 0)),
                      pl.BlockSpec((None, 1, tk), lambda b, h, j, *_: (b, 0, j))] + [ANY] * ne,
            out_specs=(seq(hg * HEAD_PAD), blk(hg * HEAD_PAD), blk(hg * 128)) + (ANY,) * ne,
            scratch_shapes=[pltpu.VMEM((S, hg * HEAD_PAD), F32), pltpu.VMEM((tk, hg * HEAD_PAD), F32),
                            pltpu.VMEM((tk, hg * 128), F32)] + _reduce_scratch(early, early_wire)),
        out_shape=(jax.ShapeDtypeStruct((T, HEADS * HEAD_PAD), BF16),
                   jax.ShapeDtypeStruct((T, HEADS * HEAD_PAD), BF16),
                   jax.ShapeDtypeStruct((T, MLA_W), BF16)) + tuple(jax.ShapeDtypeStruct(g.shape[1:], F32) for g in early),
        compiler_params=_cparams(3),
    )(*bounds, q, k, v, do, lse, delta, pos_col, pos_row, *early)
    return outs[:3], outs[3:]


def _bwd_proj(dq, dk, dv, xq, xkv, x, dz, dga, dgb, dpc, rope_tab, w_uq_t, w_ukv, w_in_t, gq, gkv, S, tm):
    T = x.shape[0]
    tps = S // tm
    hb = tm // HALO
    n_tiles = T // tm

    def body(dq_ref, dk_ref, dv_ref, xq_ref, xkv_ref, x_ref, dz_ref, dga_ref, dgb_ref, dpc_ref, dph_ref,
             tab_ref, wuq_ref, wukv_ref, win_ref, gq_ref, gkv_ref,
             dx_ref, dwin_hbm, dwuq_hbm, dwukv_hbm, dgq_ref, dgkv_ref,
             acc_win, acc_wuq, acc_wukv, dh_sc):
        i = pl.program_id(0)

        @pl.when(i == 0)
        def _():
            acc_win[...] = jnp.zeros_like(acc_win)
            acc_wuq[...] = jnp.zeros_like(acc_wuq)
            acc_wukv[...] = jnp.zeros_like(acc_wukv)
            dgq_ref[...] = jnp.zeros_like(dgq_ref)
            dgkv_ref[...] = jnp.zeros_like(dgkv_ref)
            dh_sc[...] = jnp.zeros_like(dh_sc)

        dh_prev = dh_sc[...]
        dx_ref[...] = ALPHA * dz_ref[...] + _dot(dh_prev, win_ref[...])
        acc_win[...] += _dot_tn(dh_prev, x_ref[...].astype(BF16))

        live = jnp.where(i < n_tiles, 1.0, 0.0)
        c, sa, sb = _expand_rope_table(tab_ref[...])
        dq_v = dq_ref[...].astype(F32) * (SCALE * live)
        dk_v = dk_ref[...].astype(F32) * (LN2 * live)
        dv_v = dv_ref[...].astype(F32) * live
        dq_parts, dkv_parts = [], []
        dkr = jnp.zeros((tm, 128), F32)
        for hh in range(HEADS):
            b0 = hh * HEAD_PAD
            dq_parts.append(dq_v[:, b0:b0 + 128].astype(BF16))
            dq_parts.append(_rope(dq_v[:, b0 + 128:b0 + 256], c, sa, sb, -1.0).astype(BF16))
            dkv_parts.append(dk_v[:, b0:b0 + 128].astype(BF16))
            dkv_parts.append(dv_v[:, hh * 128:(hh + 1) * 128].astype(BF16))
            dkr = dkr + dk_v[:, b0 + 128:b0 + 256]
        dqp = jnp.concatenate(dq_parts, axis=1)
        dkvp = jnp.concatenate(dkv_parts, axis=1)
        dkrr = _rope(dkr, c, sa, sb, -1.0)

        def rms_bwd(xv, g, dyn, dg_ref):
            r = lax.rsqrt(jnp.mean(xv * xv, axis=-1, keepdims=True) + RMS_EPS)
            xhat = xv * r
            dg_ref[...] += jnp.sum(dyn * xhat, axis=0, keepdims=True)
            dxh = dyn * g
            return r * (dxh - xhat * jnp.mean(dxh * xhat, axis=-1, keepdims=True))

        xq_v = xq_ref[...].astype(F32)
        gq_v = gq_ref[...]
        rq = lax.rsqrt(jnp.mean(xq_v * xq_v, axis=-1, keepdims=True) + RMS_EPS)
        acc_wuq[...] += _dot_tn(dqp, ((xq_v * rq) * gq_v).astype(BF16))
        dxq = rms_bwd(xq_v, gq_v, _dot(dqp, wuq_ref[...]), dgq_ref)

        xkv_v = xkv_ref[...].astype(F32)
        gkv_v = gkv_ref[...]
        rkv = lax.rsqrt(jnp.mean(xkv_v * xkv_v, axis=-1, keepdims=True) + RMS_EPS)
        acc_wukv[...] += _dot_tn(((xkv_v * rkv) * gkv_v).astype(BF16), dkvp)
        dxkv = rms_bwd(xkv_v, gkv_v, _dot_nt(dkvp, wukv_ref[...]), dgkv_ref)

        seq_tile = i % tps
        tpos = seq_tile * tm + lax.broadcasted_iota(jnp.int32, (tm, 1), 0)
        dpc_v = dpc_ref[...].astype(F32)
        halo = jnp.where(seq_tile == tps - 1, 0.0, dph_ref[...].astype(F32))
        n = tm + HALO
        du = []
        for g in range(POOL_G):
            lanes = slice(g * POOL_GD, (g + 1) * POOL_GD)
            f = jnp.concatenate([dpc_v[:, lanes], halo[:, lanes]], axis=0)
            for st in range(g + 1):
                f = f + pltpu.roll(f, n - (1 << st), 0)
            cnt = jnp.minimum(tpos + 1, 2 << g).astype(F32)
            du.append((f[:tm, :] - dpc_v[:, lanes] * cnt).astype(BF16))

        dh_sc[...] = jnp.concatenate([dxq.astype(BF16), dxkv.astype(BF16), dkrr.astype(BF16), dga_ref[...]]
                                     + du + [dgb_ref[...]], axis=1)

        @pl.when(i == n_tiles)
        def _():
            pltpu.sync_copy(acc_win.at[pl.ds(0, ROPE_END)], dwin_hbm.at[pl.ds(0, ROPE_END)])
            pltpu.sync_copy(acc_win.at[pl.ds(COL_GA, IN_EXT - COL_GA)], dwin_hbm.at[pl.ds(ROPE_END, IN_W - ROPE_END)])
            for hh in range(HEADS):
                pltpu.sync_copy(acc_wuq.at[pl.ds(hh * HEAD_PAD, NOPE + ROPE)], dwuq_hbm.at[hh])
            pltpu.sync_copy(acc_wukv, dwukv_hbm)

    cur = lambda w: pl.BlockSpec((tm, w), lambda i: (jnp.minimum(i, n_tiles - 1), 0))
    prev = lambda w: pl.BlockSpec((tm, w), lambda i: (jnp.maximum(i - 1, 0), 0))
    halo_spec = pl.BlockSpec((HALO, POOL_W), lambda i: (jnp.minimum((i + 1) * hb, T // HALO - 1), 0))
    return pl.pallas_call(
        body, name="bwd_proj", grid=(n_tiles + 1,),
        in_specs=[cur(1024), cur(1024), cur(512), cur(512), cur(256), prev(D_MODEL), prev(D_MODEL),
                  cur(512), cur(512), cur(512), halo_spec, cur(128),
                  _full(w_uq_t.shape), _full(w_ukv.shape), _full(w_in_t.shape), _full(gq.shape), _full(gkv.shape)],
        out_specs=(prev(D_MODEL), ANY, ANY, ANY, _full((1, Q_LORA)), _full((1, KV_LORA))),
        out_shape=(jax.ShapeDtypeStruct((T, D_MODEL), F32),
                   jax.ShapeDtypeStruct((IN_W, D_MODEL), F32),
                   jax.ShapeDtypeStruct((HEADS, NOPE + ROPE, Q_LORA), F32),
                   jax.ShapeDtypeStruct((KV_LORA, 1024), F32),
                   jax.ShapeDtypeStruct((1, Q_LORA), F32), jax.ShapeDtypeStruct((1, KV_LORA), F32)),
        scratch_shapes=[pltpu.VMEM((IN_EXT, D_MODEL), F32), pltpu.VMEM((HEADS * HEAD_PAD, Q_LORA), F32),
                        pltpu.VMEM((KV_LORA, 1024), F32), pltpu.VMEM((tm, IN_EXT), BF16)],
        compiler_params=_cparams(1),
    )(dq, dk, dv, xq, xkv, x, dz, dga, dgb, dpc, dpc, rope_tab, w_uq_t, w_ukv, w_in_t, gq, gkv)


def kernel(x, positions, w_in, q_norm_g, w_uq, kv_norm_g, w_ukv, pool_w, pool_scale, w_out, ln_g, ln_b, loss_target, m_w_in, m_q_norm_g, m_w_uq, m_kv_norm_g, m_w_ukv, m_pool_w, m_pool_scale, m_w_out, m_ln_g, m_ln_b, v_w_in, v_q_norm_g, v_w_uq, v_kv_norm_g, v_w_ukv, v_pool_w, v_pool_scale, v_w_out, v_ln_g, v_ln_b):
    nb, S, _ = x.shape
    T = nb * S
    tm = min(256, S)
    tq = min(512, S)
    tk = min(512, S)
    assert S % tm == 0 and tm % HALO == 0 and S % tq == 0 and S % tk == 0

    cx, cy, cc = lax.axis_index("x"), lax.axis_index("y"), lax.axis_index("c")
    me = 2 * cx + cy

    half = ROPE // 2
    inv_freq = ROPE_THETA ** (-jnp.arange(half, dtype=F32) / half)
    freq_row = jnp.concatenate([inv_freq, inv_freq, jnp.zeros((2 * half,), F32)]).reshape(1, 128)
    pos_col = positions.reshape(T, 1)
    pos_row = positions.reshape(nb, 1, S)
    pos_q = positions.reshape(nb, S // tq, tq)
    pos_k = positions.reshape(nb, S // tk, tk)
    bounds = (jnp.min(pos_q, axis=2).reshape(-1), jnp.max(pos_q, axis=2).reshape(-1),
              jnp.min(pos_k, axis=2).reshape(-1), jnp.max(pos_k, axis=2).reshape(-1))

    def own_slot(w, slot_rows):
        blk = jnp.pad(w.astype(BF16), ((0, slot_rows - w.shape[0]), (0, 0)))
        return lax.dynamic_update_slice(jnp.zeros((N_CHIPS,) + blk.shape, BF16), blk[None], (me, 0, 0))

    (w_in_g, w_uq_g, w_ukv_g), rope_tab = _weight_gather(
        [own_slot(w_in.T, IN_SHARD), own_slot(w_uq.T, HEAD_PAD), own_slot(w_ukv, KV_LORA)], (IN_SHARD, NOPE + ROPE, KV_LORA),
        pos_col, freq_row)
    w_in_f = w_in_g.reshape(IN_W, D_MODEL)
    w_in_t = jnp.concatenate([w_in_f[:ROPE_END], jnp.zeros((COL_GA - ROPE_END, D_MODEL), BF16), w_in_f[ROPE_END:]], axis=0)
    w_uq_t = w_uq_g.reshape(HEADS * HEAD_PAD, Q_LORA)
    w_ukv_f = w_ukv_g.transpose(1, 0, 2).reshape(KV_LORA, 1024)
    pool_w_b = pool_w.astype(BF16)
    gq2 = q_norm_g.reshape(1, Q_LORA)
    gkv2 = kv_norm_g.reshape(1, KV_LORA)
    ps2 = pool_scale.reshape(1, POOL_W)

    xf = x.reshape(T, D_MODEL)
    tgt = loss_target.reshape(T, D_MODEL)

    xq, xkv, ga, u, gb, q, k, v, w_out_g = _fwd_proj(
        xf, w_in_t, w_uq_t, w_ukv_f, gq2, gkv2, rope_tab, own_slot(w_out, 256), tm)
    w_out_f = w_out_g.reshape(D_MODEL, D_MODEL)
    o, lse = _attn_fwd(q, k, v, pos_col, pos_row, bounds, nb, S, tq, tk)

    (dz, do, delta, dga, dgb, dpc, d_w_out, d_pool_w, d_pool_scale, d_ln_g, d_ln_b, loss_part) = _mid(
        xf, tgt, o, ga, u, gb, w_out_f, pool_w_b, ps2, ln_g, ln_b, S, tm)

    wide = lambda a: jnp.pad(a.reshape(1, -1), ((0, 0), (0, D_MODEL - a.size)))
    blank = lambda r: jnp.zeros((r, D_MODEL), F32)
    to_all = lambda a: jnp.broadcast_to(a[None], (N_CHIPS,) + a.shape)
    vec_early = jnp.concatenate([d_ln_g, d_ln_b, wide(d_pool_scale), blank(6), wide(loss_part), blank(VEC_ROWS - 10)], axis=0)
    early = [d_w_out.reshape(N_CHIPS, 256, D_MODEL), to_all(d_pool_w.reshape(-1, D_MODEL)), to_all(vec_early)]
    (dq, dk, dv), (g_out, pw_sum, vec_early_sum) = _attn_bwd(
        q, k, v, do, lse, delta, pos_col, pos_row, bounds, early, (BF16, F32, F32), nb, S, tq, tk)
    dx, d_w_in_t, d_w_uq_t, d_w_ukv, d_gq, d_gkv = _bwd_proj(
        dq, dk, dv, xq, xkv, xf, dz, dga, dgb, dpc, rope_tab, w_uq_t, w_ukv_f, w_in_t, gq2, gkv2, S, tm)
    grad_x = dx.reshape(nb, S, D_MODEL)

    g_in = d_w_in_t.reshape(N_CHIPS, IN_SHARD, D_MODEL)
    g_uq = d_w_uq_t
    g_ukv = d_w_ukv.reshape(KV_LORA, N_CHIPS, 256).transpose(1, 0, 2)
    vec_late = jnp.concatenate([blank(3), wide(d_gq), blank(4), wide(d_gkv), blank(VEC_ROWS - 9)], axis=0)
    g_in, g_uq, g_ukv, vec_late_sum = _grad_reduce([g_in, g_uq, g_ukv, to_all(vec_late)], (BF16, BF16, BF16, F32))
    g_big = [g_in, g_uq, g_ukv, g_out]
    pw_sum = pw_sum.reshape(POOL_G * POOL_GD, POOL_GD)
    vec_sum = vec_early_sum + vec_late_sum

    big = _adamw_big(g_big, [w_in.T, w_uq.T, w_ukv, w_out], [m_w_in.T, m_w_uq.T, m_w_ukv, m_w_out],
                     [v_w_in.T, v_w_uq.T, v_w_ukv, v_w_out])
    two_d = lambda a: a.reshape(-1, a.shape[-1])
    small_names = lambda pw, lg, lb, ps, gq, gkv: [two_d(pw), lg, lb, ps.reshape(1, -1), gq.reshape(1, -1), gkv.reshape(1, -1)]
    small, loss_row = _adamw_small(
        pw_sum, vec_sum,
        small_names(pool_w, ln_g, ln_b, pool_scale, q_norm_g, kv_norm_g),
        small_names(m_pool_w, m_ln_g, m_ln_b, m_pool_scale, m_q_norm_g, m_kv_norm_g),
        small_names(v_pool_w, v_ln_g, v_ln_b, v_pool_scale, v_q_norm_g, v_kv_norm_g))
    loss = loss_row[0, 0]

    def leaves(kind):
        b = [g_big[t] if kind == 0 else big[t][kind - 1] for t in range(N_BIG)]
        b = [b[0].T, b[1].T, b[2], b[3]]
        s = [small[t][kind] for t in range(6)]
        return (b[0], s[4].reshape(Q_LORA), b[1], s[5].reshape(KV_LORA), b[2],
                s[0].reshape(POOL_G, POOL_GD, POOL_GD), s[3].reshape(POOL_W), b[3], s[1], s[2])

    return (loss, grad_x) + leaves(0) + leaves(1) + leaves(2) + leaves(3)
```

```python
import jax
import jax.numpy as jnp
from jax import lax
from jax.experimental import pallas as pl
from jax.experimental.pallas import tpu as pltpu

F32 = jnp.float32
BF16 = jnp.bfloat16
MESH = pl.DeviceIdType.MESH

HEADS = 4
NOPE = 128
ROPE = 64
HEAD_PAD = 256
Q_LORA = 512
KV_LORA = 256
MLA_W = 512
POOL_W = 512
POOL_G = 4
POOL_GD = 128
D_MODEL = 1024
IN_W = 2368
IN_EXT = 2432
COL_KV, COL_KR, COL_GA, COL_U, COL_GB = 512, 768, 896, 1408, 1920
ROPE_END = COL_KR + 64
IN_SHARD = IN_W // 4
ROPE_THETA = 10000.0
RMS_EPS = 1e-6
LN_EPS = 1e-5
ALPHA = 2.0 ** 0.25
SCALE = 192.0 ** -0.5
LOG2E = 1.4426950408889634
LN2 = 0.6931471805599453
QSCALE = SCALE * LOG2E
NEG = float(jnp.finfo(jnp.float32).min)
HEAD_GROUP = 2
HEAD_GROUP_FWD = 4
HALO = 16

ADAM_LR = 0.001
ADAM_B1 = 0.9
ADAM_B2 = 0.999
ADAM_EPS = 1e-08
ADAM_WD = 0.01
ADAM_STEP = 10

N_CHIPS = 4
N_BIG = 4
VEC_ROWS = 16

VMEM_LIMIT = 56 * 1024 * 1024


def _cparams(n_grid_dims=0, **kw):
    sem = ("arbitrary",) * n_grid_dims if n_grid_dims else None
    return pltpu.CompilerParams(dimension_semantics=sem, vmem_limit_bytes=VMEM_LIMIT, **kw)


def _full(shape):
    nd = len(shape)
    return pl.BlockSpec(shape, lambda *_: (0,) * nd)


def _dot(a, b):
    return jnp.dot(a, b, preferred_element_type=F32)


def _dot_nt(a, b):
    return lax.dot_general(a, b, (((1,), (1,)), ((), ())), preferred_element_type=F32)


def _dot_tn(a, b):
    return lax.dot_general(a, b, (((0,), (0,)), ((), ())), preferred_element_type=F32)


def _rope_table(pos_col, freq_row):
    lane = lax.broadcasted_iota(jnp.int32, (1, 128), 1)
    ang = pos_col.astype(F32) * freq_row
    return jnp.where(lane < 32, jnp.cos(ang), jnp.where(lane < 64, jnp.sin(ang), 0.0))


def _expand_rope_table(tab):
    lane = lax.broadcasted_iota(jnp.int32, (1, 128), 1)
    second = jnp.logical_and(lane >= 32, lane < 64)
    c = jnp.where(lane < 32, tab, jnp.where(second, pltpu.roll(tab, 32, 1), 0.0))
    sa = jnp.where(lane < 32, pltpu.roll(tab, 96, 1), 0.0)
    sb = jnp.where(second, tab, 0.0)
    return c, sa, sb


def _rope(g, c, sa, sb, sign):
    return g * c + sign * (pltpu.roll(g, 32, 1) * sb - pltpu.roll(g, 96, 1) * sa)


def _place():
    x, y, c = lax.axis_index("x"), lax.axis_index("y"), lax.axis_index("c")
    chips = [(1 - x, y), (x, 1 - y), (1 - x, 1 - y)]
    return x, y, c, chips


ANY = pl.BlockSpec(memory_space=pl.ANY)


ROPE_CHUNK = 2048


def _weight_gather(slots, valid_rows, pos_col, freq_row):
    n = len(slots)
    T = pos_col.shape[0]
    chunk = min(ROPE_CHUNK, T)
    assert T % chunk == 0

    def body(*refs):
        pos_hbm, freq_ref = refs[n:n + 2]
        outs = refs[n + 2:2 * n + 2]
        tab_hbm = refs[2 * n + 2]
        send_sems, recv_sems, pos_buf, tab_buf = refs[2 * n + 3:]
        x, y, c, chips = _place()
        me = 2 * x + y

        def copy(t, k, chip_idx, half, to):
            hc = slots[t].shape[2] // 2
            blk = outs[t].at[chip_idx, pl.ds(0, valid_rows[t]), pl.ds(half * hc, hc)]
            return pltpu.make_async_remote_copy(
                src_ref=blk, dst_ref=blk, send_sem=send_sems.at[6 * t + k], recv_sem=recv_sems.at[6 * t + k],
                device_id=to, device_id_type=MESH)

        first = [copy(t, j, me, c, (cx, cy, c)) for t in range(n) for j, (cx, cy) in enumerate(chips)]
        for cp in first:
            cp.start()

        def table_chunk(r, carry):
            rows = pl.ds(pl.multiple_of(r * chunk, chunk), chunk)
            pltpu.sync_copy(pos_hbm.at[rows], pos_buf)
            tab_buf[...] = _rope_table(pos_buf[...], freq_ref[...])
            pltpu.sync_copy(tab_buf, tab_hbm.at[rows])
            return carry

        lax.fori_loop(0, T // chunk, table_chunk, 0)
        passed = []
        for j, (cx, cy) in enumerate(chips):
            for t in range(n):
                copy(t, j, 2 * cx + cy, c, (x, y, c)).wait_recv()
                fwd = copy(t, 3 + j, 2 * cx + cy, c, (x, y, 1 - c))
                fwd.start()
                passed.append(fwd)
        for j, (cx, cy) in enumerate(chips):
            for t in range(n):
                copy(t, 3 + j, 2 * cx + cy, 1 - c, (x, y, c)).wait_recv()
        for cp in first + passed:
            cp.wait_send()

    outs = pl.pallas_call(
        body, name="weight_gather",
        out_shape=tuple(jax.ShapeDtypeStruct(a.shape, a.dtype) for a in slots) + (jax.ShapeDtypeStruct((T, 128), F32),),
        in_specs=[ANY] * n + [ANY, pl.BlockSpec(memory_space=pltpu.VMEM)], out_specs=(ANY,) * (n + 1),
        input_output_aliases={t: t for t in range(n)},
        scratch_shapes=[pltpu.SemaphoreType.DMA((6 * n,)), pltpu.SemaphoreType.DMA((6 * n,)),
                        pltpu.VMEM((chunk, 1), jnp.int32), pltpu.VMEM((chunk, 128), F32)],
    )(*slots, pos_col, freq_row)
    return outs[:n], outs[n]


def _reduce_scratch(gs, wire_dtypes):
    n = len(gs)
    half = [(g.shape[1], g.shape[2] // 2) for g in gs]
    return ([pltpu.VMEM((4,) + h, F32) for h in half] + [pltpu.VMEM((4,) + h, F32) for h in half]
            + [pltpu.VMEM((3,) + h, w) for h, w in zip(half, wire_dtypes)]
            + [pltpu.VMEM((3,) + h, w) for h, w in zip(half, wire_dtypes)]
            + [pltpu.VMEM(h, F32) for h in half]
            + [pltpu.SemaphoreType.DMA((4 * n,)), pltpu.SemaphoreType.DMA((4 * n,)),
               pltpu.SemaphoreType.DMA((3 * n,)), pltpu.SemaphoreType.DMA((3 * n,)),
               pltpu.SemaphoreType.DMA((n,)), pltpu.SemaphoreType.DMA((n,)),
               pltpu.SemaphoreType.DMA((4 * n,)), pltpu.SemaphoreType.DMA((n,))])


def _reduce_phases(gs, wire_dtypes, g_refs, out_refs, scr):
    n = len(gs)
    hcs = [g.shape[2] // 2 for g in gs]
    own, sib, wire, got, fin = (scr[i * n:(i + 1) * n] for i in range(5))
    d2d_send, d2d_recv, ici_send, ici_recv, fin_send, fin_recv, loc_in, loc_out = scr[5 * n:]

    def place():
        x, y, c, chips = _place()
        return c, chips, (x, y, 1 - c), [2 * cx + cy for cx, cy in chips] + [2 * x + y]

    def remote(src, dst, send, recv, to):
        return pltpu.make_async_remote_copy(src_ref=src, dst_ref=dst, send_sem=send, recv_sem=recv,
                                            device_id=to, device_id_type=MESH)

    def block(ref, t, half, lead=None):
        cols = pl.ds(half * hcs[t], hcs[t])
        rows = pl.ds(0, gs[t].shape[1])
        return ref.at[rows, cols] if lead is None else ref.at[lead, rows, cols]

    def load(t, j):
        c, _, _, dests = place()
        return pltpu.make_async_copy(block(g_refs[t], t, c, dests[j]), own[t].at[j], loc_in.at[4 * t + j])

    def d2d(t, j):
        c, _, sibling, dests = place()
        return remote(block(g_refs[t], t, 1 - c, dests[j]), sib[t].at[j],
                      d2d_send.at[4 * t + j], d2d_recv.at[4 * t + j], sibling)

    def ici(t, j):
        c, chips, _, _ = place()
        return remote(wire[t].at[j], got[t].at[j], ici_send.at[3 * t + j], ici_recv.at[3 * t + j], chips[j] + (c,))

    def store(t):
        c = place()[0]
        return pltpu.make_async_copy(fin[t], block(out_refs[t], t, c), loc_out.at[t])

    def final(t, half_of):
        c, _, sibling, _ = place()
        return remote(fin[t], block(out_refs[t], t, c if half_of == "mine" else 1 - c),
                      fin_send.at[t], fin_recv.at[t], sibling)

    def start():
        for j in range(4):
            for t in range(n):
                load(t, j).start()
                d2d(t, j).start()

    def exchange():
        for j in range(3):
            for t in range(n):
                load(t, j).wait()
                d2d(t, j).wait_recv()
                wire[t][j] = (own[t][j] + sib[t][j]).astype(wire_dtypes[t])
                ici(t, j).start()

    def finish():
        for t in range(n):
            load(t, 3).wait()
            d2d(t, 3).wait_recv()
            for j in range(3):
                ici(t, j).wait_recv()
            fin[t][...] = (((own[t][3] + sib[t][3]) + got[t][0].astype(F32))
                           + (got[t][1].astype(F32) + got[t][2].astype(F32)))
            store(t).start()
            final(t, "mine").start()

    def drain():
        for t in range(n):
            final(t, "theirs").wait_recv()
        for t in range(n):
            for j in range(4):
                d2d(t, j).wait_send()
            for j in range(3):
                ici(t, j).wait_send()
            final(t, "mine").wait_send()
            store(t).wait()

    return start, exchange, finish, drain


def _grad_reduce(gs, wire_dtypes):
    n = len(gs)

    def body(*refs):
        for phase in _reduce_phases(gs, wire_dtypes, refs[:n], refs[n:2 * n], refs[2 * n:]):
            phase()

    return pl.pallas_call(
        body, name="grad_reduce",
        out_shape=tuple(jax.ShapeDtypeStruct(g.shape[1:], F32) for g in gs),
        in_specs=[ANY] * n, out_specs=(ANY,) * n, scratch_shapes=_reduce_scratch(gs, wire_dtypes),
        compiler_params=_cparams(),
    )(*gs)


def _adamw_math(g, w, m, v):
    nm = ADAM_B1 * m + (1.0 - ADAM_B1) * g
    nv = ADAM_B2 * v + (1.0 - ADAM_B2) * (g * g)
    m_hat = nm / (1.0 - ADAM_B1 ** ADAM_STEP)
    v_hat = nv / (1.0 - ADAM_B2 ** ADAM_STEP)
    return -ADAM_LR * (m_hat / (jnp.sqrt(v_hat) + ADAM_EPS) + ADAM_WD * w), nm, nv


ADAM_STEPS = 8


def _adamw_big(gs, ws, ms, vs):
    n = len(gs)

    def body(*refs):
        for t in range(n):
            d, nm, nv = _adamw_math(refs[t][...], refs[n + t][...], refs[2 * n + t][...], refs[3 * n + t][...])
            refs[4 * n + 3 * t][...] = d
            refs[4 * n + 3 * t + 1][...] = nm
            refs[4 * n + 3 * t + 2][...] = nv

    def tile_spec(shape):
        rows, cols = shape
        if rows % (8 * ADAM_STEPS) == 0:
            return pl.BlockSpec((rows // ADAM_STEPS, cols), lambda i: (i, 0))
        return pl.BlockSpec((rows, cols // ADAM_STEPS), lambda i: (0, i))

    specs = [tile_spec(g.shape) for g in gs]
    out_specs, out_shape = [], []
    for t in range(n):
        out_specs += [specs[t]] * 3
        out_shape += [jax.ShapeDtypeStruct(gs[t].shape, F32)] * 3
    outs = pl.pallas_call(
        body, name="adamw_big", grid=(ADAM_STEPS,),
        in_specs=specs * 4, out_specs=tuple(out_specs), out_shape=tuple(out_shape),
        compiler_params=_cparams(1),
    )(*gs, *ws, *ms, *vs)
    return [outs[3 * t: 3 * t + 3] for t in range(n)]


def _adamw_small(pw_sum, vec_sum, ws, ms, vs):
    rows = (None, 0, 1, 2, 3, 8)
    n = len(ws)

    def body(pw_ref, vec_ref, *refs):
        outs = refs[3 * n:]
        for t in range(n):
            w_ref, m_ref, v_ref = refs[t], refs[n + t], refs[2 * n + t]
            if rows[t] is None:
                g = pw_ref[...]
            else:
                g = vec_ref[rows[t]:rows[t] + 1, 0:w_ref.shape[1]]
            d, nm, nv = _adamw_math(g, w_ref[...], m_ref[...], v_ref[...])
            outs[4 * t][...] = g
            outs[4 * t + 1][...] = d
            outs[4 * t + 2][...] = nm
            outs[4 * t + 3][...] = nv
        outs[4 * n][...] = vec_ref[9:10, 0:128]

    vm = pl.BlockSpec(memory_space=pltpu.VMEM)
    out_shape = []
    for w in ws:
        out_shape += [jax.ShapeDtypeStruct(w.shape, F32)] * 4
    out_shape.append(jax.ShapeDtypeStruct((1, 128), F32))
    outs = pl.pallas_call(
        body, name="adamw_small", in_specs=[vm] * (2 + 3 * n), out_specs=(vm,) * (4 * n + 1),
        out_shape=tuple(out_shape),
    )(pw_sum, vec_sum, *ws, *ms, *vs)
    return [outs[4 * t: 4 * t + 4] for t in range(n)], outs[4 * n]


def _fwd_proj(x, w_in_t, w_uq_t, w_ukv, gq, gkv, rope_tab, w_out_slots, tm):
    T = x.shape[0]
    n_steps = T // tm
    fwd_step = n_steps // 2

    def body(x_ref, win_ref, wuq_ref, wukv_ref, gq_ref, gkv_ref, tab_ref, wo_in,
             xq_ref, xkv_ref, ga_ref, u_ref, gb_ref, q_ref, k_ref, v_ref, wo_ref, send_sems, recv_sems):
        i = pl.program_id(0)
        px, py, pc, chips = _place()
        hc = D_MODEL // 2

        def wo_copy(k, chip_idx, half, to):
            blk = wo_ref.at[chip_idx, pl.ds(0, 256), pl.ds(half * hc, hc)]
            return pltpu.make_async_remote_copy(src_ref=blk, dst_ref=blk, send_sem=send_sems.at[k],
                                                recv_sem=recv_sems.at[k], device_id=to, device_id_type=MESH)

        @pl.when(i == 0)
        def _():
            for j, (cx, cy) in enumerate(chips):
                wo_copy(j, 2 * px + py, pc, (cx, cy, pc)).start()

        @pl.when(i == fwd_step)
        def _():
            for j, (cx, cy) in enumerate(chips):
                wo_copy(j, 2 * cx + cy, pc, (px, py, pc)).wait_recv()
                wo_copy(3 + j, 2 * cx + cy, pc, (px, py, 1 - pc)).start()

        @pl.when(i == n_steps - 1)
        def _():
            for j, (cx, cy) in enumerate(chips):
                wo_copy(3 + j, 2 * cx + cy, 1 - pc, (px, py, pc)).wait_recv()
            for j, (cx, cy) in enumerate(chips):
                wo_copy(j, 2 * px + py, pc, (cx, cy, pc)).wait_send()
                wo_copy(3 + j, 2 * cx + cy, pc, (px, py, 1 - pc)).wait_send()

        xb = x_ref[...].astype(BF16)
        h_lat = _dot_nt(xb, win_ref[0:COL_KR, :])
        h_rope = _dot_nt(xb, win_ref[COL_KR:COL_GA, :])
        h_gate = _dot_nt(xb, win_ref[ROPE_END:IN_W, :])
        xq = h_lat[:, 0:COL_KV]
        xkv = h_lat[:, COL_KV:COL_KR]
        xq_ref[...] = xq.astype(BF16)
        xkv_ref[...] = xkv.astype(BF16)
        ga_ref[...] = h_gate[:, 0:MLA_W].astype(BF16)
        u_ref[...] = h_gate[:, MLA_W:MLA_W + POOL_W].astype(BF16)
        gb_ref[...] = h_gate[:, MLA_W + POOL_W:].astype(BF16)
        c, sa, sb = _expand_rope_table(tab_ref[...])
        rq = lax.rsqrt(jnp.mean(xq * xq, axis=-1, keepdims=True) + RMS_EPS)
        q = _dot_nt(((xq * rq) * gq_ref[...]).astype(BF16), wuq_ref[...]) * QSCALE
        rkv = lax.rsqrt(jnp.mean(xkv * xkv, axis=-1, keepdims=True) + RMS_EPS)
        kv = _dot(((xkv * rkv) * gkv_ref[...]).astype(BF16), wukv_ref[...])
        kr = _rope(h_rope, c, sa, sb, 1.0).astype(BF16)
        for hh in range(HEADS):
            b0 = hh * HEAD_PAD
            q_ref[:, b0:b0 + 128] = q[:, b0:b0 + 128].astype(BF16)
            q_ref[:, b0 + 128:b0 + 256] = _rope(q[:, b0 + 128:b0 + 256], c, sa, sb, 1.0).astype(BF16)
            k_ref[:, b0:b0 + 128] = kv[:, b0:b0 + 128].astype(BF16)
            k_ref[:, b0 + 128:b0 + 256] = kr
            v_ref[:, hh * 128:(hh + 1) * 128] = kv[:, b0 + 128:b0 + 256].astype(BF16)

    row = lambda w: pl.BlockSpec((tm, w), lambda i: (i, 0))
    f = lambda w, dt: jax.ShapeDtypeStruct((T, w), dt)
    return pl.pallas_call(
        body, name="fwd_proj", grid=(n_steps,),
        in_specs=[row(D_MODEL), _full(w_in_t.shape), _full(w_uq_t.shape), _full(w_ukv.shape),
                  _full(gq.shape), _full(gkv.shape), row(128), ANY],
        out_specs=(row(512), row(256), row(512), row(512), row(512), row(1024), row(1024), row(512), ANY),
        out_shape=(f(512, BF16), f(256, BF16), f(512, BF16), f(512, BF16), f(512, BF16),
                   f(1024, BF16), f(1024, BF16), f(512, BF16),
                   jax.ShapeDtypeStruct(w_out_slots.shape, BF16)),
        input_output_aliases={7: 8},
        scratch_shapes=[pltpu.SemaphoreType.DMA((6,)), pltpu.SemaphoreType.DMA((6,))],
        compiler_params=_cparams(1),
    )(x, w_in_t, w_uq_t, w_ukv, gq, gkv, rope_tab, w_out_slots)


def _attn_fwd(q, k, v, pos_col, pos_row, bounds, nb, S, tq, tk):
    T = q.shape[0]
    nq, nk = S // tq, S // tk
    reps = tk // 128
    hg = HEAD_GROUP_FWD

    def body(qmin_ref, qmax_ref, kmin_ref, kmax_ref, q_ref, k_ref, v_ref, pc_ref, pr_ref, o_ref, lse_ref,
             m_sc, l_sc, acc_sc):
        b, i = pl.program_id(0), pl.program_id(2)
        m_sc[...] = jnp.full(m_sc.shape, NEG, F32)
        l_sc[...] = jnp.zeros_like(l_sc)
        acc_sc[...] = jnp.zeros_like(acc_sc)
        q_lo = qmin_ref[b * nq + i]
        q_hi = qmax_ref[b * nq + i]

        def tile(j, masked):
            off = pl.multiple_of(j * tk, tk)
            if masked:
                keep = pc_ref[...] >= pr_ref[pl.ds(j, 1), :]
            logits = []
            for g in range(hg):
                qk = slice(g * HEAD_PAD, (g + 1) * HEAD_PAD)
                s = _dot_nt(q_ref[:, qk], k_ref[pl.ds(off, tk), qk])
                if masked:
                    s = jnp.where(keep, s, NEG)
                logits.append(s)
            probs = []
            for g in range(hg):
                hv = slice(g * 128, (g + 1) * 128)
                s = logits[g]
                m_prev = m_sc[:, hv]
                m_new = jnp.maximum(m_prev, jnp.max(s, axis=1, keepdims=True))
                p = jnp.exp2(s - jnp.concatenate([m_new] * reps, axis=1))
                a = jnp.exp2(m_prev - m_new)
                l_sc[:, hv] = a * l_sc[:, hv] + jnp.sum(p, axis=1, keepdims=True)
                m_sc[:, hv] = m_new
                probs.append((p.astype(BF16), a))
            for g in range(hg):
                hv = slice(g * 128, (g + 1) * 128)
                p, a = probs[g]
                acc_sc[:, hv] = a * acc_sc[:, hv] + _dot(p, v_ref[pl.ds(off, tk), hv])

        def step(j, carry):
            visible = kmin_ref[b * nk + j] <= q_hi
            clear = q_lo >= kmax_ref[b * nk + j]

            @pl.when(jnp.logical_and(visible, clear))
            def _():
                tile(j, False)

            @pl.when(jnp.logical_and(visible, jnp.logical_not(clear)))
            def _():
                tile(j, True)
            return carry

        lax.fori_loop(0, nk, step, 0)
        l = l_sc[...]
        o_ref[...] = acc_sc[...] / l
        lse_ref[...] = m_sc[...] + jnp.log2(l)

    ng = HEADS // hg
    stat = pltpu.VMEM((tq, hg * 128), F32)
    return pl.pallas_call(
        body, name="attn_fwd",
        grid_spec=pltpu.PrefetchScalarGridSpec(
            num_scalar_prefetch=4, grid=(nb, ng, nq),
            in_specs=[pl.BlockSpec((tq, hg * HEAD_PAD), lambda b, h, i, *_: (b * nq + i, h)),
                      pl.BlockSpec((S, hg * HEAD_PAD), lambda b, h, i, *_: (b, h)),
                      pl.BlockSpec((S, hg * 128), lambda b, h, i, *_: (b, h)),
                      pl.BlockSpec((tq, 1), lambda b, h, i, *_: (b * nq + i, 0)),
                      pl.BlockSpec((None, nk, tk), lambda b, h, i, *_: (b, 0, 0))],
            out_specs=(pl.BlockSpec((tq, hg * 128), lambda b, h, i, *_: (b * nq + i, h)),
                       pl.BlockSpec((tq, hg * 128), lambda b, h, i, *_: (b * nq + i, h))),
            scratch_shapes=[stat, stat, stat]),
        out_shape=(jax.ShapeDtypeStruct((T, MLA_W), F32), jax.ShapeDtypeStruct((T, MLA_W), F32)),
        compiler_params=_cparams(3),
    )(*bounds, q, k, v, pos_col, pos_row.reshape(nb, nk, tk))


def _mid(x, tgt, o, ga, u, gb, w_out, pool_w, pool_scale, ln_g, ln_b, S, tm):
    T = x.shape[0]
    tps = S // tm
    hb = tm // HALO

    def body(x_ref, tgt_ref, o_ref, ga_ref, u_ref, uh_ref, gb_ref, wout_ref, pw_ref,
             ps_ref, lng_ref, lnb_ref,
             dz_ref, do_ref, delta_ref, dga_ref, dgb_ref, dpc_ref,
             dwout_ref, dpw_ref, dps_ref, dlng_ref, dlnb_ref, loss_ref):
        i = pl.program_id(0)

        @pl.when(i == 0)
        def _():
            dwout_ref[...] = jnp.zeros_like(dwout_ref)
            dpw_ref[...] = jnp.zeros_like(dpw_ref)
            dps_ref[...] = jnp.zeros_like(dps_ref)
            dlng_ref[...] = jnp.zeros_like(dlng_ref)
            dlnb_ref[...] = jnp.zeros_like(dlnb_ref)
            loss_ref[...] = jnp.zeros_like(loss_ref)

        seq_tile = i % tps
        tpos = seq_tile * tm + lax.broadcasted_iota(jnp.int32, (tm, 1), 0)
        ga_v = ga_ref[...].astype(F32)
        sig_a = jax.nn.sigmoid(ga_v)
        silu_a = ga_v * sig_a
        o_v = o_ref[...]
        ya = o_v * silu_a

        u_v = u_ref[...].astype(F32)
        halo = jnp.where(seq_tile == 0, 0.0, uh_ref[...].astype(F32))
        pooled, cnts, mixed = [], [], []
        for g in range(POOL_G):
            lanes = slice(g * POOL_GD, (g + 1) * POOL_GD)
            w = jnp.concatenate([halo[:, lanes], u_v[:, lanes]], axis=0)
            for st in range(g + 1):
                w = w + pltpu.roll(w, 1 << st, 0)
            cnt = jnp.minimum(tpos + 1, 2 << g).astype(F32)
            pg = (w[HALO:, :] / cnt - u_v[:, lanes]).astype(BF16)
            pooled.append(pg)
            cnts.append(cnt)
            mixed.append(_dot(pg, pw_ref[g]))
        mixed = jnp.concatenate(mixed, axis=1)
        ps = ps_ref[...]
        ybp = mixed * ps
        gb_v = gb_ref[...].astype(F32)
        sig_b = jax.nn.sigmoid(gb_v)
        silu_b = gb_v * sig_b
        yb = ybp * silu_b

        cat = jnp.concatenate([ya, yb], axis=1).astype(BF16)
        z = ALPHA * x_ref[...] + _dot(cat, wout_ref[...])
        mu = jnp.mean(z, axis=-1, keepdims=True)
        zc = z - mu
        rstd = lax.rsqrt(jnp.mean(zc * zc, axis=-1, keepdims=True) + LN_EPS)
        zhat = zc * rstd
        lng = lng_ref[...]
        err = (zhat * lng + lnb_ref[...]) - tgt_ref[...]
        row_loss = jnp.sum(err * err, axis=1, keepdims=True)
        loss_ref[...] += jnp.broadcast_to(jnp.sum(row_loss, axis=0, keepdims=True) * (0.5 / D_MODEL), (1, 128))
        dy = err * (1.0 / D_MODEL)
        dlng_ref[...] += jnp.sum(dy * zhat, axis=0, keepdims=True)
        dlnb_ref[...] += jnp.sum(dy, axis=0, keepdims=True)
        dzh = dy * lng
        dz = rstd * (dzh - jnp.mean(dzh, axis=-1, keepdims=True)
                     - zhat * jnp.mean(dzh * zhat, axis=-1, keepdims=True))
        dz_ref[...] = dz
        dzb = dz.astype(BF16)
        dwout_ref[...] += _dot_tn(cat, dzb)
        dcat = _dot_nt(dzb, wout_ref[...])
        dya = dcat[:, :MLA_W]
        dyb = dcat[:, MLA_W:]

        do = dya * silu_a
        do_ref[...] = do.astype(BF16)
        prod = do * o_v
        for hh in range(HEADS):
            lanes = slice(hh * 128, (hh + 1) * 128)
            delta_ref[:, lanes] = jnp.broadcast_to(jnp.sum(prod[:, lanes], axis=1, keepdims=True), (tm, 128))
        dga_ref[...] = (dya * o_v * (sig_a * (1.0 + ga_v * (1.0 - sig_a)))).astype(BF16)
        dgb_ref[...] = (dyb * ybp * (sig_b * (1.0 + gb_v * (1.0 - sig_b)))).astype(BF16)
        dybp = dyb * silu_b
        dps_ref[...] += jnp.sum(dybp * mixed, axis=0, keepdims=True)
        dmixed = (dybp * ps).astype(BF16)
        for g in range(POOL_G):
            lanes = slice(g * POOL_GD, (g + 1) * POOL_GD)
            dpw_ref[g] += _dot_tn(pooled[g], dmixed[:, lanes])
            dpc_ref[:, lanes] = (_dot_nt(dmixed[:, lanes], pw_ref[g]) / cnts[g]).astype(BF16)

    row = lambda w: pl.BlockSpec((tm, w), lambda i: (i, 0))
    f = lambda w, dt: jax.ShapeDtypeStruct((T, w), dt)
    halo_spec = pl.BlockSpec((HALO, POOL_W), lambda i: (jnp.maximum(i * hb - 1, 0), 0))
    return pl.pallas_call(
        body, name="mid", grid=(T // tm,),
        in_specs=[row(D_MODEL), row(D_MODEL), row(MLA_W), row(MLA_W), row(POOL_W), halo_spec, row(POOL_W),
                  _full(w_out.shape), _full(pool_w.shape),
                  _full(pool_scale.shape), _full(ln_g.shape), _full(ln_b.shape)],
        out_specs=(row(D_MODEL), row(MLA_W), row(MLA_W), row(MLA_W), row(POOL_W), row(POOL_W),
                   _full((D_MODEL, D_MODEL)), _full(pool_w.shape), _full((1, POOL_W)),
                   _full((1, D_MODEL)), _full((1, D_MODEL)), _full((1, 128))),
        out_shape=(f(D_MODEL, F32), f(MLA_W, BF16), f(MLA_W, F32), f(MLA_W, BF16), f(POOL_W, BF16), f(POOL_W, BF16),
                   jax.ShapeDtypeStruct((D_MODEL, D_MODEL), F32), jax.ShapeDtypeStruct(pool_w.shape, F32),
                   jax.ShapeDtypeStruct((1, POOL_W), F32), jax.ShapeDtypeStruct((1, D_MODEL), F32),
                   jax.ShapeDtypeStruct((1, D_MODEL), F32), jax.ShapeDtypeStruct((1, 128), F32)),
        compiler_params=_cparams(1),
    )(x, tgt, o, ga, u, u, gb, w_out, pool_w, pool_scale, ln_g, ln_b)


def _attn_bwd(q, k, v, do, lse, delta, pos_col, pos_row, bounds, early, early_wire, nb, S, tq, tk):
    T = q.shape[0]
    ne = len(early)
    nq, nk = S // tq, S // tk
    reps = tk // 128
    hg = HEAD_GROUP
    ng = HEADS // hg

    def body(qmin_ref, qmax_ref, kmin_ref, kmax_ref, q_ref, k_ref, v_ref, do_ref, lse_ref, dl_ref, pc_ref, pr_ref,
             *rest):
        early_in, rest = rest[:ne], rest[ne:]
        dq_out, dk_out, dv_out = rest[:3]
        early_out, rest = rest[3:3 + ne], rest[3 + ne:]
        dq_ref, dk_ref, dv_ref = rest[:3]
        b, j = pl.program_id(0), pl.program_id(2)
        flat = (b * ng + pl.program_id(1)) * nk + j
        last = nb * ng * nk - 1
        when = [0, min(3, last), min(max(5 * (last + 1) // 8, 3), last), last]
        for at, phase in zip(when, _reduce_phases(early, early_wire, early_in, early_out, rest[3:])):
            pl.when(flat == at)(phase)

        @pl.when(j == 0)
        def _():
            dq_ref[...] = jnp.zeros_like(dq_ref)

        dk_ref[...] = jnp.zeros_like(dk_ref)
        dv_ref[...] = jnp.zeros_like(dv_ref)
        k_lo = kmin_ref[b * nk + j]
        k_hi = kmax_ref[b * nk + j]

        def tile(i, masked):
            rows = pl.ds(pl.multiple_of(i * tq, tq), tq)
            if masked:
                keep = pc_ref[rows, :] >= pr_ref[...]
            stage = []
            for g in range(hg):
                qk = slice(g * HEAD_PAD, (g + 1) * HEAD_PAD)
                hv = slice(g * 128, (g + 1) * 128)
                s = _dot_nt(q_ref[rows, qk], k_ref[:, qk])
                if masked:
                    s = jnp.where(keep, s, NEG)
                stage.append((s, _dot_nt(do_ref[rows, hv], v_ref[:, hv])))
            grads = []
            for g in range(hg):
                hv = slice(g * 128, (g + 1) * 128)
                s, dp = stage[g]
                p = jnp.exp2(s - jnp.concatenate([lse_ref[rows, hv]] * reps, axis=1))
                ds = (p * (dp - jnp.concatenate([dl_ref[rows, hv]] * reps, axis=1))).astype(BF16)
                grads.append((p.astype(BF16), ds))
            for g in range(hg):
                qk = slice(g * HEAD_PAD, (g + 1) * HEAD_PAD)
                hv = slice(g * 128, (g + 1) * 128)
                p, ds = grads[g]
                dv_ref[:, hv] += _dot_tn(p, do_ref[rows, hv])
                dq_ref[rows, qk] += _dot(ds, k_ref[:, qk])
                dk_ref[:, qk] += _dot_tn(ds, q_ref[rows, qk])

        def step(i, carry):
            visible = k_lo <= qmax_ref[b * nq + i]
            clear = qmin_ref[b * nq + i] >= k_hi

            @pl.when(jnp.logical_and(visible, clear))
            def _():
                tile(i, False)

            @pl.when(jnp.logical_and(visible, jnp.logical_not(clear)))
            def _():
                tile(i, True)
            return carry

        lax.fori_loop(0, nq, step, 0)
        dk_out[...] = dk_ref[...].astype(BF16)
        dv_out[...] = dv_ref[...].astype(BF16)

        @pl.when(j == nk - 1)
        def _():
            dq_out[...] = dq_ref[...].astype(BF16)

    seq = lambda w: pl.BlockSpec((S, w), lambda b, h, j, *_: (b, h))
    blk = lambda w: pl.BlockSpec((tk, w), lambda b, h, j, *_: (b * nk + j, h))
    outs = pl.pallas_call(
        body, name="attn_bwd",
        grid_spec=pltpu.PrefetchScalarGridSpec(
            num_scalar_prefetch=4, grid=(nb, ng, nk),
            in_specs=[seq(hg * HEAD_PAD), blk(hg * HEAD_PAD), blk(hg * 128),
                      seq(hg * 128), seq(hg * 128), seq(hg * 128),
                      pl.BlockSpec((S, 1), lambda b, h, j, *_: (b, 0)),
                      pl.BlockSpec((None, 1, tk), lambda b, h, j, *_: (b, 0, j))] + [ANY] * ne,
            out_specs=(seq(hg * HEAD_PAD), blk(hg * HEAD_PAD), blk(hg * 128)) + (ANY,) * ne,
            scratch_shapes=[pltpu.VMEM((S, hg * HEAD_PAD), F32), pltpu.VMEM((tk, hg * HEAD_PAD), F32),
                            pltpu.VMEM((tk, hg * 128), F32)] + _reduce_scratch(early, early_wire)),
        out_shape=(jax.ShapeDtypeStruct((T, HEADS * HEAD_PAD), BF16),
                   jax.ShapeDtypeStruct((T, HEADS * HEAD_PAD), BF16),
                   jax.ShapeDtypeStruct((T, MLA_W), BF16)) + tuple(jax.ShapeDtypeStruct(g.shape[1:], F32) for g in early),
        compiler_params=_cparams(3),
    )(*bounds, q, k, v, do, lse, delta, pos_col, pos_row, *early)
    return outs[:3], outs[3:]


def _bwd_proj(dq, dk, dv, xq, xkv, x, dz, dga, dgb, dpc, rope_tab, w_uq_t, w_ukv, w_in_t, gq, gkv, S, tm):
    T = x.shape[0]
    tps = S // tm
    hb = tm // HALO
    n_tiles = T // tm

    def body(dq_ref, dk_ref, dv_ref, xq_ref, xkv_ref, x_ref, dz_ref, dga_ref, dgb_ref, dpc_ref, dph_ref,
             tab_ref, wuq_ref, wukv_ref, win_ref, gq_ref, gkv_ref,
             dx_ref, dwin_hbm, dwuq_hbm, dwukv_hbm, dgq_ref, dgkv_ref,
             acc_win, acc_wuq, acc_wukv, dh_sc):
        i = pl.program_id(0)

        @pl.when(i == 0)
        def _():
            acc_win[...] = jnp.zeros_like(acc_win)
            acc_wuq[...] = jnp.zeros_like(acc_wuq)
            acc_wukv[...] = jnp.zeros_like(acc_wukv)
            dgq_ref[...] = jnp.zeros_like(dgq_ref)
            dgkv_ref[...] = jnp.zeros_like(dgkv_ref)
            dh_sc[...] = jnp.zeros_like(dh_sc)

        dh_prev = dh_sc[...]
        dx_ref[...] = (ALPHA * dz_ref[...] + _dot(dh_prev[:, 0:COL_GA], win_ref[0:COL_GA, :])
                       + _dot(dh_prev[:, COL_GA:], win_ref[ROPE_END:IN_W, :]))
        acc_win[...] += _dot_tn(dh_prev, x_ref[...].astype(BF16))

        live = jnp.where(i < n_tiles, 1.0, 0.0)
        c, sa, sb = _expand_rope_table(tab_ref[...])
        dq_v = dq_ref[...].astype(F32) * (SCALE * live)
        dk_v = dk_ref[...].astype(F32) * (LN2 * live)
        dv_v = dv_ref[...].astype(F32) * live
        dq_parts, dkv_parts = [], []
        dkr = jnp.zeros((tm, 128), F32)
        for hh in range(HEADS):
            b0 = hh * HEAD_PAD
            dq_parts.append(dq_v[:, b0:b0 + 128].astype(BF16))
            dq_parts.append(_rope(dq_v[:, b0 + 128:b0 + 256], c, sa, sb, -1.0).astype(BF16))
            dkv_parts.append(dk_v[:, b0:b0 + 128].astype(BF16))
            dkv_parts.append(dv_v[:, hh * 128:(hh + 1) * 128].astype(BF16))
            dkr = dkr + dk_v[:, b0 + 128:b0 + 256]
        dqp = jnp.concatenate(dq_parts, axis=1)
        dkvp = jnp.concatenate(dkv_parts, axis=1)
        dkrr = _rope(dkr, c, sa, sb, -1.0)

        def rms_bwd(xv, g, dyn, dg_ref):
            r = lax.rsqrt(jnp.mean(xv * xv, axis=-1, keepdims=True) + RMS_EPS)
            xhat = xv * r
            dg_ref[...] += jnp.sum(dyn * xhat, axis=0, keepdims=True)
            dxh = dyn * g
            return r * (dxh - xhat * jnp.mean(dxh * xhat, axis=-1, keepdims=True))

        xq_v = xq_ref[...].astype(F32)
        gq_v = gq_ref[...]
        rq = lax.rsqrt(jnp.mean(xq_v * xq_v, axis=-1, keepdims=True) + RMS_EPS)
        acc_wuq[...] += _dot_tn(dqp, ((xq_v * rq) * gq_v).astype(BF16))
        dxq = rms_bwd(xq_v, gq_v, _dot(dqp, wuq_ref[...]), dgq_ref)

        xkv_v = xkv_ref[...].astype(F32)
        gkv_v = gkv_ref[...]
        rkv = lax.rsqrt(jnp.mean(xkv_v * xkv_v, axis=-1, keepdims=True) + RMS_EPS)
        acc_wukv[...] += _dot_tn(((xkv_v * rkv) * gkv_v).astype(BF16), dkvp)
        dxkv = rms_bwd(xkv_v, gkv_v, _dot_nt(dkvp, wukv_ref[...]), dgkv_ref)

        seq_tile = i % tps
        tpos = seq_tile * tm + lax.broadcasted_iota(jnp.int32, (tm, 1), 0)
        dpc_v = dpc_ref[...].astype(F32)
        halo = jnp.where(seq_tile == tps - 1, 0.0, dph_ref[...].astype(F32))
        n = tm + HALO
        du = []
        for g in range(POOL_G):
            lanes = slice(g * POOL_GD, (g + 1) * POOL_GD)
            f = jnp.concatenate([dpc_v[:, lanes], halo[:, lanes]], axis=0)
            for st in range(g + 1):
                f = f + pltpu.roll(f, n - (1 << st), 0)
            cnt = jnp.minimum(tpos + 1, 2 << g).astype(F32)
            du.append((f[:tm, :] - dpc_v[:, lanes] * cnt).astype(BF16))

        dh_sc[...] = jnp.concatenate([dxq.astype(BF16), dxkv.astype(BF16), dkrr.astype(BF16), dga_ref[...]]
                                     + du + [dgb_ref[...]], axis=1)

        @pl.when(i == n_tiles)
        def _():
            pltpu.sync_copy(acc_win.at[pl.ds(0, ROPE_END)], dwin_hbm.at[pl.ds(0, ROPE_END)])
            pltpu.sync_copy(acc_win.at[pl.ds(COL_GA, IN_EXT - COL_GA)], dwin_hbm.at[pl.ds(ROPE_END, IN_W - ROPE_END)])
            for hh in range(HEADS):
                pltpu.sync_copy(acc_wuq.at[pl.ds(hh * HEAD_PAD, NOPE + ROPE)], dwuq_hbm.at[hh])
            pltpu.sync_copy(acc_wukv, dwukv_hbm)

    cur = lambda w: pl.BlockSpec((tm, w), lambda i: (jnp.minimum(i, n_tiles - 1), 0))
    prev = lambda w: pl.BlockSpec((tm, w), lambda i: (jnp.maximum(i - 1, 0), 0))
    halo_spec = pl.BlockSpec((HALO, POOL_W), lambda i: (jnp.minimum((i + 1) * hb, T // HALO - 1), 0))
    return pl.pallas_call(
        body, name="bwd_proj", grid=(n_tiles + 1,),
        in_specs=[cur(1024), cur(1024), cur(512), cur(512), cur(256), prev(D_MODEL), prev(D_MODEL),
                  cur(512), cur(512), cur(512), halo_spec, cur(128),
                  _full(w_uq_t.shape), _full(w_ukv.shape), _full(w_in_t.shape), _full(gq.shape), _full(gkv.shape)],
        out_specs=(prev(D_MODEL), ANY, ANY, ANY, _full((1, Q_LORA)), _full((1, KV_LORA))),
        out_shape=(jax.ShapeDtypeStruct((T, D_MODEL), F32),
                   jax.ShapeDtypeStruct((IN_W, D_MODEL), F32),
                   jax.ShapeDtypeStruct((HEADS, NOPE + ROPE, Q_LORA), F32),
                   jax.ShapeDtypeStruct((KV_LORA, 1024), F32),
                   jax.ShapeDtypeStruct((1, Q_LORA), F32), jax.ShapeDtypeStruct((1, KV_LORA), F32)),
        scratch_shapes=[pltpu.VMEM((IN_EXT, D_MODEL), F32), pltpu.VMEM((HEADS * HEAD_PAD, Q_LORA), F32),
                        pltpu.VMEM((KV_LORA, 1024), F32), pltpu.VMEM((tm, IN_EXT), BF16)],
        compiler_params=_cparams(1),
    )(dq, dk, dv, xq, xkv, x, dz, dga, dgb, dpc, dpc, rope_tab, w_uq_t, w_ukv, w_in_t, gq, gkv)


def kernel(x, positions, w_in, q_norm_g, w_uq, kv_norm_g, w_ukv, pool_w, pool_scale, w_out, ln_g, ln_b, loss_target, m_w_in, m_q_norm_g, m_w_uq, m_kv_norm_g, m_w_ukv, m_pool_w, m_pool_scale, m_w_out, m_ln_g, m_ln_b, v_w_in, v_q_norm_g, v_w_uq, v_kv_norm_g, v_w_ukv, v_pool_w, v_pool_scale, v_w_out, v_ln_g, v_ln_b):
    nb, S, _ = x.shape
    T = nb * S
    tm = min(256, S)
    tq = min(512, S)
    tk = min(512, S)
    assert S % tm == 0 and tm % HALO == 0 and S % tq == 0 and S % tk == 0

    cx, cy, cc = lax.axis_index("x"), lax.axis_index("y"), lax.axis_index("c")
    me = 2 * cx + cy

    half = ROPE // 2
    inv_freq = ROPE_THETA ** (-jnp.arange(half, dtype=F32) / half)
    freq_row = jnp.concatenate([inv_freq, inv_freq, jnp.zeros((2 * half,), F32)]).reshape(1, 128)
    pos_col = positions.reshape(T, 1)
    pos_row = positions.reshape(nb, 1, S)
    pos_q = positions.reshape(nb, S // tq, tq)
    pos_k = positions.reshape(nb, S // tk, tk)
    bounds = (jnp.min(pos_q, axis=2).reshape(-1), jnp.max(pos_q, axis=2).reshape(-1),
              jnp.min(pos_k, axis=2).reshape(-1), jnp.max(pos_k, axis=2).reshape(-1))

    def own_slot(w, slot_rows):
        blk = jnp.pad(w.astype(BF16), ((0, slot_rows - w.shape[0]), (0, 0)))
        return lax.dynamic_update_slice(jnp.zeros((N_CHIPS,) + blk.shape, BF16), blk[None], (me, 0, 0))

    (w_in_g, w_uq_g, w_ukv_g), rope_tab = _weight_gather(
        [own_slot(w_in.T, IN_SHARD), own_slot(w_uq.T, HEAD_PAD), own_slot(w_ukv, KV_LORA)], (IN_SHARD, NOPE + ROPE, KV_LORA),
        pos_col, freq_row)
    w_in_t = w_in_g.reshape(IN_W, D_MODEL)
    w_uq_t = w_uq_g.reshape(HEADS * HEAD_PAD, Q_LORA)
    w_ukv_f = w_ukv_g.transpose(1, 0, 2).reshape(KV_LORA, 1024)
    pool_w_b = pool_w.astype(BF16)
    gq2 = q_norm_g.reshape(1, Q_LORA)
    gkv2 = kv_norm_g.reshape(1, KV_LORA)
    ps2 = pool_scale.reshape(1, POOL_W)

    xf = x.reshape(T, D_MODEL)
    tgt = loss_target.reshape(T, D_MODEL)

    xq, xkv, ga, u, gb, q, k, v, w_out_g = _fwd_proj(
        xf, w_in_t, w_uq_t, w_ukv_f, gq2, gkv2, rope_tab, own_slot(w_out, 256), tm)
    w_out_f = w_out_g.reshape(D_MODEL, D_MODEL)
    o, lse = _attn_fwd(q, k, v, pos_col, pos_row, bounds, nb, S, tq, tk)

    (dz, do, delta, dga, dgb, dpc, d_w_out, d_pool_w, d_pool_scale, d_ln_g, d_ln_b, loss_part) = _mid(
        xf, tgt, o, ga, u, gb, w_out_f, pool_w_b, ps2, ln_g, ln_b, S, tm)

    wide = lambda a: jnp.pad(a.reshape(1, -1), ((0, 0), (0, D_MODEL - a.size)))
    blank = lambda r: jnp.zeros((r, D_MODEL), F32)
    to_all = lambda a: jnp.broadcast_to(a[None], (N_CHIPS,) + a.shape)
    vec_early = jnp.concatenate([d_ln_g, d_ln_b, wide(d_pool_scale), blank(6), wide(loss_part), blank(VEC_ROWS - 10)], axis=0)
    early = [d_w_out.reshape(N_CHIPS, 256, D_MODEL), to_all(d_pool_w.reshape(-1, D_MODEL)), to_all(vec_early)]
    (dq, dk, dv), (g_out, pw_sum, vec_early_sum) = _attn_bwd(
        q, k, v, do, lse, delta, pos_col, pos_row, bounds, early, (BF16, F32, F32), nb, S, tq, tk)
    dx, d_w_in_t, d_w_uq_t, d_w_ukv, d_gq, d_gkv = _bwd_proj(
        dq, dk, dv, xq, xkv, xf, dz, dga, dgb, dpc, rope_tab, w_uq_t, w_ukv_f, w_in_t, gq2, gkv2, S, tm)
    grad_x = dx.reshape(nb, S, D_MODEL)

    g_in = d_w_in_t.reshape(N_CHIPS, IN_SHARD, D_MODEL)
    g_uq = d_w_uq_t
    g_ukv = d_w_ukv.reshape(KV_LORA, N_CHIPS, 256).transpose(1, 0, 2)
    vec_late = jnp.concatenate([blank(3), wide(d_gq), blank(4), wide(d_gkv), blank(VEC_ROWS - 9)], axis=0)
    g_in, g_uq, g_ukv, vec_late_sum = _grad_reduce([g_in, g_uq, g_ukv, to_all(vec_late)], (BF16, BF16, BF16, F32))
    g_big = [g_in, g_uq, g_ukv, g_out]
    pw_sum = pw_sum.reshape(POOL_G * POOL_GD, POOL_GD)
    vec_sum = vec_early_sum + vec_late_sum

    big = _adamw_big(g_big, [w_in.T, w_uq.T, w_ukv, w_out], [m_w_in.T, m_w_uq.T, m_w_ukv, m_w_out],
                     [v_w_in.T, v_w_uq.T, v_w_ukv, v_w_out])
    two_d = lambda a: a.reshape(-1, a.shape[-1])
    small_names = lambda pw, lg, lb, ps, gq, gkv: [two_d(pw), lg, lb, ps.reshape(1, -1), gq.reshape(1, -1), gkv.reshape(1, -1)]
    small, loss_row = _adamw_small(
        pw_sum, vec_sum,
        small_names(pool_w, ln_g, ln_b, pool_scale, q_norm_g, kv_norm_g),
        small_names(m_pool_w, m_ln_g, m_ln_b, m_pool_scale, m_q_norm_g, m_kv_norm_g),
        small_names(v_pool_w, v_ln_g, v_ln_b, v_pool_scale, v_q_norm_g, v_kv_norm_g))
    loss = loss_row[0, 0]

    def leaves(kind):
        b = [g_big[t] if kind == 0 else big[t][kind - 1] for t in range(N_BIG)]
        b = [b[0].T, b[1].T, b[2], b[3]]
        s = [small[t][kind] for t in range(6)]
        return (b[0], s[4].reshape(Q_LORA), b[1], s[5].reshape(KV_LORA), b[2],
                s[0].reshape(POOL_G, POOL_GD, POOL_GD), s[3].reshape(POOL_W), b[3], s[1], s[2])

    return (loss, grad_x) + leaves(0) + leaves(1) + leaves(2) + leaves(3)
```

```python
import jax
import jax.numpy as jnp
from jax import lax
from jax.experimental import pallas as pl
from jax.experimental.pallas import tpu as pltpu

F32 = jnp.float32
BF16 = jnp.bfloat16
MESH = pl.DeviceIdType.MESH

HEADS = 4
NOPE = 128
ROPE = 64
HEAD_PAD = 256
Q_LORA = 512
KV_LORA = 256
MLA_W = 512
POOL_W = 512
POOL_G = 4
POOL_GD = 128
D_MODEL = 1024
IN_W = 2368
IN_EXT = 2432
COL_KV, COL_KR, COL_GA, COL_U, COL_GB = 512, 768, 896, 1408, 1920
ROPE_END = COL_KR + 64
IN_SHARD = IN_W // 4
ROPE_THETA = 10000.0
RMS_EPS = 1e-6
LN_EPS = 1e-5
ALPHA = 2.0 ** 0.25
SCALE = 192.0 ** -0.5
LOG2E = 1.4426950408889634
LN2 = 0.6931471805599453
QSCALE = SCALE * LOG2E
NEG = float(jnp.finfo(jnp.float32).min)
HEAD_GROUP = 2
HEAD_GROUP_FWD = 4
HALO = 16

ADAM_LR = 0.001
ADAM_B1 = 0.9
ADAM_B2 = 0.999
ADAM_EPS = 1e-08
ADAM_WD = 0.01
ADAM_STEP = 10

N_CHIPS = 4
N_BIG = 4
VEC_ROWS = 16

VMEM_LIMIT = 56 * 1024 * 1024


def _cparams(n_grid_dims=0, **kw):
    sem = ("arbitrary",) * n_grid_dims if n_grid_dims else None
    return pltpu.CompilerParams(dimension_semantics=sem, vmem_limit_bytes=VMEM_LIMIT, **kw)


def _full(shape):
    nd = len(shape)
    return pl.BlockSpec(shape, lambda *_: (0,) * nd)


def _dot(a, b):
    return jnp.dot(a, b, preferred_element_type=F32)


def _dot_nt(a, b):
    return lax.dot_general(a, b, (((1,), (1,)), ((), ())), preferred_element_type=F32)


def _dot_tn(a, b):
    return lax.dot_general(a, b, (((0,), (0,)), ((), ())), preferred_element_type=F32)


def _rope_table(pos_col, freq_row):
    lane = lax.broadcasted_iota(jnp.int32, (1, 128), 1)
    ang = pos_col.astype(F32) * freq_row
    return jnp.where(lane < 32, jnp.cos(ang), jnp.where(lane < 64, jnp.sin(ang), 0.0))


def _expand_rope_table(tab):
    lane = lax.broadcasted_iota(jnp.int32, (1, 128), 1)
    second = jnp.logical_and(lane >= 32, lane < 64)
    c = jnp.where(lane < 32, tab, jnp.where(second, pltpu.roll(tab, 32, 1), 0.0))
    sa = jnp.where(lane < 32, pltpu.roll(tab, 96, 1), 0.0)
    sb = jnp.where(second, tab, 0.0)
    return c, sa, sb


def _rope(g, c, sa, sb, sign):
    return g * c + sign * (pltpu.roll(g, 32, 1) * sb - pltpu.roll(g, 96, 1) * sa)


def _place():
    x, y, c = lax.axis_index("x"), lax.axis_index("y"), lax.axis_index("c")
    chips = [(1 - x, y), (x, 1 - y), (1 - x, 1 - y)]
    return x, y, c, chips


ANY = pl.BlockSpec(memory_space=pl.ANY)


ROPE_CHUNK = 2048


def _weight_gather(slots, valid_rows, pos_col, freq_row):
    n = len(slots)
    T = pos_col.shape[0]
    chunk = min(ROPE_CHUNK, T)
    assert T % chunk == 0

    def body(*refs):
        pos_hbm, freq_ref = refs[n:n + 2]
        outs = refs[n + 2:2 * n + 2]
        tab_hbm = refs[2 * n + 2]
        send_sems, recv_sems, pos_buf, tab_buf = refs[2 * n + 3:]
        x, y, c, chips = _place()
        me = 2 * x + y

        def copy(t, k, chip_idx, half, to):
            hc = slots[t].shape[2] // 2
            blk = outs[t].at[chip_idx, pl.ds(0, valid_rows[t]), pl.ds(half * hc, hc)]
            return pltpu.make_async_remote_copy(
                src_ref=blk, dst_ref=blk, send_sem=send_sems.at[6 * t + k], recv_sem=recv_sems.at[6 * t + k],
                device_id=to, device_id_type=MESH)

        first = [copy(t, j, me, c, (cx, cy, c)) for t in range(n) for j, (cx, cy) in enumerate(chips)]
        for cp in first:
            cp.start()

        def table_chunk(r, carry):
            rows = pl.ds(pl.multiple_of(r * chunk, chunk), chunk)
            pltpu.sync_copy(pos_hbm.at[rows], pos_buf)
            tab_buf[...] = _rope_table(pos_buf[...], freq_ref[...])
            pltpu.sync_copy(tab_buf, tab_hbm.at[rows])
            return carry

        lax.fori_loop(0, T // chunk, table_chunk, 0)
        passed = []
        for j, (cx, cy) in enumerate(chips):
            for t in range(n):
                copy(t, j, 2 * cx + cy, c, (x, y, c)).wait_recv()
                fwd = copy(t, 3 + j, 2 * cx + cy, c, (x, y, 1 - c))
                fwd.start()
                passed.append(fwd)
        for j, (cx, cy) in enumerate(chips):
            for t in range(n):
                copy(t, 3 + j, 2 * cx + cy, 1 - c, (x, y, c)).wait_recv()
        for cp in first + passed:
            cp.wait_send()

    outs = pl.pallas_call(
        body, name="weight_gather",
        out_shape=tuple(jax.ShapeDtypeStruct(a.shape, a.dtype) for a in slots) + (jax.ShapeDtypeStruct((T, 128), F32),),
        in_specs=[ANY] * n + [ANY, pl.BlockSpec(memory_space=pltpu.VMEM)], out_specs=(ANY,) * (n + 1),
        input_output_aliases={t: t for t in range(n)},
        scratch_shapes=[pltpu.SemaphoreType.DMA((6 * n,)), pltpu.SemaphoreType.DMA((6 * n,)),
                        pltpu.VMEM((chunk, 1), jnp.int32), pltpu.VMEM((chunk, 128), F32)],
    )(*slots, pos_col, freq_row)
    return outs[:n], outs[n]


def _reduce_scratch(gs, wire_dtypes):
    n = len(gs)
    half = [(g.shape[1], g.shape[2] // 2) for g in gs]
    return ([pltpu.VMEM((4,) + h, F32) for h in half] + [pltpu.VMEM((4,) + h, F32) for h in half]
            + [pltpu.VMEM((3,) + h, w) for h, w in zip(half, wire_dtypes)]
            + [pltpu.VMEM((3,) + h, w) for h, w in zip(half, wire_dtypes)]
            + [pltpu.VMEM(h, F32) for h in half]
            + [pltpu.SemaphoreType.DMA((4 * n,)), pltpu.SemaphoreType.DMA((4 * n,)),
               pltpu.SemaphoreType.DMA((3 * n,)), pltpu.SemaphoreType.DMA((3 * n,)),
               pltpu.SemaphoreType.DMA((n,)), pltpu.SemaphoreType.DMA((n,)),
               pltpu.SemaphoreType.DMA((4 * n,)), pltpu.SemaphoreType.DMA((n,))])


def _reduce_phases(gs, wire_dtypes, g_refs, out_refs, scr):
    n = len(gs)
    hcs = [g.shape[2] // 2 for g in gs]
    own, sib, wire, got, fin = (scr[i * n:(i + 1) * n] for i in range(5))
    d2d_send, d2d_recv, ici_send, ici_recv, fin_send, fin_recv, loc_in, loc_out = scr[5 * n:]

    def place():
        x, y, c, chips = _place()
        return c, chips, (x, y, 1 - c), [2 * cx + cy for cx, cy in chips] + [2 * x + y]

    def remote(src, dst, send, recv, to):
        return pltpu.make_async_remote_copy(src_ref=src, dst_ref=dst, send_sem=send, recv_sem=recv,
                                            device_id=to, device_id_type=MESH)

    def block(ref, t, half, lead=None):
        cols = pl.ds(half * hcs[t], hcs[t])
        rows = pl.ds(0, gs[t].shape[1])
        return ref.at[rows, cols] if lead is None else ref.at[lead, rows, cols]

    def load(t, j):
        c, _, _, dests = place()
        return pltpu.make_async_copy(block(g_refs[t], t, c, dests[j]), own[t].at[j], loc_in.at[4 * t + j])

    def d2d(t, j):
        c, _, sibling, dests = place()
        return remote(block(g_refs[t], t, 1 - c, dests[j]), sib[t].at[j],
                      d2d_send.at[4 * t + j], d2d_recv.at[4 * t + j], sibling)

    def ici(t, j):
        c, chips, _, _ = place()
        return remote(wire[t].at[j], got[t].at[j], ici_send.at[3 * t + j], ici_recv.at[3 * t + j], chips[j] + (c,))

    def store(t):
        c = place()[0]
        return pltpu.make_async_copy(fin[t], block(out_refs[t], t, c), loc_out.at[t])

    def final(t, half_of):
        c, _, sibling, _ = place()
        return remote(fin[t], block(out_refs[t], t, c if half_of == "mine" else 1 - c),
                      fin_send.at[t], fin_recv.at[t], sibling)

    def start():
        for j in range(4):
            for t in range(n):
                load(t, j).start()
                d2d(t, j).start()

    def exchange():
        for j in range(3):
            for t in range(n):
                load(t, j).wait()
                d2d(t, j).wait_recv()
                wire[t][j] = (own[t][j] + sib[t][j]).astype(wire_dtypes[t])
                ici(t, j).start()

    def finish():
        for t in range(n):
            load(t, 3).wait()
            d2d(t, 3).wait_recv()
            for j in range(3):
                ici(t, j).wait_recv()
            fin[t][...] = (((own[t][3] + sib[t][3]) + got[t][0].astype(F32))
                           + (got[t][1].astype(F32) + got[t][2].astype(F32)))
            store(t).start()
            final(t, "mine").start()

    def drain():
        for t in range(n):
            final(t, "theirs").wait_recv()
        for t in range(n):
            for j in range(4):
                d2d(t, j).wait_send()
            for j in range(3):
                ici(t, j).wait_send()
            final(t, "mine").wait_send()
            store(t).wait()

    return start, exchange, finish, drain


def _grad_reduce(gs, wire_dtypes):
    n = len(gs)

    def body(*refs):
        for phase in _reduce_phases(gs, wire_dtypes, refs[:n], refs[n:2 * n], refs[2 * n:]):
            phase()

    return pl.pallas_call(
        body, name="grad_reduce",
        out_shape=tuple(jax.ShapeDtypeStruct(g.shape[1:], F32) for g in gs),
        in_specs=[ANY] * n, out_specs=(ANY,) * n, scratch_shapes=_reduce_scratch(gs, wire_dtypes),
        compiler_params=_cparams(),
    )(*gs)


def _adamw_math(g, w, m, v):
    nm = ADAM_B1 * m + (1.0 - ADAM_B1) * g
    nv = ADAM_B2 * v + (1.0 - ADAM_B2) * (g * g)
    m_hat = nm / (1.0 - ADAM_B1 ** ADAM_STEP)
    v_hat = nv / (1.0 - ADAM_B2 ** ADAM_STEP)
    return -ADAM_LR * (m_hat / (jnp.sqrt(v_hat) + ADAM_EPS) + ADAM_WD * w), nm, nv


ADAM_STEPS = 8


def _adamw_big(gs, ws, ms, vs):
    n = len(gs)

    def body(*refs):
        for t in range(n):
            d, nm, nv = _adamw_math(refs[t][...], refs[n + t][...], refs[2 * n + t][...], refs[3 * n + t][...])
            refs[4 * n + 3 * t][...] = d
            refs[4 * n + 3 * t + 1][...] = nm
            refs[4 * n + 3 * t + 2][...] = nv

    def tile_spec(shape):
        rows, cols = shape
        if rows % (8 * ADAM_STEPS) == 0:
            return pl.BlockSpec((rows // ADAM_STEPS, cols), lambda i: (i, 0))
        return pl.BlockSpec((rows, cols // ADAM_STEPS), lambda i: (0, i))

    specs = [tile_spec(g.shape) for g in gs]
    out_specs, out_shape = [], []
    for t in range(n):
        out_specs += [specs[t]] * 3
        out_shape += [jax.ShapeDtypeStruct(gs[t].shape, F32)] * 3
    outs = pl.pallas_call(
        body, name="adamw_big", grid=(ADAM_STEPS,),
        in_specs=specs * 4, out_specs=tuple(out_specs), out_shape=tuple(out_shape),
        compiler_params=_cparams(1),
    )(*gs, *ws, *ms, *vs)
    return [outs[3 * t: 3 * t + 3] for t in range(n)]


def _adamw_small(pw_sum, vec_sum, ws, ms, vs):
    rows = (None, 0, 1, 2, 3, 8)
    n = len(ws)

    def body(pw_ref, vec_ref, *refs):
        outs = refs[3 * n:]
        for t in range(n):
            w_ref, m_ref, v_ref = refs[t], refs[n + t], refs[2 * n + t]
            if rows[t] is None:
                g = pw_ref[...]
            else:
                g = vec_ref[rows[t]:rows[t] + 1, 0:w_ref.shape[1]]
            d, nm, nv = _adamw_math(g, w_ref[...], m_ref[...], v_ref[...])
            outs[4 * t][...] = g
            outs[4 * t + 1][...] = d
            outs[4 * t + 2][...] = nm
            outs[4 * t + 3][...] = nv
        outs[4 * n][...] = vec_ref[9:10, 0:128]

    vm = pl.BlockSpec(memory_space=pltpu.VMEM)
    out_shape = []
    for w in ws:
        out_shape += [jax.ShapeDtypeStruct(w.shape, F32)] * 4
    out_shape.append(jax.ShapeDtypeStruct((1, 128), F32))
    outs = pl.pallas_call(
        body, name="adamw_small", in_specs=[vm] * (2 + 3 * n), out_specs=(vm,) * (4 * n + 1),
        out_shape=tuple(out_shape),
    )(pw_sum, vec_sum, *ws, *ms, *vs)
    return [outs[4 * t: 4 * t + 4] for t in range(n)], outs[4 * n]


def _fwd_proj(x, w_in_t, w_uq_t, w_ukv, gq, gkv, rope_tab, w_out_slots, tm):
    T = x.shape[0]
    n_steps = T // tm
    fwd_step = n_steps // 2

    def body(x_ref, win_ref, wuq_ref, wukv_ref, gq_ref, gkv_ref, tab_ref, wo_in,
             xq_ref, xkv_ref, ga_ref, u_ref, gb_ref, q_ref, k_ref, v_ref, wo_ref, send_sems, recv_sems):
        i = pl.program_id(0)
        px, py, pc, chips = _place()
        hc = D_MODEL // 2

        def wo_copy(k, chip_idx, half, to):
            blk = wo_ref.at[chip_idx, pl.ds(0, 256), pl.ds(half * hc, hc)]
            return pltpu.make_async_remote_copy(src_ref=blk, dst_ref=blk, send_sem=send_sems.at[k],
                                                recv_sem=recv_sems.at[k], device_id=to, device_id_type=MESH)

        @pl.when(i == 0)
        def _():
            for j, (cx, cy) in enumerate(chips):
                wo_copy(j, 2 * px + py, pc, (cx, cy, pc)).start()

        @pl.when(i == fwd_step)
        def _():
            for j, (cx, cy) in enumerate(chips):
                wo_copy(j, 2 * cx + cy, pc, (px, py, pc)).wait_recv()
                wo_copy(3 + j, 2 * cx + cy, pc, (px, py, 1 - pc)).start()

        @pl.when(i == n_steps - 1)
        def _():
            for j, (cx, cy) in enumerate(chips):
                wo_copy(3 + j, 2 * cx + cy, 1 - pc, (px, py, pc)).wait_recv()
            for j, (cx, cy) in enumerate(chips):
                wo_copy(j, 2 * px + py, pc, (cx, cy, pc)).wait_send()
                wo_copy(3 + j, 2 * cx + cy, pc, (px, py, 1 - pc)).wait_send()

        xb = x_ref[...].astype(BF16)
        h_lat = _dot_nt(xb, win_ref[0:COL_KR, :])
        h_rope = _dot_nt(xb, win_ref[COL_KR:COL_GA, :])
        h_gate = _dot_nt(xb, win_ref[ROPE_END:IN_W, :])
        xq = h_lat[:, 0:COL_KV]
        xkv = h_lat[:, COL_KV:COL_KR]
        xq_ref[...] = xq.astype(BF16)
        xkv_ref[...] = xkv.astype(BF16)
        ga_ref[...] = h_gate[:, 0:MLA_W].astype(BF16)
        u_ref[...] = h_gate[:, MLA_W:MLA_W + POOL_W].astype(BF16)
        gb_ref[...] = h_gate[:, MLA_W + POOL_W:].astype(BF16)
        c, sa, sb = _expand_rope_table(tab_ref[...])
        rq = lax.rsqrt(jnp.mean(xq * xq, axis=-1, keepdims=True) + RMS_EPS)
        q = _dot_nt(((xq * rq) * gq_ref[...]).astype(BF16), wuq_ref[...]) * QSCALE
        rkv = lax.rsqrt(jnp.mean(xkv * xkv, axis=-1, keepdims=True) + RMS_EPS)
        kv = _dot(((xkv * rkv) * gkv_ref[...]).astype(BF16), wukv_ref[...])
        kr = _rope(h_rope, c, sa, sb, 1.0).astype(BF16)
        for hh in range(HEADS):
            b0 = hh * HEAD_PAD
            q_ref[:, b0:b0 + 128] = q[:, b0:b0 + 128].astype(BF16)
            q_ref[:, b0 + 128:b0 + 256] = _rope(q[:, b0 + 128:b0 + 256], c, sa, sb, 1.0).astype(BF16)
            k_ref[:, b0:b0 + 128] = kv[:, b0:b0 + 128].astype(BF16)
            k_ref[:, b0 + 128:b0 + 256] = kr
            v_ref[:, hh * 128:(hh + 1) * 128] = kv[:, b0 + 128:b0 + 256].astype(BF16)

    row = lambda w: pl.BlockSpec((tm, w), lambda i: (i, 0))
    f = lambda w, dt: jax.ShapeDtypeStruct((T, w), dt)
    return pl.pallas_call(
        body, name="fwd_proj", grid=(n_steps,),
        in_specs=[row(D_MODEL), _full(w_in_t.shape), _full(w_uq_t.shape), _full(w_ukv.shape),
                  _full(gq.shape), _full(gkv.shape), row(128), ANY],
        out_specs=(row(512), row(256), row(512), row(512), row(512), row(1024), row(1024), row(512), ANY),
        out_shape=(f(512, BF16), f(256, BF16), f(512, BF16), f(512, BF16), f(512, BF16),
                   f(1024, BF16), f(1024, BF16), f(512, BF16),
                   jax.ShapeDtypeStruct(w_out_slots.shape, BF16)),
        input_output_aliases={7: 8},
        scratch_shapes=[pltpu.SemaphoreType.DMA((6,)), pltpu.SemaphoreType.DMA((6,))],
        compiler_params=_cparams(1),
    )(x, w_in_t, w_uq_t, w_ukv, gq, gkv, rope_tab, w_out_slots)


def _attn_fwd(q, k, v, pos_col, pos_row, bounds, nb, S, tq, tk):
    T = q.shape[0]
    nq, nk = S // tq, S // tk
    hg = HEAD_GROUP_FWD

    def body(qmin_ref, qmax_ref, kmin_ref, kmax_ref, q_ref, k_ref, v_ref, pc_ref, pr_ref, o_ref, lse_ref,
             m_sc, l_sc, acc_sc):
        b, i = pl.program_id(0), pl.program_id(2)
        m_sc[...] = jnp.full(m_sc.shape, NEG, F32)
        l_sc[...] = jnp.zeros_like(l_sc)
        acc_sc[...] = jnp.zeros_like(acc_sc)
        q_lo = qmin_ref[b * nq + i]
        q_hi = qmax_ref[b * nq + i]

        def tile(j, masked):
            off = pl.multiple_of(j * tk, tk)
            if masked:
                keep = pr_ref[pl.ds(i, 1), :] >= pc_ref[pl.ds(off, tk), :]
            logits = []
            for g in range(hg):
                qk = slice(g * HEAD_PAD, (g + 1) * HEAD_PAD)
                s = _dot_nt(k_ref[pl.ds(off, tk), qk], q_ref[:, qk])
                if masked:
                    s = jnp.where(keep, s, NEG)
                logits.append(s)
            probs = []
            for g in range(hg):
                s = logits[g]
                m_prev = m_sc[g]
                m_new = jnp.maximum(m_prev, jnp.max(s, axis=0, keepdims=True))
                p = jnp.exp2(s - m_new)
                a = jnp.exp2(m_prev - m_new)
                l_sc[g] = a * l_sc[g] + jnp.sum(p, axis=0, keepdims=True)
                m_sc[g] = m_new
                probs.append((p.astype(BF16), a))
            for g in range(hg):
                hv = slice(g * 128, (g + 1) * 128)
                p, a = probs[g]
                acc_sc[g] = a * acc_sc[g] + _dot_tn(v_ref[pl.ds(off, tk), hv], p)

        def step(j, carry):
            visible = kmin_ref[b * nk + j] <= q_hi
            clear = q_lo >= kmax_ref[b * nk + j]

            @pl.when(jnp.logical_and(visible, clear))
            def _():
                tile(j, False)

            @pl.when(jnp.logical_and(visible, jnp.logical_not(clear)))
            def _():
                tile(j, True)
            return carry

        lax.fori_loop(0, nk, step, 0)
        for g in range(hg):
            hv = slice(g * 128, (g + 1) * 128)
            l = l_sc[g]
            o_ref[:, hv] = (acc_sc[g] / l).T
            lse_ref[:, hv] = jnp.broadcast_to(m_sc[g] + jnp.log2(l), (128, tq)).T

    ng = HEADS // hg
    return pl.pallas_call(
        body, name="attn_fwd",
        grid_spec=pltpu.PrefetchScalarGridSpec(
            num_scalar_prefetch=4, grid=(nb, ng, nq),
            in_specs=[pl.BlockSpec((tq, hg * HEAD_PAD), lambda b, h, i, *_: (b * nq + i, h)),
                      pl.BlockSpec((S, hg * HEAD_PAD), lambda b, h, i, *_: (b, h)),
                      pl.BlockSpec((S, hg * 128), lambda b, h, i, *_: (b, h)),
                      pl.BlockSpec((S, 1), lambda b, h, i, *_: (b, 0)),
                      pl.BlockSpec((None, nq, tq), lambda b, h, i, *_: (b, 0, 0))],
            out_specs=(pl.BlockSpec((tq, hg * 128), lambda b, h, i, *_: (b * nq + i, h)),
                       pl.BlockSpec((tq, hg * 128), lambda b, h, i, *_: (b * nq + i, h))),
            scratch_shapes=[pltpu.VMEM((hg, 1, tq), F32), pltpu.VMEM((hg, 1, tq), F32),
                            pltpu.VMEM((hg, 128, tq), F32)]),
        out_shape=(jax.ShapeDtypeStruct((T, MLA_W), F32), jax.ShapeDtypeStruct((T, MLA_W), F32)),
        compiler_params=_cparams(3),
    )(*bounds, q, k, v, pos_col, pos_row.reshape(nb, nq, tq))


def _mid(x, tgt, o, ga, u, gb, w_out, pool_w, pool_scale, ln_g, ln_b, S, tm):
    T = x.shape[0]
    tps = S // tm
    hb = tm // HALO

    def body(x_ref, tgt_ref, o_ref, ga_ref, u_ref, uh_ref, gb_ref, wout_ref, pw_ref,
             ps_ref, lng_ref, lnb_ref,
             dz_ref, do_ref, delta_ref, dga_ref, dgb_ref, dpc_ref,
             dwout_ref, dpw_ref, dps_ref, dlng_ref, dlnb_ref, loss_ref):
        i = pl.program_id(0)

        @pl.when(i == 0)
        def _():
            dwout_ref[...] = jnp.zeros_like(dwout_ref)
            dpw_ref[...] = jnp.zeros_like(dpw_ref)
            dps_ref[...] = jnp.zeros_like(dps_ref)
            dlng_ref[...] = jnp.zeros_like(dlng_ref)
            dlnb_ref[...] = jnp.zeros_like(dlnb_ref)
            loss_ref[...] = jnp.zeros_like(loss_ref)

        seq_tile = i % tps
        tpos = seq_tile * tm + lax.broadcasted_iota(jnp.int32, (tm, 1), 0)
        ga_v = ga_ref[...].astype(F32)
        sig_a = jax.nn.sigmoid(ga_v)
        silu_a = ga_v * sig_a
        o_v = o_ref[...]
        ya = o_v * silu_a

        u_v = u_ref[...].astype(F32)
        halo = jnp.where(seq_tile == 0, 0.0, uh_ref[...].astype(F32))
        pooled, cnts, mixed = [], [], []
        for g in range(POOL_G):
            lanes = slice(g * POOL_GD, (g + 1) * POOL_GD)
            w = jnp.concatenate([halo[:, lanes], u_v[:, lanes]], axis=0)
            for st in range(g + 1):
                w = w + pltpu.roll(w, 1 << st, 0)
            cnt = jnp.minimum(tpos + 1, 2 << g).astype(F32)
            pg = (w[HALO:, :] / cnt - u_v[:, lanes]).astype(BF16)
            pooled.append(pg)
            cnts.append(cnt)
            mixed.append(_dot(pg, pw_ref[g]))
        mixed = jnp.concatenate(mixed, axis=1)
        ps = ps_ref[...]
        ybp = mixed * ps
        gb_v = gb_ref[...].astype(F32)
        sig_b = jax.nn.sigmoid(gb_v)
        silu_b = gb_v * sig_b
        yb = ybp * silu_b

        cat = jnp.concatenate([ya, yb], axis=1).astype(BF16)
        z = ALPHA * x_ref[...] + _dot(cat, wout_ref[...])
        mu = jnp.mean(z, axis=-1, keepdims=True)
        zc = z - mu
        rstd = lax.rsqrt(jnp.mean(zc * zc, axis=-1, keepdims=True) + LN_EPS)
        zhat = zc * rstd
        lng = lng_ref[...]
        err = (zhat * lng + lnb_ref[...]) - tgt_ref[...]
        row_loss = jnp.sum(err * err, axis=1, keepdims=True)
        loss_ref[...] += jnp.broadcast_to(jnp.sum(row_loss, axis=0, keepdims=True) * (0.5 / D_MODEL), (1, 128))
        dy = err * (1.0 / D_MODEL)
        dlng_ref[...] += jnp.sum(dy * zhat, axis=0, keepdims=True)
        dlnb_ref[...] += jnp.sum(dy, axis=0, keepdims=True)
        dzh = dy * lng
        dz = rstd * (dzh - jnp.mean(dzh, axis=-1, keepdims=True)
                     - zhat * jnp.mean(dzh * zhat, axis=-1, keepdims=True))
        dz_ref[...] = dz
        dzb = dz.astype(BF16)
        dwout_ref[...] += _dot_tn(cat, dzb)
        dcat = _dot_nt(dzb, wout_ref[...])
        dya = dcat[:, :MLA_W]
        dyb = dcat[:, MLA_W:]

        do = dya * silu_a
        do_ref[...] = do.astype(BF16)
        prod = do * o_v
        for hh in range(HEADS):
            lanes = slice(hh * 128, (hh + 1) * 128)
            delta_ref[:, lanes] = jnp.broadcast_to(jnp.sum(prod[:, lanes], axis=1, keepdims=True), (tm, 128))
        dga_ref[...] = (dya * o_v * (sig_a * (1.0 + ga_v * (1.0 - sig_a)))).astype(BF16)
        dgb_ref[...] = (dyb * ybp * (sig_b * (1.0 + gb_v * (1.0 - sig_b)))).astype(BF16)
        dybp = dyb * silu_b
        dps_ref[...] += jnp.sum(dybp * mixed, axis=0, keepdims=True)
        dmixed = (dybp * ps).astype(BF16)
        for g in range(POOL_G):
            lanes = slice(g * POOL_GD, (g + 1) * POOL_GD)
            dpw_ref[g] += _dot_tn(pooled[g], dmixed[:, lanes])
            dpc_ref[:, lanes] = (_dot_nt(dmixed[:, lanes], pw_ref[g]) / cnts[g]).astype(BF16)

    row = lambda w: pl.BlockSpec((tm, w), lambda i: (i, 0))
    f = lambda w, dt: jax.ShapeDtypeStruct((T, w), dt)
    halo_spec = pl.BlockSpec((HALO, POOL_W), lambda i: (jnp.maximum(i * hb - 1, 0), 0))
    return pl.pallas_call(
        body, name="mid", grid=(T // tm,),
        in_specs=[row(D_MODEL), row(D_MODEL), row(MLA_W), row(MLA_W), row(POOL_W), halo_spec, row(POOL_W),
                  _full(w_out.shape), _full(pool_w.shape),
                  _full(pool_scale.shape), _full(ln_g.shape), _full(ln_b.shape)],
        out_specs=(row(D_MODEL), row(MLA_W), row(MLA_W), row(MLA_W), row(POOL_W), row(POOL_W),
                   _full((D_MODEL, D_MODEL)), _full(pool_w.shape), _full((1, POOL_W)),
                   _full((1, D_MODEL)), _full((1, D_MODEL)), _full((1, 128))),
        out_shape=(f(D_MODEL, F32), f(MLA_W, BF16), f(MLA_W, F32), f(MLA_W, BF16), f(POOL_W, BF16), f(POOL_W, BF16),
                   jax.ShapeDtypeStruct((D_MODEL, D_MODEL), F32), jax.ShapeDtypeStruct(pool_w.shape, F32),
                   jax.ShapeDtypeStruct((1, POOL_W), F32), jax.ShapeDtypeStruct((1, D_MODEL), F32),
                   jax.ShapeDtypeStruct((1, D_MODEL), F32), jax.ShapeDtypeStruct((1, 128), F32)),
        compiler_params=_cparams(1),
    )(x, tgt, o, ga, u, u, gb, w_out, pool_w, pool_scale, ln_g, ln_b)


def _attn_bwd(q, k, v, do, lse, delta, pos_col, pos_row, bounds, early, early_wire, nb, S, tq, tk):
    T = q.shape[0]
    ne = len(early)
    nq, nk = S // tq, S // tk
    reps = tk // 128
    hg = HEAD_GROUP
    ng = HEADS // hg

    def body(qmin_ref, qmax_ref, kmin_ref, kmax_ref, q_ref, k_ref, v_ref, do_ref, lse_ref, dl_ref, pc_ref, pr_ref,
             *rest):
        early_in, rest = rest[:ne], rest[ne:]
        dq_out, dk_out, dv_out = rest[:3]
        early_out, rest = rest[3:3 + ne], rest[3 + ne:]
        dq_ref, dk_ref, dv_ref = rest[:3]
        b, j = pl.program_id(0), pl.program_id(2)
        flat = (b * ng + pl.program_id(1)) * nk + j
        last = nb * ng * nk - 1
        when = [0, min(3, last), min(max(5 * (last + 1) // 8, 3), last), last]
        for at, phase in zip(when, _reduce_phases(early, early_wire, early_in, early_out, rest[3:])):
            pl.when(flat == at)(phase)

        @pl.when(j == 0)
        def _():
            dq_ref[...] = jnp.zeros_like(dq_ref)

        dk_ref[...] = jnp.zeros_like(dk_ref)
        dv_ref[...] = jnp.zeros_like(dv_ref)
        k_lo = kmin_ref[b * nk + j]
        k_hi = kmax_ref[b * nk + j]

        def tile(i, masked):
            rows = pl.ds(pl.multiple_of(i * tq, tq), tq)
            if masked:
                keep = pc_ref[rows, :] >= pr_ref[...]
            stage = []
            for g in range(hg):
                qk = slice(g * HEAD_PAD, (g + 1) * HEAD_PAD)
                hv = slice(g * 128, (g + 1) * 128)
                s = _dot_nt(q_ref[rows, qk], k_ref[:, qk])
                if masked:
                    s = jnp.where(keep, s, NEG)
                stage.append((s, _dot_nt(do_ref[rows, hv], v_ref[:, hv])))
            grads = []
            for g in range(hg):
                hv = slice(g * 128, (g + 1) * 128)
                s, dp = stage[g]
                p = jnp.exp2(s - jnp.concatenate([lse_ref[rows, hv]] * reps, axis=1))
                ds = (p * (dp - jnp.concatenate([dl_ref[rows, hv]] * reps, axis=1))).astype(BF16)
                grads.append((p.astype(BF16), ds))
            for g in range(hg):
                qk = slice(g * HEAD_PAD, (g + 1) * HEAD_PAD)
                hv = slice(g * 128, (g + 1) * 128)
                p, ds = grads[g]
                dv_ref[:, hv] += _dot_tn(p, do_ref[rows, hv])
                dq_ref[rows, qk] += _dot(ds, k_ref[:, qk])
                dk_ref[:, qk] += _dot_tn(ds, q_ref[rows, qk])

        def step(i, carry):
            visible = k_lo <= qmax_ref[b * nq + i]
            clear = qmin_ref[b * nq + i] >= k_hi

            @pl.when(jnp.logical_and(visible, clear))
            def _():
                tile(i, False)

            @pl.when(jnp.logical_and(visible, jnp.logical_not(clear)))
            def _():
                tile(i, True)
            return carry

        lax.fori_loop(0, nq, step, 0)
        dk_out[...] = dk_ref[...].astype(BF16)
        dv_out[...] = dv_ref[...].astype(BF16)

        @pl.when(j == nk - 1)
        def _():
            dq_out[...] = dq_ref[...].astype(BF16)

    seq = lambda w: pl.BlockSpec((S, w), lambda b, h, j, *_: (b, h))
    blk = lambda w: pl.BlockSpec((tk, w), lambda b, h, j, *_: (b * nk + j, h))
    outs = pl.pallas_call(
        body, name="attn_bwd",
        grid_spec=pltpu.PrefetchScalarGridSpec(
            num_scalar_prefetch=4, grid=(nb, ng, nk),
            in_specs=[seq(hg * HEAD_PAD), blk(hg * HEAD_PAD), blk(hg * 128),
                      seq(hg * 128), seq(hg * 128), seq(hg * 128),
                      pl.BlockSpec((S, 1), lambda b, h, j, *_: (b, 0)),
                      pl.BlockSpec((None, 1, tk), lambda b, h, j, *_: (b, 0, j))] + [ANY] * ne,
            out_specs=(seq(hg * HEAD_PAD), blk(hg * HEAD_PAD), blk(hg * 128)) + (ANY,) * ne,
            scratch_shapes=[pltpu.VMEM((S, hg * HEAD_PAD), F32), pltpu.VMEM((tk, hg * HEAD_PAD), F32),
                            pltpu.VMEM((tk, hg * 128), F32)] + _reduce_scratch(early, early_wire)),
        out_shape=(jax.ShapeDtypeStruct((T, HEADS * HEAD_PAD), BF16),
                   jax.ShapeDtypeStruct((T, HEADS * HEAD_PAD), BF16),
                   jax.ShapeDtypeStruct((T, MLA_W), BF16)) + tuple(jax.ShapeDtypeStruct(g.shape[1:], F32) for g in early),
        compiler_params=_cparams(3),
    )(*bounds, q, k, v, do, lse, delta, pos_col, pos_row, *early)
    return outs[:3], outs[3:]


def _bwd_proj(dq, dk, dv, xq, xkv, x, dz, dga, dgb, dpc, rope_tab, w_uq_t, w_ukv, w_in_t, gq, gkv, S, tm):
    T = x.shape[0]
    tps = S // tm
    hb = tm // HALO
    n_tiles = T // tm

    def body(dq_ref, dk_ref, dv_ref, xq_ref, xkv_ref, x_ref, dz_ref, dga_ref, dgb_ref, dpc_ref, dph_ref,
             tab_ref, wuq_ref, wukv_ref, win_ref, gq_ref, gkv_ref,
             dx_ref, dwin_hbm, dwuq_hbm, dwukv_hbm, dgq_ref, dgkv_ref,
             acc_win, acc_wuq, acc_wukv, dh_sc):
        i = pl.program_id(0)

        @pl.when(i == 0)
        def _():
            acc_win[...] = jnp.zeros_like(acc_win)
            acc_wuq[...] = jnp.zeros_like(acc_wuq)
            acc_wukv[...] = jnp.zeros_like(acc_wukv)
            dgq_ref[...] = jnp.zeros_like(dgq_ref)
            dgkv_ref[...] = jnp.zeros_like(dgkv_ref)
            dh_sc[...] = jnp.zeros_like(dh_sc)

        dh_prev = dh_sc[...]
        dx_ref[...] = (ALPHA * dz_ref[...] + _dot(dh_prev[:, 0:COL_GA], win_ref[0:COL_GA, :])
                       + _dot(dh_prev[:, COL_GA:], win_ref[ROPE_END:IN_W, :]))
        acc_win[...] += _dot_tn(dh_prev, x_ref[...].astype(BF16))

        live = jnp.where(i < n_tiles, 1.0, 0.0)
        c, sa, sb = _expand_rope_table(tab_ref[...])
        dq_v = dq_ref[...].astype(F32) * (SCALE * live)
        dk_v = dk_ref[...].astype(F32) * (LN2 * live)
        dv_v = dv_ref[...].astype(F32) * live
        dq_parts, dkv_parts = [], []
        dkr = jnp.zeros((tm, 128), F32)
        for hh in range(HEADS):
            b0 = hh * HEAD_PAD
            dq_parts.append(dq_v[:, b0:b0 + 128].astype(BF16))
            dq_parts.append(_rope(dq_v[:, b0 + 128:b0 + 256], c, sa, sb, -1.0).astype(BF16))
            dkv_parts.append(dk_v[:, b0:b0 + 128].astype(BF16))
            dkv_parts.append(dv_v[:, hh * 128:(hh + 1) * 128].astype(BF16))
            dkr = dkr + dk_v[:, b0 + 128:b0 + 256]
        dqp = jnp.concatenate(dq_parts, axis=1)
        dkvp = jnp.concatenate(dkv_parts, axis=1)
        dkrr = _rope(dkr, c, sa, sb, -1.0)

        def rms_bwd(xv, g, dyn, dg_ref):
            r = lax.rsqrt(jnp.mean(xv * xv, axis=-1, keepdims=True) + RMS_EPS)
            xhat = xv * r
            dg_ref[...] += jnp.sum(dyn * xhat, axis=0, keepdims=True)
            dxh = dyn * g
            return r * (dxh - xhat * jnp.mean(dxh * xhat, axis=-1, keepdims=True))

        xq_v = xq_ref[...].astype(F32)
        gq_v = gq_ref[...]
        rq = lax.rsqrt(jnp.mean(xq_v * xq_v, axis=-1, keepdims=True) + RMS_EPS)
        acc_wuq[...] += _dot_tn(dqp, ((xq_v * rq) * gq_v).astype(BF16))
        dxq = rms_bwd(xq_v, gq_v, _dot(dqp, wuq_ref[...]), dgq_ref)

        xkv_v = xkv_ref[...].astype(F32)
        gkv_v = gkv_ref[...]
        rkv = lax.rsqrt(jnp.mean(xkv_v * xkv_v, axis=-1, keepdims=True) + RMS_EPS)
        acc_wukv[...] += _dot_tn(((xkv_v * rkv) * gkv_v).astype(BF16), dkvp)
        dxkv = rms_bwd(xkv_v, gkv_v, _dot_nt(dkvp, wukv_ref[...]), dgkv_ref)

        seq_tile = i % tps
        tpos = seq_tile * tm + lax.broadcasted_iota(jnp.int32, (tm, 1), 0)
        dpc_v = dpc_ref[...].astype(F32)
        halo = jnp.where(seq_tile == tps - 1, 0.0, dph_ref[...].astype(F32))
        n = tm + HALO
        du = []
        for g in range(POOL_G):
            lanes = slice(g * POOL_GD, (g + 1) * POOL_GD)
            f = jnp.concatenate([dpc_v[:, lanes], halo[:, lanes]], axis=0)
            for st in range(g + 1):
                f = f + pltpu.roll(f, n - (1 << st), 0)
            cnt = jnp.minimum(tpos + 1, 2 << g).astype(F32)
            du.append((f[:tm, :] - dpc_v[:, lanes] * cnt).astype(BF16))

        dh_sc[...] = jnp.concatenate([dxq.astype(BF16), dxkv.astype(BF16), dkrr.astype(BF16), dga_ref[...]]
                                     + du + [dgb_ref[...]], axis=1)

        @pl.when(i == n_tiles)
        def _():
            pltpu.sync_copy(acc_win.at[pl.ds(0, ROPE_END)], dwin_hbm.at[pl.ds(0, ROPE_END)])
            pltpu.sync_copy(acc_win.at[pl.ds(COL_GA, IN_EXT - COL_GA)], dwin_hbm.at[pl.ds(ROPE_END, IN_W - ROPE_END)])
            for hh in range(HEADS):
                pltpu.sync_copy(acc_wuq.at[pl.ds(hh * HEAD_PAD, NOPE + ROPE)], dwuq_hbm.at[hh])
            pltpu.sync_copy(acc_wukv, dwukv_hbm)

    cur = lambda w: pl.BlockSpec((tm, w), lambda i: (jnp.minimum(i, n_tiles - 1), 0))
    prev = lambda w: pl.BlockSpec((tm, w), lambda i: (jnp.maximum(i - 1, 0), 0))
    halo_spec = pl.BlockSpec((HALO, POOL_W), lambda i: (jnp.minimum((i + 1) * hb, T // HALO - 1), 0))
    return pl.pallas_call(
        body, name="bwd_proj", grid=(n_tiles + 1,),
        in_specs=[cur(1024), cur(1024), cur(512), cur(512), cur(256), prev(D_MODEL), prev(D_MODEL),
                  cur(512), cur(512), cur(512), halo_spec, cur(128),
                  _full(w_uq_t.shape), _full(w_ukv.shape), _full(w_in_t.shape), _full(gq.shape), _full(gkv.shape)],
        out_specs=(prev(D_MODEL), ANY, ANY, ANY, _full((1, Q_LORA)), _full((1, KV_LORA))),
        out_shape=(jax.ShapeDtypeStruct((T, D_MODEL), F32),
                   jax.ShapeDtypeStruct((IN_W, D_MODEL), F32),
                   jax.ShapeDtypeStruct((HEADS, NOPE + ROPE, Q_LORA), F32),
                   jax.ShapeDtypeStruct((KV_LORA, 1024), F32),
                   jax.ShapeDtypeStruct((1, Q_LORA), F32), jax.ShapeDtypeStruct((1, KV_LORA), F32)),
        scratch_shapes=[pltpu.VMEM((IN_EXT, D_MODEL), F32), pltpu.VMEM((HEADS * HEAD_PAD, Q_LORA), F32),
                        pltpu.VMEM((KV_LORA, 1024), F32), pltpu.VMEM((tm, IN_EXT), BF16)],
        compiler_params=_cparams(1),
    )(dq, dk, dv, xq, xkv, x, dz, dga, dgb, dpc, dpc, rope_tab, w_uq_t, w_ukv, w_in_t, gq, gkv)


def kernel(x, positions, w_in, q_norm_g, w_uq, kv_norm_g, w_ukv, pool_w, pool_scale, w_out, ln_g, ln_b, loss_target, m_w_in, m_q_norm_g, m_w_uq, m_kv_norm_g, m_w_ukv, m_pool_w, m_pool_scale, m_w_out, m_ln_g, m_ln_b, v_w_in, v_q_norm_g, v_w_uq, v_kv_norm_g, v_w_ukv, v_pool_w, v_pool_scale, v_w_out, v_ln_g, v_ln_b):
    nb, S, _ = x.shape
    T = nb * S
    tm = min(256, S)
    tq = min(512, S)
    tk = min(512, S)
    assert S % tm == 0 and tm % HALO == 0 and S % tq == 0 and S % tk == 0

    cx, cy, cc = lax.axis_index("x"), lax.axis_index("y"), lax.axis_index("c")
    me = 2 * cx + cy

    half = ROPE // 2
    inv_freq = ROPE_THETA ** (-jnp.arange(half, dtype=F32) / half)
    freq_row = jnp.concatenate([inv_freq, inv_freq, jnp.zeros((2 * half,), F32)]).reshape(1, 128)
    pos_col = positions.reshape(T, 1)
    pos_row = positions.reshape(nb, 1, S)
    pos_q = positions.reshape(nb, S // tq, tq)
    pos_k = positions.reshape(nb, S // tk, tk)
    bounds = (jnp.min(pos_q, axis=2).reshape(-1), jnp.max(pos_q, axis=2).reshape(-1),
              jnp.min(pos_k, axis=2).reshape(-1), jnp.max(pos_k, axis=2).reshape(-1))

    def own_slot(w, slot_rows):
        blk = jnp.pad(w.astype(BF16), ((0, slot_rows - w.shape[0]), (0, 0)))
        return lax.dynamic_update_slice(jnp.zeros((N_CHIPS,) + blk.shape, BF16), blk[None], (me, 0, 0))

    (w_in_g, w_uq_g, w_ukv_g), rope_tab = _weight_gather(
        [own_slot(w_in.T, IN_SHARD), own_slot(w_uq.T, HEAD_PAD), own_slot(w_ukv, KV_LORA)], (IN_SHARD, NOPE + ROPE, KV_LORA),
        pos_col, freq_row)
    w_in_t = w_in_g.reshape(IN_W, D_MODEL)
    w_uq_t = w_uq_g.reshape(HEADS * HEAD_PAD, Q_LORA)
    w_ukv_f = w_ukv_g.transpose(1, 0, 2).reshape(KV_LORA, 1024)
    pool_w_b = pool_w.astype(BF16)
    gq2 = q_norm_g.reshape(1, Q_LORA)
    gkv2 = kv_norm_g.reshape(1, KV_LORA)
    ps2 = pool_scale.reshape(1, POOL_W)

    xf = x.reshape(T, D_MODEL)
    tgt = loss_target.reshape(T, D_MODEL)

    xq, xkv, ga, u, gb, q, k, v, w_out_g = _fwd_proj(
        xf, w_in_t, w_uq_t, w_ukv_f, gq2, gkv2, rope_tab, own_slot(w_out, 256), tm)
    w_out_f = w_out_g.reshape(D_MODEL, D_MODEL)
    o, lse = _attn_fwd(q, k, v, pos_col, pos_row, bounds, nb, S, tq, tk)

    (dz, do, delta, dga, dgb, dpc, d_w_out, d_pool_w, d_pool_scale, d_ln_g, d_ln_b, loss_part) = _mid(
        xf, tgt, o, ga, u, gb, w_out_f, pool_w_b, ps2, ln_g, ln_b, S, tm)

    wide = lambda a: jnp.pad(a.reshape(1, -1), ((0, 0), (0, D_MODEL - a.size)))
    blank = lambda r: jnp.zeros((r, D_MODEL), F32)
    to_all = lambda a: jnp.broadcast_to(a[None], (N_CHIPS,) + a.shape)
    vec_early = jnp.concatenate([d_ln_g, d_ln_b, wide(d_pool_scale), blank(6), wide(loss_part), blank(VEC_ROWS - 10)], axis=0)
    early = [d_w_out.reshape(N_CHIPS, 256, D_MODEL), to_all(d_pool_w.reshape(-1, D_MODEL)), to_all(vec_early)]
    (dq, dk, dv), (g_out, pw_sum, vec_early_sum) = _attn_bwd(
        q, k, v, do, lse, delta, pos_col, pos_row, bounds, early, (BF16, F32, F32), nb, S, tq, tk)
    dx, d_w_in_t, d_w_uq_t, d_w_ukv, d_gq, d_gkv = _bwd_proj(
        dq, dk, dv, xq, xkv, xf, dz, dga, dgb, dpc, rope_tab, w_uq_t, w_ukv_f, w_in_t, gq2, gkv2, S, tm)
    grad_x = dx.reshape(nb, S, D_MODEL)

    g_in = d_w_in_t.reshape(N_CHIPS, IN_SHARD, D_MODEL)
    g_uq = d_w_uq_t
    g_ukv = d_w_ukv.reshape(KV_LORA, N_CHIPS, 256).transpose(1, 0, 2)
    vec_late = jnp.concatenate([blank(3), wide(d_gq), blank(4), wide(d_gkv), blank(VEC_ROWS - 9)], axis=0)
    g_in, g_uq, g_ukv, vec_late_sum = _grad_reduce([g_in, g_uq, g_ukv, to_all(vec_late)], (BF16, BF16, BF16, F32))
    g_big = [g_in, g_uq, g_ukv, g_out]
    pw_sum = pw_sum.reshape(POOL_G * POOL_GD, POOL_GD)
    vec_sum = vec_early_sum + vec_late_sum

    big = _adamw_big(g_big, [w_in.T, w_uq.T, w_ukv, w_out], [m_w_in.T, m_w_uq.T, m_w_ukv, m_w_out],
                     [v_w_in.T, v_w_uq.T, v_w_ukv, v_w_out])
    two_d = lambda a: a.reshape(-1, a.shape[-1])
    small_names = lambda pw, lg, lb, ps, gq, gkv: [two_d(pw), lg, lb, ps.reshape(1, -1), gq.reshape(1, -1), gkv.reshape(1, -1)]
    small, loss_row = _adamw_small(
        pw_sum, vec_sum,
        small_names(pool_w, ln_g, ln_b, pool_scale, q_norm_g, kv_norm_g),
        small_names(m_pool_w, m_ln_g, m_ln_b, m_pool_scale, m_q_norm_g, m_kv_norm_g),
        small_names(v_pool_w, v_ln_g, v_ln_b, v_pool_scale, v_q_norm_g, v_kv_norm_g))
    loss = loss_row[0, 0]

    def leaves(kind):
        b = [g_big[t] if kind == 0 else big[t][kind - 1] for t in range(N_BIG)]
        b = [b[0].T, b[1].T, b[2], b[3]]
        s = [small[t][kind] for t in range(6)]
        return (b[0], s[4].reshape(Q_LORA), b[1], s[5].reshape(KV_LORA), b[2],
                s[0].reshape(POOL_G, POOL_GD, POOL_GD), s[3].reshape(POOL_W), b[3], s[1], s[2])

    return (loss, grad_x) + leaves(0) + leaves(1) + leaves(2) + leaves(3)
```

```python
import jax
import jax.numpy as jnp
from jax import lax
from jax.experimental import pallas as pl
from jax.experimental.pallas import tpu as pltpu

F32 = jnp.float32
BF16 = jnp.bfloat16
MESH = pl.DeviceIdType.MESH

HEADS = 4
NOPE = 128
ROPE = 64
HEAD_PAD = 256
Q_LORA = 512
KV_LORA = 256
MLA_W = 512
POOL_W = 512
POOL_G = 4
POOL_GD = 128
D_MODEL = 1024
IN_W = 2368
IN_EXT = 2432
COL_KV, COL_KR, COL_GA, COL_U, COL_GB = 512, 768, 896, 1408, 1920
ROPE_END = COL_KR + 64
IN_SHARD = IN_W // 4
ROPE_THETA = 10000.0
RMS_EPS = 1e-6
LN_EPS = 1e-5
ALPHA = 2.0 ** 0.25
SCALE = 192.0 ** -0.5
LOG2E = 1.4426950408889634
LN2 = 0.6931471805599453
QSCALE = SCALE * LOG2E
NEG = float(jnp.finfo(jnp.float32).min)
HEAD_GROUP = 2
HEAD_GROUP_FWD = 4
HALO = 16

ADAM_LR = 0.001
ADAM_B1 = 0.9
ADAM_B2 = 0.999
ADAM_EPS = 1e-08
ADAM_WD = 0.01
ADAM_STEP = 10

N_CHIPS = 4
N_BIG = 4
VEC_ROWS = 16

VMEM_LIMIT = 56 * 1024 * 1024


def _cparams(n_grid_dims=0, **kw):
    sem = ("arbitrary",) * n_grid_dims if n_grid_dims else None
    return pltpu.CompilerParams(dimension_semantics=sem, vmem_limit_bytes=VMEM_LIMIT, **kw)


def _full(shape):
    nd = len(shape)
    return pl.BlockSpec(shape, lambda *_: (0,) * nd)


def _dot(a, b):
    return jnp.dot(a, b, preferred_element_type=F32)


def _dot_nt(a, b):
    return lax.dot_general(a, b, (((1,), (1,)), ((), ())), preferred_element_type=F32)


def _dot_tn(a, b):
    return lax.dot_general(a, b, (((0,), (0,)), ((), ())), preferred_element_type=F32)


def _rope_table(pos_col, freq_row):
    lane = lax.broadcasted_iota(jnp.int32, (1, 128), 1)
    ang = pos_col.astype(F32) * freq_row
    return jnp.where(lane < 32, jnp.cos(ang), jnp.where(lane < 64, jnp.sin(ang), 0.0))


def _expand_rope_table(tab):
    lane = lax.broadcasted_iota(jnp.int32, (1, 128), 1)
    second = jnp.logical_and(lane >= 32, lane < 64)
    c = jnp.where(lane < 32, tab, jnp.where(second, pltpu.roll(tab, 32, 1), 0.0))
    sa = jnp.where(lane < 32, pltpu.roll(tab, 96, 1), 0.0)
    sb = jnp.where(second, tab, 0.0)
    return c, sa, sb


def _rope(g, c, sa, sb, sign):
    return g * c + sign * (pltpu.roll(g, 32, 1) * sb - pltpu.roll(g, 96, 1) * sa)


def _place():
    x, y, c = lax.axis_index("x"), lax.axis_index("y"), lax.axis_index("c")
    chips = [(1 - x, y), (x, 1 - y), (1 - x, 1 - y)]
    return x, y, c, chips


ANY = pl.BlockSpec(memory_space=pl.ANY)


ROPE_CHUNK = 2048


def _weight_gather(slots, valid_rows, pos_col, freq_row):
    n = len(slots)
    T = pos_col.shape[0]
    chunk = min(ROPE_CHUNK, T)
    assert T % chunk == 0

    def body(*refs):
        pos_hbm, freq_ref = refs[n:n + 2]
        outs = refs[n + 2:2 * n + 2]
        tab_hbm = refs[2 * n + 2]
        send_sems, recv_sems, pos_buf, tab_buf = refs[2 * n + 3:]
        x, y, c, chips = _place()
        me = 2 * x + y

        def copy(t, k, chip_idx, half, to):
            hc = slots[t].shape[2] // 2
            blk = outs[t].at[chip_idx, pl.ds(0, valid_rows[t]), pl.ds(half * hc, hc)]
            return pltpu.make_async_remote_copy(
                src_ref=blk, dst_ref=blk, send_sem=send_sems.at[6 * t + k], recv_sem=recv_sems.at[6 * t + k],
                device_id=to, device_id_type=MESH)

        first = [copy(t, j, me, c, (cx, cy, c)) for t in range(n) for j, (cx, cy) in enumerate(chips)]
        for cp in first:
            cp.start()

        def table_chunk(r, carry):
            rows = pl.ds(pl.multiple_of(r * chunk, chunk), chunk)
            pltpu.sync_copy(pos_hbm.at[rows], pos_buf)
            tab_buf[...] = _rope_table(pos_buf[...], freq_ref[...])
            pltpu.sync_copy(tab_buf, tab_hbm.at[rows])
            return carry

        lax.fori_loop(0, T // chunk, table_chunk, 0)
        passed = []
        for j, (cx, cy) in enumerate(chips):
            for t in range(n):
                copy(t, j, 2 * cx + cy, c, (x, y, c)).wait_recv()
                fwd = copy(t, 3 + j, 2 * cx + cy, c, (x, y, 1 - c))
                fwd.start()
                passed.append(fwd)
        for j, (cx, cy) in enumerate(chips):
            for t in range(n):
                copy(t, 3 + j, 2 * cx + cy, 1 - c, (x, y, c)).wait_recv()
        for cp in first + passed:
            cp.wait_send()

    outs = pl.pallas_call(
        body, name="weight_gather",
        out_shape=tuple(jax.ShapeDtypeStruct(a.shape, a.dtype) for a in slots) + (jax.ShapeDtypeStruct((T, 128), F32),),
        in_specs=[ANY] * n + [ANY, pl.BlockSpec(memory_space=pltpu.VMEM)], out_specs=(ANY,) * (n + 1),
        input_output_aliases={t: t for t in range(n)},
        scratch_shapes=[pltpu.SemaphoreType.DMA((6 * n,)), pltpu.SemaphoreType.DMA((6 * n,)),
                        pltpu.VMEM((chunk, 1), jnp.int32), pltpu.VMEM((chunk, 128), F32)],
    )(*slots, pos_col, freq_row)
    return outs[:n], outs[n]


def _reduce_scratch(gs, wire_dtypes):
    n = len(gs)
    half = [(g.shape[-2], g.shape[-1] // 2) for g in gs]
    return ([pltpu.VMEM((4,) + h, F32) for h in half] + [pltpu.VMEM((4,) + h, F32) for h in half]
            + [pltpu.VMEM((3,) + h, w) for h, w in zip(half, wire_dtypes)]
            + [pltpu.VMEM((3,) + h, w) for h, w in zip(half, wire_dtypes)]
            + [pltpu.VMEM(h, F32) for h in half]
            + [pltpu.SemaphoreType.DMA((4 * n,)), pltpu.SemaphoreType.DMA((4 * n,)),
               pltpu.SemaphoreType.DMA((3 * n,)), pltpu.SemaphoreType.DMA((3 * n,)),
               pltpu.SemaphoreType.DMA((n,)), pltpu.SemaphoreType.DMA((n,)),
               pltpu.SemaphoreType.DMA((4 * n,)), pltpu.SemaphoreType.DMA((n,))])


def _reduce_phases(gs, wire_dtypes, g_refs, out_refs, scr):
    n = len(gs)
    hcs = [g.shape[-1] // 2 for g in gs]
    own, sib, wire, got, fin = (scr[i * n:(i + 1) * n] for i in range(5))
    d2d_send, d2d_recv, ici_send, ici_recv, fin_send, fin_recv, loc_in, loc_out = scr[5 * n:]

    def place():
        x, y, c, chips = _place()
        return c, chips, (x, y, 1 - c), [2 * cx + cy for cx, cy in chips] + [2 * x + y]

    def remote(src, dst, send, recv, to):
        return pltpu.make_async_remote_copy(src_ref=src, dst_ref=dst, send_sem=send, recv_sem=recv,
                                            device_id=to, device_id_type=MESH)

    def block(ref, t, half, lead=None):
        cols = pl.ds(half * hcs[t], hcs[t])
        rows = pl.ds(0, gs[t].shape[-2])
        return ref.at[rows, cols] if lead is None or len(ref.shape) == 2 else ref.at[lead, rows, cols]

    def load(t, j):
        c, _, _, dests = place()
        return pltpu.make_async_copy(block(g_refs[t], t, c, dests[j]), own[t].at[j], loc_in.at[4 * t + j])

    def d2d(t, j):
        c, _, sibling, dests = place()
        return remote(block(g_refs[t], t, 1 - c, dests[j]), sib[t].at[j],
                      d2d_send.at[4 * t + j], d2d_recv.at[4 * t + j], sibling)

    def ici(t, j):
        c, chips, _, _ = place()
        return remote(wire[t].at[j], got[t].at[j], ici_send.at[3 * t + j], ici_recv.at[3 * t + j], chips[j] + (c,))

    def store(t):
        c = place()[0]
        return pltpu.make_async_copy(fin[t], block(out_refs[t], t, c), loc_out.at[t])

    def final(t, half_of):
        c, _, sibling, _ = place()
        return remote(fin[t], block(out_refs[t], t, c if half_of == "mine" else 1 - c),
                      fin_send.at[t], fin_recv.at[t], sibling)

    def start():
        for j in range(4):
            for t in range(n):
                load(t, j).start()
                d2d(t, j).start()

    def exchange():
        for j in range(3):
            for t in range(n):
                load(t, j).wait()
                d2d(t, j).wait_recv()
                wire[t][j] = (own[t][j] + sib[t][j]).astype(wire_dtypes[t])
                ici(t, j).start()

    def finish():
        for t in range(n):
            load(t, 3).wait()
            d2d(t, 3).wait_recv()
            for j in range(3):
                ici(t, j).wait_recv()
            fin[t][...] = (((own[t][3] + sib[t][3]) + got[t][0].astype(F32))
                           + (got[t][1].astype(F32) + got[t][2].astype(F32)))
            store(t).start()
            final(t, "mine").start()

    def drain():
        for t in range(n):
            final(t, "theirs").wait_recv()
        for t in range(n):
            for j in range(4):
                d2d(t, j).wait_send()
            for j in range(3):
                ici(t, j).wait_send()
            final(t, "mine").wait_send()
            store(t).wait()

    return start, exchange, finish, drain


def _grad_reduce(gs, wire_dtypes):
    n = len(gs)

    def body(*refs):
        for phase in _reduce_phases(gs, wire_dtypes, refs[:n], refs[n:2 * n], refs[2 * n:]):
            phase()

    return pl.pallas_call(
        body, name="grad_reduce",
        out_shape=tuple(jax.ShapeDtypeStruct(g.shape[-2:], F32) for g in gs),
        in_specs=[ANY] * n, out_specs=(ANY,) * n, scratch_shapes=_reduce_scratch(gs, wire_dtypes),
        compiler_params=_cparams(),
    )(*gs)


def _adamw_math(g, w, m, v):
    nm = ADAM_B1 * m + (1.0 - ADAM_B1) * g
    nv = ADAM_B2 * v + (1.0 - ADAM_B2) * (g * g)
    m_hat = nm / (1.0 - ADAM_B1 ** ADAM_STEP)
    v_hat = nv / (1.0 - ADAM_B2 ** ADAM_STEP)
    return -ADAM_LR * (m_hat / (jnp.sqrt(v_hat) + ADAM_EPS) + ADAM_WD * w), nm, nv


ADAM_STEPS = 8


def _adamw_big(gs, ws, ms, vs):
    n = len(gs)

    def body(*refs):
        for t in range(n):
            d, nm, nv = _adamw_math(refs[t][...], refs[n + t][...], refs[2 * n + t][...], refs[3 * n + t][...])
            refs[4 * n + 3 * t][...] = d
            refs[4 * n + 3 * t + 1][...] = nm
            refs[4 * n + 3 * t + 2][...] = nv

    def tile_spec(shape):
        rows, cols = shape
        if rows % (8 * ADAM_STEPS) == 0:
            return pl.BlockSpec((rows // ADAM_STEPS, cols), lambda i: (i, 0))
        return pl.BlockSpec((rows, cols // ADAM_STEPS), lambda i: (0, i))

    specs = [tile_spec(g.shape) for g in gs]
    out_specs, out_shape = [], []
    for t in range(n):
        out_specs += [specs[t]] * 3
        out_shape += [jax.ShapeDtypeStruct(gs[t].shape, F32)] * 3
    outs = pl.pallas_call(
        body, name="adamw_big", grid=(ADAM_STEPS,),
        in_specs=specs * 4, out_specs=tuple(out_specs), out_shape=tuple(out_shape),
        compiler_params=_cparams(1),
    )(*gs, *ws, *ms, *vs)
    return [outs[3 * t: 3 * t + 3] for t in range(n)]


def _adamw_small(pw_sum, vec_sum, ws, ms, vs):
    rows = (None, 0, 1, 2, 3, 8)
    n = len(ws)

    def body(pw_ref, vec_ref, *refs):
        outs = refs[3 * n:]
        for t in range(n):
            w_ref, m_ref, v_ref = refs[t], refs[n + t], refs[2 * n + t]
            if rows[t] is None:
                g = pw_ref[...]
            else:
                g = vec_ref[rows[t]:rows[t] + 1, 0:w_ref.shape[1]]
            d, nm, nv = _adamw_math(g, w_ref[...], m_ref[...], v_ref[...])
            outs[4 * t][...] = g
            outs[4 * t + 1][...] = d
            outs[4 * t + 2][...] = nm
            outs[4 * t + 3][...] = nv
        outs[4 * n][...] = vec_ref[9:10, 0:128]

    vm = pl.BlockSpec(memory_space=pltpu.VMEM)
    out_shape = []
    for w in ws:
        out_shape += [jax.ShapeDtypeStruct(w.shape, F32)] * 4
    out_shape.append(jax.ShapeDtypeStruct((1, 128), F32))
    outs = pl.pallas_call(
        body, name="adamw_small", in_specs=[vm] * (2 + 3 * n), out_specs=(vm,) * (4 * n + 1),
        out_shape=tuple(out_shape),
    )(pw_sum, vec_sum, *ws, *ms, *vs)
    return [outs[4 * t: 4 * t + 4] for t in range(n)], outs[4 * n]


def _fwd_proj(x, w_in_t, w_uq_t, w_ukv, gq, gkv, rope_tab, w_out_slots, tm):
    T = x.shape[0]
    n_steps = T // tm
    fwd_step = n_steps // 2

    def body(x_ref, win_ref, wuq_ref, wukv_ref, gq_ref, gkv_ref, tab_ref, wo_in,
             xq_ref, xkv_ref, ga_ref, u_ref, gb_ref, q_ref, k_ref, v_ref, wo_ref, send_sems, recv_sems):
        i = pl.program_id(0)
        px, py, pc, chips = _place()
        hc = D_MODEL // 2

        def wo_copy(k, chip_idx, half, to):
            blk = wo_ref.at[chip_idx, pl.ds(0, 256), pl.ds(half * hc, hc)]
            return pltpu.make_async_remote_copy(src_ref=blk, dst_ref=blk, send_sem=send_sems.at[k],
                                                recv_sem=recv_sems.at[k], device_id=to, device_id_type=MESH)

        @pl.when(i == 0)
        def _():
            for j, (cx, cy) in enumerate(chips):
                wo_copy(j, 2 * px + py, pc, (cx, cy, pc)).start()

        @pl.when(i == fwd_step)
        def _():
            for j, (cx, cy) in enumerate(chips):
                wo_copy(j, 2 * cx + cy, pc, (px, py, pc)).wait_recv()
                wo_copy(3 + j, 2 * cx + cy, pc, (px, py, 1 - pc)).start()

        @pl.when(i == n_steps - 1)
        def _():
            for j, (cx, cy) in enumerate(chips):
                wo_copy(3 + j, 2 * cx + cy, 1 - pc, (px, py, pc)).wait_recv()
            for j, (cx, cy) in enumerate(chips):
                wo_copy(j, 2 * px + py, pc, (cx, cy, pc)).wait_send()
                wo_copy(3 + j, 2 * cx + cy, pc, (px, py, 1 - pc)).wait_send()

        xb = x_ref[...].astype(BF16)
        h_lat = _dot_nt(xb, win_ref[0:COL_KR, :])
        h_rope = _dot_nt(xb, win_ref[COL_KR:COL_GA, :])
        h_gate = _dot_nt(xb, win_ref[ROPE_END:IN_W, :])
        xq = h_lat[:, 0:COL_KV]
        xkv = h_lat[:, COL_KV:COL_KR]
        xq_ref[...] = xq.astype(BF16)
        xkv_ref[...] = xkv.astype(BF16)
        ga_ref[...] = h_gate[:, 0:MLA_W].astype(BF16)
        u_ref[...] = h_gate[:, MLA_W:MLA_W + POOL_W].astype(BF16)
        gb_ref[...] = h_gate[:, MLA_W + POOL_W:].astype(BF16)
        c, sa, sb = _expand_rope_table(tab_ref[...])
        rq = lax.rsqrt(jnp.mean(xq * xq, axis=-1, keepdims=True) + RMS_EPS)
        q = _dot_nt(((xq * rq) * gq_ref[...]).astype(BF16), wuq_ref[...]) * QSCALE
        rkv = lax.rsqrt(jnp.mean(xkv * xkv, axis=-1, keepdims=True) + RMS_EPS)
        kv = _dot(((xkv * rkv) * gkv_ref[...]).astype(BF16), wukv_ref[...])
        kr = _rope(h_rope, c, sa, sb, 1.0).astype(BF16)
        for hh in range(HEADS):
            b0 = hh * HEAD_PAD
            q_ref[:, b0:b0 + 128] = q[:, b0:b0 + 128].astype(BF16)
            q_ref[:, b0 + 128:b0 + 256] = _rope(q[:, b0 + 128:b0 + 256], c, sa, sb, 1.0).astype(BF16)
            k_ref[:, b0:b0 + 128] = kv[:, b0:b0 + 128].astype(BF16)
            k_ref[:, b0 + 128:b0 + 256] = kr
            v_ref[:, hh * 128:(hh + 1) * 128] = kv[:, b0 + 128:b0 + 256].astype(BF16)

    row = lambda w: pl.BlockSpec((tm, w), lambda i: (i, 0))
    f = lambda w, dt: jax.ShapeDtypeStruct((T, w), dt)
    return pl.pallas_call(
        body, name="fwd_proj", grid=(n_steps,),
        in_specs=[row(D_MODEL), _full(w_in_t.shape), _full(w_uq_t.shape), _full(w_ukv.shape),
                  _full(gq.shape), _full(gkv.shape), row(128), ANY],
        out_specs=(row(512), row(256), row(512), row(512), row(512), row(1024), row(1024), row(512), ANY),
        out_shape=(f(512, BF16), f(256, BF16), f(512, BF16), f(512, BF16), f(512, BF16),
                   f(1024, BF16), f(1024, BF16), f(512, BF16),
                   jax.ShapeDtypeStruct(w_out_slots.shape, BF16)),
        input_output_aliases={7: 8},
        scratch_shapes=[pltpu.SemaphoreType.DMA((6,)), pltpu.SemaphoreType.DMA((6,))],
        compiler_params=_cparams(1),
    )(x, w_in_t, w_uq_t, w_ukv, gq, gkv, rope_tab, w_out_slots)


def _attn_fwd(q, k, v, pos_col, pos_row, bounds, nb, S, tq, tk):
    T = q.shape[0]
    nq, nk = S // tq, S // tk
    reps = tk // 128
    hg = HEAD_GROUP_FWD

    def body(qmin_ref, qmax_ref, kmin_ref, kmax_ref, q_ref, k_ref, v_ref, pc_ref, pr_ref, o_ref, lse_ref,
             m_sc, l_sc, acc_sc):
        b, i = pl.program_id(0), pl.program_id(2)
        m_sc[...] = jnp.full(m_sc.shape, NEG, F32)
        l_sc[...] = jnp.zeros_like(l_sc)
        acc_sc[...] = jnp.zeros_like(acc_sc)
        q_lo = qmin_ref[b * nq + i]
        q_hi = qmax_ref[b * nq + i]

        def tile(j, masked):
            off = pl.multiple_of(j * tk, tk)
            if masked:
                keep = pc_ref[...] >= pr_ref[pl.ds(j, 1), :]
            logits = []
            for g in range(hg):
                qk = slice(g * HEAD_PAD, (g + 1) * HEAD_PAD)
                s = _dot_nt(q_ref[:, qk], k_ref[pl.ds(off, tk), qk])
                if masked:
                    s = jnp.where(keep, s, NEG)
                logits.append(s)
            probs = []
            for g in range(hg):
                hv = slice(g * 128, (g + 1) * 128)
                s = logits[g]
                m_prev = m_sc[:, hv]
                m_new = jnp.maximum(m_prev, jnp.max(s, axis=1, keepdims=True))
                p = jnp.exp2(s - jnp.concatenate([m_new] * reps, axis=1))
                a = jnp.exp2(m_prev - m_new)
                l_sc[:, hv] = a * l_sc[:, hv] + jnp.sum(p, axis=1, keepdims=True)
                m_sc[:, hv] = m_new
                probs.append((p.astype(BF16), a))
            for g in range(hg):
                hv = slice(g * 128, (g + 1) * 128)
                p, a = probs[g]
                acc_sc[:, hv] = a * acc_sc[:, hv] + _dot(p, v_ref[pl.ds(off, tk), hv])

        def step(j, carry):
            visible = kmin_ref[b * nk + j] <= q_hi
            clear = q_lo >= kmax_ref[b * nk + j]

            @pl.when(jnp.logical_and(visible, clear))
            def _():
                tile(j, False)

            @pl.when(jnp.logical_and(visible, jnp.logical_not(clear)))
            def _():
                tile(j, True)
            return carry

        lax.fori_loop(0, nk, step, 0)
        l = l_sc[...]
        o_ref[...] = acc_sc[...] / l
        lse_ref[...] = m_sc[...] + jnp.log2(l)

    ng = HEADS // hg
    stat = pltpu.VMEM((tq, hg * 128), F32)
    return pl.pallas_call(
        body, name="attn_fwd",
        grid_spec=pltpu.PrefetchScalarGridSpec(
            num_scalar_prefetch=4, grid=(nb, ng, nq),
            in_specs=[pl.BlockSpec((tq, hg * HEAD_PAD), lambda b, h, i, *_: (b * nq + i, h)),
                      pl.BlockSpec((S, hg * HEAD_PAD), lambda b, h, i, *_: (b, h)),
                      pl.BlockSpec((S, hg * 128), lambda b, h, i, *_: (b, h)),
                      pl.BlockSpec((tq, 1), lambda b, h, i, *_: (b * nq + i, 0)),
                      pl.BlockSpec((None, nk, tk), lambda b, h, i, *_: (b, 0, 0))],
            out_specs=(pl.BlockSpec((tq, hg * 128), lambda b, h, i, *_: (b * nq + i, h)),
                       pl.BlockSpec((tq, hg * 128), lambda b, h, i, *_: (b * nq + i, h))),
            scratch_shapes=[stat, stat, stat]),
        out_shape=(jax.ShapeDtypeStruct((T, MLA_W), F32), jax.ShapeDtypeStruct((T, MLA_W), F32)),
        compiler_params=_cparams(3),
    )(*bounds, q, k, v, pos_col, pos_row.reshape(nb, nk, tk))


def _mid(x, tgt, o, ga, u, gb, w_out, pool_w, pool_scale, ln_g, ln_b, S, tm):
    T = x.shape[0]
    tps = S // tm
    hb = tm // HALO

    def body(x_ref, tgt_ref, o_ref, ga_ref, u_ref, uh_ref, gb_ref, wout_ref, pw_ref,
             ps_ref, lng_ref, lnb_ref,
             dz_ref, do_ref, delta_ref, dga_ref, dgb_ref, dpc_ref,
             dwout_ref, dpw_ref, vec_ref):
        i = pl.program_id(0)

        @pl.when(i == 0)
        def _():
            dwout_ref[...] = jnp.zeros_like(dwout_ref)
            dpw_ref[...] = jnp.zeros_like(dpw_ref)
            vec_ref[...] = jnp.zeros_like(vec_ref)

        seq_tile = i % tps
        tpos = seq_tile * tm + lax.broadcasted_iota(jnp.int32, (tm, 1), 0)
        ga_v = ga_ref[...].astype(F32)
        sig_a = jax.nn.sigmoid(ga_v)
        silu_a = ga_v * sig_a
        o_v = o_ref[...]
        ya = o_v * silu_a

        u_v = u_ref[...].astype(F32)
        halo = jnp.where(seq_tile == 0, 0.0, uh_ref[...].astype(F32))
        pooled, cnts, mixed = [], [], []
        for g in range(POOL_G):
            lanes = slice(g * POOL_GD, (g + 1) * POOL_GD)
            w = jnp.concatenate([halo[:, lanes], u_v[:, lanes]], axis=0)
            for st in range(g + 1):
                w = w + pltpu.roll(w, 1 << st, 0)
            cnt = jnp.minimum(tpos + 1, 2 << g).astype(F32)
            pg = (w[HALO:, :] / cnt - u_v[:, lanes]).astype(BF16)
            pooled.append(pg)
            cnts.append(cnt)
            mixed.append(_dot(pg, pw_ref[g]))
        mixed = jnp.concatenate(mixed, axis=1)
        ps = ps_ref[...]
        ybp = mixed * ps
        gb_v = gb_ref[...].astype(F32)
        sig_b = jax.nn.sigmoid(gb_v)
        silu_b = gb_v * sig_b
        yb = ybp * silu_b

        cat = jnp.concatenate([ya, yb], axis=1).astype(BF16)
        z = ALPHA * x_ref[...] + _dot(cat, wout_ref[...])
        mu = jnp.mean(z, axis=-1, keepdims=True)
        zc = z - mu
        rstd = lax.rsqrt(jnp.mean(zc * zc, axis=-1, keepdims=True) + LN_EPS)
        zhat = zc * rstd
        lng = lng_ref[...]
        err = (zhat * lng + lnb_ref[...]) - tgt_ref[...]
        row_loss = jnp.sum(err * err, axis=1, keepdims=True)
        vec_ref[9:10, 0:128] += jnp.broadcast_to(jnp.sum(row_loss, axis=0, keepdims=True) * (0.5 / D_MODEL), (1, 128))
        dy = err * (1.0 / D_MODEL)
        vec_ref[0:1, :] += jnp.sum(dy * zhat, axis=0, keepdims=True)
        vec_ref[1:2, :] += jnp.sum(dy, axis=0, keepdims=True)
        dzh = dy * lng
        dz = rstd * (dzh - jnp.mean(dzh, axis=-1, keepdims=True)
                     - zhat * jnp.mean(dzh * zhat, axis=-1, keepdims=True))
        dz_ref[...] = dz
        dzb = dz.astype(BF16)
        dwout_ref[...] += _dot_tn(cat, dzb)
        dcat = _dot_nt(dzb, wout_ref[...])
        dya = dcat[:, :MLA_W]
        dyb = dcat[:, MLA_W:]

        do = dya * silu_a
        do_ref[...] = do.astype(BF16)
        prod = do * o_v
        for hh in range(HEADS):
            lanes = slice(hh * 128, (hh + 1) * 128)
            delta_ref[:, lanes] = jnp.broadcast_to(jnp.sum(prod[:, lanes], axis=1, keepdims=True), (tm, 128))
        dga_ref[...] = (dya * o_v * (sig_a * (1.0 + ga_v * (1.0 - sig_a)))).astype(BF16)
        dgb_ref[...] = (dyb * ybp * (sig_b * (1.0 + gb_v * (1.0 - sig_b)))).astype(BF16)
        dybp = dyb * silu_b
        vec_ref[2:3, 0:POOL_W] += jnp.sum(dybp * mixed, axis=0, keepdims=True)
        dmixed = (dybp * ps).astype(BF16)
        for g in range(POOL_G):
            lanes = slice(g * POOL_GD, (g + 1) * POOL_GD)
            dpw_ref[g] += _dot_tn(pooled[g], dmixed[:, lanes])
            dpc_ref[:, lanes] = (_dot_nt(dmixed[:, lanes], pw_ref[g]) / cnts[g]).astype(BF16)

    row = lambda w: pl.BlockSpec((tm, w), lambda i: (i, 0))
    f = lambda w, dt: jax.ShapeDtypeStruct((T, w), dt)
    halo_spec = pl.BlockSpec((HALO, POOL_W), lambda i: (jnp.maximum(i * hb - 1, 0), 0))
    return pl.pallas_call(
        body, name="mid", grid=(T // tm,),
        in_specs=[row(D_MODEL), row(D_MODEL), row(MLA_W), row(MLA_W), row(POOL_W), halo_spec, row(POOL_W),
                  _full(w_out.shape), _full(pool_w.shape),
                  _full(pool_scale.shape), _full(ln_g.shape), _full(ln_b.shape)],
        out_specs=(row(D_MODEL), row(MLA_W), row(MLA_W), row(MLA_W), row(POOL_W), row(POOL_W),
                   _full((D_MODEL, D_MODEL)), _full(pool_w.shape), _full((VEC_ROWS, D_MODEL))),
        out_shape=(f(D_MODEL, F32), f(MLA_W, BF16), f(MLA_W, F32), f(MLA_W, BF16), f(POOL_W, BF16), f(POOL_W, BF16),
                   jax.ShapeDtypeStruct((D_MODEL, D_MODEL), F32), jax.ShapeDtypeStruct(pool_w.shape, F32),
                   jax.ShapeDtypeStruct((VEC_ROWS, D_MODEL), F32)),
        compiler_params=_cparams(1),
    )(x, tgt, o, ga, u, u, gb, w_out, pool_w, pool_scale, ln_g, ln_b)


def _attn_bwd(q, k, v, do, lse, delta, pos_col, pos_row, bounds, early, early_wire, nb, S, tq, tk):
    T = q.shape[0]
    ne = len(early)
    nq, nk = S // tq, S // tk
    reps = tk // 128
    hg = HEAD_GROUP
    ng = HEADS // hg

    def body(qmin_ref, qmax_ref, kmin_ref, kmax_ref, q_ref, k_ref, v_ref, do_ref, lse_ref, dl_ref, pc_ref, pr_ref,
             *rest):
        early_in, rest = rest[:ne], rest[ne:]
        dq_out, dk_out, dv_out = rest[:3]
        early_out, rest = rest[3:3 + ne], rest[3 + ne:]
        dq_ref, dk_ref, dv_ref = rest[:3]
        b, j = pl.program_id(0), pl.program_id(2)
        flat = (b * ng + pl.program_id(1)) * nk + j
        last = nb * ng * nk - 1
        when = [0, min(3, last), min(max(5 * (last + 1) // 8, 3), last), last]
        for at, phase in zip(when, _reduce_phases(early, early_wire, early_in, early_out, rest[3:])):
            pl.when(flat == at)(phase)

        @pl.when(j == 0)
        def _():
            dq_ref[...] = jnp.zeros_like(dq_ref)

        dk_ref[...] = jnp.zeros_like(dk_ref)
        dv_ref[...] = jnp.zeros_like(dv_ref)
        k_lo = kmin_ref[b * nk + j]
        k_hi = kmax_ref[b * nk + j]

        def tile(i, masked):
            rows = pl.ds(pl.multiple_of(i * tq, tq), tq)
            if masked:
                keep = pc_ref[rows, :] >= pr_ref[...]
            stage = []
            for g in range(hg):
                qk = slice(g * HEAD_PAD, (g + 1) * HEAD_PAD)
                hv = slice(g * 128, (g + 1) * 128)
                s = _dot_nt(q_ref[rows, qk], k_ref[:, qk])
                if masked:
                    s = jnp.where(keep, s, NEG)
                stage.append((s, _dot_nt(do_ref[rows, hv], v_ref[:, hv])))
            grads = []
            for g in range(hg):
                hv = slice(g * 128, (g + 1) * 128)
                s, dp = stage[g]
                p = jnp.exp2(s - jnp.concatenate([lse_ref[rows, hv]] * reps, axis=1))
                ds = (p * (dp - jnp.concatenate([dl_ref[rows, hv]] * reps, axis=1))).astype(BF16)
                grads.append((p.astype(BF16), ds))
            for g in range(hg):
                qk = slice(g * HEAD_PAD, (g + 1) * HEAD_PAD)
                hv = slice(g * 128, (g + 1) * 128)
                p, ds = grads[g]
                dv_ref[:, hv] += _dot_tn(p, do_ref[rows, hv])
                dq_ref[rows, qk] += _dot(ds, k_ref[:, qk])
                dk_ref[:, qk] += _dot_tn(ds, q_ref[rows, qk])

        def step(i, carry):
            visible = k_lo <= qmax_ref[b * nq + i]
            clear = qmin_ref[b * nq + i] >= k_hi

            @pl.when(jnp.logical_and(visible, clear))
            def _():
                tile(i, False)

            @pl.when(jnp.logical_and(visible, jnp.logical_not(clear)))
            def _():
                tile(i, True)
            return carry

        lax.fori_loop(0, nq, step, 0)
        dk_out[...] = dk_ref[...].astype(BF16)
        dv_out[...] = dv_ref[...].astype(BF16)

        @pl.when(j == nk - 1)
        def _():
            dq_out[...] = dq_ref[...].astype(BF16)

    seq = lambda w: pl.BlockSpec((S, w), lambda b, h, j, *_: (b, h))
    blk = lambda w: pl.BlockSpec((tk, w), lambda b, h, j, *_: (b * nk + j, h))
    outs = pl.pallas_call(
        body, name="attn_bwd",
        grid_spec=pltpu.PrefetchScalarGridSpec(
            num_scalar_prefetch=4, grid=(nb, ng, nk),
            in_specs=[seq(hg * HEAD_PAD), blk(hg * HEAD_PAD), blk(hg * 128),
                      seq(hg * 128), seq(hg * 128), seq(hg * 128),
                      pl.BlockSpec((S, 1), lambda b, h, j, *_: (b, 0)),
                      pl.BlockSpec((None, 1, tk), lambda b, h, j, *_: (b, 0, j))] + [ANY] * ne,
            out_specs=(seq(hg * HEAD_PAD), blk(hg * HEAD_PAD), blk(hg * 128)) + (ANY,) * ne,
            scratch_shapes=[pltpu.VMEM((S, hg * HEAD_PAD), F32), pltpu.VMEM((tk, hg * HEAD_PAD), F32),
                            pltpu.VMEM((tk, hg * 128), F32)] + _reduce_scratch(early, early_wire)),
        out_shape=(jax.ShapeDtypeStruct((T, HEADS * HEAD_PAD), BF16),
                   jax.ShapeDtypeStruct((T, HEADS * HEAD_PAD), BF16),
                   jax.ShapeDtypeStruct((T, MLA_W), BF16)) + tuple(jax.ShapeDtypeStruct(g.shape[-2:], F32) for g in early),
        compiler_params=_cparams(3),
    )(*bounds, q, k, v, do, lse, delta, pos_col, pos_row, *early)
    return outs[:3], outs[3:]


def _bwd_proj(dq, dk, dv, xq, xkv, x, dz, dga, dgb, dpc, rope_tab, w_uq_t, w_ukv, w_in_t, gq, gkv, S, tm):
    T = x.shape[0]
    tps = S // tm
    hb = tm // HALO
    n_tiles = T // tm

    def body(dq_ref, dk_ref, dv_ref, xq_ref, xkv_ref, x_ref, dz_ref, dga_ref, dgb_ref, dpc_ref, dph_ref,
             tab_ref, wuq_ref, wukv_ref, win_ref, gq_ref, gkv_ref,
             dx_ref, dwin_hbm, dwuq_hbm, dwukv_hbm, vec_ref,
             acc_win, acc_wuq, acc_wukv, dh_sc):
        i = pl.program_id(0)

        @pl.when(i == 0)
        def _():
            acc_win[...] = jnp.zeros_like(acc_win)
            acc_wuq[...] = jnp.zeros_like(acc_wuq)
            acc_wukv[...] = jnp.zeros_like(acc_wukv)
            vec_ref[...] = jnp.zeros_like(vec_ref)
            dh_sc[...] = jnp.zeros_like(dh_sc)

        dh_prev = dh_sc[...]
        dx_ref[...] = (ALPHA * dz_ref[...] + _dot(dh_prev[:, 0:COL_GA], win_ref[0:COL_GA, :])
                       + _dot(dh_prev[:, COL_GA:], win_ref[ROPE_END:IN_W, :]))
        acc_win[...] += _dot_tn(dh_prev, x_ref[...].astype(BF16))

        live = jnp.where(i < n_tiles, 1.0, 0.0)
        c, sa, sb = _expand_rope_table(tab_ref[...])
        dq_v = dq_ref[...].astype(F32) * (SCALE * live)
        dk_v = dk_ref[...].astype(F32) * (LN2 * live)
        dv_v = dv_ref[...].astype(F32) * live
        dq_parts, dkv_parts = [], []
        dkr = jnp.zeros((tm, 128), F32)
        for hh in range(HEADS):
            b0 = hh * HEAD_PAD
            dq_parts.append(dq_v[:, b0:b0 + 128].astype(BF16))
            dq_parts.append(_rope(dq_v[:, b0 + 128:b0 + 256], c, sa, sb, -1.0).astype(BF16))
            dkv_parts.append(dk_v[:, b0:b0 + 128].astype(BF16))
            dkv_parts.append(dv_v[:, hh * 128:(hh + 1) * 128].astype(BF16))
            dkr = dkr + dk_v[:, b0 + 128:b0 + 256]
        dqp = jnp.concatenate(dq_parts, axis=1)
        dkvp = jnp.concatenate(dkv_parts, axis=1)
        dkrr = _rope(dkr, c, sa, sb, -1.0)

        def rms_bwd(xv, g, dyn, dg_ref):
            r = lax.rsqrt(jnp.mean(xv * xv, axis=-1, keepdims=True) + RMS_EPS)
            xhat = xv * r
            dg_ref[...] += jnp.sum(dyn * xhat, axis=0, keepdims=True)
            dxh = dyn * g
            return r * (dxh - xhat * jnp.mean(dxh * xhat, axis=-1, keepdims=True))

        xq_v = xq_ref[...].astype(F32)
        gq_v = gq_ref[...]
        rq = lax.rsqrt(jnp.mean(xq_v * xq_v, axis=-1, keepdims=True) + RMS_EPS)
        acc_wuq[...] += _dot_tn(dqp, ((xq_v * rq) * gq_v).astype(BF16))
        dxq = rms_bwd(xq_v, gq_v, _dot(dqp, wuq_ref[...]), vec_ref.at[3:4, 0:Q_LORA])

        xkv_v = xkv_ref[...].astype(F32)
        gkv_v = gkv_ref[...]
        rkv = lax.rsqrt(jnp.mean(xkv_v * xkv_v, axis=-1, keepdims=True) + RMS_EPS)
        acc_wukv[...] += _dot_tn(((xkv_v * rkv) * gkv_v).astype(BF16), dkvp)
        dxkv = rms_bwd(xkv_v, gkv_v, _dot_nt(dkvp, wukv_ref[...]), vec_ref.at[8:9, 0:KV_LORA])

        seq_tile = i % tps
        tpos = seq_tile * tm + lax.broadcasted_iota(jnp.int32, (tm, 1), 0)
        dpc_v = dpc_ref[...].astype(F32)
        halo = jnp.where(seq_tile == tps - 1, 0.0, dph_ref[...].astype(F32))
        n = tm + HALO
        du = []
        for g in range(POOL_G):
            lanes = slice(g * POOL_GD, (g + 1) * POOL_GD)
            f = jnp.concatenate([dpc_v[:, lanes], halo[:, lanes]], axis=0)
            for st in range(g + 1):
                f = f + pltpu.roll(f, n - (1 << st), 0)
            cnt = jnp.minimum(tpos + 1, 2 << g).astype(F32)
            du.append((f[:tm, :] - dpc_v[:, lanes] * cnt).astype(BF16))

        dh_sc[...] = jnp.concatenate([dxq.astype(BF16), dxkv.astype(BF16), dkrr.astype(BF16), dga_ref[...]]
                                     + du + [dgb_ref[...]], axis=1)

        @pl.when(i == n_tiles)
        def _():
            pltpu.sync_copy(acc_win.at[pl.ds(0, ROPE_END)], dwin_hbm.at[pl.ds(0, ROPE_END)])
            pltpu.sync_copy(acc_win.at[pl.ds(COL_GA, IN_EXT - COL_GA)], dwin_hbm.at[pl.ds(ROPE_END, IN_W - ROPE_END)])
            for hh in range(HEADS):
                pltpu.sync_copy(acc_wuq.at[pl.ds(hh * HEAD_PAD, NOPE + ROPE)], dwuq_hbm.at[hh])
            for ch in range(N_CHIPS):
                pltpu.sync_copy(acc_wukv.at[:, pl.ds(ch * 256, 256)], dwukv_hbm.at[ch])

    cur = lambda w: pl.BlockSpec((tm, w), lambda i: (jnp.minimum(i, n_tiles - 1), 0))
    prev = lambda w: pl.BlockSpec((tm, w), lambda i: (jnp.maximum(i - 1, 0), 0))
    halo_spec = pl.BlockSpec((HALO, POOL_W), lambda i: (jnp.minimum((i + 1) * hb, T // HALO - 1), 0))
    return pl.pallas_call(
        body, name="bwd_proj", grid=(n_tiles + 1,),
        in_specs=[cur(1024), cur(1024), cur(512), cur(512), cur(256), prev(D_MODEL), prev(D_MODEL),
                  cur(512), cur(512), cur(512), halo_spec, cur(128),
                  _full(w_uq_t.shape), _full(w_ukv.shape), _full(w_in_t.shape), _full(gq.shape), _full(gkv.shape)],
        out_specs=(prev(D_MODEL), ANY, ANY, ANY, _full((VEC_ROWS, D_MODEL))),
        out_shape=(jax.ShapeDtypeStruct((T, D_MODEL), F32),
                   jax.ShapeDtypeStruct((IN_W, D_MODEL), F32),
                   jax.ShapeDtypeStruct((HEADS, NOPE + ROPE, Q_LORA), F32),
                   jax.ShapeDtypeStruct((N_CHIPS, KV_LORA, 256), F32),
                   jax.ShapeDtypeStruct((VEC_ROWS, D_MODEL), F32)),
        scratch_shapes=[pltpu.VMEM((IN_EXT, D_MODEL), F32), pltpu.VMEM((HEADS * HEAD_PAD, Q_LORA), F32),
                        pltpu.VMEM((KV_LORA, 1024), F32), pltpu.VMEM((tm, IN_EXT), BF16)],
        compiler_params=_cparams(1),
    )(dq, dk, dv, xq, xkv, x, dz, dga, dgb, dpc, dpc, rope_tab, w_uq_t, w_ukv, w_in_t, gq, gkv)


def kernel(x, positions, w_in, q_norm_g, w_uq, kv_norm_g, w_ukv, pool_w, pool_scale, w_out, ln_g, ln_b, loss_target, m_w_in, m_q_norm_g, m_w_uq, m_kv_norm_g, m_w_ukv, m_pool_w, m_pool_scale, m_w_out, m_ln_g, m_ln_b, v_w_in, v_q_norm_g, v_w_uq, v_kv_norm_g, v_w_ukv, v_pool_w, v_pool_scale, v_w_out, v_ln_g, v_ln_b):
    nb, S, _ = x.shape
    T = nb * S
    tm = min(256, S)
    tq = min(512, S)
    tk = min(512, S)
    assert S % tm == 0 and tm % HALO == 0 and S % tq == 0 and S % tk == 0

    cx, cy, cc = lax.axis_index("x"), lax.axis_index("y"), lax.axis_index("c")
    me = 2 * cx + cy

    half = ROPE // 2
    inv_freq = ROPE_THETA ** (-jnp.arange(half, dtype=F32) / half)
    freq_row = jnp.concatenate([inv_freq, inv_freq, jnp.zeros((2 * half,), F32)]).reshape(1, 128)
    pos_col = positions.reshape(T, 1)
    pos_row = positions.reshape(nb, 1, S)
    pos_q = positions.reshape(nb, S // tq, tq)
    pos_k = positions.reshape(nb, S // tk, tk)
    bounds = (jnp.min(pos_q, axis=2).reshape(-1), jnp.max(pos_q, axis=2).reshape(-1),
              jnp.min(pos_k, axis=2).reshape(-1), jnp.max(pos_k, axis=2).reshape(-1))

    def own_slot(w, slot_rows):
        blk = jnp.pad(w.astype(BF16), ((0, slot_rows - w.shape[0]), (0, 0)))
        return lax.dynamic_update_slice(jnp.zeros((N_CHIPS,) + blk.shape, BF16), blk[None], (me, 0, 0))

    (w_in_g, w_uq_g, w_ukv_g), rope_tab = _weight_gather(
        [own_slot(w_in.T, IN_SHARD), own_slot(w_uq.T, HEAD_PAD), own_slot(w_ukv, KV_LORA)], (IN_SHARD, NOPE + ROPE, KV_LORA),
        pos_col, freq_row)
    w_in_t = w_in_g.reshape(IN_W, D_MODEL)
    w_uq_t = w_uq_g.reshape(HEADS * HEAD_PAD, Q_LORA)
    w_ukv_f = w_ukv_g.transpose(1, 0, 2).reshape(KV_LORA, 1024)
    pool_w_b = pool_w.astype(BF16)
    gq2 = q_norm_g.reshape(1, Q_LORA)
    gkv2 = kv_norm_g.reshape(1, KV_LORA)
    ps2 = pool_scale.reshape(1, POOL_W)

    xf = x.reshape(T, D_MODEL)
    tgt = loss_target.reshape(T, D_MODEL)

    xq, xkv, ga, u, gb, q, k, v, w_out_g = _fwd_proj(
        xf, w_in_t, w_uq_t, w_ukv_f, gq2, gkv2, rope_tab, own_slot(w_out, 256), tm)
    w_out_f = w_out_g.reshape(D_MODEL, D_MODEL)
    o, lse = _attn_fwd(q, k, v, pos_col, pos_row, bounds, nb, S, tq, tk)

    dz, do, delta, dga, dgb, dpc, d_w_out, d_pool_w, vec_early = _mid(
        xf, tgt, o, ga, u, gb, w_out_f, pool_w_b, ps2, ln_g, ln_b, S, tm)

    early = [d_w_out.reshape(N_CHIPS, 256, D_MODEL), d_pool_w.reshape(-1, D_MODEL), vec_early]
    (dq, dk, dv), (g_out, pw_sum, vec_early_sum) = _attn_bwd(
        q, k, v, do, lse, delta, pos_col, pos_row, bounds, early, (BF16, F32, F32), nb, S, tq, tk)
    dx, d_w_in_t, d_w_uq_t, g_ukv, vec_late = _bwd_proj(
        dq, dk, dv, xq, xkv, xf, dz, dga, dgb, dpc, rope_tab, w_uq_t, w_ukv_f, w_in_t, gq2, gkv2, S, tm)
    grad_x = dx.reshape(nb, S, D_MODEL)

    g_in = d_w_in_t.reshape(N_CHIPS, IN_SHARD, D_MODEL)
    g_uq = d_w_uq_t
    g_in, g_uq, g_ukv, vec_late_sum = _grad_reduce([g_in, g_uq, g_ukv, vec_late], (BF16, BF16, BF16, F32))
    g_big = [g_in, g_uq, g_ukv, g_out]
    pw_sum = pw_sum.reshape(POOL_G * POOL_GD, POOL_GD)
    vec_sum = vec_early_sum + vec_late_sum

    big = _adamw_big(g_big, [w_in.T, w_uq.T, w_ukv, w_out], [m_w_in.T, m_w_uq.T, m_w_ukv, m_w_out],
                     [v_w_in.T, v_w_uq.T, v_w_ukv, v_w_out])
    two_d = lambda a: a.reshape(-1, a.shape[-1])
    small_names = lambda pw, lg, lb, ps, gq, gkv: [two_d(pw), lg, lb, ps.reshape(1, -1), gq.reshape(1, -1), gkv.reshape(1, -1)]
    small, loss_row = _adamw_small(
        pw_sum, vec_sum,
        small_names(pool_w, ln_g, ln_b, pool_scale, q_norm_g, kv_norm_g),
        small_names(m_pool_w, m_ln_g, m_ln_b, m_pool_scale, m_q_norm_g, m_kv_norm_g),
        small_names(v_pool_w, v_ln_g, v_ln_b, v_pool_scale, v_q_norm_g, v_kv_norm_g))
    loss = loss_row[0, 0]

    def leaves(kind):
        b = [g_big[t] if kind == 0 else big[t][kind - 1] for t in range(N_BIG)]
        b = [b[0].T, b[1].T, b[2], b[3]]
        s = [small[t][kind] for t in range(6)]
        return (b[0], s[4].reshape(Q_LORA), b[1], s[5].reshape(KV_LORA), b[2],
                s[0].reshape(POOL_G, POOL_GD, POOL_GD), s[3].reshape(POOL_W), b[3], s[1], s[2])

    return (loss, grad_x) + leaves(0) + leaves(1) + leaves(2) + leaves(3)
```

```python
import jax
import jax.numpy as jnp
from jax import lax
from jax.experimental import pallas as pl
from jax.experimental.pallas import tpu as pltpu

F32 = jnp.float32
BF16 = jnp.bfloat16
MESH = pl.DeviceIdType.MESH

HEADS = 4
NOPE = 128
ROPE = 64
HEAD_PAD = 256
Q_LORA = 512
KV_LORA = 256
MLA_W = 512
POOL_W = 512
POOL_G = 4
POOL_GD = 128
D_MODEL = 1024
IN_W = 2368
IN_EXT = 2432
COL_KV, COL_KR, COL_GA, COL_U, COL_GB = 512, 768, 896, 1408, 1920
ROPE_END = COL_KR + 64
IN_SHARD = IN_W // 4
ROPE_THETA = 10000.0
RMS_EPS = 1e-6
LN_EPS = 1e-5
ALPHA = 2.0 ** 0.25
SCALE = 192.0 ** -0.5
LOG2E = 1.4426950408889634
LN2 = 0.6931471805599453
QSCALE = SCALE * LOG2E
NEG = float(jnp.finfo(jnp.float32).min)
HEAD_GROUP = 2
HEAD_GROUP_FWD = 4
HALO = 16

ADAM_LR = 0.001
ADAM_B1 = 0.9
ADAM_B2 = 0.999
ADAM_EPS = 1e-08
ADAM_WD = 0.01
ADAM_STEP = 10

N_CHIPS = 4
N_BIG = 4
VEC_ROWS = 16

VMEM_LIMIT = 56 * 1024 * 1024


def _cparams(n_grid_dims=0, **kw):
    sem = ("arbitrary",) * n_grid_dims if n_grid_dims else None
    return pltpu.CompilerParams(dimension_semantics=sem, vmem_limit_bytes=VMEM_LIMIT, **kw)


def _full(shape):
    nd = len(shape)
    return pl.BlockSpec(shape, lambda *_: (0,) * nd)


def _dot(a, b):
    return jnp.dot(a, b, preferred_element_type=F32)


def _dot_nt(a, b):
    return lax.dot_general(a, b, (((1,), (1,)), ((), ())), preferred_element_type=F32)


def _dot_tn(a, b):
    return lax.dot_general(a, b, (((0,), (0,)), ((), ())), preferred_element_type=F32)


def _rope_table(pos_col, freq_row):
    lane = lax.broadcasted_iota(jnp.int32, (1, 128), 1)
    ang = pos_col.astype(F32) * freq_row
    return jnp.where(lane < 32, jnp.cos(ang), jnp.where(lane < 64, jnp.sin(ang), 0.0))


def _expand_rope_table(tab):
    lane = lax.broadcasted_iota(jnp.int32, (1, 128), 1)
    second = jnp.logical_and(lane >= 32, lane < 64)
    c = jnp.where(lane < 32, tab, jnp.where(second, pltpu.roll(tab, 32, 1), 0.0))
    sa = jnp.where(lane < 32, pltpu.roll(tab, 96, 1), 0.0)
    sb = jnp.where(second, tab, 0.0)
    return c, sa, sb


def _rope(g, c, sa, sb, sign):
    return g * c + sign * (pltpu.roll(g, 32, 1) * sb - pltpu.roll(g, 96, 1) * sa)


def _place():
    x, y, c = lax.axis_index("x"), lax.axis_index("y"), lax.axis_index("c")
    chips = [(1 - x, y), (x, 1 - y), (1 - x, 1 - y)]
    return x, y, c, chips


ANY = pl.BlockSpec(memory_space=pl.ANY)


ROPE_CHUNK = 2048


def _weight_gather(slots, valid_rows, pos_col, freq_row):
    n = len(slots)
    T = pos_col.shape[0]
    chunk = min(ROPE_CHUNK, T)
    assert T % chunk == 0

    def body(*refs):
        pos_hbm, freq_ref = refs[n:n + 2]
        outs = refs[n + 2:2 * n + 2]
        tab_hbm = refs[2 * n + 2]
        send_sems, recv_sems, pos_buf, tab_buf = refs[2 * n + 3:]
        x, y, c, chips = _place()
        me = 2 * x + y

        def copy(t, k, chip_idx, half, to):
            hc = slots[t].shape[2] // 2
            blk = outs[t].at[chip_idx, pl.ds(0, valid_rows[t]), pl.ds(half * hc, hc)]
            return pltpu.make_async_remote_copy(
                src_ref=blk, dst_ref=blk, send_sem=send_sems.at[6 * t + k], recv_sem=recv_sems.at[6 * t + k],
                device_id=to, device_id_type=MESH)

        first = [copy(t, j, me, c, (cx, cy, c)) for t in range(n) for j, (cx, cy) in enumerate(chips)]
        for cp in first:
            cp.start()

        def table_chunk(r, carry):
            rows = pl.ds(pl.multiple_of(r * chunk, chunk), chunk)
            pltpu.sync_copy(pos_hbm.at[rows], pos_buf)
            tab_buf[...] = _rope_table(pos_buf[...], freq_ref[...])
            pltpu.sync_copy(tab_buf, tab_hbm.at[rows])
            return carry

        lax.fori_loop(0, T // chunk, table_chunk, 0)
        passed = []
        for j, (cx, cy) in enumerate(chips):
            for t in range(n):
                copy(t, j, 2 * cx + cy, c, (x, y, c)).wait_recv()
                fwd = copy(t, 3 + j, 2 * cx + cy, c, (x, y, 1 - c))
                fwd.start()
                passed.append(fwd)
        for j, (cx, cy) in enumerate(chips):
            for t in range(n):
                copy(t, 3 + j, 2 * cx + cy, 1 - c, (x, y, c)).wait_recv()
        for cp in first + passed:
            cp.wait_send()

    outs = pl.pallas_call(
        body, name="weight_gather",
        out_shape=tuple(jax.ShapeDtypeStruct(a.shape, a.dtype) for a in slots) + (jax.ShapeDtypeStruct((T, 128), F32),),
        in_specs=[ANY] * n + [ANY, pl.BlockSpec(memory_space=pltpu.VMEM)], out_specs=(ANY,) * (n + 1),
        input_output_aliases={t: t for t in range(n)},
        scratch_shapes=[pltpu.SemaphoreType.DMA((6 * n,)), pltpu.SemaphoreType.DMA((6 * n,)),
                        pltpu.VMEM((chunk, 1), jnp.int32), pltpu.VMEM((chunk, 128), F32)],
    )(*slots, pos_col, freq_row)
    return outs[:n], outs[n]


def _reduce_scratch(gs, wire_dtypes):
    n = len(gs)
    half = [(g.shape[-2], g.shape[-1] // 2) for g in gs]
    return ([pltpu.VMEM((4,) + h, F32) for h in half] + [pltpu.VMEM((4,) + h, F32) for h in half]
            + [pltpu.VMEM((3,) + h, w) for h, w in zip(half, wire_dtypes)]
            + [pltpu.VMEM((3,) + h, w) for h, w in zip(half, wire_dtypes)]
            + [pltpu.VMEM(h, F32) for h in half]
            + [pltpu.SemaphoreType.DMA((4 * n,)), pltpu.SemaphoreType.DMA((4 * n,)),
               pltpu.SemaphoreType.DMA((3 * n,)), pltpu.SemaphoreType.DMA((3 * n,)),
               pltpu.SemaphoreType.DMA((n,)), pltpu.SemaphoreType.DMA((n,)),
               pltpu.SemaphoreType.DMA((4 * n,)), pltpu.SemaphoreType.DMA((n,))])


def _reduce_phases(gs, wire_dtypes, g_refs, out_refs, scr):
    n = len(gs)
    hcs = [g.shape[-1] // 2 for g in gs]
    own, sib, wire, got, fin = (scr[i * n:(i + 1) * n] for i in range(5))
    d2d_send, d2d_recv, ici_send, ici_recv, fin_send, fin_recv, loc_in, loc_out = scr[5 * n:]

    def place():
        x, y, c, chips = _place()
        return c, chips, (x, y, 1 - c), [2 * cx + cy for cx, cy in chips] + [2 * x + y]

    def remote(src, dst, send, recv, to):
        return pltpu.make_async_remote_copy(src_ref=src, dst_ref=dst, send_sem=send, recv_sem=recv,
                                            device_id=to, device_id_type=MESH)

    def block(ref, t, half, lead=None):
        cols = pl.ds(half * hcs[t], hcs[t])
        rows = pl.ds(0, gs[t].shape[-2])
        return ref.at[rows, cols] if lead is None or len(ref.shape) == 2 else ref.at[lead, rows, cols]

    def load(t, j):
        c, _, _, dests = place()
        return pltpu.make_async_copy(block(g_refs[t], t, c, dests[j]), own[t].at[j], loc_in.at[4 * t + j])

    def d2d(t, j):
        c, _, sibling, dests = place()
        return remote(block(g_refs[t], t, 1 - c, dests[j]), sib[t].at[j],
                      d2d_send.at[4 * t + j], d2d_recv.at[4 * t + j], sibling)

    def ici(t, j):
        c, chips, _, _ = place()
        return remote(wire[t].at[j], got[t].at[j], ici_send.at[3 * t + j], ici_recv.at[3 * t + j], chips[j] + (c,))

    def store(t):
        c = place()[0]
        return pltpu.make_async_copy(fin[t], block(out_refs[t], t, c), loc_out.at[t])

    def final(t, half_of):
        c, _, sibling, _ = place()
        return remote(fin[t], block(out_refs[t], t, c if half_of == "mine" else 1 - c),
                      fin_send.at[t], fin_recv.at[t], sibling)

    def start():
        for j in range(4):
            for t in range(n):
                load(t, j).start()
                d2d(t, j).start()

    def exchange():
        for j in range(3):
            for t in range(n):
                load(t, j).wait()
                d2d(t, j).wait_recv()
                wire[t][j] = (own[t][j] + sib[t][j]).astype(wire_dtypes[t])
                ici(t, j).start()

    def finish():
        for t in range(n):
            load(t, 3).wait()
            d2d(t, 3).wait_recv()
            for j in range(3):
                ici(t, j).wait_recv()
            fin[t][...] = (((own[t][3] + sib[t][3]) + got[t][0].astype(F32))
                           + (got[t][1].astype(F32) + got[t][2].astype(F32)))
            store(t).start()
            final(t, "mine").start()

    def drain():
        for t in range(n):
            final(t, "theirs").wait_recv()
        for t in range(n):
            for j in range(4):
                d2d(t, j).wait_send()
            for j in range(3):
                ici(t, j).wait_send()
            final(t, "mine").wait_send()
            store(t).wait()

    return start, exchange, finish, drain


def _grad_reduce(gs, wire_dtypes):
    n = len(gs)

    def body(*refs):
        for phase in _reduce_phases(gs, wire_dtypes, refs[:n], refs[n:2 * n], refs[2 * n:]):
            phase()

    return pl.pallas_call(
        body, name="grad_reduce",
        out_shape=tuple(jax.ShapeDtypeStruct(g.shape[-2:], F32) for g in gs),
        in_specs=[ANY] * n, out_specs=(ANY,) * n, scratch_shapes=_reduce_scratch(gs, wire_dtypes),
        compiler_params=_cparams(),
    )(*gs)


def _adamw_math(g, w, m, v):
    nm = ADAM_B1 * m + (1.0 - ADAM_B1) * g
    nv = ADAM_B2 * v + (1.0 - ADAM_B2) * (g * g)
    m_hat = nm / (1.0 - ADAM_B1 ** ADAM_STEP)
    v_hat = nv / (1.0 - ADAM_B2 ** ADAM_STEP)
    return -ADAM_LR * (m_hat / (jnp.sqrt(v_hat) + ADAM_EPS) + ADAM_WD * w), nm, nv


ADAM_STEPS = 8


def _adamw_big(gs, ws, ms, vs):
    n = len(gs)

    def body(*refs):
        for t in range(n):
            d, nm, nv = _adamw_math(refs[t][...], refs[n + t][...], refs[2 * n + t][...], refs[3 * n + t][...])
            refs[4 * n + 3 * t][...] = d
            refs[4 * n + 3 * t + 1][...] = nm
            refs[4 * n + 3 * t + 2][...] = nv

    def tile_spec(shape):
        rows, cols = shape
        if rows % (8 * ADAM_STEPS) == 0:
            return pl.BlockSpec((rows // ADAM_STEPS, cols), lambda i: (i, 0))
        return pl.BlockSpec((rows, cols // ADAM_STEPS), lambda i: (0, i))

    specs = [tile_spec(g.shape) for g in gs]
    out_specs, out_shape = [], []
    for t in range(n):
        out_specs += [specs[t]] * 3
        out_shape += [jax.ShapeDtypeStruct(gs[t].shape, F32)] * 3
    outs = pl.pallas_call(
        body, name="adamw_big", grid=(ADAM_STEPS,),
        in_specs=specs * 4, out_specs=tuple(out_specs), out_shape=tuple(out_shape),
        compiler_params=_cparams(1),
    )(*gs, *ws, *ms, *vs)
    return [outs[3 * t: 3 * t + 3] for t in range(n)]


def _adamw_small(pw_sum, vec_sum, ws, ms, vs):
    rows = (None, 0, 1, 2, 3, 8)
    n = len(ws)

    def body(pw_ref, vec_ref, *refs):
        outs = refs[3 * n:]
        for t in range(n):
            w_ref, m_ref, v_ref = refs[t], refs[n + t], refs[2 * n + t]
            if rows[t] is None:
                g = pw_ref[...]
            else:
                g = vec_ref[rows[t]:rows[t] + 1, 0:w_ref.shape[1]]
            d, nm, nv = _adamw_math(g, w_ref[...], m_ref[...], v_ref[...])
            outs[4 * t][...] = g
            outs[4 * t + 1][...] = d
            outs[4 * t + 2][...] = nm
            outs[4 * t + 3][...] = nv
        outs[4 * n][...] = vec_ref[9:10, 0:128]

    vm = pl.BlockSpec(memory_space=pltpu.VMEM)
    out_shape = []
    for w in ws:
        out_shape += [jax.ShapeDtypeStruct(w.shape, F32)] * 4
    out_shape.append(jax.ShapeDtypeStruct((1, 128), F32))
    outs = pl.pallas_call(
        body, name="adamw_small", in_specs=[vm] * (2 + 3 * n), out_specs=(vm,) * (4 * n + 1),
        out_shape=tuple(out_shape),
    )(pw_sum, vec_sum, *ws, *ms, *vs)
    return [outs[4 * t: 4 * t + 4] for t in range(n)], outs[4 * n]


def _fwd_proj(x, w_in_t, w_uq_t, w_ukv, gq, gkv, rope_tab, w_out_slots, tm):
    T = x.shape[0]
    n_steps = T // tm
    fwd_step = n_steps // 2

    def body(x_ref, win_ref, wuq_ref, wukv_ref, gq_ref, gkv_ref, tab_ref, wo_in,
             xq_ref, xkv_ref, ga_ref, u_ref, gb_ref, q_ref, k_ref, v_ref, wo_ref, send_sems, recv_sems):
        i = pl.program_id(0)
        px, py, pc, chips = _place()
        hc = D_MODEL // 2

        def wo_copy(k, chip_idx, half, to):
            blk = wo_ref.at[chip_idx, pl.ds(0, 256), pl.ds(half * hc, hc)]
            return pltpu.make_async_remote_copy(src_ref=blk, dst_ref=blk, send_sem=send_sems.at[k],
                                                recv_sem=recv_sems.at[k], device_id=to, device_id_type=MESH)

        @pl.when(i == 0)
        def _():
            for j, (cx, cy) in enumerate(chips):
                wo_copy(j, 2 * px + py, pc, (cx, cy, pc)).start()

        @pl.when(i == fwd_step)
        def _():
            for j, (cx, cy) in enumerate(chips):
                wo_copy(j, 2 * cx + cy, pc, (px, py, pc)).wait_recv()
                wo_copy(3 + j, 2 * cx + cy, pc, (px, py, 1 - pc)).start()

        @pl.when(i == n_steps - 1)
        def _():
            for j, (cx, cy) in enumerate(chips):
                wo_copy(3 + j, 2 * cx + cy, 1 - pc, (px, py, pc)).wait_recv()
            for j, (cx, cy) in enumerate(chips):
                wo_copy(j, 2 * px + py, pc, (cx, cy, pc)).wait_send()
                wo_copy(3 + j, 2 * cx + cy, pc, (px, py, 1 - pc)).wait_send()

        xb = x_ref[...].astype(BF16)
        h_lat = _dot_nt(xb, win_ref[0:COL_KR, :])
        h_rope = _dot_nt(xb, win_ref[COL_KR:COL_GA, :])
        h_gate = _dot_nt(xb, win_ref[ROPE_END:IN_W, :])
        xq = h_lat[:, 0:COL_KV]
        xkv = h_lat[:, COL_KV:COL_KR]
        xq_ref[...] = xq.astype(BF16)
        xkv_ref[...] = xkv.astype(BF16)
        ga_ref[...] = h_gate[:, 0:MLA_W].astype(BF16)
        u_ref[...] = h_gate[:, MLA_W:MLA_W + POOL_W].astype(BF16)
        gb_ref[...] = h_gate[:, MLA_W + POOL_W:].astype(BF16)
        c, sa, sb = _expand_rope_table(tab_ref[...])
        rq = lax.rsqrt(jnp.mean(xq * xq, axis=-1, keepdims=True) + RMS_EPS)
        q = _dot_nt(((xq * rq) * gq_ref[...]).astype(BF16), wuq_ref[...]) * QSCALE
        rkv = lax.rsqrt(jnp.mean(xkv * xkv, axis=-1, keepdims=True) + RMS_EPS)
        kv = _dot(((xkv * rkv) * gkv_ref[...]).astype(BF16), wukv_ref[...])
        kr = _rope(h_rope, c, sa, sb, 1.0).astype(BF16)
        for hh in range(HEADS):
            b0 = hh * HEAD_PAD
            q_ref[:, b0:b0 + 128] = q[:, b0:b0 + 128].astype(BF16)
            q_ref[:, b0 + 128:b0 + 256] = _rope(q[:, b0 + 128:b0 + 256], c, sa, sb, 1.0).astype(BF16)
            k_ref[:, b0:b0 + 128] = kv[:, b0:b0 + 128].astype(BF16)
            k_ref[:, b0 + 128:b0 + 256] = kr
            v_ref[:, hh * 128:(hh + 1) * 128] = kv[:, b0 + 128:b0 + 256].astype(BF16)

    row = lambda w: pl.BlockSpec((tm, w), lambda i: (i, 0))
    f = lambda w, dt: jax.ShapeDtypeStruct((T, w), dt)
    return pl.pallas_call(
        body, name="fwd_proj", grid=(n_steps,),
        in_specs=[row(D_MODEL), _full(w_in_t.shape), _full(w_uq_t.shape), _full(w_ukv.shape),
                  _full(gq.shape), _full(gkv.shape), row(128), ANY],
        out_specs=(row(512), row(256), row(512), row(512), row(512), row(1024), row(1024), row(512), ANY),
        out_shape=(f(512, BF16), f(256, BF16), f(512, BF16), f(512, BF16), f(512, BF16),
                   f(1024, BF16), f(1024, BF16), f(512, BF16),
                   jax.ShapeDtypeStruct(w_out_slots.shape, BF16)),
        input_output_aliases={7: 8},
        scratch_shapes=[pltpu.SemaphoreType.DMA((6,)), pltpu.SemaphoreType.DMA((6,))],
        compiler_params=_cparams(1),
    )(x, w_in_t, w_uq_t, w_ukv, gq, gkv, rope_tab, w_out_slots)


def _attn_fwd(q, k, v, pos_col, pos_row, bounds, nb, S, tq, tk):
    T = q.shape[0]
    nq, nk = S // tq, S // tk
    reps = tk // 128
    hg = HEAD_GROUP_FWD

    def body(qmin_ref, qmax_ref, kmin_ref, kmax_ref, q_ref, k_ref, v_ref, pc_ref, pr_ref, o_ref, lse_ref,
             m_sc, l_sc, acc_sc):
        b, i = pl.program_id(0), pl.program_id(2)
        m_sc[...] = jnp.full(m_sc.shape, NEG, F32)
        l_sc[...] = jnp.zeros_like(l_sc)
        acc_sc[...] = jnp.zeros_like(acc_sc)
        q_lo = qmin_ref[b * nq + i]
        q_hi = qmax_ref[b * nq + i]

        def tile(j, masked):
            off = pl.multiple_of(j * tk, tk)
            if masked:
                keep = pc_ref[...] >= pr_ref[pl.ds(j, 1), :]
            logits = []
            for g in range(hg):
                qk = slice(g * HEAD_PAD, (g + 1) * HEAD_PAD)
                s = _dot_nt(q_ref[:, qk], k_ref[pl.ds(off, tk), qk])
                if masked:
                    s = jnp.where(keep, s, NEG)
                logits.append(s)
            probs = []
            for g in range(hg):
                hv = slice(g * 128, (g + 1) * 128)
                s = logits[g]
                m_prev = m_sc[:, hv]
                m_new = jnp.maximum(m_prev, jnp.max(s, axis=1, keepdims=True))
                p = jnp.exp2(s - jnp.concatenate([m_new] * reps, axis=1))
                a = jnp.exp2(m_prev - m_new)
                l_sc[:, hv] = a * l_sc[:, hv] + jnp.sum(p, axis=1, keepdims=True)
                m_sc[:, hv] = m_new
                probs.append((p.astype(BF16), a))
            for g in range(hg):
                hv = slice(g * 128, (g + 1) * 128)
                p, a = probs[g]
                acc_sc[:, hv] = a * acc_sc[:, hv] + _dot(p, v_ref[pl.ds(off, tk), hv])

        def step(j, carry):
            visible = kmin_ref[b * nk + j] <= q_hi
            clear = q_lo >= kmax_ref[b * nk + j]

            @pl.when(jnp.logical_and(visible, clear))
            def _():
                tile(j, False)

            @pl.when(jnp.logical_and(visible, jnp.logical_not(clear)))
            def _():
                tile(j, True)
            return carry

        lax.fori_loop(0, nk, step, 0)
        l = l_sc[...]
        o_ref[...] = acc_sc[...] / l
        lse_ref[...] = m_sc[...] + jnp.log2(l)

    ng = HEADS // hg
    stat = pltpu.VMEM((tq, hg * 128), F32)
    return pl.pallas_call(
        body, name="attn_fwd",
        grid_spec=pltpu.PrefetchScalarGridSpec(
            num_scalar_prefetch=4, grid=(nb, ng, nq),
            in_specs=[pl.BlockSpec((tq, hg * HEAD_PAD), lambda b, h, i, *_: (b * nq + i, h)),
                      pl.BlockSpec((S, hg * HEAD_PAD), lambda b, h, i, *_: (b, h)),
                      pl.BlockSpec((S, hg * 128), lambda b, h, i, *_: (b, h)),
                      pl.BlockSpec((tq, 1), lambda b, h, i, *_: (b * nq + i, 0)),
                      pl.BlockSpec((None, nk, tk), lambda b, h, i, *_: (b, 0, 0))],
            out_specs=(pl.BlockSpec((tq, hg * 128), lambda b, h, i, *_: (b * nq + i, h)),
                       pl.BlockSpec((tq, hg * 128), lambda b, h, i, *_: (b * nq + i, h))),
            scratch_shapes=[stat, stat, stat]),
        out_shape=(jax.ShapeDtypeStruct((T, MLA_W), F32), jax.ShapeDtypeStruct((T, MLA_W), F32)),
        compiler_params=_cparams(3),
    )(*bounds, q, k, v, pos_col, pos_row.reshape(nb, nk, tk))


def _mid(x, tgt, o, ga, u, gb, w_out, pool_w, pool_scale, ln_g, ln_b, S, tm):
    T = x.shape[0]
    tps = S // tm
    hb = tm // HALO

    def body(x_ref, tgt_ref, o_ref, ga_ref, u_ref, uh_ref, gb_ref, wout_ref, pw_ref,
             ps_ref, lng_ref, lnb_ref,
             dz_ref, do_ref, delta_ref, dga_ref, dgb_ref, dpc_ref,
             dwout_ref, dpw_ref, vec_ref, dps_ref, dlng_ref, dlnb_ref, loss_ref):
        i = pl.program_id(0)

        @pl.when(i == 0)
        def _():
            dwout_ref[...] = jnp.zeros_like(dwout_ref)
            dpw_ref[...] = jnp.zeros_like(dpw_ref)
            vec_ref[...] = jnp.zeros_like(vec_ref)
            dps_ref[...] = jnp.zeros_like(dps_ref)
            dlng_ref[...] = jnp.zeros_like(dlng_ref)
            dlnb_ref[...] = jnp.zeros_like(dlnb_ref)
            loss_ref[...] = jnp.zeros_like(loss_ref)

        seq_tile = i % tps
        tpos = seq_tile * tm + lax.broadcasted_iota(jnp.int32, (tm, 1), 0)
        ga_v = ga_ref[...].astype(F32)
        sig_a = jax.nn.sigmoid(ga_v)
        silu_a = ga_v * sig_a
        o_v = o_ref[...]
        ya = o_v * silu_a

        u_v = u_ref[...].astype(F32)
        halo = jnp.where(seq_tile == 0, 0.0, uh_ref[...].astype(F32))
        pooled, cnts, mixed = [], [], []
        for g in range(POOL_G):
            lanes = slice(g * POOL_GD, (g + 1) * POOL_GD)
            w = jnp.concatenate([halo[:, lanes], u_v[:, lanes]], axis=0)
            for st in range(g + 1):
                w = w + pltpu.roll(w, 1 << st, 0)
            cnt = jnp.minimum(tpos + 1, 2 << g).astype(F32)
            pg = (w[HALO:, :] / cnt - u_v[:, lanes]).astype(BF16)
            pooled.append(pg)
            cnts.append(cnt)
            mixed.append(_dot(pg, pw_ref[g]))
        mixed = jnp.concatenate(mixed, axis=1)
        ps = ps_ref[...]
        ybp = mixed * ps
        gb_v = gb_ref[...].astype(F32)
        sig_b = jax.nn.sigmoid(gb_v)
        silu_b = gb_v * sig_b
        yb = ybp * silu_b

        cat = jnp.concatenate([ya, yb], axis=1).astype(BF16)
        z = ALPHA * x_ref[...] + _dot(cat, wout_ref[...])
        mu = jnp.mean(z, axis=-1, keepdims=True)
        zc = z - mu
        rstd = lax.rsqrt(jnp.mean(zc * zc, axis=-1, keepdims=True) + LN_EPS)
        zhat = zc * rstd
        lng = lng_ref[...]
        err = (zhat * lng + lnb_ref[...]) - tgt_ref[...]
        row_loss = jnp.sum(err * err, axis=1, keepdims=True)
        loss_ref[...] += jnp.broadcast_to(jnp.sum(row_loss, axis=0, keepdims=True) * (0.5 / D_MODEL), (1, 128))
        dy = err * (1.0 / D_MODEL)
        dlng_ref[...] += jnp.sum(dy * zhat, axis=0, keepdims=True)
        dlnb_ref[...] += jnp.sum(dy, axis=0, keepdims=True)
        dzh = dy * lng
        dz = rstd * (dzh - jnp.mean(dzh, axis=-1, keepdims=True)
                     - zhat * jnp.mean(dzh * zhat, axis=-1, keepdims=True))
        dz_ref[...] = dz
        dzb = dz.astype(BF16)
        dwout_ref[...] += _dot_tn(cat, dzb)
        dcat = _dot_nt(dzb, wout_ref[...])
        dya = dcat[:, :MLA_W]
        dyb = dcat[:, MLA_W:]

        do = dya * silu_a
        do_ref[...] = do.astype(BF16)
        prod = do * o_v
        for hh in range(HEADS):
            lanes = slice(hh * 128, (hh + 1) * 128)
            delta_ref[:, lanes] = jnp.broadcast_to(jnp.sum(prod[:, lanes], axis=1, keepdims=True), (tm, 128))
        dga_ref[...] = (dya * o_v * (sig_a * (1.0 + ga_v * (1.0 - sig_a)))).astype(BF16)
        dgb_ref[...] = (dyb * ybp * (sig_b * (1.0 + gb_v * (1.0 - sig_b)))).astype(BF16)
        dybp = dyb * silu_b
        dps_ref[...] += jnp.sum(dybp * mixed, axis=0, keepdims=True)
        dmixed = (dybp * ps).astype(BF16)
        for g in range(POOL_G):
            lanes = slice(g * POOL_GD, (g + 1) * POOL_GD)
            dpw_ref[g] += _dot_tn(pooled[g], dmixed[:, lanes])
            dpc_ref[:, lanes] = (_dot_nt(dmixed[:, lanes], pw_ref[g]) / cnts[g]).astype(BF16)

        @pl.when(i == pl.num_programs(0) - 1)
        def _():
            vec_ref[0:1, :] = dlng_ref[...]
            vec_ref[1:2, :] = dlnb_ref[...]
            vec_ref[2:3, 0:POOL_W] = dps_ref[...]
            vec_ref[9:10, 0:128] = loss_ref[...]

    row = lambda w: pl.BlockSpec((tm, w), lambda i: (i, 0))
    f = lambda w, dt: jax.ShapeDtypeStruct((T, w), dt)
    halo_spec = pl.BlockSpec((HALO, POOL_W), lambda i: (jnp.maximum(i * hb - 1, 0), 0))
    return pl.pallas_call(
        body, name="mid", grid=(T // tm,),
        in_specs=[row(D_MODEL), row(D_MODEL), row(MLA_W), row(MLA_W), row(POOL_W), halo_spec, row(POOL_W),
                  _full(w_out.shape), _full(pool_w.shape),
                  _full(pool_scale.shape), _full(ln_g.shape), _full(ln_b.shape)],
        out_specs=(row(D_MODEL), row(MLA_W), row(MLA_W), row(MLA_W), row(POOL_W), row(POOL_W),
                   _full((D_MODEL, D_MODEL)), _full(pool_w.shape), _full((VEC_ROWS, D_MODEL))),
        out_shape=(f(D_MODEL, F32), f(MLA_W, BF16), f(MLA_W, F32), f(MLA_W, BF16), f(POOL_W, BF16), f(POOL_W, BF16),
                   jax.ShapeDtypeStruct((D_MODEL, D_MODEL), F32), jax.ShapeDtypeStruct(pool_w.shape, F32),
                   jax.ShapeDtypeStruct((VEC_ROWS, D_MODEL), F32)),
        scratch_shapes=[pltpu.VMEM((1, POOL_W), F32), pltpu.VMEM((1, D_MODEL), F32), pltpu.VMEM((1, D_MODEL), F32),
                        pltpu.VMEM((1, 128), F32)],
        compiler_params=_cparams(1),
    )(x, tgt, o, ga, u, u, gb, w_out, pool_w, pool_scale, ln_g, ln_b)


def _attn_bwd(q, k, v, do, lse, delta, pos_col, pos_row, bounds, early, early_wire, nb, S, tq, tk):
    T = q.shape[0]
    ne = len(early)
    nq, nk = S // tq, S // tk
    reps = tk // 128
    hg = HEAD_GROUP
    ng = HEADS // hg

    def body(qmin_ref, qmax_ref, kmin_ref, kmax_ref, q_ref, k_ref, v_ref, do_ref, lse_ref, dl_ref, pc_ref, pr_ref,
             *rest):
        early_in, rest = rest[:ne], rest[ne:]
        dq_out, dk_out, dv_out = rest[:3]
        early_out, rest = rest[3:3 + ne], rest[3 + ne:]
        dq_ref, dk_ref, dv_ref = rest[:3]
        b, j = pl.program_id(0), pl.program_id(2)
        flat = (b * ng + pl.program_id(1)) * nk + j
        last = nb * ng * nk - 1
        when = [0, min(3, last), min(max(5 * (last + 1) // 8, 3), last), last]
        for at, phase in zip(when, _reduce_phases(early, early_wire, early_in, early_out, rest[3:])):
            pl.when(flat == at)(phase)

        @pl.when(j == 0)
        def _():
            dq_ref[...] = jnp.zeros_like(dq_ref)

        dk_ref[...] = jnp.zeros_like(dk_ref)
        dv_ref[...] = jnp.zeros_like(dv_ref)
        k_lo = kmin_ref[b * nk + j]
        k_hi = kmax_ref[b * nk + j]

        def tile(i, masked):
            rows = pl.ds(pl.multiple_of(i * tq, tq), tq)
            if masked:
                keep = pc_ref[rows, :] >= pr_ref[...]
            stage = []
            for g in range(hg):
                qk = slice(g * HEAD_PAD, (g + 1) * HEAD_PAD)
                hv = slice(g * 128, (g + 1) * 128)
                s = _dot_nt(q_ref[rows, qk], k_ref[:, qk])
                if masked:
                    s = jnp.where(keep, s, NEG)
                stage.append((s, _dot_nt(do_ref[rows, hv], v_ref[:, hv])))
            grads = []
            for g in range(hg):
                hv = slice(g * 128, (g + 1) * 128)
                s, dp = stage[g]
                p = jnp.exp2(s - jnp.concatenate([lse_ref[rows, hv]] * reps, axis=1))
                ds = (p * (dp - jnp.concatenate([dl_ref[rows, hv]] * reps, axis=1))).astype(BF16)
                grads.append((p.astype(BF16), ds))
            for g in range(hg):
                qk = slice(g * HEAD_PAD, (g + 1) * HEAD_PAD)
                hv = slice(g * 128, (g + 1) * 128)
                p, ds = grads[g]
                dv_ref[:, hv] += _dot_tn(p, do_ref[rows, hv])
                dq_ref[rows, qk] += _dot(ds, k_ref[:, qk])
                dk_ref[:, qk] += _dot_tn(ds, q_ref[rows, qk])

        def step(i, carry):
            visible = k_lo <= qmax_ref[b * nq + i]
            clear = qmin_ref[b * nq + i] >= k_hi

            @pl.when(jnp.logical_and(visible, clear))
            def _():
                tile(i, False)

            @pl.when(jnp.logical_and(visible, jnp.logical_not(clear)))
            def _():
                tile(i, True)
            return carry

        lax.fori_loop(0, nq, step, 0)
        dk_out[...] = dk_ref[...].astype(BF16)
        dv_out[...] = dv_ref[...].astype(BF16)

        @pl.when(j == nk - 1)
        def _():
            dq_out[...] = dq_ref[...].astype(BF16)

    seq = lambda w: pl.BlockSpec((S, w), lambda b, h, j, *_: (b, h))
    blk = lambda w: pl.BlockSpec((tk, w), lambda b, h, j, *_: (b * nk + j, h))
    outs = pl.pallas_call(
        body, name="attn_bwd",
        grid_spec=pltpu.PrefetchScalarGridSpec(
            num_scalar_prefetch=4, grid=(nb, ng, nk),
            in_specs=[seq(hg * HEAD_PAD), blk(hg * HEAD_PAD), blk(hg * 128),
                      seq(hg * 128), seq(hg * 128), seq(hg * 128),
                      pl.BlockSpec((S, 1), lambda b, h, j, *_: (b, 0)),
                      pl.BlockSpec((None, 1, tk), lambda b, h, j, *_: (b, 0, j))] + [ANY] * ne,
            out_specs=(seq(hg * HEAD_PAD), blk(hg * HEAD_PAD), blk(hg * 128)) + (ANY,) * ne,
            scratch_shapes=[pltpu.VMEM((S, hg * HEAD_PAD), F32), pltpu.VMEM((tk, hg * HEAD_PAD), F32),
                            pltpu.VMEM((tk, hg * 128), F32)] + _reduce_scratch(early, early_wire)),
        out_shape=(jax.ShapeDtypeStruct((T, HEADS * HEAD_PAD), BF16),
                   jax.ShapeDtypeStruct((T, HEADS * HEAD_PAD), BF16),
                   jax.ShapeDtypeStruct((T, MLA_W), BF16)) + tuple(jax.ShapeDtypeStruct(g.shape[-2:], F32) for g in early),
        compiler_params=_cparams(3),
    )(*bounds, q, k, v, do, lse, delta, pos_col, pos_row, *early)
    return outs[:3], outs[3:]


def _bwd_proj(dq, dk, dv, xq, xkv, x, dz, dga, dgb, dpc, rope_tab, w_uq_t, w_ukv, w_in_t, gq, gkv, S, tm):
    T = x.shape[0]
    tps = S // tm
    hb = tm // HALO
    n_tiles = T // tm

    def body(dq_ref, dk_ref, dv_ref, xq_ref, xkv_ref, x_ref, dz_ref, dga_ref, dgb_ref, dpc_ref, dph_ref,
             tab_ref, wuq_ref, wukv_ref, win_ref, gq_ref, gkv_ref,
             dx_ref, dwin_hbm, dwuq_hbm, dwukv_hbm, vec_ref,
             acc_win, acc_wuq, acc_wukv, dh_sc):
        i = pl.program_id(0)

        @pl.when(i == 0)
        def _():
            acc_win[...] = jnp.zeros_like(acc_win)
            acc_wuq[...] = jnp.zeros_like(acc_wuq)
            acc_wukv[...] = jnp.zeros_like(acc_wukv)
            vec_ref[...] = jnp.zeros_like(vec_ref)
            dh_sc[...] = jnp.zeros_like(dh_sc)

        dh_prev = dh_sc[...]
        dx_ref[...] = (ALPHA * dz_ref[...] + _dot(dh_prev[:, 0:COL_GA], win_ref[0:COL_GA, :])
                       + _dot(dh_prev[:, COL_GA:], win_ref[ROPE_END:IN_W, :]))
        acc_win[...] += _dot_tn(dh_prev, x_ref[...].astype(BF16))

        live = jnp.where(i < n_tiles, 1.0, 0.0)
        c, sa, sb = _expand_rope_table(tab_ref[...])
        dq_v = dq_ref[...].astype(F32) * (SCALE * live)
        dk_v = dk_ref[...].astype(F32) * (LN2 * live)
        dv_v = dv_ref[...].astype(F32) * live
        dq_parts, dkv_parts = [], []
        dkr = jnp.zeros((tm, 128), F32)
        for hh in range(HEADS):
            b0 = hh * HEAD_PAD
            dq_parts.append(dq_v[:, b0:b0 + 128].astype(BF16))
            dq_parts.append(_rope(dq_v[:, b0 + 128:b0 + 256], c, sa, sb, -1.0).astype(BF16))
            dkv_parts.append(dk_v[:, b0:b0 + 128].astype(BF16))
            dkv_parts.append(dv_v[:, hh * 128:(hh + 1) * 128].astype(BF16))
            dkr = dkr + dk_v[:, b0 + 128:b0 + 256]
        dqp = jnp.concatenate(dq_parts, axis=1)
        dkvp = jnp.concatenate(dkv_parts, axis=1)
        dkrr = _rope(dkr, c, sa, sb, -1.0)

        def rms_bwd(xv, g, dyn, dg_ref):
            r = lax.rsqrt(jnp.mean(xv * xv, axis=-1, keepdims=True) + RMS_EPS)
            xhat = xv * r
            dg_ref[...] += jnp.sum(dyn * xhat, axis=0, keepdims=True)
            dxh = dyn * g
            return r * (dxh - xhat * jnp.mean(dxh * xhat, axis=-1, keepdims=True))

        xq_v = xq_ref[...].astype(F32)
        gq_v = gq_ref[...]
        rq = lax.rsqrt(jnp.mean(xq_v * xq_v, axis=-1, keepdims=True) + RMS_EPS)
        acc_wuq[...] += _dot_tn(dqp, ((xq_v * rq) * gq_v).astype(BF16))
        dxq = rms_bwd(xq_v, gq_v, _dot(dqp, wuq_ref[...]), vec_ref.at[3:4, 0:Q_LORA])

        xkv_v = xkv_ref[...].astype(F32)
        gkv_v = gkv_ref[...]
        rkv = lax.rsqrt(jnp.mean(xkv_v * xkv_v, axis=-1, keepdims=True) + RMS_EPS)
        acc_wukv[...] += _dot_tn(((xkv_v * rkv) * gkv_v).astype(BF16), dkvp)
        dxkv = rms_bwd(xkv_v, gkv_v, _dot_nt(dkvp, wukv_ref[...]), vec_ref.at[8:9, 0:KV_LORA])

        seq_tile = i % tps
        tpos = seq_tile * tm + lax.broadcasted_iota(jnp.int32, (tm, 1), 0)
        dpc_v = dpc_ref[...].astype(F32)
        halo = jnp.where(seq_tile == tps - 1, 0.0, dph_ref[...].astype(F32))
        n = tm + HALO
        du = []
        for g in range(POOL_G):
            lanes = slice(g * POOL_GD, (g + 1) * POOL_GD)
            f = jnp.concatenate([dpc_v[:, lanes], halo[:, lanes]], axis=0)
            for st in range(g + 1):
                f = f + pltpu.roll(f, n - (1 << st), 0)
            cnt = jnp.minimum(tpos + 1, 2 << g).astype(F32)
            du.append((f[:tm, :] - dpc_v[:, lanes] * cnt).astype(BF16))

        dh_sc[...] = jnp.concatenate([dxq.astype(BF16), dxkv.astype(BF16), dkrr.astype(BF16), dga_ref[...]]
                                     + du + [dgb_ref[...]], axis=1)

        @pl.when(i == n_tiles)
        def _():
            pltpu.sync_copy(acc_win.at[pl.ds(0, ROPE_END)], dwin_hbm.at[pl.ds(0, ROPE_END)])
            pltpu.sync_copy(acc_win.at[pl.ds(COL_GA, IN_EXT - COL_GA)], dwin_hbm.at[pl.ds(ROPE_END, IN_W - ROPE_END)])
            for hh in range(HEADS):
                pltpu.sync_copy(acc_wuq.at[pl.ds(hh * HEAD_PAD, NOPE + ROPE)], dwuq_hbm.at[hh])
            for ch in range(N_CHIPS):
                pltpu.sync_copy(acc_wukv.at[:, pl.ds(ch * 256, 256)], dwukv_hbm.at[ch])

    cur = lambda w: pl.BlockSpec((tm, w), lambda i: (jnp.minimum(i, n_tiles - 1), 0))
    prev = lambda w: pl.BlockSpec((tm, w), lambda i: (jnp.maximum(i - 1, 0), 0))
    halo_spec = pl.BlockSpec((HALO, POOL_W), lambda i: (jnp.minimum((i + 1) * hb, T // HALO - 1), 0))
    return pl.pallas_call(
        body, name="bwd_proj", grid=(n_tiles + 1,),
        in_specs=[cur(1024), cur(1024), cur(512), cur(512), cur(256), prev(D_MODEL), prev(D_MODEL),
                  cur(512), cur(512), cur(512), halo_spec, cur(128),
                  _full(w_uq_t.shape), _full(w_ukv.shape), _full(w_in_t.shape), _full(gq.shape), _full(gkv.shape)],
        out_specs=(prev(D_MODEL), ANY, ANY, ANY, _full((VEC_ROWS, D_MODEL))),
        out_shape=(jax.ShapeDtypeStruct((T, D_MODEL), F32),
                   jax.ShapeDtypeStruct((IN_W, D_MODEL), F32),
                   jax.ShapeDtypeStruct((HEADS, NOPE + ROPE, Q_LORA), F32),
                   jax.ShapeDtypeStruct((N_CHIPS, KV_LORA, 256), F32),
                   jax.ShapeDtypeStruct((VEC_ROWS, D_MODEL), F32)),
        scratch_shapes=[pltpu.VMEM((IN_EXT, D_MODEL), F32), pltpu.VMEM((HEADS * HEAD_PAD, Q_LORA), F32),
                        pltpu.VMEM((KV_LORA, 1024), F32), pltpu.VMEM((tm, IN_EXT), BF16)],
        compiler_params=_cparams(1),
    )(dq, dk, dv, xq, xkv, x, dz, dga, dgb, dpc, dpc, rope_tab, w_uq_t, w_ukv, w_in_t, gq, gkv)


def kernel(x, positions, w_in, q_norm_g, w_uq, kv_norm_g, w_ukv, pool_w, pool_scale, w_out, ln_g, ln_b, loss_target, m_w_in, m_q_norm_g, m_w_uq, m_kv_norm_g, m_w_ukv, m_pool_w, m_pool_scale, m_w_out, m_ln_g, m_ln_b, v_w_in, v_q_norm_g, v_w_uq, v_kv_norm_g, v_w_ukv, v_pool_w, v_pool_scale, v_w_out, v_ln_g, v_ln_b):
    nb, S, _ = x.shape
    T = nb * S
    tm = min(256, S)
    tq = min(512, S)
    tk = min(512, S)
    assert S % tm == 0 and tm % HALO == 0 and S % tq == 0 and S % tk == 0

    cx, cy, cc = lax.axis_index("x"), lax.axis_index("y"), lax.axis_index("c")
    me = 2 * cx + cy

    half = ROPE // 2
    inv_freq = ROPE_THETA ** (-jnp.arange(half, dtype=F32) / half)
    freq_row = jnp.concatenate([inv_freq, inv_freq, jnp.zeros((2 * half,), F32)]).reshape(1, 128)
    pos_col = positions.reshape(T, 1)
    pos_row = positions.reshape(nb, 1, S)
    pos_q = positions.reshape(nb, S // tq, tq)
    pos_k = positions.reshape(nb, S // tk, tk)
    bounds = (jnp.min(pos_q, axis=2).reshape(-1), jnp.max(pos_q, axis=2).reshape(-1),
              jnp.min(pos_k, axis=2).reshape(-1), jnp.max(pos_k, axis=2).reshape(-1))

    def own_slot(w, slot_rows):
        blk = jnp.pad(w.astype(BF16), ((0, slot_rows - w.shape[0]), (0, 0)))
        return lax.dynamic_update_slice(jnp.zeros((N_CHIPS,) + blk.shape, BF16), blk[None], (me, 0, 0))

    (w_in_g, w_uq_g, w_ukv_g), rope_tab = _weight_gather(
        [own_slot(w_in.T, IN_SHARD), own_slot(w_uq.T, HEAD_PAD), own_slot(w_ukv, KV_LORA)], (IN_SHARD, NOPE + ROPE, KV_LORA),
        pos_col, freq_row)
    w_in_t = w_in_g.reshape(IN_W, D_MODEL)
    w_uq_t = w_uq_g.reshape(HEADS * HEAD_PAD, Q_LORA)
    w_ukv_f = w_ukv_g.transpose(1, 0, 2).reshape(KV_LORA, 1024)
    pool_w_b = pool_w.astype(BF16)
    gq2 = q_norm_g.reshape(1, Q_LORA)
    gkv2 = kv_norm_g.reshape(1, KV_LORA)
    ps2 = pool_scale.reshape(1, POOL_W)

    xf = x.reshape(T, D_MODEL)
    tgt = loss_target.reshape(T, D_MODEL)

    xq, xkv, ga, u, gb, q, k, v, w_out_g = _fwd_proj(
        xf, w_in_t, w_uq_t, w_ukv_f, gq2, gkv2, rope_tab, own_slot(w_out, 256), tm)
    w_out_f = w_out_g.reshape(D_MODEL, D_MODEL)
    o, lse = _attn_fwd(q, k, v, pos_col, pos_row, bounds, nb, S, tq, tk)

    dz, do, delta, dga, dgb, dpc, d_w_out, d_pool_w, vec_early = _mid(
        xf, tgt, o, ga, u, gb, w_out_f, pool_w_b, ps2, ln_g, ln_b, S, tm)

    early = [d_w_out.reshape(N_CHIPS, 256, D_MODEL), d_pool_w.reshape(-1, D_MODEL), vec_early]
    (dq, dk, dv), (g_out, pw_sum, vec_early_sum) = _attn_bwd(
        q, k, v, do, lse, delta, pos_col, pos_row, bounds, early, (BF16, F32, F32), nb, S, tq, tk)
    dx, d_w_in_t, d_w_uq_t, g_ukv, vec_late = _bwd_proj(
        dq, dk, dv, xq, xkv, xf, dz, dga, dgb, dpc, rope_tab, w_uq_t, w_ukv_f, w_in_t, gq2, gkv2, S, tm)
    grad_x = dx.reshape(nb, S, D_MODEL)

    g_in = d_w_in_t.reshape(N_CHIPS, IN_SHARD, D_MODEL)
    g_uq = d_w_uq_t
    g_in, g_uq, g_ukv, vec_late_sum = _grad_reduce([g_in, g_uq, g_ukv, vec_late], (BF16, BF16, BF16, F32))
    g_big = [g_in, g_uq, g_ukv, g_out]
    pw_sum = pw_sum.reshape(POOL_G * POOL_GD, POOL_GD)
    vec_sum = vec_early_sum + vec_late_sum

    big = _adamw_big(g_big, [w_in.T, w_uq.T, w_ukv, w_out], [m_w_in.T, m_w_uq.T, m_w_ukv, m_w_out],
                     [v_w_in.T, v_w_uq.T, v_w_ukv, v_w_out])
    two_d = lambda a: a.reshape(-1, a.shape[-1])
    small_names = lambda pw, lg, lb, ps, gq, gkv: [two_d(pw), lg, lb, ps.reshape(1, -1), gq.reshape(1, -1), gkv.reshape(1, -1)]
    small, loss_row = _adamw_small(
        pw_sum, vec_sum,
        small_names(pool_w, ln_g, ln_b, pool_scale, q_norm_g, kv_norm_g),
        small_names(m_pool_w, m_ln_g, m_ln_b, m_pool_scale, m_q_norm_g, m_kv_norm_g),
        small_names(v_pool_w, v_ln_g, v_ln_b, v_pool_scale, v_q_norm_g, v_kv_norm_g))
    loss = loss_row[0, 0]

    def leaves(kind):
        b = [g_big[t] if kind == 0 else big[t][kind - 1] for t in range(N_BIG)]
        b = [b[0].T, b[1].T, b[2], b[3]]
        s = [small[t][kind] for t in range(6)]
        return (b[0], s[4].reshape(Q_LORA), b[1], s[5].reshape(KV_LORA), b[2],
                s[0].reshape(POOL_G, POOL_GD, POOL_GD), s[3].reshape(POOL_W), b[3], s[1], s[2])

    return (loss, grad_x) + leaves(0) + leaves(1) + leaves(2) + leaves(3)
```
